```python
import jax, jax.numpy as jnp
from jax import lax
import numpy as np

D_MODEL = 2048
BATCH = 16
SEQ = 2048
DEPTH = 1

HEAD_DIM = 64
ATTN_WIDTH = D_MODEL // 2
N_Q_HEADS = ATTN_WIDTH // HEAD_DIM
N_KV_HEADS = 2
GQA_GROUP = N_Q_HEADS // N_KV_HEADS
KV_WIDTH = N_KV_HEADS * HEAD_DIM
WINDOW = 128
BLOCK = 128
GATE_WIDTH = D_MODEL - ATTN_WIDTH
GATE_GROUP_DIM = 128
N_GATE_GROUPS = GATE_WIDTH // GATE_GROUP_DIM
CHUNK = 128
MIX_WIDTH = ATTN_WIDTH + GATE_WIDTH
IN_WIDTH = ATTN_WIDTH + 2 * KV_WIDTH + 2 * GATE_WIDTH
D_FF = -(-8 * D_MODEL // (3 * 256)) * 256
EPS = 1e-6

kernel_name = 'hymba_swa_sink_gmlp_swiglu'


def _rms_norm(x, gain):
    xf = x.astype(jnp.float32)
    y = xf * lax.rsqrt(jnp.mean(xf * xf, axis=-1, keepdims=True) + EPS)
    return (y * gain.astype(jnp.float32)).astype(x.dtype)


def _alibi_slopes(n_heads):
    return jnp.exp2(-8.0 * jnp.arange(1, n_heads + 1, dtype=jnp.float32) / n_heads)


def _sliding_window_attention(q, k, v, sinks):
    bsz, seq = q.shape[0], q.shape[1]
    nb = seq // BLOCK
    qb = q.reshape(bsz, nb, BLOCK, N_KV_HEADS, GQA_GROUP, HEAD_DIM)

    def band(t):
        tb = t.reshape(bsz, nb, BLOCK, N_KV_HEADS, HEAD_DIM)
        prev = jnp.pad(tb, ((0, 0), (1, 0), (0, 0), (0, 0), (0, 0)))[:, :nb]
        return jnp.concatenate([prev, tb], axis=2)

    kb, vb = band(k), band(v)
    scores = jnp.einsum('bnqkgd,bnskd->bkgnqs', qb, kb).astype(jnp.float32) * (HEAD_DIM ** -0.5)
    qi = jnp.arange(BLOCK)[:, None]
    kj = jnp.arange(2 * BLOCK)[None, :]
    dist = qi + BLOCK - kj
    key_pos = jnp.arange(nb)[:, None, None] * BLOCK - BLOCK + kj[None]
    valid = (dist >= 0) & (dist < WINDOW) & (key_pos >= 0)
    slopes = _alibi_slopes(N_Q_HEADS).reshape(N_KV_HEADS, GQA_GROUP, 1, 1, 1)
    logits = jnp.where(valid, scores - slopes * dist.astype(jnp.float32), -jnp.inf)
    sink = jnp.broadcast_to(
        sinks.astype(jnp.float32).reshape(1, N_KV_HEADS, GQA_GROUP, 1, 1, 1),
        logits.shape[:-1] + (1,))
    probs = jax.nn.softmax(jnp.concatenate([logits, sink], axis=-1), axis=-1)[..., :-1]
    out = jnp.einsum('bkgnqs,bnskd->bnqkgd', probs.astype(v.dtype), vb)
    return out.reshape(bsz, seq, ATTN_WIDTH)


def _spatial_gating(z, ln_g, ln_b, w_s, b_s):
    bsz, seq = z.shape[0], z.shape[1]
    nc = seq // CHUNK
    u, v = jnp.split(z, 2, axis=-1)
    vc = v.reshape(bsz, nc, CHUNK, N_GATE_GROUPS, GATE_GROUP_DIM).astype(jnp.float32)
    mu = jnp.mean(vc, axis=-1, keepdims=True)
    var = jnp.mean(jnp.square(vc - mu), axis=-1, keepdims=True)
    vn = ((vc - mu) * lax.rsqrt(var + EPS) * ln_g.astype(jnp.float32)
          + ln_b.astype(jnp.float32)).astype(z.dtype)
    causal = jnp.tril(jnp.ones((CHUNK, CHUNK), dtype=bool))[None]
    w = jnp.where(causal, w_s, jnp.zeros_like(w_s))
    mixed = jnp.einsum('gts,bcsge->bctge', w, vn) + b_s.T[:, :, None]
    return u * mixed.reshape(bsz, seq, GATE_WIDTH)


def _normal(key, shape, scale):
    return jax.random.normal(key, shape, jnp.float32) * scale


def _fwd_setup_inputs(seed: int = 0) -> dict:
    key = jax.random.key(seed)
    ks = jax.random.split(key, 17)
    L = DEPTH
    return {
        'x': _normal(ks[0], (BATCH, SEQ, D_MODEL), 1.0),
        'norm1_g': 1.0 + _normal(ks[1], (L, D_MODEL), 0.1),
        'w_in': _normal(ks[2], (L, D_MODEL, IN_WIDTH), D_MODEL ** -0.5),
        'q_norm_g': 1.0 + _normal(ks[3], (L, HEAD_DIM), 0.1),
        'k_norm_g': 1.0 + _normal(ks[4], (L, HEAD_DIM), 0.1),
        'attn_sinks': _normal(ks[5], (L, N_Q_HEADS), 0.5),
        'gate_ln_g': 1.0 + _normal(ks[6], (L, N_GATE_GROUPS, GATE_GROUP_DIM), 0.1),
        'gate_ln_b': _normal(ks[7], (L, N_GATE_GROUPS, GATE_GROUP_DIM), 0.02),
        'w_spatial': _normal(ks[8], (L, N_GATE_GROUPS, CHUNK, CHUNK), CHUNK ** -0.5),
        'b_spatial': 1.0 + _normal(ks[9], (L, N_GATE_GROUPS, CHUNK), 0.1),
        'out_norm_attn_g': 1.0 + _normal(ks[10], (L, ATTN_WIDTH), 0.1),
        'out_norm_gate_g': 1.0 + _normal(ks[11], (L, GATE_WIDTH), 0.1),
        'w_out': _normal(ks[12], (L, MIX_WIDTH, D_MODEL), MIX_WIDTH ** -0.5),
        'norm2_g': 1.0 + _normal(ks[13], (L, D_MODEL), 0.1),
        'w_ffn_gate': _normal(ks[14], (L, D_MODEL, D_FF), D_MODEL ** -0.5),
        'w_ffn_up': _normal(ks[15], (L, D_MODEL, D_FF), D_MODEL ** -0.5),
        'w_ffn_down': _normal(ks[16], (L, D_FF, D_MODEL), D_FF ** -0.5),
    }


def _fwd_reference(x, norm1_g, w_in, q_norm_g, k_norm_g, attn_sinks, gate_ln_g, gate_ln_b,
              w_spatial, b_spatial, out_norm_attn_g, out_norm_gate_g, w_out, norm2_g,
              w_ffn_gate, w_ffn_up, w_ffn_down):
    bsz, seq = x.shape[0], x.shape[1]
    for l in range(DEPTH):
        h = _rms_norm(x, norm1_g[l])
        proj = h @ w_in[l]
        q, k, v, z = jnp.split(
            proj, [ATTN_WIDTH, ATTN_WIDTH + KV_WIDTH, ATTN_WIDTH + 2 * KV_WIDTH], axis=-1)
        q = _rms_norm(q.reshape(bsz, seq, N_Q_HEADS, HEAD_DIM), q_norm_g[l])
        k = _rms_norm(k.reshape(bsz, seq, N_KV_HEADS, HEAD_DIM), k_norm_g[l])
        v = v.reshape(bsz, seq, N_KV_HEADS, HEAD_DIM)
        y_attn = _sliding_window_attention(q, k, v, attn_sinks[l])
        y_gate = _spatial_gating(jax.nn.gelu(z, approximate=False), gate_ln_g[l], gate_ln_b[l],
                                 w_spatial[l], b_spatial[l])
        y = jnp.concatenate([_rms_norm(y_attn, out_norm_attn_g[l]),
                             _rms_norm(y_gate, out_norm_gate_g[l])], axis=-1)
        x = x + y @ w_out[l]
        h = _rms_norm(x, norm2_g[l])
        x = x + (jax.nn.silu(h @ w_ffn_gate[l]) * (h @ w_ffn_up[l])) @ w_ffn_down[l]
    return x


import jax as _jax
import jax.numpy as _jnp

TWIN_FORMAT = 'train_step'
FWD_PARAMS = ['x', 'norm1_g', 'w_in', 'q_norm_g', 'k_norm_g', 'attn_sinks', 'gate_ln_g', 'gate_ln_b', 'w_spatial', 'b_spatial', 'out_norm_attn_g', 'out_norm_gate_g', 'w_out', 'norm2_g', 'w_ffn_gate', 'w_ffn_up', 'w_ffn_down']
TWIN_WEIGHTS = ['norm1_g', 'w_in', 'q_norm_g', 'k_norm_g', 'attn_sinks', 'gate_ln_g', 'gate_ln_b', 'w_spatial', 'b_spatial', 'out_norm_attn_g', 'out_norm_gate_g', 'w_out', 'norm2_g', 'w_ffn_gate', 'w_ffn_up', 'w_ffn_down']
TWIN_DIFF_INPUT = 'x'
TWIN_INPUTS = ['x', 'norm1_g', 'w_in', 'q_norm_g', 'k_norm_g', 'attn_sinks', 'gate_ln_g', 'gate_ln_b', 'w_spatial', 'b_spatial', 'out_norm_attn_g', 'out_norm_gate_g', 'w_out', 'norm2_g', 'w_ffn_gate', 'w_ffn_up', 'w_ffn_down', 'loss_target', 'm_norm1_g', 'm_w_in', 'm_q_norm_g', 'm_k_norm_g', 'm_attn_sinks', 'm_gate_ln_g', 'm_gate_ln_b', 'm_w_spatial', 'm_b_spatial', 'm_out_norm_attn_g', 'm_out_norm_gate_g', 'm_w_out', 'm_norm2_g', 'm_w_ffn_gate', 'm_w_ffn_up', 'm_w_ffn_down', 'v_norm1_g', 'v_w_in', 'v_q_norm_g', 'v_k_norm_g', 'v_attn_sinks', 'v_gate_ln_g', 'v_gate_ln_b', 'v_w_spatial', 'v_b_spatial', 'v_out_norm_attn_g', 'v_out_norm_gate_g', 'v_w_out', 'v_norm2_g', 'v_w_ffn_gate', 'v_w_ffn_up', 'v_w_ffn_down']
TWIN_OUTPUTS = ['loss', 'grad_x', 'grad_norm1_g', 'grad_w_in', 'grad_q_norm_g', 'grad_k_norm_g', 'grad_attn_sinks', 'grad_gate_ln_g', 'grad_gate_ln_b', 'grad_w_spatial', 'grad_b_spatial', 'grad_out_norm_attn_g', 'grad_out_norm_gate_g', 'grad_w_out', 'grad_norm2_g', 'grad_w_ffn_gate', 'grad_w_ffn_up', 'grad_w_ffn_down', 'delta_norm1_g', 'delta_w_in', 'delta_q_norm_g', 'delta_k_norm_g', 'delta_attn_sinks', 'delta_gate_ln_g', 'delta_gate_ln_b', 'delta_w_spatial', 'delta_b_spatial', 'delta_out_norm_attn_g', 'delta_out_norm_gate_g', 'delta_w_out', 'delta_norm2_g', 'delta_w_ffn_gate', 'delta_w_ffn_up', 'delta_w_ffn_down', 'new_m_norm1_g', 'new_m_w_in', 'new_m_q_norm_g', 'new_m_k_norm_g', 'new_m_attn_sinks', 'new_m_gate_ln_g', 'new_m_gate_ln_b', 'new_m_w_spatial', 'new_m_b_spatial', 'new_m_out_norm_attn_g', 'new_m_out_norm_gate_g', 'new_m_w_out', 'new_m_norm2_g', 'new_m_w_ffn_gate', 'new_m_w_ffn_up', 'new_m_w_ffn_down', 'new_v_norm1_g', 'new_v_w_in', 'new_v_q_norm_g', 'new_v_k_norm_g', 'new_v_attn_sinks', 'new_v_gate_ln_g', 'new_v_gate_ln_b', 'new_v_w_spatial', 'new_v_b_spatial', 'new_v_out_norm_attn_g', 'new_v_out_norm_gate_g', 'new_v_w_out', 'new_v_norm2_g', 'new_v_w_ffn_gate', 'new_v_w_ffn_up', 'new_v_w_ffn_down']
TWIN_LEAF_KINDS = {'loss': 'loss', 'grad_x': 'grad_x', 'grad_norm1_g': 'grad_w', 'grad_w_in': 'grad_w', 'grad_q_norm_g': 'grad_w', 'grad_k_norm_g': 'grad_w', 'grad_attn_sinks': 'grad_w', 'grad_gate_ln_g': 'grad_w', 'grad_gate_ln_b': 'grad_w', 'grad_w_spatial': 'grad_w', 'grad_b_spatial': 'grad_w', 'grad_out_norm_attn_g': 'grad_w', 'grad_out_norm_gate_g': 'grad_w', 'grad_w_out': 'grad_w', 'grad_norm2_g': 'grad_w', 'grad_w_ffn_gate': 'grad_w', 'grad_w_ffn_up': 'grad_w', 'grad_w_ffn_down': 'grad_w', 'delta_norm1_g': 'delta_w', 'delta_w_in': 'delta_w', 'delta_q_norm_g': 'delta_w', 'delta_k_norm_g': 'delta_w', 'delta_attn_sinks': 'delta_w', 'delta_gate_ln_g': 'delta_w', 'delta_gate_ln_b': 'delta_w', 'delta_w_spatial': 'delta_w', 'delta_b_spatial': 'delta_w', 'delta_out_norm_attn_g': 'delta_w', 'delta_out_norm_gate_g': 'delta_w', 'delta_w_out': 'delta_w', 'delta_norm2_g': 'delta_w', 'delta_w_ffn_gate': 'delta_w', 'delta_w_ffn_up': 'delta_w', 'delta_w_ffn_down': 'delta_w', 'new_m_norm1_g': 'new_m', 'new_m_w_in': 'new_m', 'new_m_q_norm_g': 'new_m', 'new_m_k_norm_g': 'new_m', 'new_m_attn_sinks': 'new_m', 'new_m_gate_ln_g': 'new_m', 'new_m_gate_ln_b': 'new_m', 'new_m_w_spatial': 'new_m', 'new_m_b_spatial': 'new_m', 'new_m_out_norm_attn_g': 'new_m', 'new_m_out_norm_gate_g': 'new_m', 'new_m_w_out': 'new_m', 'new_m_norm2_g': 'new_m', 'new_m_w_ffn_gate': 'new_m', 'new_m_w_ffn_up': 'new_m', 'new_m_w_ffn_down': 'new_m', 'new_v_norm1_g': 'new_v', 'new_v_w_in': 'new_v', 'new_v_q_norm_g': 'new_v', 'new_v_k_norm_g': 'new_v', 'new_v_attn_sinks': 'new_v', 'new_v_gate_ln_g': 'new_v', 'new_v_gate_ln_b': 'new_v', 'new_v_w_spatial': 'new_v', 'new_v_b_spatial': 'new_v', 'new_v_out_norm_attn_g': 'new_v', 'new_v_out_norm_gate_g': 'new_v', 'new_v_w_out': 'new_v', 'new_v_norm2_g': 'new_v', 'new_v_w_ffn_gate': 'new_v', 'new_v_w_ffn_up': 'new_v', 'new_v_w_ffn_down': 'new_v'}


def _forward(args):
    return _fwd_reference(*[args[k] for k in FWD_PARAMS])


def _output_shape():
    out = _jax.eval_shape(lambda: _forward(_fwd_setup_inputs(0)))
    return out.shape, out.dtype

N_MICROBATCH = 1
ADAM_LR = 0.001
ADAM_B1 = 0.9
ADAM_B2 = 0.999
ADAM_EPS = 1e-08
ADAM_WD = 0.01
ADAM_STEP = 10
PER_EXAMPLE_BATCH_AXIS = {'x': 0, 'loss_target': 0}
SHARED_INPUTS = []
_WEIGHT_DTYPES = {'norm1_g': _jnp.float32, 'w_in': _jnp.float32, 'q_norm_g': _jnp.float32, 'k_norm_g': _jnp.float32, 'attn_sinks': _jnp.float32, 'gate_ln_g': _jnp.float32, 'gate_ln_b': _jnp.float32, 'w_spatial': _jnp.float32, 'b_spatial': _jnp.float32, 'out_norm_attn_g': _jnp.float32, 'out_norm_gate_g': _jnp.float32, 'w_out': _jnp.float32, 'norm2_g': _jnp.float32, 'w_ffn_gate': _jnp.float32, 'w_ffn_up': _jnp.float32, 'w_ffn_down': _jnp.float32}
MOMENT_SCALE = {'norm1_g': 3.524862e-01, 'w_in': 2.526703e-01, 'q_norm_g': 9.725611e-01, 'k_norm_g': 1.094187e+00, 'attn_sinks': 7.218838e+00, 'gate_ln_g': 7.883549e-01, 'gate_ln_b': 1.293884e-01, 'w_spatial': 1.146674e-01, 'b_spatial': 1.932054e-01, 'out_norm_attn_g': 1.903500e+01, 'out_norm_gate_g': 1.610901e+01, 'w_out': 1.077735e+00, 'norm2_g': 1.245419e+01, 'w_ffn_gate': 2.439894e-01, 'w_ffn_up': 1.286994e-01, 'w_ffn_down': 1.843770e-01}


def _to_microbatches(a, axis):
    t = _jnp.moveaxis(a, axis, 0)
    t = t.reshape((N_MICROBATCH, t.shape[0] // N_MICROBATCH) + t.shape[1:])
    return _jnp.moveaxis(t, 1, axis + 1)


def setup_inputs(seed: int = 0) -> dict:
    inp = _fwd_setup_inputs(seed)
    key = _jax.random.fold_in(_jax.random.key(seed), 7919)
    shape, _ = _output_shape()
    out = dict(inp)
    out["loss_target"] = _jax.random.normal(_jax.random.fold_in(key, 0), shape, _jnp.float32)
    for i, name in enumerate(TWIN_WEIGHTS):
        w = inp[name].astype(_jnp.float32)
        if MOMENT_SCALE is None:
            s = _jnp.sqrt(_jnp.mean(_jnp.square(w)) + 1e-30)
        else:
            s = MOMENT_SCALE[name]
        km, kv = _jax.random.split(_jax.random.fold_in(key, i + 1))
        out[name] = w
        out["m_" + name] = s * _jax.random.normal(km, w.shape, _jnp.float32)
        out["v_" + name] = (s * s) * _jax.random.uniform(kv, w.shape, _jnp.float32, 0.5, 1.5)
    if N_MICROBATCH > 1:
        for name, axis in PER_EXAMPLE_BATCH_AXIS.items():
            out[name] = _to_microbatches(out[name], axis)
    return {'x': out['x'], 'norm1_g': out['norm1_g'], 'w_in': out['w_in'], 'q_norm_g': out['q_norm_g'], 'k_norm_g': out['k_norm_g'], 'attn_sinks': out['attn_sinks'], 'gate_ln_g': out['gate_ln_g'], 'gate_ln_b': out['gate_ln_b'], 'w_spatial': out['w_spatial'], 'b_spatial': out['b_spatial'], 'out_norm_attn_g': out['out_norm_attn_g'], 'out_norm_gate_g': out['out_norm_gate_g'], 'w_out': out['w_out'], 'norm2_g': out['norm2_g'], 'w_ffn_gate': out['w_ffn_gate'], 'w_ffn_up': out['w_ffn_up'], 'w_ffn_down': out['w_ffn_down'], 'loss_target': out['loss_target'], 'm_norm1_g': out['m_norm1_g'], 'm_w_in': out['m_w_in'], 'm_q_norm_g': out['m_q_norm_g'], 'm_k_norm_g': out['m_k_norm_g'], 'm_attn_sinks': out['m_attn_sinks'], 'm_gate_ln_g': out['m_gate_ln_g'], 'm_gate_ln_b': out['m_gate_ln_b'], 'm_w_spatial': out['m_w_spatial'], 'm_b_spatial': out['m_b_spatial'], 'm_out_norm_attn_g': out['m_out_norm_attn_g'], 'm_out_norm_gate_g': out['m_out_norm_gate_g'], 'm_w_out': out['m_w_out'], 'm_norm2_g': out['m_norm2_g'], 'm_w_ffn_gate': out['m_w_ffn_gate'], 'm_w_ffn_up': out['m_w_ffn_up'], 'm_w_ffn_down': out['m_w_ffn_down'], 'v_norm1_g': out['v_norm1_g'], 'v_w_in': out['v_w_in'], 'v_q_norm_g': out['v_q_norm_g'], 'v_k_norm_g': out['v_k_norm_g'], 'v_attn_sinks': out['v_attn_sinks'], 'v_gate_ln_g': out['v_gate_ln_g'], 'v_gate_ln_b': out['v_gate_ln_b'], 'v_w_spatial': out['v_w_spatial'], 'v_b_spatial': out['v_b_spatial'], 'v_out_norm_attn_g': out['v_out_norm_attn_g'], 'v_out_norm_gate_g': out['v_out_norm_gate_g'], 'v_w_out': out['v_w_out'], 'v_norm2_g': out['v_norm2_g'], 'v_w_ffn_gate': out['v_w_ffn_gate'], 'v_w_ffn_up': out['v_w_ffn_up'], 'v_w_ffn_down': out['v_w_ffn_down']}


def _loss(weights, diff, rest, loss_target):
    with _jax.named_scope("forward"):
        args = {**rest, TWIN_DIFF_INPUT: diff, **{k: w.astype(_WEIGHT_DTYPES[k]) for k, w in weights.items()}}
        y = _forward(args)
    with _jax.named_scope("loss_head"):
        err = _jnp.square(y.astype(_jnp.float32) - loss_target)
        return 0.5 * _jnp.sum(_jnp.mean(err, axis=-1)) if err.ndim else 0.5 * err


def _adamw(w, g, m, v):
    m = ADAM_B1 * m + (1.0 - ADAM_B1) * g
    v = ADAM_B2 * v + (1.0 - ADAM_B2) * _jnp.square(g)
    m_hat = m / (1.0 - ADAM_B1 ** ADAM_STEP)
    v_hat = v / (1.0 - ADAM_B2 ** ADAM_STEP)
    delta = -ADAM_LR * (m_hat / (_jnp.sqrt(v_hat) + ADAM_EPS) + ADAM_WD * w)
    return delta, m, v


def reference(x, norm1_g, w_in, q_norm_g, k_norm_g, attn_sinks, gate_ln_g, gate_ln_b, w_spatial, b_spatial, out_norm_attn_g, out_norm_gate_g, w_out, norm2_g, w_ffn_gate, w_ffn_up, w_ffn_down, loss_target, m_norm1_g, m_w_in, m_q_norm_g, m_k_norm_g, m_attn_sinks, m_gate_ln_g, m_gate_ln_b, m_w_spatial, m_b_spatial, m_out_norm_attn_g, m_out_norm_gate_g, m_w_out, m_norm2_g, m_w_ffn_gate, m_w_ffn_up, m_w_ffn_down, v_norm1_g, v_w_in, v_q_norm_g, v_k_norm_g, v_attn_sinks, v_gate_ln_g, v_gate_ln_b, v_w_spatial, v_b_spatial, v_out_norm_attn_g, v_out_norm_gate_g, v_w_out, v_norm2_g, v_w_ffn_gate, v_w_ffn_up, v_w_ffn_down):
    given = dict(x=x, norm1_g=norm1_g, w_in=w_in, q_norm_g=q_norm_g, k_norm_g=k_norm_g, attn_sinks=attn_sinks, gate_ln_g=gate_ln_g, gate_ln_b=gate_ln_b, w_spatial=w_spatial, b_spatial=b_spatial, out_norm_attn_g=out_norm_attn_g, out_norm_gate_g=out_norm_gate_g, w_out=w_out, norm2_g=norm2_g, w_ffn_gate=w_ffn_gate, w_ffn_up=w_ffn_up, w_ffn_down=w_ffn_down, loss_target=loss_target, m_norm1_g=m_norm1_g, m_w_in=m_w_in, m_q_norm_g=m_q_norm_g, m_k_norm_g=m_k_norm_g, m_attn_sinks=m_attn_sinks, m_gate_ln_g=m_gate_ln_g, m_gate_ln_b=m_gate_ln_b, m_w_spatial=m_w_spatial, m_b_spatial=m_b_spatial, m_out_norm_attn_g=m_out_norm_attn_g, m_out_norm_gate_g=m_out_norm_gate_g, m_w_out=m_w_out, m_norm2_g=m_norm2_g, m_w_ffn_gate=m_w_ffn_gate, m_w_ffn_up=m_w_ffn_up, m_w_ffn_down=m_w_ffn_down, v_norm1_g=v_norm1_g, v_w_in=v_w_in, v_q_norm_g=v_q_norm_g, v_k_norm_g=v_k_norm_g, v_attn_sinks=v_attn_sinks, v_gate_ln_g=v_gate_ln_g, v_gate_ln_b=v_gate_ln_b, v_w_spatial=v_w_spatial, v_b_spatial=v_b_spatial, v_out_norm_attn_g=v_out_norm_attn_g, v_out_norm_gate_g=v_out_norm_gate_g, v_w_out=v_w_out, v_norm2_g=v_norm2_g, v_w_ffn_gate=v_w_ffn_gate, v_w_ffn_up=v_w_ffn_up, v_w_ffn_down=v_w_ffn_down)
    weights = {n: given[n] for n in TWIN_WEIGHTS}
    shared = {n: given[n] for n in SHARED_INPUTS}
    per_example = {n: given[n] for n in ['x']}
    grad_fn = _jax.value_and_grad(_loss, argnums=(0, 1))

    def one_microbatch(ex, loss_target):
        ex = dict(ex)
        diff = ex.pop(TWIN_DIFF_INPUT)
        return grad_fn(weights, diff, {**shared, **ex}, loss_target)

    if N_MICROBATCH == 1:
        loss, (grad_w, grad_x) = one_microbatch(per_example, given["loss_target"])
    else:
        def body(carry, xs):
            loss_sum, grad_sum = carry
            l_k, (gw_k, gx_k) = one_microbatch(xs[0], xs[1])
            with _jax.named_scope("update"):
                return (loss_sum + l_k, _jax.tree.map(_jnp.add, grad_sum, gw_k)), gx_k

        init = (_jnp.zeros((), _jnp.float32), _jax.tree.map(_jnp.zeros_like, weights))
        (loss, grad_w), grad_x = _jax.lax.scan(body, init, (per_example, given["loss_target"]))
    with _jax.named_scope("update"):
        delta_w, new_m, new_v = {}, {}, {}
        for n in TWIN_WEIGHTS:
            delta_w[n], new_m[n], new_v[n] = _adamw(weights[n], grad_w[n], given["m_" + n], given["v_" + n])
    return (loss, grad_x, *[grad_w[n] for n in TWIN_WEIGHTS], *[delta_w[n] for n in TWIN_WEIGHTS],
            *[new_m[n] for n in TWIN_WEIGHTS], *[new_v[n] for n in TWIN_WEIGHTS])
```

```python
import functools
import math

import jax
import jax.numpy as jnp
from jax import lax
from jax.experimental import pallas as pl
from jax.experimental.pallas import tpu as pltpu

F32 = jnp.float32
MXU = jnp.bfloat16
WIRE = jnp.bfloat16
EPS = 1e-6
BLOCK = 128
GROUP_DIM = 128
N_KV_HEADS = 2
NEG = -1e30
N_DEV = 8
LANES = 128
MIB = 1024 * 1024

ADAM_LR = 0.001
ADAM_B1 = 0.9
ADAM_B2 = 0.999
ADAM_EPS = 1e-08
ADAM_WD = 0.01
ADAM_STEP = 10

MESH = pl.DeviceIdType.MESH
ANY = pl.BlockSpec(memory_space=pl.ANY)


def _pick(n, cands):
    for c in cands:
        if n % c == 0:
            return c
    return n


def _cparams(sem, vmem_mb):
    return pltpu.CompilerParams(dimension_semantics=sem, vmem_limit_bytes=vmem_mb * MIB)


def _matmul(name, a, b, mode, tm, tn, tk, out_defs, epilogue=None, extras=(), vmem_mb=48):
    if mode == "nn":
        (m, kk), n = a.shape, b.shape[1]
        a_spec = pl.BlockSpec((tm, tk), lambda i, j, k: (i, k))
        b_spec = pl.BlockSpec((tk, tn), lambda i, j, k: (k, j))
        dn = (((1,), (0,)), ((), ()))
    elif mode == "nt":
        (m, kk), n = a.shape, b.shape[0]
        a_spec = pl.BlockSpec((tm, tk), lambda i, j, k: (i, k))
        b_spec = pl.BlockSpec((tn, tk), lambda i, j, k: (j, k))
        dn = (((1,), (1,)), ((), ()))
    else:
        (kk, m), n = a.shape, b.shape[1]
        a_spec = pl.BlockSpec((tk, tm), lambda i, j, k: (k, i))
        b_spec = pl.BlockSpec((tk, tn), lambda i, j, k: (k, j))
        dn = (((0,), (0,)), ((), ()))
    assert m % tm == 0 and n % tn == 0 and kk % tk == 0, (name, m, n, kk, tm, tn, tk)
    nk = kk // tk
    ne, no = len(extras), len(out_defs)

    def body(a_ref, b_ref, *rest):
        extra_refs, out_refs = rest[:ne], rest[ne:ne + no]

        def finish(acc):
            tiles = (acc,) if epilogue is None else epilogue(acc, *[r[...] for r in extra_refs])
            for o_ref, t in zip(out_refs, tiles):
                o_ref[...] = t.astype(o_ref.dtype)

        p = lax.dot_general(a_ref[...], b_ref[...], dn, preferred_element_type=F32)
        if nk == 1:
            finish(p)
        else:
            acc_ref = rest[-1]
            k = pl.program_id(2)

            @pl.when(k == 0)
            def _():
                acc_ref[...] = p

            @pl.when(k > 0)
            def _():
                acc_ref[...] += p

            @pl.when(k == nk - 1)
            def _():
                finish(acc_ref[...])

    return pl.pallas_call(
        body,
        name=name,
        grid=(m // tm, n // tn, nk),
        in_specs=[a_spec, b_spec] + [pl.BlockSpec((tm, tn), lambda i, j, k: (i, j)) for _ in extras],
        out_specs=[pl.BlockSpec(blk, functools.partial(lambda i, j, k, f: f(i, j), f=imap)) for (_, _, blk, imap) in out_defs],
        out_shape=[jax.ShapeDtypeStruct(shp, dt) for (shp, dt, _, _) in out_defs],
        scratch_shapes=[pltpu.VMEM((tm, tn), F32)] if nk > 1 else [],
        compiler_params=_cparams(("parallel", "parallel", "arbitrary"), vmem_mb),
    )(a, b, *extras)


def _tile_out(m, n, tm, tn, dtype):
    return ((m, n), dtype, (tm, tn), lambda i, j: (i, j))


def _rms_fwd(name, x, g):
    t, d = x.shape
    tm = _pick(t, (512, 256, 128))

    def body(x_ref, g_ref, h_ref):
        xv = x_ref[...]
        r = lax.rsqrt(jnp.mean(xv * xv, axis=-1, keepdims=True) + EPS)
        h_ref[...] = (xv * r * g_ref[...]).astype(h_ref.dtype)

    return pl.pallas_call(
        body, name=name, grid=(t // tm,),
        in_specs=[pl.BlockSpec((tm, d), lambda i: (i, 0)), pl.BlockSpec((1, d), lambda i: (0, 0))],
        out_specs=pl.BlockSpec((tm, d), lambda i: (i, 0)),
        out_shape=jax.ShapeDtypeStruct((t, d), MXU),
        compiler_params=_cparams(("parallel",), 32),
    )(x, g)


def _rms_bwd(name, dh, x, g, res):
    t, d = x.shape
    tm = _pick(t, (256, 128))

    def body(dh_ref, x_ref, g_ref, res_ref, dx_ref, dxb_ref, dg_ref):
        @pl.when(pl.program_id(0) == 0)
        def _():
            dg_ref[...] = jnp.zeros_like(dg_ref)

        xv, dhv = x_ref[...], dh_ref[...]
        r = lax.rsqrt(jnp.mean(xv * xv, axis=-1, keepdims=True) + EPS)
        xh = xv * r
        dg_ref[...] += jnp.sum(dhv * xh, axis=0, keepdims=True)
        dxh = dhv * g_ref[...]
        dx = res_ref[...] + r * (dxh - xh * jnp.mean(dxh * xh, axis=-1, keepdims=True))
        dx_ref[...] = dx
        dxb_ref[...] = dx.astype(dxb_ref.dtype)

    row = pl.BlockSpec((tm, d), lambda i: (i, 0))
    vec = pl.BlockSpec((1, d), lambda i: (0, 0))
    return pl.pallas_call(
        body, name=name, grid=(t // tm,),
        in_specs=[row, row, vec, row],
        out_specs=[row, row, vec],
        out_shape=[jax.ShapeDtypeStruct((t, d), F32), jax.ShapeDtypeStruct((t, d), MXU), jax.ShapeDtypeStruct((1, d), F32)],
        compiler_params=_cparams(("arbitrary",), 40),
    )(dh, x, g, res)


_INV_SQRT2 = 0.7071067811865476
_INV_SQRT_2PI = 0.3989422804014327


def _dot_nt(a, b):
    return lax.dot_general(a, b, (((1,), (1,)), ((), ())), preferred_element_type=F32)


def _dot_tn(a, b):
    return lax.dot_general(a, b, (((0,), (0,)), ((), ())), preferred_element_type=F32)


def _dot(a, b):
    return jnp.dot(a, b, preferred_element_type=F32)


def _row_rms(v):
    return lax.rsqrt(jnp.mean(v * v, axis=-1, keepdims=True) + EPS)


class _Dims:
    def __init__(self, d_model, in_width, head_dim):
        self.d = d_model
        self.aw = d_model // 2
        self.gw = d_model - self.aw
        self.kvw = (in_width - self.aw - 2 * self.gw) // 2
        self.hd = head_dim
        self.nh = self.aw // head_dim
        self.nkv = self.kvw // head_dim
        self.grp = self.nh // self.nkv
        self.ng = self.gw // GROUP_DIM
        self.inw = in_width
        self.zoff = self.aw + 2 * self.kvw
        assert self.nkv == N_KV_HEADS and self.zoff + 2 * self.gw == in_width


def _band_masks(first):
    qi = lax.broadcasted_iota(jnp.int32, (BLOCK, 2 * BLOCK), 0)
    kj = lax.broadcasted_iota(jnp.int32, (BLOCK, 2 * BLOCK), 1)
    dist = qi + BLOCK - kj
    valid = (dist >= 0) & (dist < BLOCK) & ((kj >= BLOCK) | jnp.logical_not(first))
    return valid, dist.astype(F32)


def _attn_head_fwd(dm, h, q_raw, gq, knb, vbb, sink, valid, dist):
    rq = _row_rms(q_raw)
    qh = q_raw * rq
    qnb = (qh * gq).astype(MXU)
    slope = math.pow(2.0, -8.0 * (h + 1) / dm.nh)
    s = _dot_nt(qnb, knb) * (dm.hd ** -0.5)
    logits = jnp.where(valid, s - slope * dist, NEG)
    m = jnp.maximum(jnp.max(logits, axis=-1, keepdims=True), sink)
    e = jnp.exp(logits - m)
    es = jnp.exp(sink - m)
    inv = 1.0 / (jnp.sum(e, axis=-1, keepdims=True) + es)
    p = e * inv
    o = _dot(p.astype(MXU), vbb)
    return o, p, es * inv, rq, qh, qnb


def _gate_group_fwd(g, u_g, v_g, lng, lnb, ws_ref, bst_ref, tril):
    mu = jnp.mean(v_g, axis=-1, keepdims=True)
    xc = v_g - mu
    rstd = lax.rsqrt(jnp.mean(xc * xc, axis=-1, keepdims=True) + EPS)
    xh = xc * rstd
    vnb = (xh * lng[g:g + 1, :] + lnb[g:g + 1, :]).astype(MXU)
    wt = jnp.where(tril, ws_ref[g], 0.0).astype(MXU)
    mixed = _dot(wt, vnb) + bst_ref[:, g:g + 1]
    return u_g * mixed, mixed, xh, rstd, vnb, wt


def _gelu_cdf(z):
    return 0.5 * (1.0 + lax.erf(z * _INV_SQRT2))


def _mixer_specs(dm, nb, clamp):
    kvblk = dm.aw // (2 * dm.kvw)

    def cur(s, i):
        return (s * nb + clamp(i), 0)

    def prev(s, i):
        return (s * nb + jnp.maximum(clamp(i) - 1, 0), kvblk)

    full = lambda shape: pl.BlockSpec(shape, lambda s, i: tuple(0 for _ in shape))
    return cur, prev, full


def _mixer_fwd(proj, prm, dm, nseq, nb):
    gq, gk, sinks, lng, lnb, ws, bst, goa, gog = prm
    t = proj.shape[0]
    cur, prev, full = _mixer_specs(dm, nb, lambda i: i)
    assert dm.aw % (2 * dm.kvw) == 0

    def body(proj_ref, pkv_ref, gq_ref, gk_ref, sink_ref, lng_ref, lnb_ref, ws_ref, bst_ref, goa_ref, gog_ref, y_ref, att_scr, gate_scr):
        i = pl.program_id(1)
        valid, dist = _band_masks(i == 0)
        gqv, gkv = gq_ref[...], gk_ref[...]
        for kh in range(dm.nkv):
            ko = dm.aw + kh * dm.hd
            vo = dm.aw + dm.kvw + kh * dm.hd
            kb = jnp.concatenate([pkv_ref[:, kh * dm.hd:(kh + 1) * dm.hd], proj_ref[:, ko:ko + dm.hd]], axis=0)
            vb = jnp.concatenate([pkv_ref[:, dm.kvw + kh * dm.hd:dm.kvw + (kh + 1) * dm.hd], proj_ref[:, vo:vo + dm.hd]], axis=0)
            knb = (kb * _row_rms(kb) * gkv).astype(MXU)
            vbb = vb.astype(MXU)
            for g in range(dm.grp):
                h = kh * dm.grp + g
                o = _attn_head_fwd(dm, h, proj_ref[:, h * dm.hd:(h + 1) * dm.hd], gqv, knb, vbb, sink_ref[h], valid, dist)[0]
                att_scr[:, h * dm.hd:(h + 1) * dm.hd] = o
        att = att_scr[...]
        y_ref[:, :dm.aw] = (att * _row_rms(att) * goa_ref[...]).astype(y_ref.dtype)

        tril = lax.broadcasted_iota(jnp.int32, (BLOCK, BLOCK), 0) >= lax.broadcasted_iota(jnp.int32, (BLOCK, BLOCK), 1)
        lngv, lnbv = lng_ref[...], lnb_ref[...]
        for g in range(dm.ng):
            zu = proj_ref[:, dm.zoff + g * GROUP_DIM:dm.zoff + (g + 1) * GROUP_DIM]
            zv = proj_ref[:, dm.zoff + dm.gw + g * GROUP_DIM:dm.zoff + dm.gw + (g + 1) * GROUP_DIM]
            u_g, v_g = zu * _gelu_cdf(zu), zv * _gelu_cdf(zv)
            gate_scr[:, g * GROUP_DIM:(g + 1) * GROUP_DIM] = _gate_group_fwd(g, u_g, v_g, lngv, lnbv, ws_ref, bst_ref, tril)[0]
        gt = gate_scr[...]
        y_ref[:, dm.aw:] = (gt * _row_rms(gt) * gog_ref[...]).astype(y_ref.dtype)

    return pl.pallas_call(
        body, name="mixer_fwd", grid=(nseq, nb),
        in_specs=[pl.BlockSpec((BLOCK, dm.inw), cur), pl.BlockSpec((BLOCK, 2 * dm.kvw), prev),
                  full(gq.shape), full(gk.shape), pl.BlockSpec(memory_space=pltpu.SMEM),
                  full(lng.shape), full(lnb.shape), full(ws.shape), full(bst.shape), full(goa.shape), full(gog.shape)],
        out_specs=pl.BlockSpec((BLOCK, dm.d), cur),
        out_shape=jax.ShapeDtypeStruct((t, dm.d), MXU),
        scratch_shapes=[pltpu.VMEM((BLOCK, dm.aw), F32), pltpu.VMEM((BLOCK, dm.gw), F32)],
        compiler_params=_cparams(("parallel", "arbitrary"), 40),
    )(proj, proj, gq, gk, sinks, lng, lnb, ws, bst, goa, gog)


def _mixer_bwd(proj, dy, prm, dm, nseq, nb):
    gq, gk, sinks, lng, lnb, ws, bst, goa, gog = prm
    t = proj.shape[0]
    clamp = lambda i: jnp.minimum(i, nb - 1)
    cur, prev, full = _mixer_specs(dm, nb, clamp)
    kvw2 = 2 * dm.kvw

    def prev_kv_out(s, i):
        return (s * nb + jnp.maximum(i - 1, 0), 0)

    def body(proj_ref, pkv_ref, dy_ref, gq_ref, gk_ref, sink_ref, lng_ref, lnb_ref, ws_ref, bst_ref, goa_ref, gog_ref,
             dproj_ref, dkv_ref, dgq_ref, dgk_ref, dsink_ref, dlng_ref, dlnb_ref, dws_ref, dbst_ref, dgoa_ref, dgog_ref,
             att_scr, datt_scr, gate_scr, carry_scr, prevpart_scr, curpart_scr, kprev_scr):
        s_id, i = pl.program_id(0), pl.program_id(1)

        @pl.when((s_id == 0) & (i == 0))
        def _():
            for r in (dgq_ref, dgk_ref, dsink_ref, dlng_ref, dlnb_ref, dws_ref, dbst_ref, dgoa_ref, dgog_ref):
                r[...] = jnp.zeros_like(r)

        gqv, gkv = gq_ref[...], gk_ref[...]

        @pl.when(i < nb)
        def _():
            valid, dist = _band_masks(i == 0)
            lane = lax.broadcasted_iota(jnp.int32, (1, LANES), 1)
            heads = []
            kvs = []
            for kh in range(dm.nkv):
                ko = dm.aw + kh * dm.hd
                vo = dm.aw + dm.kvw + kh * dm.hd
                kb = jnp.concatenate([pkv_ref[:, kh * dm.hd:(kh + 1) * dm.hd], proj_ref[:, ko:ko + dm.hd]], axis=0)
                vb = jnp.concatenate([pkv_ref[:, dm.kvw + kh * dm.hd:dm.kvw + (kh + 1) * dm.hd], proj_ref[:, vo:vo + dm.hd]], axis=0)
                knb = (kb * _row_rms(kb) * gkv).astype(MXU)
                vbb = vb.astype(MXU)
                kvs.append((knb, vbb))
                for g in range(dm.grp):
                    h = kh * dm.grp + g
                    o = _attn_head_fwd(dm, h, proj_ref[:, h * dm.hd:(h + 1) * dm.hd], gqv, knb, vbb, sink_ref[h], valid, dist)[0]
                    att_scr[:, h * dm.hd:(h + 1) * dm.hd] = o
            att = att_scr[...]
            dya = dy_ref[:, :dm.aw]
            ra = _row_rms(att)
            ah = att * ra
            dgoa_ref[...] += jnp.sum(dya * ah, axis=0, keepdims=True)
            dah = dya * goa_ref[...]
            datt_scr[...] = ra * (dah - ah * jnp.mean(dah * ah, axis=-1, keepdims=True))
            dsink_acc = jnp.zeros((1, LANES), F32)
            dgq_acc = jnp.zeros((1, dm.hd), F32)
            for kh in range(dm.nkv):
                knb, vbb = kvs[kh]
                dkn = jnp.zeros((2 * BLOCK, dm.hd), F32)
                dvb = jnp.zeros((2 * BLOCK, dm.hd), F32)
                for g in range(dm.grp):
                    h = kh * dm.grp + g
                    _, p, ps, rq, qh, qnb = _attn_head_fwd(dm, h, proj_ref[:, h * dm.hd:(h + 1) * dm.hd], gqv, knb, vbb, sink_ref[h], valid, dist)
                    do_b = datt_scr[:, h * dm.hd:(h + 1) * dm.hd].astype(MXU)
                    dp = _dot_nt(do_b, vbb)
                    delta = jnp.sum(p * dp, axis=-1, keepdims=True)
                    dsb = (p * (dp - delta) * (dm.hd ** -0.5)).astype(MXU)
                    dsink_acc += jnp.where(lane == h, -jnp.sum(ps * delta, axis=0, keepdims=True), 0.0)
                    dqn = _dot(dsb, knb)
                    dkn += _dot_tn(dsb, qnb)
                    dvb += _dot_tn(p.astype(MXU), do_b)
                    dgq_acc += jnp.sum(dqn * qh, axis=0, keepdims=True)
                    dqh = dqn * gqv
                    dq = rq * (dqh - qh * jnp.mean(dqh * qh, axis=-1, keepdims=True))
                    dproj_ref[:, h * dm.hd:(h + 1) * dm.hd] = dq.astype(dproj_ref.dtype)
                prevpart_scr[:, kh * dm.hd:(kh + 1) * dm.hd] = dkn[:BLOCK]
                prevpart_scr[:, dm.kvw + kh * dm.hd:dm.kvw + (kh + 1) * dm.hd] = dvb[:BLOCK]
                curpart_scr[:, kh * dm.hd:(kh + 1) * dm.hd] = dkn[BLOCK:]
                curpart_scr[:, dm.kvw + kh * dm.hd:dm.kvw + (kh + 1) * dm.hd] = dvb[BLOCK:]
            dsink_ref[...] += dsink_acc
            dgq_ref[...] += dgq_acc
            dproj_ref[:, dm.aw:dm.zoff] = jnp.zeros((BLOCK, kvw2), dproj_ref.dtype)
            tril = lax.broadcasted_iota(jnp.int32, (BLOCK, BLOCK), 0) >= lax.broadcasted_iota(jnp.int32, (BLOCK, BLOCK), 1)
            lngv, lnbv = lng_ref[...], lnb_ref[...]
            saved = []
            for g in range(dm.ng):
                zu = proj_ref[:, dm.zoff + g * GROUP_DIM:dm.zoff + (g + 1) * GROUP_DIM]
                zv = proj_ref[:, dm.zoff + dm.gw + g * GROUP_DIM:dm.zoff + dm.gw + (g + 1) * GROUP_DIM]
                u_g, v_g = zu * _gelu_cdf(zu), zv * _gelu_cdf(zv)
                yg, mixed, xh, rstd, vnb, wt = _gate_group_fwd(g, u_g, v_g, lngv, lnbv, ws_ref, bst_ref, tril)
                gate_scr[:, g * GROUP_DIM:(g + 1) * GROUP_DIM] = yg
                saved.append((u_g, mixed, xh, rstd, vnb, wt))
            gt = gate_scr[...]
            dyg = dy_ref[:, dm.aw:]
            rg = _row_rms(gt)
            gh = gt * rg
            dgog_ref[...] += jnp.sum(dyg * gh, axis=0, keepdims=True)
            dgh = dyg * gog_ref[...]
            gate_scr[...] = rg * (dgh - gh * jnp.mean(dgh * gh, axis=-1, keepdims=True))
            for g in range(dm.ng):
                u_g, mixed, xh, rstd, vnb, wt = saved[g]
                dyg_g = gate_scr[:, g * GROUP_DIM:(g + 1) * GROUP_DIM]
                du = dyg_g * mixed
                dmix = dyg_g * u_g
                dmixb = dmix.astype(MXU)
                dbst_ref[:, g:g + 1] += jnp.sum(dmix, axis=-1, keepdims=True)
                dws_ref[g] += jnp.where(tril, _dot_nt(dmixb, vnb), 0.0)
                dvn = _dot_tn(wt, dmixb)
                dlng_ref[g:g + 1, :] += jnp.sum(dvn * xh, axis=0, keepdims=True)
                dlnb_ref[g:g + 1, :] += jnp.sum(dvn, axis=0, keepdims=True)
                dxh = dvn * lngv[g:g + 1, :]
                dv = rstd * (dxh - jnp.mean(dxh, axis=-1, keepdims=True) - xh * jnp.mean(dxh * xh, axis=-1, keepdims=True))
                uo = dm.zoff + g * GROUP_DIM
                vo = dm.zoff + dm.gw + g * GROUP_DIM
                zu, zv = proj_ref[:, uo:uo + GROUP_DIM], proj_ref[:, vo:vo + GROUP_DIM]
                dgu = _gelu_cdf(zu) + zu * (jnp.exp(-0.5 * zu * zu) * _INV_SQRT_2PI)
                dgv = _gelu_cdf(zv) + zv * (jnp.exp(-0.5 * zv * zv) * _INV_SQRT_2PI)
                dproj_ref[:, uo:uo + GROUP_DIM] = (du * dgu).astype(dproj_ref.dtype)
                dproj_ref[:, vo:vo + GROUP_DIM] = (dv * dgv).astype(dproj_ref.dtype)

        @pl.when(i == nb)
        def _():
            prevpart_scr[...] = jnp.zeros_like(prevpart_scr)

        @pl.when(i >= 1)
        def _():
            tot = carry_scr[...] + prevpart_scr[...]
            dgk_acc = jnp.zeros((1, dm.hd), F32)
            for kh in range(dm.nkv):
                kraw = kprev_scr[:, kh * dm.hd:(kh + 1) * dm.hd]
                rk = _row_rms(kraw)
                khat = kraw * rk
                dkn = tot[:, kh * dm.hd:(kh + 1) * dm.hd]
                dgk_acc += jnp.sum(dkn * khat, axis=0, keepdims=True)
                dkh = dkn * gkv
                dk = rk * (dkh - khat * jnp.mean(dkh * khat, axis=-1, keepdims=True))
                dkv_ref[:, kh * dm.hd:(kh + 1) * dm.hd] = dk.astype(dkv_ref.dtype)
            dkv_ref[:, dm.kvw:] = tot[:, dm.kvw:].astype(dkv_ref.dtype)
            dgk_ref[...] += dgk_acc

        @pl.when(i < nb)
        def _():
            carry_scr[...] = curpart_scr[...]
            kprev_scr[...] = proj_ref[:, dm.aw:dm.aw + dm.kvw]

    acc = lambda a: jax.ShapeDtypeStruct(a.shape, F32)
    outs = pl.pallas_call(
        body, name="mixer_bwd", grid=(nseq, nb + 1),
        in_specs=[pl.BlockSpec((BLOCK, dm.inw), cur), pl.BlockSpec((BLOCK, kvw2), prev), pl.BlockSpec((BLOCK, dm.d), cur),
                  full(gq.shape), full(gk.shape), pl.BlockSpec(memory_space=pltpu.SMEM),
                  full(lng.shape), full(lnb.shape), full(ws.shape), full(bst.shape), full(goa.shape), full(gog.shape)],
        out_specs=[pl.BlockSpec((BLOCK, dm.inw), cur), pl.BlockSpec((BLOCK, kvw2), prev_kv_out),
                   full(gq.shape), full(gk.shape), full((1, LANES)), full(lng.shape), full(lnb.shape), full(ws.shape),
                   full(bst.shape), full(goa.shape), full(gog.shape)],
        out_shape=[jax.ShapeDtypeStruct((t, dm.inw), MXU), jax.ShapeDtypeStruct((t, kvw2), MXU),
                   acc(gq), acc(gk), jax.ShapeDtypeStruct((1, LANES), F32), acc(lng), acc(lnb), acc(ws), acc(bst), acc(goa), acc(gog)],
        scratch_shapes=[pltpu.VMEM((BLOCK, dm.aw), F32), pltpu.VMEM((BLOCK, dm.aw), F32), pltpu.VMEM((BLOCK, dm.gw), F32),
                        pltpu.VMEM((BLOCK, kvw2), F32), pltpu.VMEM((BLOCK, kvw2), F32), pltpu.VMEM((BLOCK, kvw2), F32),
                        pltpu.VMEM((BLOCK, dm.kvw), F32)],
        compiler_params=_cparams(("arbitrary", "arbitrary"), 48),
    )(proj, proj, dy, gq, gk, sinks, lng, lnb, ws, bst, goa, gog)
    return outs


def _patch_kv(dproj, dkv, dm):
    t = dproj.shape[0]
    tm = _pick(t, (1024, 512, 256, 128))
    kvw2 = 2 * dm.kvw
    kvblk = dm.aw // kvw2

    def body(dproj_hbm, dkv_ref, out_ref):
        del dproj_hbm
        out_ref[...] = dkv_ref[...]

    return pl.pallas_call(
        body, name="patch_kv", grid=(t // tm,),
        in_specs=[ANY, pl.BlockSpec((tm, kvw2), lambda i: (i, 0))],
        out_specs=pl.BlockSpec((tm, kvw2), lambda i: (i, kvblk)),
        out_shape=jax.ShapeDtypeStruct(dproj.shape, dproj.dtype),
        input_output_aliases={0: 0},
        compiler_params=_cparams(("parallel",), 32),
    )(dproj, dkv)


def _place():
    x, y, c = lax.axis_index("x"), lax.axis_index("y"), lax.axis_index("c")
    return x, y, c


def _allgather_weights(shards):
    nw = len(shards)

    def body(*refs):
        src, out = refs[:nw], refs[nw:2 * nw]
        send_sems, recv_sems, local_sems = refs[2 * nw:]
        x, y, c = _place()
        me, sibling = (x, y, c), (x, y, 1 - c)
        chips = [(1 - x, y), (x, 1 - y), (1 - x, 1 - y)]

        def rows(w, place):
            n = src[w].shape[0]
            px, py, pc = place
            return out[w].at[pl.ds(pl.multiple_of((4 * px + 2 * py + pc) * n, 16), n), :]

        def copy(w, k, block, to, from_src=False):
            return pltpu.make_async_remote_copy(
                src_ref=src[w] if from_src else rows(w, block), dst_ref=rows(w, block),
                send_sem=send_sems.at[w, k], recv_sem=recv_sems.at[w, k], device_id=to, device_id_type=MESH)

        mine = [pltpu.make_async_copy(src[w], rows(w, me), local_sems.at[w]) for w in range(nw)]
        for cp in mine:
            cp.start()
        first = []
        for w in range(nw):
            first.append(copy(w, 0, me, sibling, from_src=True))
            first += [copy(w, 1 + j, me, (*chip, c), from_src=True) for j, chip in enumerate(chips)]
        for cp in first:
            cp.start()
        passed = []
        for j, chip in enumerate(chips):
            for w in range(nw):
                copy(w, 1 + j, (*chip, c), me).wait_recv()
                fwd = copy(w, 4 + j, (*chip, c), sibling)
                fwd.start()
                passed.append(fwd)
        for w in range(nw):
            copy(w, 0, sibling, me).wait_recv()
            for j, chip in enumerate(chips):
                copy(w, 4 + j, (*chip, 1 - c), me).wait_recv()
        for cp in first + passed:
            cp.wait_send()
        for cp in mine:
            cp.wait()

    return pl.pallas_call(
        body, name="allgather_weights",
        in_specs=[ANY] * nw, out_specs=[ANY] * nw,
        out_shape=[jax.ShapeDtypeStruct((N_DEV * s.shape[0], s.shape[1]), s.dtype) for s in shards],
        scratch_shapes=[pltpu.SemaphoreType.DMA((nw, 7)), pltpu.SemaphoreType.DMA((nw, 7)), pltpu.SemaphoreType.DMA((nw,))],
    )(*shards)


_FLIPS = [(0, 0, 1), (1, 0, 0), (0, 1, 0), (1, 1, 0), (1, 0, 1), (0, 1, 1), (1, 1, 1)]


def _scatter_grads(grads):
    nw = len(grads)

    def body(*refs):
        src, out = refs[:nw], refs[nw:2 * nw]
        send_sems, recv_sems, local_sems = refs[2 * nw:]
        x, y, c = _place()
        me_idx = 4 * x + 2 * y + c

        def block(w, idx):
            n = out[w].shape[1]
            return src[w].at[pl.ds(pl.multiple_of(idx * n, 16), n), :]

        copies = []
        for w in range(nw):
            for k, (fx, fy, fc) in enumerate(_FLIPS):
                px, py, pc = x ^ fx, y ^ fy, c ^ fc
                copies.append(pltpu.make_async_remote_copy(
                    src_ref=block(w, 4 * px + 2 * py + pc), dst_ref=out[w].at[me_idx],
                    send_sem=send_sems.at[w, k], recv_sem=recv_sems.at[w, k], device_id=(px, py, pc), device_id_type=MESH))
        mine = [pltpu.make_async_copy(block(w, me_idx), out[w].at[me_idx], local_sems.at[w]) for w in range(nw)]
        for cp in mine + copies:
            cp.start()
        for cp in copies:
            cp.wait_recv()
        for cp in copies:
            cp.wait_send()
        for cp in mine:
            cp.wait()

    return pl.pallas_call(
        body, name="scatter_grads",
        in_specs=[ANY] * nw, out_specs=[ANY] * nw,
        out_shape=[jax.ShapeDtypeStruct((N_DEV, g.shape[0] // N_DEV, g.shape[1]), g.dtype) for g in grads],
        scratch_shapes=[pltpu.SemaphoreType.DMA((nw, 7)), pltpu.SemaphoreType.DMA((nw, 7)), pltpu.SemaphoreType.DMA((nw,))],
    )(*grads)


def _sum_slots(name, slots):
    _, n, kk = slots.shape
    tr = _pick(n, (256, 208, 176, 128, 96, 64, 32, 16))

    def body(s_ref, o_ref):
        acc = s_ref[0].astype(F32)
        for p in range(1, N_DEV):
            acc = acc + s_ref[p].astype(F32)
        o_ref[...] = acc

    return pl.pallas_call(
        body, name=name, grid=(n // tr,),
        in_specs=[pl.BlockSpec((N_DEV, tr, kk), lambda i: (0, i, 0))],
        out_specs=pl.BlockSpec((tr, kk), lambda i: (i, 0)),
        out_shape=jax.ShapeDtypeStruct((n, kk), F32),
        compiler_params=_cparams(("parallel",), 40),
    )(slots)


def _adamw_math(w, g, m, v):
    m = ADAM_B1 * m + (1.0 - ADAM_B1) * g
    v = ADAM_B2 * v + (1.0 - ADAM_B2) * (g * g)
    m_hat = m / (1.0 - ADAM_B1 ** ADAM_STEP)
    v_hat = v / (1.0 - ADAM_B2 ** ADAM_STEP)
    delta = -ADAM_LR * (m_hat / (jnp.sqrt(v_hat) + ADAM_EPS) + ADAM_WD * w)
    return delta, m, v


def _adamw(name, w, g, m, v):
    r, cdim = w.shape
    tr = _pick(r, (256, 128, 64, 32, 16, 8))

    def body(w_ref, g_ref, m_ref, v_ref, d_ref, nm_ref, nv_ref):
        d_ref[...], nm_ref[...], nv_ref[...] = _adamw_math(w_ref[...], g_ref[...], m_ref[...], v_ref[...])

    spec = pl.BlockSpec((tr, cdim), lambda i: (i, 0))
    return pl.pallas_call(
        body, name=name, grid=(r // tr,),
        in_specs=[spec] * 4, out_specs=[spec] * 3,
        out_shape=[jax.ShapeDtypeStruct(w.shape, F32)] * 3,
        compiler_params=_cparams(("parallel",), 40),
    )(w, g, m, v)


def _allreduce_small_adamw(part, w, m, v):
    r = part.shape[0]

    def body(part_ref, w_ref, m_ref, v_ref, g_ref, d_ref, nm_ref, nv_ref, slots, send_sems, recv_sems):
        x, y, c = _place()
        me_idx = 4 * x + 2 * y + c
        copies = []
        for k, (fx, fy, fc) in enumerate(_FLIPS):
            px, py, pc = x ^ fx, y ^ fy, c ^ fc
            copies.append(pltpu.make_async_remote_copy(
                src_ref=part_ref, dst_ref=slots.at[me_idx], send_sem=send_sems.at[k], recv_sem=recv_sems.at[k],
                device_id=(px, py, pc), device_id_type=MESH))
        for cp in copies:
            cp.start()
        slots[me_idx] = part_ref[...]
        for cp in copies:
            cp.wait_recv()
        for cp in copies:
            cp.wait_send()
        g = slots[0]
        for p in range(1, N_DEV):
            g = g + slots[p]
        g_ref[...] = g
        d_ref[...], nm_ref[...], nv_ref[...] = _adamw_math(w_ref[...], g, m_ref[...], v_ref[...])

    vm = pl.BlockSpec(memory_space=pltpu.VMEM)
    return pl.pallas_call(
        body, name="allreduce_small_adamw",
        in_specs=[vm] * 4, out_specs=[vm] * 4,
        out_shape=[jax.ShapeDtypeStruct((r, LANES), F32)] * 4,
        scratch_shapes=[pltpu.VMEM((N_DEV, r, LANES), F32), pltpu.SemaphoreType.DMA((7,)), pltpu.SemaphoreType.DMA((7,))],
        compiler_params=pltpu.CompilerParams(vmem_limit_bytes=48 * MIB),
    )(part, w, m, v)


def _pack(arrs):
    parts, meta, off = [], [], 0
    for a in arrs:
        flat = a.reshape(-1).astype(F32)
        rows = -(-flat.shape[0] // LANES)
        rows8 = -(-rows // 8) * 8
        flat = jnp.pad(flat, (0, rows8 * LANES - flat.shape[0]))
        parts.append(flat.reshape(rows8, LANES))
        meta.append((off, a.shape, a.size))
        off += rows8
    return jnp.concatenate(parts, axis=0), meta


def _unpack(packed, meta):
    outs = []
    for off, shape, size in meta:
        rows = -(-size // LANES)
        outs.append(packed[off:off + rows].reshape(-1)[:size].reshape(shape))
    return outs


def _silu_parts(a):
    sg = 1.0 / (1.0 + jnp.exp(-a))
    return a * sg, sg * (1.0 + a * (1.0 - sg))


def kernel(x, norm1_g, w_in, q_norm_g, k_norm_g, attn_sinks, gate_ln_g, gate_ln_b, w_spatial, b_spatial, out_norm_attn_g, out_norm_gate_g, w_out, norm2_g, w_ffn_gate, w_ffn_up, w_ffn_down, loss_target, m_norm1_g, m_w_in, m_q_norm_g, m_k_norm_g, m_attn_sinks, m_gate_ln_g, m_gate_ln_b, m_w_spatial, m_b_spatial, m_out_norm_attn_g, m_out_norm_gate_g, m_w_out, m_norm2_g, m_w_ffn_gate, m_w_ffn_up, m_w_ffn_down, v_norm1_g, v_w_in, v_q_norm_g, v_k_norm_g, v_attn_sinks, v_gate_ln_g, v_gate_ln_b, v_w_spatial, v_b_spatial, v_out_norm_attn_g, v_out_norm_gate_g, v_w_out, v_norm2_g, v_w_ffn_gate, v_w_ffn_up, v_w_ffn_down):
    nseq, seq, d = x.shape
    t = nseq * seq
    nb = seq // BLOCK
    inw = w_in.shape[2] * N_DEV
    dff = w_ffn_gate.shape[2] * N_DEV
    dm = _Dims(d, inw, q_norm_g.shape[-1])
    xf = x.reshape(t, d)
    tgt = loss_target.reshape(t, d)

    shards = [w_in[0].T.astype(WIRE), w_out[0].astype(WIRE), w_ffn_gate[0].T.astype(WIRE),
              w_ffn_up[0].T.astype(WIRE), w_ffn_down[0].astype(WIRE)]
    win_t, wout, wg_t, wu_t, wd = _allgather_weights(shards)

    ws = w_spatial[0]
    prm = (q_norm_g, k_norm_g, attn_sinks[0], gate_ln_g[0], gate_ln_b[0], ws, b_spatial[0].T, out_norm_attn_g, out_norm_gate_g)

    tm = _pick(t, (512, 256, 128))
    h1 = _rms_fwd("rms1_fwd", xf, norm1_g)
    tn_in = _pick(inw, (1664, 1024, 512, 256, 128))
    (proj,) = _matmul("mm_in", h1, win_t, "nt", tm, tn_in, d, [_tile_out(t, inw, tm, tn_in, F32)])
    y = _mixer_fwd(proj, prm, dm, nseq, nb)
    tn_d = _pick(d, (1024, 512, 256))
    (x2,) = _matmul("mm_out", y, wout, "nn", tm, tn_d, d, [_tile_out(t, d, tm, tn_d, F32)],
                    epilogue=lambda acc, xr: (xr + acc,), extras=[xf])
    h2 = _rms_fwd("rms2_fwd", x2, norm2_g)
    tn_f = _pick(dff, (1408, 1024, 768, 512, 256, 128))
    (a,) = _matmul("mm_gate", h2, wg_t, "nt", tm, tn_f, d, [_tile_out(t, dff, tm, tn_f, F32)])
    b, s = _matmul("mm_up", h2, wu_t, "nt", tm, tn_f, d, [_tile_out(t, dff, tm, tn_f, F32), _tile_out(t, dff, tm, tn_f, MXU)],
                   epilogue=lambda acc, av: (acc, _silu_parts(av)[0] * acc), extras=[a])
    tk_f = _pick(dff, (1408, 1024, 768, 512, 256, 128))

    def loss_epilogue(acc, x2v, tv):
        diff = (x2v + acc) - tv
        dx3 = diff * (1.0 / d)
        return dx3, dx3, jnp.full((8, LANES), jnp.sum(diff * diff), F32)

    dx3, dx3b, lossp = _matmul(
        "mm_down", s, wd, "nn", tm, tn_d, tk_f,
        [_tile_out(t, d, tm, tn_d, F32), _tile_out(t, d, tm, tn_d, MXU),
         ((t // tm * 8, d // tn_d * LANES), F32, (8, LANES), lambda i, j: (i, j))],
        epilogue=loss_epilogue, extras=[x2, tgt])
    loss_part = (0.5 / d) * jnp.sum(lossp[::8, ::LANES])

    def dswiglu(acc, av, bv):
        silu, dsilu = _silu_parts(av)
        return acc * bv * dsilu, acc * silu

    da, db = _matmul("mm_d_down", dx3b, wd, "nt", tm, tn_f, d, [_tile_out(t, dff, tm, tn_f, MXU), _tile_out(t, dff, tm, tn_f, MXU)],
                     epilogue=dswiglu, extras=[a, b])
    tkt = _pick(t, (1024, 512, 256, 128))
    tmw_f = _pick(dff, (1408, 512, 256, 128))
    (g_wd,) = _matmul("mm_gw_down", s, dx3b, "tn", tmw_f, tn_d, tkt, [_tile_out(dff, d, tmw_f, tn_d, WIRE)])
    (dh2a,) = _matmul("mm_dh2_gate", da, wg_t, "nn", tm, tn_d, tk_f, [_tile_out(t, d, tm, tn_d, F32)])
    (dh2,) = _matmul("mm_dh2_up", db, wu_t, "nn", tm, tn_d, tk_f, [_tile_out(t, d, tm, tn_d, F32)],
                     epilogue=lambda acc, pv: (pv + acc,), extras=[dh2a])
    (g_wg,) = _matmul("mm_gw_gate", da, h2, "tn", tmw_f, tn_d, tkt, [_tile_out(dff, d, tmw_f, tn_d, WIRE)])
    (g_wu,) = _matmul("mm_gw_up", db, h2, "tn", tmw_f, tn_d, tkt, [_tile_out(dff, d, tmw_f, tn_d, WIRE)])
    dx2, dx2b, dg2 = _rms_bwd("rms2_bwd", dh2, x2, norm2_g, dx3)

    (dy,) = _matmul("mm_d_out", dx2b, wout, "nt", tm, tn_d, d, [_tile_out(t, d, tm, tn_d, F32)])
    tmw_d = _pick(d, (1024, 512, 256, 128))
    (g_wout,) = _matmul("mm_gw_out", y, dx2b, "tn", tmw_d, tn_d, tkt, [_tile_out(d, d, tmw_d, tn_d, WIRE)])
    (dproj0, dkv, dgq, dgk, dsink, dlng, dlnb, dws, dbst, dgoa, dgog) = _mixer_bwd(proj, dy, prm, dm, nseq, nb)
    dproj = _patch_kv(dproj0, dkv, dm)
    tk_in = _pick(inw, (1664, 1024, 512, 256, 128))
    (dh1,) = _matmul("mm_d_in", dproj, win_t, "nn", tm, tn_d, tk_in, [_tile_out(t, d, tm, tn_d, F32)])
    tmw_in = _pick(inw, (1664, 1024, 512, 256, 128))
    (g_win,) = _matmul("mm_gw_in", dproj, h1, "tn", tmw_in, tn_d, tkt, [_tile_out(inw, d, tmw_in, tn_d, WIRE)])
    dx, _, dg1 = _rms_bwd("rms1_bwd", dh1, xf, norm1_g, dx2)

    slots = _scatter_grads([g_win, g_wout, g_wg, g_wu, g_wd])
    big = [("w_in", w_in, m_w_in, v_w_in, True), ("w_out", w_out, m_w_out, v_w_out, False),
           ("w_ffn_gate", w_ffn_gate, m_w_ffn_gate, v_w_ffn_gate, True), ("w_ffn_up", w_ffn_up, m_w_ffn_up, v_w_ffn_up, True),
           ("w_ffn_down", w_ffn_down, m_w_ffn_down, v_w_ffn_down, False)]
    big_out = {}
    for (nm, wv, mv, vv, transposed), sl in zip(big, slots):
        g = _sum_slots("sum_" + nm, sl)
        if transposed:
            g = g.T
        dlt, nmv, nvv = _adamw("adamw_" + nm, wv[0], g, mv[0], vv[0])
        big_out[nm] = (g[None], dlt[None], nmv[None], nvv[None])

    nh = dm.nh
    small_names = ["norm1_g", "q_norm_g", "k_norm_g", "attn_sinks", "gate_ln_g", "gate_ln_b", "w_spatial", "b_spatial",
                   "out_norm_attn_g", "out_norm_gate_g", "norm2_g"]
    small_g = [dg1, dgq, dgk, dsink[:, :nh], dlng[None], dlnb[None], dws[None], dbst.T[None], dgoa, dgog, dg2]
    small_w = [norm1_g, q_norm_g, k_norm_g, attn_sinks, gate_ln_g, gate_ln_b, w_spatial, b_spatial, out_norm_attn_g, out_norm_gate_g, norm2_g]
    small_m = [m_norm1_g, m_q_norm_g, m_k_norm_g, m_attn_sinks, m_gate_ln_g, m_gate_ln_b, m_w_spatial, m_b_spatial, m_out_norm_attn_g, m_out_norm_gate_g, m_norm2_g]
    small_v = [v_norm1_g, v_q_norm_g, v_k_norm_g, v_attn_sinks, v_gate_ln_g, v_gate_ln_b, v_w_spatial, v_b_spatial, v_out_norm_attn_g, v_out_norm_gate_g, v_norm2_g]
    zero = jnp.zeros((1,), F32)
    part, meta = _pack(small_g + [loss_part.reshape(1)])
    pw, _ = _pack(small_w + [zero])
    pm, _ = _pack(small_m + [zero])
    pv, _ = _pack(small_v + [zero])
    sg, sd, sm, sv = _allreduce_small_adamw(part, pw, pm, pv)
    ug, ud, um, uv = _unpack(sg, meta), _unpack(sd, meta), _unpack(sm, meta), _unpack(sv, meta)
    loss = ug[-1].reshape(())
    small_out = {nm: (ug[k], ud[k], um[k], uv[k]) for k, nm in enumerate(small_names)}

    order = ["norm1_g", "w_in", "q_norm_g", "k_norm_g", "attn_sinks", "gate_ln_g", "gate_ln_b", "w_spatial", "b_spatial",
             "out_norm_attn_g", "out_norm_gate_g", "w_out", "norm2_g", "w_ffn_gate", "w_ffn_up", "w_ffn_down"]
    allo = {**big_out, **small_out}
    outs = [loss, dx.reshape(nseq, seq, d)]
    for k in range(4):
        outs += [allo[nm][k] for nm in order]
    return tuple(outs)
```

```python
import functools
import math

import jax
import jax.numpy as jnp
from jax import lax
from jax.experimental import pallas as pl
from jax.experimental.pallas import tpu as pltpu
from jax.experimental.pallas import tpu_sc as plsc

F32 = jnp.float32
MXU = jnp.bfloat16
WIRE = jnp.bfloat16
EPS = 1e-6
BLOCK = 128
GROUP_DIM = 128
N_KV_HEADS = 2
NEG = -1e30
N_DEV = 8
LANES = 128
MIB = 1024 * 1024

ADAM_LR = 0.001
ADAM_B1 = 0.9
ADAM_B2 = 0.999
ADAM_EPS = 1e-08
ADAM_WD = 0.01
ADAM_STEP = 10

MESH = pl.DeviceIdType.MESH
ANY = pl.BlockSpec(memory_space=pl.ANY)


def _pick(n, cands):
    for c in cands:
        if n % c == 0:
            return c
    return n


def _cparams(sem, vmem_mb):
    return pltpu.CompilerParams(dimension_semantics=sem, vmem_limit_bytes=vmem_mb * MIB)


def _matmul(name, a, b, mode, tm, tn, tk, out_defs, epilogue=None, extras=(), after=(), vmem_mb=48):
    if mode == "nn":
        (m, kk), n = a.shape, b.shape[1]
        a_spec = pl.BlockSpec((tm, tk), lambda i, j, k: (i, k))
        b_spec = pl.BlockSpec((tk, tn), lambda i, j, k: (k, j))
        dn = (((1,), (0,)), ((), ()))
    elif mode == "nt":
        (m, kk), n = a.shape, b.shape[0]
        a_spec = pl.BlockSpec((tm, tk), lambda i, j, k: (i, k))
        b_spec = pl.BlockSpec((tn, tk), lambda i, j, k: (j, k))
        dn = (((1,), (1,)), ((), ()))
    else:
        (kk, m), n = a.shape, b.shape[1]
        a_spec = pl.BlockSpec((tk, tm), lambda i, j, k: (k, i))
        b_spec = pl.BlockSpec((tk, tn), lambda i, j, k: (k, j))
        dn = (((0,), (0,)), ((), ()))
    assert m % tm == 0 and n % tn == 0 and kk % tk == 0, (name, m, n, kk, tm, tn, tk)
    nk = kk // tk
    ne, na, no = len(extras), len(after), len(out_defs)

    def body(a_ref, b_ref, *rest):
        extra_refs, out_refs = rest[:ne], rest[ne + na:ne + na + no]

        def finish(acc):
            tiles = (acc,) if epilogue is None else epilogue(acc, *[r[...] for r in extra_refs])
            for o_ref, t in zip(out_refs, tiles):
                o_ref[...] = t.astype(o_ref.dtype)

        p = lax.dot_general(a_ref[...], b_ref[...], dn, preferred_element_type=F32)
        if nk == 1:
            finish(p)
        else:
            acc_ref = rest[-1]
            k = pl.program_id(2)

            @pl.when(k == 0)
            def _():
                acc_ref[...] = p

            @pl.when(k > 0)
            def _():
                acc_ref[...] += p

            @pl.when(k == nk - 1)
            def _():
                finish(acc_ref[...])

    return pl.pallas_call(
        body,
        name=name,
        grid=(m // tm, n // tn, nk),
        in_specs=[a_spec, b_spec] + [pl.BlockSpec((tm, tn), lambda i, j, k: (i, j)) for _ in extras] + [ANY] * na,
        out_specs=[pl.BlockSpec(blk, functools.partial(lambda i, j, k, f: f(i, j), f=imap)) for (_, _, blk, imap) in out_defs],
        out_shape=[jax.ShapeDtypeStruct(shp, dt) for (shp, dt, _, _) in out_defs],
        scratch_shapes=[pltpu.VMEM((tm, tn), F32)] if nk > 1 else [],
        compiler_params=_cparams(("parallel", "parallel", "arbitrary"), vmem_mb),
    )(a, b, *extras, *after)


def _tile_out(m, n, tm, tn, dtype):
    return ((m, n), dtype, (tm, tn), lambda i, j: (i, j))


def _rms_fwd(name, x, g):
    t, d = x.shape
    tm = _pick(t, (512, 256, 128))

    def body(x_ref, g_ref, h_ref):
        xv = x_ref[...]
        r = lax.rsqrt(jnp.mean(xv * xv, axis=-1, keepdims=True) + EPS)
        h_ref[...] = (xv * r * g_ref[...]).astype(h_ref.dtype)

    return pl.pallas_call(
        body, name=name, grid=(t // tm,),
        in_specs=[pl.BlockSpec((tm, d), lambda i: (i, 0)), pl.BlockSpec((1, d), lambda i: (0, 0))],
        out_specs=pl.BlockSpec((tm, d), lambda i: (i, 0)),
        out_shape=jax.ShapeDtypeStruct((t, d), MXU),
        compiler_params=_cparams(("parallel",), 32),
    )(x, g)


def _rms_bwd(name, dh, x, g, res):
    t, d = x.shape
    tm = _pick(t, (256, 128))

    def body(dh_ref, x_ref, g_ref, res_ref, dx_ref, dxb_ref, dg_ref):
        @pl.when(pl.program_id(0) == 0)
        def _():
            dg_ref[...] = jnp.zeros_like(dg_ref)

        xv, dhv = x_ref[...], dh_ref[...]
        r = lax.rsqrt(jnp.mean(xv * xv, axis=-1, keepdims=True) + EPS)
        xh = xv * r
        dg_ref[...] += jnp.sum(dhv * xh, axis=0, keepdims=True)
        dxh = dhv * g_ref[...]
        dx = res_ref[...] + r * (dxh - xh * jnp.mean(dxh * xh, axis=-1, keepdims=True))
        dx_ref[...] = dx
        dxb_ref[...] = dx.astype(dxb_ref.dtype)

    row = pl.BlockSpec((tm, d), lambda i: (i, 0))
    vec = pl.BlockSpec((1, d), lambda i: (0, 0))
    return pl.pallas_call(
        body, name=name, grid=(t // tm,),
        in_specs=[row, row, vec, row],
        out_specs=[row, row, vec],
        out_shape=[jax.ShapeDtypeStruct((t, d), F32), jax.ShapeDtypeStruct((t, d), MXU), jax.ShapeDtypeStruct((1, d), F32)],
        compiler_params=_cparams(("arbitrary",), 40),
    )(dh, x, g, res)


_INV_SQRT2 = 0.7071067811865476
_INV_SQRT_2PI = 0.3989422804014327


def _dot_nt(a, b):
    return lax.dot_general(a, b, (((1,), (1,)), ((), ())), preferred_element_type=F32)


def _dot_tn(a, b):
    return lax.dot_general(a, b, (((0,), (0,)), ((), ())), preferred_element_type=F32)


def _dot(a, b):
    return jnp.dot(a, b, preferred_element_type=F32)


def _row_rms(v):
    return lax.rsqrt(jnp.mean(v * v, axis=-1, keepdims=True) + EPS)


class _Dims:
    def __init__(self, d_model, in_width, head_dim):
        self.d = d_model
        self.aw = d_model // 2
        self.gw = d_model - self.aw
        self.kvw = (in_width - self.aw - 2 * self.gw) // 2
        self.hd = head_dim
        self.nh = self.aw // head_dim
        self.nkv = self.kvw // head_dim
        self.grp = self.nh // self.nkv
        self.ng = self.gw // GROUP_DIM
        self.inw = in_width
        self.zoff = self.aw + 2 * self.kvw
        assert self.nkv == N_KV_HEADS and self.zoff + 2 * self.gw == in_width


def _band_masks(first):
    qi = lax.broadcasted_iota(jnp.int32, (BLOCK, 2 * BLOCK), 0)
    kj = lax.broadcasted_iota(jnp.int32, (BLOCK, 2 * BLOCK), 1)
    dist = qi + BLOCK - kj
    valid = (dist >= 0) & (dist < BLOCK) & ((kj >= BLOCK) | jnp.logical_not(first))
    return valid, dist.astype(F32)


def _attn_head_fwd(dm, h, q_raw, gq, knb, vbb, sink, valid, dist):
    rq = _row_rms(q_raw)
    qh = q_raw * rq
    qnb = (qh * gq).astype(MXU)
    slope = math.pow(2.0, -8.0 * (h + 1) / dm.nh)
    s = _dot_nt(qnb, knb) * (dm.hd ** -0.5)
    logits = jnp.where(valid, s - slope * dist, NEG)
    m = jnp.maximum(jnp.max(logits, axis=-1, keepdims=True), sink)
    e = jnp.exp(logits - m)
    es = jnp.exp(sink - m)
    inv = 1.0 / (jnp.sum(e, axis=-1, keepdims=True) + es)
    p = e * inv
    o = _dot(p.astype(MXU), vbb)
    return o, p, es * inv, rq, qh, qnb


def _gate_group_fwd(g, u_g, v_g, lng, lnb, ws_ref, bst_ref, tril):
    mu = jnp.mean(v_g, axis=-1, keepdims=True)
    xc = v_g - mu
    rstd = lax.rsqrt(jnp.mean(xc * xc, axis=-1, keepdims=True) + EPS)
    xh = xc * rstd
    vnb = (xh * lng[g:g + 1, :] + lnb[g:g + 1, :]).astype(MXU)
    wt = jnp.where(tril, ws_ref[g], 0.0).astype(MXU)
    mixed = _dot(wt, vnb) + bst_ref[:, g:g + 1]
    return u_g * mixed, mixed, xh, rstd, vnb, wt


def _gelu_cdf(z):
    return 0.5 * (1.0 + lax.erf(z * _INV_SQRT2))


def _mixer_specs(dm, nb, clamp):
    kvblk = dm.aw // (2 * dm.kvw)

    def cur(s, i):
        return (s * nb + clamp(i), 0)

    def prev(s, i):
        return (s * nb + jnp.maximum(clamp(i) - 1, 0), kvblk)

    full = lambda shape: pl.BlockSpec(shape, lambda s, i: tuple(0 for _ in shape))
    return cur, prev, full


def _mixer_fwd(proj, prm, dm, nseq, nb):
    gq, gk, sinks, lng, lnb, ws, bst, goa, gog = prm
    t = proj.shape[0]
    cur, prev, full = _mixer_specs(dm, nb, lambda i: i)
    assert dm.aw % (2 * dm.kvw) == 0

    def body(proj_ref, pkv_ref, gq_ref, gk_ref, sink_ref, lng_ref, lnb_ref, ws_ref, bst_ref, goa_ref, gog_ref, y_ref, att_scr, gate_scr):
        i = pl.program_id(1)
        valid, dist = _band_masks(i == 0)
        gqv, gkv = gq_ref[...], gk_ref[...]
        for kh in range(dm.nkv):
            ko = dm.aw + kh * dm.hd
            vo = dm.aw + dm.kvw + kh * dm.hd
            kb = jnp.concatenate([pkv_ref[:, kh * dm.hd:(kh + 1) * dm.hd], proj_ref[:, ko:ko + dm.hd]], axis=0)
            vb = jnp.concatenate([pkv_ref[:, dm.kvw + kh * dm.hd:dm.kvw + (kh + 1) * dm.hd], proj_ref[:, vo:vo + dm.hd]], axis=0)
            knb = (kb * _row_rms(kb) * gkv).astype(MXU)
            vbb = vb.astype(MXU)
            for g in range(dm.grp):
                h = kh * dm.grp + g
                o = _attn_head_fwd(dm, h, proj_ref[:, h * dm.hd:(h + 1) * dm.hd], gqv, knb, vbb, sink_ref[h], valid, dist)[0]
                att_scr[:, h * dm.hd:(h + 1) * dm.hd] = o
        att = att_scr[...]
        y_ref[:, :dm.aw] = (att * _row_rms(att) * goa_ref[...]).astype(y_ref.dtype)

        tril = lax.broadcasted_iota(jnp.int32, (BLOCK, BLOCK), 0) >= lax.broadcasted_iota(jnp.int32, (BLOCK, BLOCK), 1)
        lngv, lnbv = lng_ref[...], lnb_ref[...]
        for g in range(dm.ng):
            zu = proj_ref[:, dm.zoff + g * GROUP_DIM:dm.zoff + (g + 1) * GROUP_DIM]
            zv = proj_ref[:, dm.zoff + dm.gw + g * GROUP_DIM:dm.zoff + dm.gw + (g + 1) * GROUP_DIM]
            u_g, v_g = zu * _gelu_cdf(zu), zv * _gelu_cdf(zv)
            gate_scr[:, g * GROUP_DIM:(g + 1) * GROUP_DIM] = _gate_group_fwd(g, u_g, v_g, lngv, lnbv, ws_ref, bst_ref, tril)[0]
        gt = gate_scr[...]
        y_ref[:, dm.aw:] = (gt * _row_rms(gt) * gog_ref[...]).astype(y_ref.dtype)

    return pl.pallas_call(
        body, name="mixer_fwd", grid=(nseq, nb),
        in_specs=[pl.BlockSpec((BLOCK, dm.inw), cur), pl.BlockSpec((BLOCK, 2 * dm.kvw), prev),
                  full(gq.shape), full(gk.shape), pl.BlockSpec(memory_space=pltpu.SMEM),
                  full(lng.shape), full(lnb.shape), full(ws.shape), full(bst.shape), full(goa.shape), full(gog.shape)],
        out_specs=pl.BlockSpec((BLOCK, dm.d), cur),
        out_shape=jax.ShapeDtypeStruct((t, dm.d), MXU),
        scratch_shapes=[pltpu.VMEM((BLOCK, dm.aw), F32), pltpu.VMEM((BLOCK, dm.gw), F32)],
        compiler_params=_cparams(("parallel", "arbitrary"), 40),
    )(proj, proj, gq, gk, sinks, lng, lnb, ws, bst, goa, gog)


def _mixer_bwd(proj, dy, prm, dm, nseq, nb):
    gq, gk, sinks, lng, lnb, ws, bst, goa, gog = prm
    t = proj.shape[0]
    clamp = lambda i: jnp.minimum(i, nb - 1)
    cur, prev, full = _mixer_specs(dm, nb, clamp)
    kvw2 = 2 * dm.kvw

    def prev_kv_out(s, i):
        return (s * nb + jnp.maximum(i - 1, 0), 0)

    def body(proj_ref, pkv_ref, dy_ref, gq_ref, gk_ref, sink_ref, lng_ref, lnb_ref, ws_ref, bst_ref, goa_ref, gog_ref,
             dproj_ref, dkv_ref, dgq_ref, dgk_ref, dsink_ref, dlng_ref, dlnb_ref, dws_ref, dbst_ref, dgoa_ref, dgog_ref,
             att_scr, datt_scr, gate_scr, carry_scr, prevpart_scr, curpart_scr, kprev_scr):
        s_id, i = pl.program_id(0), pl.program_id(1)

        @pl.when((s_id == 0) & (i == 0))
        def _():
            for r in (dgq_ref, dgk_ref, dsink_ref, dlng_ref, dlnb_ref, dws_ref, dbst_ref, dgoa_ref, dgog_ref):
                r[...] = jnp.zeros_like(r)

        gqv, gkv = gq_ref[...], gk_ref[...]

        @pl.when(i < nb)
        def _():
            valid, dist = _band_masks(i == 0)
            lane = lax.broadcasted_iota(jnp.int32, (1, LANES), 1)
            heads = []
            kvs = []
            for kh in range(dm.nkv):
                ko = dm.aw + kh * dm.hd
                vo = dm.aw + dm.kvw + kh * dm.hd
                kb = jnp.concatenate([pkv_ref[:, kh * dm.hd:(kh + 1) * dm.hd], proj_ref[:, ko:ko + dm.hd]], axis=0)
                vb = jnp.concatenate([pkv_ref[:, dm.kvw + kh * dm.hd:dm.kvw + (kh + 1) * dm.hd], proj_ref[:, vo:vo + dm.hd]], axis=0)
                knb = (kb * _row_rms(kb) * gkv).astype(MXU)
                vbb = vb.astype(MXU)
                kvs.append((knb, vbb))
                for g in range(dm.grp):
                    h = kh * dm.grp + g
                    o = _attn_head_fwd(dm, h, proj_ref[:, h * dm.hd:(h + 1) * dm.hd], gqv, knb, vbb, sink_ref[h], valid, dist)[0]
                    att_scr[:, h * dm.hd:(h + 1) * dm.hd] = o
            att = att_scr[...]
            dya = dy_ref[:, :dm.aw]
            ra = _row_rms(att)
            ah = att * ra
            dgoa_ref[...] += jnp.sum(dya * ah, axis=0, keepdims=True)
            dah = dya * goa_ref[...]
            datt_scr[...] = ra * (dah - ah * jnp.mean(dah * ah, axis=-1, keepdims=True))
            dsink_acc = jnp.zeros((1, LANES), F32)
            dgq_acc = jnp.zeros((1, dm.hd), F32)
            for kh in range(dm.nkv):
                knb, vbb = kvs[kh]
                dkn = jnp.zeros((2 * BLOCK, dm.hd), F32)
                dvb = jnp.zeros((2 * BLOCK, dm.hd), F32)
                for g in range(dm.grp):
                    h = kh * dm.grp + g
                    _, p, ps, rq, qh, qnb = _attn_head_fwd(dm, h, proj_ref[:, h * dm.hd:(h + 1) * dm.hd], gqv, knb, vbb, sink_ref[h], valid, dist)
                    do_b = datt_scr[:, h * dm.hd:(h + 1) * dm.hd].astype(MXU)
                    dp = _dot_nt(do_b, vbb)
                    delta = jnp.sum(p * dp, axis=-1, keepdims=True)
                    dsb = (p * (dp - delta) * (dm.hd ** -0.5)).astype(MXU)
                    dsink_acc += jnp.where(lane == h, -jnp.sum(ps * delta, axis=0, keepdims=True), 0.0)
                    dqn = _dot(dsb, knb)
                    dkn += _dot_tn(dsb, qnb)
                    dvb += _dot_tn(p.astype(MXU), do_b)
                    dgq_acc += jnp.sum(dqn * qh, axis=0, keepdims=True)
                    dqh = dqn * gqv
                    dq = rq * (dqh - qh * jnp.mean(dqh * qh, axis=-1, keepdims=True))
                    dproj_ref[:, h * dm.hd:(h + 1) * dm.hd] = dq.astype(dproj_ref.dtype)
                prevpart_scr[:, kh * dm.hd:(kh + 1) * dm.hd] = dkn[:BLOCK]
                prevpart_scr[:, dm.kvw + kh * dm.hd:dm.kvw + (kh + 1) * dm.hd] = dvb[:BLOCK]
                curpart_scr[:, kh * dm.hd:(kh + 1) * dm.hd] = dkn[BLOCK:]
                curpart_scr[:, dm.kvw + kh * dm.hd:dm.kvw + (kh + 1) * dm.hd] = dvb[BLOCK:]
            dsink_ref[...] += dsink_acc
            dgq_ref[...] += dgq_acc
            dproj_ref[:, dm.aw:dm.zoff] = jnp.zeros((BLOCK, kvw2), dproj_ref.dtype)
            tril = lax.broadcasted_iota(jnp.int32, (BLOCK, BLOCK), 0) >= lax.broadcasted_iota(jnp.int32, (BLOCK, BLOCK), 1)
            lngv, lnbv = lng_ref[...], lnb_ref[...]
            saved = []
            for g in range(dm.ng):
                zu = proj_ref[:, dm.zoff + g * GROUP_DIM:dm.zoff + (g + 1) * GROUP_DIM]
                zv = proj_ref[:, dm.zoff + dm.gw + g * GROUP_DIM:dm.zoff + dm.gw + (g + 1) * GROUP_DIM]
                u_g, v_g = zu * _gelu_cdf(zu), zv * _gelu_cdf(zv)
                yg, mixed, xh, rstd, vnb, wt = _gate_group_fwd(g, u_g, v_g, lngv, lnbv, ws_ref, bst_ref, tril)
                gate_scr[:, g * GROUP_DIM:(g + 1) * GROUP_DIM] = yg
                saved.append((u_g, mixed, xh, rstd, vnb, wt))
            gt = gate_scr[...]
            dyg = dy_ref[:, dm.aw:]
            rg = _row_rms(gt)
            gh = gt * rg
            dgog_ref[...] += jnp.sum(dyg * gh, axis=0, keepdims=True)
            dgh = dyg * gog_ref[...]
            gate_scr[...] = rg * (dgh - gh * jnp.mean(dgh * gh, axis=-1, keepdims=True))
            for g in range(dm.ng):
                u_g, mixed, xh, rstd, vnb, wt = saved[g]
                dyg_g = gate_scr[:, g * GROUP_DIM:(g + 1) * GROUP_DIM]
                du = dyg_g * mixed
                dmix = dyg_g * u_g
                dmixb = dmix.astype(MXU)
                dbst_ref[:, g:g + 1] += jnp.sum(dmix, axis=-1, keepdims=True)
                dws_ref[g] += jnp.where(tril, _dot_nt(dmixb, vnb), 0.0)
                dvn = _dot_tn(wt, dmixb)
                dlng_ref[g:g + 1, :] += jnp.sum(dvn * xh, axis=0, keepdims=True)
                dlnb_ref[g:g + 1, :] += jnp.sum(dvn, axis=0, keepdims=True)
                dxh = dvn * lngv[g:g + 1, :]
                dv = rstd * (dxh - jnp.mean(dxh, axis=-1, keepdims=True) - xh * jnp.mean(dxh * xh, axis=-1, keepdims=True))
                uo = dm.zoff + g * GROUP_DIM
                vo = dm.zoff + dm.gw + g * GROUP_DIM
                zu, zv = proj_ref[:, uo:uo + GROUP_DIM], proj_ref[:, vo:vo + GROUP_DIM]
                dgu = _gelu_cdf(zu) + zu * (jnp.exp(-0.5 * zu * zu) * _INV_SQRT_2PI)
                dgv = _gelu_cdf(zv) + zv * (jnp.exp(-0.5 * zv * zv) * _INV_SQRT_2PI)
                dproj_ref[:, uo:uo + GROUP_DIM] = (du * dgu).astype(dproj_ref.dtype)
                dproj_ref[:, vo:vo + GROUP_DIM] = (dv * dgv).astype(dproj_ref.dtype)

        @pl.when(i == nb)
        def _():
            prevpart_scr[...] = jnp.zeros_like(prevpart_scr)

        @pl.when(i >= 1)
        def _():
            tot = carry_scr[...] + prevpart_scr[...]
            dgk_acc = jnp.zeros((1, dm.hd), F32)
            for kh in range(dm.nkv):
                kraw = kprev_scr[:, kh * dm.hd:(kh + 1) * dm.hd]
                rk = _row_rms(kraw)
                khat = kraw * rk
                dkn = tot[:, kh * dm.hd:(kh + 1) * dm.hd]
                dgk_acc += jnp.sum(dkn * khat, axis=0, keepdims=True)
                dkh = dkn * gkv
                dk = rk * (dkh - khat * jnp.mean(dkh * khat, axis=-1, keepdims=True))
                dkv_ref[:, kh * dm.hd:(kh + 1) * dm.hd] = dk.astype(dkv_ref.dtype)
            dkv_ref[:, dm.kvw:] = tot[:, dm.kvw:].astype(dkv_ref.dtype)
            dgk_ref[...] += dgk_acc

        @pl.when(i < nb)
        def _():
            carry_scr[...] = curpart_scr[...]
            kprev_scr[...] = proj_ref[:, dm.aw:dm.aw + dm.kvw]

    acc = lambda a: jax.ShapeDtypeStruct(a.shape, F32)
    outs = pl.pallas_call(
        body, name="mixer_bwd", grid=(nseq, nb + 1),
        in_specs=[pl.BlockSpec((BLOCK, dm.inw), cur), pl.BlockSpec((BLOCK, kvw2), prev), pl.BlockSpec((BLOCK, dm.d), cur),
                  full(gq.shape), full(gk.shape), pl.BlockSpec(memory_space=pltpu.SMEM),
                  full(lng.shape), full(lnb.shape), full(ws.shape), full(bst.shape), full(goa.shape), full(gog.shape)],
        out_specs=[pl.BlockSpec((BLOCK, dm.inw), cur), pl.BlockSpec((BLOCK, kvw2), prev_kv_out),
                   full(gq.shape), full(gk.shape), full((1, LANES)), full(lng.shape), full(lnb.shape), full(ws.shape),
                   full(bst.shape), full(goa.shape), full(gog.shape)],
        out_shape=[jax.ShapeDtypeStruct((t, dm.inw), MXU), jax.ShapeDtypeStruct((t, kvw2), MXU),
                   acc(gq), acc(gk), jax.ShapeDtypeStruct((1, LANES), F32), acc(lng), acc(lnb), acc(ws), acc(bst), acc(goa), acc(gog)],
        scratch_shapes=[pltpu.VMEM((BLOCK, dm.aw), F32), pltpu.VMEM((BLOCK, dm.aw), F32), pltpu.VMEM((BLOCK, dm.gw), F32),
                        pltpu.VMEM((BLOCK, kvw2), F32), pltpu.VMEM((BLOCK, kvw2), F32), pltpu.VMEM((BLOCK, kvw2), F32),
                        pltpu.VMEM((BLOCK, dm.kvw), F32)],
        compiler_params=_cparams(("arbitrary", "arbitrary"), 48),
    )(proj, proj, dy, gq, gk, sinks, lng, lnb, ws, bst, goa, gog)
    return outs


def _patch_kv(dproj, dkv, dm):
    t = dproj.shape[0]
    tm = _pick(t, (1024, 512, 256, 128))
    kvw2 = 2 * dm.kvw
    kvblk = dm.aw // kvw2

    def body(dproj_hbm, dkv_ref, out_ref):
        del dproj_hbm
        out_ref[...] = dkv_ref[...]

    return pl.pallas_call(
        body, name="patch_kv", grid=(t // tm,),
        in_specs=[ANY, pl.BlockSpec((tm, kvw2), lambda i: (i, 0))],
        out_specs=pl.BlockSpec((tm, kvw2), lambda i: (i, kvblk)),
        out_shape=jax.ShapeDtypeStruct(dproj.shape, dproj.dtype),
        input_output_aliases={0: 0},
        compiler_params=_cparams(("parallel",), 32),
    )(dproj, dkv)


def _place():
    x, y, c = lax.axis_index("x"), lax.axis_index("y"), lax.axis_index("c")
    return x, y, c


def _handshake(peers):
    barrier = pltpu.get_barrier_semaphore()
    for p in peers:
        pl.semaphore_signal(barrier, inc=1, device_id=p, device_id_type=MESH)
    pl.semaphore_wait(barrier, len(peers))


def _sequencer_mesh():
    return plsc.ScalarSubcoreMesh(axis_name="sequencer", num_cores=1)


def _allgather_weights(name, collective_id, shards, after=None):
    nw = len(shards)

    def body(*refs):
        src, out = refs[:nw], refs[-3 - nw:-3]
        send_sems, recv_sems, local_sems = refs[-3:]
        x, y, c = _place()
        me, sibling = (x, y, c), (x, y, 1 - c)
        chips = [(1 - x, y), (x, 1 - y), (1 - x, 1 - y)]
        _handshake([sibling] + [(*chip, c) for chip in chips])

        def rows(w, place):
            n = src[w].shape[0]
            px, py, pc = place
            return out[w].at[pl.ds(pl.multiple_of((4 * px + 2 * py + pc) * n, 16), n), :]

        def copy(w, k, block, to, from_src=False):
            return pltpu.make_async_remote_copy(
                src_ref=src[w] if from_src else rows(w, block), dst_ref=rows(w, block),
                send_sem=send_sems.at[w, k], recv_sem=recv_sems.at[w, k], device_id=to, device_id_type=MESH)

        mine = [pltpu.make_async_copy(src[w], rows(w, me), local_sems.at[w]) for w in range(nw)]
        for cp in mine:
            cp.start()
        first = []
        for w in range(nw):
            first.append(copy(w, 0, me, sibling, from_src=True))
            first += [copy(w, 1 + j, me, (*chip, c), from_src=True) for j, chip in enumerate(chips)]
        for cp in first:
            cp.start()
        passed = []
        for w in range(nw):
            for j, chip in enumerate(chips):
                copy(w, 1 + j, (*chip, c), me).wait_recv()
                fwd = copy(w, 4 + j, (*chip, c), sibling)
                fwd.start()
                passed.append(fwd)
        for w in range(nw):
            copy(w, 0, sibling, me).wait_recv()
            for j, chip in enumerate(chips):
                copy(w, 4 + j, (*chip, 1 - c), me).wait_recv()
        for cp in first + passed:
            cp.wait_send()
        for cp in mine:
            cp.wait()

    return pl.kernel(
        body, name=name,
        out_type=[jax.ShapeDtypeStruct((N_DEV * s.shape[0], s.shape[1]), s.dtype) for s in shards],
        mesh=_sequencer_mesh(),
        scratch_types=[pltpu.SemaphoreType.DMA((nw, 7)), pltpu.SemaphoreType.DMA((nw, 7)), pltpu.SemaphoreType.DMA((nw,))],
        compiler_params=pltpu.CompilerParams(collective_id=collective_id),
    )(*shards, *([] if after is None else [after]))


_FLIPS = [(0, 0, 1), (1, 0, 0), (0, 1, 0), (1, 1, 0), (1, 0, 1), (0, 1, 1), (1, 1, 1)]


def _scatter_grad(name, collective_id, grad):
    n = grad.shape[0] // N_DEV

    def body(src, out, send_sems, recv_sems, local_sem):
        x, y, c = _place()
        me_idx = 4 * x + 2 * y + c
        peers = [(x ^ fx, y ^ fy, c ^ fc) for (fx, fy, fc) in _FLIPS]
        _handshake(peers)

        def block(idx):
            return src.at[pl.ds(pl.multiple_of(idx * n, 16), n), :]

        copies = [pltpu.make_async_remote_copy(
            src_ref=block(4 * px + 2 * py + pc), dst_ref=out.at[me_idx], send_sem=send_sems.at[k], recv_sem=recv_sems.at[k],
            device_id=(px, py, pc), device_id_type=MESH) for k, (px, py, pc) in enumerate(peers)]
        mine = pltpu.make_async_copy(block(me_idx), out.at[me_idx], local_sem)
        mine.start()
        for cp in copies:
            cp.start()
        for cp in copies:
            cp.wait_recv()
        for cp in copies:
            cp.wait_send()
        mine.wait()

    return pl.kernel(
        body, name=name,
        out_type=jax.ShapeDtypeStruct((N_DEV, n, grad.shape[1]), grad.dtype),
        mesh=_sequencer_mesh(),
        scratch_types=[pltpu.SemaphoreType.DMA((7,)), pltpu.SemaphoreType.DMA((7,)), pltpu.SemaphoreType.DMA],
        compiler_params=pltpu.CompilerParams(collective_id=collective_id),
    )(grad)


def _sum_slots(name, slots, after):
    _, n, kk = slots.shape
    tr = _pick(n, (256, 208, 176, 128, 96, 64, 32, 16))

    def body(s_ref, after_ref, o_ref):
        del after_ref
        acc = s_ref[0].astype(F32)
        for p in range(1, N_DEV):
            acc = acc + s_ref[p].astype(F32)
        o_ref[...] = acc

    return pl.pallas_call(
        body, name=name, grid=(n // tr,),
        in_specs=[pl.BlockSpec((N_DEV, tr, kk), lambda i: (0, i, 0)), ANY],
        out_specs=pl.BlockSpec((tr, kk), lambda i: (i, 0)),
        out_shape=jax.ShapeDtypeStruct((n, kk), F32),
        compiler_params=_cparams(("parallel",), 40),
    )(slots, after)


def _adamw_math(w, g, m, v):
    m = ADAM_B1 * m + (1.0 - ADAM_B1) * g
    v = ADAM_B2 * v + (1.0 - ADAM_B2) * (g * g)
    m_hat = m / (1.0 - ADAM_B1 ** ADAM_STEP)
    v_hat = v / (1.0 - ADAM_B2 ** ADAM_STEP)
    delta = -ADAM_LR * (m_hat / (jnp.sqrt(v_hat) + ADAM_EPS) + ADAM_WD * w)
    return delta, m, v


def _adamw(name, w, g, m, v):
    r, cdim = w.shape
    tr = _pick(r, (256, 128, 64, 32, 16, 8))

    def body(w_ref, g_ref, m_ref, v_ref, d_ref, nm_ref, nv_ref):
        d_ref[...], nm_ref[...], nv_ref[...] = _adamw_math(w_ref[...], g_ref[...], m_ref[...], v_ref[...])

    spec = pl.BlockSpec((tr, cdim), lambda i: (i, 0))
    return pl.pallas_call(
        body, name=name, grid=(r // tr,),
        in_specs=[spec] * 4, out_specs=[spec] * 3,
        out_shape=[jax.ShapeDtypeStruct(w.shape, F32)] * 3,
        compiler_params=_cparams(("parallel",), 40),
    )(w, g, m, v)


def _allreduce_small_adamw(part, w, m, v):
    r = part.shape[0]

    def body(part_ref, w_ref, m_ref, v_ref, g_ref, d_ref, nm_ref, nv_ref, slots, send_sems, recv_sems):
        x, y, c = _place()
        me_idx = 4 * x + 2 * y + c
        copies = []
        for k, (fx, fy, fc) in enumerate(_FLIPS):
            px, py, pc = x ^ fx, y ^ fy, c ^ fc
            copies.append(pltpu.make_async_remote_copy(
                src_ref=part_ref, dst_ref=slots.at[me_idx], send_sem=send_sems.at[k], recv_sem=recv_sems.at[k],
                device_id=(px, py, pc), device_id_type=MESH))
        for cp in copies:
            cp.start()
        slots[me_idx] = part_ref[...]
        for cp in copies:
            cp.wait_recv()
        for cp in copies:
            cp.wait_send()
        g = slots[0]
        for p in range(1, N_DEV):
            g = g + slots[p]
        g_ref[...] = g
        d_ref[...], nm_ref[...], nv_ref[...] = _adamw_math(w_ref[...], g, m_ref[...], v_ref[...])

    vm = pl.BlockSpec(memory_space=pltpu.VMEM)
    return pl.pallas_call(
        body, name="allreduce_small_adamw",
        in_specs=[vm] * 4, out_specs=[vm] * 4,
        out_shape=[jax.ShapeDtypeStruct((r, LANES), F32)] * 4,
        scratch_shapes=[pltpu.VMEM((N_DEV, r, LANES), F32), pltpu.SemaphoreType.DMA((7,)), pltpu.SemaphoreType.DMA((7,))],
        compiler_params=pltpu.CompilerParams(vmem_limit_bytes=48 * MIB),
    )(part, w, m, v)


def _pack(arrs):
    parts, meta, off = [], [], 0
    for a in arrs:
        flat = a.reshape(-1).astype(F32)
        rows = -(-flat.shape[0] // LANES)
        rows8 = -(-rows // 8) * 8
        flat = jnp.pad(flat, (0, rows8 * LANES - flat.shape[0]))
        parts.append(flat.reshape(rows8, LANES))
        meta.append((off, a.shape, a.size))
        off += rows8
    return jnp.concatenate(parts, axis=0), meta


def _unpack(packed, meta):
    outs = []
    for off, shape, size in meta:
        rows = -(-size // LANES)
        outs.append(packed[off:off + rows].reshape(-1)[:size].reshape(shape))
    return outs


def _silu_parts(a):
    sg = 1.0 / (1.0 + jnp.exp(-a))
    return a * sg, sg * (1.0 + a * (1.0 - sg))


def kernel(x, norm1_g, w_in, q_norm_g, k_norm_g, attn_sinks, gate_ln_g, gate_ln_b, w_spatial, b_spatial, out_norm_attn_g, out_norm_gate_g, w_out, norm2_g, w_ffn_gate, w_ffn_up, w_ffn_down, loss_target, m_norm1_g, m_w_in, m_q_norm_g, m_k_norm_g, m_attn_sinks, m_gate_ln_g, m_gate_ln_b, m_w_spatial, m_b_spatial, m_out_norm_attn_g, m_out_norm_gate_g, m_w_out, m_norm2_g, m_w_ffn_gate, m_w_ffn_up, m_w_ffn_down, v_norm1_g, v_w_in, v_q_norm_g, v_k_norm_g, v_attn_sinks, v_gate_ln_g, v_gate_ln_b, v_w_spatial, v_b_spatial, v_out_norm_attn_g, v_out_norm_gate_g, v_w_out, v_norm2_g, v_w_ffn_gate, v_w_ffn_up, v_w_ffn_down):
    nseq, seq, d = x.shape
    t = nseq * seq
    nb = seq // BLOCK
    inw = w_in.shape[2] * N_DEV
    dff = w_ffn_gate.shape[2] * N_DEV
    dm = _Dims(d, inw, q_norm_g.shape[-1])
    xf = x.reshape(t, d)
    tgt = loss_target.reshape(t, d)

    shards = [w_in[0].T.astype(WIRE), w_out[0].astype(WIRE), w_ffn_gate[0].T.astype(WIRE),
              w_ffn_up[0].T.astype(WIRE), w_ffn_down[0].astype(WIRE)]
    (win_t,) = _allgather_weights("gather_w_in", 1, shards[:1])
    wout, wg_t, wu_t = _allgather_weights("gather_w_out_gate_up", 2, shards[1:4], after=win_t)

    ws = w_spatial[0]
    prm = (q_norm_g, k_norm_g, attn_sinks[0], gate_ln_g[0], gate_ln_b[0], ws, b_spatial[0].T, out_norm_attn_g, out_norm_gate_g)

    tm = _pick(t, (512, 256, 128))
    h1 = _rms_fwd("rms1_fwd", xf, norm1_g)
    tn_in = _pick(inw, (1664, 1024, 512, 256, 128))
    (proj,) = _matmul("mm_in", h1, win_t, "nt", tm, tn_in, d, [_tile_out(t, inw, tm, tn_in, F32)])
    y = _mixer_fwd(proj, prm, dm, nseq, nb)
    (wd,) = _allgather_weights("gather_w_down", 3, shards[4:], after=y)
    tn_d = _pick(d, (1024, 512, 256))
    (x2,) = _matmul("mm_out", y, wout, "nn", tm, tn_d, d, [_tile_out(t, d, tm, tn_d, F32)],
                    epilogue=lambda acc, xr: (xr + acc,), extras=[xf])
    h2 = _rms_fwd("rms2_fwd", x2, norm2_g)
    tn_f = _pick(dff, (1408, 1024, 768, 512, 256, 128))
    (a,) = _matmul("mm_gate", h2, wg_t, "nt", tm, tn_f, d, [_tile_out(t, dff, tm, tn_f, F32)])
    b, s = _matmul("mm_up", h2, wu_t, "nt", tm, tn_f, d, [_tile_out(t, dff, tm, tn_f, F32), _tile_out(t, dff, tm, tn_f, MXU)],
                   epilogue=lambda acc, av: (acc, _silu_parts(av)[0] * acc), extras=[a])
    tk_f = _pick(dff, (1408, 1024, 768, 512, 256, 128))

    def loss_epilogue(acc, x2v, tv):
        diff = (x2v + acc) - tv
        dx3 = diff * (1.0 / d)
        return dx3, dx3, jnp.full((8, LANES), jnp.sum(diff * diff), F32)

    dx3, dx3b, lossp = _matmul(
        "mm_down", s, wd, "nn", tm, tn_d, tk_f,
        [_tile_out(t, d, tm, tn_d, F32), _tile_out(t, d, tm, tn_d, MXU),
         ((t // tm * 8, d // tn_d * LANES), F32, (8, LANES), lambda i, j: (i, j))],
        epilogue=loss_epilogue, extras=[x2, tgt])
    loss_part = (0.5 / d) * jnp.sum(lossp[::8, ::LANES])

    def dswiglu(acc, av, bv):
        silu, dsilu = _silu_parts(av)
        return acc * bv * dsilu, acc * silu

    da, db = _matmul("mm_d_down", dx3b, wd, "nt", tm, tn_f, d, [_tile_out(t, dff, tm, tn_f, MXU), _tile_out(t, dff, tm, tn_f, MXU)],
                     epilogue=dswiglu, extras=[a, b])
    tkt = _pick(t, (1024, 512, 256, 128))
    tmw_f = _pick(dff, (1408, 512, 256, 128))
    (g_wd,) = _matmul("mm_gw_down", s, dx3b, "tn", tmw_f, tn_d, tkt, [_tile_out(dff, d, tmw_f, tn_d, WIRE)])
    sl_wd = _scatter_grad("scatter_w_ffn_down", 4, g_wd)
    (dh2a,) = _matmul("mm_dh2_gate", da, wg_t, "nn", tm, tn_d, tk_f, [_tile_out(t, d, tm, tn_d, F32)], after=[g_wd])
    (g_wg,) = _matmul("mm_gw_gate", da, h2, "tn", tmw_f, tn_d, tkt, [_tile_out(dff, d, tmw_f, tn_d, WIRE)], after=[dh2a])
    sl_wg = _scatter_grad("scatter_w_ffn_gate", 5, g_wg)
    (dh2,) = _matmul("mm_dh2_up", db, wu_t, "nn", tm, tn_d, tk_f, [_tile_out(t, d, tm, tn_d, F32)],
                     epilogue=lambda acc, pv: (pv + acc,), extras=[dh2a], after=[g_wg])
    (g_wu,) = _matmul("mm_gw_up", db, h2, "tn", tmw_f, tn_d, tkt, [_tile_out(dff, d, tmw_f, tn_d, WIRE)], after=[dh2])
    sl_wu = _scatter_grad("scatter_w_ffn_up", 6, g_wu)
    dx2, dx2b, dg2 = _rms_bwd("rms2_bwd", dh2, x2, norm2_g, dx3)

    (dy,) = _matmul("mm_d_out", dx2b, wout, "nt", tm, tn_d, d, [_tile_out(t, d, tm, tn_d, F32)], after=[g_wu])
    tmw_d = _pick(d, (1024, 512, 256, 128))
    (g_wout,) = _matmul("mm_gw_out", y, dx2b, "tn", tmw_d, tn_d, tkt, [_tile_out(d, d, tmw_d, tn_d, WIRE)], after=[dy])
    sl_wout = _scatter_grad("scatter_w_out", 7, g_wout)
    (dproj0, dkv, dgq, dgk, dsink, dlng, dlnb, dws, dbst, dgoa, dgog) = _mixer_bwd(proj, dy, prm, dm, nseq, nb)
    dproj = _patch_kv(dproj0, dkv, dm)
    tmw_in = _pick(inw, (1664, 1024, 512, 256, 128))
    (g_win,) = _matmul("mm_gw_in", dproj, h1, "tn", tmw_in, tn_d, tkt, [_tile_out(inw, d, tmw_in, tn_d, WIRE)])
    sl_win = _scatter_grad("scatter_w_in", 8, g_win)
    tk_in = _pick(inw, (1664, 1024, 512, 256, 128))
    (dh1,) = _matmul("mm_d_in", dproj, win_t, "nn", tm, tn_d, tk_in, [_tile_out(t, d, tm, tn_d, F32)], after=[g_win])
    dx, _, dg1 = _rms_bwd("rms1_bwd", dh1, xf, norm1_g, dx2)

    big = [("w_ffn_down", w_ffn_down, m_w_ffn_down, v_w_ffn_down, False, sl_wd),
           ("w_ffn_gate", w_ffn_gate, m_w_ffn_gate, v_w_ffn_gate, True, sl_wg),
           ("w_ffn_up", w_ffn_up, m_w_ffn_up, v_w_ffn_up, True, sl_wu),
           ("w_out", w_out, m_w_out, v_w_out, False, sl_wout),
           ("w_in", w_in, m_w_in, v_w_in, True, sl_win)]
    big_out = {}
    last = dx
    for nm, wv, mv, vv, transposed, sl in big:
        g = _sum_slots("sum_" + nm, sl, last)
        if transposed:
            g = g.T
        dlt, nmv, nvv = _adamw("adamw_" + nm, wv[0], g, mv[0], vv[0])
        big_out[nm] = (g[None], dlt[None], nmv[None], nvv[None])
        last = dlt

    nh = dm.nh
    small_names = ["norm1_g", "q_norm_g", "k_norm_g", "attn_sinks", "gate_ln_g", "gate_ln_b", "w_spatial", "b_spatial",
                   "out_norm_attn_g", "out_norm_gate_g", "norm2_g"]
    small_g = [dg1, dgq, dgk, dsink[:, :nh], dlng[None], dlnb[None], dws[None], dbst.T[None], dgoa, dgog, dg2]
    small_w = [norm1_g, q_norm_g, k_norm_g, attn_sinks, gate_ln_g, gate_ln_b, w_spatial, b_spatial, out_norm_attn_g, out_norm_gate_g, norm2_g]
    small_m = [m_norm1_g, m_q_norm_g, m_k_norm_g, m_attn_sinks, m_gate_ln_g, m_gate_ln_b, m_w_spatial, m_b_spatial, m_out_norm_attn_g, m_out_norm_gate_g, m_norm2_g]
    small_v = [v_norm1_g, v_q_norm_g, v_k_norm_g, v_attn_sinks, v_gate_ln_g, v_gate_ln_b, v_w_spatial, v_b_spatial, v_out_norm_attn_g, v_out_norm_gate_g, v_norm2_g]
    zero = jnp.zeros((1,), F32)
    part, meta = _pack(small_g + [loss_part.reshape(1)])
    pw, _ = _pack(small_w + [zero])
    pm, _ = _pack(small_m + [zero])
    pv, _ = _pack(small_v + [zero])
    sg, sd, sm, sv = _allreduce_small_adamw(part, pw, pm, pv)
    ug, ud, um, uv = _unpack(sg, meta), _unpack(sd, meta), _unpack(sm, meta), _unpack(sv, meta)
    loss = ug[-1].reshape(())
    small_out = {nm: (ug[k], ud[k], um[k], uv[k]) for k, nm in enumerate(small_names)}

    order = ["norm1_g", "w_in", "q_norm_g", "k_norm_g", "attn_sinks", "gate_ln_g", "gate_ln_b", "w_spatial", "b_spatial",
             "out_norm_attn_g", "out_norm_gate_g", "w_out", "norm2_g", "w_ffn_gate", "w_ffn_up", "w_ffn_down"]
    allo = {**big_out, **small_out}
    outs = [loss, dx.reshape(nseq, seq, d)]
    for k in range(4):
        outs += [allo[nm][k] for nm in order]
    return tuple(outs)
```

```python
import functools
import math

import jax
import jax.numpy as jnp
from jax import lax
from jax.experimental import pallas as pl
from jax.experimental.pallas import tpu as pltpu
from jax.experimental.pallas import tpu_sc as plsc

F32 = jnp.float32
MXU = jnp.bfloat16
WIRE = jnp.bfloat16
EPS = 1e-6
BLOCK = 128
GROUP_DIM = 128
N_KV_HEADS = 2
NEG = -1e30
N_DEV = 8
LANES = 128
MIB = 1024 * 1024

ADAM_LR = 0.001
ADAM_B1 = 0.9
ADAM_B2 = 0.999
ADAM_EPS = 1e-08
ADAM_WD = 0.01
ADAM_STEP = 10

MESH = pl.DeviceIdType.MESH
ANY = pl.BlockSpec(memory_space=pl.ANY)


def _pick(n, cands):
    for c in cands:
        if n % c == 0:
            return c
    return n


def _cparams(sem, vmem_mb):
    return pltpu.CompilerParams(dimension_semantics=sem, vmem_limit_bytes=vmem_mb * MIB)


VMEM_TILE_BUDGET = 44 * MIB
HBM_BYTES_PER_US = 3.0e6
STEP_US = 0.4
MIN_TILE_N = 512


def _tile_candidates(n):
    return [c for c in range(min(n, 2048), 0, -LANES) if n % c == 0 and c % LANES == 0] or [n]


def _matmul_tiles(m, n, kk, esz, n_extras, out_sizes, partial):
    best = None
    wide = [c for c in _tile_candidates(n) if c >= MIN_TILE_N] or _tile_candidates(n)
    for tm in _tile_candidates(m):
        for tn in wide:
            vmem = 2 * (tm + tn) * kk * esz + tm * tn * (4 + 2 * 4 * n_extras + 2 * sum(out_sizes)) + partial * 2 * 8 * LANES * 4
            if vmem > VMEM_TILE_BUDGET:
                continue
            cost = (m // tm) * n * kk * esz / HBM_BYTES_PER_US + (m // tm) * (n // tn) * STEP_US
            if best is None or cost < best[0]:
                best = (cost, tm, tn, vmem)
    assert best is not None, (m, n, kk)
    return best[1:]


def _matmul(name, a, b, mode, out_dtypes, epilogue=None, extras=(), after=(), partial=False):
    if mode == "nn":
        (m, kk), n = a.shape, b.shape[1]
        dn = (((1,), (0,)), ((), ()))
    elif mode == "nt":
        (m, kk), n = a.shape, b.shape[0]
        dn = (((1,), (1,)), ((), ()))
    else:
        (kk, m), n = a.shape, b.shape[1]
        dn = (((0,), (0,)), ((), ()))
    tm, tn, vmem = _matmul_tiles(m, n, kk, a.dtype.itemsize, len(extras), [jnp.dtype(dt).itemsize for dt in out_dtypes], partial)
    a_spec = pl.BlockSpec((kk, tm), lambda i, j: (0, i)) if mode == "tn" else pl.BlockSpec((tm, kk), lambda i, j: (i, 0))
    b_spec = pl.BlockSpec((tn, kk), lambda i, j: (j, 0)) if mode == "nt" else pl.BlockSpec((kk, tn), lambda i, j: (0, j))
    tile = pl.BlockSpec((tm, tn), lambda i, j: (i, j))
    ne, na, no = len(extras), len(after), len(out_dtypes)

    def body(a_ref, b_ref, *rest):
        extra_refs, out_refs = rest[:ne], rest[ne + na:]
        acc = lax.dot_general(a_ref[...], b_ref[...], dn, preferred_element_type=F32)
        tiles = (acc,) if epilogue is None else epilogue(acc, *[r[...] for r in extra_refs])
        for o_ref, t in zip(out_refs[:no], tiles[:no]):
            o_ref[...] = t.astype(o_ref.dtype)
        if partial:
            out_refs[no][...] = jnp.full((8, LANES), tiles[no], F32)

    out_specs = [tile] * no
    out_shape = [jax.ShapeDtypeStruct((m, n), dt) for dt in out_dtypes]
    if partial:
        out_specs.append(pl.BlockSpec((8, LANES), lambda i, j: (i, j)))
        out_shape.append(jax.ShapeDtypeStruct((m // tm * 8, n // tn * LANES), F32))
    return pl.pallas_call(
        body, name=name, grid=(m // tm, n // tn),
        in_specs=[a_spec, b_spec] + [tile] * ne + [ANY] * na,
        out_specs=out_specs, out_shape=out_shape,
        compiler_params=_cparams(("parallel", "arbitrary"), min(vmem // MIB + 8, 60)),
    )(a, b, *extras, *after)


def _rms_fwd(name, x, g):
    t, d = x.shape
    tm = _pick(t, (512, 256, 128))

    def body(x_ref, g_ref, h_ref):
        xv = x_ref[...]
        r = lax.rsqrt(jnp.mean(xv * xv, axis=-1, keepdims=True) + EPS)
        h_ref[...] = (xv * r * g_ref[...]).astype(h_ref.dtype)

    return pl.pallas_call(
        body, name=name, grid=(t // tm,),
        in_specs=[pl.BlockSpec((tm, d), lambda i: (i, 0)), pl.BlockSpec((1, d), lambda i: (0, 0))],
        out_specs=pl.BlockSpec((tm, d), lambda i: (i, 0)),
        out_shape=jax.ShapeDtypeStruct((t, d), MXU),
        compiler_params=_cparams(("parallel",), 32),
    )(x, g)


def _rms_bwd(name, dh, x, g, res):
    t, d = x.shape
    tm = _pick(t, (256, 128))

    def body(dh_ref, x_ref, g_ref, res_ref, dx_ref, dxb_ref, dg_ref):
        @pl.when(pl.program_id(0) == 0)
        def _():
            dg_ref[...] = jnp.zeros_like(dg_ref)

        xv, dhv = x_ref[...], dh_ref[...]
        r = lax.rsqrt(jnp.mean(xv * xv, axis=-1, keepdims=True) + EPS)
        xh = xv * r
        dg_ref[...] += jnp.sum(dhv * xh, axis=0, keepdims=True)
        dxh = dhv * g_ref[...]
        dx = res_ref[...] + r * (dxh - xh * jnp.mean(dxh * xh, axis=-1, keepdims=True))
        dx_ref[...] = dx
        dxb_ref[...] = dx.astype(dxb_ref.dtype)

    row = pl.BlockSpec((tm, d), lambda i: (i, 0))
    vec = pl.BlockSpec((1, d), lambda i: (0, 0))
    return pl.pallas_call(
        body, name=name, grid=(t // tm,),
        in_specs=[row, row, vec, row],
        out_specs=[row, row, vec],
        out_shape=[jax.ShapeDtypeStruct((t, d), F32), jax.ShapeDtypeStruct((t, d), MXU), jax.ShapeDtypeStruct((1, d), F32)],
        compiler_params=_cparams(("arbitrary",), 40),
    )(dh, x, g, res)


_INV_SQRT2 = 0.7071067811865476
_INV_SQRT_2PI = 0.3989422804014327


def _dot_nt(a, b):
    return lax.dot_general(a, b, (((1,), (1,)), ((), ())), preferred_element_type=F32)


def _dot_tn(a, b):
    return lax.dot_general(a, b, (((0,), (0,)), ((), ())), preferred_element_type=F32)


def _dot(a, b):
    return jnp.dot(a, b, preferred_element_type=F32)


def _row_rms(v):
    return lax.rsqrt(jnp.mean(v * v, axis=-1, keepdims=True) + EPS)


class _Dims:
    def __init__(self, d_model, in_width, head_dim):
        self.d = d_model
        self.aw = d_model // 2
        self.gw = d_model - self.aw
        self.kvw = (in_width - self.aw - 2 * self.gw) // 2
        self.hd = head_dim
        self.nh = self.aw // head_dim
        self.nkv = self.kvw // head_dim
        self.grp = self.nh // self.nkv
        self.ng = self.gw // GROUP_DIM
        self.inw = in_width
        self.zoff = self.aw + 2 * self.kvw
        assert self.nkv == N_KV_HEADS and self.zoff + 2 * self.gw == in_width


def _band_masks(first):
    qi = lax.broadcasted_iota(jnp.int32, (BLOCK, 2 * BLOCK), 0)
    kj = lax.broadcasted_iota(jnp.int32, (BLOCK, 2 * BLOCK), 1)
    dist = qi + BLOCK - kj
    valid = (dist >= 0) & (dist < BLOCK) & ((kj >= BLOCK) | jnp.logical_not(first))
    return valid, dist.astype(F32)


def _attn_head_fwd(dm, h, q_raw, gq, knb, vbb, sink, valid, dist):
    rq = _row_rms(q_raw)
    qh = q_raw * rq
    qnb = (qh * gq).astype(MXU)
    slope = math.pow(2.0, -8.0 * (h + 1) / dm.nh)
    s = _dot_nt(qnb, knb) * (dm.hd ** -0.5)
    logits = jnp.where(valid, s - slope * dist, NEG)
    m = jnp.maximum(jnp.max(logits, axis=-1, keepdims=True), sink)
    e = jnp.exp(logits - m)
    es = jnp.exp(sink - m)
    inv = 1.0 / (jnp.sum(e, axis=-1, keepdims=True) + es)
    p = e * inv
    o = _dot(p.astype(MXU), vbb)
    return o, p, es * inv, rq, qh, qnb


def _gate_group_fwd(g, u_g, v_g, lng, lnb, ws_ref, bst_ref, tril):
    mu = jnp.mean(v_g, axis=-1, keepdims=True)
    xc = v_g - mu
    rstd = lax.rsqrt(jnp.mean(xc * xc, axis=-1, keepdims=True) + EPS)
    xh = xc * rstd
    vnb = (xh * lng[g:g + 1, :] + lnb[g:g + 1, :]).astype(MXU)
    wt = jnp.where(tril, ws_ref[g], 0.0).astype(MXU)
    mixed = _dot(wt, vnb) + bst_ref[:, g:g + 1]
    return u_g * mixed, mixed, xh, rstd, vnb, wt


def _gelu_cdf(z):
    return 0.5 * (1.0 + lax.erf(z * _INV_SQRT2))


def _mixer_specs(dm, nb, clamp):
    kvblk = dm.aw // (2 * dm.kvw)

    def cur(s, i):
        return (s * nb + clamp(i), 0)

    def prev(s, i):
        return (s * nb + jnp.maximum(clamp(i) - 1, 0), kvblk)

    full = lambda shape: pl.BlockSpec(shape, lambda s, i: tuple(0 for _ in shape))
    return cur, prev, full


def _mixer_fwd(proj, prm, dm, nseq, nb):
    gq, gk, sinks, lng, lnb, ws, bst, goa, gog = prm
    t = proj.shape[0]
    cur, prev, full = _mixer_specs(dm, nb, lambda i: i)
    assert dm.aw % (2 * dm.kvw) == 0

    def body(proj_ref, pkv_ref, gq_ref, gk_ref, sink_ref, lng_ref, lnb_ref, ws_ref, bst_ref, goa_ref, gog_ref, y_ref, att_scr, gate_scr):
        i = pl.program_id(1)
        valid, dist = _band_masks(i == 0)
        gqv, gkv = gq_ref[...], gk_ref[...]
        for kh in range(dm.nkv):
            ko = dm.aw + kh * dm.hd
            vo = dm.aw + dm.kvw + kh * dm.hd
            kb = jnp.concatenate([pkv_ref[:, kh * dm.hd:(kh + 1) * dm.hd], proj_ref[:, ko:ko + dm.hd]], axis=0)
            vb = jnp.concatenate([pkv_ref[:, dm.kvw + kh * dm.hd:dm.kvw + (kh + 1) * dm.hd], proj_ref[:, vo:vo + dm.hd]], axis=0)
            knb = (kb * _row_rms(kb) * gkv).astype(MXU)
            vbb = vb.astype(MXU)
            for g in range(dm.grp):
                h = kh * dm.grp + g
                o = _attn_head_fwd(dm, h, proj_ref[:, h * dm.hd:(h + 1) * dm.hd], gqv, knb, vbb, sink_ref[h], valid, dist)[0]
                att_scr[:, h * dm.hd:(h + 1) * dm.hd] = o
        att = att_scr[...]
        y_ref[:, :dm.aw] = (att * _row_rms(att) * goa_ref[...]).astype(y_ref.dtype)

        tril = lax.broadcasted_iota(jnp.int32, (BLOCK, BLOCK), 0) >= lax.broadcasted_iota(jnp.int32, (BLOCK, BLOCK), 1)
        lngv, lnbv = lng_ref[...], lnb_ref[...]
        for g in range(dm.ng):
            zu = proj_ref[:, dm.zoff + g * GROUP_DIM:dm.zoff + (g + 1) * GROUP_DIM]
            zv = proj_ref[:, dm.zoff + dm.gw + g * GROUP_DIM:dm.zoff + dm.gw + (g + 1) * GROUP_DIM]
            u_g, v_g = zu * _gelu_cdf(zu), zv * _gelu_cdf(zv)
            gate_scr[:, g * GROUP_DIM:(g + 1) * GROUP_DIM] = _gate_group_fwd(g, u_g, v_g, lngv, lnbv, ws_ref, bst_ref, tril)[0]
        gt = gate_scr[...]
        y_ref[:, dm.aw:] = (gt * _row_rms(gt) * gog_ref[...]).astype(y_ref.dtype)

    return pl.pallas_call(
        body, name="mixer_fwd", grid=(nseq, nb),
        in_specs=[pl.BlockSpec((BLOCK, dm.inw), cur), pl.BlockSpec((BLOCK, 2 * dm.kvw), prev),
                  full(gq.shape), full(gk.shape), pl.BlockSpec(memory_space=pltpu.SMEM),
                  full(lng.shape), full(lnb.shape), full(ws.shape), full(bst.shape), full(goa.shape), full(gog.shape)],
        out_specs=pl.BlockSpec((BLOCK, dm.d), cur),
        out_shape=jax.ShapeDtypeStruct((t, dm.d), MXU),
        scratch_shapes=[pltpu.VMEM((BLOCK, dm.aw), F32), pltpu.VMEM((BLOCK, dm.gw), F32)],
        compiler_params=_cparams(("parallel", "arbitrary"), 40),
    )(proj, proj, gq, gk, sinks, lng, lnb, ws, bst, goa, gog)


def _mixer_bwd(proj, dy, prm, dm, nseq, nb):
    gq, gk, sinks, lng, lnb, ws, bst, goa, gog = prm
    t = proj.shape[0]
    clamp = lambda i: jnp.minimum(i, nb - 1)
    cur, prev, full = _mixer_specs(dm, nb, clamp)
    kvw2 = 2 * dm.kvw

    def prev_kv_out(s, i):
        return (s * nb + jnp.maximum(i - 1, 0), 0)

    def body(proj_ref, pkv_ref, dy_ref, gq_ref, gk_ref, sink_ref, lng_ref, lnb_ref, ws_ref, bst_ref, goa_ref, gog_ref,
             dproj_ref, dkv_ref, dgq_ref, dgk_ref, dsink_ref, dlng_ref, dlnb_ref, dws_ref, dbst_ref, dgoa_ref, dgog_ref,
             att_scr, datt_scr, gate_scr, carry_scr, prevpart_scr, curpart_scr, kprev_scr):
        s_id, i = pl.program_id(0), pl.program_id(1)

        @pl.when((s_id == 0) & (i == 0))
        def _():
            for r in (dgq_ref, dgk_ref, dsink_ref, dlng_ref, dlnb_ref, dws_ref, dbst_ref, dgoa_ref, dgog_ref):
                r[...] = jnp.zeros_like(r)

        gqv, gkv = gq_ref[...], gk_ref[...]

        @pl.when(i < nb)
        def _():
            valid, dist = _band_masks(i == 0)
            lane = lax.broadcasted_iota(jnp.int32, (1, LANES), 1)
            heads = []
            kvs = []
            for kh in range(dm.nkv):
                ko = dm.aw + kh * dm.hd
                vo = dm.aw + dm.kvw + kh * dm.hd
                kb = jnp.concatenate([pkv_ref[:, kh * dm.hd:(kh + 1) * dm.hd], proj_ref[:, ko:ko + dm.hd]], axis=0)
                vb = jnp.concatenate([pkv_ref[:, dm.kvw + kh * dm.hd:dm.kvw + (kh + 1) * dm.hd], proj_ref[:, vo:vo + dm.hd]], axis=0)
                knb = (kb * _row_rms(kb) * gkv).astype(MXU)
                vbb = vb.astype(MXU)
                kvs.append((knb, vbb))
                for g in range(dm.grp):
                    h = kh * dm.grp + g
                    o = _attn_head_fwd(dm, h, proj_ref[:, h * dm.hd:(h + 1) * dm.hd], gqv, knb, vbb, sink_ref[h], valid, dist)[0]
                    att_scr[:, h * dm.hd:(h + 1) * dm.hd] = o
            att = att_scr[...]
            dya = dy_ref[:, :dm.aw]
            ra = _row_rms(att)
            ah = att * ra
            dgoa_ref[...] += jnp.sum(dya * ah, axis=0, keepdims=True)
            dah = dya * goa_ref[...]
            datt_scr[...] = ra * (dah - ah * jnp.mean(dah * ah, axis=-1, keepdims=True))
            dsink_acc = jnp.zeros((1, LANES), F32)
            dgq_acc = jnp.zeros((1, dm.hd), F32)
            for kh in range(dm.nkv):
                knb, vbb = kvs[kh]
                dkn = jnp.zeros((2 * BLOCK, dm.hd), F32)
                dvb = jnp.zeros((2 * BLOCK, dm.hd), F32)
                for g in range(dm.grp):
                    h = kh * dm.grp + g
                    _, p, ps, rq, qh, qnb = _attn_head_fwd(dm, h, proj_ref[:, h * dm.hd:(h + 1) * dm.hd], gqv, knb, vbb, sink_ref[h], valid, dist)
                    do_b = datt_scr[:, h * dm.hd:(h + 1) * dm.hd].astype(MXU)
                    dp = _dot_nt(do_b, vbb)
                    delta = jnp.sum(p * dp, axis=-1, keepdims=True)
                    dsb = (p * (dp - delta) * (dm.hd ** -0.5)).astype(MXU)
                    dsink_acc += jnp.where(lane == h, -jnp.sum(ps * delta, axis=0, keepdims=True), 0.0)
                    dqn = _dot(dsb, knb)
                    dkn += _dot_tn(dsb, qnb)
                    dvb += _dot_tn(p.astype(MXU), do_b)
                    dgq_acc += jnp.sum(dqn * qh, axis=0, keepdims=True)
                    dqh = dqn * gqv
                    dq = rq * (dqh - qh * jnp.mean(dqh * qh, axis=-1, keepdims=True))
                    dproj_ref[:, h * dm.hd:(h + 1) * dm.hd] = dq.astype(dproj_ref.dtype)
                prevpart_scr[:, kh * dm.hd:(kh + 1) * dm.hd] = dkn[:BLOCK]
                prevpart_scr[:, dm.kvw + kh * dm.hd:dm.kvw + (kh + 1) * dm.hd] = dvb[:BLOCK]
                curpart_scr[:, kh * dm.hd:(kh + 1) * dm.hd] = dkn[BLOCK:]
                curpart_scr[:, dm.kvw + kh * dm.hd:dm.kvw + (kh + 1) * dm.hd] = dvb[BLOCK:]
            dsink_ref[...] += dsink_acc
            dgq_ref[...] += dgq_acc
            dproj_ref[:, dm.aw:dm.zoff] = jnp.zeros((BLOCK, kvw2), dproj_ref.dtype)
            tril = lax.broadcasted_iota(jnp.int32, (BLOCK, BLOCK), 0) >= lax.broadcasted_iota(jnp.int32, (BLOCK, BLOCK), 1)
            lngv, lnbv = lng_ref[...], lnb_ref[...]
            saved = []
            for g in range(dm.ng):
                zu = proj_ref[:, dm.zoff + g * GROUP_DIM:dm.zoff + (g + 1) * GROUP_DIM]
                zv = proj_ref[:, dm.zoff + dm.gw + g * GROUP_DIM:dm.zoff + dm.gw + (g + 1) * GROUP_DIM]
                u_g, v_g = zu * _gelu_cdf(zu), zv * _gelu_cdf(zv)
                yg, mixed, xh, rstd, vnb, wt = _gate_group_fwd(g, u_g, v_g, lngv, lnbv, ws_ref, bst_ref, tril)
                gate_scr[:, g * GROUP_DIM:(g + 1) * GROUP_DIM] = yg
                saved.append((u_g, mixed, xh, rstd, vnb, wt))
            gt = gate_scr[...]
            dyg = dy_ref[:, dm.aw:]
            rg = _row_rms(gt)
            gh = gt * rg
            dgog_ref[...] += jnp.sum(dyg * gh, axis=0, keepdims=True)
            dgh = dyg * gog_ref[...]
            gate_scr[...] = rg * (dgh - gh * jnp.mean(dgh * gh, axis=-1, keepdims=True))
            for g in range(dm.ng):
                u_g, mixed, xh, rstd, vnb, wt = saved[g]
                dyg_g = gate_scr[:, g * GROUP_DIM:(g + 1) * GROUP_DIM]
                du = dyg_g * mixed
                dmix = dyg_g * u_g
                dmixb = dmix.astype(MXU)
                dbst_ref[:, g:g + 1] += jnp.sum(dmix, axis=-1, keepdims=True)
                dws_ref[g] += jnp.where(tril, _dot_nt(dmixb, vnb), 0.0)
                dvn = _dot_tn(wt, dmixb)
                dlng_ref[g:g + 1, :] += jnp.sum(dvn * xh, axis=0, keepdims=True)
                dlnb_ref[g:g + 1, :] += jnp.sum(dvn, axis=0, keepdims=True)
                dxh = dvn * lngv[g:g + 1, :]
                dv = rstd * (dxh - jnp.mean(dxh, axis=-1, keepdims=True) - xh * jnp.mean(dxh * xh, axis=-1, keepdims=True))
                uo = dm.zoff + g * GROUP_DIM
                vo = dm.zoff + dm.gw + g * GROUP_DIM
                zu, zv = proj_ref[:, uo:uo + GROUP_DIM], proj_ref[:, vo:vo + GROUP_DIM]
                dgu = _gelu_cdf(zu) + zu * (jnp.exp(-0.5 * zu * zu) * _INV_SQRT_2PI)
                dgv = _gelu_cdf(zv) + zv * (jnp.exp(-0.5 * zv * zv) * _INV_SQRT_2PI)
                dproj_ref[:, uo:uo + GROUP_DIM] = (du * dgu).astype(dproj_ref.dtype)
                dproj_ref[:, vo:vo + GROUP_DIM] = (dv * dgv).astype(dproj_ref.dtype)

        @pl.when(i == nb)
        def _():
            prevpart_scr[...] = jnp.zeros_like(prevpart_scr)

        @pl.when(i >= 1)
        def _():
            tot = carry_scr[...] + prevpart_scr[...]
            dgk_acc = jnp.zeros((1, dm.hd), F32)
            for kh in range(dm.nkv):
                kraw = kprev_scr[:, kh * dm.hd:(kh + 1) * dm.hd]
                rk = _row_rms(kraw)
                khat = kraw * rk
                dkn = tot[:, kh * dm.hd:(kh + 1) * dm.hd]
                dgk_acc += jnp.sum(dkn * khat, axis=0, keepdims=True)
                dkh = dkn * gkv
                dk = rk * (dkh - khat * jnp.mean(dkh * khat, axis=-1, keepdims=True))
                dkv_ref[:, kh * dm.hd:(kh + 1) * dm.hd] = dk.astype(dkv_ref.dtype)
            dkv_ref[:, dm.kvw:] = tot[:, dm.kvw:].astype(dkv_ref.dtype)
            dgk_ref[...] += dgk_acc

        @pl.when(i < nb)
        def _():
            carry_scr[...] = curpart_scr[...]
            kprev_scr[...] = proj_ref[:, dm.aw:dm.aw + dm.kvw]

    acc = lambda a: jax.ShapeDtypeStruct(a.shape, F32)
    outs = pl.pallas_call(
        body, name="mixer_bwd", grid=(nseq, nb + 1),
        in_specs=[pl.BlockSpec((BLOCK, dm.inw), cur), pl.BlockSpec((BLOCK, kvw2), prev), pl.BlockSpec((BLOCK, dm.d), cur),
                  full(gq.shape), full(gk.shape), pl.BlockSpec(memory_space=pltpu.SMEM),
                  full(lng.shape), full(lnb.shape), full(ws.shape), full(bst.shape), full(goa.shape), full(gog.shape)],
        out_specs=[pl.BlockSpec((BLOCK, dm.inw), cur), pl.BlockSpec((BLOCK, kvw2), prev_kv_out),
                   full(gq.shape), full(gk.shape), full((1, LANES)), full(lng.shape), full(lnb.shape), full(ws.shape),
                   full(bst.shape), full(goa.shape), full(gog.shape)],
        out_shape=[jax.ShapeDtypeStruct((t, dm.inw), MXU), jax.ShapeDtypeStruct((t, kvw2), MXU),
                   acc(gq), acc(gk), jax.ShapeDtypeStruct((1, LANES), F32), acc(lng), acc(lnb), acc(ws), acc(bst), acc(goa), acc(gog)],
        scratch_shapes=[pltpu.VMEM((BLOCK, dm.aw), F32), pltpu.VMEM((BLOCK, dm.aw), F32), pltpu.VMEM((BLOCK, dm.gw), F32),
                        pltpu.VMEM((BLOCK, kvw2), F32), pltpu.VMEM((BLOCK, kvw2), F32), pltpu.VMEM((BLOCK, kvw2), F32),
                        pltpu.VMEM((BLOCK, dm.kvw), F32)],
        compiler_params=_cparams(("arbitrary", "arbitrary"), 48),
    )(proj, proj, dy, gq, gk, sinks, lng, lnb, ws, bst, goa, gog)
    return outs


def _patch_kv(dproj, dkv, dm):
    t = dproj.shape[0]
    tm = _pick(t, (1024, 512, 256, 128))
    kvw2 = 2 * dm.kvw
    kvblk = dm.aw // kvw2

    def body(dproj_hbm, dkv_ref, out_ref):
        del dproj_hbm
        out_ref[...] = dkv_ref[...]

    return pl.pallas_call(
        body, name="patch_kv", grid=(t // tm,),
        in_specs=[ANY, pl.BlockSpec((tm, kvw2), lambda i: (i, 0))],
        out_specs=pl.BlockSpec((tm, kvw2), lambda i: (i, kvblk)),
        out_shape=jax.ShapeDtypeStruct(dproj.shape, dproj.dtype),
        input_output_aliases={0: 0},
        compiler_params=_cparams(("parallel",), 32),
    )(dproj, dkv)


def _place():
    x, y, c = lax.axis_index("x"), lax.axis_index("y"), lax.axis_index("c")
    return x, y, c


def _handshake(peers):
    barrier = pltpu.get_barrier_semaphore()
    for p in peers:
        pl.semaphore_signal(barrier, inc=1, device_id=p, device_id_type=MESH)
    pl.semaphore_wait(barrier, len(peers))


def _sequencer_mesh():
    return plsc.ScalarSubcoreMesh(axis_name="sequencer", num_cores=1)


def _allgather_weights(name, collective_id, shards, after=None):
    nw = len(shards)

    def body(*refs):
        src, out = refs[:nw], refs[-3 - nw:-3]
        send_sems, recv_sems, local_sems = refs[-3:]
        x, y, c = _place()
        me, sibling = (x, y, c), (x, y, 1 - c)
        chips = [(1 - x, y), (x, 1 - y), (1 - x, 1 - y)]
        _handshake([sibling] + [(*chip, c) for chip in chips])

        def rows(w, place):
            n = src[w].shape[0]
            px, py, pc = place
            return out[w].at[pl.ds(pl.multiple_of((4 * px + 2 * py + pc) * n, 16), n), :]

        def copy(w, k, block, to, from_src=False):
            return pltpu.make_async_remote_copy(
                src_ref=src[w] if from_src else rows(w, block), dst_ref=rows(w, block),
                send_sem=send_sems.at[w, k], recv_sem=recv_sems.at[w, k], device_id=to, device_id_type=MESH)

        mine = [pltpu.make_async_copy(src[w], rows(w, me), local_sems.at[w]) for w in range(nw)]
        for cp in mine:
            cp.start()
        first = []
        for w in range(nw):
            first.append(copy(w, 0, me, sibling, from_src=True))
            first += [copy(w, 1 + j, me, (*chip, c), from_src=True) for j, chip in enumerate(chips)]
        for cp in first:
            cp.start()
        passed = []
        for w in range(nw):
            for j, chip in enumerate(chips):
                copy(w, 1 + j, (*chip, c), me).wait_recv()
                fwd = copy(w, 4 + j, (*chip, c), sibling)
                fwd.start()
                passed.append(fwd)
        for w in range(nw):
            copy(w, 0, sibling, me).wait_recv()
            for j, chip in enumerate(chips):
                copy(w, 4 + j, (*chip, 1 - c), me).wait_recv()
        for cp in first + passed:
            cp.wait_send()
        for cp in mine:
            cp.wait()

    return pl.kernel(
        body, name=name,
        out_type=[jax.ShapeDtypeStruct((N_DEV * s.shape[0], s.shape[1]), s.dtype) for s in shards],
        mesh=_sequencer_mesh(),
        scratch_types=[pltpu.SemaphoreType.DMA((nw, 7)), pltpu.SemaphoreType.DMA((nw, 7)), pltpu.SemaphoreType.DMA((nw,))],
        compiler_params=pltpu.CompilerParams(collective_id=collective_id),
    )(*shards, *([] if after is None else [after]))


_FLIPS = [(0, 0, 1), (1, 0, 0), (0, 1, 0), (1, 1, 0), (1, 0, 1), (0, 1, 1), (1, 1, 1)]


def _scatter_grad(name, collective_id, grad):
    n = grad.shape[0] // N_DEV

    def body(src, out, send_sems, recv_sems, local_sem):
        x, y, c = _place()
        me_idx = 4 * x + 2 * y + c
        peers = [(x ^ fx, y ^ fy, c ^ fc) for (fx, fy, fc) in _FLIPS]
        _handshake(peers)

        def block(idx):
            return src.at[pl.ds(pl.multiple_of(idx * n, 16), n), :]

        copies = [pltpu.make_async_remote_copy(
            src_ref=block(4 * px + 2 * py + pc), dst_ref=out.at[me_idx], send_sem=send_sems.at[k], recv_sem=recv_sems.at[k],
            device_id=(px, py, pc), device_id_type=MESH) for k, (px, py, pc) in enumerate(peers)]
        mine = pltpu.make_async_copy(block(me_idx), out.at[me_idx], local_sem)
        mine.start()
        for cp in copies:
            cp.start()
        for cp in copies:
            cp.wait_recv()
        for cp in copies:
            cp.wait_send()
        mine.wait()

    return pl.kernel(
        body, name=name,
        out_type=jax.ShapeDtypeStruct((N_DEV, n, grad.shape[1]), grad.dtype),
        mesh=_sequencer_mesh(),
        scratch_types=[pltpu.SemaphoreType.DMA((7,)), pltpu.SemaphoreType.DMA((7,)), pltpu.SemaphoreType.DMA],
        compiler_params=pltpu.CompilerParams(collective_id=collective_id),
    )(grad)


def _adamw_math(w, g, m, v):
    m = ADAM_B1 * m + (1.0 - ADAM_B1) * g
    v = ADAM_B2 * v + (1.0 - ADAM_B2) * (g * g)
    m_hat = m / (1.0 - ADAM_B1 ** ADAM_STEP)
    v_hat = v / (1.0 - ADAM_B2 ** ADAM_STEP)
    delta = -ADAM_LR * (m_hat / (jnp.sqrt(v_hat) + ADAM_EPS) + ADAM_WD * w)
    return delta, m, v


def _sum_adamw(name, slots, w, m, v, after):
    _, n, kk = slots.shape
    tr = _pick(n, (208, 176, 128, 96, 64, 32, 16))

    def body(s_ref, w_ref, m_ref, v_ref, after_ref, g_ref, d_ref, nm_ref, nv_ref):
        del after_ref
        g = s_ref[0].astype(F32)
        for p in range(1, N_DEV):
            g = g + s_ref[p].astype(F32)
        g_ref[...] = g
        d_ref[...], nm_ref[...], nv_ref[...] = _adamw_math(w_ref[...], g, m_ref[...], v_ref[...])

    row = pl.BlockSpec((tr, kk), lambda i: (i, 0))
    return pl.pallas_call(
        body, name=name, grid=(n // tr,),
        in_specs=[pl.BlockSpec((N_DEV, tr, kk), lambda i: (0, i, 0)), row, row, row, ANY],
        out_specs=[row] * 4,
        out_shape=[jax.ShapeDtypeStruct((n, kk), F32)] * 4,
        compiler_params=_cparams(("parallel",), 48),
    )(slots, w, m, v, after)


def _allreduce_small_adamw(part, w, m, v):
    r = part.shape[0]

    def body(part_ref, w_ref, m_ref, v_ref, g_ref, d_ref, nm_ref, nv_ref, slots, send_sems, recv_sems):
        x, y, c = _place()
        me_idx = 4 * x + 2 * y + c
        copies = []
        for k, (fx, fy, fc) in enumerate(_FLIPS):
            px, py, pc = x ^ fx, y ^ fy, c ^ fc
            copies.append(pltpu.make_async_remote_copy(
                src_ref=part_ref, dst_ref=slots.at[me_idx], send_sem=send_sems.at[k], recv_sem=recv_sems.at[k],
                device_id=(px, py, pc), device_id_type=MESH))
        for cp in copies:
            cp.start()
        slots[me_idx] = part_ref[...]
        for cp in copies:
            cp.wait_recv()
        for cp in copies:
            cp.wait_send()
        g = slots[0]
        for p in range(1, N_DEV):
            g = g + slots[p]
        g_ref[...] = g
        d_ref[...], nm_ref[...], nv_ref[...] = _adamw_math(w_ref[...], g, m_ref[...], v_ref[...])

    vm = pl.BlockSpec(memory_space=pltpu.VMEM)
    return pl.pallas_call(
        body, name="allreduce_small_adamw",
        in_specs=[vm] * 4, out_specs=[vm] * 4,
        out_shape=[jax.ShapeDtypeStruct((r, LANES), F32)] * 4,
        scratch_shapes=[pltpu.VMEM((N_DEV, r, LANES), F32), pltpu.SemaphoreType.DMA((7,)), pltpu.SemaphoreType.DMA((7,))],
        compiler_params=pltpu.CompilerParams(vmem_limit_bytes=48 * MIB),
    )(part, w, m, v)


def _pack(arrs):
    parts, meta, off = [], [], 0
    for a in arrs:
        flat = a.reshape(-1).astype(F32)
        rows = -(-flat.shape[0] // LANES)
        rows8 = -(-rows // 8) * 8
        flat = jnp.pad(flat, (0, rows8 * LANES - flat.shape[0]))
        parts.append(flat.reshape(rows8, LANES))
        meta.append((off, a.shape, a.size))
        off += rows8
    return jnp.concatenate(parts, axis=0), meta


def _unpack(packed, meta):
    outs = []
    for off, shape, size in meta:
        rows = -(-size // LANES)
        outs.append(packed[off:off + rows].reshape(-1)[:size].reshape(shape))
    return outs


def _silu_parts(a):
    sg = 1.0 / (1.0 + jnp.exp(-a))
    return a * sg, sg * (1.0 + a * (1.0 - sg))


def kernel(x, norm1_g, w_in, q_norm_g, k_norm_g, attn_sinks, gate_ln_g, gate_ln_b, w_spatial, b_spatial, out_norm_attn_g, out_norm_gate_g, w_out, norm2_g, w_ffn_gate, w_ffn_up, w_ffn_down, loss_target, m_norm1_g, m_w_in, m_q_norm_g, m_k_norm_g, m_attn_sinks, m_gate_ln_g, m_gate_ln_b, m_w_spatial, m_b_spatial, m_out_norm_attn_g, m_out_norm_gate_g, m_w_out, m_norm2_g, m_w_ffn_gate, m_w_ffn_up, m_w_ffn_down, v_norm1_g, v_w_in, v_q_norm_g, v_k_norm_g, v_attn_sinks, v_gate_ln_g, v_gate_ln_b, v_w_spatial, v_b_spatial, v_out_norm_attn_g, v_out_norm_gate_g, v_w_out, v_norm2_g, v_w_ffn_gate, v_w_ffn_up, v_w_ffn_down):
    nseq, seq, d = x.shape
    t = nseq * seq
    nb = seq // BLOCK
    inw = w_in.shape[2] * N_DEV
    dff = w_ffn_gate.shape[2] * N_DEV
    dm = _Dims(d, inw, q_norm_g.shape[-1])
    xf = x.reshape(t, d)
    tgt = loss_target.reshape(t, d)

    rows = lambda wv, transposed: jnp.swapaxes(wv, 1, 2)[0] if transposed else wv[0]
    big = {"w_in": (w_in, m_w_in, v_w_in, True), "w_out": (w_out, m_w_out, v_w_out, False),
           "w_ffn_gate": (w_ffn_gate, m_w_ffn_gate, v_w_ffn_gate, True), "w_ffn_up": (w_ffn_up, m_w_ffn_up, v_w_ffn_up, True),
           "w_ffn_down": (w_ffn_down, m_w_ffn_down, v_w_ffn_down, False)}
    big_rows = {nm: tuple(rows(arr, tr) for arr in (wv, mv, vv)) for nm, (wv, mv, vv, tr) in big.items()}
    shard = lambda nm: big_rows[nm][0].astype(WIRE)
    (win_t,) = _allgather_weights("gather_w_in", 1, [shard("w_in")])
    wout, wg_t, wu_t = _allgather_weights("gather_w_out_gate_up", 2, [shard("w_out"), shard("w_ffn_gate"), shard("w_ffn_up")], after=win_t)

    ws = w_spatial[0]
    prm = (q_norm_g, k_norm_g, attn_sinks[0], gate_ln_g[0], gate_ln_b[0], ws, b_spatial[0].T, out_norm_attn_g, out_norm_gate_g)

    h1 = _rms_fwd("rms1_fwd", xf, norm1_g)
    (proj,) = _matmul("mm_in", h1, win_t, "nt", [F32])
    y = _mixer_fwd(proj, prm, dm, nseq, nb)
    (wd,) = _allgather_weights("gather_w_down", 3, [shard("w_ffn_down")], after=y)
    (x2,) = _matmul("mm_out", y, wout, "nn", [F32], epilogue=lambda acc, xr: (xr + acc,), extras=[xf])
    h2 = _rms_fwd("rms2_fwd", x2, norm2_g)
    (a,) = _matmul("mm_gate", h2, wg_t, "nt", [F32])
    b, s = _matmul("mm_up", h2, wu_t, "nt", [F32, MXU], epilogue=lambda acc, av: (acc, _silu_parts(av)[0] * acc), extras=[a])

    def loss_epilogue(acc, x2v, tv):
        diff = (x2v + acc) - tv
        dx3 = diff * (1.0 / d)
        return dx3, dx3, jnp.sum(diff * diff)

    dx3, dx3b, lossp = _matmul("mm_down", s, wd, "nn", [F32, MXU], epilogue=loss_epilogue, extras=[x2, tgt], partial=True)
    loss_part = (0.5 / d) * jnp.sum(lossp[::8, ::LANES])

    def dswiglu(acc, av, bv):
        silu, dsilu = _silu_parts(av)
        return acc * bv * dsilu, acc * silu

    da, db = _matmul("mm_d_down", dx3b, wd, "nt", [MXU, MXU], epilogue=dswiglu, extras=[a, b])
    (g_wd,) = _matmul("mm_gw_down", s, dx3b, "tn", [WIRE])
    sl_wd = _scatter_grad("scatter_w_ffn_down", 4, g_wd)
    (dh2a,) = _matmul("mm_dh2_gate", da, wg_t, "nn", [F32], after=[g_wd])
    (g_wg,) = _matmul("mm_gw_gate", da, h2, "tn", [WIRE], after=[dh2a])
    sl_wg = _scatter_grad("scatter_w_ffn_gate", 5, g_wg)
    (dh2,) = _matmul("mm_dh2_up", db, wu_t, "nn", [F32], epilogue=lambda acc, pv: (pv + acc,), extras=[dh2a], after=[g_wg])
    (g_wu,) = _matmul("mm_gw_up", db, h2, "tn", [WIRE], after=[dh2])
    sl_wu = _scatter_grad("scatter_w_ffn_up", 6, g_wu)
    dx2, dx2b, dg2 = _rms_bwd("rms2_bwd", dh2, x2, norm2_g, dx3)

    (dy,) = _matmul("mm_d_out", dx2b, wout, "nt", [F32], after=[g_wu])
    (g_wout,) = _matmul("mm_gw_out", y, dx2b, "tn", [WIRE], after=[dy])
    sl_wout = _scatter_grad("scatter_w_out", 7, g_wout)
    (dproj0, dkv, dgq, dgk, dsink, dlng, dlnb, dws, dbst, dgoa, dgog) = _mixer_bwd(proj, dy, prm, dm, nseq, nb)
    dproj = _patch_kv(dproj0, dkv, dm)
    (g_win,) = _matmul("mm_gw_in", dproj, h1, "tn", [WIRE])
    sl_win = _scatter_grad("scatter_w_in", 8, g_win)
    (dh1,) = _matmul("mm_d_in", dproj, win_t, "nn", [F32], after=[g_win])
    dx, _, dg1 = _rms_bwd("rms1_bwd", dh1, xf, norm1_g, dx2)

    big_out = {}
    last = dx
    for nm, sl in (("w_ffn_down", sl_wd), ("w_ffn_gate", sl_wg), ("w_ffn_up", sl_wu), ("w_out", sl_wout), ("w_in", sl_win)):
        res = _sum_adamw("adamw_" + nm, sl, *big_rows[nm], after=last)
        last = res[1]
        big_out[nm] = tuple(jnp.swapaxes(r[None], 1, 2) if big[nm][3] else r[None] for r in res)

    nh = dm.nh
    small_names = ["norm1_g", "q_norm_g", "k_norm_g", "attn_sinks", "gate_ln_g", "gate_ln_b", "w_spatial", "b_spatial",
                   "out_norm_attn_g", "out_norm_gate_g", "norm2_g"]
    small_g = [dg1, dgq, dgk, dsink[:, :nh], dlng[None], dlnb[None], dws[None], dbst.T[None], dgoa, dgog, dg2]
    small_w = [norm1_g, q_norm_g, k_norm_g, attn_sinks, gate_ln_g, gate_ln_b, w_spatial, b_spatial, out_norm_attn_g, out_norm_gate_g, norm2_g]
    small_m = [m_norm1_g, m_q_norm_g, m_k_norm_g, m_attn_sinks, m_gate_ln_g, m_gate_ln_b, m_w_spatial, m_b_spatial, m_out_norm_attn_g, m_out_norm_gate_g, m_norm2_g]
    small_v = [v_norm1_g, v_q_norm_g, v_k_norm_g, v_attn_sinks, v_gate_ln_g, v_gate_ln_b, v_w_spatial, v_b_spatial, v_out_norm_attn_g, v_out_norm_gate_g, v_norm2_g]
    zero = jnp.zeros((1,), F32)
    part, meta = _pack(small_g + [loss_part.reshape(1)])
    pw, _ = _pack(small_w + [zero])
    pm, _ = _pack(small_m + [zero])
    pv, _ = _pack(small_v + [zero])
    sg, sd, sm, sv = _allreduce_small_adamw(part, pw, pm, pv)
    ug, ud, um, uv = _unpack(sg, meta), _unpack(sd, meta), _unpack(sm, meta), _unpack(sv, meta)
    loss = ug[-1].reshape(())
    small_out = {nm: (ug[k], ud[k], um[k], uv[k]) for k, nm in enumerate(small_names)}

    order = ["norm1_g", "w_in", "q_norm_g", "k_norm_g", "attn_sinks", "gate_ln_g", "gate_ln_b", "w_spatial", "b_spatial",
             "out_norm_attn_g", "out_norm_gate_g", "w_out", "norm2_g", "w_ffn_gate", "w_ffn_up", "w_ffn_down"]
    allo = {**big_out, **small_out}
    outs = [loss, dx.reshape(nseq, seq, d)]
    for k in range(4):
        outs += [allo[nm][k] for nm in order]
    return tuple(outs)
```

```python
import math

import jax
import jax.numpy as jnp
from jax import lax
from jax.experimental import pallas as pl
from jax.experimental.pallas import tpu as pltpu
from jax.experimental.pallas import tpu_sc as plsc

F32 = jnp.float32
MXU = jnp.bfloat16
WIRE = jnp.bfloat16
EPS = 1e-6
BLOCK = 128
GROUP_DIM = 128
N_KV_HEADS = 2
NEG = -1e30
N_DEV = 8
LANES = 128
MIB = 1024 * 1024

ADAM_LR = 0.001
ADAM_B1 = 0.9
ADAM_B2 = 0.999
ADAM_EPS = 1e-08
ADAM_WD = 0.01
ADAM_STEP = 10

MESH = pl.DeviceIdType.MESH
ANY = pl.BlockSpec(memory_space=pl.ANY)


def _pick(n, cands):
    for c in cands:
        if n % c == 0:
            return c
    return n


def _cparams(sem, vmem_mb):
    return pltpu.CompilerParams(dimension_semantics=sem, vmem_limit_bytes=vmem_mb * MIB)


VMEM_TILE_BUDGET = 44 * MIB
HBM_BYTES_PER_US = 3.0e6
STEP_US = 0.4
MIN_TILE_N = 512


def _tile_candidates(n):
    return [c for c in range(min(n, 2048), 0, -LANES) if n % c == 0 and c % LANES == 0] or [n]


def _matmul_tiles(m, n, kk, esz, n_extras, out_sizes, partial):
    best = None
    wide = [c for c in _tile_candidates(n) if c >= MIN_TILE_N] or _tile_candidates(n)
    for tm in _tile_candidates(m):
        for tn in wide:
            vmem = 2 * (tm + tn) * kk * esz + tm * tn * (4 + 2 * 4 * n_extras + 2 * sum(out_sizes)) + partial * 2 * 8 * LANES * 4
            if vmem > VMEM_TILE_BUDGET:
                continue
            cost = (m // tm) * n * kk * esz / HBM_BYTES_PER_US + (m // tm) * (n // tn) * STEP_US
            if best is None or cost < best[0]:
                best = (cost, tm, tn, vmem)
    assert best is not None, (m, n, kk)
    return best[1:]


def _matmul(name, a, b, mode, out_dtypes, epilogue=None, extras=(), after=(), partial=False):
    if mode == "nn":
        (m, kk), n = a.shape, b.shape[1]
        dn = (((1,), (0,)), ((), ()))
    elif mode == "nt":
        (m, kk), n = a.shape, b.shape[0]
        dn = (((1,), (1,)), ((), ()))
    else:
        (kk, m), n = a.shape, b.shape[1]
        dn = (((0,), (0,)), ((), ()))
    tm, tn, vmem = _matmul_tiles(m, n, kk, a.dtype.itemsize, len(extras), [jnp.dtype(dt).itemsize for dt in out_dtypes], partial)
    a_spec = pl.BlockSpec((kk, tm), lambda i, j: (0, i)) if mode == "tn" else pl.BlockSpec((tm, kk), lambda i, j: (i, 0))
    b_spec = pl.BlockSpec((tn, kk), lambda i, j: (j, 0)) if mode == "nt" else pl.BlockSpec((kk, tn), lambda i, j: (0, j))
    tile = pl.BlockSpec((tm, tn), lambda i, j: (i, j))
    ne, na, no = len(extras), len(after), len(out_dtypes)

    def body(a_ref, b_ref, *rest):
        extra_refs, out_refs = rest[:ne], rest[ne + na:]
        acc = lax.dot_general(a_ref[...], b_ref[...], dn, preferred_element_type=F32)
        tiles = (acc,) if epilogue is None else epilogue(acc, *[r[...] for r in extra_refs])
        for o_ref, t in zip(out_refs[:no], tiles[:no]):
            o_ref[...] = t.astype(o_ref.dtype)
        if partial:
            out_refs[no][...] = jnp.full((8, LANES), tiles[no], F32)

    out_specs = [tile] * no
    out_shape = [jax.ShapeDtypeStruct((m, n), dt) for dt in out_dtypes]
    if partial:
        out_specs.append(pl.BlockSpec((8, LANES), lambda i, j: (i, j)))
        out_shape.append(jax.ShapeDtypeStruct((m // tm * 8, n // tn * LANES), F32))
    return pl.pallas_call(
        body, name=name, grid=(m // tm, n // tn),
        in_specs=[a_spec, b_spec] + [tile] * ne + [ANY] * na,
        out_specs=out_specs, out_shape=out_shape,
        compiler_params=_cparams(("parallel", "arbitrary"), min(vmem // MIB + 8, 60)),
    )(a, b, *extras, *after)


def _rms_fwd(name, x, g):
    t, d = x.shape
    tm = _pick(t, (512, 256, 128))

    def body(x_ref, g_ref, h_ref):
        xv = x_ref[...]
        r = lax.rsqrt(jnp.mean(xv * xv, axis=-1, keepdims=True) + EPS)
        h_ref[...] = (xv * r * g_ref[...]).astype(h_ref.dtype)

    return pl.pallas_call(
        body, name=name, grid=(t // tm,),
        in_specs=[pl.BlockSpec((tm, d), lambda i: (i, 0)), pl.BlockSpec((1, d), lambda i: (0, 0))],
        out_specs=pl.BlockSpec((tm, d), lambda i: (i, 0)),
        out_shape=jax.ShapeDtypeStruct((t, d), MXU),
        compiler_params=_cparams(("parallel",), 32),
    )(x, g)


def _rms_bwd(name, dh, x, g, res):
    t, d = x.shape
    tm = _pick(t, (256, 128))

    def body(dh_ref, x_ref, g_ref, res_ref, dx_ref, dxb_ref, dg_ref):
        @pl.when(pl.program_id(0) == 0)
        def _():
            dg_ref[...] = jnp.zeros_like(dg_ref)

        xv, dhv = x_ref[...], dh_ref[...]
        r = lax.rsqrt(jnp.mean(xv * xv, axis=-1, keepdims=True) + EPS)
        xh = xv * r
        dg_ref[...] += jnp.sum(dhv * xh, axis=0, keepdims=True)
        dxh = dhv * g_ref[...]
        dx = res_ref[...] + r * (dxh - xh * jnp.mean(dxh * xh, axis=-1, keepdims=True))
        dx_ref[...] = dx
        dxb_ref[...] = dx.astype(dxb_ref.dtype)

    row = pl.BlockSpec((tm, d), lambda i: (i, 0))
    vec = pl.BlockSpec((1, d), lambda i: (0, 0))
    return pl.pallas_call(
        body, name=name, grid=(t // tm,),
        in_specs=[row, row, vec, row],
        out_specs=[row, row, vec],
        out_shape=[jax.ShapeDtypeStruct((t, d), F32), jax.ShapeDtypeStruct((t, d), MXU), jax.ShapeDtypeStruct((1, d), F32)],
        compiler_params=_cparams(("arbitrary",), 40),
    )(dh, x, g, res)


_INV_SQRT2 = 0.7071067811865476
_INV_SQRT_2PI = 0.3989422804014327


def _dot_nt(a, b):
    return lax.dot_general(a, b, (((1,), (1,)), ((), ())), preferred_element_type=F32)


def _dot_tn(a, b):
    return lax.dot_general(a, b, (((0,), (0,)), ((), ())), preferred_element_type=F32)


def _dot(a, b):
    return jnp.dot(a, b, preferred_element_type=F32)


def _col_rms(v):
    return lax.rsqrt(jnp.mean(v * v, axis=0, keepdims=True) + EPS)


class _Dims:
    def __init__(self, d_model, in_width, head_dim):
        self.d = d_model
        self.aw = d_model // 2
        self.gw = d_model - self.aw
        self.kvw = (in_width - self.aw - 2 * self.gw) // 2
        self.hd = head_dim
        self.nh = self.aw // head_dim
        self.nkv = self.kvw // head_dim
        self.grp = self.nh // self.nkv
        self.ng = self.gw // GROUP_DIM
        self.inw = in_width
        self.zoff = self.aw + 2 * self.kvw
        assert self.nkv == N_KV_HEADS and self.zoff + 2 * self.gw == in_width and self.aw % (2 * self.kvw) == 0


def _band_masks(first):
    kj = lax.broadcasted_iota(jnp.int32, (2 * BLOCK, BLOCK), 0)
    qi = lax.broadcasted_iota(jnp.int32, (2 * BLOCK, BLOCK), 1)
    dist = qi + BLOCK - kj
    valid = (dist >= 0) & (dist < BLOCK) & ((kj >= BLOCK) | jnp.logical_not(first))
    return valid, dist.astype(F32)


def _kv_band(dm, kh, p_ref, pkv_ref, gk2):
    ko = dm.aw + kh * dm.hd
    vo = dm.aw + dm.kvw + kh * dm.hd
    k_t = jnp.concatenate([pkv_ref[kh * dm.hd:(kh + 1) * dm.hd, :], p_ref[ko:ko + dm.hd, :]], axis=1)
    v_t = jnp.concatenate([pkv_ref[dm.kvw + kh * dm.hd:dm.kvw + (kh + 1) * dm.hd, :], p_ref[vo:vo + dm.hd, :]], axis=1)
    kn_t = k_t * _col_rms(k_t) * gk2
    return kn_t.astype(MXU), kn_t.T.astype(MXU), v_t.astype(MXU), v_t.T.astype(MXU)


def _attn_head_fwd(dm, h, q_t, gq, kn, v_tb, sink, valid, dist):
    rq = _col_rms(q_t)
    qh = q_t * rq
    qnb = (qh * gq).astype(MXU)
    slope = math.pow(2.0, -8.0 * (h + 1) / dm.nh)
    s = _dot(kn, qnb) * (dm.hd ** -0.5)
    logits = jnp.where(valid, s - slope * dist, NEG)
    m = jnp.maximum(jnp.max(logits, axis=0, keepdims=True), sink)
    e = jnp.exp(logits - m)
    es = jnp.exp(sink - m)
    inv = 1.0 / (jnp.sum(e, axis=0, keepdims=True) + es)
    p = e * inv
    o = _dot(v_tb, p.astype(MXU))
    return o, p, es * inv, rq, qh, qnb


def _gelu_cdf(z):
    return 0.5 * (1.0 + lax.erf(z * _INV_SQRT2))


def _gate_group_fwd(dm, g, p_ref, lng_ref, lnb_ref, ws_ref, bs_ref, tril):
    uo = dm.zoff + g * GROUP_DIM
    vo = dm.zoff + dm.gw + g * GROUP_DIM
    zu, zv = p_ref[uo:uo + GROUP_DIM, :], p_ref[vo:vo + GROUP_DIM, :]
    u, v = zu * _gelu_cdf(zu), zv * _gelu_cdf(zv)
    xc = v - jnp.mean(v, axis=0, keepdims=True)
    rstd = lax.rsqrt(jnp.mean(xc * xc, axis=0, keepdims=True) + EPS)
    xh = xc * rstd
    rows = slice(g * GROUP_DIM, (g + 1) * GROUP_DIM)
    vnb = (xh * lng_ref[rows, :] + lnb_ref[rows, :]).astype(MXU)
    wt = jnp.where(tril, ws_ref[g], 0.0).astype(MXU)
    mixed = _dot_nt(vnb, wt) + bs_ref[g:g + 1, :]
    return u * mixed, u, mixed, xh, rstd, vnb, wt


def _mixer_specs(dm, nb, clamp):
    kvblk = dm.aw // (2 * dm.kvw)

    def cur(s, i):
        return (0, s * nb + clamp(i))

    def prev(s, i):
        return (kvblk, s * nb + jnp.maximum(clamp(i) - 1, 0))

    full = lambda shape: pl.BlockSpec(shape, lambda s, i: tuple(0 for _ in shape))
    return cur, prev, full


def _tril():
    return lax.broadcasted_iota(jnp.int32, (BLOCK, BLOCK), 0) >= lax.broadcasted_iota(jnp.int32, (BLOCK, BLOCK), 1)


def _mixer_fwd(proj_t, prm, dm, nseq, nb):
    gq, gk2, sinks, lng, lnb, ws, bs, goa, gog = prm
    t = proj_t.shape[1]
    cur, prev, full = _mixer_specs(dm, nb, lambda i: i)

    def body(p_ref, pkv_ref, gq_ref, gk_ref, sink_ref, lng_ref, lnb_ref, ws_ref, bs_ref, goa_ref, gog_ref, y_ref, att_scr, gate_scr):
        i = pl.program_id(1)
        valid, dist = _band_masks(i == 0)
        gqv, gkv = gq_ref[...], gk_ref[...]
        for kh in range(dm.nkv):
            _, kn, v_tb, _ = _kv_band(dm, kh, p_ref, pkv_ref, gkv)
            for g in range(dm.grp):
                h = kh * dm.grp + g
                rows = slice(h * dm.hd, (h + 1) * dm.hd)
                att_scr[rows, :] = _attn_head_fwd(dm, h, p_ref[rows, :], gqv, kn, v_tb, sink_ref[h], valid, dist)[0]
        att = att_scr[...]
        y_ref[:dm.aw, :] = (att * _col_rms(att) * goa_ref[...]).astype(y_ref.dtype)
        tril = _tril()
        for g in range(dm.ng):
            gate_scr[g * GROUP_DIM:(g + 1) * GROUP_DIM, :] = _gate_group_fwd(dm, g, p_ref, lng_ref, lnb_ref, ws_ref, bs_ref, tril)[0]
        gt = gate_scr[...]
        y_ref[dm.aw:, :] = (gt * _col_rms(gt) * gog_ref[...]).astype(y_ref.dtype)

    return pl.pallas_call(
        body, name="mixer_fwd", grid=(nseq, nb),
        in_specs=[pl.BlockSpec((dm.inw, BLOCK), cur), pl.BlockSpec((2 * dm.kvw, BLOCK), prev),
                  full(gq.shape), full(gk2.shape), pl.BlockSpec(memory_space=pltpu.SMEM),
                  full(lng.shape), full(lnb.shape), full(ws.shape), full(bs.shape), full(goa.shape), full(gog.shape)],
        out_specs=pl.BlockSpec((dm.d, BLOCK), cur),
        out_shape=jax.ShapeDtypeStruct((dm.d, t), MXU),
        scratch_shapes=[pltpu.VMEM((dm.aw, BLOCK), F32), pltpu.VMEM((dm.gw, BLOCK), F32)],
        compiler_params=_cparams(("parallel", "arbitrary"), 40),
    )(proj_t, proj_t, gq, gk2, sinks, lng, lnb, ws, bs, goa, gog)


def _mixer_bwd(proj_t, dy_t, prm, dm, nseq, nb):
    gq, gk2, sinks, lng, lnb, ws, bs, goa, gog = prm
    t = proj_t.shape[1]
    clamp = lambda i: jnp.minimum(i, nb - 1)
    cur, prev, full = _mixer_specs(dm, nb, clamp)
    kvw2 = 2 * dm.kvw

    def prev_kv_out(s, i):
        return (0, s * nb + jnp.maximum(i - 1, 0))

    def body(p_ref, pkv_ref, dy_ref, gq_ref, gk_ref, sink_ref, lng_ref, lnb_ref, ws_ref, bs_ref, goa_ref, gog_ref,
             dproj_ref, dkv_ref, dgq_ref, dgk_ref, dsink_ref, dlng_ref, dlnb_ref, dws_ref, dbs_ref, dgoa_ref, dgog_ref,
             att_scr, datt_scr, gate_scr, carry_scr, prevpart_scr, curpart_scr, kprev_scr,
             a_gq, a_gk, a_sink, a_lng, a_lnb, a_goa, a_gog):
        s_id, i = pl.program_id(0), pl.program_id(1)
        lane_accs = ((a_gq, dgq_ref), (a_gk, dgk_ref), (a_sink, dsink_ref), (a_lng, dlng_ref), (a_lnb, dlnb_ref),
                     (a_goa, dgoa_ref), (a_gog, dgog_ref))

        @pl.when((s_id == 0) & (i == 0))
        def _():
            for acc, _ in lane_accs:
                acc[...] = jnp.zeros_like(acc)
            dws_ref[...] = jnp.zeros_like(dws_ref)
            dbs_ref[...] = jnp.zeros_like(dbs_ref)

        gqv, gkv = gq_ref[...], gk_ref[...]

        @pl.when(i < nb)
        def _():
            valid, dist = _band_masks(i == 0)
            kvs = []
            for kh in range(dm.nkv):
                kv = _kv_band(dm, kh, p_ref, pkv_ref, gkv)
                kvs.append(kv)
                for g in range(dm.grp):
                    h = kh * dm.grp + g
                    rows = slice(h * dm.hd, (h + 1) * dm.hd)
                    att_scr[rows, :] = _attn_head_fwd(dm, h, p_ref[rows, :], gqv, kv[1], kv[2], sink_ref[h], valid, dist)[0]
            att = att_scr[...]
            dya = dy_ref[:dm.aw, :]
            ra = _col_rms(att)
            ah = att * ra
            a_goa[...] += dya * ah
            dah = dya * goa_ref[...]
            datt_scr[...] = ra * (dah - ah * jnp.mean(dah * ah, axis=0, keepdims=True))
            for kh in range(dm.nkv):
                kn_tb, kn, v_tb, vb = kvs[kh]
                dkn = jnp.zeros((dm.hd, 2 * BLOCK), F32)
                dvb = jnp.zeros((dm.hd, 2 * BLOCK), F32)
                for g in range(dm.grp):
                    h = kh * dm.grp + g
                    rows = slice(h * dm.hd, (h + 1) * dm.hd)
                    _, p, ps, rq, qh, qnb = _attn_head_fwd(dm, h, p_ref[rows, :], gqv, kn, v_tb, sink_ref[h], valid, dist)
                    do_b = datt_scr[rows, :].astype(MXU)
                    dp = _dot(vb, do_b)
                    delta = jnp.sum(p * dp, axis=0, keepdims=True)
                    dsb = (p * (dp - delta) * (dm.hd ** -0.5)).astype(MXU)
                    a_sink[h:h + 1, :] += -(ps * delta)
                    dqn = _dot(kn_tb, dsb)
                    dkn += _dot_nt(qnb, dsb)
                    dvb += _dot_nt(do_b, p.astype(MXU))
                    a_gq[...] += dqn * qh
                    dqh = dqn * gqv
                    dq = rq * (dqh - qh * jnp.mean(dqh * qh, axis=0, keepdims=True))
                    dproj_ref[rows, :] = dq.astype(dproj_ref.dtype)
                krows = slice(kh * dm.hd, (kh + 1) * dm.hd)
                vrows = slice(dm.kvw + kh * dm.hd, dm.kvw + (kh + 1) * dm.hd)
                prevpart_scr[krows, :] = dkn[:, :BLOCK]
                prevpart_scr[vrows, :] = dvb[:, :BLOCK]
                curpart_scr[krows, :] = dkn[:, BLOCK:]
                curpart_scr[vrows, :] = dvb[:, BLOCK:]
            dproj_ref[dm.aw:dm.zoff, :] = jnp.zeros((kvw2, BLOCK), dproj_ref.dtype)
            tril = _tril()
            saved = []
            for g in range(dm.ng):
                yg, u, mixed, xh, rstd, vnb, wt = _gate_group_fwd(dm, g, p_ref, lng_ref, lnb_ref, ws_ref, bs_ref, tril)
                gate_scr[g * GROUP_DIM:(g + 1) * GROUP_DIM, :] = yg
                saved.append((u, mixed, xh, rstd, vnb, wt))
            gt = gate_scr[...]
            dyg = dy_ref[dm.aw:, :]
            rg = _col_rms(gt)
            gh = gt * rg
            a_gog[...] += dyg * gh
            dgh = dyg * gog_ref[...]
            gate_scr[...] = rg * (dgh - gh * jnp.mean(dgh * gh, axis=0, keepdims=True))
            for g in range(dm.ng):
                u, mixed, xh, rstd, vnb, wt = saved[g]
                rows = slice(g * GROUP_DIM, (g + 1) * GROUP_DIM)
                dyg_g = gate_scr[rows, :]
                du = dyg_g * mixed
                dmix = dyg_g * u
                dmixb = dmix.astype(MXU)
                dbs_ref[g:g + 1, :] += jnp.sum(dmix, axis=0, keepdims=True)
                dws_ref[g] += jnp.where(tril, _dot_tn(dmixb, vnb), 0.0)
                dvn = _dot(dmixb, wt)
                a_lng[rows, :] += dvn * xh
                a_lnb[rows, :] += dvn
                dxh = dvn * lng_ref[rows, :]
                dv = rstd * (dxh - jnp.mean(dxh, axis=0, keepdims=True) - xh * jnp.mean(dxh * xh, axis=0, keepdims=True))
                uo = dm.zoff + g * GROUP_DIM
                vo = dm.zoff + dm.gw + g * GROUP_DIM
                zu, zv = p_ref[uo:uo + GROUP_DIM, :], p_ref[vo:vo + GROUP_DIM, :]
                dgu = _gelu_cdf(zu) + zu * (jnp.exp(-0.5 * zu * zu) * _INV_SQRT_2PI)
                dgv = _gelu_cdf(zv) + zv * (jnp.exp(-0.5 * zv * zv) * _INV_SQRT_2PI)
                dproj_ref[uo:uo + GROUP_DIM, :] = (du * dgu).astype(dproj_ref.dtype)
                dproj_ref[vo:vo + GROUP_DIM, :] = (dv * dgv).astype(dproj_ref.dtype)

        @pl.when(i == nb)
        def _():
            prevpart_scr[...] = jnp.zeros_like(prevpart_scr)

        @pl.when(i >= 1)
        def _():
            tot = carry_scr[...] + prevpart_scr[...]
            for kh in range(dm.nkv):
                krows = slice(kh * dm.hd, (kh + 1) * dm.hd)
                kraw = kprev_scr[krows, :]
                rk = _col_rms(kraw)
                khat = kraw * rk
                dkn = tot[krows, :]
                a_gk[...] += dkn * khat
                dkh = dkn * gkv[:, :BLOCK]
                dk = rk * (dkh - khat * jnp.mean(dkh * khat, axis=0, keepdims=True))
                dkv_ref[krows, :] = dk.astype(dkv_ref.dtype)
            dkv_ref[dm.kvw:, :] = tot[dm.kvw:, :].astype(dkv_ref.dtype)

        @pl.when(i < nb)
        def _():
            carry_scr[...] = curpart_scr[...]
            kprev_scr[...] = p_ref[dm.aw:dm.aw + dm.kvw, :]

        @pl.when((s_id == nseq - 1) & (i == nb))
        def _():
            for acc, out in lane_accs:
                out[...] = jnp.sum(acc[...], axis=1, keepdims=True)

    col = lambda rows: jax.ShapeDtypeStruct((rows, 1), F32)
    lane = lambda rows: pltpu.VMEM((rows, LANES), F32)
    return pl.pallas_call(
        body, name="mixer_bwd", grid=(nseq, nb + 1),
        in_specs=[pl.BlockSpec((dm.inw, BLOCK), cur), pl.BlockSpec((kvw2, BLOCK), prev), pl.BlockSpec((dm.d, BLOCK), cur),
                  full(gq.shape), full(gk2.shape), pl.BlockSpec(memory_space=pltpu.SMEM),
                  full(lng.shape), full(lnb.shape), full(ws.shape), full(bs.shape), full(goa.shape), full(gog.shape)],
        out_specs=[pl.BlockSpec((dm.inw, BLOCK), cur), pl.BlockSpec((kvw2, BLOCK), prev_kv_out),
                   full((dm.hd, 1)), full((dm.hd, 1)), full((dm.nh, 1)), full((dm.gw, 1)), full((dm.gw, 1)), full(ws.shape),
                   full(bs.shape), full((dm.aw, 1)), full((dm.gw, 1))],
        out_shape=[jax.ShapeDtypeStruct((dm.inw, t), MXU), jax.ShapeDtypeStruct((kvw2, t), MXU),
                   col(dm.hd), col(dm.hd), col(dm.nh), col(dm.gw), col(dm.gw), jax.ShapeDtypeStruct(ws.shape, F32),
                   jax.ShapeDtypeStruct(bs.shape, F32), col(dm.aw), col(dm.gw)],
        scratch_shapes=[pltpu.VMEM((dm.aw, BLOCK), F32), pltpu.VMEM((dm.aw, BLOCK), F32), pltpu.VMEM((dm.gw, BLOCK), F32),
                        pltpu.VMEM((kvw2, BLOCK), F32), pltpu.VMEM((kvw2, BLOCK), F32), pltpu.VMEM((kvw2, BLOCK), F32),
                        pltpu.VMEM((dm.kvw, BLOCK), F32),
                        lane(dm.hd), lane(dm.hd), lane(dm.nh), lane(dm.gw), lane(dm.gw), lane(dm.aw), lane(dm.gw)],
        compiler_params=_cparams(("arbitrary", "arbitrary"), 48),
    )(proj_t, proj_t, dy_t, gq, gk2, sinks, lng, lnb, ws, bs, goa, gog)


def _patch_kv(dproj_t, dkv_t, dm):
    t = dproj_t.shape[1]
    tc = _pick(t, (1024, 512, 256, 128))
    kvw2 = 2 * dm.kvw
    kvblk = dm.aw // kvw2

    def body(dproj_hbm, dkv_ref, out_ref):
        del dproj_hbm
        out_ref[...] = dkv_ref[...]

    return pl.pallas_call(
        body, name="patch_kv", grid=(t // tc,),
        in_specs=[ANY, pl.BlockSpec((kvw2, tc), lambda i: (0, i))],
        out_specs=pl.BlockSpec((kvw2, tc), lambda i: (kvblk, i)),
        out_shape=jax.ShapeDtypeStruct(dproj_t.shape, dproj_t.dtype),
        input_output_aliases={0: 0},
        compiler_params=_cparams(("parallel",), 32),
    )(dproj_t, dkv_t)


def _place():
    x, y, c = lax.axis_index("x"), lax.axis_index("y"), lax.axis_index("c")
    return x, y, c


def _handshake(peers):
    barrier = pltpu.get_barrier_semaphore()
    for p in peers:
        pl.semaphore_signal(barrier, inc=1, device_id=p, device_id_type=MESH)
    pl.semaphore_wait(barrier, len(peers))


def _sequencer_mesh():
    return plsc.ScalarSubcoreMesh(axis_name="sequencer", num_cores=1)


def _allgather_weights(name, collective_id, shards, after=None):
    nw = len(shards)

    def body(*refs):
        src, out = refs[:nw], refs[-3 - nw:-3]
        send_sems, recv_sems, local_sems = refs[-3:]
        x, y, c = _place()
        me, sibling = (x, y, c), (x, y, 1 - c)
        chips = [(1 - x, y), (x, 1 - y), (1 - x, 1 - y)]
        _handshake([sibling] + [(*chip, c) for chip in chips])

        def rows(w, place):
            n = src[w].shape[0]
            px, py, pc = place
            return out[w].at[pl.ds(pl.multiple_of((4 * px + 2 * py + pc) * n, 16), n), :]

        def copy(w, k, block, to, from_src=False):
            return pltpu.make_async_remote_copy(
                src_ref=src[w] if from_src else rows(w, block), dst_ref=rows(w, block),
                send_sem=send_sems.at[w, k], recv_sem=recv_sems.at[w, k], device_id=to, device_id_type=MESH)

        mine = [pltpu.make_async_copy(src[w], rows(w, me), local_sems.at[w]) for w in range(nw)]
        for cp in mine:
            cp.start()
        first = []
        for w in range(nw):
            first.append(copy(w, 0, me, sibling, from_src=True))
            first += [copy(w, 1 + j, me, (*chip, c), from_src=True) for j, chip in enumerate(chips)]
        for cp in first:
            cp.start()
        passed = []
        for w in range(nw):
            for j, chip in enumerate(chips):
                copy(w, 1 + j, (*chip, c), me).wait_recv()
                fwd = copy(w, 4 + j, (*chip, c), sibling)
                fwd.start()
                passed.append(fwd)
        for w in range(nw):
            copy(w, 0, sibling, me).wait_recv()
            for j, chip in enumerate(chips):
                copy(w, 4 + j, (*chip, 1 - c), me).wait_recv()
        for cp in first + passed:
            cp.wait_send()
        for cp in mine:
            cp.wait()

    return pl.kernel(
        body, name=name,
        out_type=[jax.ShapeDtypeStruct((N_DEV * s.shape[0], s.shape[1]), s.dtype) for s in shards],
        mesh=_sequencer_mesh(),
        scratch_types=[pltpu.SemaphoreType.DMA((nw, 7)), pltpu.SemaphoreType.DMA((nw, 7)), pltpu.SemaphoreType.DMA((nw,))],
        compiler_params=pltpu.CompilerParams(collective_id=collective_id),
    )(*shards, *([] if after is None else [after]))


_FLIPS = [(0, 0, 1), (1, 0, 0), (0, 1, 0), (1, 1, 0), (1, 0, 1), (0, 1, 1), (1, 1, 1)]


def _scatter_grad(name, collective_id, grad):
    n = grad.shape[0] // N_DEV

    def body(src, out, send_sems, recv_sems, local_sem):
        x, y, c = _place()
        me_idx = 4 * x + 2 * y + c
        peers = [(x ^ fx, y ^ fy, c ^ fc) for (fx, fy, fc) in _FLIPS]
        _handshake(peers)

        def block(idx):
            return src.at[pl.ds(pl.multiple_of(idx * n, 16), n), :]

        copies = [pltpu.make_async_remote_copy(
            src_ref=block(4 * px + 2 * py + pc), dst_ref=out.at[me_idx], send_sem=send_sems.at[k], recv_sem=recv_sems.at[k],
            device_id=(px, py, pc), device_id_type=MESH) for k, (px, py, pc) in enumerate(peers)]
        mine = pltpu.make_async_copy(block(me_idx), out.at[me_idx], local_sem)
        mine.start()
        for cp in copies:
            cp.start()
        for cp in copies:
            cp.wait_recv()
        for cp in copies:
            cp.wait_send()
        mine.wait()

    return pl.kernel(
        body, name=name,
        out_type=jax.ShapeDtypeStruct((N_DEV, n, grad.shape[1]), grad.dtype),
        mesh=_sequencer_mesh(),
        scratch_types=[pltpu.SemaphoreType.DMA((7,)), pltpu.SemaphoreType.DMA((7,)), pltpu.SemaphoreType.DMA],
        compiler_params=pltpu.CompilerParams(collective_id=collective_id),
    )(grad)


def _adamw_math(w, g, m, v):
    m = ADAM_B1 * m + (1.0 - ADAM_B1) * g
    v = ADAM_B2 * v + (1.0 - ADAM_B2) * (g * g)
    m_hat = m / (1.0 - ADAM_B1 ** ADAM_STEP)
    v_hat = v / (1.0 - ADAM_B2 ** ADAM_STEP)
    delta = -ADAM_LR * (m_hat / (jnp.sqrt(v_hat) + ADAM_EPS) + ADAM_WD * w)
    return delta, m, v


def _sum_adamw(name, slots, w, m, v, after):
    _, n, kk = slots.shape
    tr = _pick(n, (208, 176, 128, 96, 64, 32, 16))

    def body(s_ref, w_ref, m_ref, v_ref, after_ref, g_ref, d_ref, nm_ref, nv_ref):
        del after_ref
        g = s_ref[0].astype(F32)
        for p in range(1, N_DEV):
            g = g + s_ref[p].astype(F32)
        g_ref[...] = g
        d_ref[...], nm_ref[...], nv_ref[...] = _adamw_math(w_ref[...], g, m_ref[...], v_ref[...])

    row = pl.BlockSpec((tr, kk), lambda i: (i, 0))
    return pl.pallas_call(
        body, name=name, grid=(n // tr,),
        in_specs=[pl.BlockSpec((N_DEV, tr, kk), lambda i: (0, i, 0)), row, row, row, ANY],
        out_specs=[row] * 4,
        out_shape=[jax.ShapeDtypeStruct((n, kk), F32)] * 4,
        compiler_params=_cparams(("parallel",), 48),
    )(slots, w, m, v, after)


def _allreduce_small_adamw(part, w, m, v):
    r = part.shape[0]

    def body(part_ref, w_ref, m_ref, v_ref, g_ref, d_ref, nm_ref, nv_ref, slots, send_sems, recv_sems):
        x, y, c = _place()
        me_idx = 4 * x + 2 * y + c
        copies = []
        for k, (fx, fy, fc) in enumerate(_FLIPS):
            px, py, pc = x ^ fx, y ^ fy, c ^ fc
            copies.append(pltpu.make_async_remote_copy(
                src_ref=part_ref, dst_ref=slots.at[me_idx], send_sem=send_sems.at[k], recv_sem=recv_sems.at[k],
                device_id=(px, py, pc), device_id_type=MESH))
        for cp in copies:
            cp.start()
        slots[me_idx] = part_ref[...]
        for cp in copies:
            cp.wait_recv()
        for cp in copies:
            cp.wait_send()
        g = slots[0]
        for p in range(1, N_DEV):
            g = g + slots[p]
        g_ref[...] = g
        d_ref[...], nm_ref[...], nv_ref[...] = _adamw_math(w_ref[...], g, m_ref[...], v_ref[...])

    vm = pl.BlockSpec(memory_space=pltpu.VMEM)
    return pl.pallas_call(
        body, name="allreduce_small_adamw",
        in_specs=[vm] * 4, out_specs=[vm] * 4,
        out_shape=[jax.ShapeDtypeStruct((r, LANES), F32)] * 4,
        scratch_shapes=[pltpu.VMEM((N_DEV, r, LANES), F32), pltpu.SemaphoreType.DMA((7,)), pltpu.SemaphoreType.DMA((7,))],
        compiler_params=pltpu.CompilerParams(vmem_limit_bytes=48 * MIB),
    )(part, w, m, v)


def _pack(arrs):
    parts, meta, off = [], [], 0
    for a in arrs:
        flat = a.reshape(-1).astype(F32)
        rows = -(-flat.shape[0] // LANES)
        rows8 = -(-rows // 8) * 8
        flat = jnp.pad(flat, (0, rows8 * LANES - flat.shape[0]))
        parts.append(flat.reshape(rows8, LANES))
        meta.append((off, a.shape, a.size))
        off += rows8
    return jnp.concatenate(parts, axis=0), meta


def _unpack(packed, meta):
    outs = []
    for off, shape, size in meta:
        rows = -(-size // LANES)
        outs.append(packed[off:off + rows].reshape(-1)[:size].reshape(shape))
    return outs


def _silu_parts(a):
    sg = 1.0 / (1.0 + jnp.exp(-a))
    return a * sg, sg * (1.0 + a * (1.0 - sg))


def kernel(x, norm1_g, w_in, q_norm_g, k_norm_g, attn_sinks, gate_ln_g, gate_ln_b, w_spatial, b_spatial, out_norm_attn_g, out_norm_gate_g, w_out, norm2_g, w_ffn_gate, w_ffn_up, w_ffn_down, loss_target, m_norm1_g, m_w_in, m_q_norm_g, m_k_norm_g, m_attn_sinks, m_gate_ln_g, m_gate_ln_b, m_w_spatial, m_b_spatial, m_out_norm_attn_g, m_out_norm_gate_g, m_w_out, m_norm2_g, m_w_ffn_gate, m_w_ffn_up, m_w_ffn_down, v_norm1_g, v_w_in, v_q_norm_g, v_k_norm_g, v_attn_sinks, v_gate_ln_g, v_gate_ln_b, v_w_spatial, v_b_spatial, v_out_norm_attn_g, v_out_norm_gate_g, v_w_out, v_norm2_g, v_w_ffn_gate, v_w_ffn_up, v_w_ffn_down):
    nseq, seq, d = x.shape
    t = nseq * seq
    nb = seq // BLOCK
    inw = w_in.shape[2] * N_DEV
    dm = _Dims(d, inw, q_norm_g.shape[-1])
    xf = x.reshape(t, d)
    tgt = loss_target.reshape(t, d)

    rows = lambda wv, transposed: jnp.swapaxes(wv, 1, 2)[0] if transposed else wv[0]
    big = {"w_in": (w_in, m_w_in, v_w_in, True), "w_out": (w_out, m_w_out, v_w_out, False),
           "w_ffn_gate": (w_ffn_gate, m_w_ffn_gate, v_w_ffn_gate, True), "w_ffn_up": (w_ffn_up, m_w_ffn_up, v_w_ffn_up, True),
           "w_ffn_down": (w_ffn_down, m_w_ffn_down, v_w_ffn_down, False)}
    big_rows = {nm: tuple(rows(arr, tr) for arr in (wv, mv, vv)) for nm, (wv, mv, vv, tr) in big.items()}
    shard = lambda nm: big_rows[nm][0].astype(WIRE)
    (win_t,) = _allgather_weights("gather_w_in", 1, [shard("w_in")])
    wout, wg_t, wu_t = _allgather_weights("gather_w_out_gate_up", 2, [shard("w_out"), shard("w_ffn_gate"), shard("w_ffn_up")], after=win_t)

    lanes = lambda v, n=BLOCK: jnp.broadcast_to(v.reshape(-1, 1), (v.size, n))
    prm = (lanes(q_norm_g), lanes(k_norm_g, 2 * BLOCK), attn_sinks[0], lanes(gate_ln_g), lanes(gate_ln_b), w_spatial[0], b_spatial[0],
           lanes(out_norm_attn_g), lanes(out_norm_gate_g))

    h1 = _rms_fwd("rms1_fwd", xf, norm1_g)
    (proj_t,) = _matmul("mm_in", win_t, h1, "nt", [F32])
    y_t = _mixer_fwd(proj_t, prm, dm, nseq, nb)
    (wd,) = _allgather_weights("gather_w_down", 3, [shard("w_ffn_down")], after=y_t)
    (x2,) = _matmul("mm_out", y_t, wout, "tn", [F32], epilogue=lambda acc, xr: (xr + acc,), extras=[xf])
    h2 = _rms_fwd("rms2_fwd", x2, norm2_g)
    (a,) = _matmul("mm_gate", h2, wg_t, "nt", [F32])
    b, s = _matmul("mm_up", h2, wu_t, "nt", [F32, MXU], epilogue=lambda acc, av: (acc, _silu_parts(av)[0] * acc), extras=[a])

    def loss_epilogue(acc, x2v, tv):
        diff = (x2v + acc) - tv
        dx3 = diff * (1.0 / d)
        return dx3, dx3, jnp.sum(diff * diff)

    dx3, dx3b, lossp = _matmul("mm_down", s, wd, "nn", [F32, MXU], epilogue=loss_epilogue, extras=[x2, tgt], partial=True)
    loss_part = (0.5 / d) * jnp.sum(lossp[::8, ::LANES])

    def dswiglu(acc, av, bv):
        silu, dsilu = _silu_parts(av)
        return acc * bv * dsilu, acc * silu

    da, db = _matmul("mm_d_down", dx3b, wd, "nt", [MXU, MXU], epilogue=dswiglu, extras=[a, b])
    (g_wd,) = _matmul("mm_gw_down", s, dx3b, "tn", [WIRE])
    sl_wd = _scatter_grad("scatter_w_ffn_down", 4, g_wd)
    (dh2a,) = _matmul("mm_dh2_gate", da, wg_t, "nn", [F32], after=[g_wd])
    (g_wg,) = _matmul("mm_gw_gate", da, h2, "tn", [WIRE], after=[dh2a])
    sl_wg = _scatter_grad("scatter_w_ffn_gate", 5, g_wg)
    (dh2,) = _matmul("mm_dh2_up", db, wu_t, "nn", [F32], epilogue=lambda acc, pv: (pv + acc,), extras=[dh2a], after=[g_wg])
    (g_wu,) = _matmul("mm_gw_up", db, h2, "tn", [WIRE], after=[dh2])
    sl_wu = _scatter_grad("scatter_w_ffn_up", 6, g_wu)
    dx2, dx2b, dg2 = _rms_bwd("rms2_bwd", dh2, x2, norm2_g, dx3)

    (dy_t,) = _matmul("mm_d_out", wout, dx2b, "nt", [F32], after=[g_wu])
    (g_wout,) = _matmul("mm_gw_out", y_t, dx2b, "nn", [WIRE], after=[dy_t])
    sl_wout = _scatter_grad("scatter_w_out", 7, g_wout)
    (dproj0, dkv, dgq, dgk, dsink, dlng, dlnb, dws, dbs, dgoa, dgog) = _mixer_bwd(proj_t, dy_t, prm, dm, nseq, nb)
    dproj_t = _patch_kv(dproj0, dkv, dm)
    (g_win,) = _matmul("mm_gw_in", dproj_t, h1, "nn", [WIRE])
    sl_win = _scatter_grad("scatter_w_in", 8, g_win)
    (dh1,) = _matmul("mm_d_in", dproj_t, win_t, "tn", [F32], after=[g_win])
    dx, _, dg1 = _rms_bwd("rms1_bwd", dh1, xf, norm1_g, dx2)

    big_out = {}
    last = dx
    for nm, sl in (("w_ffn_down", sl_wd), ("w_ffn_gate", sl_wg), ("w_ffn_up", sl_wu), ("w_out", sl_wout), ("w_in", sl_win)):
        res = _sum_adamw("adamw_" + nm, sl, *big_rows[nm], after=last)
        last = res[1]
        big_out[nm] = tuple(jnp.swapaxes(r[None], 1, 2) if big[nm][3] else r[None] for r in res)

    small_names = ["norm1_g", "q_norm_g", "k_norm_g", "attn_sinks", "gate_ln_g", "gate_ln_b", "w_spatial", "b_spatial",
                   "out_norm_attn_g", "out_norm_gate_g", "norm2_g"]
    small_g = [dg1, dgq, dgk, dsink, dlng, dlnb, dws, dbs, dgoa, dgog, dg2]
    small_w = [norm1_g, q_norm_g, k_norm_g, attn_sinks, gate_ln_g, gate_ln_b, w_spatial, b_spatial, out_norm_attn_g, out_norm_gate_g, norm2_g]
    small_m = [m_norm1_g, m_q_norm_g, m_k_norm_g, m_attn_sinks, m_gate_ln_g, m_gate_ln_b, m_w_spatial, m_b_spatial, m_out_norm_attn_g, m_out_norm_gate_g, m_norm2_g]
    small_v = [v_norm1_g, v_q_norm_g, v_k_norm_g, v_attn_sinks, v_gate_ln_g, v_gate_ln_b, v_w_spatial, v_b_spatial, v_out_norm_attn_g, v_out_norm_gate_g, v_norm2_g]
    zero = jnp.zeros((1,), F32)
    part, _ = _pack(small_g + [loss_part.reshape(1)])
    pw, meta = _pack(small_w + [zero])
    pm, _ = _pack(small_m + [zero])
    pv, _ = _pack(small_v + [zero])
    sg, sd, sm, sv = _allreduce_small_adamw(part, pw, pm, pv)
    ug, ud, um, uv = _unpack(sg, meta), _unpack(sd, meta), _unpack(sm, meta), _unpack(sv, meta)
    loss = ug[-1].reshape(())
    small_out = {nm: (ug[k], ud[k], um[k], uv[k]) for k, nm in enumerate(small_names)}

    order = ["norm1_g", "w_in", "q_norm_g", "k_norm_g", "attn_sinks", "gate_ln_g", "gate_ln_b", "w_spatial", "b_spatial",
             "out_norm_attn_g", "out_norm_gate_g", "w_out", "norm2_g", "w_ffn_gate", "w_ffn_up", "w_ffn_down"]
    allo = {**big_out, **small_out}
    outs = [loss, dx.reshape(nseq, seq, d)]
    for k in range(4):
        outs += [allo[nm][k] for nm in order]
    return tuple(outs)
```

```python
import math

import jax
import jax.numpy as jnp
from jax import lax
from jax.experimental import pallas as pl
from jax.experimental.pallas import tpu as pltpu
from jax.experimental.pallas import tpu_sc as plsc

F32 = jnp.float32
MXU = jnp.bfloat16
WIRE = jnp.bfloat16
EPS = 1e-6
BLOCK = 128
GROUP_DIM = 128
N_KV_HEADS = 2
NEG = -1e30
N_DEV = 8
LANES = 128
MIB = 1024 * 1024

ADAM_LR = 0.001
ADAM_B1 = 0.9
ADAM_B2 = 0.999
ADAM_EPS = 1e-08
ADAM_WD = 0.01
ADAM_STEP = 10

MESH = pl.DeviceIdType.MESH
ANY = pl.BlockSpec(memory_space=pl.ANY)


def _pick(n, cands):
    for c in cands:
        if n % c == 0:
            return c
    return n


def _cparams(sem, vmem_mb):
    return pltpu.CompilerParams(dimension_semantics=sem, vmem_limit_bytes=vmem_mb * MIB)


VMEM_TILE_BUDGET = 44 * MIB
HBM_BYTES_PER_US = 3.0e6
STEP_US = 0.4
MIN_TILE_N = 512


def _tile_candidates(n):
    return [c for c in range(min(n, 2048), 0, -LANES) if n % c == 0 and c % LANES == 0] or [n]


def _matmul_tiles(m, n, kk, esz, n_extras, out_sizes, partial):
    best = None
    wide = [c for c in _tile_candidates(n) if c >= MIN_TILE_N] or _tile_candidates(n)
    for tm in _tile_candidates(m):
        for tn in wide:
            vmem = 2 * (tm + tn) * kk * esz + tm * tn * (4 + 2 * 4 * n_extras + 2 * sum(out_sizes)) + partial * 2 * 8 * LANES * 4
            if vmem > VMEM_TILE_BUDGET:
                continue
            cost = (m // tm) * n * kk * esz / HBM_BYTES_PER_US + (m // tm) * (n // tn) * STEP_US
            if best is None or cost < best[0]:
                best = (cost, tm, tn, vmem)
    assert best is not None, (m, n, kk)
    return best[1:]


def _matmul(name, a, b, mode, out_dtypes, epilogue=None, extras=(), after=(), partial=False):
    if mode == "nn":
        (m, kk), n = a.shape, b.shape[1]
        dn = (((1,), (0,)), ((), ()))
    elif mode == "nt":
        (m, kk), n = a.shape, b.shape[0]
        dn = (((1,), (1,)), ((), ()))
    else:
        (kk, m), n = a.shape, b.shape[1]
        dn = (((0,), (0,)), ((), ()))
    tm, tn, vmem = _matmul_tiles(m, n, kk, a.dtype.itemsize, len(extras), [jnp.dtype(dt).itemsize for dt in out_dtypes], partial)
    a_spec = pl.BlockSpec((kk, tm), lambda i, j: (0, i)) if mode == "tn" else pl.BlockSpec((tm, kk), lambda i, j: (i, 0))
    b_spec = pl.BlockSpec((tn, kk), lambda i, j: (j, 0)) if mode == "nt" else pl.BlockSpec((kk, tn), lambda i, j: (0, j))
    tile = pl.BlockSpec((tm, tn), lambda i, j: (i, j))
    ne, na, no = len(extras), len(after), len(out_dtypes)

    def body(a_ref, b_ref, *rest):
        extra_refs, out_refs = rest[:ne], rest[ne + na:]
        acc = lax.dot_general(a_ref[...], b_ref[...], dn, preferred_element_type=F32)
        tiles = (acc,) if epilogue is None else epilogue(acc, *[r[...] for r in extra_refs])
        for o_ref, t in zip(out_refs[:no], tiles[:no]):
            o_ref[...] = t.astype(o_ref.dtype)
        if partial:
            out_refs[no][...] = jnp.full((8, LANES), tiles[no], F32)

    out_specs = [tile] * no
    out_shape = [jax.ShapeDtypeStruct((m, n), dt) for dt in out_dtypes]
    if partial:
        out_specs.append(pl.BlockSpec((8, LANES), lambda i, j: (i, j)))
        out_shape.append(jax.ShapeDtypeStruct((m // tm * 8, n // tn * LANES), F32))
    return pl.pallas_call(
        body, name=name, grid=(m // tm, n // tn),
        in_specs=[a_spec, b_spec] + [tile] * ne + [ANY] * na,
        out_specs=out_specs, out_shape=out_shape,
        compiler_params=_cparams(("parallel", "arbitrary"), min(vmem // MIB + 8, 60)),
    )(a, b, *extras, *after)


def _rms_fwd(name, x, g):
    t, d = x.shape
    tm = _pick(t, (512, 256, 128))

    def body(x_ref, g_ref, h_ref):
        xv = x_ref[...]
        r = lax.rsqrt(jnp.mean(xv * xv, axis=-1, keepdims=True) + EPS)
        h_ref[...] = (xv * r * g_ref[...]).astype(h_ref.dtype)

    return pl.pallas_call(
        body, name=name, grid=(t // tm,),
        in_specs=[pl.BlockSpec((tm, d), lambda i: (i, 0)), pl.BlockSpec((1, d), lambda i: (0, 0))],
        out_specs=pl.BlockSpec((tm, d), lambda i: (i, 0)),
        out_shape=jax.ShapeDtypeStruct((t, d), MXU),
        compiler_params=_cparams(("parallel",), 32),
    )(x, g)


def _rms_bwd(name, dh, x, g, res):
    t, d = x.shape
    tm = _pick(t, (256, 128))

    def body(dh_ref, x_ref, g_ref, res_ref, dx_ref, dxb_ref, dg_ref):
        @pl.when(pl.program_id(0) == 0)
        def _():
            dg_ref[...] = jnp.zeros_like(dg_ref)

        xv, dhv = x_ref[...], dh_ref[...]
        r = lax.rsqrt(jnp.mean(xv * xv, axis=-1, keepdims=True) + EPS)
        xh = xv * r
        dg_ref[...] += jnp.sum(dhv * xh, axis=0, keepdims=True)
        dxh = dhv * g_ref[...]
        dx = res_ref[...] + r * (dxh - xh * jnp.mean(dxh * xh, axis=-1, keepdims=True))
        dx_ref[...] = dx
        dxb_ref[...] = dx.astype(dxb_ref.dtype)

    row = pl.BlockSpec((tm, d), lambda i: (i, 0))
    vec = pl.BlockSpec((1, d), lambda i: (0, 0))
    return pl.pallas_call(
        body, name=name, grid=(t // tm,),
        in_specs=[row, row, vec, row],
        out_specs=[row, row, vec],
        out_shape=[jax.ShapeDtypeStruct((t, d), F32), jax.ShapeDtypeStruct((t, d), MXU), jax.ShapeDtypeStruct((1, d), F32)],
        compiler_params=_cparams(("arbitrary",), 40),
    )(dh, x, g, res)


_INV_SQRT2 = 0.7071067811865476
_INV_SQRT_2PI = 0.3989422804014327


def _dot_nt(a, b):
    return lax.dot_general(a, b, (((1,), (1,)), ((), ())), preferred_element_type=F32)


def _dot_tn(a, b):
    return lax.dot_general(a, b, (((0,), (0,)), ((), ())), preferred_element_type=F32)


def _dot(a, b):
    return jnp.dot(a, b, preferred_element_type=F32)


def _col_rms(v):
    return lax.rsqrt(jnp.mean(v * v, axis=0, keepdims=True) + EPS)


class _Dims:
    def __init__(self, d_model, in_width, head_dim):
        self.d = d_model
        self.aw = d_model // 2
        self.gw = d_model - self.aw
        self.kvw = (in_width - self.aw - 2 * self.gw) // 2
        self.hd = head_dim
        self.nh = self.aw // head_dim
        self.nkv = self.kvw // head_dim
        self.grp = self.nh // self.nkv
        self.ng = self.gw // GROUP_DIM
        self.inw = in_width
        self.zoff = self.aw + 2 * self.kvw
        assert self.nkv == N_KV_HEADS and self.zoff + 2 * self.gw == in_width and self.aw % (2 * self.kvw) == 0


def _band_masks(first, grp):
    kj = lax.broadcasted_iota(jnp.int32, (2 * BLOCK, grp * BLOCK), 0)
    qi = lax.broadcasted_iota(jnp.int32, (2 * BLOCK, grp * BLOCK), 1) & (BLOCK - 1)
    dist = qi + BLOCK - kj
    valid = (dist >= 0) & (dist < BLOCK) & ((kj >= BLOCK) | jnp.logical_not(first))
    return valid, dist.astype(F32)


def _kv_band(dm, kh, p_ref, pkv_ref, gk2):
    ko = dm.aw + kh * dm.hd
    vo = dm.aw + dm.kvw + kh * dm.hd
    k_t = jnp.concatenate([pkv_ref[kh * dm.hd:(kh + 1) * dm.hd, :], p_ref[ko:ko + dm.hd, :]], axis=1)
    v_t = jnp.concatenate([pkv_ref[dm.kvw + kh * dm.hd:dm.kvw + (kh + 1) * dm.hd, :], p_ref[vo:vo + dm.hd, :]], axis=1)
    kn_t = k_t * _col_rms(k_t) * gk2
    return kn_t.astype(MXU), kn_t.T.astype(MXU), v_t.astype(MXU), v_t.T.astype(MXU)


def _group_heads(dm, kh):
    return range(kh * dm.grp, (kh + 1) * dm.grp)


def _lane_row(vals):
    return jnp.concatenate([jnp.full((1, BLOCK), v, F32) for v in vals], axis=1)


def _attn_group_fwd(dm, kh, p_ref, gq, kn, v_tb, sink_ref, valid, dist):
    heads = _group_heads(dm, kh)
    q = jnp.concatenate([p_ref[h * dm.hd:(h + 1) * dm.hd, :] for h in heads], axis=1)
    rq = _col_rms(q)
    qh = q * rq
    qnb = (qh * gq).astype(MXU)
    slopes = _lane_row([math.pow(2.0, -8.0 * (h + 1) / dm.nh) for h in heads])
    sinks = _lane_row([sink_ref[h] for h in heads])
    s = _dot(kn, qnb) * (dm.hd ** -0.5)
    logits = jnp.where(valid, s - slopes * dist, NEG)
    m = jnp.maximum(jnp.max(logits, axis=0, keepdims=True), sinks)
    e = jnp.exp(logits - m)
    es = jnp.exp(sinks - m)
    inv = 1.0 / (jnp.sum(e, axis=0, keepdims=True) + es)
    p = e * inv
    o = _dot(v_tb, p.astype(MXU))
    return o, p, es * inv, rq, qh, qnb


def _gelu_cdf(z):
    return 0.5 * (1.0 + lax.erf(z * _INV_SQRT2))


def _by_group(v, ng):
    return v.reshape(ng, GROUP_DIM, v.shape[1])


def _gate_fwd(dm, p_ref, lng_ref, lnb_ref, ws_ref, bs_ref, tril):
    zu, zv = p_ref[dm.zoff:dm.zoff + dm.gw, :], p_ref[dm.zoff + dm.gw:dm.zoff + 2 * dm.gw, :]
    u, v = zu * _gelu_cdf(zu), zv * _gelu_cdf(zv)
    v3 = _by_group(v, dm.ng)
    xc = v3 - jnp.mean(v3, axis=1, keepdims=True)
    rstd = lax.rsqrt(jnp.mean(xc * xc, axis=1, keepdims=True) + EPS)
    xh = (xc * rstd).reshape(dm.gw, BLOCK)
    vnb = (xh * lng_ref[...] + lnb_ref[...]).astype(MXU)
    wts = [jnp.where(tril, ws_ref[g], 0.0).astype(MXU) for g in range(dm.ng)]
    mixed = jnp.concatenate([_dot_nt(vnb[g * GROUP_DIM:(g + 1) * GROUP_DIM], wts[g]) + bs_ref[g:g + 1, :]
                             for g in range(dm.ng)], axis=0)
    return u * mixed, u, mixed, xh, rstd, vnb, wts, zu, zv


def _mixer_specs(dm, nb, clamp):
    kvblk = dm.aw // (2 * dm.kvw)

    def cur(s, i):
        return (0, s * nb + clamp(i))

    def prev(s, i):
        return (kvblk, s * nb + jnp.maximum(clamp(i) - 1, 0))

    full = lambda shape: pl.BlockSpec(shape, lambda s, i: tuple(0 for _ in shape))
    return cur, prev, full


def _tril():
    return lax.broadcasted_iota(jnp.int32, (BLOCK, BLOCK), 0) >= lax.broadcasted_iota(jnp.int32, (BLOCK, BLOCK), 1)


def _mixer_fwd(proj_t, prm, dm, nseq, nb):
    gq, gk2, sinks, lng, lnb, ws, bs, goa, gog = prm
    t = proj_t.shape[1]
    cur, prev, full = _mixer_specs(dm, nb, lambda i: i)

    def body(p_ref, pkv_ref, gq_ref, gk_ref, sink_ref, lng_ref, lnb_ref, ws_ref, bs_ref, goa_ref, gog_ref, y_ref, att_scr):
        i = pl.program_id(1)
        valid, dist = _band_masks(i == 0, dm.grp)
        gqv, gkv = gq_ref[...], gk_ref[...]
        for kh in range(dm.nkv):
            _, kn, v_tb, _ = _kv_band(dm, kh, p_ref, pkv_ref, gkv)
            o = _attn_group_fwd(dm, kh, p_ref, gqv, kn, v_tb, sink_ref, valid, dist)[0]
            for g, h in enumerate(_group_heads(dm, kh)):
                att_scr[h * dm.hd:(h + 1) * dm.hd, :] = o[:, g * BLOCK:(g + 1) * BLOCK]
        att = att_scr[...]
        y_ref[:dm.aw, :] = (att * _col_rms(att) * goa_ref[...]).astype(y_ref.dtype)
        gt = _gate_fwd(dm, p_ref, lng_ref, lnb_ref, ws_ref, bs_ref, _tril())[0]
        y_ref[dm.aw:, :] = (gt * _col_rms(gt) * gog_ref[...]).astype(y_ref.dtype)

    return pl.pallas_call(
        body, name="mixer_fwd", grid=(nseq, nb),
        in_specs=[pl.BlockSpec((dm.inw, BLOCK), cur), pl.BlockSpec((2 * dm.kvw, BLOCK), prev),
                  full(gq.shape), full(gk2.shape), pl.BlockSpec(memory_space=pltpu.SMEM),
                  full(lng.shape), full(lnb.shape), full(ws.shape), full(bs.shape), full(goa.shape), full(gog.shape)],
        out_specs=pl.BlockSpec((dm.d, BLOCK), cur),
        out_shape=jax.ShapeDtypeStruct((dm.d, t), MXU),
        scratch_shapes=[pltpu.VMEM((dm.aw, BLOCK), F32)],
        compiler_params=_cparams(("parallel", "arbitrary"), 40),
    )(proj_t, proj_t, gq, gk2, sinks, lng, lnb, ws, bs, goa, gog)


def _mixer_bwd(proj_t, dy_t, prm, dm, nseq, nb):
    gq, gk2, sinks, lng, lnb, ws, bs, goa, gog = prm
    t = proj_t.shape[1]
    clamp = lambda i: jnp.minimum(i, nb - 1)
    cur, prev, full = _mixer_specs(dm, nb, clamp)
    kvw2 = 2 * dm.kvw

    def prev_kv_out(s, i):
        return (0, s * nb + jnp.maximum(i - 1, 0))

    def body(p_ref, pkv_ref, dy_ref, gq_ref, gk_ref, sink_ref, lng_ref, lnb_ref, ws_ref, bs_ref, goa_ref, gog_ref,
             dproj_ref, dkv_ref, dgq_ref, dgk_ref, dsink_ref, dlng_ref, dlnb_ref, dws_ref, dbs_ref, dgoa_ref, dgog_ref,
             att_scr, datt_scr, carry_scr, prevpart_scr, curpart_scr, kprev_scr,
             a_gq, a_gk, a_sink, a_lng, a_lnb, a_goa, a_gog):
        s_id, i = pl.program_id(0), pl.program_id(1)
        lane_accs = ((a_gq, dgq_ref), (a_gk, dgk_ref), (a_sink, dsink_ref), (a_lng, dlng_ref), (a_lnb, dlnb_ref),
                     (a_goa, dgoa_ref), (a_gog, dgog_ref))

        @pl.when((s_id == 0) & (i == 0))
        def _():
            for acc, _ in lane_accs:
                acc[...] = jnp.zeros_like(acc)
            dws_ref[...] = jnp.zeros_like(dws_ref)
            dbs_ref[...] = jnp.zeros_like(dbs_ref)

        gqv, gkv = gq_ref[...], gk_ref[...]

        @pl.when(i < nb)
        def _():
            valid, dist = _band_masks(i == 0, dm.grp)
            kvs, fwd = [], []
            for kh in range(dm.nkv):
                kv = _kv_band(dm, kh, p_ref, pkv_ref, gkv)
                kvs.append(kv)
                fwd.append(_attn_group_fwd(dm, kh, p_ref, gqv, kv[1], kv[2], sink_ref, valid, dist))
                for g, h in enumerate(_group_heads(dm, kh)):
                    att_scr[h * dm.hd:(h + 1) * dm.hd, :] = fwd[kh][0][:, g * BLOCK:(g + 1) * BLOCK]
            att = att_scr[...]
            dya = dy_ref[:dm.aw, :]
            ra = _col_rms(att)
            ah = att * ra
            a_goa[...] += dya * ah
            dah = dya * goa_ref[...]
            datt_scr[...] = ra * (dah - ah * jnp.mean(dah * ah, axis=0, keepdims=True))
            for kh in range(dm.nkv):
                kn_tb, kn, v_tb, vb = kvs[kh]
                _, p, ps, rq, qh, qnb = fwd[kh]
                heads = _group_heads(dm, kh)
                do_b = jnp.concatenate([datt_scr[h * dm.hd:(h + 1) * dm.hd, :] for h in heads], axis=1).astype(MXU)
                dp = _dot(vb, do_b)
                delta = jnp.sum(p * dp, axis=0, keepdims=True)
                dsb = (p * (dp - delta) * (dm.hd ** -0.5)).astype(MXU)
                dsink = ps * delta
                dqn = _dot(kn_tb, dsb)
                dkn = _dot_nt(qnb, dsb)
                dvb = _dot_nt(do_b, p.astype(MXU))
                a_gq[...] += dqn * qh
                dqh = dqn * gqv
                dq = rq * (dqh - qh * jnp.mean(dqh * qh, axis=0, keepdims=True))
                for g, h in enumerate(heads):
                    a_sink[h:h + 1, :] += -dsink[:, g * BLOCK:(g + 1) * BLOCK]
                    dproj_ref[h * dm.hd:(h + 1) * dm.hd, :] = dq[:, g * BLOCK:(g + 1) * BLOCK].astype(dproj_ref.dtype)
                krows = slice(kh * dm.hd, (kh + 1) * dm.hd)
                vrows = slice(dm.kvw + kh * dm.hd, dm.kvw + (kh + 1) * dm.hd)
                prevpart_scr[krows, :] = dkn[:, :BLOCK]
                prevpart_scr[vrows, :] = dvb[:, :BLOCK]
                curpart_scr[krows, :] = dkn[:, BLOCK:]
                curpart_scr[vrows, :] = dvb[:, BLOCK:]
            dproj_ref[dm.aw:dm.zoff, :] = jnp.zeros((kvw2, BLOCK), dproj_ref.dtype)
            tril = _tril()
            gt, u, mixed, xh, rstd, vnb, wts, zu, zv = _gate_fwd(dm, p_ref, lng_ref, lnb_ref, ws_ref, bs_ref, tril)
            dyg = dy_ref[dm.aw:, :]
            rg = _col_rms(gt)
            gh = gt * rg
            a_gog[...] += dyg * gh
            dgh = dyg * gog_ref[...]
            dgt = rg * (dgh - gh * jnp.mean(dgh * gh, axis=0, keepdims=True))
            du = dgt * mixed
            dmix = dgt * u
            dmixb = dmix.astype(MXU)
            dbs_ref[...] += jnp.sum(_by_group(dmix, dm.ng), axis=1)
            dvn = []
            for g in range(dm.ng):
                rows = slice(g * GROUP_DIM, (g + 1) * GROUP_DIM)
                dws_ref[g] += jnp.where(tril, _dot_tn(dmixb[rows], vnb[rows]), 0.0)
                dvn.append(_dot(dmixb[rows], wts[g]))
            dvn = jnp.concatenate(dvn, axis=0)
            a_lng[...] += dvn * xh
            a_lnb[...] += dvn
            dxh3, xh3 = _by_group(dvn * lng_ref[...], dm.ng), _by_group(xh, dm.ng)
            dv = (rstd * (dxh3 - jnp.mean(dxh3, axis=1, keepdims=True) - xh3 * jnp.mean(dxh3 * xh3, axis=1, keepdims=True))).reshape(dm.gw, BLOCK)
            dgu = _gelu_cdf(zu) + zu * (jnp.exp(-0.5 * zu * zu) * _INV_SQRT_2PI)
            dgv = _gelu_cdf(zv) + zv * (jnp.exp(-0.5 * zv * zv) * _INV_SQRT_2PI)
            dproj_ref[dm.zoff:dm.zoff + dm.gw, :] = (du * dgu).astype(dproj_ref.dtype)
            dproj_ref[dm.zoff + dm.gw:, :] = (dv * dgv).astype(dproj_ref.dtype)

        @pl.when(i == nb)
        def _():
            prevpart_scr[...] = jnp.zeros_like(prevpart_scr)

        @pl.when(i >= 1)
        def _():
            tot = carry_scr[...] + prevpart_scr[...]
            for kh in range(dm.nkv):
                krows = slice(kh * dm.hd, (kh + 1) * dm.hd)
                kraw = kprev_scr[krows, :]
                rk = _col_rms(kraw)
                khat = kraw * rk
                dkn = tot[krows, :]
                a_gk[...] += dkn * khat
                dkh = dkn * gkv[:, :BLOCK]
                dk = rk * (dkh - khat * jnp.mean(dkh * khat, axis=0, keepdims=True))
                dkv_ref[krows, :] = dk.astype(dkv_ref.dtype)
            dkv_ref[dm.kvw:, :] = tot[dm.kvw:, :].astype(dkv_ref.dtype)

        @pl.when(i < nb)
        def _():
            carry_scr[...] = curpart_scr[...]
            kprev_scr[...] = p_ref[dm.aw:dm.aw + dm.kvw, :]

        @pl.when((s_id == nseq - 1) & (i == nb))
        def _():
            for acc, out in lane_accs:
                out[...] = jnp.sum(acc[...], axis=1, keepdims=True)

    col = lambda rows: jax.ShapeDtypeStruct((rows, 1), F32)
    lane = lambda rows: pltpu.VMEM((rows, LANES), F32)
    return pl.pallas_call(
        body, name="mixer_bwd", grid=(nseq, nb + 1),
        in_specs=[pl.BlockSpec((dm.inw, BLOCK), cur), pl.BlockSpec((kvw2, BLOCK), prev), pl.BlockSpec((dm.d, BLOCK), cur),
                  full(gq.shape), full(gk2.shape), pl.BlockSpec(memory_space=pltpu.SMEM),
                  full(lng.shape), full(lnb.shape), full(ws.shape), full(bs.shape), full(goa.shape), full(gog.shape)],
        out_specs=[pl.BlockSpec((dm.inw, BLOCK), cur), pl.BlockSpec((kvw2, BLOCK), prev_kv_out),
                   full((dm.hd, 1)), full((dm.hd, 1)), full((dm.nh, 1)), full((dm.gw, 1)), full((dm.gw, 1)), full(ws.shape),
                   full(bs.shape), full((dm.aw, 1)), full((dm.gw, 1))],
        out_shape=[jax.ShapeDtypeStruct((dm.inw, t), MXU), jax.ShapeDtypeStruct((kvw2, t), MXU),
                   col(dm.hd), col(dm.hd), col(dm.nh), col(dm.gw), col(dm.gw), jax.ShapeDtypeStruct(ws.shape, F32),
                   jax.ShapeDtypeStruct(bs.shape, F32), col(dm.aw), col(dm.gw)],
        scratch_shapes=[pltpu.VMEM((dm.aw, BLOCK), F32), pltpu.VMEM((dm.aw, BLOCK), F32),
                        pltpu.VMEM((kvw2, BLOCK), F32), pltpu.VMEM((kvw2, BLOCK), F32), pltpu.VMEM((kvw2, BLOCK), F32),
                        pltpu.VMEM((dm.kvw, BLOCK), F32),
                        pltpu.VMEM((dm.hd, dm.grp * BLOCK), F32), lane(dm.hd), lane(dm.nh), lane(dm.gw), lane(dm.gw), lane(dm.aw), lane(dm.gw)],
        compiler_params=_cparams(("arbitrary", "arbitrary"), 48),
    )(proj_t, proj_t, dy_t, gq, gk2, sinks, lng, lnb, ws, bs, goa, gog)


def _patch_kv(dproj_t, dkv_t, dm):
    t = dproj_t.shape[1]
    tc = _pick(t, (1024, 512, 256, 128))
    kvw2 = 2 * dm.kvw
    kvblk = dm.aw // kvw2

    def body(dproj_hbm, dkv_ref, out_ref):
        del dproj_hbm
        out_ref[...] = dkv_ref[...]

    return pl.pallas_call(
        body, name="patch_kv", grid=(t // tc,),
        in_specs=[ANY, pl.BlockSpec((kvw2, tc), lambda i: (0, i))],
        out_specs=pl.BlockSpec((kvw2, tc), lambda i: (kvblk, i)),
        out_shape=jax.ShapeDtypeStruct(dproj_t.shape, dproj_t.dtype),
        input_output_aliases={0: 0},
        compiler_params=_cparams(("parallel",), 32),
    )(dproj_t, dkv_t)


def _place():
    x, y, c = lax.axis_index("x"), lax.axis_index("y"), lax.axis_index("c")
    return x, y, c


def _handshake(peers):
    barrier = pltpu.get_barrier_semaphore()
    for p in peers:
        pl.semaphore_signal(barrier, inc=1, device_id=p, device_id_type=MESH)
    pl.semaphore_wait(barrier, len(peers))


def _sequencer_mesh():
    return plsc.ScalarSubcoreMesh(axis_name="sequencer", num_cores=1)


def _allgather_weights(name, collective_id, shards, after=None):
    nw = len(shards)

    def body(*refs):
        src, out = refs[:nw], refs[-3 - nw:-3]
        send_sems, recv_sems, local_sems = refs[-3:]
        x, y, c = _place()
        me, sibling = (x, y, c), (x, y, 1 - c)
        chips = [(1 - x, y), (x, 1 - y), (1 - x, 1 - y)]
        _handshake([sibling] + [(*chip, c) for chip in chips])

        def rows(w, place):
            n = src[w].shape[0]
            px, py, pc = place
            return out[w].at[pl.ds(pl.multiple_of((4 * px + 2 * py + pc) * n, 16), n), :]

        def copy(w, k, block, to, from_src=False):
            return pltpu.make_async_remote_copy(
                src_ref=src[w] if from_src else rows(w, block), dst_ref=rows(w, block),
                send_sem=send_sems.at[w, k], recv_sem=recv_sems.at[w, k], device_id=to, device_id_type=MESH)

        mine = [pltpu.make_async_copy(src[w], rows(w, me), local_sems.at[w]) for w in range(nw)]
        for cp in mine:
            cp.start()
        first = []
        for w in range(nw):
            first.append(copy(w, 0, me, sibling, from_src=True))
            first += [copy(w, 1 + j, me, (*chip, c), from_src=True) for j, chip in enumerate(chips)]
        for cp in first:
            cp.start()
        passed = []
        for w in range(nw):
            for j, chip in enumerate(chips):
                copy(w, 1 + j, (*chip, c), me).wait_recv()
                fwd = copy(w, 4 + j, (*chip, c), sibling)
                fwd.start()
                passed.append(fwd)
        for w in range(nw):
            copy(w, 0, sibling, me).wait_recv()
            for j, chip in enumerate(chips):
                copy(w, 4 + j, (*chip, 1 - c), me).wait_recv()
        for cp in first + passed:
            cp.wait_send()
        for cp in mine:
            cp.wait()

    return pl.kernel(
        body, name=name,
        out_type=[jax.ShapeDtypeStruct((N_DEV * s.shape[0], s.shape[1]), s.dtype) for s in shards],
        mesh=_sequencer_mesh(),
        scratch_types=[pltpu.SemaphoreType.DMA((nw, 7)), pltpu.SemaphoreType.DMA((nw, 7)), pltpu.SemaphoreType.DMA((nw,))],
        compiler_params=pltpu.CompilerParams(collective_id=collective_id),
    )(*shards, *([] if after is None else [after]))


_FLIPS = [(0, 0, 1), (1, 0, 0), (0, 1, 0), (1, 1, 0), (1, 0, 1), (0, 1, 1), (1, 1, 1)]


def _scatter_grad(name, collective_id, grad):
    n = grad.shape[0] // N_DEV

    def body(src, out, send_sems, recv_sems, local_sem):
        x, y, c = _place()
        me_idx = 4 * x + 2 * y + c
        peers = [(x ^ fx, y ^ fy, c ^ fc) for (fx, fy, fc) in _FLIPS]
        _handshake(peers)

        def block(idx):
            return src.at[pl.ds(pl.multiple_of(idx * n, 16), n), :]

        copies = [pltpu.make_async_remote_copy(
            src_ref=block(4 * px + 2 * py + pc), dst_ref=out.at[me_idx], send_sem=send_sems.at[k], recv_sem=recv_sems.at[k],
            device_id=(px, py, pc), device_id_type=MESH) for k, (px, py, pc) in enumerate(peers)]
        mine = pltpu.make_async_copy(block(me_idx), out.at[me_idx], local_sem)
        mine.start()
        for cp in copies:
            cp.start()
        for cp in copies:
            cp.wait_recv()
        for cp in copies:
            cp.wait_send()
        mine.wait()

    return pl.kernel(
        body, name=name,
        out_type=jax.ShapeDtypeStruct((N_DEV, n, grad.shape[1]), grad.dtype),
        mesh=_sequencer_mesh(),
        scratch_types=[pltpu.SemaphoreType.DMA((7,)), pltpu.SemaphoreType.DMA((7,)), pltpu.SemaphoreType.DMA],
        compiler_params=pltpu.CompilerParams(collective_id=collective_id),
    )(grad)


def _adamw_math(w, g, m, v):
    m = ADAM_B1 * m + (1.0 - ADAM_B1) * g
    v = ADAM_B2 * v + (1.0 - ADAM_B2) * (g * g)
    m_hat = m / (1.0 - ADAM_B1 ** ADAM_STEP)
    v_hat = v / (1.0 - ADAM_B2 ** ADAM_STEP)
    delta = -ADAM_LR * (m_hat / (jnp.sqrt(v_hat) + ADAM_EPS) + ADAM_WD * w)
    return delta, m, v


def _sum_adamw(name, slots, w, m, v, after):
    _, n, kk = slots.shape
    tr = _pick(n, (208, 176, 128, 96, 64, 32, 16))

    def body(s_ref, w_ref, m_ref, v_ref, after_ref, g_ref, d_ref, nm_ref, nv_ref):
        del after_ref
        g = s_ref[0].astype(F32)
        for p in range(1, N_DEV):
            g = g + s_ref[p].astype(F32)
        g_ref[...] = g
        d_ref[...], nm_ref[...], nv_ref[...] = _adamw_math(w_ref[...], g, m_ref[...], v_ref[...])

    row = pl.BlockSpec((tr, kk), lambda i: (i, 0))
    return pl.pallas_call(
        body, name=name, grid=(n // tr,),
        in_specs=[pl.BlockSpec((N_DEV, tr, kk), lambda i: (0, i, 0)), row, row, row, ANY],
        out_specs=[row] * 4,
        out_shape=[jax.ShapeDtypeStruct((n, kk), F32)] * 4,
        compiler_params=_cparams(("parallel",), 48),
    )(slots, w, m, v, after)


def _allreduce_small_adamw(part, w, m, v):
    r = part.shape[0]

    def body(part_ref, w_ref, m_ref, v_ref, g_ref, d_ref, nm_ref, nv_ref, slots, send_sems, recv_sems):
        x, y, c = _place()
        me_idx = 4 * x + 2 * y + c
        copies = []
        for k, (fx, fy, fc) in enumerate(_FLIPS):
            px, py, pc = x ^ fx, y ^ fy, c ^ fc
            copies.append(pltpu.make_async_remote_copy(
                src_ref=part_ref, dst_ref=slots.at[me_idx], send_sem=send_sems.at[k], recv_sem=recv_sems.at[k],
                device_id=(px, py, pc), device_id_type=MESH))
        for cp in copies:
            cp.start()
        slots[me_idx] = part_ref[...]
        for cp in copies:
            cp.wait_recv()
        for cp in copies:
            cp.wait_send()
        g = slots[0]
        for p in range(1, N_DEV):
            g = g + slots[p]
        g_ref[...] = g
        d_ref[...], nm_ref[...], nv_ref[...] = _adamw_math(w_ref[...], g, m_ref[...], v_ref[...])

    vm = pl.BlockSpec(memory_space=pltpu.VMEM)
    return pl.pallas_call(
        body, name="allreduce_small_adamw",
        in_specs=[vm] * 4, out_specs=[vm] * 4,
        out_shape=[jax.ShapeDtypeStruct((r, LANES), F32)] * 4,
        scratch_shapes=[pltpu.VMEM((N_DEV, r, LANES), F32), pltpu.SemaphoreType.DMA((7,)), pltpu.SemaphoreType.DMA((7,))],
        compiler_params=pltpu.CompilerParams(vmem_limit_bytes=48 * MIB),
    )(part, w, m, v)


def _pack(arrs):
    parts, meta, off = [], [], 0
    for a in arrs:
        flat = a.reshape(-1).astype(F32)
        rows = -(-flat.shape[0] // LANES)
        rows8 = -(-rows // 8) * 8
        flat = jnp.pad(flat, (0, rows8 * LANES - flat.shape[0]))
        parts.append(flat.reshape(rows8, LANES))
        meta.append((off, a.shape, a.size))
        off += rows8
    return jnp.concatenate(parts, axis=0), meta


def _unpack(packed, meta):
    outs = []
    for off, shape, size in meta:
        rows = -(-size // LANES)
        outs.append(packed[off:off + rows].reshape(-1)[:size].reshape(shape))
    return outs


def _silu_parts(a):
    sg = 1.0 / (1.0 + jnp.exp(-a))
    return a * sg, sg * (1.0 + a * (1.0 - sg))


def kernel(x, norm1_g, w_in, q_norm_g, k_norm_g, attn_sinks, gate_ln_g, gate_ln_b, w_spatial, b_spatial, out_norm_attn_g, out_norm_gate_g, w_out, norm2_g, w_ffn_gate, w_ffn_up, w_ffn_down, loss_target, m_norm1_g, m_w_in, m_q_norm_g, m_k_norm_g, m_attn_sinks, m_gate_ln_g, m_gate_ln_b, m_w_spatial, m_b_spatial, m_out_norm_attn_g, m_out_norm_gate_g, m_w_out, m_norm2_g, m_w_ffn_gate, m_w_ffn_up, m_w_ffn_down, v_norm1_g, v_w_in, v_q_norm_g, v_k_norm_g, v_attn_sinks, v_gate_ln_g, v_gate_ln_b, v_w_spatial, v_b_spatial, v_out_norm_attn_g, v_out_norm_gate_g, v_w_out, v_norm2_g, v_w_ffn_gate, v_w_ffn_up, v_w_ffn_down):
    nseq, seq, d = x.shape
    t = nseq * seq
    nb = seq // BLOCK
    inw = w_in.shape[2] * N_DEV
    dm = _Dims(d, inw, q_norm_g.shape[-1])
    xf = x.reshape(t, d)
    tgt = loss_target.reshape(t, d)

    rows = lambda wv, transposed: jnp.swapaxes(wv, 1, 2)[0] if transposed else wv[0]
    big = {"w_in": (w_in, m_w_in, v_w_in, True), "w_out": (w_out, m_w_out, v_w_out, False),
           "w_ffn_gate": (w_ffn_gate, m_w_ffn_gate, v_w_ffn_gate, True), "w_ffn_up": (w_ffn_up, m_w_ffn_up, v_w_ffn_up, True),
           "w_ffn_down": (w_ffn_down, m_w_ffn_down, v_w_ffn_down, False)}
    big_rows = {nm: tuple(rows(arr, tr) for arr in (wv, mv, vv)) for nm, (wv, mv, vv, tr) in big.items()}
    shard = lambda nm: big_rows[nm][0].astype(WIRE)
    (win_t,) = _allgather_weights("gather_w_in", 1, [shard("w_in")])
    wout, wg_t, wu_t = _allgather_weights("gather_w_out_gate_up", 2, [shard("w_out"), shard("w_ffn_gate"), shard("w_ffn_up")], after=win_t)

    lanes = lambda v, n=BLOCK: jnp.broadcast_to(v.reshape(-1, 1), (v.size, n))
    prm = (lanes(q_norm_g, dm.grp * BLOCK), lanes(k_norm_g, 2 * BLOCK), attn_sinks[0], lanes(gate_ln_g), lanes(gate_ln_b), w_spatial[0], b_spatial[0],
           lanes(out_norm_attn_g), lanes(out_norm_gate_g))

    h1 = _rms_fwd("rms1_fwd", xf, norm1_g)
    (proj_t,) = _matmul("mm_in", win_t, h1, "nt", [F32])
    y_t = _mixer_fwd(proj_t, prm, dm, nseq, nb)
    (wd,) = _allgather_weights("gather_w_down", 3, [shard("w_ffn_down")], after=y_t)
    (x2,) = _matmul("mm_out", y_t, wout, "tn", [F32], epilogue=lambda acc, xr: (xr + acc,), extras=[xf])
    h2 = _rms_fwd("rms2_fwd", x2, norm2_g)
    (a,) = _matmul("mm_gate", h2, wg_t, "nt", [F32])
    b, s = _matmul("mm_up", h2, wu_t, "nt", [F32, MXU], epilogue=lambda acc, av: (acc, _silu_parts(av)[0] * acc), extras=[a])

    def loss_epilogue(acc, x2v, tv):
        diff = (x2v + acc) - tv
        dx3 = diff * (1.0 / d)
        return dx3, dx3, jnp.sum(diff * diff)

    dx3, dx3b, lossp = _matmul("mm_down", s, wd, "nn", [F32, MXU], epilogue=loss_epilogue, extras=[x2, tgt], partial=True)
    loss_part = (0.5 / d) * jnp.sum(lossp[::8, ::LANES])

    def dswiglu(acc, av, bv):
        silu, dsilu = _silu_parts(av)
        return acc * bv * dsilu, acc * silu

    da, db = _matmul("mm_d_down", dx3b, wd, "nt", [MXU, MXU], epilogue=dswiglu, extras=[a, b])
    (g_wd,) = _matmul("mm_gw_down", s, dx3b, "tn", [WIRE])
    sl_wd = _scatter_grad("scatter_w_ffn_down", 4, g_wd)
    (dh2a,) = _matmul("mm_dh2_gate", da, wg_t, "nn", [F32], after=[g_wd])
    (g_wg,) = _matmul("mm_gw_gate", da, h2, "tn", [WIRE], after=[dh2a])
    sl_wg = _scatter_grad("scatter_w_ffn_gate", 5, g_wg)
    (dh2,) = _matmul("mm_dh2_up", db, wu_t, "nn", [F32], epilogue=lambda acc, pv: (pv + acc,), extras=[dh2a], after=[g_wg])
    (g_wu,) = _matmul("mm_gw_up", db, h2, "tn", [WIRE], after=[dh2])
    sl_wu = _scatter_grad("scatter_w_ffn_up", 6, g_wu)
    dx2, dx2b, dg2 = _rms_bwd("rms2_bwd", dh2, x2, norm2_g, dx3)

    (dy_t,) = _matmul("mm_d_out", wout, dx2b, "nt", [F32], after=[g_wu])
    (g_wout,) = _matmul("mm_gw_out", y_t, dx2b, "nn", [WIRE], after=[dy_t])
    sl_wout = _scatter_grad("scatter_w_out", 7, g_wout)
    (dproj0, dkv, dgq, dgk, dsink, dlng, dlnb, dws, dbs, dgoa, dgog) = _mixer_bwd(proj_t, dy_t, prm, dm, nseq, nb)
    dproj_t = _patch_kv(dproj0, dkv, dm)
    (g_win,) = _matmul("mm_gw_in", dproj_t, h1, "nn", [WIRE])
    sl_win = _scatter_grad("scatter_w_in", 8, g_win)
    (dh1,) = _matmul("mm_d_in", dproj_t, win_t, "tn", [F32], after=[g_win])
    dx, _, dg1 = _rms_bwd("rms1_bwd", dh1, xf, norm1_g, dx2)

    big_out = {}
    last = dx
    for nm, sl in (("w_ffn_down", sl_wd), ("w_ffn_gate", sl_wg), ("w_ffn_up", sl_wu), ("w_out", sl_wout), ("w_in", sl_win)):
        res = _sum_adamw("adamw_" + nm, sl, *big_rows[nm], after=last)
        last = res[1]
        big_out[nm] = tuple(jnp.swapaxes(r[None], 1, 2) if big[nm][3] else r[None] for r in res)

    small_names = ["norm1_g", "q_norm_g", "k_norm_g", "attn_sinks", "gate_ln_g", "gate_ln_b", "w_spatial", "b_spatial",
                   "out_norm_attn_g", "out_norm_gate_g", "norm2_g"]
    small_g = [dg1, dgq, dgk, dsink, dlng, dlnb, dws, dbs, dgoa, dgog, dg2]
    small_w = [norm1_g, q_norm_g, k_norm_g, attn_sinks, gate_ln_g, gate_ln_b, w_spatial, b_spatial, out_norm_attn_g, out_norm_gate_g, norm2_g]
    small_m = [m_norm1_g, m_q_norm_g, m_k_norm_g, m_attn_sinks, m_gate_ln_g, m_gate_ln_b, m_w_spatial, m_b_spatial, m_out_norm_attn_g, m_out_norm_gate_g, m_norm2_g]
    small_v = [v_norm1_g, v_q_norm_g, v_k_norm_g, v_attn_sinks, v_gate_ln_g, v_gate_ln_b, v_w_spatial, v_b_spatial, v_out_norm_attn_g, v_out_norm_gate_g, v_norm2_g]
    zero = jnp.zeros((1,), F32)
    part, _ = _pack(small_g + [loss_part.reshape(1)])
    pw, meta = _pack(small_w + [zero])
    pm, _ = _pack(small_m + [zero])
    pv, _ = _pack(small_v + [zero])
    sg, sd, sm, sv = _allreduce_small_adamw(part, pw, pm, pv)
    ug, ud, um, uv = _unpack(sg, meta), _unpack(sd, meta), _unpack(sm, meta), _unpack(sv, meta)
    loss = ug[-1].reshape(())
    small_out = {nm: (ug[k], ud[k], um[k], uv[k]) for k, nm in enumerate(small_names)}

    order = ["norm1_g", "w_in", "q_norm_g", "k_norm_g", "attn_sinks", "gate_ln_g", "gate_ln_b", "w_spatial", "b_spatial",
             "out_norm_attn_g", "out_norm_gate_g", "w_out", "norm2_g", "w_ffn_gate", "w_ffn_up", "w_ffn_down"]
    allo = {**big_out, **small_out}
    outs = [loss, dx.reshape(nseq, seq, d)]
    for k in range(4):
        outs += [allo[nm][k] for nm in order]
    return tuple(outs)
```

```python
import math

import jax
import jax.numpy as jnp
from jax import lax
from jax.experimental import pallas as pl
from jax.experimental.pallas import tpu as pltpu
from jax.experimental.pallas import tpu_sc as plsc

F32 = jnp.float32
MXU = jnp.bfloat16
WIRE = jnp.bfloat16
EPS = 1e-6
BLOCK = 128
GROUP_DIM = 128
N_KV_HEADS = 2
NEG = -1e30
N_DEV = 8
LANES = 128
MIB = 1024 * 1024

ADAM_LR = 0.001
ADAM_B1 = 0.9
ADAM_B2 = 0.999
ADAM_EPS = 1e-08
ADAM_WD = 0.01
ADAM_STEP = 10

MESH = pl.DeviceIdType.MESH
ANY = pl.BlockSpec(memory_space=pl.ANY)


def _pick(n, cands):
    for c in cands:
        if n % c == 0:
            return c
    return n


def _cparams(sem, vmem_mb):
    return pltpu.CompilerParams(dimension_semantics=sem, vmem_limit_bytes=vmem_mb * MIB)


VMEM_TILE_BUDGET = 44 * MIB
HBM_BYTES_PER_US = 3.0e6
STEP_US = 0.4
MIN_TILE_N = 512


def _tile_candidates(n):
    return [c for c in range(min(n, 2048), 0, -LANES) if n % c == 0 and c % LANES == 0] or [n]


def _matmul_tiles(m, n, kk, esz, n_extras, out_sizes, full_rows):
    best = None
    wide = [n] if full_rows else [c for c in _tile_candidates(n) if c >= MIN_TILE_N] or _tile_candidates(n)
    for tm in _tile_candidates(m):
        for tn in wide:
            vmem = 2 * (tm + tn) * kk * esz + tm * tn * (4 + 2 * 4 * n_extras + 2 * sum(out_sizes))
            if vmem > VMEM_TILE_BUDGET:
                continue
            cost = (m // tm) * n * kk * esz / HBM_BYTES_PER_US + (m // tm) * (n // tn) * STEP_US
            if best is None or cost < best[0]:
                best = (cost, tm, tn, vmem)
    assert best is not None, (m, n, kk)
    return best[1:]


def _matmul(name, a, b, mode, out_dtypes, epilogue=None, extras=(), rowvecs=(), after=(), partial=False, col_sum=False, full_rows=False):
    if mode == "nn":
        (m, kk), n = a.shape, b.shape[1]
        dn = (((1,), (0,)), ((), ()))
    elif mode == "nt":
        (m, kk), n = a.shape, b.shape[0]
        dn = (((1,), (1,)), ((), ()))
    else:
        (kk, m), n = a.shape, b.shape[1]
        dn = (((0,), (0,)), ((), ()))
    tm, tn, vmem = _matmul_tiles(m, n, kk, a.dtype.itemsize, len(extras), [jnp.dtype(dt).itemsize for dt in out_dtypes], full_rows)
    a_spec = pl.BlockSpec((kk, tm), lambda i, j: (0, i)) if mode == "tn" else pl.BlockSpec((tm, kk), lambda i, j: (i, 0))
    b_spec = pl.BlockSpec((tn, kk), lambda i, j: (j, 0)) if mode == "nt" else pl.BlockSpec((kk, tn), lambda i, j: (0, j))
    tile = pl.BlockSpec((tm, tn), lambda i, j: (i, j))
    row = pl.BlockSpec((1, tn), lambda i, j: (0, j))
    ne, nr, na, no = len(extras), len(rowvecs), len(after), len(out_dtypes)

    def body(a_ref, b_ref, *rest):
        in_refs, out_refs = rest[:ne + nr], rest[ne + nr + na:]
        acc = lax.dot_general(a_ref[...], b_ref[...], dn, preferred_element_type=F32)
        vals = (acc,) if epilogue is None else epilogue(acc, *[r[...] for r in in_refs])
        for o_ref, t in zip(out_refs[:no], vals[:no]):
            o_ref[...] = t.astype(o_ref.dtype)
        if partial:
            out_refs[no][...] = jnp.full((8, LANES), vals[no], F32)
        if col_sum:
            sum_ref = out_refs[-1]

            @pl.when(pl.program_id(0) == 0)
            def _():
                sum_ref[...] = jnp.zeros_like(sum_ref)

            sum_ref[...] += vals[-1]

    out_specs = [tile] * no
    out_shape = [jax.ShapeDtypeStruct((m, n), dt) for dt in out_dtypes]
    if partial:
        out_specs.append(pl.BlockSpec((8, LANES), lambda i, j: (i, j)))
        out_shape.append(jax.ShapeDtypeStruct((m // tm * 8, n // tn * LANES), F32))
    if col_sum:
        out_specs.append(row)
        out_shape.append(jax.ShapeDtypeStruct((1, n), F32))
    return pl.pallas_call(
        body, name=name, grid=(m // tm, n // tn),
        in_specs=[a_spec, b_spec] + [tile] * ne + [row] * nr + [ANY] * na,
        out_specs=out_specs, out_shape=out_shape,
        compiler_params=_cparams(("arbitrary" if col_sum else "parallel", "arbitrary"), min(vmem // MIB + 8, 60)),
    )(a, b, *extras, *rowvecs, *after)


def _rms_fwd(name, x, g):
    t, d = x.shape
    tm = _pick(t, (512, 256, 128))

    def body(x_ref, g_ref, h_ref):
        xv = x_ref[...]
        r = lax.rsqrt(jnp.mean(xv * xv, axis=-1, keepdims=True) + EPS)
        h_ref[...] = (xv * r * g_ref[...]).astype(h_ref.dtype)

    return pl.pallas_call(
        body, name=name, grid=(t // tm,),
        in_specs=[pl.BlockSpec((tm, d), lambda i: (i, 0)), pl.BlockSpec((1, d), lambda i: (0, 0))],
        out_specs=pl.BlockSpec((tm, d), lambda i: (i, 0)),
        out_shape=jax.ShapeDtypeStruct((t, d), MXU),
        compiler_params=_cparams(("parallel",), 32),
    )(x, g)


def _rms_bwd(name, dh, x, g, res):
    t, d = x.shape
    tm = _pick(t, (256, 128))

    def body(dh_ref, x_ref, g_ref, res_ref, dx_ref, dxb_ref, dg_ref):
        @pl.when(pl.program_id(0) == 0)
        def _():
            dg_ref[...] = jnp.zeros_like(dg_ref)

        xv, dhv = x_ref[...], dh_ref[...]
        r = lax.rsqrt(jnp.mean(xv * xv, axis=-1, keepdims=True) + EPS)
        xh = xv * r
        dg_ref[...] += jnp.sum(dhv * xh, axis=0, keepdims=True)
        dxh = dhv * g_ref[...]
        dx = res_ref[...] + r * (dxh - xh * jnp.mean(dxh * xh, axis=-1, keepdims=True))
        dx_ref[...] = dx
        dxb_ref[...] = dx.astype(dxb_ref.dtype)

    row = pl.BlockSpec((tm, d), lambda i: (i, 0))
    vec = pl.BlockSpec((1, d), lambda i: (0, 0))
    return pl.pallas_call(
        body, name=name, grid=(t // tm,),
        in_specs=[row, row, vec, row],
        out_specs=[row, row, vec],
        out_shape=[jax.ShapeDtypeStruct((t, d), F32), jax.ShapeDtypeStruct((t, d), MXU), jax.ShapeDtypeStruct((1, d), F32)],
        compiler_params=_cparams(("arbitrary",), 40),
    )(dh, x, g, res)


_INV_SQRT2 = 0.7071067811865476
_INV_SQRT_2PI = 0.3989422804014327


def _dot_nt(a, b):
    return lax.dot_general(a, b, (((1,), (1,)), ((), ())), preferred_element_type=F32)


def _dot_tn(a, b):
    return lax.dot_general(a, b, (((0,), (0,)), ((), ())), preferred_element_type=F32)


def _dot(a, b):
    return jnp.dot(a, b, preferred_element_type=F32)


def _col_rms(v):
    return lax.rsqrt(jnp.mean(v * v, axis=0, keepdims=True) + EPS)


class _Dims:
    def __init__(self, d_model, in_width, head_dim):
        self.d = d_model
        self.aw = d_model // 2
        self.gw = d_model - self.aw
        self.kvw = (in_width - self.aw - 2 * self.gw) // 2
        self.hd = head_dim
        self.nh = self.aw // head_dim
        self.nkv = self.kvw // head_dim
        self.grp = self.nh // self.nkv
        self.ng = self.gw // GROUP_DIM
        self.inw = in_width
        self.zoff = self.aw + 2 * self.kvw
        assert self.nkv == N_KV_HEADS and self.zoff + 2 * self.gw == in_width and self.aw % (2 * self.kvw) == 0


def _band_masks(first, grp):
    kj = lax.broadcasted_iota(jnp.int32, (2 * BLOCK, grp * BLOCK), 0)
    qi = lax.broadcasted_iota(jnp.int32, (2 * BLOCK, grp * BLOCK), 1) & (BLOCK - 1)
    dist = qi + BLOCK - kj
    valid = (dist >= 0) & (dist < BLOCK) & ((kj >= BLOCK) | jnp.logical_not(first))
    return valid, dist.astype(F32)


def _kv_band(dm, kh, p_ref, pkv_ref, gk2):
    ko = dm.aw + kh * dm.hd
    vo = dm.aw + dm.kvw + kh * dm.hd
    k_t = jnp.concatenate([pkv_ref[kh * dm.hd:(kh + 1) * dm.hd, :], p_ref[ko:ko + dm.hd, :]], axis=1)
    v_t = jnp.concatenate([pkv_ref[dm.kvw + kh * dm.hd:dm.kvw + (kh + 1) * dm.hd, :], p_ref[vo:vo + dm.hd, :]], axis=1)
    kn_t = k_t * _col_rms(k_t) * gk2
    return kn_t.astype(MXU), kn_t.T.astype(MXU), v_t.astype(MXU), v_t.T.astype(MXU)


def _group_heads(dm, kh):
    return range(kh * dm.grp, (kh + 1) * dm.grp)


def _lane_row(vals):
    return jnp.concatenate([jnp.full((1, BLOCK), v, F32) for v in vals], axis=1)


def _attn_group_fwd(dm, kh, p_ref, gq, kn, v_tb, sink_ref, valid, dist):
    heads = _group_heads(dm, kh)
    q = jnp.concatenate([p_ref[h * dm.hd:(h + 1) * dm.hd, :] for h in heads], axis=1)
    rq = _col_rms(q)
    qh = q * rq
    qnb = (qh * gq).astype(MXU)
    slopes = _lane_row([math.pow(2.0, -8.0 * (h + 1) / dm.nh) for h in heads])
    sinks = _lane_row([sink_ref[h] for h in heads])
    s = _dot(kn, qnb) * (dm.hd ** -0.5)
    logits = jnp.where(valid, s - slopes * dist, NEG)
    m = jnp.maximum(jnp.max(logits, axis=0, keepdims=True), sinks)
    e = jnp.exp(logits - m)
    es = jnp.exp(sinks - m)
    inv = 1.0 / (jnp.sum(e, axis=0, keepdims=True) + es)
    p = e * inv
    o = _dot(v_tb, p.astype(MXU))
    return o, p, es * inv, rq, qh, qnb


def _gelu_cdf(z):
    return 0.5 * (1.0 + lax.erf(z * _INV_SQRT2))


def _by_group(v, ng):
    return v.reshape(ng, GROUP_DIM, v.shape[1])


def _gate_fwd(dm, p_ref, lng_ref, lnb_ref, ws_ref, bs_ref, tril):
    zu, zv = p_ref[dm.zoff:dm.zoff + dm.gw, :], p_ref[dm.zoff + dm.gw:dm.zoff + 2 * dm.gw, :]
    u, v = zu * _gelu_cdf(zu), zv * _gelu_cdf(zv)
    v3 = _by_group(v, dm.ng)
    xc = v3 - jnp.mean(v3, axis=1, keepdims=True)
    rstd = lax.rsqrt(jnp.mean(xc * xc, axis=1, keepdims=True) + EPS)
    xh = (xc * rstd).reshape(dm.gw, BLOCK)
    vnb = (xh * lng_ref[...] + lnb_ref[...]).astype(MXU)
    wts = [jnp.where(tril, ws_ref[g], 0.0).astype(MXU) for g in range(dm.ng)]
    mixed = jnp.concatenate([_dot_nt(vnb[g * GROUP_DIM:(g + 1) * GROUP_DIM], wts[g]) + bs_ref[g:g + 1, :]
                             for g in range(dm.ng)], axis=0)
    return u * mixed, u, mixed, xh, rstd, vnb, wts, zu, zv


def _mixer_specs(dm, nb, clamp):
    kvblk = dm.aw // (2 * dm.kvw)

    def cur(s, i):
        return (0, s * nb + clamp(i))

    def prev(s, i):
        return (kvblk, s * nb + jnp.maximum(clamp(i) - 1, 0))

    full = lambda shape: pl.BlockSpec(shape, lambda s, i: tuple(0 for _ in shape))
    return cur, prev, full


def _tril():
    return lax.broadcasted_iota(jnp.int32, (BLOCK, BLOCK), 0) >= lax.broadcasted_iota(jnp.int32, (BLOCK, BLOCK), 1)


def _mixer_fwd(proj_t, prm, dm, nseq, nb):
    gq, gk2, sinks, lng, lnb, ws, bs, goa, gog = prm
    t = proj_t.shape[1]
    cur, prev, full = _mixer_specs(dm, nb, lambda i: i)

    def body(p_ref, pkv_ref, gq_ref, gk_ref, sink_ref, lng_ref, lnb_ref, ws_ref, bs_ref, goa_ref, gog_ref, y_ref, att_scr):
        i = pl.program_id(1)
        valid, dist = _band_masks(i == 0, dm.grp)
        gqv, gkv = gq_ref[...], gk_ref[...]
        for kh in range(dm.nkv):
            _, kn, v_tb, _ = _kv_band(dm, kh, p_ref, pkv_ref, gkv)
            o = _attn_group_fwd(dm, kh, p_ref, gqv, kn, v_tb, sink_ref, valid, dist)[0]
            for g, h in enumerate(_group_heads(dm, kh)):
                att_scr[h * dm.hd:(h + 1) * dm.hd, :] = o[:, g * BLOCK:(g + 1) * BLOCK]
        att = att_scr[...]
        y_ref[:dm.aw, :] = (att * _col_rms(att) * goa_ref[...]).astype(y_ref.dtype)
        gt = _gate_fwd(dm, p_ref, lng_ref, lnb_ref, ws_ref, bs_ref, _tril())[0]
        y_ref[dm.aw:, :] = (gt * _col_rms(gt) * gog_ref[...]).astype(y_ref.dtype)

    return pl.pallas_call(
        body, name="mixer_fwd", grid=(nseq, nb),
        in_specs=[pl.BlockSpec((dm.inw, BLOCK), cur), pl.BlockSpec((2 * dm.kvw, BLOCK), prev),
                  full(gq.shape), full(gk2.shape), pl.BlockSpec(memory_space=pltpu.SMEM),
                  full(lng.shape), full(lnb.shape), full(ws.shape), full(bs.shape), full(goa.shape), full(gog.shape)],
        out_specs=pl.BlockSpec((dm.d, BLOCK), cur),
        out_shape=jax.ShapeDtypeStruct((dm.d, t), MXU),
        scratch_shapes=[pltpu.VMEM((dm.aw, BLOCK), F32)],
        compiler_params=_cparams(("parallel", "arbitrary"), 40),
    )(proj_t, proj_t, gq, gk2, sinks, lng, lnb, ws, bs, goa, gog)


def _mixer_bwd(proj_t, dy_t, prm, dm, nseq, nb):
    gq, gk2, sinks, lng, lnb, ws, bs, goa, gog = prm
    t = proj_t.shape[1]
    clamp = lambda i: jnp.minimum(i, nb - 1)
    cur, prev, full = _mixer_specs(dm, nb, clamp)
    kvw2 = 2 * dm.kvw

    def prev_kv_out(s, i):
        return (0, s * nb + jnp.maximum(i - 1, 0))

    def body(p_ref, pkv_ref, dy_ref, gq_ref, gk_ref, sink_ref, lng_ref, lnb_ref, ws_ref, bs_ref, goa_ref, gog_ref,
             dproj_ref, dkv_ref, dgq_ref, dgk_ref, dsink_ref, dlng_ref, dlnb_ref, dws_ref, dbs_ref, dgoa_ref, dgog_ref,
             att_scr, datt_scr, carry_scr, prevpart_scr, curpart_scr, kprev_scr,
             a_gq, a_gk, a_sink, a_lng, a_lnb, a_goa, a_gog):
        s_id, i = pl.program_id(0), pl.program_id(1)
        lane_accs = ((a_gq, dgq_ref), (a_gk, dgk_ref), (a_sink, dsink_ref), (a_lng, dlng_ref), (a_lnb, dlnb_ref),
                     (a_goa, dgoa_ref), (a_gog, dgog_ref))

        @pl.when((s_id == 0) & (i == 0))
        def _():
            for acc, _ in lane_accs:
                acc[...] = jnp.zeros_like(acc)
            dws_ref[...] = jnp.zeros_like(dws_ref)
            dbs_ref[...] = jnp.zeros_like(dbs_ref)

        gqv, gkv = gq_ref[...], gk_ref[...]

        @pl.when(i < nb)
        def _():
            valid, dist = _band_masks(i == 0, dm.grp)
            kvs, fwd = [], []
            for kh in range(dm.nkv):
                kv = _kv_band(dm, kh, p_ref, pkv_ref, gkv)
                kvs.append(kv)
                fwd.append(_attn_group_fwd(dm, kh, p_ref, gqv, kv[1], kv[2], sink_ref, valid, dist))
                for g, h in enumerate(_group_heads(dm, kh)):
                    att_scr[h * dm.hd:(h + 1) * dm.hd, :] = fwd[kh][0][:, g * BLOCK:(g + 1) * BLOCK]
            att = att_scr[...]
            dya = dy_ref[:dm.aw, :]
            ra = _col_rms(att)
            ah = att * ra
            a_goa[...] += dya * ah
            dah = dya * goa_ref[...]
            datt_scr[...] = ra * (dah - ah * jnp.mean(dah * ah, axis=0, keepdims=True))
            for kh in range(dm.nkv):
                kn_tb, kn, v_tb, vb = kvs[kh]
                _, p, ps, rq, qh, qnb = fwd[kh]
                heads = _group_heads(dm, kh)
                do_b = jnp.concatenate([datt_scr[h * dm.hd:(h + 1) * dm.hd, :] for h in heads], axis=1).astype(MXU)
                dp = _dot(vb, do_b)
                delta = jnp.sum(p * dp, axis=0, keepdims=True)
                dsb = (p * (dp - delta) * (dm.hd ** -0.5)).astype(MXU)
                dsink = ps * delta
                dqn = _dot(kn_tb, dsb)
                dkn = _dot_nt(qnb, dsb)
                dvb = _dot_nt(do_b, p.astype(MXU))
                a_gq[...] += dqn * qh
                dqh = dqn * gqv
                dq = rq * (dqh - qh * jnp.mean(dqh * qh, axis=0, keepdims=True))
                for g, h in enumerate(heads):
                    a_sink[h:h + 1, :] += -dsink[:, g * BLOCK:(g + 1) * BLOCK]
                    dproj_ref[h * dm.hd:(h + 1) * dm.hd, :] = dq[:, g * BLOCK:(g + 1) * BLOCK].astype(dproj_ref.dtype)
                krows = slice(kh * dm.hd, (kh + 1) * dm.hd)
                vrows = slice(dm.kvw + kh * dm.hd, dm.kvw + (kh + 1) * dm.hd)
                prevpart_scr[krows, :] = dkn[:, :BLOCK]
                prevpart_scr[vrows, :] = dvb[:, :BLOCK]
                curpart_scr[krows, :] = dkn[:, BLOCK:]
                curpart_scr[vrows, :] = dvb[:, BLOCK:]
            dproj_ref[dm.aw:dm.zoff, :] = jnp.zeros((kvw2, BLOCK), dproj_ref.dtype)
            tril = _tril()
            gt, u, mixed, xh, rstd, vnb, wts, zu, zv = _gate_fwd(dm, p_ref, lng_ref, lnb_ref, ws_ref, bs_ref, tril)
            dyg = dy_ref[dm.aw:, :]
            rg = _col_rms(gt)
            gh = gt * rg
            a_gog[...] += dyg * gh
            dgh = dyg * gog_ref[...]
            dgt = rg * (dgh - gh * jnp.mean(dgh * gh, axis=0, keepdims=True))
            du = dgt * mixed
            dmix = dgt * u
            dmixb = dmix.astype(MXU)
            dbs_ref[...] += jnp.sum(_by_group(dmix, dm.ng), axis=1)
            dvn = []
            for g in range(dm.ng):
                rows = slice(g * GROUP_DIM, (g + 1) * GROUP_DIM)
                dws_ref[g] += jnp.where(tril, _dot_tn(dmixb[rows], vnb[rows]), 0.0)
                dvn.append(_dot(dmixb[rows], wts[g]))
            dvn = jnp.concatenate(dvn, axis=0)
            a_lng[...] += dvn * xh
            a_lnb[...] += dvn
            dxh3, xh3 = _by_group(dvn * lng_ref[...], dm.ng), _by_group(xh, dm.ng)
            dv = (rstd * (dxh3 - jnp.mean(dxh3, axis=1, keepdims=True) - xh3 * jnp.mean(dxh3 * xh3, axis=1, keepdims=True))).reshape(dm.gw, BLOCK)
            dgu = _gelu_cdf(zu) + zu * (jnp.exp(-0.5 * zu * zu) * _INV_SQRT_2PI)
            dgv = _gelu_cdf(zv) + zv * (jnp.exp(-0.5 * zv * zv) * _INV_SQRT_2PI)
            dproj_ref[dm.zoff:dm.zoff + dm.gw, :] = (du * dgu).astype(dproj_ref.dtype)
            dproj_ref[dm.zoff + dm.gw:, :] = (dv * dgv).astype(dproj_ref.dtype)

        @pl.when(i == nb)
        def _():
            prevpart_scr[...] = jnp.zeros_like(prevpart_scr)

        @pl.when(i >= 1)
        def _():
            tot = carry_scr[...] + prevpart_scr[...]
            for kh in range(dm.nkv):
                krows = slice(kh * dm.hd, (kh + 1) * dm.hd)
                kraw = kprev_scr[krows, :]
                rk = _col_rms(kraw)
                khat = kraw * rk
                dkn = tot[krows, :]
                a_gk[...] += dkn * khat
                dkh = dkn * gkv[:, :BLOCK]
                dk = rk * (dkh - khat * jnp.mean(dkh * khat, axis=0, keepdims=True))
                dkv_ref[krows, :] = dk.astype(dkv_ref.dtype)
            dkv_ref[dm.kvw:, :] = tot[dm.kvw:, :].astype(dkv_ref.dtype)

        @pl.when(i < nb)
        def _():
            carry_scr[...] = curpart_scr[...]
            kprev_scr[...] = p_ref[dm.aw:dm.aw + dm.kvw, :]

        @pl.when((s_id == nseq - 1) & (i == nb))
        def _():
            for acc, out in lane_accs:
                out[...] = jnp.sum(acc[...], axis=1, keepdims=True)

    col = lambda rows: jax.ShapeDtypeStruct((rows, 1), F32)
    lane = lambda rows: pltpu.VMEM((rows, LANES), F32)
    return pl.pallas_call(
        body, name="mixer_bwd", grid=(nseq, nb + 1),
        in_specs=[pl.BlockSpec((dm.inw, BLOCK), cur), pl.BlockSpec((kvw2, BLOCK), prev), pl.BlockSpec((dm.d, BLOCK), cur),
                  full(gq.shape), full(gk2.shape), pl.BlockSpec(memory_space=pltpu.SMEM),
                  full(lng.shape), full(lnb.shape), full(ws.shape), full(bs.shape), full(goa.shape), full(gog.shape)],
        out_specs=[pl.BlockSpec((dm.inw, BLOCK), cur), pl.BlockSpec((kvw2, BLOCK), prev_kv_out),
                   full((dm.hd, 1)), full((dm.hd, 1)), full((dm.nh, 1)), full((dm.gw, 1)), full((dm.gw, 1)), full(ws.shape),
                   full(bs.shape), full((dm.aw, 1)), full((dm.gw, 1))],
        out_shape=[jax.ShapeDtypeStruct((dm.inw, t), MXU), jax.ShapeDtypeStruct((kvw2, t), MXU),
                   col(dm.hd), col(dm.hd), col(dm.nh), col(dm.gw), col(dm.gw), jax.ShapeDtypeStruct(ws.shape, F32),
                   jax.ShapeDtypeStruct(bs.shape, F32), col(dm.aw), col(dm.gw)],
        scratch_shapes=[pltpu.VMEM((dm.aw, BLOCK), F32), pltpu.VMEM((dm.aw, BLOCK), F32),
                        pltpu.VMEM((kvw2, BLOCK), F32), pltpu.VMEM((kvw2, BLOCK), F32), pltpu.VMEM((kvw2, BLOCK), F32),
                        pltpu.VMEM((dm.kvw, BLOCK), F32),
                        pltpu.VMEM((dm.hd, dm.grp * BLOCK), F32), lane(dm.hd), lane(dm.nh), lane(dm.gw), lane(dm.gw), lane(dm.aw), lane(dm.gw)],
        compiler_params=_cparams(("arbitrary", "arbitrary"), 48),
    )(proj_t, proj_t, dy_t, gq, gk2, sinks, lng, lnb, ws, bs, goa, gog)


def _patch_kv(dproj_t, dkv_t, dm):
    t = dproj_t.shape[1]
    tc = _pick(t, (1024, 512, 256, 128))
    kvw2 = 2 * dm.kvw
    kvblk = dm.aw // kvw2

    def body(dproj_hbm, dkv_ref, out_ref):
        del dproj_hbm
        out_ref[...] = dkv_ref[...]

    return pl.pallas_call(
        body, name="patch_kv", grid=(t // tc,),
        in_specs=[ANY, pl.BlockSpec((kvw2, tc), lambda i: (0, i))],
        out_specs=pl.BlockSpec((kvw2, tc), lambda i: (kvblk, i)),
        out_shape=jax.ShapeDtypeStruct(dproj_t.shape, dproj_t.dtype),
        input_output_aliases={0: 0},
        compiler_params=_cparams(("parallel",), 32),
    )(dproj_t, dkv_t)


def _place():
    x, y, c = lax.axis_index("x"), lax.axis_index("y"), lax.axis_index("c")
    return x, y, c


def _handshake(peers):
    barrier = pltpu.get_barrier_semaphore()
    for p in peers:
        pl.semaphore_signal(barrier, inc=1, device_id=p, device_id_type=MESH)
    pl.semaphore_wait(barrier, len(peers))


def _sequencer_mesh():
    return plsc.ScalarSubcoreMesh(axis_name="sequencer", num_cores=1)


def _allgather_weights(name, collective_id, shards, after=None):
    nw = len(shards)

    def body(*refs):
        src, out = refs[:nw], refs[-3 - nw:-3]
        send_sems, recv_sems, local_sems = refs[-3:]
        x, y, c = _place()
        me, sibling = (x, y, c), (x, y, 1 - c)
        chips = [(1 - x, y), (x, 1 - y), (1 - x, 1 - y)]
        _handshake([sibling] + [(*chip, c) for chip in chips])

        def rows(w, place):
            n = src[w].shape[0]
            px, py, pc = place
            return out[w].at[pl.ds(pl.multiple_of((4 * px + 2 * py + pc) * n, 16), n), :]

        def copy(w, k, block, to, from_src=False):
            return pltpu.make_async_remote_copy(
                src_ref=src[w] if from_src else rows(w, block), dst_ref=rows(w, block),
                send_sem=send_sems.at[w, k], recv_sem=recv_sems.at[w, k], device_id=to, device_id_type=MESH)

        mine = [pltpu.make_async_copy(src[w], rows(w, me), local_sems.at[w]) for w in range(nw)]
        for cp in mine:
            cp.start()
        first = []
        for w in range(nw):
            first.append(copy(w, 0, me, sibling, from_src=True))
            first += [copy(w, 1 + j, me, (*chip, c), from_src=True) for j, chip in enumerate(chips)]
        for cp in first:
            cp.start()
        passed = []
        for w in range(nw):
            for j, chip in enumerate(chips):
                copy(w, 1 + j, (*chip, c), me).wait_recv()
                fwd = copy(w, 4 + j, (*chip, c), sibling)
                fwd.start()
                passed.append(fwd)
        for w in range(nw):
            copy(w, 0, sibling, me).wait_recv()
            for j, chip in enumerate(chips):
                copy(w, 4 + j, (*chip, 1 - c), me).wait_recv()
        for cp in first + passed:
            cp.wait_send()
        for cp in mine:
            cp.wait()

    return pl.kernel(
        body, name=name,
        out_type=[jax.ShapeDtypeStruct((N_DEV * s.shape[0], s.shape[1]), s.dtype) for s in shards],
        mesh=_sequencer_mesh(),
        scratch_types=[pltpu.SemaphoreType.DMA((nw, 7)), pltpu.SemaphoreType.DMA((nw, 7)), pltpu.SemaphoreType.DMA((nw,))],
        compiler_params=pltpu.CompilerParams(collective_id=collective_id),
    )(*shards, *([] if after is None else [after]))


_FLIPS = [(0, 0, 1), (1, 0, 0), (0, 1, 0), (1, 1, 0), (1, 0, 1), (0, 1, 1), (1, 1, 1)]


def _scatter_grad(name, collective_id, grad, rows=None):
    n = grad.shape[0] // N_DEV
    with_rows = rows is not None

    def body(*refs):
        src, out = refs[0], refs[2 if with_rows else 1]
        send_sems, recv_sems, local_sems = refs[-3:]
        x, y, c = _place()
        me_idx = 4 * x + 2 * y + c
        peers = [(x ^ fx, y ^ fy, c ^ fc) for (fx, fy, fc) in _FLIPS]
        _handshake(peers)

        def block(idx):
            return src.at[pl.ds(pl.multiple_of(idx * n, 16), n), :]

        copies = [pltpu.make_async_remote_copy(
            src_ref=block(4 * px + 2 * py + pc), dst_ref=out.at[me_idx], send_sem=send_sems.at[k], recv_sem=recv_sems.at[k],
            device_id=(px, py, pc), device_id_type=MESH) for k, (px, py, pc) in enumerate(peers)]
        mine = [pltpu.make_async_copy(block(me_idx), out.at[me_idx], local_sems.at[0])]
        if with_rows:
            rows_src, rows_out = refs[1], refs[3]
            copies += [pltpu.make_async_remote_copy(
                src_ref=rows_src, dst_ref=rows_out.at[me_idx], send_sem=send_sems.at[7 + k], recv_sem=recv_sems.at[7 + k],
                device_id=peer, device_id_type=MESH) for k, peer in enumerate(peers)]
            mine.append(pltpu.make_async_copy(rows_src, rows_out.at[me_idx], local_sems.at[1]))
        for cp in mine + copies:
            cp.start()
        for cp in copies:
            cp.wait_recv()
        for cp in copies:
            cp.wait_send()
        for cp in mine:
            cp.wait()

    slots = jax.ShapeDtypeStruct((N_DEV, n, grad.shape[1]), grad.dtype)
    return pl.kernel(
        body, name=name,
        out_type=[slots, jax.ShapeDtypeStruct((N_DEV,) + rows.shape, rows.dtype)] if with_rows else slots,
        mesh=_sequencer_mesh(),
        scratch_types=[pltpu.SemaphoreType.DMA((14,)), pltpu.SemaphoreType.DMA((14,)), pltpu.SemaphoreType.DMA((2,))],
        compiler_params=pltpu.CompilerParams(collective_id=collective_id),
    )(*([grad, rows] if with_rows else [grad]))


def _adamw_math(w, g, m, v):
    m = ADAM_B1 * m + (1.0 - ADAM_B1) * g
    v = ADAM_B2 * v + (1.0 - ADAM_B2) * (g * g)
    m_hat = m / (1.0 - ADAM_B1 ** ADAM_STEP)
    v_hat = v / (1.0 - ADAM_B2 ** ADAM_STEP)
    delta = -ADAM_LR * (m_hat / (jnp.sqrt(v_hat) + ADAM_EPS) + ADAM_WD * w)
    return delta, m, v


def _sum_adamw(name, slots, w, m, v, after):
    _, n, kk = slots.shape
    tr = _pick(n, (208, 176, 128, 96, 64, 32, 16))

    def body(s_ref, w_ref, m_ref, v_ref, after_ref, g_ref, d_ref, nm_ref, nv_ref):
        del after_ref
        g = s_ref[0].astype(F32)
        for p in range(1, N_DEV):
            g = g + s_ref[p].astype(F32)
        g_ref[...] = g
        d_ref[...], nm_ref[...], nv_ref[...] = _adamw_math(w_ref[...], g, m_ref[...], v_ref[...])

    row = pl.BlockSpec((tr, kk), lambda i: (i, 0))
    return pl.pallas_call(
        body, name=name, grid=(n // tr,),
        in_specs=[pl.BlockSpec((N_DEV, tr, kk), lambda i: (0, i, 0)), row, row, row, ANY],
        out_specs=[row] * 4,
        out_shape=[jax.ShapeDtypeStruct((n, kk), F32)] * 4,
        compiler_params=_cparams(("parallel",), 48),
    )(slots, w, m, v, after)


def _allreduce_small_adamw(early_slots, late, w, m, v, after):
    ra, rb = early_slots.shape[1], late.shape[0]

    def body(early_ref, late_ref, w_ref, m_ref, v_ref, after_ref, g_ref, d_ref, nm_ref, nv_ref, slots, send_sems, recv_sems):
        del after_ref
        x, y, c = _place()
        me_idx = 4 * x + 2 * y + c
        copies = []
        for k, (fx, fy, fc) in enumerate(_FLIPS):
            px, py, pc = x ^ fx, y ^ fy, c ^ fc
            copies.append(pltpu.make_async_remote_copy(
                src_ref=late_ref, dst_ref=slots.at[me_idx], send_sem=send_sems.at[k], recv_sem=recv_sems.at[k],
                device_id=(px, py, pc), device_id_type=MESH))
        for cp in copies:
            cp.start()
        slots[me_idx] = late_ref[...]
        g = early_ref[0]
        for p in range(1, N_DEV):
            g = g + early_ref[p]
        early = pl.ds(0, ra)
        g_ref[early, :] = g
        d_ref[early, :], nm_ref[early, :], nv_ref[early, :] = _adamw_math(w_ref[early, :], g, m_ref[early, :], v_ref[early, :])
        for cp in copies:
            cp.wait_recv()
        for cp in copies:
            cp.wait_send()
        g = slots[0]
        for p in range(1, N_DEV):
            g = g + slots[p]
        tail = pl.ds(ra, rb)
        g_ref[tail, :] = g
        d_ref[tail, :], nm_ref[tail, :], nv_ref[tail, :] = _adamw_math(w_ref[tail, :], g, m_ref[tail, :], v_ref[tail, :])

    vm = pl.BlockSpec(memory_space=pltpu.VMEM)
    return pl.pallas_call(
        body, name="allreduce_small_adamw",
        in_specs=[vm] * 5 + [ANY], out_specs=[vm] * 4,
        out_shape=[jax.ShapeDtypeStruct((ra + rb, LANES), F32)] * 4,
        scratch_shapes=[pltpu.VMEM((N_DEV, rb, LANES), F32), pltpu.SemaphoreType.DMA((7,)), pltpu.SemaphoreType.DMA((7,))],
        compiler_params=pltpu.CompilerParams(vmem_limit_bytes=48 * MIB),
    )(early_slots, late, w, m, v, after)


def _pack(arrs):
    parts, meta, off = [], [], 0
    for a in arrs:
        flat = a.reshape(-1).astype(F32)
        rows = -(-flat.shape[0] // LANES)
        rows8 = -(-rows // 8) * 8
        flat = jnp.pad(flat, (0, rows8 * LANES - flat.shape[0]))
        parts.append(flat.reshape(rows8, LANES))
        meta.append((off, a.shape, a.size))
        off += rows8
    return jnp.concatenate(parts, axis=0), meta


def _unpack(packed, meta):
    outs = []
    for off, shape, size in meta:
        rows = -(-size // LANES)
        outs.append(packed[off:off + rows].reshape(-1)[:size].reshape(shape))
    return outs


def _silu_parts(a):
    sg = 0.5 + 0.5 * jnp.tanh(0.5 * a)
    return a * sg, sg * (1.0 + a * (1.0 - sg))


def kernel(x, norm1_g, w_in, q_norm_g, k_norm_g, attn_sinks, gate_ln_g, gate_ln_b, w_spatial, b_spatial, out_norm_attn_g, out_norm_gate_g, w_out, norm2_g, w_ffn_gate, w_ffn_up, w_ffn_down, loss_target, m_norm1_g, m_w_in, m_q_norm_g, m_k_norm_g, m_attn_sinks, m_gate_ln_g, m_gate_ln_b, m_w_spatial, m_b_spatial, m_out_norm_attn_g, m_out_norm_gate_g, m_w_out, m_norm2_g, m_w_ffn_gate, m_w_ffn_up, m_w_ffn_down, v_norm1_g, v_w_in, v_q_norm_g, v_k_norm_g, v_attn_sinks, v_gate_ln_g, v_gate_ln_b, v_w_spatial, v_b_spatial, v_out_norm_attn_g, v_out_norm_gate_g, v_w_out, v_norm2_g, v_w_ffn_gate, v_w_ffn_up, v_w_ffn_down):
    nseq, seq, d = x.shape
    t = nseq * seq
    nb = seq // BLOCK
    inw = w_in.shape[2] * N_DEV
    dm = _Dims(d, inw, q_norm_g.shape[-1])
    xf = x.reshape(t, d)
    tgt = loss_target.reshape(t, d)

    rows = lambda wv, transposed: jnp.swapaxes(wv, 1, 2)[0] if transposed else wv[0]
    big = {"w_in": (w_in, m_w_in, v_w_in, True), "w_out": (w_out, m_w_out, v_w_out, False),
           "w_ffn_gate": (w_ffn_gate, m_w_ffn_gate, v_w_ffn_gate, True), "w_ffn_up": (w_ffn_up, m_w_ffn_up, v_w_ffn_up, True),
           "w_ffn_down": (w_ffn_down, m_w_ffn_down, v_w_ffn_down, False)}
    big_rows = {nm: tuple(rows(arr, tr) for arr in (wv, mv, vv)) for nm, (wv, mv, vv, tr) in big.items()}
    shard = lambda nm: big_rows[nm][0].astype(WIRE)
    (win_t,) = _allgather_weights("gather_w_in", 1, [shard("w_in")])
    (wout,) = _allgather_weights("gather_w_out", 2, [shard("w_out")], after=win_t)
    (wg_t,) = _allgather_weights("gather_w_ffn_gate", 3, [shard("w_ffn_gate")], after=win_t)
    (wu_t,) = _allgather_weights("gather_w_ffn_up", 9, [shard("w_ffn_up")], after=win_t)
    (wd,) = _allgather_weights("gather_w_ffn_down", 10, [shard("w_ffn_down")], after=win_t)

    lanes = lambda v, n=BLOCK: jnp.broadcast_to(v.reshape(-1, 1), (v.size, n))
    prm = (lanes(q_norm_g, dm.grp * BLOCK), lanes(k_norm_g, 2 * BLOCK), attn_sinks[0], lanes(gate_ln_g), lanes(gate_ln_b), w_spatial[0], b_spatial[0],
           lanes(out_norm_attn_g), lanes(out_norm_gate_g))

    h1 = _rms_fwd("rms1_fwd", xf, norm1_g)
    (proj_t,) = _matmul("mm_in", win_t, h1, "nt", [F32])
    y_t = _mixer_fwd(proj_t, prm, dm, nseq, nb)

    def residual_norm(acc, xr, g2):
        x2v = xr + acc
        return x2v, x2v * lax.rsqrt(jnp.mean(x2v * x2v, axis=-1, keepdims=True) + EPS) * g2

    x2, h2 = _matmul("mm_out", y_t, wout, "tn", [F32, MXU], epilogue=residual_norm, extras=[xf], rowvecs=[norm2_g], full_rows=True)
    (a,) = _matmul("mm_gate", h2, wg_t, "nt", [F32])
    b, s = _matmul("mm_up", h2, wu_t, "nt", [F32, MXU], epilogue=lambda acc, av: (acc, _silu_parts(av)[0] * acc), extras=[a])

    def loss_epilogue(acc, x2v, tv):
        diff = (x2v + acc) - tv
        dx3 = diff * (1.0 / d)
        return dx3, dx3, jnp.sum(diff * diff)

    dx3, dx3b, lossp = _matmul("mm_down", s, wd, "nn", [F32, MXU], epilogue=loss_epilogue, extras=[x2, tgt], partial=True)
    loss_part = (0.5 / d) * jnp.sum(lossp[::8, ::LANES])

    def dswiglu(acc, av, bv):
        silu, dsilu = _silu_parts(av)
        return acc * bv * dsilu, acc * silu

    da, db = _matmul("mm_d_down", dx3b, wd, "nt", [MXU, MXU], epilogue=dswiglu, extras=[a, b])
    (g_wd,) = _matmul("mm_gw_down", s, dx3b, "tn", [WIRE])
    sl_wd = _scatter_grad("scatter_w_ffn_down", 4, g_wd)
    (dh2a,) = _matmul("mm_dh2_gate", da, wg_t, "nn", [F32], after=[g_wd])
    (g_wg,) = _matmul("mm_gw_gate", da, h2, "tn", [WIRE], after=[dh2a])
    sl_wg = _scatter_grad("scatter_w_ffn_gate", 5, g_wg)
    (dh2,) = _matmul("mm_dh2_up", db, wu_t, "nn", [F32], epilogue=lambda acc, pv: (pv + acc,), extras=[dh2a], after=[g_wg])
    (g_wu,) = _matmul("mm_gw_up", db, h2, "tn", [WIRE], after=[dh2])
    sl_wu = _scatter_grad("scatter_w_ffn_up", 6, g_wu)
    dx2, dx2b, dg2 = _rms_bwd("rms2_bwd", dh2, x2, norm2_g, dx3)

    (dy_t,) = _matmul("mm_d_out", wout, dx2b, "nt", [F32], after=[g_wu])
    (g_wout,) = _matmul("mm_gw_out", y_t, dx2b, "nn", [WIRE], after=[dy_t])
    sl_wout = _scatter_grad("scatter_w_out", 7, g_wout)
    (dproj0, dkv, dgq, dgk, dsink, dlng, dlnb, dws, dbs, dgoa, dgog) = _mixer_bwd(proj_t, dy_t, prm, dm, nseq, nb)
    dproj_t = _patch_kv(dproj0, dkv, dm)
    (g_win,) = _matmul("mm_gw_in", dproj_t, h1, "nn", [WIRE])
    early_g = [dgq, dgk, dsink, dlng, dlnb, dws, dbs, dgoa, dgog, dg2, loss_part.reshape(1)]
    sl_win, early_slots = _scatter_grad("scatter_w_in", 8, g_win, rows=_pack(early_g)[0])

    def norm1_backward(dh1, xv, dx2v, g1):
        r = lax.rsqrt(jnp.mean(xv * xv, axis=-1, keepdims=True) + EPS)
        xh = xv * r
        dxh = dh1 * g1
        return dx2v + r * (dxh - xh * jnp.mean(dxh * xh, axis=-1, keepdims=True)), jnp.sum(dh1 * xh, axis=0, keepdims=True)

    dx, dg1 = _matmul("mm_d_in", dproj_t, win_t, "tn", [F32], epilogue=norm1_backward, extras=[xf, dx2], rowvecs=[norm1_g],
                      after=[g_win], col_sum=True, full_rows=True)

    big_out = {}
    last = dx
    for nm, sl in (("w_ffn_down", sl_wd), ("w_ffn_gate", sl_wg), ("w_ffn_up", sl_wu), ("w_out", sl_wout), ("w_in", sl_win)):
        res = _sum_adamw("adamw_" + nm, sl, *big_rows[nm], after=last)
        last = res[1]
        big_out[nm] = tuple(jnp.swapaxes(r[None], 1, 2) if big[nm][3] else r[None] for r in res)

    zero = jnp.zeros((1,), F32)
    small_names = ["q_norm_g", "k_norm_g", "attn_sinks", "gate_ln_g", "gate_ln_b", "w_spatial", "b_spatial",
                   "out_norm_attn_g", "out_norm_gate_g", "norm2_g", "loss", "norm1_g"]
    small_w = [q_norm_g, k_norm_g, attn_sinks, gate_ln_g, gate_ln_b, w_spatial, b_spatial, out_norm_attn_g, out_norm_gate_g, norm2_g, zero, norm1_g]
    small_m = [m_q_norm_g, m_k_norm_g, m_attn_sinks, m_gate_ln_g, m_gate_ln_b, m_w_spatial, m_b_spatial, m_out_norm_attn_g, m_out_norm_gate_g, m_norm2_g, zero, m_norm1_g]
    small_v = [v_q_norm_g, v_k_norm_g, v_attn_sinks, v_gate_ln_g, v_gate_ln_b, v_w_spatial, v_b_spatial, v_out_norm_attn_g, v_out_norm_gate_g, v_norm2_g, zero, v_norm1_g]
    pw, meta = _pack(small_w)
    sg, sd, sm, sv = _allreduce_small_adamw(early_slots, _pack([dg1])[0], pw, _pack(small_m)[0], _pack(small_v)[0], after=last)
    ug, ud, um, uv = _unpack(sg, meta), _unpack(sd, meta), _unpack(sm, meta), _unpack(sv, meta)
    small_out = {nm: (ug[k], ud[k], um[k], uv[k]) for k, nm in enumerate(small_names)}
    loss = small_out["loss"][0].reshape(())

    order = ["norm1_g", "w_in", "q_norm_g", "k_norm_g", "attn_sinks", "gate_ln_g", "gate_ln_b", "w_spatial", "b_spatial",
             "out_norm_attn_g", "out_norm_gate_g", "w_out", "norm2_g", "w_ffn_gate", "w_ffn_up", "w_ffn_down"]
    allo = {**big_out, **small_out}
    outs = [loss, dx.reshape(nseq, seq, d)]
    for k in range(4):
        outs += [allo[nm][k] for nm in order]
    return tuple(outs)
```

```python
import math

import jax
import jax.numpy as jnp
from jax import lax
from jax.experimental import pallas as pl
from jax.experimental.pallas import tpu as pltpu
from jax.experimental.pallas import tpu_sc as plsc

F32 = jnp.float32
MXU = jnp.bfloat16
WIRE = jnp.bfloat16
EPS = 1e-6
BLOCK = 128
GROUP_DIM = 128
N_KV_HEADS = 2
NEG = -1e30
N_DEV = 8
LANES = 128
MIB = 1024 * 1024

ADAM_LR = 0.001
ADAM_B1 = 0.9
ADAM_B2 = 0.999
ADAM_EPS = 1e-08
ADAM_WD = 0.01
ADAM_STEP = 10

MESH = pl.DeviceIdType.MESH
ANY = pl.BlockSpec(memory_space=pl.ANY)


def _pick(n, cands):
    for c in cands:
        if n % c == 0:
            return c
    return n


def _cparams(sem, vmem_mb):
    return pltpu.CompilerParams(dimension_semantics=sem, vmem_limit_bytes=vmem_mb * MIB)


VMEM_TILE_BUDGET = 44 * MIB
HBM_BYTES_PER_US = 3.0e6
STEP_US = 0.4
MIN_TILE_N = 512


def _tile_candidates(n):
    return [c for c in range(min(n, 2048), 0, -LANES) if n % c == 0 and c % LANES == 0] or [n]


def _matmul_tiles(m, n, kk, esz, n_b, extra_sizes, out_sizes, full_rows):
    best = None
    wide = [n] if full_rows else [c for c in _tile_candidates(n) if c >= MIN_TILE_N] or _tile_candidates(n)
    for tm in _tile_candidates(m):
        for tn in wide:
            vmem = 2 * (tm + n_b * tn) * kk * esz + tm * tn * (4 * n_b + 2 * sum(extra_sizes) + 2 * sum(out_sizes))
            if vmem > VMEM_TILE_BUDGET:
                continue
            cost = (m // tm) * n_b * n * kk * esz / HBM_BYTES_PER_US + (m // tm) * (n // tn) * STEP_US
            if best is None or cost < best[0]:
                best = (cost, tm, tn, vmem)
    assert best is not None, (m, n, kk)
    return best[1:]


def _matmul(name, a, b, mode, out_dtypes, epilogue=None, extras=(), rowvecs=(), after=(), partial=False, col_sum=False, full_rows=False, b2=None):
    if mode == "nn":
        (m, kk), n = a.shape, b.shape[1]
        dn = (((1,), (0,)), ((), ()))
    elif mode == "nt":
        (m, kk), n = a.shape, b.shape[0]
        dn = (((1,), (1,)), ((), ()))
    else:
        (kk, m), n = a.shape, b.shape[1]
        dn = (((0,), (0,)), ((), ()))
    bs = [b] if b2 is None else [b, b2]
    tm, tn, vmem = _matmul_tiles(m, n, kk, a.dtype.itemsize, len(bs), [e.dtype.itemsize for e in extras],
                                 [jnp.dtype(dt).itemsize for dt in out_dtypes], full_rows)
    a_spec = pl.BlockSpec((kk, tm), lambda i, j: (0, i)) if mode == "tn" else pl.BlockSpec((tm, kk), lambda i, j: (i, 0))
    b_spec = pl.BlockSpec((tn, kk), lambda i, j: (j, 0)) if mode == "nt" else pl.BlockSpec((kk, tn), lambda i, j: (0, j))
    tile = pl.BlockSpec((tm, tn), lambda i, j: (i, j))
    row = pl.BlockSpec((1, tn), lambda i, j: (0, j))
    nb, ne, nr, na, no = len(bs), len(extras), len(rowvecs), len(after), len(out_dtypes)

    def body(a_ref, *rest):
        b_refs, in_refs, out_refs = rest[:nb], rest[nb:nb + ne + nr], rest[nb + ne + nr + na:]
        av = a_ref[...]
        accs = [lax.dot_general(av, b_ref[...], dn, preferred_element_type=F32) for b_ref in b_refs]
        vals = tuple(accs) if epilogue is None else epilogue(*accs, *[r[...] for r in in_refs])
        for o_ref, t in zip(out_refs[:no], vals[:no]):
            o_ref[...] = t.astype(o_ref.dtype)
        if partial:
            out_refs[no][...] = jnp.full((8, LANES), vals[no], F32)
        if col_sum:
            sum_ref = out_refs[-1]

            @pl.when(pl.program_id(0) == 0)
            def _():
                sum_ref[...] = jnp.zeros_like(sum_ref)

            sum_ref[...] += vals[-1]

    out_specs = [tile] * no
    out_shape = [jax.ShapeDtypeStruct((m, n), dt) for dt in out_dtypes]
    if partial:
        out_specs.append(pl.BlockSpec((8, LANES), lambda i, j: (i, j)))
        out_shape.append(jax.ShapeDtypeStruct((m // tm * 8, n // tn * LANES), F32))
    if col_sum:
        out_specs.append(row)
        out_shape.append(jax.ShapeDtypeStruct((1, n), F32))
    return pl.pallas_call(
        body, name=name, grid=(m // tm, n // tn),
        in_specs=[a_spec] + [b_spec] * nb + [tile] * ne + [row] * nr + [ANY] * na,
        out_specs=out_specs, out_shape=out_shape,
        compiler_params=_cparams(("arbitrary" if col_sum else "parallel", "arbitrary"), min(vmem // MIB + 8, 60)),
    )(a, *bs, *extras, *rowvecs, *after)


def _rms_fwd(name, x, g):
    t, d = x.shape
    tm = _pick(t, (512, 256, 128))

    def body(x_ref, g_ref, h_ref):
        xv = x_ref[...]
        r = lax.rsqrt(jnp.mean(xv * xv, axis=-1, keepdims=True) + EPS)
        h_ref[...] = (xv * r * g_ref[...]).astype(h_ref.dtype)

    return pl.pallas_call(
        body, name=name, grid=(t // tm,),
        in_specs=[pl.BlockSpec((tm, d), lambda i: (i, 0)), pl.BlockSpec((1, d), lambda i: (0, 0))],
        out_specs=pl.BlockSpec((tm, d), lambda i: (i, 0)),
        out_shape=jax.ShapeDtypeStruct((t, d), MXU),
        compiler_params=_cparams(("parallel",), 32),
    )(x, g)


def _rms_bwd(name, dh, x, g, res):
    t, d = x.shape
    tm = _pick(t, (256, 128))

    def body(dh_ref, x_ref, g_ref, res_ref, dx_ref, dxb_ref, dg_ref):
        @pl.when(pl.program_id(0) == 0)
        def _():
            dg_ref[...] = jnp.zeros_like(dg_ref)

        xv, dhv = x_ref[...], dh_ref[...]
        r = lax.rsqrt(jnp.mean(xv * xv, axis=-1, keepdims=True) + EPS)
        xh = xv * r
        dg_ref[...] += jnp.sum(dhv * xh, axis=0, keepdims=True)
        dxh = dhv * g_ref[...]
        dx = res_ref[...] + r * (dxh - xh * jnp.mean(dxh * xh, axis=-1, keepdims=True))
        dx_ref[...] = dx
        dxb_ref[...] = dx.astype(dxb_ref.dtype)

    row = pl.BlockSpec((tm, d), lambda i: (i, 0))
    vec = pl.BlockSpec((1, d), lambda i: (0, 0))
    return pl.pallas_call(
        body, name=name, grid=(t // tm,),
        in_specs=[row, row, vec, row],
        out_specs=[row, row, vec],
        out_shape=[jax.ShapeDtypeStruct((t, d), F32), jax.ShapeDtypeStruct((t, d), MXU), jax.ShapeDtypeStruct((1, d), F32)],
        compiler_params=_cparams(("arbitrary",), 40),
    )(dh, x, g, res)


_INV_SQRT2 = 0.7071067811865476
_INV_SQRT_2PI = 0.3989422804014327


def _dot_nt(a, b):
    return lax.dot_general(a, b, (((1,), (1,)), ((), ())), preferred_element_type=F32)


def _dot_tn(a, b):
    return lax.dot_general(a, b, (((0,), (0,)), ((), ())), preferred_element_type=F32)


def _dot(a, b):
    return jnp.dot(a, b, preferred_element_type=F32)


def _col_rms(v):
    return lax.rsqrt(jnp.mean(v * v, axis=0, keepdims=True) + EPS)


class _Dims:
    def __init__(self, d_model, in_width, head_dim):
        self.d = d_model
        self.aw = d_model // 2
        self.gw = d_model - self.aw
        self.kvw = (in_width - self.aw - 2 * self.gw) // 2
        self.hd = head_dim
        self.nh = self.aw // head_dim
        self.nkv = self.kvw // head_dim
        self.grp = self.nh // self.nkv
        self.ng = self.gw // GROUP_DIM
        self.inw = in_width
        self.zoff = self.aw + 2 * self.kvw
        assert self.nkv == N_KV_HEADS and self.zoff + 2 * self.gw == in_width and self.aw % (2 * self.kvw) == 0


def _band_masks(first, grp):
    kj = lax.broadcasted_iota(jnp.int32, (2 * BLOCK, grp * BLOCK), 0)
    qi = lax.broadcasted_iota(jnp.int32, (2 * BLOCK, grp * BLOCK), 1) & (BLOCK - 1)
    dist = qi + BLOCK - kj
    valid = (dist >= 0) & (dist < BLOCK) & ((kj >= BLOCK) | jnp.logical_not(first))
    return valid, dist.astype(F32)


def _kv_band(dm, kh, p_ref, pkv_ref, gk2):
    ko = dm.aw + kh * dm.hd
    vo = dm.aw + dm.kvw + kh * dm.hd
    k_t = jnp.concatenate([pkv_ref[kh * dm.hd:(kh + 1) * dm.hd, :], p_ref[ko:ko + dm.hd, :]], axis=1)
    v_t = jnp.concatenate([pkv_ref[dm.kvw + kh * dm.hd:dm.kvw + (kh + 1) * dm.hd, :], p_ref[vo:vo + dm.hd, :]], axis=1)
    kn_t = k_t * _col_rms(k_t) * gk2
    return kn_t.astype(MXU), kn_t.T.astype(MXU), v_t.astype(MXU), v_t.T.astype(MXU)


def _group_heads(dm, kh):
    return range(kh * dm.grp, (kh + 1) * dm.grp)


def _lane_row(vals):
    return jnp.concatenate([jnp.full((1, BLOCK), v, F32) for v in vals], axis=1)


def _attn_group_fwd(dm, kh, p_ref, gq, kn, v_tb, sink_ref, valid, dist):
    heads = _group_heads(dm, kh)
    q = jnp.concatenate([p_ref[h * dm.hd:(h + 1) * dm.hd, :] for h in heads], axis=1)
    rq = _col_rms(q)
    qh = q * rq
    qnb = (qh * gq).astype(MXU)
    slopes = _lane_row([math.pow(2.0, -8.0 * (h + 1) / dm.nh) for h in heads])
    sinks = _lane_row([sink_ref[h] for h in heads])
    s = _dot(kn, qnb) * (dm.hd ** -0.5)
    logits = jnp.where(valid, s - slopes * dist, NEG)
    m = jnp.maximum(jnp.max(logits, axis=0, keepdims=True), sinks)
    e = jnp.exp(logits - m)
    es = jnp.exp(sinks - m)
    inv = 1.0 / (jnp.sum(e, axis=0, keepdims=True) + es)
    p = e * inv
    o = _dot(v_tb, p.astype(MXU))
    return o, p, es * inv, rq, qh, qnb


def _gelu_cdf(z):
    return 0.5 * (1.0 + lax.erf(z * _INV_SQRT2))


def _by_group(v, ng):
    return v.reshape(ng, GROUP_DIM, v.shape[1])


def _gate_fwd(dm, p_ref, lng_ref, lnb_ref, ws_ref, bs_ref, tril):
    zu, zv = p_ref[dm.zoff:dm.zoff + dm.gw, :], p_ref[dm.zoff + dm.gw:dm.zoff + 2 * dm.gw, :]
    u, v = zu * _gelu_cdf(zu), zv * _gelu_cdf(zv)
    v3 = _by_group(v, dm.ng)
    xc = v3 - jnp.mean(v3, axis=1, keepdims=True)
    rstd = lax.rsqrt(jnp.mean(xc * xc, axis=1, keepdims=True) + EPS)
    xh = (xc * rstd).reshape(dm.gw, BLOCK)
    vnb = (xh * lng_ref[...] + lnb_ref[...]).astype(MXU)
    wts = [jnp.where(tril, ws_ref[g], 0.0).astype(MXU) for g in range(dm.ng)]
    mixed = jnp.concatenate([_dot_nt(vnb[g * GROUP_DIM:(g + 1) * GROUP_DIM], wts[g]) + bs_ref[g:g + 1, :]
                             for g in range(dm.ng)], axis=0)
    return u * mixed, u, mixed, xh, rstd, vnb, wts, zu, zv


def _mixer_specs(dm, nb, clamp):
    kvblk = dm.aw // (2 * dm.kvw)

    def cur(s, i):
        return (0, s * nb + clamp(i))

    def prev(s, i):
        return (kvblk, s * nb + jnp.maximum(clamp(i) - 1, 0))

    full = lambda shape: pl.BlockSpec(shape, lambda s, i: tuple(0 for _ in shape))
    return cur, prev, full


def _tril():
    return lax.broadcasted_iota(jnp.int32, (BLOCK, BLOCK), 0) >= lax.broadcasted_iota(jnp.int32, (BLOCK, BLOCK), 1)


def _mixer_fwd(proj_t, prm, dm, nseq, nb):
    gq, gk2, sinks, lng, lnb, ws, bs, goa, gog = prm
    t = proj_t.shape[1]
    cur, prev, full = _mixer_specs(dm, nb, lambda i: i)

    def body(p_ref, pkv_ref, gq_ref, gk_ref, sink_ref, lng_ref, lnb_ref, ws_ref, bs_ref, goa_ref, gog_ref, y_ref, att_scr):
        i = pl.program_id(1)
        valid, dist = _band_masks(i == 0, dm.grp)
        gqv, gkv = gq_ref[...], gk_ref[...]
        for kh in range(dm.nkv):
            _, kn, v_tb, _ = _kv_band(dm, kh, p_ref, pkv_ref, gkv)
            o = _attn_group_fwd(dm, kh, p_ref, gqv, kn, v_tb, sink_ref, valid, dist)[0]
            for g, h in enumerate(_group_heads(dm, kh)):
                att_scr[h * dm.hd:(h + 1) * dm.hd, :] = o[:, g * BLOCK:(g + 1) * BLOCK]
        att = att_scr[...]
        y_ref[:dm.aw, :] = (att * _col_rms(att) * goa_ref[...]).astype(y_ref.dtype)
        gt = _gate_fwd(dm, p_ref, lng_ref, lnb_ref, ws_ref, bs_ref, _tril())[0]
        y_ref[dm.aw:, :] = (gt * _col_rms(gt) * gog_ref[...]).astype(y_ref.dtype)

    return pl.pallas_call(
        body, name="mixer_fwd", grid=(nseq, nb),
        in_specs=[pl.BlockSpec((dm.inw, BLOCK), cur), pl.BlockSpec((2 * dm.kvw, BLOCK), prev),
                  full(gq.shape), full(gk2.shape), pl.BlockSpec(memory_space=pltpu.SMEM),
                  full(lng.shape), full(lnb.shape), full(ws.shape), full(bs.shape), full(goa.shape), full(gog.shape)],
        out_specs=pl.BlockSpec((dm.d, BLOCK), cur),
        out_shape=jax.ShapeDtypeStruct((dm.d, t), MXU),
        scratch_shapes=[pltpu.VMEM((dm.aw, BLOCK), F32)],
        compiler_params=_cparams(("parallel", "arbitrary"), 40),
    )(proj_t, proj_t, gq, gk2, sinks, lng, lnb, ws, bs, goa, gog)


def _mixer_bwd(proj_t, dy_t, prm, dm, nseq, nb):
    gq, gk2, sinks, lng, lnb, ws, bs, goa, gog = prm
    t = proj_t.shape[1]
    clamp = lambda i: jnp.minimum(i, nb - 1)
    cur, prev, full = _mixer_specs(dm, nb, clamp)
    kvw2 = 2 * dm.kvw

    def prev_kv_out(s, i):
        return (0, s * nb + jnp.maximum(i - 1, 0))

    def body(p_ref, pkv_ref, dy_ref, gq_ref, gk_ref, sink_ref, lng_ref, lnb_ref, ws_ref, bs_ref, goa_ref, gog_ref,
             dproj_ref, dkv_ref, dgq_ref, dgk_ref, dsink_ref, dlng_ref, dlnb_ref, dws_ref, dbs_ref, dgoa_ref, dgog_ref,
             att_scr, datt_scr, carry_scr, prevpart_scr, curpart_scr, kprev_scr,
             a_gq, a_gk, a_sink, a_lng, a_lnb, a_goa, a_gog):
        s_id, i = pl.program_id(0), pl.program_id(1)
        lane_accs = ((a_gq, dgq_ref), (a_gk, dgk_ref), (a_sink, dsink_ref), (a_lng, dlng_ref), (a_lnb, dlnb_ref),
                     (a_goa, dgoa_ref), (a_gog, dgog_ref))

        @pl.when((s_id == 0) & (i == 0))
        def _():
            for acc, _ in lane_accs:
                acc[...] = jnp.zeros_like(acc)
            dws_ref[...] = jnp.zeros_like(dws_ref)
            dbs_ref[...] = jnp.zeros_like(dbs_ref)

        gqv, gkv = gq_ref[...], gk_ref[...]

        @pl.when(i < nb)
        def _():
            valid, dist = _band_masks(i == 0, dm.grp)
            kvs, fwd = [], []
            for kh in range(dm.nkv):
                kv = _kv_band(dm, kh, p_ref, pkv_ref, gkv)
                kvs.append(kv)
                fwd.append(_attn_group_fwd(dm, kh, p_ref, gqv, kv[1], kv[2], sink_ref, valid, dist))
                for g, h in enumerate(_group_heads(dm, kh)):
                    att_scr[h * dm.hd:(h + 1) * dm.hd, :] = fwd[kh][0][:, g * BLOCK:(g + 1) * BLOCK]
            att = att_scr[...]
            dya = dy_ref[:dm.aw, :]
            ra = _col_rms(att)
            ah = att * ra
            a_goa[...] += dya * ah
            dah = dya * goa_ref[...]
            datt_scr[...] = ra * (dah - ah * jnp.mean(dah * ah, axis=0, keepdims=True))
            for kh in range(dm.nkv):
                kn_tb, kn, v_tb, vb = kvs[kh]
                _, p, ps, rq, qh, qnb = fwd[kh]
                heads = _group_heads(dm, kh)
                do_b = jnp.concatenate([datt_scr[h * dm.hd:(h + 1) * dm.hd, :] for h in heads], axis=1).astype(MXU)
                dp = _dot(vb, do_b)
                delta = jnp.sum(p * dp, axis=0, keepdims=True)
                dsb = (p * (dp - delta) * (dm.hd ** -0.5)).astype(MXU)
                dsink = ps * delta
                dqn = _dot(kn_tb, dsb)
                dkn = _dot_nt(qnb, dsb)
                dvb = _dot_nt(do_b, p.astype(MXU))
                a_gq[...] += dqn * qh
                dqh = dqn * gqv
                dq = rq * (dqh - qh * jnp.mean(dqh * qh, axis=0, keepdims=True))
                for g, h in enumerate(heads):
                    a_sink[h:h + 1, :] += -dsink[:, g * BLOCK:(g + 1) * BLOCK]
                    dproj_ref[h * dm.hd:(h + 1) * dm.hd, :] = dq[:, g * BLOCK:(g + 1) * BLOCK].astype(dproj_ref.dtype)
                krows = slice(kh * dm.hd, (kh + 1) * dm.hd)
                vrows = slice(dm.kvw + kh * dm.hd, dm.kvw + (kh + 1) * dm.hd)
                prevpart_scr[krows, :] = dkn[:, :BLOCK]
                prevpart_scr[vrows, :] = dvb[:, :BLOCK]
                curpart_scr[krows, :] = dkn[:, BLOCK:]
                curpart_scr[vrows, :] = dvb[:, BLOCK:]
            dproj_ref[dm.aw:dm.zoff, :] = jnp.zeros((kvw2, BLOCK), dproj_ref.dtype)
            tril = _tril()
            gt, u, mixed, xh, rstd, vnb, wts, zu, zv = _gate_fwd(dm, p_ref, lng_ref, lnb_ref, ws_ref, bs_ref, tril)
            dyg = dy_ref[dm.aw:, :]
            rg = _col_rms(gt)
            gh = gt * rg
            a_gog[...] += dyg * gh
            dgh = dyg * gog_ref[...]
            dgt = rg * (dgh - gh * jnp.mean(dgh * gh, axis=0, keepdims=True))
            du = dgt * mixed
            dmix = dgt * u
            dmixb = dmix.astype(MXU)
            dbs_ref[...] += jnp.sum(_by_group(dmix, dm.ng), axis=1)
            dvn = []
            for g in range(dm.ng):
                rows = slice(g * GROUP_DIM, (g + 1) * GROUP_DIM)
                dws_ref[g] += jnp.where(tril, _dot_tn(dmixb[rows], vnb[rows]), 0.0)
                dvn.append(_dot(dmixb[rows], wts[g]))
            dvn = jnp.concatenate(dvn, axis=0)
            a_lng[...] += dvn * xh
            a_lnb[...] += dvn
            dxh3, xh3 = _by_group(dvn * lng_ref[...], dm.ng), _by_group(xh, dm.ng)
            dv = (rstd * (dxh3 - jnp.mean(dxh3, axis=1, keepdims=True) - xh3 * jnp.mean(dxh3 * xh3, axis=1, keepdims=True))).reshape(dm.gw, BLOCK)
            dgu = _gelu_cdf(zu) + zu * (jnp.exp(-0.5 * zu * zu) * _INV_SQRT_2PI)
            dgv = _gelu_cdf(zv) + zv * (jnp.exp(-0.5 * zv * zv) * _INV_SQRT_2PI)
            dproj_ref[dm.zoff:dm.zoff + dm.gw, :] = (du * dgu).astype(dproj_ref.dtype)
            dproj_ref[dm.zoff + dm.gw:, :] = (dv * dgv).astype(dproj_ref.dtype)

        @pl.when(i == nb)
        def _():
            prevpart_scr[...] = jnp.zeros_like(prevpart_scr)

        @pl.when(i >= 1)
        def _():
            tot = carry_scr[...] + prevpart_scr[...]
            for kh in range(dm.nkv):
                krows = slice(kh * dm.hd, (kh + 1) * dm.hd)
                kraw = kprev_scr[krows, :]
                rk = _col_rms(kraw)
                khat = kraw * rk
                dkn = tot[krows, :]
                a_gk[...] += dkn * khat
                dkh = dkn * gkv[:, :BLOCK]
                dk = rk * (dkh - khat * jnp.mean(dkh * khat, axis=0, keepdims=True))
                dkv_ref[krows, :] = dk.astype(dkv_ref.dtype)
            dkv_ref[dm.kvw:, :] = tot[dm.kvw:, :].astype(dkv_ref.dtype)

        @pl.when(i < nb)
        def _():
            carry_scr[...] = curpart_scr[...]
            kprev_scr[...] = p_ref[dm.aw:dm.aw + dm.kvw, :]

        @pl.when((s_id == nseq - 1) & (i == nb))
        def _():
            for acc, out in lane_accs:
                out[...] = jnp.sum(acc[...], axis=1, keepdims=True)

    col = lambda rows: jax.ShapeDtypeStruct((rows, 1), F32)
    lane = lambda rows: pltpu.VMEM((rows, LANES), F32)
    return pl.pallas_call(
        body, name="mixer_bwd", grid=(nseq, nb + 1),
        in_specs=[pl.BlockSpec((dm.inw, BLOCK), cur), pl.BlockSpec((kvw2, BLOCK), prev), pl.BlockSpec((dm.d, BLOCK), cur),
                  full(gq.shape), full(gk2.shape), pl.BlockSpec(memory_space=pltpu.SMEM),
                  full(lng.shape), full(lnb.shape), full(ws.shape), full(bs.shape), full(goa.shape), full(gog.shape)],
        out_specs=[pl.BlockSpec((dm.inw, BLOCK), cur), pl.BlockSpec((kvw2, BLOCK), prev_kv_out),
                   full((dm.hd, 1)), full((dm.hd, 1)), full((dm.nh, 1)), full((dm.gw, 1)), full((dm.gw, 1)), full(ws.shape),
                   full(bs.shape), full((dm.aw, 1)), full((dm.gw, 1))],
        out_shape=[jax.ShapeDtypeStruct((dm.inw, t), MXU), jax.ShapeDtypeStruct((kvw2, t), MXU),
                   col(dm.hd), col(dm.hd), col(dm.nh), col(dm.gw), col(dm.gw), jax.ShapeDtypeStruct(ws.shape, F32),
                   jax.ShapeDtypeStruct(bs.shape, F32), col(dm.aw), col(dm.gw)],
        scratch_shapes=[pltpu.VMEM((dm.aw, BLOCK), F32), pltpu.VMEM((dm.aw, BLOCK), F32),
                        pltpu.VMEM((kvw2, BLOCK), F32), pltpu.VMEM((kvw2, BLOCK), F32), pltpu.VMEM((kvw2, BLOCK), F32),
                        pltpu.VMEM((dm.kvw, BLOCK), F32),
                        pltpu.VMEM((dm.hd, dm.grp * BLOCK), F32), lane(dm.hd), lane(dm.nh), lane(dm.gw), lane(dm.gw), lane(dm.aw), lane(dm.gw)],
        compiler_params=_cparams(("arbitrary", "arbitrary"), 48),
    )(proj_t, proj_t, dy_t, gq, gk2, sinks, lng, lnb, ws, bs, goa, gog)


def _patch_kv(dproj_t, dkv_t, dm):
    t = dproj_t.shape[1]
    tc = _pick(t, (1024, 512, 256, 128))
    kvw2 = 2 * dm.kvw
    kvblk = dm.aw // kvw2

    def body(dproj_hbm, dkv_ref, out_ref):
        del dproj_hbm
        out_ref[...] = dkv_ref[...]

    return pl.pallas_call(
        body, name="patch_kv", grid=(t // tc,),
        in_specs=[ANY, pl.BlockSpec((kvw2, tc), lambda i: (0, i))],
        out_specs=pl.BlockSpec((kvw2, tc), lambda i: (kvblk, i)),
        out_shape=jax.ShapeDtypeStruct(dproj_t.shape, dproj_t.dtype),
        input_output_aliases={0: 0},
        compiler_params=_cparams(("parallel",), 32),
    )(dproj_t, dkv_t)


def _place():
    x, y, c = lax.axis_index("x"), lax.axis_index("y"), lax.axis_index("c")
    return x, y, c


def _handshake(peers):
    barrier = pltpu.get_barrier_semaphore()
    for p in peers:
        pl.semaphore_signal(barrier, inc=1, device_id=p, device_id_type=MESH)
    pl.semaphore_wait(barrier, len(peers))


def _sequencer_mesh():
    return plsc.ScalarSubcoreMesh(axis_name="sequencer", num_cores=1)


def _allgather_weights(name, collective_id, shards, after=None):
    nw = len(shards)

    def body(*refs):
        src, out = refs[:nw], refs[-3 - nw:-3]
        send_sems, recv_sems, local_sems = refs[-3:]
        x, y, c = _place()
        me, sibling = (x, y, c), (x, y, 1 - c)
        chips = [(1 - x, y), (x, 1 - y), (1 - x, 1 - y)]
        _handshake([sibling] + [(*chip, c) for chip in chips])

        def rows(w, place):
            n = src[w].shape[0]
            px, py, pc = place
            return out[w].at[pl.ds(pl.multiple_of((4 * px + 2 * py + pc) * n, 16), n), :]

        def copy(w, k, block, to, from_src=False):
            return pltpu.make_async_remote_copy(
                src_ref=src[w] if from_src else rows(w, block), dst_ref=rows(w, block),
                send_sem=send_sems.at[w, k], recv_sem=recv_sems.at[w, k], device_id=to, device_id_type=MESH)

        mine = [pltpu.make_async_copy(src[w], rows(w, me), local_sems.at[w]) for w in range(nw)]
        for cp in mine:
            cp.start()
        first = []
        for w in range(nw):
            first.append(copy(w, 0, me, sibling, from_src=True))
            first += [copy(w, 1 + j, me, (*chip, c), from_src=True) for j, chip in enumerate(chips)]
        for cp in first:
            cp.start()
        passed = []
        for w in range(nw):
            for j, chip in enumerate(chips):
                copy(w, 1 + j, (*chip, c), me).wait_recv()
                fwd = copy(w, 4 + j, (*chip, c), sibling)
                fwd.start()
                passed.append(fwd)
        for w in range(nw):
            copy(w, 0, sibling, me).wait_recv()
            for j, chip in enumerate(chips):
                copy(w, 4 + j, (*chip, 1 - c), me).wait_recv()
        for cp in first + passed:
            cp.wait_send()
        for cp in mine:
            cp.wait()

    return pl.kernel(
        body, name=name,
        out_type=[jax.ShapeDtypeStruct((N_DEV * s.shape[0], s.shape[1]), s.dtype) for s in shards],
        mesh=_sequencer_mesh(),
        scratch_types=[pltpu.SemaphoreType.DMA((nw, 7)), pltpu.SemaphoreType.DMA((nw, 7)), pltpu.SemaphoreType.DMA((nw,))],
        compiler_params=pltpu.CompilerParams(collective_id=collective_id),
    )(*shards, *([] if after is None else [after]))


_FLIPS = [(0, 0, 1), (1, 0, 0), (0, 1, 0), (1, 1, 0), (1, 0, 1), (0, 1, 1), (1, 1, 1)]


def _scatter_grad(name, collective_id, grad, rows=None):
    n = grad.shape[0] // N_DEV
    with_rows = rows is not None

    def body(*refs):
        src, out = refs[0], refs[2 if with_rows else 1]
        send_sems, recv_sems, local_sems = refs[-3:]
        x, y, c = _place()
        me_idx = 4 * x + 2 * y + c
        peers = [(x ^ fx, y ^ fy, c ^ fc) for (fx, fy, fc) in _FLIPS]
        _handshake(peers)

        def block(idx):
            return src.at[pl.ds(pl.multiple_of(idx * n, 16), n), :]

        copies = [pltpu.make_async_remote_copy(
            src_ref=block(4 * px + 2 * py + pc), dst_ref=out.at[me_idx], send_sem=send_sems.at[k], recv_sem=recv_sems.at[k],
            device_id=(px, py, pc), device_id_type=MESH) for k, (px, py, pc) in enumerate(peers)]
        mine = [pltpu.make_async_copy(block(me_idx), out.at[me_idx], local_sems.at[0])]
        if with_rows:
            rows_src, rows_out = refs[1], refs[3]
            copies += [pltpu.make_async_remote_copy(
                src_ref=rows_src, dst_ref=rows_out.at[me_idx], send_sem=send_sems.at[7 + k], recv_sem=recv_sems.at[7 + k],
                device_id=peer, device_id_type=MESH) for k, peer in enumerate(peers)]
            mine.append(pltpu.make_async_copy(rows_src, rows_out.at[me_idx], local_sems.at[1]))
        for cp in mine + copies:
            cp.start()
        for cp in copies:
            cp.wait_recv()
        for cp in copies:
            cp.wait_send()
        for cp in mine:
            cp.wait()

    slots = jax.ShapeDtypeStruct((N_DEV, n, grad.shape[1]), grad.dtype)
    return pl.kernel(
        body, name=name,
        out_type=[slots, jax.ShapeDtypeStruct((N_DEV,) + rows.shape, rows.dtype)] if with_rows else slots,
        mesh=_sequencer_mesh(),
        scratch_types=[pltpu.SemaphoreType.DMA((14,)), pltpu.SemaphoreType.DMA((14,)), pltpu.SemaphoreType.DMA((2,))],
        compiler_params=pltpu.CompilerParams(collective_id=collective_id),
    )(*([grad, rows] if with_rows else [grad]))


def _adamw_math(w, g, m, v):
    m = ADAM_B1 * m + (1.0 - ADAM_B1) * g
    v = ADAM_B2 * v + (1.0 - ADAM_B2) * (g * g)
    m_hat = m / (1.0 - ADAM_B1 ** ADAM_STEP)
    v_hat = v / (1.0 - ADAM_B2 ** ADAM_STEP)
    delta = -ADAM_LR * (m_hat / (jnp.sqrt(v_hat) + ADAM_EPS) + ADAM_WD * w)
    return delta, m, v


def _sum_adamw(name, slots, w, m, v, after):
    _, n, kk = slots.shape
    tr = _pick(n, (208, 176, 128, 96, 64, 32, 16))

    def body(s_ref, w_ref, m_ref, v_ref, after_ref, g_ref, d_ref, nm_ref, nv_ref):
        del after_ref
        g = s_ref[0].astype(F32)
        for p in range(1, N_DEV):
            g = g + s_ref[p].astype(F32)
        g_ref[...] = g
        d_ref[...], nm_ref[...], nv_ref[...] = _adamw_math(w_ref[...], g, m_ref[...], v_ref[...])

    row = pl.BlockSpec((tr, kk), lambda i: (i, 0))
    return pl.pallas_call(
        body, name=name, grid=(n // tr,),
        in_specs=[pl.BlockSpec((N_DEV, tr, kk), lambda i: (0, i, 0)), row, row, row, ANY],
        out_specs=[row] * 4,
        out_shape=[jax.ShapeDtypeStruct((n, kk), F32)] * 4,
        compiler_params=_cparams(("parallel",), 48),
    )(slots, w, m, v, after)


def _allreduce_small_adamw(early_slots, late, w, m, v, after):
    ra, rb = early_slots.shape[1], late.shape[0]

    def body(early_ref, late_ref, w_ref, m_ref, v_ref, after_ref, g_ref, d_ref, nm_ref, nv_ref, slots, send_sems, recv_sems):
        del after_ref
        x, y, c = _place()
        me_idx = 4 * x + 2 * y + c
        copies = []
        for k, (fx, fy, fc) in enumerate(_FLIPS):
            px, py, pc = x ^ fx, y ^ fy, c ^ fc
            copies.append(pltpu.make_async_remote_copy(
                src_ref=late_ref, dst_ref=slots.at[me_idx], send_sem=send_sems.at[k], recv_sem=recv_sems.at[k],
                device_id=(px, py, pc), device_id_type=MESH))
        for cp in copies:
            cp.start()
        slots[me_idx] = late_ref[...]
        g = early_ref[0]
        for p in range(1, N_DEV):
            g = g + early_ref[p]
        early = pl.ds(0, ra)
        g_ref[early, :] = g
        d_ref[early, :], nm_ref[early, :], nv_ref[early, :] = _adamw_math(w_ref[early, :], g, m_ref[early, :], v_ref[early, :])
        for cp in copies:
            cp.wait_recv()
        for cp in copies:
            cp.wait_send()
        g = slots[0]
        for p in range(1, N_DEV):
            g = g + slots[p]
        tail = pl.ds(ra, rb)
        g_ref[tail, :] = g
        d_ref[tail, :], nm_ref[tail, :], nv_ref[tail, :] = _adamw_math(w_ref[tail, :], g, m_ref[tail, :], v_ref[tail, :])

    vm = pl.BlockSpec(memory_space=pltpu.VMEM)
    return pl.pallas_call(
        body, name="allreduce_small_adamw",
        in_specs=[vm] * 5 + [ANY], out_specs=[vm] * 4,
        out_shape=[jax.ShapeDtypeStruct((ra + rb, LANES), F32)] * 4,
        scratch_shapes=[pltpu.VMEM((N_DEV, rb, LANES), F32), pltpu.SemaphoreType.DMA((7,)), pltpu.SemaphoreType.DMA((7,))],
        compiler_params=pltpu.CompilerParams(vmem_limit_bytes=48 * MIB),
    )(early_slots, late, w, m, v, after)


def _pack(arrs):
    parts, meta, off = [], [], 0
    for a in arrs:
        flat = a.reshape(-1).astype(F32)
        rows = -(-flat.shape[0] // LANES)
        rows8 = -(-rows // 8) * 8
        flat = jnp.pad(flat, (0, rows8 * LANES - flat.shape[0]))
        parts.append(flat.reshape(rows8, LANES))
        meta.append((off, a.shape, a.size))
        off += rows8
    return jnp.concatenate(parts, axis=0), meta


def _unpack(packed, meta):
    outs = []
    for off, shape, size in meta:
        rows = -(-size // LANES)
        outs.append(packed[off:off + rows].reshape(-1)[:size].reshape(shape))
    return outs


def _silu_parts(a):
    sg = 0.5 + 0.5 * jnp.tanh(0.5 * a)
    return a * sg, sg * (1.0 + a * (1.0 - sg))


def kernel(x, norm1_g, w_in, q_norm_g, k_norm_g, attn_sinks, gate_ln_g, gate_ln_b, w_spatial, b_spatial, out_norm_attn_g, out_norm_gate_g, w_out, norm2_g, w_ffn_gate, w_ffn_up, w_ffn_down, loss_target, m_norm1_g, m_w_in, m_q_norm_g, m_k_norm_g, m_attn_sinks, m_gate_ln_g, m_gate_ln_b, m_w_spatial, m_b_spatial, m_out_norm_attn_g, m_out_norm_gate_g, m_w_out, m_norm2_g, m_w_ffn_gate, m_w_ffn_up, m_w_ffn_down, v_norm1_g, v_w_in, v_q_norm_g, v_k_norm_g, v_attn_sinks, v_gate_ln_g, v_gate_ln_b, v_w_spatial, v_b_spatial, v_out_norm_attn_g, v_out_norm_gate_g, v_w_out, v_norm2_g, v_w_ffn_gate, v_w_ffn_up, v_w_ffn_down):
    nseq, seq, d = x.shape
    t = nseq * seq
    nb = seq // BLOCK
    inw = w_in.shape[2] * N_DEV
    dm = _Dims(d, inw, q_norm_g.shape[-1])
    xf = x.reshape(t, d)
    tgt = loss_target.reshape(t, d)

    rows = lambda wv, transposed: jnp.swapaxes(wv, 1, 2)[0] if transposed else wv[0]
    big = {"w_in": (w_in, m_w_in, v_w_in, True), "w_out": (w_out, m_w_out, v_w_out, False),
           "w_ffn_gate": (w_ffn_gate, m_w_ffn_gate, v_w_ffn_gate, True), "w_ffn_up": (w_ffn_up, m_w_ffn_up, v_w_ffn_up, True),
           "w_ffn_down": (w_ffn_down, m_w_ffn_down, v_w_ffn_down, False)}
    big_rows = {nm: tuple(rows(arr, tr) for arr in (wv, mv, vv)) for nm, (wv, mv, vv, tr) in big.items()}
    shard = lambda nm: big_rows[nm][0].astype(WIRE)
    (win_t,) = _allgather_weights("gather_w_in", 1, [shard("w_in")])
    (wout,) = _allgather_weights("gather_w_out", 2, [shard("w_out")], after=win_t)
    (wg_t,) = _allgather_weights("gather_w_ffn_gate", 3, [shard("w_ffn_gate")], after=win_t)
    (wu_t,) = _allgather_weights("gather_w_ffn_up", 9, [shard("w_ffn_up")], after=win_t)
    (wd,) = _allgather_weights("gather_w_ffn_down", 10, [shard("w_ffn_down")], after=win_t)

    lanes = lambda v, n=BLOCK: jnp.broadcast_to(v.reshape(-1, 1), (v.size, n))
    prm = (lanes(q_norm_g, dm.grp * BLOCK), lanes(k_norm_g, 2 * BLOCK), attn_sinks[0], lanes(gate_ln_g), lanes(gate_ln_b), w_spatial[0], b_spatial[0],
           lanes(out_norm_attn_g), lanes(out_norm_gate_g))

    h1 = _rms_fwd("rms1_fwd", xf, norm1_g)
    (proj_t,) = _matmul("mm_in", win_t, h1, "nt", [F32])
    y_t = _mixer_fwd(proj_t, prm, dm, nseq, nb)

    def residual_norm(acc, xr, g2):
        x2v = xr + acc
        return x2v, x2v * lax.rsqrt(jnp.mean(x2v * x2v, axis=-1, keepdims=True) + EPS) * g2

    x2, h2 = _matmul("mm_out", y_t, wout, "tn", [F32, MXU], epilogue=residual_norm, extras=[xf], rowvecs=[norm2_g], full_rows=True)
    a, b, s = _matmul("mm_gate_up", h2, wg_t, "nt", [MXU, MXU, MXU], b2=wu_t, epilogue=lambda ga, ub: (ga, ub, _silu_parts(ga)[0] * ub))

    def loss_epilogue(acc, x2v, tv):
        diff = (x2v + acc) - tv
        dx3 = diff * (1.0 / d)
        return dx3, dx3, jnp.sum(diff * diff)

    dx3, dx3b, lossp = _matmul("mm_down", s, wd, "nn", [F32, MXU], epilogue=loss_epilogue, extras=[x2, tgt], partial=True)
    loss_part = (0.5 / d) * jnp.sum(lossp[::8, ::LANES])

    def dswiglu(acc, av, bv):
        silu, dsilu = _silu_parts(av.astype(F32))
        return acc * bv.astype(F32) * dsilu, acc * silu

    da, db = _matmul("mm_d_down", dx3b, wd, "nt", [MXU, MXU], epilogue=dswiglu, extras=[a, b])
    (g_wd,) = _matmul("mm_gw_down", s, dx3b, "tn", [WIRE])
    sl_wd = _scatter_grad("scatter_w_ffn_down", 4, g_wd)
    (dh2a,) = _matmul("mm_dh2_gate", da, wg_t, "nn", [F32], after=[g_wd])
    (g_wg,) = _matmul("mm_gw_gate", da, h2, "tn", [WIRE], after=[dh2a])
    sl_wg = _scatter_grad("scatter_w_ffn_gate", 5, g_wg)
    (dh2,) = _matmul("mm_dh2_up", db, wu_t, "nn", [F32], epilogue=lambda acc, pv: (pv + acc,), extras=[dh2a], after=[g_wg])
    (g_wu,) = _matmul("mm_gw_up", db, h2, "tn", [WIRE], after=[dh2])
    sl_wu = _scatter_grad("scatter_w_ffn_up", 6, g_wu)
    dx2, dx2b, dg2 = _rms_bwd("rms2_bwd", dh2, x2, norm2_g, dx3)

    (dy_t,) = _matmul("mm_d_out", wout, dx2b, "nt", [F32], after=[g_wu])
    (g_wout,) = _matmul("mm_gw_out", y_t, dx2b, "nn", [WIRE], after=[dy_t])
    sl_wout = _scatter_grad("scatter_w_out", 7, g_wout)
    (dproj0, dkv, dgq, dgk, dsink, dlng, dlnb, dws, dbs, dgoa, dgog) = _mixer_bwd(proj_t, dy_t, prm, dm, nseq, nb)
    dproj_t = _patch_kv(dproj0, dkv, dm)
    (g_win,) = _matmul("mm_gw_in", dproj_t, h1, "nn", [WIRE])
    early_g = [dgq, dgk, dsink, dlng, dlnb, dws, dbs, dgoa, dgog, dg2, loss_part.reshape(1)]
    sl_win, early_slots = _scatter_grad("scatter_w_in", 8, g_win, rows=_pack(early_g)[0])

    def norm1_backward(dh1, xv, dx2v, g1):
        r = lax.rsqrt(jnp.mean(xv * xv, axis=-1, keepdims=True) + EPS)
        xh = xv * r
        dxh = dh1 * g1
        return dx2v + r * (dxh - xh * jnp.mean(dxh * xh, axis=-1, keepdims=True)), jnp.sum(dh1 * xh, axis=0, keepdims=True)

    dx, dg1 = _matmul("mm_d_in", dproj_t, win_t, "tn", [F32], epilogue=norm1_backward, extras=[xf, dx2], rowvecs=[norm1_g],
                      after=[g_win], col_sum=True, full_rows=True)

    big_out = {}
    last = dx
    for nm, sl in (("w_ffn_down", sl_wd), ("w_ffn_gate", sl_wg), ("w_ffn_up", sl_wu), ("w_out", sl_wout), ("w_in", sl_win)):
        res = _sum_adamw("adamw_" + nm, sl, *big_rows[nm], after=last)
        last = res[1]
        big_out[nm] = tuple(jnp.swapaxes(r[None], 1, 2) if big[nm][3] else r[None] for r in res)

    zero = jnp.zeros((1,), F32)
    small_names = ["q_norm_g", "k_norm_g", "attn_sinks", "gate_ln_g", "gate_ln_b", "w_spatial", "b_spatial",
                   "out_norm_attn_g", "out_norm_gate_g", "norm2_g", "loss", "norm1_g"]
    small_w = [q_norm_g, k_norm_g, attn_sinks, gate_ln_g, gate_ln_b, w_spatial, b_spatial, out_norm_attn_g, out_norm_gate_g, norm2_g, zero, norm1_g]
    small_m = [m_q_norm_g, m_k_norm_g, m_attn_sinks, m_gate_ln_g, m_gate_ln_b, m_w_spatial, m_b_spatial, m_out_norm_attn_g, m_out_norm_gate_g, m_norm2_g, zero, m_norm1_g]
    small_v = [v_q_norm_g, v_k_norm_g, v_attn_sinks, v_gate_ln_g, v_gate_ln_b, v_w_spatial, v_b_spatial, v_out_norm_attn_g, v_out_norm_gate_g, v_norm2_g, zero, v_norm1_g]
    pw, meta = _pack(small_w)
    sg, sd, sm, sv = _allreduce_small_adamw(early_slots, _pack([dg1])[0], pw, _pack(small_m)[0], _pack(small_v)[0], after=last)
    ug, ud, um, uv = _unpack(sg, meta), _unpack(sd, meta), _unpack(sm, meta), _unpack(sv, meta)
    small_out = {nm: (ug[k], ud[k], um[k], uv[k]) for k, nm in enumerate(small_names)}
    loss = small_out["loss"][0].reshape(())

    order = ["norm1_g", "w_in", "q_norm_g", "k_norm_g", "attn_sinks", "gate_ln_g", "gate_ln_b", "w_spatial", "b_spatial",
             "out_norm_attn_g", "out_norm_gate_g", "w_out", "norm2_g", "w_ffn_gate", "w_ffn_up", "w_ffn_down"]
    allo = {**big_out, **small_out}
    outs = [loss, dx.reshape(nseq, seq, d)]
    for k in range(4):
        outs += [allo[nm][k] for nm in order]
    return tuple(outs)
```

```python
import math

import jax
import jax.numpy as jnp
from jax import lax
from jax.experimental import pallas as pl
from jax.experimental.pallas import tpu as pltpu
from jax.experimental.pallas import tpu_sc as plsc

F32 = jnp.float32
MXU = jnp.bfloat16
WIRE = jnp.bfloat16
EPS = 1e-6
BLOCK = 128
GROUP_DIM = 128
N_KV_HEADS = 2
NEG = -1e30
N_DEV = 8
LANES = 128
MIB = 1024 * 1024

ADAM_LR = 0.001
ADAM_B1 = 0.9
ADAM_B2 = 0.999
ADAM_EPS = 1e-08
ADAM_WD = 0.01
ADAM_STEP = 10

MESH = pl.DeviceIdType.MESH
ANY = pl.BlockSpec(memory_space=pl.ANY)


def _pick(n, cands):
    for c in cands:
        if n % c == 0:
            return c
    return n


def _cparams(sem, vmem_mb):
    return pltpu.CompilerParams(dimension_semantics=sem, vmem_limit_bytes=vmem_mb * MIB)


VMEM_TILE_BUDGET = 44 * MIB
HBM_BYTES_PER_US = 3.0e6
STEP_US = 0.4
MIN_TILE_N = 512


def _tile_candidates(n):
    return [c for c in range(min(n, 2048), 0, -LANES) if n % c == 0 and c % LANES == 0] or [n]


def _matmul_tiles(m, n, kk, esz, n_b, extra_sizes, out_sizes, full_rows):
    best = None
    wide = [n] if full_rows else [c for c in _tile_candidates(n) if c >= MIN_TILE_N] or _tile_candidates(n)
    for tm in _tile_candidates(m):
        for tn in wide:
            vmem = 2 * (tm + n_b * tn) * kk * esz + tm * tn * (4 * n_b + 2 * sum(extra_sizes) + 2 * sum(out_sizes))
            if vmem > VMEM_TILE_BUDGET:
                continue
            cost = (m // tm) * n_b * n * kk * esz / HBM_BYTES_PER_US + (m // tm) * (n // tn) * STEP_US
            if best is None or cost < best[0]:
                best = (cost, tm, tn, vmem)
    assert best is not None, (m, n, kk)
    return best[1:]


def _matmul(name, a, b, mode, out_dtypes, epilogue=None, extras=(), rowvecs=(), after=(), partial=False, col_sum=False, full_rows=False, b2=None):
    if mode == "nn":
        (m, kk), n = a.shape, b.shape[1]
        dn = (((1,), (0,)), ((), ()))
    elif mode == "nt":
        (m, kk), n = a.shape, b.shape[0]
        dn = (((1,), (1,)), ((), ()))
    else:
        (kk, m), n = a.shape, b.shape[1]
        dn = (((0,), (0,)), ((), ()))
    bs = [b] if b2 is None else [b, b2]
    tm, tn, vmem = _matmul_tiles(m, n, kk, a.dtype.itemsize, len(bs), [e.dtype.itemsize for e in extras],
                                 [jnp.dtype(dt).itemsize for dt in out_dtypes], full_rows)
    a_spec = pl.BlockSpec((kk, tm), lambda i, j: (0, i)) if mode == "tn" else pl.BlockSpec((tm, kk), lambda i, j: (i, 0))
    b_spec = pl.BlockSpec((tn, kk), lambda i, j: (j, 0)) if mode == "nt" else pl.BlockSpec((kk, tn), lambda i, j: (0, j))
    tile = pl.BlockSpec((tm, tn), lambda i, j: (i, j))
    row = pl.BlockSpec((1, tn), lambda i, j: (0, j))
    nb, ne, nr, na, no = len(bs), len(extras), len(rowvecs), len(after), len(out_dtypes)

    def body(a_ref, *rest):
        b_refs, in_refs, out_refs = rest[:nb], rest[nb:nb + ne + nr], rest[nb + ne + nr + na:]
        av = a_ref[...]
        accs = [lax.dot_general(av, b_ref[...], dn, preferred_element_type=F32) for b_ref in b_refs]
        vals = tuple(accs) if epilogue is None else epilogue(*accs, *[r[...] for r in in_refs])
        for o_ref, t in zip(out_refs[:no], vals[:no]):
            o_ref[...] = t.astype(o_ref.dtype)
        if partial:
            out_refs[no][...] = jnp.full((8, LANES), vals[no], F32)
        if col_sum:
            sum_ref = out_refs[-1]

            @pl.when(pl.program_id(0) == 0)
            def _():
                sum_ref[...] = jnp.zeros_like(sum_ref)

            sum_ref[...] += vals[-1]

    out_specs = [tile] * no
    out_shape = [jax.ShapeDtypeStruct((m, n), dt) for dt in out_dtypes]
    if partial:
        out_specs.append(pl.BlockSpec((8, LANES), lambda i, j: (i, j)))
        out_shape.append(jax.ShapeDtypeStruct((m // tm * 8, n // tn * LANES), F32))
    if col_sum:
        out_specs.append(row)
        out_shape.append(jax.ShapeDtypeStruct((1, n), F32))
    return pl.pallas_call(
        body, name=name, grid=(m // tm, n // tn),
        in_specs=[a_spec] + [b_spec] * nb + [tile] * ne + [row] * nr + [ANY] * na,
        out_specs=out_specs, out_shape=out_shape,
        compiler_params=_cparams(("arbitrary" if col_sum else "parallel", "arbitrary"), min(vmem // MIB + 8, 60)),
    )(a, *bs, *extras, *rowvecs, *after)


def _rms_fwd(name, x, g):
    t, d = x.shape
    tm = _pick(t, (512, 256, 128))

    def body(x_ref, g_ref, h_ref):
        xv = x_ref[...]
        r = lax.rsqrt(jnp.mean(xv * xv, axis=-1, keepdims=True) + EPS)
        h_ref[...] = (xv * r * g_ref[...]).astype(h_ref.dtype)

    return pl.pallas_call(
        body, name=name, grid=(t // tm,),
        in_specs=[pl.BlockSpec((tm, d), lambda i: (i, 0)), pl.BlockSpec((1, d), lambda i: (0, 0))],
        out_specs=pl.BlockSpec((tm, d), lambda i: (i, 0)),
        out_shape=jax.ShapeDtypeStruct((t, d), MXU),
        compiler_params=_cparams(("parallel",), 32),
    )(x, g)


def _rms_bwd(name, dh, x, g, res):
    t, d = x.shape
    tm = _pick(t, (256, 128))

    def body(dh_ref, x_ref, g_ref, res_ref, dx_ref, dxb_ref, dg_ref):
        @pl.when(pl.program_id(0) == 0)
        def _():
            dg_ref[...] = jnp.zeros_like(dg_ref)

        xv, dhv = x_ref[...], dh_ref[...]
        r = lax.rsqrt(jnp.mean(xv * xv, axis=-1, keepdims=True) + EPS)
        xh = xv * r
        dg_ref[...] += jnp.sum(dhv * xh, axis=0, keepdims=True)
        dxh = dhv * g_ref[...]
        dx = res_ref[...] + r * (dxh - xh * jnp.mean(dxh * xh, axis=-1, keepdims=True))
        dx_ref[...] = dx
        dxb_ref[...] = dx.astype(dxb_ref.dtype)

    row = pl.BlockSpec((tm, d), lambda i: (i, 0))
    vec = pl.BlockSpec((1, d), lambda i: (0, 0))
    return pl.pallas_call(
        body, name=name, grid=(t // tm,),
        in_specs=[row, row, vec, row],
        out_specs=[row, row, vec],
        out_shape=[jax.ShapeDtypeStruct((t, d), F32), jax.ShapeDtypeStruct((t, d), MXU), jax.ShapeDtypeStruct((1, d), F32)],
        compiler_params=_cparams(("arbitrary",), 40),
    )(dh, x, g, res)


_INV_SQRT2 = 0.7071067811865476
_INV_SQRT_2PI = 0.3989422804014327


def _dot_nt(a, b):
    return lax.dot_general(a, b, (((1,), (1,)), ((), ())), preferred_element_type=F32)


def _dot_tn(a, b):
    return lax.dot_general(a, b, (((0,), (0,)), ((), ())), preferred_element_type=F32)


def _dot(a, b):
    return jnp.dot(a, b, preferred_element_type=F32)


def _col_rms(v):
    return lax.rsqrt(jnp.mean(v * v, axis=0, keepdims=True) + EPS)


class _Dims:
    def __init__(self, d_model, in_width, head_dim):
        self.d = d_model
        self.aw = d_model // 2
        self.gw = d_model - self.aw
        self.kvw = (in_width - self.aw - 2 * self.gw) // 2
        self.hd = head_dim
        self.nh = self.aw // head_dim
        self.nkv = self.kvw // head_dim
        self.grp = self.nh // self.nkv
        self.ng = self.gw // GROUP_DIM
        self.inw = in_width
        self.zoff = self.aw + 2 * self.kvw
        assert self.nkv == N_KV_HEADS and self.zoff + 2 * self.gw == in_width and self.aw % (2 * self.kvw) == 0


def _band_masks(first, grp):
    kj = lax.broadcasted_iota(jnp.int32, (2 * BLOCK, grp * BLOCK), 0)
    qi = lax.broadcasted_iota(jnp.int32, (2 * BLOCK, grp * BLOCK), 1) & (BLOCK - 1)
    dist = qi + BLOCK - kj
    valid = (dist >= 0) & (dist < BLOCK) & ((kj >= BLOCK) | jnp.logical_not(first))
    return valid, dist.astype(F32)


def _kv_band(dm, kh, p_ref, pkv_ref, gk2):
    ko = dm.aw + kh * dm.hd
    vo = dm.aw + dm.kvw + kh * dm.hd
    k_t = jnp.concatenate([pkv_ref[kh * dm.hd:(kh + 1) * dm.hd, :], p_ref[ko:ko + dm.hd, :]], axis=1)
    v_t = jnp.concatenate([pkv_ref[dm.kvw + kh * dm.hd:dm.kvw + (kh + 1) * dm.hd, :], p_ref[vo:vo + dm.hd, :]], axis=1)
    kn_t = k_t * _col_rms(k_t) * gk2
    return kn_t.astype(MXU), kn_t.T.astype(MXU), v_t.astype(MXU), v_t.T.astype(MXU)


def _group_heads(dm, kh):
    return range(kh * dm.grp, (kh + 1) * dm.grp)


def _lane_row(vals):
    return jnp.concatenate([jnp.full((1, BLOCK), v, F32) for v in vals], axis=1)


def _attn_group_fwd(dm, kh, p_ref, gq, kn, v_tb, sink_ref, valid, dist):
    heads = _group_heads(dm, kh)
    q = jnp.concatenate([p_ref[h * dm.hd:(h + 1) * dm.hd, :] for h in heads], axis=1)
    rq = _col_rms(q)
    qh = q * rq
    qnb = (qh * gq).astype(MXU)
    slopes = _lane_row([math.pow(2.0, -8.0 * (h + 1) / dm.nh) for h in heads])
    sinks = _lane_row([sink_ref[h] for h in heads])
    s = _dot(kn, qnb) * (dm.hd ** -0.5)
    logits = jnp.where(valid, s - slopes * dist, NEG)
    m = jnp.maximum(jnp.max(logits, axis=0, keepdims=True), sinks)
    e = jnp.exp(logits - m)
    es = jnp.exp(sinks - m)
    inv = 1.0 / (jnp.sum(e, axis=0, keepdims=True) + es)
    p = e * inv
    o = _dot(v_tb, p.astype(MXU))
    return o, p, es * inv, rq, qh, qnb


def _gelu_cdf(z):
    return 0.5 * (1.0 + lax.erf(z * _INV_SQRT2))


def _by_group(v, ng):
    return v.reshape(ng, GROUP_DIM, v.shape[1])


def _gate_fwd(dm, p_ref, lng_ref, lnb_ref, ws_ref, bs_ref, tril):
    zu, zv = p_ref[dm.zoff:dm.zoff + dm.gw, :], p_ref[dm.zoff + dm.gw:dm.zoff + 2 * dm.gw, :]
    u, v = zu * _gelu_cdf(zu), zv * _gelu_cdf(zv)
    v3 = _by_group(v, dm.ng)
    xc = v3 - jnp.mean(v3, axis=1, keepdims=True)
    rstd = lax.rsqrt(jnp.mean(xc * xc, axis=1, keepdims=True) + EPS)
    xh = (xc * rstd).reshape(dm.gw, BLOCK)
    vnb = (xh * lng_ref[...] + lnb_ref[...]).astype(MXU)
    wts = [jnp.where(tril, ws_ref[g], 0.0).astype(MXU) for g in range(dm.ng)]
    mixed = jnp.concatenate([_dot_nt(vnb[g * GROUP_DIM:(g + 1) * GROUP_DIM], wts[g]) + bs_ref[g:g + 1, :]
                             for g in range(dm.ng)], axis=0)
    return u * mixed, u, mixed, xh, rstd, vnb, wts, zu, zv


def _mixer_specs(dm, nb, clamp):
    kvblk = dm.aw // (2 * dm.kvw)

    def cur(s, i):
        return (0, s * nb + clamp(i))

    def prev(s, i):
        return (kvblk, s * nb + jnp.maximum(clamp(i) - 1, 0))

    full = lambda shape: pl.BlockSpec(shape, lambda s, i: tuple(0 for _ in shape))
    return cur, prev, full


def _tril():
    return lax.broadcasted_iota(jnp.int32, (BLOCK, BLOCK), 0) >= lax.broadcasted_iota(jnp.int32, (BLOCK, BLOCK), 1)


def _mixer_fwd(proj_t, prm, dm, nseq, nb):
    gq, gk2, sinks, lng, lnb, ws, bs, goa, gog = prm
    t = proj_t.shape[1]
    cur, prev, full = _mixer_specs(dm, nb, lambda i: i)

    def body(p_ref, pkv_ref, gq_ref, gk_ref, sink_ref, lng_ref, lnb_ref, ws_ref, bs_ref, goa_ref, gog_ref, y_ref, att_scr):
        i = pl.program_id(1)
        valid, dist = _band_masks(i == 0, dm.grp)
        gqv, gkv = gq_ref[...], gk_ref[...]
        for kh in range(dm.nkv):
            _, kn, v_tb, _ = _kv_band(dm, kh, p_ref, pkv_ref, gkv)
            o = _attn_group_fwd(dm, kh, p_ref, gqv, kn, v_tb, sink_ref, valid, dist)[0]
            for g, h in enumerate(_group_heads(dm, kh)):
                att_scr[h * dm.hd:(h + 1) * dm.hd, :] = o[:, g * BLOCK:(g + 1) * BLOCK]
        att = att_scr[...]
        y_ref[:dm.aw, :] = (att * _col_rms(att) * goa_ref[...]).astype(y_ref.dtype)
        gt = _gate_fwd(dm, p_ref, lng_ref, lnb_ref, ws_ref, bs_ref, _tril())[0]
        y_ref[dm.aw:, :] = (gt * _col_rms(gt) * gog_ref[...]).astype(y_ref.dtype)

    return pl.pallas_call(
        body, name="mixer_fwd", grid=(nseq, nb),
        in_specs=[pl.BlockSpec((dm.inw, BLOCK), cur), pl.BlockSpec((2 * dm.kvw, BLOCK), prev),
                  full(gq.shape), full(gk2.shape), pl.BlockSpec(memory_space=pltpu.SMEM),
                  full(lng.shape), full(lnb.shape), full(ws.shape), full(bs.shape), full(goa.shape), full(gog.shape)],
        out_specs=pl.BlockSpec((dm.d, BLOCK), cur),
        out_shape=jax.ShapeDtypeStruct((dm.d, t), MXU),
        scratch_shapes=[pltpu.VMEM((dm.aw, BLOCK), F32)],
        compiler_params=_cparams(("parallel", "arbitrary"), 40),
    )(proj_t, proj_t, gq, gk2, sinks, lng, lnb, ws, bs, goa, gog)


def _mixer_bwd(proj_t, dy_t, prm, dm, nseq, nb):
    gq, gk2, sinks, lng, lnb, ws, bs, goa, gog = prm
    t = proj_t.shape[1]
    clamp = lambda i: jnp.minimum(i, nb - 1)
    cur, prev, full = _mixer_specs(dm, nb, clamp)
    kvw2 = 2 * dm.kvw

    def prev_kv_out(s, i):
        return (0, s * nb + jnp.maximum(i - 1, 0))

    def body(p_ref, pkv_ref, dy_ref, gq_ref, gk_ref, sink_ref, lng_ref, lnb_ref, ws_ref, bs_ref, goa_ref, gog_ref,
             dproj_ref, dkv_ref, dgq_ref, dgk_ref, dsink_ref, dlng_ref, dlnb_ref, dws_ref, dbs_ref, dgoa_ref, dgog_ref,
             att_scr, datt_scr, carry_scr, prevpart_scr, curpart_scr, kprev_scr,
             a_gq, a_gk, a_sink, a_lng, a_lnb, a_goa, a_gog):
        s_id, i = pl.program_id(0), pl.program_id(1)
        lane_accs = ((a_gq, dgq_ref), (a_gk, dgk_ref), (a_sink, dsink_ref), (a_lng, dlng_ref), (a_lnb, dlnb_ref),
                     (a_goa, dgoa_ref), (a_gog, dgog_ref))

        @pl.when((s_id == 0) & (i == 0))
        def _():
            for acc, _ in lane_accs:
                acc[...] = jnp.zeros_like(acc)
            dws_ref[...] = jnp.zeros_like(dws_ref)
            dbs_ref[...] = jnp.zeros_like(dbs_ref)

        gqv, gkv = gq_ref[...], gk_ref[...]

        @pl.when(i < nb)
        def _():
            valid, dist = _band_masks(i == 0, dm.grp)
            kvs, fwd = [], []
            for kh in range(dm.nkv):
                kv = _kv_band(dm, kh, p_ref, pkv_ref, gkv)
                kvs.append(kv)
                fwd.append(_attn_group_fwd(dm, kh, p_ref, gqv, kv[1], kv[2], sink_ref, valid, dist))
                for g, h in enumerate(_group_heads(dm, kh)):
                    att_scr[h * dm.hd:(h + 1) * dm.hd, :] = fwd[kh][0][:, g * BLOCK:(g + 1) * BLOCK]
            att = att_scr[...]
            dya = dy_ref[:dm.aw, :]
            ra = _col_rms(att)
            ah = att * ra
            a_goa[...] += dya * ah
            dah = dya * goa_ref[...]
            datt_scr[...] = ra * (dah - ah * jnp.mean(dah * ah, axis=0, keepdims=True))
            for kh in range(dm.nkv):
                kn_tb, kn, v_tb, vb = kvs[kh]
                _, p, ps, rq, qh, qnb = fwd[kh]
                heads = _group_heads(dm, kh)
                do_b = jnp.concatenate([datt_scr[h * dm.hd:(h + 1) * dm.hd, :] for h in heads], axis=1).astype(MXU)
                dp = _dot(vb, do_b)
                delta = jnp.sum(p * dp, axis=0, keepdims=True)
                dsb = (p * (dp - delta) * (dm.hd ** -0.5)).astype(MXU)
                dsink = ps * delta
                dqn = _dot(kn_tb, dsb)
                dkn = _dot_nt(qnb, dsb)
                dvb = _dot_nt(do_b, p.astype(MXU))
                a_gq[...] += dqn * qh
                dqh = dqn * gqv
                dq = rq * (dqh - qh * jnp.mean(dqh * qh, axis=0, keepdims=True))
                for g, h in enumerate(heads):
                    a_sink[h:h + 1, :] += -dsink[:, g * BLOCK:(g + 1) * BLOCK]
                    dproj_ref[h * dm.hd:(h + 1) * dm.hd, :] = dq[:, g * BLOCK:(g + 1) * BLOCK].astype(dproj_ref.dtype)
                krows = slice(kh * dm.hd, (kh + 1) * dm.hd)
                vrows = slice(dm.kvw + kh * dm.hd, dm.kvw + (kh + 1) * dm.hd)
                prevpart_scr[krows, :] = dkn[:, :BLOCK]
                prevpart_scr[vrows, :] = dvb[:, :BLOCK]
                curpart_scr[krows, :] = dkn[:, BLOCK:]
                curpart_scr[vrows, :] = dvb[:, BLOCK:]
            dproj_ref[dm.aw:dm.zoff, :] = jnp.zeros((kvw2, BLOCK), dproj_ref.dtype)
            tril = _tril()
            gt, u, mixed, xh, rstd, vnb, wts, zu, zv = _gate_fwd(dm, p_ref, lng_ref, lnb_ref, ws_ref, bs_ref, tril)
            dyg = dy_ref[dm.aw:, :]
            rg = _col_rms(gt)
            gh = gt * rg
            a_gog[...] += dyg * gh
            dgh = dyg * gog_ref[...]
            dgt = rg * (dgh - gh * jnp.mean(dgh * gh, axis=0, keepdims=True))
            du = dgt * mixed
            dmix = dgt * u
            dmixb = dmix.astype(MXU)
            dbs_ref[...] += jnp.sum(_by_group(dmix, dm.ng), axis=1)
            dvn = []
            for g in range(dm.ng):
                rows = slice(g * GROUP_DIM, (g + 1) * GROUP_DIM)
                dws_ref[g] += jnp.where(tril, _dot_tn(dmixb[rows], vnb[rows]), 0.0)
                dvn.append(_dot(dmixb[rows], wts[g]))
            dvn = jnp.concatenate(dvn, axis=0)
            a_lng[...] += dvn * xh
            a_lnb[...] += dvn
            dxh3, xh3 = _by_group(dvn * lng_ref[...], dm.ng), _by_group(xh, dm.ng)
            dv = (rstd * (dxh3 - jnp.mean(dxh3, axis=1, keepdims=True) - xh3 * jnp.mean(dxh3 * xh3, axis=1, keepdims=True))).reshape(dm.gw, BLOCK)
            dgu = _gelu_cdf(zu) + zu * (jnp.exp(-0.5 * zu * zu) * _INV_SQRT_2PI)
            dgv = _gelu_cdf(zv) + zv * (jnp.exp(-0.5 * zv * zv) * _INV_SQRT_2PI)
            dproj_ref[dm.zoff:dm.zoff + dm.gw, :] = (du * dgu).astype(dproj_ref.dtype)
            dproj_ref[dm.zoff + dm.gw:, :] = (dv * dgv).astype(dproj_ref.dtype)

        @pl.when(i == nb)
        def _():
            prevpart_scr[...] = jnp.zeros_like(prevpart_scr)

        @pl.when(i >= 1)
        def _():
            tot = carry_scr[...] + prevpart_scr[...]
            for kh in range(dm.nkv):
                krows = slice(kh * dm.hd, (kh + 1) * dm.hd)
                kraw = kprev_scr[krows, :]
                rk = _col_rms(kraw)
                khat = kraw * rk
                dkn = tot[krows, :]
                a_gk[...] += dkn * khat
                dkh = dkn * gkv[:, :BLOCK]
                dk = rk * (dkh - khat * jnp.mean(dkh * khat, axis=0, keepdims=True))
                dkv_ref[krows, :] = dk.astype(dkv_ref.dtype)
            dkv_ref[dm.kvw:, :] = tot[dm.kvw:, :].astype(dkv_ref.dtype)

        @pl.when(i < nb)
        def _():
            carry_scr[...] = curpart_scr[...]
            kprev_scr[...] = p_ref[dm.aw:dm.aw + dm.kvw, :]

        @pl.when((s_id == nseq - 1) & (i == nb))
        def _():
            for acc, out in lane_accs:
                out[...] = jnp.sum(acc[...], axis=1, keepdims=True)

    col = lambda rows: jax.ShapeDtypeStruct((rows, 1), F32)
    lane = lambda rows: pltpu.VMEM((rows, LANES), F32)
    return pl.pallas_call(
        body, name="mixer_bwd", grid=(nseq, nb + 1),
        in_specs=[pl.BlockSpec((dm.inw, BLOCK), cur), pl.BlockSpec((kvw2, BLOCK), prev), pl.BlockSpec((dm.d, BLOCK), cur),
                  full(gq.shape), full(gk2.shape), pl.BlockSpec(memory_space=pltpu.SMEM),
                  full(lng.shape), full(lnb.shape), full(ws.shape), full(bs.shape), full(goa.shape), full(gog.shape)],
        out_specs=[pl.BlockSpec((dm.inw, BLOCK), cur), pl.BlockSpec((kvw2, BLOCK), prev_kv_out),
                   full((dm.hd, 1)), full((dm.hd, 1)), full((dm.nh, 1)), full((dm.gw, 1)), full((dm.gw, 1)), full(ws.shape),
                   full(bs.shape), full((dm.aw, 1)), full((dm.gw, 1))],
        out_shape=[jax.ShapeDtypeStruct((dm.inw, t), MXU), jax.ShapeDtypeStruct((kvw2, t), MXU),
                   col(dm.hd), col(dm.hd), col(dm.nh), col(dm.gw), col(dm.gw), jax.ShapeDtypeStruct(ws.shape, F32),
                   jax.ShapeDtypeStruct(bs.shape, F32), col(dm.aw), col(dm.gw)],
        scratch_shapes=[pltpu.VMEM((dm.aw, BLOCK), F32), pltpu.VMEM((dm.aw, BLOCK), F32),
                        pltpu.VMEM((kvw2, BLOCK), F32), pltpu.VMEM((kvw2, BLOCK), F32), pltpu.VMEM((kvw2, BLOCK), F32),
                        pltpu.VMEM((dm.kvw, BLOCK), F32),
                        pltpu.VMEM((dm.hd, dm.grp * BLOCK), F32), lane(dm.hd), lane(dm.nh), lane(dm.gw), lane(dm.gw), lane(dm.aw), lane(dm.gw)],
        compiler_params=_cparams(("arbitrary", "arbitrary"), 48),
    )(proj_t, proj_t, dy_t, gq, gk2, sinks, lng, lnb, ws, bs, goa, gog)


def _patch_kv(dproj_t, dkv_t, dm):
    t = dproj_t.shape[1]
    tc = _pick(t, (1024, 512, 256, 128))
    kvw2 = 2 * dm.kvw
    kvblk = dm.aw // kvw2

    def body(dproj_hbm, dkv_ref, out_ref):
        del dproj_hbm
        out_ref[...] = dkv_ref[...]

    return pl.pallas_call(
        body, name="patch_kv", grid=(t // tc,),
        in_specs=[ANY, pl.BlockSpec((kvw2, tc), lambda i: (0, i))],
        out_specs=pl.BlockSpec((kvw2, tc), lambda i: (kvblk, i)),
        out_shape=jax.ShapeDtypeStruct(dproj_t.shape, dproj_t.dtype),
        input_output_aliases={0: 0},
        compiler_params=_cparams(("parallel",), 32),
    )(dproj_t, dkv_t)


def _place():
    x, y, c = lax.axis_index("x"), lax.axis_index("y"), lax.axis_index("c")
    return x, y, c


def _handshake(peers):
    barrier = pltpu.get_barrier_semaphore()
    for p in peers:
        pl.semaphore_signal(barrier, inc=1, device_id=p, device_id_type=MESH)
    pl.semaphore_wait(barrier, len(peers))


def _sequencer_mesh():
    return plsc.ScalarSubcoreMesh(axis_name="sequencer", num_cores=1)


GATHER_CHUNKS = 4
BF16_ROWS = 16


def _row_chunks(n, k):
    tiles = n // BF16_ROWS
    sizes = [(tiles // k + (1 if i < tiles % k else 0)) * BF16_ROWS for i in range(k)]
    return [(sum(sizes[:i]), sz) for i, sz in enumerate(sizes) if sz]


def _allgather_weight(name, collective_id, shard, after=None):
    n = shard.shape[0]
    assert n % BF16_ROWS == 0
    chunks = _row_chunks(n, GATHER_CHUNKS)
    nc = len(chunks)

    def body(*refs):
        src, out = refs[0], refs[-4]
        send_sems, recv_sems, local_sem = refs[-3:]
        x, y, c = _place()
        me, sib, xn, yn, diag = (x, y, c), (x, y, 1 - c), (1 - x, y, c), (x, 1 - y, c), (1 - x, 1 - y, c)
        relay_to = (x ^ c, y ^ (1 - c), c)
        relay_of = (x ^ (1 - c), y ^ c, c)
        _handshake([sib, xn, yn])

        def rows(place, ci):
            px, py, pc = place
            off, size = chunks[ci]
            return out.at[pl.ds(pl.multiple_of((4 * px + 2 * py + pc) * n + off, BF16_ROWS), size), :]

        def copy(k, ci, block, to, from_src=False):
            off, size = chunks[ci]
            return pltpu.make_async_remote_copy(
                src_ref=src.at[pl.ds(off, size), :] if from_src else rows(block, ci), dst_ref=rows(block, ci),
                send_sem=send_sems.at[ci, k], recv_sem=recv_sems.at[ci, k], device_id=to, device_id_type=MESH)

        mine = pltpu.make_async_copy(src, out.at[pl.ds(pl.multiple_of((4 * x + 2 * y + c) * n, BF16_ROWS), n), :], local_sem)
        mine.start()
        sent = []
        for ci in range(nc):
            sent += [copy(0, ci, me, sib, from_src=True), copy(1, ci, me, xn, from_src=True), copy(2, ci, me, yn, from_src=True)]
        for cp in sent:
            cp.start()
        for ci in range(nc):
            copy(1, ci, xn, me).wait_recv()
            copy(2, ci, yn, me).wait_recv()
            passed = [copy(3, ci, relay_of, relay_to), copy(4, ci, xn, sib), copy(5, ci, yn, sib)]
            for cp in passed:
                cp.start()
            sent += passed
        for ci in range(nc):
            copy(3, ci, diag, me).wait_recv()
            passed = copy(6, ci, diag, sib)
            passed.start()
            sent.append(passed)
        for ci in range(nc):
            copy(0, ci, sib, me).wait_recv()
            for k, block in ((4, (1 - x, y, 1 - c)), (5, (x, 1 - y, 1 - c)), (6, (1 - x, 1 - y, 1 - c))):
                copy(k, ci, block, me).wait_recv()
        for cp in sent:
            cp.wait_send()
        mine.wait()

    return pl.kernel(
        body, name=name,
        out_type=jax.ShapeDtypeStruct((N_DEV * n, shard.shape[1]), shard.dtype),
        mesh=_sequencer_mesh(),
        scratch_types=[pltpu.SemaphoreType.DMA((nc, 7)), pltpu.SemaphoreType.DMA((nc, 7)), pltpu.SemaphoreType.DMA],
        compiler_params=pltpu.CompilerParams(collective_id=collective_id),
    )(*([shard] if after is None else [shard, after]))


_FLIPS = [(0, 0, 1), (1, 0, 0), (0, 1, 0), (1, 1, 0), (1, 0, 1), (0, 1, 1), (1, 1, 1)]


def _scatter_grad(name, collective_id, grad, rows=None):
    n = grad.shape[0] // N_DEV
    with_rows = rows is not None

    def body(*refs):
        src, out = refs[0], refs[2 if with_rows else 1]
        send_sems, recv_sems, local_sems = refs[-3:]
        x, y, c = _place()
        me_idx = 4 * x + 2 * y + c
        peers = [(x ^ fx, y ^ fy, c ^ fc) for (fx, fy, fc) in _FLIPS]
        _handshake(peers)

        def block(idx):
            return src.at[pl.ds(pl.multiple_of(idx * n, 16), n), :]

        copies = [pltpu.make_async_remote_copy(
            src_ref=block(4 * px + 2 * py + pc), dst_ref=out.at[me_idx], send_sem=send_sems.at[k], recv_sem=recv_sems.at[k],
            device_id=(px, py, pc), device_id_type=MESH) for k, (px, py, pc) in enumerate(peers)]
        mine = [pltpu.make_async_copy(block(me_idx), out.at[me_idx], local_sems.at[0])]
        if with_rows:
            rows_src, rows_out = refs[1], refs[3]
            copies += [pltpu.make_async_remote_copy(
                src_ref=rows_src, dst_ref=rows_out.at[me_idx], send_sem=send_sems.at[7 + k], recv_sem=recv_sems.at[7 + k],
                device_id=peer, device_id_type=MESH) for k, peer in enumerate(peers)]
            mine.append(pltpu.make_async_copy(rows_src, rows_out.at[me_idx], local_sems.at[1]))
        for cp in mine + copies:
            cp.start()
        for cp in copies:
            cp.wait_recv()
        for cp in copies:
            cp.wait_send()
        for cp in mine:
            cp.wait()

    slots = jax.ShapeDtypeStruct((N_DEV, n, grad.shape[1]), grad.dtype)
    return pl.kernel(
        body, name=name,
        out_type=[slots, jax.ShapeDtypeStruct((N_DEV,) + rows.shape, rows.dtype)] if with_rows else slots,
        mesh=_sequencer_mesh(),
        scratch_types=[pltpu.SemaphoreType.DMA((14,)), pltpu.SemaphoreType.DMA((14,)), pltpu.SemaphoreType.DMA((2,))],
        compiler_params=pltpu.CompilerParams(collective_id=collective_id),
    )(*([grad, rows] if with_rows else [grad]))


def _adamw_math(w, g, m, v):
    m = ADAM_B1 * m + (1.0 - ADAM_B1) * g
    v = ADAM_B2 * v + (1.0 - ADAM_B2) * (g * g)
    m_hat = m / (1.0 - ADAM_B1 ** ADAM_STEP)
    v_hat = v / (1.0 - ADAM_B2 ** ADAM_STEP)
    delta = -ADAM_LR * (m_hat / (jnp.sqrt(v_hat) + ADAM_EPS) + ADAM_WD * w)
    return delta, m, v


def _sum_adamw(name, slots, w, m, v, after):
    _, n, kk = slots.shape
    tr = _pick(n, (208, 176, 128, 96, 64, 32, 16))

    def body(s_ref, w_ref, m_ref, v_ref, after_ref, g_ref, d_ref, nm_ref, nv_ref):
        del after_ref
        g = s_ref[0].astype(F32)
        for p in range(1, N_DEV):
            g = g + s_ref[p].astype(F32)
        g_ref[...] = g
        d_ref[...], nm_ref[...], nv_ref[...] = _adamw_math(w_ref[...], g, m_ref[...], v_ref[...])

    row = pl.BlockSpec((tr, kk), lambda i: (i, 0))
    return pl.pallas_call(
        body, name=name, grid=(n // tr,),
        in_specs=[pl.BlockSpec((N_DEV, tr, kk), lambda i: (0, i, 0)), row, row, row, ANY],
        out_specs=[row] * 4,
        out_shape=[jax.ShapeDtypeStruct((n, kk), F32)] * 4,
        compiler_params=_cparams(("parallel",), 48),
    )(slots, w, m, v, after)


def _allreduce_small_adamw(early_slots, late, w, m, v, after):
    ra, rb = early_slots.shape[1], late.shape[0]

    def body(early_ref, late_ref, w_ref, m_ref, v_ref, after_ref, g_ref, d_ref, nm_ref, nv_ref, slots, send_sems, recv_sems):
        del after_ref
        x, y, c = _place()
        me_idx = 4 * x + 2 * y + c
        copies = []
        for k, (fx, fy, fc) in enumerate(_FLIPS):
            px, py, pc = x ^ fx, y ^ fy, c ^ fc
            copies.append(pltpu.make_async_remote_copy(
                src_ref=late_ref, dst_ref=slots.at[me_idx], send_sem=send_sems.at[k], recv_sem=recv_sems.at[k],
                device_id=(px, py, pc), device_id_type=MESH))
        for cp in copies:
            cp.start()
        slots[me_idx] = late_ref[...]
        g = early_ref[0]
        for p in range(1, N_DEV):
            g = g + early_ref[p]
        early = pl.ds(0, ra)
        g_ref[early, :] = g
        d_ref[early, :], nm_ref[early, :], nv_ref[early, :] = _adamw_math(w_ref[early, :], g, m_ref[early, :], v_ref[early, :])
        for cp in copies:
            cp.wait_recv()
        for cp in copies:
            cp.wait_send()
        g = slots[0]
        for p in range(1, N_DEV):
            g = g + slots[p]
        tail = pl.ds(ra, rb)
        g_ref[tail, :] = g
        d_ref[tail, :], nm_ref[tail, :], nv_ref[tail, :] = _adamw_math(w_ref[tail, :], g, m_ref[tail, :], v_ref[tail, :])

    vm = pl.BlockSpec(memory_space=pltpu.VMEM)
    return pl.pallas_call(
        body, name="allreduce_small_adamw",
        in_specs=[vm] * 5 + [ANY], out_specs=[vm] * 4,
        out_shape=[jax.ShapeDtypeStruct((ra + rb, LANES), F32)] * 4,
        scratch_shapes=[pltpu.VMEM((N_DEV, rb, LANES), F32), pltpu.SemaphoreType.DMA((7,)), pltpu.SemaphoreType.DMA((7,))],
        compiler_params=pltpu.CompilerParams(vmem_limit_bytes=48 * MIB),
    )(early_slots, late, w, m, v, after)


def _pack(arrs):
    parts, meta, off = [], [], 0
    for a in arrs:
        flat = a.reshape(-1).astype(F32)
        rows = -(-flat.shape[0] // LANES)
        rows8 = -(-rows // 8) * 8
        flat = jnp.pad(flat, (0, rows8 * LANES - flat.shape[0]))
        parts.append(flat.reshape(rows8, LANES))
        meta.append((off, a.shape, a.size))
        off += rows8
    return jnp.concatenate(parts, axis=0), meta


def _unpack(packed, meta):
    outs = []
    for off, shape, size in meta:
        rows = -(-size // LANES)
        outs.append(packed[off:off + rows].reshape(-1)[:size].reshape(shape))
    return outs


def _silu_parts(a):
    sg = 0.5 + 0.5 * jnp.tanh(0.5 * a)
    return a * sg, sg * (1.0 + a * (1.0 - sg))


def kernel(x, norm1_g, w_in, q_norm_g, k_norm_g, attn_sinks, gate_ln_g, gate_ln_b, w_spatial, b_spatial, out_norm_attn_g, out_norm_gate_g, w_out, norm2_g, w_ffn_gate, w_ffn_up, w_ffn_down, loss_target, m_norm1_g, m_w_in, m_q_norm_g, m_k_norm_g, m_attn_sinks, m_gate_ln_g, m_gate_ln_b, m_w_spatial, m_b_spatial, m_out_norm_attn_g, m_out_norm_gate_g, m_w_out, m_norm2_g, m_w_ffn_gate, m_w_ffn_up, m_w_ffn_down, v_norm1_g, v_w_in, v_q_norm_g, v_k_norm_g, v_attn_sinks, v_gate_ln_g, v_gate_ln_b, v_w_spatial, v_b_spatial, v_out_norm_attn_g, v_out_norm_gate_g, v_w_out, v_norm2_g, v_w_ffn_gate, v_w_ffn_up, v_w_ffn_down):
    nseq, seq, d = x.shape
    t = nseq * seq
    nb = seq // BLOCK
    inw = w_in.shape[2] * N_DEV
    dm = _Dims(d, inw, q_norm_g.shape[-1])
    xf = x.reshape(t, d)
    tgt = loss_target.reshape(t, d)

    rows = lambda wv, transposed: jnp.swapaxes(wv, 1, 2)[0] if transposed else wv[0]
    big = {"w_in": (w_in, m_w_in, v_w_in, True), "w_out": (w_out, m_w_out, v_w_out, False),
           "w_ffn_gate": (w_ffn_gate, m_w_ffn_gate, v_w_ffn_gate, True), "w_ffn_up": (w_ffn_up, m_w_ffn_up, v_w_ffn_up, True),
           "w_ffn_down": (w_ffn_down, m_w_ffn_down, v_w_ffn_down, False)}
    big_rows = {nm: tuple(rows(arr, tr) for arr in (wv, mv, vv)) for nm, (wv, mv, vv, tr) in big.items()}
    shard = lambda nm: big_rows[nm][0].astype(WIRE)
    win_t = _allgather_weight("gather_w_in", 1, shard("w_in"))
    wout = _allgather_weight("gather_w_out", 2, shard("w_out"), after=win_t)
    wg_t = _allgather_weight("gather_w_ffn_gate", 3, shard("w_ffn_gate"), after=win_t)
    wu_t = _allgather_weight("gather_w_ffn_up", 9, shard("w_ffn_up"), after=win_t)
    wd = _allgather_weight("gather_w_ffn_down", 10, shard("w_ffn_down"), after=win_t)

    lanes = lambda v, n=BLOCK: jnp.broadcast_to(v.reshape(-1, 1), (v.size, n))
    prm = (lanes(q_norm_g, dm.grp * BLOCK), lanes(k_norm_g, 2 * BLOCK), attn_sinks[0], lanes(gate_ln_g), lanes(gate_ln_b), w_spatial[0], b_spatial[0],
           lanes(out_norm_attn_g), lanes(out_norm_gate_g))

    h1 = _rms_fwd("rms1_fwd", xf, norm1_g)
    (proj_t,) = _matmul("mm_in", win_t, h1, "nt", [F32])
    y_t = _mixer_fwd(proj_t, prm, dm, nseq, nb)

    def residual_norm(acc, xr, g2):
        x2v = xr + acc
        return x2v, x2v * lax.rsqrt(jnp.mean(x2v * x2v, axis=-1, keepdims=True) + EPS) * g2

    x2, h2 = _matmul("mm_out", y_t, wout, "tn", [F32, MXU], epilogue=residual_norm, extras=[xf], rowvecs=[norm2_g], full_rows=True)
    a, b, s = _matmul("mm_gate_up", h2, wg_t, "nt", [MXU, MXU, MXU], b2=wu_t, epilogue=lambda ga, ub: (ga, ub, _silu_parts(ga)[0] * ub))

    def loss_epilogue(acc, x2v, tv):
        diff = (x2v + acc) - tv
        dx3 = diff * (1.0 / d)
        return dx3, dx3, jnp.sum(diff * diff)

    dx3, dx3b, lossp = _matmul("mm_down", s, wd, "nn", [F32, MXU], epilogue=loss_epilogue, extras=[x2, tgt], partial=True)
    loss_part = (0.5 / d) * jnp.sum(lossp[::8, ::LANES])

    def dswiglu(acc, av, bv):
        silu, dsilu = _silu_parts(av.astype(F32))
        return acc * bv.astype(F32) * dsilu, acc * silu

    da, db = _matmul("mm_d_down", dx3b, wd, "nt", [MXU, MXU], epilogue=dswiglu, extras=[a, b])
    (g_wd,) = _matmul("mm_gw_down", s, dx3b, "tn", [WIRE])
    sl_wd = _scatter_grad("scatter_w_ffn_down", 4, g_wd)
    (dh2a,) = _matmul("mm_dh2_gate", da, wg_t, "nn", [F32], after=[g_wd])
    (g_wg,) = _matmul("mm_gw_gate", da, h2, "tn", [WIRE], after=[dh2a])
    sl_wg = _scatter_grad("scatter_w_ffn_gate", 5, g_wg)
    (dh2,) = _matmul("mm_dh2_up", db, wu_t, "nn", [F32], epilogue=lambda acc, pv: (pv + acc,), extras=[dh2a], after=[g_wg])
    (g_wu,) = _matmul("mm_gw_up", db, h2, "tn", [WIRE], after=[dh2])
    sl_wu = _scatter_grad("scatter_w_ffn_up", 6, g_wu)
    dx2, dx2b, dg2 = _rms_bwd("rms2_bwd", dh2, x2, norm2_g, dx3)

    (dy_t,) = _matmul("mm_d_out", wout, dx2b, "nt", [F32], after=[g_wu])
    (g_wout,) = _matmul("mm_gw_out", y_t, dx2b, "nn", [WIRE], after=[dy_t])
    sl_wout = _scatter_grad("scatter_w_out", 7, g_wout)
    (dproj0, dkv, dgq, dgk, dsink, dlng, dlnb, dws, dbs, dgoa, dgog) = _mixer_bwd(proj_t, dy_t, prm, dm, nseq, nb)
    dproj_t = _patch_kv(dproj0, dkv, dm)
    (g_win,) = _matmul("mm_gw_in", dproj_t, h1, "nn", [WIRE])
    early_g = [dgq, dgk, dsink, dlng, dlnb, dws, dbs, dgoa, dgog, dg2, loss_part.reshape(1)]
    sl_win, early_slots = _scatter_grad("scatter_w_in", 8, g_win, rows=_pack(early_g)[0])

    def norm1_backward(dh1, xv, dx2v, g1):
        r = lax.rsqrt(jnp.mean(xv * xv, axis=-1, keepdims=True) + EPS)
        xh = xv * r
        dxh = dh1 * g1
        return dx2v + r * (dxh - xh * jnp.mean(dxh * xh, axis=-1, keepdims=True)), jnp.sum(dh1 * xh, axis=0, keepdims=True)

    dx, dg1 = _matmul("mm_d_in", dproj_t, win_t, "tn", [F32], epilogue=norm1_backward, extras=[xf, dx2], rowvecs=[norm1_g],
                      after=[g_win], col_sum=True, full_rows=True)

    big_out = {}
    last = dx
    for nm, sl in (("w_ffn_down", sl_wd), ("w_ffn_gate", sl_wg), ("w_ffn_up", sl_wu), ("w_out", sl_wout), ("w_in", sl_win)):
        res = _sum_adamw("adamw_" + nm, sl, *big_rows[nm], after=last)
        last = res[1]
        big_out[nm] = tuple(jnp.swapaxes(r[None], 1, 2) if big[nm][3] else r[None] for r in res)

    zero = jnp.zeros((1,), F32)
    small_names = ["q_norm_g", "k_norm_g", "attn_sinks", "gate_ln_g", "gate_ln_b", "w_spatial", "b_spatial",
                   "out_norm_attn_g", "out_norm_gate_g", "norm2_g", "loss", "norm1_g"]
    small_w = [q_norm_g, k_norm_g, attn_sinks, gate_ln_g, gate_ln_b, w_spatial, b_spatial, out_norm_attn_g, out_norm_gate_g, norm2_g, zero, norm1_g]
    small_m = [m_q_norm_g, m_k_norm_g, m_attn_sinks, m_gate_ln_g, m_gate_ln_b, m_w_spatial, m_b_spatial, m_out_norm_attn_g, m_out_norm_gate_g, m_norm2_g, zero, m_norm1_g]
    small_v = [v_q_norm_g, v_k_norm_g, v_attn_sinks, v_gate_ln_g, v_gate_ln_b, v_w_spatial, v_b_spatial, v_out_norm_attn_g, v_out_norm_gate_g, v_norm2_g, zero, v_norm1_g]
    pw, meta = _pack(small_w)
    sg, sd, sm, sv = _allreduce_small_adamw(early_slots, _pack([dg1])[0], pw, _pack(small_m)[0], _pack(small_v)[0], after=last)
    ug, ud, um, uv = _unpack(sg, meta), _unpack(sd, meta), _unpack(sm, meta), _unpack(sv, meta)
    small_out = {nm: (ug[k], ud[k], um[k], uv[k]) for k, nm in enumerate(small_names)}
    loss = small_out["loss"][0].reshape(())

    order = ["norm1_g", "w_in", "q_norm_g", "k_norm_g", "attn_sinks", "gate_ln_g", "gate_ln_b", "w_spatial", "b_spatial",
             "out_norm_attn_g", "out_norm_gate_g", "w_out", "norm2_g", "w_ffn_gate", "w_ffn_up", "w_ffn_down"]
    allo = {**big_out, **small_out}
    outs = [loss, dx.reshape(nseq, seq, d)]
    for k in range(4):
        outs += [allo[nm][k] for nm in order]
    return tuple(outs)
```

```python
import math

import jax
import jax.numpy as jnp
from jax import lax
from jax.experimental import pallas as pl
from jax.experimental.pallas import tpu as pltpu
from jax.experimental.pallas import tpu_sc as plsc

F32 = jnp.float32
MXU = jnp.bfloat16
WIRE = jnp.bfloat16
EPS = 1e-6
BLOCK = 128
GROUP_DIM = 128
N_KV_HEADS = 2
NEG = -1e30
N_DEV = 8
LANES = 128
MIB = 1024 * 1024

ADAM_LR = 0.001
ADAM_B1 = 0.9
ADAM_B2 = 0.999
ADAM_EPS = 1e-08
ADAM_WD = 0.01
ADAM_STEP = 10

MESH = pl.DeviceIdType.MESH
ANY = pl.BlockSpec(memory_space=pl.ANY)


def _pick(n, cands):
    for c in cands:
        if n % c == 0:
            return c
    return n


def _cparams(sem, vmem_mb):
    return pltpu.CompilerParams(dimension_semantics=sem, vmem_limit_bytes=vmem_mb * MIB)


VMEM_TILE_BUDGET = 44 * MIB
HBM_BYTES_PER_US = 3.0e6
STEP_US = 0.4
MIN_TILE_N = 512


def _tile_candidates(n):
    return [c for c in range(min(n, 2048), 0, -LANES) if n % c == 0 and c % LANES == 0] or [n]


def _matmul_tiles(m, n, kk, esz, n_b, extra_sizes, out_sizes, full_rows):
    best = None
    wide = [n] if full_rows else [c for c in _tile_candidates(n) if c >= MIN_TILE_N] or _tile_candidates(n)
    for tm in _tile_candidates(m):
        for tn in wide:
            vmem = 2 * (tm + n_b * tn) * kk * esz + tm * tn * (4 * n_b + 2 * sum(extra_sizes) + 2 * sum(out_sizes))
            if vmem > VMEM_TILE_BUDGET:
                continue
            cost = (m // tm) * n_b * n * kk * esz / HBM_BYTES_PER_US + (m // tm) * (n // tn) * STEP_US
            if best is None or cost < best[0]:
                best = (cost, tm, tn, vmem)
    assert best is not None, (m, n, kk)
    return best[1:]


def _matmul(name, a, b, mode, out_dtypes, epilogue=None, extras=(), rowvecs=(), after=(), partial=False, col_sum=False, full_rows=False, b2=None):
    if mode == "nn":
        (m, kk), n = a.shape, b.shape[1]
        dn = (((1,), (0,)), ((), ()))
    elif mode == "nt":
        (m, kk), n = a.shape, b.shape[0]
        dn = (((1,), (1,)), ((), ()))
    else:
        (kk, m), n = a.shape, b.shape[1]
        dn = (((0,), (0,)), ((), ()))
    bs = [b] if b2 is None else [b, b2]
    tm, tn, vmem = _matmul_tiles(m, n, kk, a.dtype.itemsize, len(bs), [e.dtype.itemsize for e in extras],
                                 [jnp.dtype(dt).itemsize for dt in out_dtypes], full_rows)
    a_spec = pl.BlockSpec((kk, tm), lambda i, j: (0, i)) if mode == "tn" else pl.BlockSpec((tm, kk), lambda i, j: (i, 0))
    b_spec = pl.BlockSpec((tn, kk), lambda i, j: (j, 0)) if mode == "nt" else pl.BlockSpec((kk, tn), lambda i, j: (0, j))
    tile = pl.BlockSpec((tm, tn), lambda i, j: (i, j))
    row = pl.BlockSpec((1, tn), lambda i, j: (0, j))
    nb, ne, nr, na, no = len(bs), len(extras), len(rowvecs), len(after), len(out_dtypes)

    def body(a_ref, *rest):
        b_refs, in_refs, out_refs = rest[:nb], rest[nb:nb + ne + nr], rest[nb + ne + nr + na:]
        av = a_ref[...]
        accs = [lax.dot_general(av, b_ref[...], dn, preferred_element_type=F32) for b_ref in b_refs]
        vals = tuple(accs) if epilogue is None else epilogue(*accs, *[r[...] for r in in_refs])
        for o_ref, t in zip(out_refs[:no], vals[:no]):
            o_ref[...] = t.astype(o_ref.dtype)
        if partial:
            out_refs[no][...] = jnp.full((8, LANES), vals[no], F32)
        if col_sum:
            sum_ref = out_refs[-1]

            @pl.when(pl.program_id(0) == 0)
            def _():
                sum_ref[...] = jnp.zeros_like(sum_ref)

            sum_ref[...] += vals[-1]

    out_specs = [tile] * no
    out_shape = [jax.ShapeDtypeStruct((m, n), dt) for dt in out_dtypes]
    if partial:
        out_specs.append(pl.BlockSpec((8, LANES), lambda i, j: (i, j)))
        out_shape.append(jax.ShapeDtypeStruct((m // tm * 8, n // tn * LANES), F32))
    if col_sum:
        out_specs.append(row)
        out_shape.append(jax.ShapeDtypeStruct((1, n), F32))
    return pl.pallas_call(
        body, name=name, grid=(m // tm, n // tn),
        in_specs=[a_spec] + [b_spec] * nb + [tile] * ne + [row] * nr + [ANY] * na,
        out_specs=out_specs, out_shape=out_shape,
        compiler_params=_cparams(("arbitrary" if col_sum else "parallel", "arbitrary"), min(vmem // MIB + 8, 60)),
    )(a, *bs, *extras, *rowvecs, *after)


def _rms_fwd(name, x, g):
    t, d = x.shape
    tm = _pick(t, (512, 256, 128))

    def body(x_ref, g_ref, h_ref):
        xv = x_ref[...]
        r = lax.rsqrt(jnp.mean(xv * xv, axis=-1, keepdims=True) + EPS)
        h_ref[...] = (xv * r * g_ref[...]).astype(h_ref.dtype)

    return pl.pallas_call(
        body, name=name, grid=(t // tm,),
        in_specs=[pl.BlockSpec((tm, d), lambda i: (i, 0)), pl.BlockSpec((1, d), lambda i: (0, 0))],
        out_specs=pl.BlockSpec((tm, d), lambda i: (i, 0)),
        out_shape=jax.ShapeDtypeStruct((t, d), MXU),
        compiler_params=_cparams(("parallel",), 32),
    )(x, g)


def _rms_bwd(name, dh, x, g, res):
    t, d = x.shape
    tm = _pick(t, (256, 128))

    def body(dh_ref, x_ref, g_ref, res_ref, dx_ref, dxb_ref, dg_ref):
        @pl.when(pl.program_id(0) == 0)
        def _():
            dg_ref[...] = jnp.zeros_like(dg_ref)

        xv, dhv = x_ref[...], dh_ref[...]
        r = lax.rsqrt(jnp.mean(xv * xv, axis=-1, keepdims=True) + EPS)
        xh = xv * r
        dg_ref[...] += jnp.sum(dhv * xh, axis=0, keepdims=True)
        dxh = dhv * g_ref[...]
        dx = res_ref[...] + r * (dxh - xh * jnp.mean(dxh * xh, axis=-1, keepdims=True))
        dx_ref[...] = dx
        dxb_ref[...] = dx.astype(dxb_ref.dtype)

    row = pl.BlockSpec((tm, d), lambda i: (i, 0))
    vec = pl.BlockSpec((1, d), lambda i: (0, 0))
    return pl.pallas_call(
        body, name=name, grid=(t // tm,),
        in_specs=[row, row, vec, row],
        out_specs=[row, row, vec],
        out_shape=[jax.ShapeDtypeStruct((t, d), F32), jax.ShapeDtypeStruct((t, d), MXU), jax.ShapeDtypeStruct((1, d), F32)],
        compiler_params=_cparams(("arbitrary",), 40),
    )(dh, x, g, res)


_INV_SQRT2 = 0.7071067811865476
_INV_SQRT_2PI = 0.3989422804014327


def _dot_nt(a, b):
    return lax.dot_general(a, b, (((1,), (1,)), ((), ())), preferred_element_type=F32)


def _dot_tn(a, b):
    return lax.dot_general(a, b, (((0,), (0,)), ((), ())), preferred_element_type=F32)


def _dot(a, b):
    return jnp.dot(a, b, preferred_element_type=F32)


def _col_rms(v):
    return lax.rsqrt(jnp.mean(v * v, axis=0, keepdims=True) + EPS)


class _Dims:
    def __init__(self, d_model, in_width, head_dim):
        self.d = d_model
        self.aw = d_model // 2
        self.gw = d_model - self.aw
        self.kvw = (in_width - self.aw - 2 * self.gw) // 2
        self.hd = head_dim
        self.nh = self.aw // head_dim
        self.nkv = self.kvw // head_dim
        self.grp = self.nh // self.nkv
        self.ng = self.gw // GROUP_DIM
        self.inw = in_width
        self.zoff = self.aw + 2 * self.kvw
        assert self.nkv == N_KV_HEADS and self.zoff + 2 * self.gw == in_width and self.aw % (2 * self.kvw) == 0


def _band_masks(first, grp):
    kj = lax.broadcasted_iota(jnp.int32, (2 * BLOCK, grp * BLOCK), 0)
    qi = lax.broadcasted_iota(jnp.int32, (2 * BLOCK, grp * BLOCK), 1) & (BLOCK - 1)
    dist = qi + BLOCK - kj
    valid = (dist >= 0) & (dist < BLOCK) & ((kj >= BLOCK) | jnp.logical_not(first))
    return valid, dist.astype(F32)


def _kv_band(dm, kh, p_ref, pkv_ref, gk2):
    ko = dm.aw + kh * dm.hd
    vo = dm.aw + dm.kvw + kh * dm.hd
    k_t = jnp.concatenate([pkv_ref[kh * dm.hd:(kh + 1) * dm.hd, :], p_ref[ko:ko + dm.hd, :]], axis=1)
    v_t = jnp.concatenate([pkv_ref[dm.kvw + kh * dm.hd:dm.kvw + (kh + 1) * dm.hd, :], p_ref[vo:vo + dm.hd, :]], axis=1)
    kn_t = k_t * _col_rms(k_t) * gk2
    return kn_t.astype(MXU), kn_t.T.astype(MXU), v_t.astype(MXU), v_t.T.astype(MXU)


def _group_heads(dm, kh):
    return range(kh * dm.grp, (kh + 1) * dm.grp)


def _lane_row(vals):
    return jnp.concatenate([jnp.full((1, BLOCK), v, F32) for v in vals], axis=1)


def _attn_group_fwd(dm, kh, p_ref, gq, kn, v_tb, sink_ref, valid, dist):
    heads = _group_heads(dm, kh)
    q = jnp.concatenate([p_ref[h * dm.hd:(h + 1) * dm.hd, :] for h in heads], axis=1)
    rq = _col_rms(q)
    qh = q * rq
    qnb = (qh * gq).astype(MXU)
    slopes = _lane_row([math.pow(2.0, -8.0 * (h + 1) / dm.nh) for h in heads])
    sinks = _lane_row([sink_ref[h] for h in heads])
    s = _dot(kn, qnb) * (dm.hd ** -0.5)
    logits = jnp.where(valid, s - slopes * dist, NEG)
    m = jnp.maximum(jnp.max(logits, axis=0, keepdims=True), sinks)
    e = jnp.exp(logits - m)
    es = jnp.exp(sinks - m)
    inv = 1.0 / (jnp.sum(e, axis=0, keepdims=True) + es)
    p = e * inv
    o = _dot(v_tb, p.astype(MXU))
    return o, p, es * inv, rq, qh, qnb


def _gelu_cdf(z):
    return 0.5 * (1.0 + lax.erf(z * _INV_SQRT2))


def _by_group(v, ng):
    return v.reshape(ng, GROUP_DIM, v.shape[1])


def _gate_fwd(dm, p_ref, lng_ref, lnb_ref, ws_ref, bs_ref, tril):
    zu, zv = p_ref[dm.zoff:dm.zoff + dm.gw, :], p_ref[dm.zoff + dm.gw:dm.zoff + 2 * dm.gw, :]
    cu, cv = _gelu_cdf(zu), _gelu_cdf(zv)
    u, v = zu * cu, zv * cv
    v3 = _by_group(v, dm.ng)
    xc = v3 - jnp.mean(v3, axis=1, keepdims=True)
    rstd = lax.rsqrt(jnp.mean(xc * xc, axis=1, keepdims=True) + EPS)
    xh = (xc * rstd).reshape(dm.gw, BLOCK)
    vnb = (xh * lng_ref[...] + lnb_ref[...]).astype(MXU)
    wts = [jnp.where(tril, ws_ref[g], 0.0).astype(MXU) for g in range(dm.ng)]
    mixed = jnp.concatenate([_dot_nt(vnb[g * GROUP_DIM:(g + 1) * GROUP_DIM], wts[g]) + bs_ref[g:g + 1, :]
                             for g in range(dm.ng)], axis=0)
    return u * mixed, u, mixed, xh, rstd, vnb, wts, (zu, cu), (zv, cv)


def _mixer_specs(dm, nb, clamp):
    kvblk = dm.aw // (2 * dm.kvw)

    def cur(s, i):
        return (0, s * nb + clamp(i))

    def prev(s, i):
        return (kvblk, s * nb + jnp.maximum(clamp(i) - 1, 0))

    full = lambda shape: pl.BlockSpec(shape, lambda s, i: tuple(0 for _ in shape))
    return cur, prev, full


def _tril():
    return lax.broadcasted_iota(jnp.int32, (BLOCK, BLOCK), 0) >= lax.broadcasted_iota(jnp.int32, (BLOCK, BLOCK), 1)


def _mixer_fwd(proj_t, prm, dm, nseq, nb):
    gq, gk2, sinks, lng, lnb, ws, bs, goa, gog = prm
    t = proj_t.shape[1]
    cur, prev, full = _mixer_specs(dm, nb, lambda i: i)

    def body(p_ref, pkv_ref, gq_ref, gk_ref, sink_ref, lng_ref, lnb_ref, ws_ref, bs_ref, goa_ref, gog_ref, y_ref, att_scr):
        i = pl.program_id(1)
        valid, dist = _band_masks(i == 0, dm.grp)
        gqv, gkv = gq_ref[...], gk_ref[...]
        for kh in range(dm.nkv):
            _, kn, v_tb, _ = _kv_band(dm, kh, p_ref, pkv_ref, gkv)
            o = _attn_group_fwd(dm, kh, p_ref, gqv, kn, v_tb, sink_ref, valid, dist)[0]
            for g, h in enumerate(_group_heads(dm, kh)):
                att_scr[h * dm.hd:(h + 1) * dm.hd, :] = o[:, g * BLOCK:(g + 1) * BLOCK]
        att = att_scr[...]
        y_ref[:dm.aw, :] = (att * _col_rms(att) * goa_ref[...]).astype(y_ref.dtype)
        gt = _gate_fwd(dm, p_ref, lng_ref, lnb_ref, ws_ref, bs_ref, _tril())[0]
        y_ref[dm.aw:, :] = (gt * _col_rms(gt) * gog_ref[...]).astype(y_ref.dtype)

    return pl.pallas_call(
        body, name="mixer_fwd", grid=(nseq, nb),
        in_specs=[pl.BlockSpec((dm.inw, BLOCK), cur), pl.BlockSpec((2 * dm.kvw, BLOCK), prev),
                  full(gq.shape), full(gk2.shape), pl.BlockSpec(memory_space=pltpu.SMEM),
                  full(lng.shape), full(lnb.shape), full(ws.shape), full(bs.shape), full(goa.shape), full(gog.shape)],
        out_specs=pl.BlockSpec((dm.d, BLOCK), cur),
        out_shape=jax.ShapeDtypeStruct((dm.d, t), MXU),
        scratch_shapes=[pltpu.VMEM((dm.aw, BLOCK), F32)],
        compiler_params=_cparams(("parallel", "arbitrary"), 40),
    )(proj_t, proj_t, gq, gk2, sinks, lng, lnb, ws, bs, goa, gog)


def _mixer_bwd(proj_t, dy_t, prm, dm, nseq, nb):
    gq, gk2, sinks, lng, lnb, ws, bs, goa, gog = prm
    t = proj_t.shape[1]
    clamp = lambda i: jnp.minimum(i, nb - 1)
    cur, prev, full = _mixer_specs(dm, nb, clamp)
    kvw2 = 2 * dm.kvw

    def prev_kv_out(s, i):
        return (0, s * nb + jnp.maximum(i - 1, 0))

    def body(p_ref, pkv_ref, dy_ref, gq_ref, gk_ref, sink_ref, lng_ref, lnb_ref, ws_ref, bs_ref, goa_ref, gog_ref,
             dproj_ref, dkv_ref, dgq_ref, dgk_ref, dsink_ref, dlng_ref, dlnb_ref, dws_ref, dbs_ref, dgoa_ref, dgog_ref,
             att_scr, datt_scr, carry_scr, prevpart_scr, curpart_scr, kprev_scr,
             a_gq, a_gk, a_sink, a_lng, a_lnb, a_goa, a_gog):
        s_id, i = pl.program_id(0), pl.program_id(1)
        lane_accs = ((a_gq, dgq_ref), (a_gk, dgk_ref), (a_sink, dsink_ref), (a_lng, dlng_ref), (a_lnb, dlnb_ref),
                     (a_goa, dgoa_ref), (a_gog, dgog_ref))

        @pl.when((s_id == 0) & (i == 0))
        def _():
            for acc, _ in lane_accs:
                acc[...] = jnp.zeros_like(acc)
            dws_ref[...] = jnp.zeros_like(dws_ref)
            dbs_ref[...] = jnp.zeros_like(dbs_ref)

        gqv, gkv = gq_ref[...], gk_ref[...]

        @pl.when(i < nb)
        def _():
            valid, dist = _band_masks(i == 0, dm.grp)
            kvs, fwd = [], []
            for kh in range(dm.nkv):
                kv = _kv_band(dm, kh, p_ref, pkv_ref, gkv)
                kvs.append(kv)
                fwd.append(_attn_group_fwd(dm, kh, p_ref, gqv, kv[1], kv[2], sink_ref, valid, dist))
                for g, h in enumerate(_group_heads(dm, kh)):
                    att_scr[h * dm.hd:(h + 1) * dm.hd, :] = fwd[kh][0][:, g * BLOCK:(g + 1) * BLOCK]
            att = att_scr[...]
            dya = dy_ref[:dm.aw, :]
            ra = _col_rms(att)
            ah = att * ra
            a_goa[...] += dya * ah
            dah = dya * goa_ref[...]
            datt_scr[...] = ra * (dah - ah * jnp.mean(dah * ah, axis=0, keepdims=True))
            for kh in range(dm.nkv):
                kn_tb, kn, v_tb, vb = kvs[kh]
                _, p, ps, rq, qh, qnb = fwd[kh]
                heads = _group_heads(dm, kh)
                do_b = jnp.concatenate([datt_scr[h * dm.hd:(h + 1) * dm.hd, :] for h in heads], axis=1).astype(MXU)
                dp = _dot(vb, do_b)
                delta = jnp.sum(p * dp, axis=0, keepdims=True)
                dsb = (p * (dp - delta) * (dm.hd ** -0.5)).astype(MXU)
                dsink = ps * delta
                dqn = _dot(kn_tb, dsb)
                dkn = _dot_nt(qnb, dsb)
                dvb = _dot_nt(do_b, p.astype(MXU))
                a_gq[...] += dqn * qh
                dqh = dqn * gqv
                dq = rq * (dqh - qh * jnp.mean(dqh * qh, axis=0, keepdims=True))
                for g, h in enumerate(heads):
                    a_sink[h:h + 1, :] += -dsink[:, g * BLOCK:(g + 1) * BLOCK]
                    dproj_ref[h * dm.hd:(h + 1) * dm.hd, :] = dq[:, g * BLOCK:(g + 1) * BLOCK].astype(dproj_ref.dtype)
                krows = slice(kh * dm.hd, (kh + 1) * dm.hd)
                vrows = slice(dm.kvw + kh * dm.hd, dm.kvw + (kh + 1) * dm.hd)
                prevpart_scr[krows, :] = dkn[:, :BLOCK]
                prevpart_scr[vrows, :] = dvb[:, :BLOCK]
                curpart_scr[krows, :] = dkn[:, BLOCK:]
                curpart_scr[vrows, :] = dvb[:, BLOCK:]
            dproj_ref[dm.aw:dm.zoff, :] = jnp.zeros((kvw2, BLOCK), dproj_ref.dtype)
            tril = _tril()
            gt, u, mixed, xh, rstd, vnb, wts, (zu, cu), (zv, cv) = _gate_fwd(dm, p_ref, lng_ref, lnb_ref, ws_ref, bs_ref, tril)
            dyg = dy_ref[dm.aw:, :]
            rg = _col_rms(gt)
            gh = gt * rg
            a_gog[...] += dyg * gh
            dgh = dyg * gog_ref[...]
            dgt = rg * (dgh - gh * jnp.mean(dgh * gh, axis=0, keepdims=True))
            du = dgt * mixed
            dmix = dgt * u
            dmixb = dmix.astype(MXU)
            dbs_ref[...] += jnp.sum(_by_group(dmix, dm.ng), axis=1)
            dvn = []
            for g in range(dm.ng):
                rows = slice(g * GROUP_DIM, (g + 1) * GROUP_DIM)
                dws_ref[g] += jnp.where(tril, _dot_tn(dmixb[rows], vnb[rows]), 0.0)
                dvn.append(_dot(dmixb[rows], wts[g]))
            dvn = jnp.concatenate(dvn, axis=0)
            a_lng[...] += dvn * xh
            a_lnb[...] += dvn
            dxh3, xh3 = _by_group(dvn * lng_ref[...], dm.ng), _by_group(xh, dm.ng)
            dv = (rstd * (dxh3 - jnp.mean(dxh3, axis=1, keepdims=True) - xh3 * jnp.mean(dxh3 * xh3, axis=1, keepdims=True))).reshape(dm.gw, BLOCK)
            dgu = cu + zu * (jnp.exp(-0.5 * zu * zu) * _INV_SQRT_2PI)
            dgv = cv + zv * (jnp.exp(-0.5 * zv * zv) * _INV_SQRT_2PI)
            dproj_ref[dm.zoff:dm.zoff + dm.gw, :] = (du * dgu).astype(dproj_ref.dtype)
            dproj_ref[dm.zoff + dm.gw:, :] = (dv * dgv).astype(dproj_ref.dtype)

        @pl.when(i == nb)
        def _():
            prevpart_scr[...] = jnp.zeros_like(prevpart_scr)

        @pl.when(i >= 1)
        def _():
            tot = carry_scr[...] + prevpart_scr[...]
            for kh in range(dm.nkv):
                krows = slice(kh * dm.hd, (kh + 1) * dm.hd)
                kraw = kprev_scr[krows, :]
                rk = _col_rms(kraw)
                khat = kraw * rk
                dkn = tot[krows, :]
                a_gk[...] += dkn * khat
                dkh = dkn * gkv[:, :BLOCK]
                dk = rk * (dkh - khat * jnp.mean(dkh * khat, axis=0, keepdims=True))
                dkv_ref[krows, :] = dk.astype(dkv_ref.dtype)
            dkv_ref[dm.kvw:, :] = tot[dm.kvw:, :].astype(dkv_ref.dtype)

        @pl.when(i < nb)
        def _():
            carry_scr[...] = curpart_scr[...]
            kprev_scr[...] = p_ref[dm.aw:dm.aw + dm.kvw, :]

        @pl.when((s_id == nseq - 1) & (i == nb))
        def _():
            for acc, out in lane_accs:
                out[...] = jnp.sum(acc[...], axis=1, keepdims=True)

    col = lambda rows: jax.ShapeDtypeStruct((rows, 1), F32)
    lane = lambda rows: pltpu.VMEM((rows, LANES), F32)
    return pl.pallas_call(
        body, name="mixer_bwd", grid=(nseq, nb + 1),
        in_specs=[pl.BlockSpec((dm.inw, BLOCK), cur), pl.BlockSpec((kvw2, BLOCK), prev), pl.BlockSpec((dm.d, BLOCK), cur),
                  full(gq.shape), full(gk2.shape), pl.BlockSpec(memory_space=pltpu.SMEM),
                  full(lng.shape), full(lnb.shape), full(ws.shape), full(bs.shape), full(goa.shape), full(gog.shape)],
        out_specs=[pl.BlockSpec((dm.inw, BLOCK), cur), pl.BlockSpec((kvw2, BLOCK), prev_kv_out),
                   full((dm.hd, 1)), full((dm.hd, 1)), full((dm.nh, 1)), full((dm.gw, 1)), full((dm.gw, 1)), full(ws.shape),
                   full(bs.shape), full((dm.aw, 1)), full((dm.gw, 1))],
        out_shape=[jax.ShapeDtypeStruct((dm.inw, t), MXU), jax.ShapeDtypeStruct((kvw2, t), MXU),
                   col(dm.hd), col(dm.hd), col(dm.nh), col(dm.gw), col(dm.gw), jax.ShapeDtypeStruct(ws.shape, F32),
                   jax.ShapeDtypeStruct(bs.shape, F32), col(dm.aw), col(dm.gw)],
        scratch_shapes=[pltpu.VMEM((dm.aw, BLOCK), F32), pltpu.VMEM((dm.aw, BLOCK), F32),
                        pltpu.VMEM((kvw2, BLOCK), F32), pltpu.VMEM((kvw2, BLOCK), F32), pltpu.VMEM((kvw2, BLOCK), F32),
                        pltpu.VMEM((dm.kvw, BLOCK), F32),
                        pltpu.VMEM((dm.hd, dm.grp * BLOCK), F32), lane(dm.hd), lane(dm.nh), lane(dm.gw), lane(dm.gw), lane(dm.aw), lane(dm.gw)],
        compiler_params=_cparams(("arbitrary", "arbitrary"), 48),
    )(proj_t, proj_t, dy_t, gq, gk2, sinks, lng, lnb, ws, bs, goa, gog)


def _patch_kv(dproj_t, dkv_t, dm):
    t = dproj_t.shape[1]
    tc = _pick(t, (1024, 512, 256, 128))
    kvw2 = 2 * dm.kvw
    kvblk = dm.aw // kvw2

    def body(dproj_hbm, dkv_ref, out_ref):
        del dproj_hbm
        out_ref[...] = dkv_ref[...]

    return pl.pallas_call(
        body, name="patch_kv", grid=(t // tc,),
        in_specs=[ANY, pl.BlockSpec((kvw2, tc), lambda i: (0, i))],
        out_specs=pl.BlockSpec((kvw2, tc), lambda i: (kvblk, i)),
        out_shape=jax.ShapeDtypeStruct(dproj_t.shape, dproj_t.dtype),
        input_output_aliases={0: 0},
        compiler_params=_cparams(("parallel",), 32),
    )(dproj_t, dkv_t)


def _place():
    x, y, c = lax.axis_index("x"), lax.axis_index("y"), lax.axis_index("c")
    return x, y, c


def _handshake(peers):
    barrier = pltpu.get_barrier_semaphore()
    for p in peers:
        pl.semaphore_signal(barrier, inc=1, device_id=p, device_id_type=MESH)
    pl.semaphore_wait(barrier, len(peers))


def _sequencer_mesh():
    return plsc.ScalarSubcoreMesh(axis_name="sequencer", num_cores=1)


GATHER_CHUNKS = 4
BF16_ROWS = 16


def _row_chunks(n, k):
    tiles = n // BF16_ROWS
    sizes = [(tiles // k + (1 if i < tiles % k else 0)) * BF16_ROWS for i in range(k)]
    return [(sum(sizes[:i]), sz) for i, sz in enumerate(sizes) if sz]


def _allgather_weight(name, collective_id, shard, after=None):
    n = shard.shape[0]
    assert n % BF16_ROWS == 0
    chunks = _row_chunks(n, GATHER_CHUNKS)
    nc = len(chunks)

    def body(*refs):
        src, out = refs[0], refs[-4]
        send_sems, recv_sems, local_sem = refs[-3:]
        x, y, c = _place()
        me, sib, xn, yn, diag = (x, y, c), (x, y, 1 - c), (1 - x, y, c), (x, 1 - y, c), (1 - x, 1 - y, c)
        relay_to = (x ^ c, y ^ (1 - c), c)
        relay_of = (x ^ (1 - c), y ^ c, c)
        _handshake([sib, xn, yn])

        def rows(place, ci):
            px, py, pc = place
            off, size = chunks[ci]
            return out.at[pl.ds(pl.multiple_of((4 * px + 2 * py + pc) * n + off, BF16_ROWS), size), :]

        def copy(k, ci, block, to, from_src=False):
            off, size = chunks[ci]
            return pltpu.make_async_remote_copy(
                src_ref=src.at[pl.ds(off, size), :] if from_src else rows(block, ci), dst_ref=rows(block, ci),
                send_sem=send_sems.at[ci, k], recv_sem=recv_sems.at[ci, k], device_id=to, device_id_type=MESH)

        mine = pltpu.make_async_copy(src, out.at[pl.ds(pl.multiple_of((4 * x + 2 * y + c) * n, BF16_ROWS), n), :], local_sem)
        mine.start()
        sent = []
        for ci in range(nc):
            sent += [copy(0, ci, me, sib, from_src=True), copy(1, ci, me, xn, from_src=True), copy(2, ci, me, yn, from_src=True)]
        for cp in sent:
            cp.start()
        for ci in range(nc):
            copy(1, ci, xn, me).wait_recv()
            copy(2, ci, yn, me).wait_recv()
            passed = [copy(3, ci, relay_of, relay_to), copy(4, ci, xn, sib), copy(5, ci, yn, sib)]
            for cp in passed:
                cp.start()
            sent += passed
        for ci in range(nc):
            copy(3, ci, diag, me).wait_recv()
            passed = copy(6, ci, diag, sib)
            passed.start()
            sent.append(passed)
        for ci in range(nc):
            copy(0, ci, sib, me).wait_recv()
            for k, block in ((4, (1 - x, y, 1 - c)), (5, (x, 1 - y, 1 - c)), (6, (1 - x, 1 - y, 1 - c))):
                copy(k, ci, block, me).wait_recv()
        for cp in sent:
            cp.wait_send()
        mine.wait()

    return pl.kernel(
        body, name=name,
        out_type=jax.ShapeDtypeStruct((N_DEV * n, shard.shape[1]), shard.dtype),
        mesh=_sequencer_mesh(),
        scratch_types=[pltpu.SemaphoreType.DMA((nc, 7)), pltpu.SemaphoreType.DMA((nc, 7)), pltpu.SemaphoreType.DMA],
        compiler_params=pltpu.CompilerParams(collective_id=collective_id),
    )(*([shard] if after is None else [shard, after]))


_FLIPS = [(0, 0, 1), (1, 0, 0), (0, 1, 0), (1, 1, 0), (1, 0, 1), (0, 1, 1), (1, 1, 1)]


def _scatter_grad(name, collective_id, grad):
    n = grad.shape[0] // N_DEV

    def body(src, out, send_sems, recv_sems, local_sem):
        x, y, c = _place()
        me_idx = 4 * x + 2 * y + c
        peers = [(x ^ fx, y ^ fy, c ^ fc) for (fx, fy, fc) in _FLIPS]
        _handshake(peers)

        def block(idx):
            return src.at[pl.ds(pl.multiple_of(idx * n, 16), n), :]

        copies = [pltpu.make_async_remote_copy(
            src_ref=block(4 * px + 2 * py + pc), dst_ref=out.at[me_idx], send_sem=send_sems.at[k], recv_sem=recv_sems.at[k],
            device_id=(px, py, pc), device_id_type=MESH) for k, (px, py, pc) in enumerate(peers)]
        mine = pltpu.make_async_copy(block(me_idx), out.at[me_idx], local_sem)
        mine.start()
        for cp in copies:
            cp.start()
        for cp in copies:
            cp.wait_recv()
        for cp in copies:
            cp.wait_send()
        mine.wait()

    return pl.kernel(
        body, name=name,
        out_type=jax.ShapeDtypeStruct((N_DEV, n, grad.shape[1]), grad.dtype),
        mesh=_sequencer_mesh(),
        scratch_types=[pltpu.SemaphoreType.DMA((7,)), pltpu.SemaphoreType.DMA((7,)), pltpu.SemaphoreType.DMA],
        compiler_params=pltpu.CompilerParams(collective_id=collective_id),
    )(grad)


def _allgather_rows(name, collective_id, part):
    def body(src, out, send_sems, recv_sems, local_sem):
        x, y, c = _place()
        me_idx = 4 * x + 2 * y + c
        peers = [(x ^ fx, y ^ fy, c ^ fc) for (fx, fy, fc) in _FLIPS]
        _handshake(peers)
        copies = [pltpu.make_async_remote_copy(
            src_ref=src, dst_ref=out.at[me_idx], send_sem=send_sems.at[k], recv_sem=recv_sems.at[k],
            device_id=peer, device_id_type=MESH) for k, peer in enumerate(peers)]
        mine = pltpu.make_async_copy(src, out.at[me_idx], local_sem)
        mine.start()
        for cp in copies:
            cp.start()
        for cp in copies:
            cp.wait_recv()
        for cp in copies:
            cp.wait_send()
        mine.wait()

    return pl.kernel(
        body, name=name,
        out_type=jax.ShapeDtypeStruct((N_DEV,) + part.shape, part.dtype),
        mesh=_sequencer_mesh(),
        scratch_types=[pltpu.SemaphoreType.DMA((7,)), pltpu.SemaphoreType.DMA((7,)), pltpu.SemaphoreType.DMA],
        compiler_params=pltpu.CompilerParams(collective_id=collective_id),
    )(part)


def _adamw_math(w, g, m, v):
    m = ADAM_B1 * m + (1.0 - ADAM_B1) * g
    v = ADAM_B2 * v + (1.0 - ADAM_B2) * (g * g)
    m_hat = m / (1.0 - ADAM_B1 ** ADAM_STEP)
    v_hat = v / (1.0 - ADAM_B2 ** ADAM_STEP)
    delta = -ADAM_LR * (m_hat / (jnp.sqrt(v_hat) + ADAM_EPS) + ADAM_WD * w)
    return delta, m, v


def _sum_adamw(name, slots, w, m, v, after):
    _, n, kk = slots.shape
    tr = _pick(n, (208, 176, 128, 96, 64, 32, 16))

    def body(s_ref, w_ref, m_ref, v_ref, after_ref, g_ref, d_ref, nm_ref, nv_ref):
        del after_ref
        g = s_ref[0].astype(F32)
        for p in range(1, N_DEV):
            g = g + s_ref[p].astype(F32)
        g_ref[...] = g
        d_ref[...], nm_ref[...], nv_ref[...] = _adamw_math(w_ref[...], g, m_ref[...], v_ref[...])

    row = pl.BlockSpec((tr, kk), lambda i: (i, 0))
    return pl.pallas_call(
        body, name=name, grid=(n // tr,),
        in_specs=[pl.BlockSpec((N_DEV, tr, kk), lambda i: (0, i, 0)), row, row, row, ANY],
        out_specs=[row] * 4,
        out_shape=[jax.ShapeDtypeStruct((n, kk), F32)] * 4,
        compiler_params=_cparams(("parallel",), 48),
    )(slots, w, m, v, after)


def _allreduce_small_adamw(early_slots, late, w, m, v, after):
    ra, rb = early_slots.shape[1], late.shape[0]

    def body(early_ref, late_ref, w_ref, m_ref, v_ref, after_ref, g_ref, d_ref, nm_ref, nv_ref, slots, send_sems, recv_sems):
        del after_ref
        x, y, c = _place()
        me_idx = 4 * x + 2 * y + c
        copies = []
        for k, (fx, fy, fc) in enumerate(_FLIPS):
            px, py, pc = x ^ fx, y ^ fy, c ^ fc
            copies.append(pltpu.make_async_remote_copy(
                src_ref=late_ref, dst_ref=slots.at[me_idx], send_sem=send_sems.at[k], recv_sem=recv_sems.at[k],
                device_id=(px, py, pc), device_id_type=MESH))
        for cp in copies:
            cp.start()
        slots[me_idx] = late_ref[...]
        g = early_ref[0]
        for p in range(1, N_DEV):
            g = g + early_ref[p]
        early = pl.ds(0, ra)
        g_ref[early, :] = g
        d_ref[early, :], nm_ref[early, :], nv_ref[early, :] = _adamw_math(w_ref[early, :], g, m_ref[early, :], v_ref[early, :])
        for cp in copies:
            cp.wait_recv()
        for cp in copies:
            cp.wait_send()
        g = slots[0]
        for p in range(1, N_DEV):
            g = g + slots[p]
        tail = pl.ds(ra, rb)
        g_ref[tail, :] = g
        d_ref[tail, :], nm_ref[tail, :], nv_ref[tail, :] = _adamw_math(w_ref[tail, :], g, m_ref[tail, :], v_ref[tail, :])

    vm = pl.BlockSpec(memory_space=pltpu.VMEM)
    return pl.pallas_call(
        body, name="allreduce_small_adamw",
        in_specs=[vm] * 5 + [ANY], out_specs=[vm] * 4,
        out_shape=[jax.ShapeDtypeStruct((ra + rb, LANES), F32)] * 4,
        scratch_shapes=[pltpu.VMEM((N_DEV, rb, LANES), F32), pltpu.SemaphoreType.DMA((7,)), pltpu.SemaphoreType.DMA((7,))],
        compiler_params=pltpu.CompilerParams(vmem_limit_bytes=48 * MIB),
    )(early_slots, late, w, m, v, after)


def _pack(arrs):
    parts, meta, off = [], [], 0
    for a in arrs:
        flat = a.reshape(-1).astype(F32)
        rows = -(-flat.shape[0] // LANES)
        rows8 = -(-rows // 8) * 8
        flat = jnp.pad(flat, (0, rows8 * LANES - flat.shape[0]))
        parts.append(flat.reshape(rows8, LANES))
        meta.append((off, a.shape, a.size))
        off += rows8
    return jnp.concatenate(parts, axis=0), meta


def _unpack(packed, meta):
    outs = []
    for off, shape, size in meta:
        rows = -(-size // LANES)
        outs.append(packed[off:off + rows].reshape(-1)[:size].reshape(shape))
    return outs


def _silu_parts(a):
    sg = 0.5 + 0.5 * jnp.tanh(0.5 * a)
    return a * sg, sg * (1.0 + a * (1.0 - sg))


def kernel(x, norm1_g, w_in, q_norm_g, k_norm_g, attn_sinks, gate_ln_g, gate_ln_b, w_spatial, b_spatial, out_norm_attn_g, out_norm_gate_g, w_out, norm2_g, w_ffn_gate, w_ffn_up, w_ffn_down, loss_target, m_norm1_g, m_w_in, m_q_norm_g, m_k_norm_g, m_attn_sinks, m_gate_ln_g, m_gate_ln_b, m_w_spatial, m_b_spatial, m_out_norm_attn_g, m_out_norm_gate_g, m_w_out, m_norm2_g, m_w_ffn_gate, m_w_ffn_up, m_w_ffn_down, v_norm1_g, v_w_in, v_q_norm_g, v_k_norm_g, v_attn_sinks, v_gate_ln_g, v_gate_ln_b, v_w_spatial, v_b_spatial, v_out_norm_attn_g, v_out_norm_gate_g, v_w_out, v_norm2_g, v_w_ffn_gate, v_w_ffn_up, v_w_ffn_down):
    nseq, seq, d = x.shape
    t = nseq * seq
    nb = seq // BLOCK
    inw = w_in.shape[2] * N_DEV
    dm = _Dims(d, inw, q_norm_g.shape[-1])
    xf = x.reshape(t, d)
    tgt = loss_target.reshape(t, d)

    rows = lambda wv, transposed: jnp.swapaxes(wv, 1, 2)[0] if transposed else wv[0]
    big = {"w_in": (w_in, m_w_in, v_w_in, True), "w_out": (w_out, m_w_out, v_w_out, False),
           "w_ffn_gate": (w_ffn_gate, m_w_ffn_gate, v_w_ffn_gate, True), "w_ffn_up": (w_ffn_up, m_w_ffn_up, v_w_ffn_up, True),
           "w_ffn_down": (w_ffn_down, m_w_ffn_down, v_w_ffn_down, False)}
    big_rows = {nm: tuple(rows(arr, tr) for arr in (wv, mv, vv)) for nm, (wv, mv, vv, tr) in big.items()}
    shard = lambda nm: big_rows[nm][0].astype(WIRE)
    win_t = _allgather_weight("gather_w_in", 1, shard("w_in"))
    wout = _allgather_weight("gather_w_out", 2, shard("w_out"), after=win_t)
    wg_t = _allgather_weight("gather_w_ffn_gate", 3, shard("w_ffn_gate"), after=win_t)
    wu_t = _allgather_weight("gather_w_ffn_up", 9, shard("w_ffn_up"), after=win_t)
    wd = _allgather_weight("gather_w_ffn_down", 10, shard("w_ffn_down"), after=win_t)

    lanes = lambda v, n=BLOCK: jnp.broadcast_to(v.reshape(-1, 1), (v.size, n))
    prm = (lanes(q_norm_g, dm.grp * BLOCK), lanes(k_norm_g, 2 * BLOCK), attn_sinks[0], lanes(gate_ln_g), lanes(gate_ln_b), w_spatial[0], b_spatial[0],
           lanes(out_norm_attn_g), lanes(out_norm_gate_g))

    h1 = _rms_fwd("rms1_fwd", xf, norm1_g)
    (proj_t,) = _matmul("mm_in", win_t, h1, "nt", [F32])
    y_t = _mixer_fwd(proj_t, prm, dm, nseq, nb)

    def residual_norm(acc, xr, g2):
        x2v = xr + acc
        return x2v, x2v * lax.rsqrt(jnp.mean(x2v * x2v, axis=-1, keepdims=True) + EPS) * g2

    x2, h2 = _matmul("mm_out", y_t, wout, "tn", [F32, MXU], epilogue=residual_norm, extras=[xf], rowvecs=[norm2_g], full_rows=True)
    a, b, s = _matmul("mm_gate_up", h2, wg_t, "nt", [MXU, MXU, MXU], b2=wu_t, epilogue=lambda ga, ub: (ga, ub, _silu_parts(ga)[0] * ub))

    def loss_epilogue(acc, x2v, tv):
        diff = (x2v + acc) - tv
        dx3 = diff * (1.0 / d)
        return dx3, dx3, jnp.sum(diff * diff)

    dx3, dx3b, lossp = _matmul("mm_down", s, wd, "nn", [F32, MXU], epilogue=loss_epilogue, extras=[x2, tgt], partial=True)
    loss_part = (0.5 / d) * jnp.sum(lossp[::8, ::LANES])

    def dswiglu(acc, av, bv):
        silu, dsilu = _silu_parts(av.astype(F32))
        return acc * bv.astype(F32) * dsilu, acc * silu

    da, db = _matmul("mm_d_down", dx3b, wd, "nt", [MXU, MXU], epilogue=dswiglu, extras=[a, b])
    (g_wd,) = _matmul("mm_gw_down", s, dx3b, "tn", [WIRE])
    sl_wd = _scatter_grad("scatter_w_ffn_down", 4, g_wd)
    (dh2a,) = _matmul("mm_dh2_gate", da, wg_t, "nn", [F32], after=[g_wd])
    (g_wg,) = _matmul("mm_gw_gate", da, h2, "tn", [WIRE], after=[dh2a])
    sl_wg = _scatter_grad("scatter_w_ffn_gate", 5, g_wg)
    (dh2,) = _matmul("mm_dh2_up", db, wu_t, "nn", [F32], epilogue=lambda acc, pv: (pv + acc,), extras=[dh2a], after=[g_wg])
    (g_wu,) = _matmul("mm_gw_up", db, h2, "tn", [WIRE], after=[dh2])
    sl_wu = _scatter_grad("scatter_w_ffn_up", 6, g_wu)
    dx2, dx2b, dg2 = _rms_bwd("rms2_bwd", dh2, x2, norm2_g, dx3)

    (dy_t,) = _matmul("mm_d_out", wout, dx2b, "nt", [F32], after=[g_wu])
    (g_wout,) = _matmul("mm_gw_out", y_t, dx2b, "nn", [WIRE], after=[dy_t])
    sl_wout = _scatter_grad("scatter_w_out", 7, g_wout)
    (dproj0, dkv, dgq, dgk, dsink, dlng, dlnb, dws, dbs, dgoa, dgog) = _mixer_bwd(proj_t, dy_t, prm, dm, nseq, nb)
    early_g = [dgq, dgk, dsink, dlng, dlnb, dws, dbs, dgoa, dgog, dg2, loss_part.reshape(1)]
    early_slots = _allgather_rows("gather_small_grads", 11, _pack(early_g)[0])
    dproj_t = _patch_kv(dproj0, dkv, dm)
    (g_win,) = _matmul("mm_gw_in", dproj_t, h1, "nn", [WIRE])
    sl_win = _scatter_grad("scatter_w_in", 8, g_win)

    def norm1_backward(dh1, xv, dx2v, g1):
        r = lax.rsqrt(jnp.mean(xv * xv, axis=-1, keepdims=True) + EPS)
        xh = xv * r
        dxh = dh1 * g1
        return dx2v + r * (dxh - xh * jnp.mean(dxh * xh, axis=-1, keepdims=True)), jnp.sum(dh1 * xh, axis=0, keepdims=True)

    dx, dg1 = _matmul("mm_d_in", dproj_t, win_t, "tn", [F32], epilogue=norm1_backward, extras=[xf, dx2], rowvecs=[norm1_g],
                      after=[g_win], col_sum=True, full_rows=True)

    big_out = {}
    last = dx

    def big_update(nm, sl, after):
        res = _sum_adamw("adamw_" + nm, sl, *big_rows[nm], after=after)
        big_out[nm] = tuple(jnp.swapaxes(r[None], 1, 2) if big[nm][3] else r[None] for r in res)
        return res[1]

    for nm, sl in (("w_ffn_down", sl_wd), ("w_ffn_gate", sl_wg), ("w_ffn_up", sl_wu), ("w_out", sl_wout)):
        last = big_update(nm, sl, last)

    zero = jnp.zeros((1,), F32)
    small_names = ["q_norm_g", "k_norm_g", "attn_sinks", "gate_ln_g", "gate_ln_b", "w_spatial", "b_spatial",
                   "out_norm_attn_g", "out_norm_gate_g", "norm2_g", "loss", "norm1_g"]
    small_w = [q_norm_g, k_norm_g, attn_sinks, gate_ln_g, gate_ln_b, w_spatial, b_spatial, out_norm_attn_g, out_norm_gate_g, norm2_g, zero, norm1_g]
    small_m = [m_q_norm_g, m_k_norm_g, m_attn_sinks, m_gate_ln_g, m_gate_ln_b, m_w_spatial, m_b_spatial, m_out_norm_attn_g, m_out_norm_gate_g, m_norm2_g, zero, m_norm1_g]
    small_v = [v_q_norm_g, v_k_norm_g, v_attn_sinks, v_gate_ln_g, v_gate_ln_b, v_w_spatial, v_b_spatial, v_out_norm_attn_g, v_out_norm_gate_g, v_norm2_g, zero, v_norm1_g]
    pw, meta = _pack(small_w)
    sg, sd, sm, sv = _allreduce_small_adamw(early_slots, _pack([dg1])[0], pw, _pack(small_m)[0], _pack(small_v)[0], after=last)
    big_update("w_in", sl_win, sd)
    ug, ud, um, uv = _unpack(sg, meta), _unpack(sd, meta), _unpack(sm, meta), _unpack(sv, meta)
    small_out = {nm: (ug[k], ud[k], um[k], uv[k]) for k, nm in enumerate(small_names)}
    loss = small_out["loss"][0].reshape(())

    order = ["norm1_g", "w_in", "q_norm_g", "k_norm_g", "attn_sinks", "gate_ln_g", "gate_ln_b", "w_spatial", "b_spatial",
             "out_norm_attn_g", "out_norm_gate_g", "w_out", "norm2_g", "w_ffn_gate", "w_ffn_up", "w_ffn_down"]
    allo = {**big_out, **small_out}
    outs = [loss, dx.reshape(nseq, seq, d)]
    for k in range(4):
        outs += [allo[nm][k] for nm in order]
    return tuple(outs)
```

```python
import math

import jax
import jax.numpy as jnp
from jax import lax
from jax.experimental import pallas as pl
from jax.experimental.pallas import tpu as pltpu
from jax.experimental.pallas import tpu_sc as plsc

F32 = jnp.float32
MXU = jnp.bfloat16
WIRE = jnp.bfloat16
EPS = 1e-6
BLOCK = 128
GROUP_DIM = 128
N_KV_HEADS = 2
NEG = -1e30
N_DEV = 8
LANES = 128
MIB = 1024 * 1024

ADAM_LR = 0.001
ADAM_B1 = 0.9
ADAM_B2 = 0.999
ADAM_EPS = 1e-08
ADAM_WD = 0.01
ADAM_STEP = 10

MESH = pl.DeviceIdType.MESH
ANY = pl.BlockSpec(memory_space=pl.ANY)


def _pick(n, cands):
    for c in cands:
        if n % c == 0:
            return c
    return n


def _cparams(sem, vmem_mb):
    return pltpu.CompilerParams(dimension_semantics=sem, vmem_limit_bytes=vmem_mb * MIB)


VMEM_TILE_BUDGET = 44 * MIB
HBM_BYTES_PER_US = 3.0e6
STEP_US = 0.4
MIN_TILE_N = 512


def _tile_candidates(n):
    return [c for c in range(min(n, 2048), 0, -LANES) if n % c == 0 and c % LANES == 0] or [n]


def _matmul_tiles(m, n, kk, esz, n_b, extra_sizes, out_sizes, full_rows):
    best = None
    wide = [n] if full_rows else [c for c in _tile_candidates(n) if c >= MIN_TILE_N] or _tile_candidates(n)
    for tm in _tile_candidates(m):
        for tn in wide:
            vmem = 2 * (tm + n_b * tn) * kk * esz + tm * tn * (4 * n_b + 2 * sum(extra_sizes) + 2 * sum(out_sizes))
            if vmem > VMEM_TILE_BUDGET:
                continue
            cost = (m // tm) * n_b * n * kk * esz / HBM_BYTES_PER_US + (m // tm) * (n // tn) * STEP_US
            if best is None or cost < best[0]:
                best = (cost, tm, tn, vmem)
    assert best is not None, (m, n, kk)
    return best[1:]


def _matmul(name, a, b, mode, out_dtypes, epilogue=None, extras=(), rowvecs=(), after=(), partial=False, col_sum=False, full_rows=False, b2=None):
    if mode == "nn":
        (m, kk), n = a.shape, b.shape[1]
        dn = (((1,), (0,)), ((), ()))
    elif mode == "nt":
        (m, kk), n = a.shape, b.shape[0]
        dn = (((1,), (1,)), ((), ()))
    else:
        (kk, m), n = a.shape, b.shape[1]
        dn = (((0,), (0,)), ((), ()))
    bs = [b] if b2 is None else [b, b2]
    tm, tn, vmem = _matmul_tiles(m, n, kk, a.dtype.itemsize, len(bs), [e.dtype.itemsize for e in extras],
                                 [jnp.dtype(dt).itemsize for dt in out_dtypes], full_rows)
    a_spec = pl.BlockSpec((kk, tm), lambda i, j: (0, i)) if mode == "tn" else pl.BlockSpec((tm, kk), lambda i, j: (i, 0))
    b_spec = pl.BlockSpec((tn, kk), lambda i, j: (j, 0)) if mode == "nt" else pl.BlockSpec((kk, tn), lambda i, j: (0, j))
    tile = pl.BlockSpec((tm, tn), lambda i, j: (i, j))
    row = pl.BlockSpec((1, tn), lambda i, j: (0, j))
    nb, ne, nr, na, no = len(bs), len(extras), len(rowvecs), len(after), len(out_dtypes)

    def body(a_ref, *rest):
        b_refs, in_refs, out_refs = rest[:nb], rest[nb:nb + ne + nr], rest[nb + ne + nr + na:]
        av = a_ref[...]
        accs = [lax.dot_general(av, b_ref[...], dn, preferred_element_type=F32) for b_ref in b_refs]
        vals = tuple(accs) if epilogue is None else epilogue(*accs, *[r[...] for r in in_refs])
        for o_ref, t in zip(out_refs[:no], vals[:no]):
            o_ref[...] = t.astype(o_ref.dtype)
        if partial:
            out_refs[no][...] = jnp.full((8, LANES), vals[no], F32)
        if col_sum:
            sum_ref = out_refs[-1]

            @pl.when(pl.program_id(0) == 0)
            def _():
                sum_ref[...] = jnp.zeros_like(sum_ref)

            sum_ref[...] += vals[-1]

    out_specs = [tile] * no
    out_shape = [jax.ShapeDtypeStruct((m, n), dt) for dt in out_dtypes]
    if partial:
        out_specs.append(pl.BlockSpec((8, LANES), lambda i, j: (i, j)))
        out_shape.append(jax.ShapeDtypeStruct((m // tm * 8, n // tn * LANES), F32))
    if col_sum:
        out_specs.append(row)
        out_shape.append(jax.ShapeDtypeStruct((1, n), F32))
    return pl.pallas_call(
        body, name=name, grid=(m // tm, n // tn),
        in_specs=[a_spec] + [b_spec] * nb + [tile] * ne + [row] * nr + [ANY] * na,
        out_specs=out_specs, out_shape=out_shape,
        compiler_params=_cparams(("arbitrary" if col_sum else "parallel", "arbitrary"), min(vmem // MIB + 8, 60)),
    )(a, *bs, *extras, *rowvecs, *after)


def _rms_fwd(name, x, g):
    t, d = x.shape
    tm = _pick(t, (512, 256, 128))

    def body(x_ref, g_ref, h_ref):
        xv = x_ref[...]
        r = lax.rsqrt(jnp.mean(xv * xv, axis=-1, keepdims=True) + EPS)
        h_ref[...] = (xv * r * g_ref[...]).astype(h_ref.dtype)

    return pl.pallas_call(
        body, name=name, grid=(t // tm,),
        in_specs=[pl.BlockSpec((tm, d), lambda i: (i, 0)), pl.BlockSpec((1, d), lambda i: (0, 0))],
        out_specs=pl.BlockSpec((tm, d), lambda i: (i, 0)),
        out_shape=jax.ShapeDtypeStruct((t, d), MXU),
        compiler_params=_cparams(("parallel",), 32),
    )(x, g)


def _norm_bwd_matmul(name, w, dh, x, g, res, after):
    t, d = x.shape
    m = w.shape[0]
    tn = _pick(t, (256, 128))

    def body(w_ref, dh_ref, x_ref, g_ref, res_ref, after_ref, out_ref, dx_ref, dxb_ref, dg_ref):
        del after_ref

        @pl.when(pl.program_id(0) == 0)
        def _():
            dg_ref[...] = jnp.zeros_like(dg_ref)

        xv, dhv = x_ref[...], dh_ref[...]
        r = lax.rsqrt(jnp.mean(xv * xv, axis=-1, keepdims=True) + EPS)
        xh = xv * r
        dg_ref[...] += jnp.sum(dhv * xh, axis=0, keepdims=True)
        dxh = dhv * g_ref[...]
        dx = res_ref[...] + r * (dxh - xh * jnp.mean(dxh * xh, axis=-1, keepdims=True))
        dx_ref[...] = dx
        dxb = dx.astype(MXU)
        dxb_ref[...] = dxb
        out_ref[...] = lax.dot_general(w_ref[...], dxb, (((1,), (1,)), ((), ())), preferred_element_type=F32)

    row = pl.BlockSpec((tn, d), lambda j: (j, 0))
    vec = pl.BlockSpec((1, d), lambda j: (0, 0))
    return pl.pallas_call(
        body, name=name, grid=(t // tn,),
        in_specs=[pl.BlockSpec((m, d), lambda j: (0, 0)), row, row, vec, row, ANY],
        out_specs=[pl.BlockSpec((m, tn), lambda j: (0, j)), row, row, vec],
        out_shape=[jax.ShapeDtypeStruct((m, t), F32), jax.ShapeDtypeStruct((t, d), F32), jax.ShapeDtypeStruct((t, d), MXU),
                   jax.ShapeDtypeStruct((1, d), F32)],
        compiler_params=_cparams(("arbitrary",), 52),
    )(w, dh, x, g, res, after)


_INV_SQRT2 = 0.7071067811865476
_INV_SQRT_2PI = 0.3989422804014327


def _dot_nt(a, b):
    return lax.dot_general(a, b, (((1,), (1,)), ((), ())), preferred_element_type=F32)


def _dot_tn(a, b):
    return lax.dot_general(a, b, (((0,), (0,)), ((), ())), preferred_element_type=F32)


def _dot(a, b):
    return jnp.dot(a, b, preferred_element_type=F32)


def _col_rms(v):
    return lax.rsqrt(jnp.mean(v * v, axis=0, keepdims=True) + EPS)


class _Dims:
    def __init__(self, d_model, in_width, head_dim):
        self.d = d_model
        self.aw = d_model // 2
        self.gw = d_model - self.aw
        self.kvw = (in_width - self.aw - 2 * self.gw) // 2
        self.hd = head_dim
        self.nh = self.aw // head_dim
        self.nkv = self.kvw // head_dim
        self.grp = self.nh // self.nkv
        self.ng = self.gw // GROUP_DIM
        self.inw = in_width
        self.zoff = self.aw + 2 * self.kvw
        assert self.nkv == N_KV_HEADS and self.zoff + 2 * self.gw == in_width and self.aw % (2 * self.kvw) == 0


def _band_masks(first, grp):
    kj = lax.broadcasted_iota(jnp.int32, (2 * BLOCK, grp * BLOCK), 0)
    qi = lax.broadcasted_iota(jnp.int32, (2 * BLOCK, grp * BLOCK), 1) & (BLOCK - 1)
    dist = qi + BLOCK - kj
    valid = (dist >= 0) & (dist < BLOCK) & ((kj >= BLOCK) | jnp.logical_not(first))
    return valid, dist.astype(F32)


def _kv_band(dm, kh, p_ref, pkv_ref, gk2):
    ko = dm.aw + kh * dm.hd
    vo = dm.aw + dm.kvw + kh * dm.hd
    k_t = jnp.concatenate([pkv_ref[kh * dm.hd:(kh + 1) * dm.hd, :], p_ref[ko:ko + dm.hd, :]], axis=1)
    v_t = jnp.concatenate([pkv_ref[dm.kvw + kh * dm.hd:dm.kvw + (kh + 1) * dm.hd, :], p_ref[vo:vo + dm.hd, :]], axis=1)
    kn_t = k_t * _col_rms(k_t) * gk2
    return kn_t.astype(MXU), kn_t.T.astype(MXU), v_t.astype(MXU), v_t.T.astype(MXU)


def _group_heads(dm, kh):
    return range(kh * dm.grp, (kh + 1) * dm.grp)


def _lane_row(vals):
    return jnp.concatenate([jnp.full((1, BLOCK), v, F32) for v in vals], axis=1)


def _attn_group_fwd(dm, kh, p_ref, gq, kn, v_tb, sink_ref, valid, dist):
    heads = _group_heads(dm, kh)
    q = jnp.concatenate([p_ref[h * dm.hd:(h + 1) * dm.hd, :] for h in heads], axis=1)
    rq = _col_rms(q)
    qh = q * rq
    qnb = (qh * gq).astype(MXU)
    slopes = _lane_row([math.pow(2.0, -8.0 * (h + 1) / dm.nh) for h in heads])
    sinks = _lane_row([sink_ref[h] for h in heads])
    s = _dot(kn, qnb) * (dm.hd ** -0.5)
    logits = jnp.where(valid, s - slopes * dist, NEG)
    m = jnp.maximum(jnp.max(logits, axis=0, keepdims=True), sinks)
    e = jnp.exp(logits - m)
    es = jnp.exp(sinks - m)
    inv = 1.0 / (jnp.sum(e, axis=0, keepdims=True) + es)
    p = e * inv
    o = _dot(v_tb, p.astype(MXU))
    return o, p, es * inv, rq, qh, qnb


def _gelu_cdf(z):
    return 0.5 * (1.0 + lax.erf(z * _INV_SQRT2))


def _by_group(v, ng):
    return v.reshape(ng, GROUP_DIM, v.shape[1])


def _gate_fwd(dm, p_ref, lng_ref, lnb_ref, ws_ref, bs_ref, tril):
    zu, zv = p_ref[dm.zoff:dm.zoff + dm.gw, :], p_ref[dm.zoff + dm.gw:dm.zoff + 2 * dm.gw, :]
    cu, cv = _gelu_cdf(zu), _gelu_cdf(zv)
    u, v = zu * cu, zv * cv
    v3 = _by_group(v, dm.ng)
    xc = v3 - jnp.mean(v3, axis=1, keepdims=True)
    rstd = lax.rsqrt(jnp.mean(xc * xc, axis=1, keepdims=True) + EPS)
    xh = (xc * rstd).reshape(dm.gw, BLOCK)
    vnb = (xh * lng_ref[...] + lnb_ref[...]).astype(MXU)
    wts = [jnp.where(tril, ws_ref[g], 0.0).astype(MXU) for g in range(dm.ng)]
    mixed = jnp.concatenate([_dot_nt(vnb[g * GROUP_DIM:(g + 1) * GROUP_DIM], wts[g]) + bs_ref[g:g + 1, :]
                             for g in range(dm.ng)], axis=0)
    return u * mixed, u, mixed, xh, rstd, vnb, wts, (zu, cu), (zv, cv)


def _mixer_specs(dm, nb, clamp):
    kvblk = dm.aw // (2 * dm.kvw)

    def cur(s, i):
        return (0, s * nb + clamp(i))

    def prev(s, i):
        return (kvblk, s * nb + jnp.maximum(clamp(i) - 1, 0))

    full = lambda shape: pl.BlockSpec(shape, lambda s, i: tuple(0 for _ in shape))
    return cur, prev, full


def _tril():
    return lax.broadcasted_iota(jnp.int32, (BLOCK, BLOCK), 0) >= lax.broadcasted_iota(jnp.int32, (BLOCK, BLOCK), 1)


def _mixer_fwd(proj_t, prm, dm, nseq, nb):
    gq, gk2, sinks, lng, lnb, ws, bs, goa, gog = prm
    t = proj_t.shape[1]
    cur, prev, full = _mixer_specs(dm, nb, lambda i: i)

    def body(p_ref, pkv_ref, gq_ref, gk_ref, sink_ref, lng_ref, lnb_ref, ws_ref, bs_ref, goa_ref, gog_ref, y_ref, att_scr):
        i = pl.program_id(1)
        valid, dist = _band_masks(i == 0, dm.grp)
        gqv, gkv = gq_ref[...], gk_ref[...]
        for kh in range(dm.nkv):
            _, kn, v_tb, _ = _kv_band(dm, kh, p_ref, pkv_ref, gkv)
            o = _attn_group_fwd(dm, kh, p_ref, gqv, kn, v_tb, sink_ref, valid, dist)[0]
            for g, h in enumerate(_group_heads(dm, kh)):
                att_scr[h * dm.hd:(h + 1) * dm.hd, :] = o[:, g * BLOCK:(g + 1) * BLOCK]
        att = att_scr[...]
        y_ref[:dm.aw, :] = (att * _col_rms(att) * goa_ref[...]).astype(y_ref.dtype)
        gt = _gate_fwd(dm, p_ref, lng_ref, lnb_ref, ws_ref, bs_ref, _tril())[0]
        y_ref[dm.aw:, :] = (gt * _col_rms(gt) * gog_ref[...]).astype(y_ref.dtype)

    return pl.pallas_call(
        body, name="mixer_fwd", grid=(nseq, nb),
        in_specs=[pl.BlockSpec((dm.inw, BLOCK), cur), pl.BlockSpec((2 * dm.kvw, BLOCK), prev),
                  full(gq.shape), full(gk2.shape), pl.BlockSpec(memory_space=pltpu.SMEM),
                  full(lng.shape), full(lnb.shape), full(ws.shape), full(bs.shape), full(goa.shape), full(gog.shape)],
        out_specs=pl.BlockSpec((dm.d, BLOCK), cur),
        out_shape=jax.ShapeDtypeStruct((dm.d, t), MXU),
        scratch_shapes=[pltpu.VMEM((dm.aw, BLOCK), F32)],
        compiler_params=_cparams(("parallel", "arbitrary"), 40),
    )(proj_t, proj_t, gq, gk2, sinks, lng, lnb, ws, bs, goa, gog)


def _mixer_bwd(proj_t, dy_t, prm, dm, nseq, nb):
    gq, gk2, sinks, lng, lnb, ws, bs, goa, gog = prm
    t = proj_t.shape[1]
    clamp = lambda i: jnp.minimum(i, nb - 1)
    cur, prev, full = _mixer_specs(dm, nb, clamp)
    kvw2 = 2 * dm.kvw

    def prev_kv_out(s, i):
        return (0, s * nb + jnp.maximum(i - 1, 0))

    def body(p_ref, pkv_ref, dy_ref, gq_ref, gk_ref, sink_ref, lng_ref, lnb_ref, ws_ref, bs_ref, goa_ref, gog_ref,
             dproj_ref, dkv_ref, dgq_ref, dgk_ref, dsink_ref, dlng_ref, dlnb_ref, dws_ref, dbs_ref, dgoa_ref, dgog_ref,
             att_scr, datt_scr, carry_scr, prevpart_scr, curpart_scr, kprev_scr,
             a_gq, a_gk, a_sink, a_lng, a_lnb, a_goa, a_gog):
        s_id, i = pl.program_id(0), pl.program_id(1)
        lane_accs = ((a_gq, dgq_ref), (a_gk, dgk_ref), (a_sink, dsink_ref), (a_lng, dlng_ref), (a_lnb, dlnb_ref),
                     (a_goa, dgoa_ref), (a_gog, dgog_ref))

        @pl.when((s_id == 0) & (i == 0))
        def _():
            for acc, _ in lane_accs:
                acc[...] = jnp.zeros_like(acc)
            dws_ref[...] = jnp.zeros_like(dws_ref)
            dbs_ref[...] = jnp.zeros_like(dbs_ref)

        gqv, gkv = gq_ref[...], gk_ref[...]

        @pl.when(i < nb)
        def _():
            valid, dist = _band_masks(i == 0, dm.grp)
            kvs, fwd = [], []
            for kh in range(dm.nkv):
                kv = _kv_band(dm, kh, p_ref, pkv_ref, gkv)
                kvs.append(kv)
                fwd.append(_attn_group_fwd(dm, kh, p_ref, gqv, kv[1], kv[2], sink_ref, valid, dist))
                for g, h in enumerate(_group_heads(dm, kh)):
                    att_scr[h * dm.hd:(h + 1) * dm.hd, :] = fwd[kh][0][:, g * BLOCK:(g + 1) * BLOCK]
            att = att_scr[...]
            dya = dy_ref[:dm.aw, :]
            ra = _col_rms(att)
            ah = att * ra
            a_goa[...] += dya * ah
            dah = dya * goa_ref[...]
            datt_scr[...] = ra * (dah - ah * jnp.mean(dah * ah, axis=0, keepdims=True))
            for kh in range(dm.nkv):
                kn_tb, kn, v_tb, vb = kvs[kh]
                _, p, ps, rq, qh, qnb = fwd[kh]
                heads = _group_heads(dm, kh)
                do_b = jnp.concatenate([datt_scr[h * dm.hd:(h + 1) * dm.hd, :] for h in heads], axis=1).astype(MXU)
                dp = _dot(vb, do_b)
                delta = jnp.sum(p * dp, axis=0, keepdims=True)
                dsb = (p * (dp - delta) * (dm.hd ** -0.5)).astype(MXU)
                dsink = ps * delta
                dqn = _dot(kn_tb, dsb)
                dkn = _dot_nt(qnb, dsb)
                dvb = _dot_nt(do_b, p.astype(MXU))
                a_gq[...] += dqn * qh
                dqh = dqn * gqv
                dq = rq * (dqh - qh * jnp.mean(dqh * qh, axis=0, keepdims=True))
                for g, h in enumerate(heads):
                    a_sink[h:h + 1, :] += -dsink[:, g * BLOCK:(g + 1) * BLOCK]
                    dproj_ref[h * dm.hd:(h + 1) * dm.hd, :] = dq[:, g * BLOCK:(g + 1) * BLOCK].astype(dproj_ref.dtype)
                krows = slice(kh * dm.hd, (kh + 1) * dm.hd)
                vrows = slice(dm.kvw + kh * dm.hd, dm.kvw + (kh + 1) * dm.hd)
                prevpart_scr[krows, :] = dkn[:, :BLOCK]
                prevpart_scr[vrows, :] = dvb[:, :BLOCK]
                curpart_scr[krows, :] = dkn[:, BLOCK:]
                curpart_scr[vrows, :] = dvb[:, BLOCK:]
            dproj_ref[dm.aw:dm.zoff, :] = jnp.zeros((kvw2, BLOCK), dproj_ref.dtype)
            tril = _tril()
            gt, u, mixed, xh, rstd, vnb, wts, (zu, cu), (zv, cv) = _gate_fwd(dm, p_ref, lng_ref, lnb_ref, ws_ref, bs_ref, tril)
            dyg = dy_ref[dm.aw:, :]
            rg = _col_rms(gt)
            gh = gt * rg
            a_gog[...] += dyg * gh
            dgh = dyg * gog_ref[...]
            dgt = rg * (dgh - gh * jnp.mean(dgh * gh, axis=0, keepdims=True))
            du = dgt * mixed
            dmix = dgt * u
            dmixb = dmix.astype(MXU)
            dbs_ref[...] += jnp.sum(_by_group(dmix, dm.ng), axis=1)
            dvn = []
            for g in range(dm.ng):
                rows = slice(g * GROUP_DIM, (g + 1) * GROUP_DIM)
                dws_ref[g] += jnp.where(tril, _dot_tn(dmixb[rows], vnb[rows]), 0.0)
                dvn.append(_dot(dmixb[rows], wts[g]))
            dvn = jnp.concatenate(dvn, axis=0)
            a_lng[...] += dvn * xh
            a_lnb[...] += dvn
            dxh3, xh3 = _by_group(dvn * lng_ref[...], dm.ng), _by_group(xh, dm.ng)
            dv = (rstd * (dxh3 - jnp.mean(dxh3, axis=1, keepdims=True) - xh3 * jnp.mean(dxh3 * xh3, axis=1, keepdims=True))).reshape(dm.gw, BLOCK)
            dgu = cu + zu * (jnp.exp(-0.5 * zu * zu) * _INV_SQRT_2PI)
            dgv = cv + zv * (jnp.exp(-0.5 * zv * zv) * _INV_SQRT_2PI)
            dproj_ref[dm.zoff:dm.zoff + dm.gw, :] = (du * dgu).astype(dproj_ref.dtype)
            dproj_ref[dm.zoff + dm.gw:, :] = (dv * dgv).astype(dproj_ref.dtype)

        @pl.when(i == nb)
        def _():
            prevpart_scr[...] = jnp.zeros_like(prevpart_scr)

        @pl.when(i >= 1)
        def _():
            tot = carry_scr[...] + prevpart_scr[...]
            for kh in range(dm.nkv):
                krows = slice(kh * dm.hd, (kh + 1) * dm.hd)
                kraw = kprev_scr[krows, :]
                rk = _col_rms(kraw)
                khat = kraw * rk
                dkn = tot[krows, :]
                a_gk[...] += dkn * khat
                dkh = dkn * gkv[:, :BLOCK]
                dk = rk * (dkh - khat * jnp.mean(dkh * khat, axis=0, keepdims=True))
                dkv_ref[krows, :] = dk.astype(dkv_ref.dtype)
            dkv_ref[dm.kvw:, :] = tot[dm.kvw:, :].astype(dkv_ref.dtype)

        @pl.when(i < nb)
        def _():
            carry_scr[...] = curpart_scr[...]
            kprev_scr[...] = p_ref[dm.aw:dm.aw + dm.kvw, :]

        @pl.when((s_id == nseq - 1) & (i == nb))
        def _():
            for acc, out in lane_accs:
                out[...] = jnp.sum(acc[...], axis=1, keepdims=True)

    col = lambda rows: jax.ShapeDtypeStruct((rows, 1), F32)
    lane = lambda rows: pltpu.VMEM((rows, LANES), F32)
    return pl.pallas_call(
        body, name="mixer_bwd", grid=(nseq, nb + 1),
        in_specs=[pl.BlockSpec((dm.inw, BLOCK), cur), pl.BlockSpec((kvw2, BLOCK), prev), pl.BlockSpec((dm.d, BLOCK), cur),
                  full(gq.shape), full(gk2.shape), pl.BlockSpec(memory_space=pltpu.SMEM),
                  full(lng.shape), full(lnb.shape), full(ws.shape), full(bs.shape), full(goa.shape), full(gog.shape)],
        out_specs=[pl.BlockSpec((dm.inw, BLOCK), cur), pl.BlockSpec((kvw2, BLOCK), prev_kv_out),
                   full((dm.hd, 1)), full((dm.hd, 1)), full((dm.nh, 1)), full((dm.gw, 1)), full((dm.gw, 1)), full(ws.shape),
                   full(bs.shape), full((dm.aw, 1)), full((dm.gw, 1))],
        out_shape=[jax.ShapeDtypeStruct((dm.inw, t), MXU), jax.ShapeDtypeStruct((kvw2, t), MXU),
                   col(dm.hd), col(dm.hd), col(dm.nh), col(dm.gw), col(dm.gw), jax.ShapeDtypeStruct(ws.shape, F32),
                   jax.ShapeDtypeStruct(bs.shape, F32), col(dm.aw), col(dm.gw)],
        scratch_shapes=[pltpu.VMEM((dm.aw, BLOCK), F32), pltpu.VMEM((dm.aw, BLOCK), F32),
                        pltpu.VMEM((kvw2, BLOCK), F32), pltpu.VMEM((kvw2, BLOCK), F32), pltpu.VMEM((kvw2, BLOCK), F32),
                        pltpu.VMEM((dm.kvw, BLOCK), F32),
                        pltpu.VMEM((dm.hd, dm.grp * BLOCK), F32), lane(dm.hd), lane(dm.nh), lane(dm.gw), lane(dm.gw), lane(dm.aw), lane(dm.gw)],
        compiler_params=_cparams(("arbitrary", "arbitrary"), 48),
    )(proj_t, proj_t, dy_t, gq, gk2, sinks, lng, lnb, ws, bs, goa, gog)


def _patch_kv(dproj_t, dkv_t, dm):
    t = dproj_t.shape[1]
    tc = _pick(t, (1024, 512, 256, 128))
    kvw2 = 2 * dm.kvw
    kvblk = dm.aw // kvw2

    def body(dproj_hbm, dkv_ref, out_ref):
        del dproj_hbm
        out_ref[...] = dkv_ref[...]

    return pl.pallas_call(
        body, name="patch_kv", grid=(t // tc,),
        in_specs=[ANY, pl.BlockSpec((kvw2, tc), lambda i: (0, i))],
        out_specs=pl.BlockSpec((kvw2, tc), lambda i: (kvblk, i)),
        out_shape=jax.ShapeDtypeStruct(dproj_t.shape, dproj_t.dtype),
        input_output_aliases={0: 0},
        compiler_params=_cparams(("parallel",), 32),
    )(dproj_t, dkv_t)


def _place():
    x, y, c = lax.axis_index("x"), lax.axis_index("y"), lax.axis_index("c")
    return x, y, c


def _handshake(peers):
    barrier = pltpu.get_barrier_semaphore()
    for p in peers:
        pl.semaphore_signal(barrier, inc=1, device_id=p, device_id_type=MESH)
    pl.semaphore_wait(barrier, len(peers))


def _sequencer_mesh():
    return plsc.ScalarSubcoreMesh(axis_name="sequencer", num_cores=1)


GATHER_CHUNKS = 4
BF16_ROWS = 16


def _row_chunks(n, k):
    tiles = n // BF16_ROWS
    sizes = [(tiles // k + (1 if i < tiles % k else 0)) * BF16_ROWS for i in range(k)]
    return [(sum(sizes[:i]), sz) for i, sz in enumerate(sizes) if sz]


def _allgather_weight(name, collective_id, shard, after=None):
    n = shard.shape[0]
    assert n % BF16_ROWS == 0
    chunks = _row_chunks(n, GATHER_CHUNKS)
    nc = len(chunks)

    def body(*refs):
        src, out = refs[0], refs[-4]
        send_sems, recv_sems, local_sem = refs[-3:]
        x, y, c = _place()
        me, sib, xn, yn, diag = (x, y, c), (x, y, 1 - c), (1 - x, y, c), (x, 1 - y, c), (1 - x, 1 - y, c)
        relay_to = (x ^ c, y ^ (1 - c), c)
        relay_of = (x ^ (1 - c), y ^ c, c)
        _handshake([sib, xn, yn])

        def rows(place, ci):
            px, py, pc = place
            off, size = chunks[ci]
            return out.at[pl.ds(pl.multiple_of((4 * px + 2 * py + pc) * n + off, BF16_ROWS), size), :]

        def copy(k, ci, block, to, from_src=False):
            off, size = chunks[ci]
            return pltpu.make_async_remote_copy(
                src_ref=src.at[pl.ds(off, size), :] if from_src else rows(block, ci), dst_ref=rows(block, ci),
                send_sem=send_sems.at[ci, k], recv_sem=recv_sems.at[ci, k], device_id=to, device_id_type=MESH)

        mine = pltpu.make_async_copy(src, out.at[pl.ds(pl.multiple_of((4 * x + 2 * y + c) * n, BF16_ROWS), n), :], local_sem)
        mine.start()
        sent = []
        for ci in range(nc):
            sent += [copy(0, ci, me, sib, from_src=True), copy(1, ci, me, xn, from_src=True), copy(2, ci, me, yn, from_src=True)]
        for cp in sent:
            cp.start()
        for ci in range(nc):
            copy(1, ci, xn, me).wait_recv()
            copy(2, ci, yn, me).wait_recv()
            passed = [copy(3, ci, relay_of, relay_to), copy(4, ci, xn, sib), copy(5, ci, yn, sib)]
            for cp in passed:
                cp.start()
            sent += passed
        for ci in range(nc):
            copy(3, ci, diag, me).wait_recv()
            passed = copy(6, ci, diag, sib)
            passed.start()
            sent.append(passed)
        for ci in range(nc):
            copy(0, ci, sib, me).wait_recv()
            for k, block in ((4, (1 - x, y, 1 - c)), (5, (x, 1 - y, 1 - c)), (6, (1 - x, 1 - y, 1 - c))):
                copy(k, ci, block, me).wait_recv()
        for cp in sent:
            cp.wait_send()
        mine.wait()

    return pl.kernel(
        body, name=name,
        out_type=jax.ShapeDtypeStruct((N_DEV * n, shard.shape[1]), shard.dtype),
        mesh=_sequencer_mesh(),
        scratch_types=[pltpu.SemaphoreType.DMA((nc, 7)), pltpu.SemaphoreType.DMA((nc, 7)), pltpu.SemaphoreType.DMA],
        compiler_params=pltpu.CompilerParams(collective_id=collective_id),
    )(*([shard] if after is None else [shard, after]))


_FLIPS = [(0, 0, 1), (1, 0, 0), (0, 1, 0), (1, 1, 0), (1, 0, 1), (0, 1, 1), (1, 1, 1)]


def _scatter_grad(name, collective_id, grad):
    n = grad.shape[0] // N_DEV

    def body(src, out, send_sems, recv_sems, local_sem):
        x, y, c = _place()
        me_idx = 4 * x + 2 * y + c
        peers = [(x ^ fx, y ^ fy, c ^ fc) for (fx, fy, fc) in _FLIPS]
        _handshake(peers)

        def block(idx):
            return src.at[pl.ds(pl.multiple_of(idx * n, 16), n), :]

        copies = [pltpu.make_async_remote_copy(
            src_ref=block(4 * px + 2 * py + pc), dst_ref=out.at[me_idx], send_sem=send_sems.at[k], recv_sem=recv_sems.at[k],
            device_id=(px, py, pc), device_id_type=MESH) for k, (px, py, pc) in enumerate(peers)]
        mine = pltpu.make_async_copy(block(me_idx), out.at[me_idx], local_sem)
        mine.start()
        for cp in copies:
            cp.start()
        for cp in copies:
            cp.wait_recv()
        for cp in copies:
            cp.wait_send()
        mine.wait()

    return pl.kernel(
        body, name=name,
        out_type=jax.ShapeDtypeStruct((N_DEV, n, grad.shape[1]), grad.dtype),
        mesh=_sequencer_mesh(),
        scratch_types=[pltpu.SemaphoreType.DMA((7,)), pltpu.SemaphoreType.DMA((7,)), pltpu.SemaphoreType.DMA],
        compiler_params=pltpu.CompilerParams(collective_id=collective_id),
    )(grad)


def _allgather_rows(name, collective_id, part):
    def body(src, out, send_sems, recv_sems, local_sem):
        x, y, c = _place()
        me_idx = 4 * x + 2 * y + c
        peers = [(x ^ fx, y ^ fy, c ^ fc) for (fx, fy, fc) in _FLIPS]
        _handshake(peers)
        copies = [pltpu.make_async_remote_copy(
            src_ref=src, dst_ref=out.at[me_idx], send_sem=send_sems.at[k], recv_sem=recv_sems.at[k],
            device_id=peer, device_id_type=MESH) for k, peer in enumerate(peers)]
        mine = pltpu.make_async_copy(src, out.at[me_idx], local_sem)
        mine.start()
        for cp in copies:
            cp.start()
        for cp in copies:
            cp.wait_recv()
        for cp in copies:
            cp.wait_send()
        mine.wait()

    return pl.kernel(
        body, name=name,
        out_type=jax.ShapeDtypeStruct((N_DEV,) + part.shape, part.dtype),
        mesh=_sequencer_mesh(),
        scratch_types=[pltpu.SemaphoreType.DMA((7,)), pltpu.SemaphoreType.DMA((7,)), pltpu.SemaphoreType.DMA],
        compiler_params=pltpu.CompilerParams(collective_id=collective_id),
    )(part)


def _adamw_math(w, g, m, v):
    m = ADAM_B1 * m + (1.0 - ADAM_B1) * g
    v = ADAM_B2 * v + (1.0 - ADAM_B2) * (g * g)
    m_hat = m / (1.0 - ADAM_B1 ** ADAM_STEP)
    v_hat = v / (1.0 - ADAM_B2 ** ADAM_STEP)
    delta = -ADAM_LR * (m_hat / (jnp.sqrt(v_hat) + ADAM_EPS) + ADAM_WD * w)
    return delta, m, v


def _sum_adamw(name, slots, w, m, v, after):
    _, n, kk = slots.shape
    tr = _pick(n, (208, 176, 128, 96, 64, 32, 16))

    def body(s_ref, w_ref, m_ref, v_ref, after_ref, g_ref, d_ref, nm_ref, nv_ref):
        del after_ref
        g = s_ref[0].astype(F32)
        for p in range(1, N_DEV):
            g = g + s_ref[p].astype(F32)
        g_ref[...] = g
        d_ref[...], nm_ref[...], nv_ref[...] = _adamw_math(w_ref[...], g, m_ref[...], v_ref[...])

    row = pl.BlockSpec((tr, kk), lambda i: (i, 0))
    return pl.pallas_call(
        body, name=name, grid=(n // tr,),
        in_specs=[pl.BlockSpec((N_DEV, tr, kk), lambda i: (0, i, 0)), row, row, row, ANY],
        out_specs=[row] * 4,
        out_shape=[jax.ShapeDtypeStruct((n, kk), F32)] * 4,
        compiler_params=_cparams(("parallel",), 48),
    )(slots, w, m, v, after)


def _allreduce_small_adamw(early_slots, late, w, m, v, after):
    ra, rb = early_slots.shape[1], late.shape[0]

    def body(early_ref, late_ref, w_ref, m_ref, v_ref, after_ref, g_ref, d_ref, nm_ref, nv_ref, slots, send_sems, recv_sems):
        del after_ref
        x, y, c = _place()
        me_idx = 4 * x + 2 * y + c
        copies = []
        for k, (fx, fy, fc) in enumerate(_FLIPS):
            px, py, pc = x ^ fx, y ^ fy, c ^ fc
            copies.append(pltpu.make_async_remote_copy(
                src_ref=late_ref, dst_ref=slots.at[me_idx], send_sem=send_sems.at[k], recv_sem=recv_sems.at[k],
                device_id=(px, py, pc), device_id_type=MESH))
        for cp in copies:
            cp.start()
        slots[me_idx] = late_ref[...]
        g = early_ref[0]
        for p in range(1, N_DEV):
            g = g + early_ref[p]
        early = pl.ds(0, ra)
        g_ref[early, :] = g
        d_ref[early, :], nm_ref[early, :], nv_ref[early, :] = _adamw_math(w_ref[early, :], g, m_ref[early, :], v_ref[early, :])
        for cp in copies:
            cp.wait_recv()
        for cp in copies:
            cp.wait_send()
        g = slots[0]
        for p in range(1, N_DEV):
            g = g + slots[p]
        tail = pl.ds(ra, rb)
        g_ref[tail, :] = g
        d_ref[tail, :], nm_ref[tail, :], nv_ref[tail, :] = _adamw_math(w_ref[tail, :], g, m_ref[tail, :], v_ref[tail, :])

    vm = pl.BlockSpec(memory_space=pltpu.VMEM)
    return pl.pallas_call(
        body, name="allreduce_small_adamw",
        in_specs=[vm] * 5 + [ANY], out_specs=[vm] * 4,
        out_shape=[jax.ShapeDtypeStruct((ra + rb, LANES), F32)] * 4,
        scratch_shapes=[pltpu.VMEM((N_DEV, rb, LANES), F32), pltpu.SemaphoreType.DMA((7,)), pltpu.SemaphoreType.DMA((7,))],
        compiler_params=pltpu.CompilerParams(vmem_limit_bytes=48 * MIB),
    )(early_slots, late, w, m, v, after)


def _pack(arrs):
    parts, meta, off = [], [], 0
    for a in arrs:
        flat = a.reshape(-1).astype(F32)
        rows = -(-flat.shape[0] // LANES)
        rows8 = -(-rows // 8) * 8
        flat = jnp.pad(flat, (0, rows8 * LANES - flat.shape[0]))
        parts.append(flat.reshape(rows8, LANES))
        meta.append((off, a.shape, a.size))
        off += rows8
    return jnp.concatenate(parts, axis=0), meta


def _unpack(packed, meta):
    outs = []
    for off, shape, size in meta:
        rows = -(-size // LANES)
        outs.append(packed[off:off + rows].reshape(-1)[:size].reshape(shape))
    return outs


def _silu_parts(a):
    sg = 0.5 + 0.5 * jnp.tanh(0.5 * a)
    return a * sg, sg * (1.0 + a * (1.0 - sg))


def kernel(x, norm1_g, w_in, q_norm_g, k_norm_g, attn_sinks, gate_ln_g, gate_ln_b, w_spatial, b_spatial, out_norm_attn_g, out_norm_gate_g, w_out, norm2_g, w_ffn_gate, w_ffn_up, w_ffn_down, loss_target, m_norm1_g, m_w_in, m_q_norm_g, m_k_norm_g, m_attn_sinks, m_gate_ln_g, m_gate_ln_b, m_w_spatial, m_b_spatial, m_out_norm_attn_g, m_out_norm_gate_g, m_w_out, m_norm2_g, m_w_ffn_gate, m_w_ffn_up, m_w_ffn_down, v_norm1_g, v_w_in, v_q_norm_g, v_k_norm_g, v_attn_sinks, v_gate_ln_g, v_gate_ln_b, v_w_spatial, v_b_spatial, v_out_norm_attn_g, v_out_norm_gate_g, v_w_out, v_norm2_g, v_w_ffn_gate, v_w_ffn_up, v_w_ffn_down):
    nseq, seq, d = x.shape
    t = nseq * seq
    nb = seq // BLOCK
    inw = w_in.shape[2] * N_DEV
    dm = _Dims(d, inw, q_norm_g.shape[-1])
    xf = x.reshape(t, d)
    tgt = loss_target.reshape(t, d)

    rows = lambda wv, transposed: jnp.swapaxes(wv, 1, 2)[0] if transposed else wv[0]
    big = {"w_in": (w_in, m_w_in, v_w_in, True), "w_out": (w_out, m_w_out, v_w_out, False),
           "w_ffn_gate": (w_ffn_gate, m_w_ffn_gate, v_w_ffn_gate, True), "w_ffn_up": (w_ffn_up, m_w_ffn_up, v_w_ffn_up, True),
           "w_ffn_down": (w_ffn_down, m_w_ffn_down, v_w_ffn_down, False)}
    big_rows = {nm: tuple(rows(arr, tr) for arr in (wv, mv, vv)) for nm, (wv, mv, vv, tr) in big.items()}
    shard = lambda nm: big_rows[nm][0].astype(WIRE)
    win_t = _allgather_weight("gather_w_in", 1, shard("w_in"))
    wout = _allgather_weight("gather_w_out", 2, shard("w_out"), after=win_t)
    wg_t = _allgather_weight("gather_w_ffn_gate", 3, shard("w_ffn_gate"), after=win_t)
    wu_t = _allgather_weight("gather_w_ffn_up", 9, shard("w_ffn_up"), after=win_t)
    wd = _allgather_weight("gather_w_ffn_down", 10, shard("w_ffn_down"), after=win_t)

    lanes = lambda v, n=BLOCK: jnp.broadcast_to(v.reshape(-1, 1), (v.size, n))
    prm = (lanes(q_norm_g, dm.grp * BLOCK), lanes(k_norm_g, 2 * BLOCK), attn_sinks[0], lanes(gate_ln_g), lanes(gate_ln_b), w_spatial[0], b_spatial[0],
           lanes(out_norm_attn_g), lanes(out_norm_gate_g))

    h1 = _rms_fwd("rms1_fwd", xf, norm1_g)
    (proj_t,) = _matmul("mm_in", win_t, h1, "nt", [F32])
    y_t = _mixer_fwd(proj_t, prm, dm, nseq, nb)

    def residual_norm(acc, xr, g2):
        x2v = xr + acc
        return x2v, x2v * lax.rsqrt(jnp.mean(x2v * x2v, axis=-1, keepdims=True) + EPS) * g2

    x2, h2 = _matmul("mm_out", y_t, wout, "tn", [F32, MXU], epilogue=residual_norm, extras=[xf], rowvecs=[norm2_g], full_rows=True)
    a, b, s = _matmul("mm_gate_up", h2, wg_t, "nt", [MXU, MXU, MXU], b2=wu_t, epilogue=lambda ga, ub: (ga, ub, _silu_parts(ga)[0] * ub))

    def loss_epilogue(acc, x2v, tv):
        diff = (x2v + acc) - tv
        dx3 = diff * (1.0 / d)
        return dx3, dx3, jnp.sum(diff * diff)

    dx3, dx3b, lossp = _matmul("mm_down", s, wd, "nn", [F32, MXU], epilogue=loss_epilogue, extras=[x2, tgt], partial=True)
    loss_part = (0.5 / d) * jnp.sum(lossp[::8, ::LANES])

    def dswiglu(acc, av, bv):
        silu, dsilu = _silu_parts(av.astype(F32))
        return acc * bv.astype(F32) * dsilu, acc * silu

    da, db = _matmul("mm_d_down", dx3b, wd, "nt", [MXU, MXU], epilogue=dswiglu, extras=[a, b])
    (g_wd,) = _matmul("mm_gw_down", s, dx3b, "tn", [WIRE])
    sl_wd = _scatter_grad("scatter_w_ffn_down", 4, g_wd)
    (dh2a,) = _matmul("mm_dh2_gate", da, wg_t, "nn", [F32], after=[g_wd])
    (g_wg,) = _matmul("mm_gw_gate", da, h2, "tn", [WIRE], after=[dh2a])
    sl_wg = _scatter_grad("scatter_w_ffn_gate", 5, g_wg)
    (dh2,) = _matmul("mm_dh2_up", db, wu_t, "nn", [F32], epilogue=lambda acc, pv: (pv + acc,), extras=[dh2a], after=[g_wg])
    (g_wu,) = _matmul("mm_gw_up", db, h2, "tn", [WIRE], after=[dh2])
    sl_wu = _scatter_grad("scatter_w_ffn_up", 6, g_wu)

    dy_t, dx2, dx2b, dg2 = _norm_bwd_matmul("mm_d_out", wout, dh2, x2, norm2_g, dx3, after=g_wu)
    (g_wout,) = _matmul("mm_gw_out", y_t, dx2b, "nn", [WIRE], after=[dy_t])
    sl_wout = _scatter_grad("scatter_w_out", 7, g_wout)
    (dproj0, dkv, dgq, dgk, dsink, dlng, dlnb, dws, dbs, dgoa, dgog) = _mixer_bwd(proj_t, dy_t, prm, dm, nseq, nb)
    early_g = [dgq, dgk, dsink, dlng, dlnb, dws, dbs, dgoa, dgog, dg2, loss_part.reshape(1)]
    early_slots = _allgather_rows("gather_small_grads", 11, _pack(early_g)[0])
    dproj_t = _patch_kv(dproj0, dkv, dm)
    (g_win,) = _matmul("mm_gw_in", dproj_t, h1, "nn", [WIRE])
    sl_win = _scatter_grad("scatter_w_in", 8, g_win)

    def norm1_backward(dh1, xv, dx2v, g1):
        r = lax.rsqrt(jnp.mean(xv * xv, axis=-1, keepdims=True) + EPS)
        xh = xv * r
        dxh = dh1 * g1
        return dx2v + r * (dxh - xh * jnp.mean(dxh * xh, axis=-1, keepdims=True)), jnp.sum(dh1 * xh, axis=0, keepdims=True)

    dx, dg1 = _matmul("mm_d_in", dproj_t, win_t, "tn", [F32], epilogue=norm1_backward, extras=[xf, dx2], rowvecs=[norm1_g],
                      after=[g_win], col_sum=True, full_rows=True)

    big_out = {}
    last = dx

    def big_update(nm, sl, after):
        res = _sum_adamw("adamw_" + nm, sl, *big_rows[nm], after=after)
        big_out[nm] = tuple(jnp.swapaxes(r[None], 1, 2) if big[nm][3] else r[None] for r in res)
        return res[1]

    for nm, sl in (("w_ffn_down", sl_wd), ("w_ffn_gate", sl_wg), ("w_ffn_up", sl_wu), ("w_out", sl_wout)):
        last = big_update(nm, sl, last)

    zero = jnp.zeros((1,), F32)
    small_names = ["q_norm_g", "k_norm_g", "attn_sinks", "gate_ln_g", "gate_ln_b", "w_spatial", "b_spatial",
                   "out_norm_attn_g", "out_norm_gate_g", "norm2_g", "loss", "norm1_g"]
    small_w = [q_norm_g, k_norm_g, attn_sinks, gate_ln_g, gate_ln_b, w_spatial, b_spatial, out_norm_attn_g, out_norm_gate_g, norm2_g, zero, norm1_g]
    small_m = [m_q_norm_g, m_k_norm_g, m_attn_sinks, m_gate_ln_g, m_gate_ln_b, m_w_spatial, m_b_spatial, m_out_norm_attn_g, m_out_norm_gate_g, m_norm2_g, zero, m_norm1_g]
    small_v = [v_q_norm_g, v_k_norm_g, v_attn_sinks, v_gate_ln_g, v_gate_ln_b, v_w_spatial, v_b_spatial, v_out_norm_attn_g, v_out_norm_gate_g, v_norm2_g, zero, v_norm1_g]
    pw, meta = _pack(small_w)
    sg, sd, sm, sv = _allreduce_small_adamw(early_slots, _pack([dg1])[0], pw, _pack(small_m)[0], _pack(small_v)[0], after=last)
    big_update("w_in", sl_win, sd)
    ug, ud, um, uv = _unpack(sg, meta), _unpack(sd, meta), _unpack(sm, meta), _unpack(sv, meta)
    small_out = {nm: (ug[k], ud[k], um[k], uv[k]) for k, nm in enumerate(small_names)}
    loss = small_out["loss"][0].reshape(())

    order = ["norm1_g", "w_in", "q_norm_g", "k_norm_g", "attn_sinks", "gate_ln_g", "gate_ln_b", "w_spatial", "b_spatial",
             "out_norm_attn_g", "out_norm_gate_g", "w_out", "norm2_g", "w_ffn_gate", "w_ffn_up", "w_ffn_down"]
    allo = {**big_out, **small_out}
    outs = [loss, dx.reshape(nseq, seq, d)]
    for k in range(4):
        outs += [allo[nm][k] for nm in order]
    return tuple(outs)
```

```python
import math

import jax
import jax.numpy as jnp
from jax import lax
from jax.experimental import pallas as pl
from jax.experimental.pallas import tpu as pltpu
from jax.experimental.pallas import tpu_sc as plsc

F32 = jnp.float32
MXU = jnp.bfloat16
WIRE = jnp.bfloat16
EPS = 1e-6
BLOCK = 128
GROUP_DIM = 128
N_KV_HEADS = 2
NEG = -1e30
N_DEV = 8
LANES = 128
MIB = 1024 * 1024

ADAM_LR = 0.001
ADAM_B1 = 0.9
ADAM_B2 = 0.999
ADAM_EPS = 1e-08
ADAM_WD = 0.01
ADAM_STEP = 10

MESH = pl.DeviceIdType.MESH
ANY = pl.BlockSpec(memory_space=pl.ANY)


def _pick(n, cands):
    for c in cands:
        if n % c == 0:
            return c
    return n


def _cparams(sem, vmem_mb):
    return pltpu.CompilerParams(dimension_semantics=sem, vmem_limit_bytes=vmem_mb * MIB)


VMEM_TILE_BUDGET = 44 * MIB
HBM_BYTES_PER_US = 3.0e6
STEP_US = 0.4
MIN_TILE_N = 512


def _tile_candidates(n):
    return [c for c in range(min(n, 2048), 0, -LANES) if n % c == 0 and c % LANES == 0] or [n]


def _matmul_tiles(m, n, kk, esz, n_b, extra_sizes, out_sizes, full_rows, n_acc):
    best = None
    wide = [n] if full_rows else [c for c in _tile_candidates(n) if c >= MIN_TILE_N] or _tile_candidates(n)
    for tm in _tile_candidates(m):
        for tn in wide:
            vmem = 2 * (tm + n_b * tn) * kk * esz + tm * tn * (4 * n_b * (3 * n_acc - 2) + 2 * sum(extra_sizes) + 2 * sum(out_sizes))
            if vmem > VMEM_TILE_BUDGET:
                continue
            cost = (m // tm) * n_b * n * kk * esz / HBM_BYTES_PER_US + (m // tm) * (n // tn) * STEP_US
            if best is None or cost < best[0]:
                best = (cost, tm, tn, vmem)
    assert best is not None, (m, n, kk)
    return best[1:]


def _matmul(name, a, b, mode, out_dtypes, epilogue=None, extras=(), rowvecs=(), after=(), partial=False, col_sum=False, full_rows=False,
            b2=None, overlap=False):
    if mode == "nn":
        (m, kk), n = a.shape, b.shape[1]
        dn = (((1,), (0,)), ((), ()))
    elif mode == "nt":
        (m, kk), n = a.shape, b.shape[0]
        dn = (((1,), (1,)), ((), ()))
    else:
        (kk, m), n = a.shape, b.shape[1]
        dn = (((0,), (0,)), ((), ()))
    bs = [b] if b2 is None else [b, b2]
    tm, tn, vmem = _matmul_tiles(m, n, kk, a.dtype.itemsize, len(bs), [e.dtype.itemsize for e in extras],
                                 [jnp.dtype(dt).itemsize for dt in out_dtypes], full_rows, 2 if overlap else 1)
    ni, nj = m // tm, n // tn
    if overlap:
        grid = (ni * nj + 1,)
        at_dot = lambda f: (lambda s: f(jnp.minimum(s, ni * nj - 1) // nj, jnp.minimum(s, ni * nj - 1) % nj))
        at_epi = lambda f: (lambda s: f(jnp.maximum(s - 1, 0) // nj, jnp.maximum(s - 1, 0) % nj))
    else:
        grid = (ni, nj)
        at_dot = at_epi = lambda f: f
    a_spec = pl.BlockSpec((kk, tm), at_dot(lambda i, j: (0, i))) if mode == "tn" else pl.BlockSpec((tm, kk), at_dot(lambda i, j: (i, 0)))
    b_spec = pl.BlockSpec((tn, kk), at_dot(lambda i, j: (j, 0))) if mode == "nt" else pl.BlockSpec((kk, tn), at_dot(lambda i, j: (0, j)))
    tile = pl.BlockSpec((tm, tn), at_epi(lambda i, j: (i, j)))
    row = pl.BlockSpec((1, tn), at_epi(lambda i, j: (0, j)))
    nb, ne, nr, na, no = len(bs), len(extras), len(rowvecs), len(after), len(out_dtypes)
    n_out = no + partial + col_sum

    def body(a_ref, *rest):
        b_refs, in_refs = rest[:nb], rest[nb:nb + ne + nr]
        out_refs, scr = rest[nb + ne + nr + na:nb + ne + nr + na + n_out], rest[nb + ne + nr + na + n_out:]
        step = pl.program_id(0)

        def finish(accs, count):
            vals = tuple(accs) if epilogue is None else epilogue(*accs, *[r[...] for r in in_refs])
            for o_ref, t in zip(out_refs[:no], vals[:no]):
                o_ref[...] = t.astype(o_ref.dtype)
            if partial:
                out_refs[no][...] = jnp.full((8, LANES), vals[no], F32)
            if col_sum:
                sum_ref = out_refs[-1]

                @pl.when(step == 0)
                def _():
                    sum_ref[...] = jnp.zeros_like(sum_ref)

                @pl.when(count)
                def _():
                    sum_ref[...] += vals[-1]

        def dots():
            av = a_ref[...]
            return [lax.dot_general(av, b_ref[...], dn, preferred_element_type=F32) for b_ref in b_refs]

        if not overlap:
            finish(dots(), step >= 0)
            return

        @pl.when(step == 0)
        def _():
            for r in scr[nb:]:
                r[...] = jnp.zeros_like(r)

        def one_step(write, read):
            accs = dots()
            prev = [r[...] for r in read]
            for w, acc in zip(write, accs):
                w[...] = acc
            finish(prev, step > 0)

        @pl.when(step % 2 == 0)
        def _():
            one_step(scr[:nb], scr[nb:])

        @pl.when(step % 2 == 1)
        def _():
            one_step(scr[nb:], scr[:nb])

    out_specs = [tile] * no
    out_shape = [jax.ShapeDtypeStruct((m, n), dt) for dt in out_dtypes]
    if partial:
        out_specs.append(pl.BlockSpec((8, LANES), at_epi(lambda i, j: (i, j))))
        out_shape.append(jax.ShapeDtypeStruct((ni * 8, nj * LANES), F32))
    if col_sum:
        out_specs.append(row)
        out_shape.append(jax.ShapeDtypeStruct((1, n), F32))
    if overlap:
        sem = ("arbitrary",)
    else:
        sem = ("arbitrary" if col_sum else "parallel", "arbitrary")
    return pl.pallas_call(
        body, name=name, grid=grid,
        in_specs=[a_spec] + [b_spec] * nb + [tile] * ne + [row] * nr + [ANY] * na,
        out_specs=out_specs, out_shape=out_shape,
        scratch_shapes=[pltpu.VMEM((tm, tn), F32)] * (2 * nb) if overlap else [],
        compiler_params=_cparams(sem, min(vmem // MIB + 8, 60)),
    )(a, *bs, *extras, *rowvecs, *after)


def _rms_fwd(name, x, g):
    t, d = x.shape
    tm = _pick(t, (512, 256, 128))

    def body(x_ref, g_ref, h_ref):
        xv = x_ref[...]
        r = lax.rsqrt(jnp.mean(xv * xv, axis=-1, keepdims=True) + EPS)
        h_ref[...] = (xv * r * g_ref[...]).astype(h_ref.dtype)

    return pl.pallas_call(
        body, name=name, grid=(t // tm,),
        in_specs=[pl.BlockSpec((tm, d), lambda i: (i, 0)), pl.BlockSpec((1, d), lambda i: (0, 0))],
        out_specs=pl.BlockSpec((tm, d), lambda i: (i, 0)),
        out_shape=jax.ShapeDtypeStruct((t, d), MXU),
        compiler_params=_cparams(("parallel",), 32),
    )(x, g)


def _norm_bwd_matmul(name, w, dh, x, g, res, after):
    t, d = x.shape
    m = w.shape[0]
    tn = _pick(t, (256, 128))

    def body(w_ref, dh_ref, x_ref, g_ref, res_ref, after_ref, out_ref, dx_ref, dxb_ref, dg_ref):
        del after_ref

        @pl.when(pl.program_id(0) == 0)
        def _():
            dg_ref[...] = jnp.zeros_like(dg_ref)

        xv, dhv = x_ref[...], dh_ref[...]
        r = lax.rsqrt(jnp.mean(xv * xv, axis=-1, keepdims=True) + EPS)
        xh = xv * r
        dg_ref[...] += jnp.sum(dhv * xh, axis=0, keepdims=True)
        dxh = dhv * g_ref[...]
        dx = res_ref[...] + r * (dxh - xh * jnp.mean(dxh * xh, axis=-1, keepdims=True))
        dx_ref[...] = dx
        dxb = dx.astype(MXU)
        dxb_ref[...] = dxb
        out_ref[...] = lax.dot_general(w_ref[...], dxb, (((1,), (1,)), ((), ())), preferred_element_type=F32)

    row = pl.BlockSpec((tn, d), lambda j: (j, 0))
    vec = pl.BlockSpec((1, d), lambda j: (0, 0))
    return pl.pallas_call(
        body, name=name, grid=(t // tn,),
        in_specs=[pl.BlockSpec((m, d), lambda j: (0, 0)), row, row, vec, row, ANY],
        out_specs=[pl.BlockSpec((m, tn), lambda j: (0, j)), row, row, vec],
        out_shape=[jax.ShapeDtypeStruct((m, t), F32), jax.ShapeDtypeStruct((t, d), F32), jax.ShapeDtypeStruct((t, d), MXU),
                   jax.ShapeDtypeStruct((1, d), F32)],
        compiler_params=_cparams(("arbitrary",), 52),
    )(w, dh, x, g, res, after)


_INV_SQRT2 = 0.7071067811865476
_INV_SQRT_2PI = 0.3989422804014327


def _dot_nt(a, b):
    return lax.dot_general(a, b, (((1,), (1,)), ((), ())), preferred_element_type=F32)


def _dot_tn(a, b):
    return lax.dot_general(a, b, (((0,), (0,)), ((), ())), preferred_element_type=F32)


def _dot(a, b):
    return jnp.dot(a, b, preferred_element_type=F32)


def _col_rms(v):
    return lax.rsqrt(jnp.mean(v * v, axis=0, keepdims=True) + EPS)


class _Dims:
    def __init__(self, d_model, in_width, head_dim):
        self.d = d_model
        self.aw = d_model // 2
        self.gw = d_model - self.aw
        self.kvw = (in_width - self.aw - 2 * self.gw) // 2
        self.hd = head_dim
        self.nh = self.aw // head_dim
        self.nkv = self.kvw // head_dim
        self.grp = self.nh // self.nkv
        self.ng = self.gw // GROUP_DIM
        self.inw = in_width
        self.zoff = self.aw + 2 * self.kvw
        assert self.nkv == N_KV_HEADS and self.zoff + 2 * self.gw == in_width and self.aw % (2 * self.kvw) == 0


def _band_masks(first, grp):
    kj = lax.broadcasted_iota(jnp.int32, (2 * BLOCK, grp * BLOCK), 0)
    qi = lax.broadcasted_iota(jnp.int32, (2 * BLOCK, grp * BLOCK), 1) & (BLOCK - 1)
    dist = qi + BLOCK - kj
    valid = (dist >= 0) & (dist < BLOCK) & ((kj >= BLOCK) | jnp.logical_not(first))
    return valid, dist.astype(F32)


def _kv_band(dm, kh, p_ref, pkv_ref, gk2):
    ko = dm.aw + kh * dm.hd
    vo = dm.aw + dm.kvw + kh * dm.hd
    k_t = jnp.concatenate([pkv_ref[kh * dm.hd:(kh + 1) * dm.hd, :], p_ref[ko:ko + dm.hd, :]], axis=1)
    v_t = jnp.concatenate([pkv_ref[dm.kvw + kh * dm.hd:dm.kvw + (kh + 1) * dm.hd, :], p_ref[vo:vo + dm.hd, :]], axis=1)
    kn_t = k_t * _col_rms(k_t) * gk2
    return kn_t.astype(MXU), kn_t.T.astype(MXU), v_t.astype(MXU), v_t.T.astype(MXU)


def _group_heads(dm, kh):
    return range(kh * dm.grp, (kh + 1) * dm.grp)


def _lane_row(vals):
    return jnp.concatenate([jnp.full((1, BLOCK), v, F32) for v in vals], axis=1)


def _attn_group_fwd(dm, kh, p_ref, gq, kn, v_tb, sink_ref, valid, dist):
    heads = _group_heads(dm, kh)
    q = jnp.concatenate([p_ref[h * dm.hd:(h + 1) * dm.hd, :] for h in heads], axis=1)
    rq = _col_rms(q)
    qh = q * rq
    qnb = (qh * gq).astype(MXU)
    slopes = _lane_row([math.pow(2.0, -8.0 * (h + 1) / dm.nh) for h in heads])
    sinks = _lane_row([sink_ref[h] for h in heads])
    s = _dot(kn, qnb) * (dm.hd ** -0.5)
    logits = jnp.where(valid, s - slopes * dist, NEG)
    m = jnp.maximum(jnp.max(logits, axis=0, keepdims=True), sinks)
    e = jnp.exp(logits - m)
    es = jnp.exp(sinks - m)
    inv = 1.0 / (jnp.sum(e, axis=0, keepdims=True) + es)
    p = e * inv
    o = _dot(v_tb, p.astype(MXU))
    return o, p, es * inv, rq, qh, qnb


def _gelu_cdf(z):
    return 0.5 * (1.0 + lax.erf(z * _INV_SQRT2))


def _by_group(v, ng):
    return v.reshape(ng, GROUP_DIM, v.shape[1])


def _gate_fwd(dm, p_ref, lng_ref, lnb_ref, ws_ref, bs_ref, tril):
    zu, zv = p_ref[dm.zoff:dm.zoff + dm.gw, :], p_ref[dm.zoff + dm.gw:dm.zoff + 2 * dm.gw, :]
    cu, cv = _gelu_cdf(zu), _gelu_cdf(zv)
    u, v = zu * cu, zv * cv
    v3 = _by_group(v, dm.ng)
    xc = v3 - jnp.mean(v3, axis=1, keepdims=True)
    rstd = lax.rsqrt(jnp.mean(xc * xc, axis=1, keepdims=True) + EPS)
    xh = (xc * rstd).reshape(dm.gw, BLOCK)
    vnb = (xh * lng_ref[...] + lnb_ref[...]).astype(MXU)
    wts = [jnp.where(tril, ws_ref[g], 0.0).astype(MXU) for g in range(dm.ng)]
    mixed = jnp.concatenate([_dot_nt(vnb[g * GROUP_DIM:(g + 1) * GROUP_DIM], wts[g]) + bs_ref[g:g + 1, :]
                             for g in range(dm.ng)], axis=0)
    return u * mixed, u, mixed, xh, rstd, vnb, wts, (zu, cu), (zv, cv)


def _mixer_specs(dm, nb, clamp):
    kvblk = dm.aw // (2 * dm.kvw)

    def cur(s, i):
        return (0, s * nb + clamp(i))

    def prev(s, i):
        return (kvblk, s * nb + jnp.maximum(clamp(i) - 1, 0))

    full = lambda shape: pl.BlockSpec(shape, lambda s, i: tuple(0 for _ in shape))
    return cur, prev, full


def _tril():
    return lax.broadcasted_iota(jnp.int32, (BLOCK, BLOCK), 0) >= lax.broadcasted_iota(jnp.int32, (BLOCK, BLOCK), 1)


def _mixer_fwd(proj_t, prm, dm, nseq, nb):
    gq, gk2, sinks, lng, lnb, ws, bs, goa, gog = prm
    t = proj_t.shape[1]
    cur, prev, full = _mixer_specs(dm, nb, lambda i: i)

    def body(p_ref, pkv_ref, gq_ref, gk_ref, sink_ref, lng_ref, lnb_ref, ws_ref, bs_ref, goa_ref, gog_ref, y_ref, att_scr):
        i = pl.program_id(1)
        valid, dist = _band_masks(i == 0, dm.grp)
        gqv, gkv = gq_ref[...], gk_ref[...]
        for kh in range(dm.nkv):
            _, kn, v_tb, _ = _kv_band(dm, kh, p_ref, pkv_ref, gkv)
            o = _attn_group_fwd(dm, kh, p_ref, gqv, kn, v_tb, sink_ref, valid, dist)[0]
            for g, h in enumerate(_group_heads(dm, kh)):
                att_scr[h * dm.hd:(h + 1) * dm.hd, :] = o[:, g * BLOCK:(g + 1) * BLOCK]
        att = att_scr[...]
        y_ref[:dm.aw, :] = (att * _col_rms(att) * goa_ref[...]).astype(y_ref.dtype)
        gt = _gate_fwd(dm, p_ref, lng_ref, lnb_ref, ws_ref, bs_ref, _tril())[0]
        y_ref[dm.aw:, :] = (gt * _col_rms(gt) * gog_ref[...]).astype(y_ref.dtype)

    return pl.pallas_call(
        body, name="mixer_fwd", grid=(nseq, nb),
        in_specs=[pl.BlockSpec((dm.inw, BLOCK), cur), pl.BlockSpec((2 * dm.kvw, BLOCK), prev),
                  full(gq.shape), full(gk2.shape), pl.BlockSpec(memory_space=pltpu.SMEM),
                  full(lng.shape), full(lnb.shape), full(ws.shape), full(bs.shape), full(goa.shape), full(gog.shape)],
        out_specs=pl.BlockSpec((dm.d, BLOCK), cur),
        out_shape=jax.ShapeDtypeStruct((dm.d, t), MXU),
        scratch_shapes=[pltpu.VMEM((dm.aw, BLOCK), F32)],
        compiler_params=_cparams(("parallel", "arbitrary"), 40),
    )(proj_t, proj_t, gq, gk2, sinks, lng, lnb, ws, bs, goa, gog)


def _mixer_bwd(proj_t, dy_t, prm, dm, nseq, nb):
    gq, gk2, sinks, lng, lnb, ws, bs, goa, gog = prm
    t = proj_t.shape[1]
    clamp = lambda i: jnp.minimum(i, nb - 1)
    cur, prev, full = _mixer_specs(dm, nb, clamp)
    kvw2 = 2 * dm.kvw

    def prev_kv_out(s, i):
        return (0, s * nb + jnp.maximum(i - 1, 0))

    def body(p_ref, pkv_ref, dy_ref, gq_ref, gk_ref, sink_ref, lng_ref, lnb_ref, ws_ref, bs_ref, goa_ref, gog_ref,
             dproj_ref, dkv_ref, dgq_ref, dgk_ref, dsink_ref, dlng_ref, dlnb_ref, dws_ref, dbs_ref, dgoa_ref, dgog_ref,
             att_scr, datt_scr, carry_scr, prevpart_scr, curpart_scr, kprev_scr,
             a_gq, a_gk, a_sink, a_lng, a_lnb, a_goa, a_gog):
        s_id, i = pl.program_id(0), pl.program_id(1)
        lane_accs = ((a_gq, dgq_ref), (a_gk, dgk_ref), (a_sink, dsink_ref), (a_lng, dlng_ref), (a_lnb, dlnb_ref),
                     (a_goa, dgoa_ref), (a_gog, dgog_ref))

        @pl.when((s_id == 0) & (i == 0))
        def _():
            for acc, _ in lane_accs:
                acc[...] = jnp.zeros_like(acc)
            dws_ref[...] = jnp.zeros_like(dws_ref)
            dbs_ref[...] = jnp.zeros_like(dbs_ref)

        gqv, gkv = gq_ref[...], gk_ref[...]

        @pl.when(i < nb)
        def _():
            valid, dist = _band_masks(i == 0, dm.grp)
            kvs, fwd = [], []
            for kh in range(dm.nkv):
                kv = _kv_band(dm, kh, p_ref, pkv_ref, gkv)
                kvs.append(kv)
                fwd.append(_attn_group_fwd(dm, kh, p_ref, gqv, kv[1], kv[2], sink_ref, valid, dist))
                for g, h in enumerate(_group_heads(dm, kh)):
                    att_scr[h * dm.hd:(h + 1) * dm.hd, :] = fwd[kh][0][:, g * BLOCK:(g + 1) * BLOCK]
            att = att_scr[...]
            dya = dy_ref[:dm.aw, :]
            ra = _col_rms(att)
            ah = att * ra
            a_goa[...] += dya * ah
            dah = dya * goa_ref[...]
            datt_scr[...] = ra * (dah - ah * jnp.mean(dah * ah, axis=0, keepdims=True))
            for kh in range(dm.nkv):
                kn_tb, kn, v_tb, vb = kvs[kh]
                _, p, ps, rq, qh, qnb = fwd[kh]
                heads = _group_heads(dm, kh)
                do_b = jnp.concatenate([datt_scr[h * dm.hd:(h + 1) * dm.hd, :] for h in heads], axis=1).astype(MXU)
                dp = _dot(vb, do_b)
                delta = jnp.sum(p * dp, axis=0, keepdims=True)
                dsb = (p * (dp - delta) * (dm.hd ** -0.5)).astype(MXU)
                dsink = ps * delta
                dqn = _dot(kn_tb, dsb)
                dkn = _dot_nt(qnb, dsb)
                dvb = _dot_nt(do_b, p.astype(MXU))
                a_gq[...] += dqn * qh
                dqh = dqn * gqv
                dq = rq * (dqh - qh * jnp.mean(dqh * qh, axis=0, keepdims=True))
                for g, h in enumerate(heads):
                    a_sink[h:h + 1, :] += -dsink[:, g * BLOCK:(g + 1) * BLOCK]
                    dproj_ref[h * dm.hd:(h + 1) * dm.hd, :] = dq[:, g * BLOCK:(g + 1) * BLOCK].astype(dproj_ref.dtype)
                krows = slice(kh * dm.hd, (kh + 1) * dm.hd)
                vrows = slice(dm.kvw + kh * dm.hd, dm.kvw + (kh + 1) * dm.hd)
                prevpart_scr[krows, :] = dkn[:, :BLOCK]
                prevpart_scr[vrows, :] = dvb[:, :BLOCK]
                curpart_scr[krows, :] = dkn[:, BLOCK:]
                curpart_scr[vrows, :] = dvb[:, BLOCK:]
            dproj_ref[dm.aw:dm.zoff, :] = jnp.zeros((kvw2, BLOCK), dproj_ref.dtype)
            tril = _tril()
            gt, u, mixed, xh, rstd, vnb, wts, (zu, cu), (zv, cv) = _gate_fwd(dm, p_ref, lng_ref, lnb_ref, ws_ref, bs_ref, tril)
            dyg = dy_ref[dm.aw:, :]
            rg = _col_rms(gt)
            gh = gt * rg
            a_gog[...] += dyg * gh
            dgh = dyg * gog_ref[...]
            dgt = rg * (dgh - gh * jnp.mean(dgh * gh, axis=0, keepdims=True))
            du = dgt * mixed
            dmix = dgt * u
            dmixb = dmix.astype(MXU)
            dbs_ref[...] += jnp.sum(_by_group(dmix, dm.ng), axis=1)
            dvn = []
            for g in range(dm.ng):
                rows = slice(g * GROUP_DIM, (g + 1) * GROUP_DIM)
                dws_ref[g] += jnp.where(tril, _dot_tn(dmixb[rows], vnb[rows]), 0.0)
                dvn.append(_dot(dmixb[rows], wts[g]))
            dvn = jnp.concatenate(dvn, axis=0)
            a_lng[...] += dvn * xh
            a_lnb[...] += dvn
            dxh3, xh3 = _by_group(dvn * lng_ref[...], dm.ng), _by_group(xh, dm.ng)
            dv = (rstd * (dxh3 - jnp.mean(dxh3, axis=1, keepdims=True) - xh3 * jnp.mean(dxh3 * xh3, axis=1, keepdims=True))).reshape(dm.gw, BLOCK)
            dgu = cu + zu * (jnp.exp(-0.5 * zu * zu) * _INV_SQRT_2PI)
            dgv = cv + zv * (jnp.exp(-0.5 * zv * zv) * _INV_SQRT_2PI)
            dproj_ref[dm.zoff:dm.zoff + dm.gw, :] = (du * dgu).astype(dproj_ref.dtype)
            dproj_ref[dm.zoff + dm.gw:, :] = (dv * dgv).astype(dproj_ref.dtype)

        @pl.when(i == nb)
        def _():
            prevpart_scr[...] = jnp.zeros_like(prevpart_scr)

        @pl.when(i >= 1)
        def _():
            tot = carry_scr[...] + prevpart_scr[...]
            for kh in range(dm.nkv):
                krows = slice(kh * dm.hd, (kh + 1) * dm.hd)
                kraw = kprev_scr[krows, :]
                rk = _col_rms(kraw)
                khat = kraw * rk
                dkn = tot[krows, :]
                a_gk[...] += dkn * khat
                dkh = dkn * gkv[:, :BLOCK]
                dk = rk * (dkh - khat * jnp.mean(dkh * khat, axis=0, keepdims=True))
                dkv_ref[krows, :] = dk.astype(dkv_ref.dtype)
            dkv_ref[dm.kvw:, :] = tot[dm.kvw:, :].astype(dkv_ref.dtype)

        @pl.when(i < nb)
        def _():
            carry_scr[...] = curpart_scr[...]
            kprev_scr[...] = p_ref[dm.aw:dm.aw + dm.kvw, :]

        @pl.when((s_id == nseq - 1) & (i == nb))
        def _():
            for acc, out in lane_accs:
                out[...] = jnp.sum(acc[...], axis=1, keepdims=True)

    col = lambda rows: jax.ShapeDtypeStruct((rows, 1), F32)
    lane = lambda rows: pltpu.VMEM((rows, LANES), F32)
    return pl.pallas_call(
        body, name="mixer_bwd", grid=(nseq, nb + 1),
        in_specs=[pl.BlockSpec((dm.inw, BLOCK), cur), pl.BlockSpec((kvw2, BLOCK), prev), pl.BlockSpec((dm.d, BLOCK), cur),
                  full(gq.shape), full(gk2.shape), pl.BlockSpec(memory_space=pltpu.SMEM),
                  full(lng.shape), full(lnb.shape), full(ws.shape), full(bs.shape), full(goa.shape), full(gog.shape)],
        out_specs=[pl.BlockSpec((dm.inw, BLOCK), cur), pl.BlockSpec((kvw2, BLOCK), prev_kv_out),
                   full((dm.hd, 1)), full((dm.hd, 1)), full((dm.nh, 1)), full((dm.gw, 1)), full((dm.gw, 1)), full(ws.shape),
                   full(bs.shape), full((dm.aw, 1)), full((dm.gw, 1))],
        out_shape=[jax.ShapeDtypeStruct((dm.inw, t), MXU), jax.ShapeDtypeStruct((kvw2, t), MXU),
                   col(dm.hd), col(dm.hd), col(dm.nh), col(dm.gw), col(dm.gw), jax.ShapeDtypeStruct(ws.shape, F32),
                   jax.ShapeDtypeStruct(bs.shape, F32), col(dm.aw), col(dm.gw)],
        scratch_shapes=[pltpu.VMEM((dm.aw, BLOCK), F32), pltpu.VMEM((dm.aw, BLOCK), F32),
                        pltpu.VMEM((kvw2, BLOCK), F32), pltpu.VMEM((kvw2, BLOCK), F32), pltpu.VMEM((kvw2, BLOCK), F32),
                        pltpu.VMEM((dm.kvw, BLOCK), F32),
                        pltpu.VMEM((dm.hd, dm.grp * BLOCK), F32), lane(dm.hd), lane(dm.nh), lane(dm.gw), lane(dm.gw), lane(dm.aw), lane(dm.gw)],
        compiler_params=_cparams(("arbitrary", "arbitrary"), 48),
    )(proj_t, proj_t, dy_t, gq, gk2, sinks, lng, lnb, ws, bs, goa, gog)


def _patch_kv(dproj_t, dkv_t, dm):
    t = dproj_t.shape[1]
    tc = _pick(t, (1024, 512, 256, 128))
    kvw2 = 2 * dm.kvw
    kvblk = dm.aw // kvw2

    def body(dproj_hbm, dkv_ref, out_ref):
        del dproj_hbm
        out_ref[...] = dkv_ref[...]

    return pl.pallas_call(
        body, name="patch_kv", grid=(t // tc,),
        in_specs=[ANY, pl.BlockSpec((kvw2, tc), lambda i: (0, i))],
        out_specs=pl.BlockSpec((kvw2, tc), lambda i: (kvblk, i)),
        out_shape=jax.ShapeDtypeStruct(dproj_t.shape, dproj_t.dtype),
        input_output_aliases={0: 0},
        compiler_params=_cparams(("parallel",), 32),
    )(dproj_t, dkv_t)


def _place():
    x, y, c = lax.axis_index("x"), lax.axis_index("y"), lax.axis_index("c")
    return x, y, c


def _handshake(peers):
    barrier = pltpu.get_barrier_semaphore()
    for p in peers:
        pl.semaphore_signal(barrier, inc=1, device_id=p, device_id_type=MESH)
    pl.semaphore_wait(barrier, len(peers))


def _sequencer_mesh():
    return plsc.ScalarSubcoreMesh(axis_name="sequencer", num_cores=1)


GATHER_CHUNKS = 4
BF16_ROWS = 16


def _row_chunks(n, k):
    tiles = n // BF16_ROWS
    sizes = [(tiles // k + (1 if i < tiles % k else 0)) * BF16_ROWS for i in range(k)]
    return [(sum(sizes[:i]), sz) for i, sz in enumerate(sizes) if sz]


def _allgather_weight(name, collective_id, shard, after=None):
    n = shard.shape[0]
    assert n % BF16_ROWS == 0
    chunks = _row_chunks(n, GATHER_CHUNKS)
    nc = len(chunks)

    def body(*refs):
        src, out = refs[0], refs[-4]
        send_sems, recv_sems, local_sem = refs[-3:]
        x, y, c = _place()
        me, sib, xn, yn, diag = (x, y, c), (x, y, 1 - c), (1 - x, y, c), (x, 1 - y, c), (1 - x, 1 - y, c)
        relay_to = (x ^ c, y ^ (1 - c), c)
        relay_of = (x ^ (1 - c), y ^ c, c)
        _handshake([sib, xn, yn])

        def rows(place, ci):
            px, py, pc = place
            off, size = chunks[ci]
            return out.at[pl.ds(pl.multiple_of((4 * px + 2 * py + pc) * n + off, BF16_ROWS), size), :]

        def copy(k, ci, block, to, from_src=False):
            off, size = chunks[ci]
            return pltpu.make_async_remote_copy(
                src_ref=src.at[pl.ds(off, size), :] if from_src else rows(block, ci), dst_ref=rows(block, ci),
                send_sem=send_sems.at[ci, k], recv_sem=recv_sems.at[ci, k], device_id=to, device_id_type=MESH)

        mine = pltpu.make_async_copy(src, out.at[pl.ds(pl.multiple_of((4 * x + 2 * y + c) * n, BF16_ROWS), n), :], local_sem)
        mine.start()
        sent = []
        for ci in range(nc):
            sent += [copy(0, ci, me, sib, from_src=True), copy(1, ci, me, xn, from_src=True), copy(2, ci, me, yn, from_src=True)]
        for cp in sent:
            cp.start()
        for ci in range(nc):
            copy(1, ci, xn, me).wait_recv()
            copy(2, ci, yn, me).wait_recv()
            passed = [copy(3, ci, relay_of, relay_to), copy(4, ci, xn, sib), copy(5, ci, yn, sib)]
            for cp in passed:
                cp.start()
            sent += passed
        for ci in range(nc):
            copy(3, ci, diag, me).wait_recv()
            passed = copy(6, ci, diag, sib)
            passed.start()
            sent.append(passed)
        for ci in range(nc):
            copy(0, ci, sib, me).wait_recv()
            for k, block in ((4, (1 - x, y, 1 - c)), (5, (x, 1 - y, 1 - c)), (6, (1 - x, 1 - y, 1 - c))):
                copy(k, ci, block, me).wait_recv()
        for cp in sent:
            cp.wait_send()
        mine.wait()

    return pl.kernel(
        body, name=name,
        out_type=jax.ShapeDtypeStruct((N_DEV * n, shard.shape[1]), shard.dtype),
        mesh=_sequencer_mesh(),
        scratch_types=[pltpu.SemaphoreType.DMA((nc, 7)), pltpu.SemaphoreType.DMA((nc, 7)), pltpu.SemaphoreType.DMA],
        compiler_params=pltpu.CompilerParams(collective_id=collective_id),
    )(*([shard] if after is None else [shard, after]))


_FLIPS = [(0, 0, 1), (1, 0, 0), (0, 1, 0), (1, 1, 0), (1, 0, 1), (0, 1, 1), (1, 1, 1)]


def _scatter_grad(name, collective_id, grad):
    n = grad.shape[0] // N_DEV

    def body(src, out, send_sems, recv_sems, local_sem):
        x, y, c = _place()
        me_idx = 4 * x + 2 * y + c
        peers = [(x ^ fx, y ^ fy, c ^ fc) for (fx, fy, fc) in _FLIPS]
        _handshake(peers)

        def block(idx):
            return src.at[pl.ds(pl.multiple_of(idx * n, 16), n), :]

        copies = [pltpu.make_async_remote_copy(
            src_ref=block(4 * px + 2 * py + pc), dst_ref=out.at[me_idx], send_sem=send_sems.at[k], recv_sem=recv_sems.at[k],
            device_id=(px, py, pc), device_id_type=MESH) for k, (px, py, pc) in enumerate(peers)]
        mine = pltpu.make_async_copy(block(me_idx), out.at[me_idx], local_sem)
        mine.start()
        for cp in copies:
            cp.start()
        for cp in copies:
            cp.wait_recv()
        for cp in copies:
            cp.wait_send()
        mine.wait()

    return pl.kernel(
        body, name=name,
        out_type=jax.ShapeDtypeStruct((N_DEV, n, grad.shape[1]), grad.dtype),
        mesh=_sequencer_mesh(),
        scratch_types=[pltpu.SemaphoreType.DMA((7,)), pltpu.SemaphoreType.DMA((7,)), pltpu.SemaphoreType.DMA],
        compiler_params=pltpu.CompilerParams(collective_id=collective_id),
    )(grad)


def _allgather_rows(name, collective_id, part):
    def body(src, out, send_sems, recv_sems, local_sem):
        x, y, c = _place()
        me_idx = 4 * x + 2 * y + c
        peers = [(x ^ fx, y ^ fy, c ^ fc) for (fx, fy, fc) in _FLIPS]
        _handshake(peers)
        copies = [pltpu.make_async_remote_copy(
            src_ref=src, dst_ref=out.at[me_idx], send_sem=send_sems.at[k], recv_sem=recv_sems.at[k],
            device_id=peer, device_id_type=MESH) for k, peer in enumerate(peers)]
        mine = pltpu.make_async_copy(src, out.at[me_idx], local_sem)
        mine.start()
        for cp in copies:
            cp.start()
        for cp in copies:
            cp.wait_recv()
        for cp in copies:
            cp.wait_send()
        mine.wait()

    return pl.kernel(
        body, name=name,
        out_type=jax.ShapeDtypeStruct((N_DEV,) + part.shape, part.dtype),
        mesh=_sequencer_mesh(),
        scratch_types=[pltpu.SemaphoreType.DMA((7,)), pltpu.SemaphoreType.DMA((7,)), pltpu.SemaphoreType.DMA],
        compiler_params=pltpu.CompilerParams(collective_id=collective_id),
    )(part)


def _adamw_math(w, g, m, v):
    m = ADAM_B1 * m + (1.0 - ADAM_B1) * g
    v = ADAM_B2 * v + (1.0 - ADAM_B2) * (g * g)
    m_hat = m / (1.0 - ADAM_B1 ** ADAM_STEP)
    v_hat = v / (1.0 - ADAM_B2 ** ADAM_STEP)
    delta = -ADAM_LR * (m_hat / (jnp.sqrt(v_hat) + ADAM_EPS) + ADAM_WD * w)
    return delta, m, v


def _sum_adamw(name, slots, w, m, v, after):
    _, n, kk = slots.shape
    tr = _pick(n, (208, 176, 128, 96, 64, 32, 16))

    def body(s_ref, w_ref, m_ref, v_ref, after_ref, g_ref, d_ref, nm_ref, nv_ref):
        del after_ref
        g = s_ref[0].astype(F32)
        for p in range(1, N_DEV):
            g = g + s_ref[p].astype(F32)
        g_ref[...] = g
        d_ref[...], nm_ref[...], nv_ref[...] = _adamw_math(w_ref[...], g, m_ref[...], v_ref[...])

    row = pl.BlockSpec((tr, kk), lambda i: (i, 0))
    return pl.pallas_call(
        body, name=name, grid=(n // tr,),
        in_specs=[pl.BlockSpec((N_DEV, tr, kk), lambda i: (0, i, 0)), row, row, row, ANY],
        out_specs=[row] * 4,
        out_shape=[jax.ShapeDtypeStruct((n, kk), F32)] * 4,
        compiler_params=_cparams(("parallel",), 48),
    )(slots, w, m, v, after)


def _allreduce_small_adamw(early_slots, late, w, m, v, after):
    ra, rb = early_slots.shape[1], late.shape[0]

    def body(early_ref, late_ref, w_ref, m_ref, v_ref, after_ref, g_ref, d_ref, nm_ref, nv_ref, slots, send_sems, recv_sems):
        del after_ref
        x, y, c = _place()
        me_idx = 4 * x + 2 * y + c
        copies = []
        for k, (fx, fy, fc) in enumerate(_FLIPS):
            px, py, pc = x ^ fx, y ^ fy, c ^ fc
            copies.append(pltpu.make_async_remote_copy(
                src_ref=late_ref, dst_ref=slots.at[me_idx], send_sem=send_sems.at[k], recv_sem=recv_sems.at[k],
                device_id=(px, py, pc), device_id_type=MESH))
        for cp in copies:
            cp.start()
        slots[me_idx] = late_ref[...]
        g = early_ref[0]
        for p in range(1, N_DEV):
            g = g + early_ref[p]
        early = pl.ds(0, ra)
        g_ref[early, :] = g
        d_ref[early, :], nm_ref[early, :], nv_ref[early, :] = _adamw_math(w_ref[early, :], g, m_ref[early, :], v_ref[early, :])
        for cp in copies:
            cp.wait_recv()
        for cp in copies:
            cp.wait_send()
        g = slots[0]
        for p in range(1, N_DEV):
            g = g + slots[p]
        tail = pl.ds(ra, rb)
        g_ref[tail, :] = g
        d_ref[tail, :], nm_ref[tail, :], nv_ref[tail, :] = _adamw_math(w_ref[tail, :], g, m_ref[tail, :], v_ref[tail, :])

    vm = pl.BlockSpec(memory_space=pltpu.VMEM)
    return pl.pallas_call(
        body, name="allreduce_small_adamw",
        in_specs=[vm] * 5 + [ANY], out_specs=[vm] * 4,
        out_shape=[jax.ShapeDtypeStruct((ra + rb, LANES), F32)] * 4,
        scratch_shapes=[pltpu.VMEM((N_DEV, rb, LANES), F32), pltpu.SemaphoreType.DMA((7,)), pltpu.SemaphoreType.DMA((7,))],
        compiler_params=pltpu.CompilerParams(vmem_limit_bytes=48 * MIB),
    )(early_slots, late, w, m, v, after)


def _pack(arrs):
    parts, meta, off = [], [], 0
    for a in arrs:
        flat = a.reshape(-1).astype(F32)
        rows = -(-flat.shape[0] // LANES)
        rows8 = -(-rows // 8) * 8
        flat = jnp.pad(flat, (0, rows8 * LANES - flat.shape[0]))
        parts.append(flat.reshape(rows8, LANES))
        meta.append((off, a.shape, a.size))
        off += rows8
    return jnp.concatenate(parts, axis=0), meta


def _unpack(packed, meta):
    outs = []
    for off, shape, size in meta:
        rows = -(-size // LANES)
        outs.append(packed[off:off + rows].reshape(-1)[:size].reshape(shape))
    return outs


def _silu_parts(a):
    sg = 0.5 + 0.5 * jnp.tanh(0.5 * a)
    return a * sg, sg * (1.0 + a * (1.0 - sg))


def kernel(x, norm1_g, w_in, q_norm_g, k_norm_g, attn_sinks, gate_ln_g, gate_ln_b, w_spatial, b_spatial, out_norm_attn_g, out_norm_gate_g, w_out, norm2_g, w_ffn_gate, w_ffn_up, w_ffn_down, loss_target, m_norm1_g, m_w_in, m_q_norm_g, m_k_norm_g, m_attn_sinks, m_gate_ln_g, m_gate_ln_b, m_w_spatial, m_b_spatial, m_out_norm_attn_g, m_out_norm_gate_g, m_w_out, m_norm2_g, m_w_ffn_gate, m_w_ffn_up, m_w_ffn_down, v_norm1_g, v_w_in, v_q_norm_g, v_k_norm_g, v_attn_sinks, v_gate_ln_g, v_gate_ln_b, v_w_spatial, v_b_spatial, v_out_norm_attn_g, v_out_norm_gate_g, v_w_out, v_norm2_g, v_w_ffn_gate, v_w_ffn_up, v_w_ffn_down):
    nseq, seq, d = x.shape
    t = nseq * seq
    nb = seq // BLOCK
    inw = w_in.shape[2] * N_DEV
    dm = _Dims(d, inw, q_norm_g.shape[-1])
    xf = x.reshape(t, d)
    tgt = loss_target.reshape(t, d)

    rows = lambda wv, transposed: jnp.swapaxes(wv, 1, 2)[0] if transposed else wv[0]
    big = {"w_in": (w_in, m_w_in, v_w_in, True), "w_out": (w_out, m_w_out, v_w_out, False),
           "w_ffn_gate": (w_ffn_gate, m_w_ffn_gate, v_w_ffn_gate, True), "w_ffn_up": (w_ffn_up, m_w_ffn_up, v_w_ffn_up, True),
           "w_ffn_down": (w_ffn_down, m_w_ffn_down, v_w_ffn_down, False)}
    big_rows = {nm: tuple(rows(arr, tr) for arr in (wv, mv, vv)) for nm, (wv, mv, vv, tr) in big.items()}
    shard = lambda nm: big_rows[nm][0].astype(WIRE)
    win_t = _allgather_weight("gather_w_in", 1, shard("w_in"))
    wout = _allgather_weight("gather_w_out", 2, shard("w_out"), after=win_t)
    wg_t = _allgather_weight("gather_w_ffn_gate", 3, shard("w_ffn_gate"), after=win_t)
    wu_t = _allgather_weight("gather_w_ffn_up", 9, shard("w_ffn_up"), after=win_t)
    wd = _allgather_weight("gather_w_ffn_down", 10, shard("w_ffn_down"), after=win_t)

    lanes = lambda v, n=BLOCK: jnp.broadcast_to(v.reshape(-1, 1), (v.size, n))
    prm = (lanes(q_norm_g, dm.grp * BLOCK), lanes(k_norm_g, 2 * BLOCK), attn_sinks[0], lanes(gate_ln_g), lanes(gate_ln_b), w_spatial[0], b_spatial[0],
           lanes(out_norm_attn_g), lanes(out_norm_gate_g))

    h1 = _rms_fwd("rms1_fwd", xf, norm1_g)
    (proj_t,) = _matmul("mm_in", win_t, h1, "nt", [F32])
    y_t = _mixer_fwd(proj_t, prm, dm, nseq, nb)

    def residual_norm(acc, xr, g2):
        x2v = xr + acc
        return x2v, x2v * lax.rsqrt(jnp.mean(x2v * x2v, axis=-1, keepdims=True) + EPS) * g2

    x2, h2 = _matmul("mm_out", y_t, wout, "tn", [F32, MXU], epilogue=residual_norm, extras=[xf], rowvecs=[norm2_g], full_rows=True)
    a, b, s = _matmul("mm_gate_up", h2, wg_t, "nt", [MXU, MXU, MXU], b2=wu_t, epilogue=lambda ga, ub: (ga, ub, _silu_parts(ga)[0] * ub),
                        overlap=True)

    def loss_epilogue(acc, x2v, tv):
        diff = (x2v + acc) - tv
        dx3 = diff * (1.0 / d)
        return dx3, dx3, jnp.sum(diff * diff)

    dx3, dx3b, lossp = _matmul("mm_down", s, wd, "nn", [F32, MXU], epilogue=loss_epilogue, extras=[x2, tgt], partial=True, overlap=True)
    loss_part = (0.5 / d) * jnp.sum(lossp[::8, ::LANES])

    def dswiglu(acc, av, bv):
        silu, dsilu = _silu_parts(av.astype(F32))
        return acc * bv.astype(F32) * dsilu, acc * silu

    da, db = _matmul("mm_d_down", dx3b, wd, "nt", [MXU, MXU], epilogue=dswiglu, extras=[a, b], overlap=True)
    (g_wd,) = _matmul("mm_gw_down", s, dx3b, "tn", [WIRE])
    sl_wd = _scatter_grad("scatter_w_ffn_down", 4, g_wd)
    (dh2a,) = _matmul("mm_dh2_gate", da, wg_t, "nn", [F32], after=[g_wd])
    (g_wg,) = _matmul("mm_gw_gate", da, h2, "tn", [WIRE], after=[dh2a])
    sl_wg = _scatter_grad("scatter_w_ffn_gate", 5, g_wg)
    (dh2,) = _matmul("mm_dh2_up", db, wu_t, "nn", [F32], epilogue=lambda acc, pv: (pv + acc,), extras=[dh2a], after=[g_wg])
    (g_wu,) = _matmul("mm_gw_up", db, h2, "tn", [WIRE], after=[dh2])
    sl_wu = _scatter_grad("scatter_w_ffn_up", 6, g_wu)

    dy_t, dx2, dx2b, dg2 = _norm_bwd_matmul("mm_d_out", wout, dh2, x2, norm2_g, dx3, after=g_wu)
    (g_wout,) = _matmul("mm_gw_out", y_t, dx2b, "nn", [WIRE], after=[dy_t])
    sl_wout = _scatter_grad("scatter_w_out", 7, g_wout)
    (dproj0, dkv, dgq, dgk, dsink, dlng, dlnb, dws, dbs, dgoa, dgog) = _mixer_bwd(proj_t, dy_t, prm, dm, nseq, nb)
    early_g = [dgq, dgk, dsink, dlng, dlnb, dws, dbs, dgoa, dgog, dg2, loss_part.reshape(1)]
    early_slots = _allgather_rows("gather_small_grads", 11, _pack(early_g)[0])
    dproj_t = _patch_kv(dproj0, dkv, dm)
    (g_win,) = _matmul("mm_gw_in", dproj_t, h1, "nn", [WIRE])
    sl_win = _scatter_grad("scatter_w_in", 8, g_win)

    def norm1_backward(dh1, xv, dx2v, g1):
        r = lax.rsqrt(jnp.mean(xv * xv, axis=-1, keepdims=True) + EPS)
        xh = xv * r
        dxh = dh1 * g1
        return dx2v + r * (dxh - xh * jnp.mean(dxh * xh, axis=-1, keepdims=True)), jnp.sum(dh1 * xh, axis=0, keepdims=True)

    dx, dg1 = _matmul("mm_d_in", dproj_t, win_t, "tn", [F32], epilogue=norm1_backward, extras=[xf, dx2], rowvecs=[norm1_g],
                      after=[g_win], col_sum=True, full_rows=True)

    big_out = {}
    last = dx

    def big_update(nm, sl, after):
        res = _sum_adamw("adamw_" + nm, sl, *big_rows[nm], after=after)
        big_out[nm] = tuple(jnp.swapaxes(r[None], 1, 2) if big[nm][3] else r[None] for r in res)
        return res[1]

    for nm, sl in (("w_ffn_down", sl_wd), ("w_ffn_gate", sl_wg), ("w_ffn_up", sl_wu), ("w_out", sl_wout)):
        last = big_update(nm, sl, last)

    zero = jnp.zeros((1,), F32)
    small_names = ["q_norm_g", "k_norm_g", "attn_sinks", "gate_ln_g", "gate_ln_b", "w_spatial", "b_spatial",
                   "out_norm_attn_g", "out_norm_gate_g", "norm2_g", "loss", "norm1_g"]
    small_w = [q_norm_g, k_norm_g, attn_sinks, gate_ln_g, gate_ln_b, w_spatial, b_spatial, out_norm_attn_g, out_norm_gate_g, norm2_g, zero, norm1_g]
    small_m = [m_q_norm_g, m_k_norm_g, m_attn_sinks, m_gate_ln_g, m_gate_ln_b, m_w_spatial, m_b_spatial, m_out_norm_attn_g, m_out_norm_gate_g, m_norm2_g, zero, m_norm1_g]
    small_v = [v_q_norm_g, v_k_norm_g, v_attn_sinks, v_gate_ln_g, v_gate_ln_b, v_w_spatial, v_b_spatial, v_out_norm_attn_g, v_out_norm_gate_g, v_norm2_g, zero, v_norm1_g]
    pw, meta = _pack(small_w)
    sg, sd, sm, sv = _allreduce_small_adamw(early_slots, _pack([dg1])[0], pw, _pack(small_m)[0], _pack(small_v)[0], after=last)
    big_update("w_in", sl_win, sd)
    ug, ud, um, uv = _unpack(sg, meta), _unpack(sd, meta), _unpack(sm, meta), _unpack(sv, meta)
    small_out = {nm: (ug[k], ud[k], um[k], uv[k]) for k, nm in enumerate(small_names)}
    loss = small_out["loss"][0].reshape(())

    order = ["norm1_g", "w_in", "q_norm_g", "k_norm_g", "attn_sinks", "gate_ln_g", "gate_ln_b", "w_spatial", "b_spatial",
             "out_norm_attn_g", "out_norm_gate_g", "w_out", "norm2_g", "w_ffn_gate", "w_ffn_up", "w_ffn_down"]
    allo = {**big_out, **small_out}
    outs = [loss, dx.reshape(nseq, seq, d)]
    for k in range(4):
        outs += [allo[nm][k] for nm in order]
    return tuple(outs)
```

```python
import math

import jax
import jax.numpy as jnp
from jax import lax
from jax.experimental import pallas as pl
from jax.experimental.pallas import tpu as pltpu
from jax.experimental.pallas import tpu_sc as plsc

F32 = jnp.float32
MXU = jnp.bfloat16
WIRE = jnp.bfloat16
EPS = 1e-6
BLOCK = 128
GROUP_DIM = 128
N_KV_HEADS = 2
NEG = -1e30
N_DEV = 8
LANES = 128
MIB = 1024 * 1024

ADAM_LR = 0.001
ADAM_B1 = 0.9
ADAM_B2 = 0.999
ADAM_EPS = 1e-08
ADAM_WD = 0.01
ADAM_STEP = 10

MESH = pl.DeviceIdType.MESH
ANY = pl.BlockSpec(memory_space=pl.ANY)


def _pick(n, cands):
    for c in cands:
        if n % c == 0:
            return c
    return n


def _cparams(sem, vmem_mb):
    return pltpu.CompilerParams(dimension_semantics=sem, vmem_limit_bytes=vmem_mb * MIB)


VMEM_TILE_BUDGET = 44 * MIB
HBM_BYTES_PER_US = 3.0e6
STEP_US = 0.4
MIN_TILE_N = 512


def _tile_candidates(n):
    return [c for c in range(min(n, 2048), 0, -LANES) if n % c == 0 and c % LANES == 0] or [n]


def _matmul_tiles(m, n, kk, esz, n_b, extra_sizes, out_sizes, full_rows):
    best = None
    wide = [n] if full_rows else [c for c in _tile_candidates(n) if c >= MIN_TILE_N] or _tile_candidates(n)
    for tm in _tile_candidates(m):
        for tn in wide:
            vmem = 2 * (tm + n_b * tn) * kk * esz + tm * tn * (4 * n_b + 2 * sum(extra_sizes) + 2 * sum(out_sizes))
            if vmem > VMEM_TILE_BUDGET:
                continue
            cost = (m // tm) * n_b * n * kk * esz / HBM_BYTES_PER_US + (m // tm) * (n // tn) * STEP_US
            if best is None or cost < best[0]:
                best = (cost, tm, tn, vmem)
    assert best is not None, (m, n, kk)
    return best[1:]


def _matmul(name, a, b, mode, out_dtypes, epilogue=None, extras=(), rowvecs=(), after=(), partial=False, col_sum=False, full_rows=False, b2=None):
    if mode == "nn":
        (m, kk), n = a.shape, b.shape[1]
        dn = (((1,), (0,)), ((), ()))
    elif mode == "nt":
        (m, kk), n = a.shape, b.shape[0]
        dn = (((1,), (1,)), ((), ()))
    else:
        (kk, m), n = a.shape, b.shape[1]
        dn = (((0,), (0,)), ((), ()))
    bs = [b] if b2 is None else [b, b2]
    tm, tn, vmem = _matmul_tiles(m, n, kk, a.dtype.itemsize, len(bs), [e.dtype.itemsize for e in extras],
                                 [jnp.dtype(dt).itemsize for dt in out_dtypes], full_rows)
    a_spec = pl.BlockSpec((kk, tm), lambda i, j: (0, i)) if mode == "tn" else pl.BlockSpec((tm, kk), lambda i, j: (i, 0))
    b_spec = pl.BlockSpec((tn, kk), lambda i, j: (j, 0)) if mode == "nt" else pl.BlockSpec((kk, tn), lambda i, j: (0, j))
    tile = pl.BlockSpec((tm, tn), lambda i, j: (i, j))
    row = pl.BlockSpec((1, tn), lambda i, j: (0, j))
    nb, ne, nr, na, no = len(bs), len(extras), len(rowvecs), len(after), len(out_dtypes)

    def body(a_ref, *rest):
        b_refs, in_refs, out_refs = rest[:nb], rest[nb:nb + ne + nr], rest[nb + ne + nr + na:]
        av = a_ref[...]
        accs = [lax.dot_general(av, b_ref[...], dn, preferred_element_type=F32) for b_ref in b_refs]
        vals = tuple(accs) if epilogue is None else epilogue(*accs, *[r[...] for r in in_refs])
        for o_ref, t in zip(out_refs[:no], vals[:no]):
            o_ref[...] = t.astype(o_ref.dtype)
        if partial:
            out_refs[no][...] = jnp.full((8, LANES), vals[no], F32)
        if col_sum:
            sum_ref = out_refs[-1]

            @pl.when(pl.program_id(0) == 0)
            def _():
                sum_ref[...] = jnp.zeros_like(sum_ref)

            sum_ref[...] += vals[-1]

    out_specs = [tile] * no
    out_shape = [jax.ShapeDtypeStruct((m, n), dt) for dt in out_dtypes]
    if partial:
        out_specs.append(pl.BlockSpec((8, LANES), lambda i, j: (i, j)))
        out_shape.append(jax.ShapeDtypeStruct((m // tm * 8, n // tn * LANES), F32))
    if col_sum:
        out_specs.append(row)
        out_shape.append(jax.ShapeDtypeStruct((1, n), F32))
    return pl.pallas_call(
        body, name=name, grid=(m // tm, n // tn),
        in_specs=[a_spec] + [b_spec] * nb + [tile] * ne + [row] * nr + [ANY] * na,
        out_specs=out_specs, out_shape=out_shape,
        compiler_params=_cparams(("arbitrary" if col_sum else "parallel", "arbitrary"), min(vmem // MIB + 8, 60)),
    )(a, *bs, *extras, *rowvecs, *after)


def _rms_fwd(name, x, g):
    t, d = x.shape
    tm = _pick(t, (512, 256, 128))

    def body(x_ref, g_ref, h_ref):
        xv = x_ref[...]
        r = lax.rsqrt(jnp.mean(xv * xv, axis=-1, keepdims=True) + EPS)
        h_ref[...] = (xv * r * g_ref[...]).astype(h_ref.dtype)

    return pl.pallas_call(
        body, name=name, grid=(t // tm,),
        in_specs=[pl.BlockSpec((tm, d), lambda i: (i, 0)), pl.BlockSpec((1, d), lambda i: (0, 0))],
        out_specs=pl.BlockSpec((tm, d), lambda i: (i, 0)),
        out_shape=jax.ShapeDtypeStruct((t, d), MXU),
        compiler_params=_cparams(("parallel",), 32),
    )(x, g)


def _norm_bwd_matmul(name, w, dh, x, g, res, after):
    t, d = x.shape
    m = w.shape[0]
    tn = _pick(t, (256, 128))

    def body(w_ref, dh_ref, x_ref, g_ref, res_ref, after_ref, out_ref, dx_ref, dxb_ref, dg_ref):
        del after_ref

        @pl.when(pl.program_id(0) == 0)
        def _():
            dg_ref[...] = jnp.zeros_like(dg_ref)

        xv, dhv = x_ref[...], dh_ref[...]
        r = lax.rsqrt(jnp.mean(xv * xv, axis=-1, keepdims=True) + EPS)
        xh = xv * r
        dg_ref[...] += jnp.sum(dhv * xh, axis=0, keepdims=True)
        dxh = dhv * g_ref[...]
        dx = res_ref[...] + r * (dxh - xh * jnp.mean(dxh * xh, axis=-1, keepdims=True))
        dx_ref[...] = dx
        dxb = dx.astype(MXU)
        dxb_ref[...] = dxb
        out_ref[...] = lax.dot_general(w_ref[...], dxb, (((1,), (1,)), ((), ())), preferred_element_type=F32)

    row = pl.BlockSpec((tn, d), lambda j: (j, 0))
    vec = pl.BlockSpec((1, d), lambda j: (0, 0))
    return pl.pallas_call(
        body, name=name, grid=(t // tn,),
        in_specs=[pl.BlockSpec((m, d), lambda j: (0, 0)), row, row, vec, row, ANY],
        out_specs=[pl.BlockSpec((m, tn), lambda j: (0, j)), row, row, vec],
        out_shape=[jax.ShapeDtypeStruct((m, t), F32), jax.ShapeDtypeStruct((t, d), F32), jax.ShapeDtypeStruct((t, d), MXU),
                   jax.ShapeDtypeStruct((1, d), F32)],
        compiler_params=_cparams(("arbitrary",), 52),
    )(w, dh, x, g, res, after)


_INV_SQRT2 = 0.7071067811865476
_INV_SQRT_2PI = 0.3989422804014327


def _dot_nt(a, b):
    return lax.dot_general(a, b, (((1,), (1,)), ((), ())), preferred_element_type=F32)


def _dot_tn(a, b):
    return lax.dot_general(a, b, (((0,), (0,)), ((), ())), preferred_element_type=F32)


def _dot(a, b):
    return jnp.dot(a, b, preferred_element_type=F32)


def _col_rms(v):
    return lax.rsqrt(jnp.mean(v * v, axis=0, keepdims=True) + EPS)


class _Dims:
    def __init__(self, d_model, in_width, head_dim):
        self.d = d_model
        self.aw = d_model // 2
        self.gw = d_model - self.aw
        self.kvw = (in_width - self.aw - 2 * self.gw) // 2
        self.hd = head_dim
        self.nh = self.aw // head_dim
        self.nkv = self.kvw // head_dim
        self.grp = self.nh // self.nkv
        self.ng = self.gw // GROUP_DIM
        self.inw = in_width
        self.zoff = self.aw + 2 * self.kvw
        assert self.nkv == N_KV_HEADS and self.zoff + 2 * self.gw == in_width and self.aw % (2 * self.kvw) == 0


def _band_masks(first):
    r = lax.broadcasted_iota(jnp.int32, (BLOCK, BLOCK), 0)
    t = lax.broadcasted_iota(jnp.int32, (BLOCK, BLOCK), 1)
    upper = r > t
    dist = t - r + jnp.where(upper, BLOCK, 0)
    return upper, jnp.logical_not(upper & first), dist.astype(F32)


def _fold(full, upper):
    return jnp.where(upper, full[:BLOCK], full[BLOCK:])


def _unfold(folded, upper):
    zero = jnp.zeros_like(folded)
    return jnp.concatenate([jnp.where(upper, folded, zero), jnp.where(upper, zero, folded)], axis=0)


def _kv_band(dm, kh, p_ref, pkv_ref, gk2):
    ko = dm.aw + kh * dm.hd
    vo = dm.aw + dm.kvw + kh * dm.hd
    k_t = jnp.concatenate([pkv_ref[kh * dm.hd:(kh + 1) * dm.hd, :], p_ref[ko:ko + dm.hd, :]], axis=1)
    v_t = jnp.concatenate([pkv_ref[dm.kvw + kh * dm.hd:dm.kvw + (kh + 1) * dm.hd, :], p_ref[vo:vo + dm.hd, :]], axis=1)
    kn_t = k_t * _col_rms(k_t) * gk2
    return kn_t.astype(MXU), kn_t.T.astype(MXU), v_t.astype(MXU), v_t.T.astype(MXU)


def _group_heads(dm, kh):
    return range(kh * dm.grp, (kh + 1) * dm.grp)


def _attn_group_fwd(dm, kh, p_ref, gq, kn, v_tb, sink_ref, masks):
    heads = _group_heads(dm, kh)
    q = jnp.concatenate([p_ref[h * dm.hd:(h + 1) * dm.hd, :] for h in heads], axis=1)
    rq = _col_rms(q)
    qh = q * rq
    qnb = (qh * gq).astype(MXU)
    upper, valid, dist = masks
    s = _dot(kn, qnb)
    probs, probs_b, sink_probs = [], [], []
    for g, h in enumerate(heads):
        slope, sink = math.pow(2.0, -8.0 * (h + 1) / dm.nh), sink_ref[h]
        logits = jnp.where(valid, _fold(s[:, g * BLOCK:(g + 1) * BLOCK], upper) * (dm.hd ** -0.5) - slope * dist, NEG)
        m = jnp.maximum(jnp.max(logits, axis=0, keepdims=True), sink)
        e = jnp.exp(logits - m)
        es = jnp.exp(sink - m)
        inv = 1.0 / (jnp.sum(e, axis=0, keepdims=True) + es)
        probs.append(e * inv)
        probs_b.append(_unfold(probs[g], upper).astype(MXU))
        sink_probs.append(es * inv)
    probs_b = jnp.concatenate(probs_b, axis=1)
    o = _dot(v_tb, probs_b)
    return o, probs, probs_b, sink_probs, rq, qh, qnb


def _gelu_cdf(z):
    return 0.5 * (1.0 + lax.erf(z * _INV_SQRT2))


def _by_group(v, ng):
    return v.reshape(ng, GROUP_DIM, v.shape[1])


def _gate_fwd(dm, p_ref, lng_ref, lnb_ref, ws_ref, bs_ref, tril):
    zu, zv = p_ref[dm.zoff:dm.zoff + dm.gw, :], p_ref[dm.zoff + dm.gw:dm.zoff + 2 * dm.gw, :]
    cu, cv = _gelu_cdf(zu), _gelu_cdf(zv)
    u, v = zu * cu, zv * cv
    v3 = _by_group(v, dm.ng)
    xc = v3 - jnp.mean(v3, axis=1, keepdims=True)
    rstd = lax.rsqrt(jnp.mean(xc * xc, axis=1, keepdims=True) + EPS)
    xh = (xc * rstd).reshape(dm.gw, BLOCK)
    vnb = (xh * lng_ref[...] + lnb_ref[...]).astype(MXU)
    wts = [jnp.where(tril, ws_ref[g], 0.0).astype(MXU) for g in range(dm.ng)]
    mixed = jnp.concatenate([_dot_nt(vnb[g * GROUP_DIM:(g + 1) * GROUP_DIM], wts[g]) + bs_ref[g:g + 1, :]
                             for g in range(dm.ng)], axis=0)
    return u * mixed, u, mixed, xh, rstd, vnb, wts, (zu, cu), (zv, cv)


def _mixer_specs(dm, nb, clamp):
    kvblk = dm.aw // (2 * dm.kvw)

    def cur(s, i):
        return (0, s * nb + clamp(i))

    def prev(s, i):
        return (kvblk, s * nb + jnp.maximum(clamp(i) - 1, 0))

    full = lambda shape: pl.BlockSpec(shape, lambda s, i: tuple(0 for _ in shape))
    return cur, prev, full


def _tril():
    return lax.broadcasted_iota(jnp.int32, (BLOCK, BLOCK), 0) >= lax.broadcasted_iota(jnp.int32, (BLOCK, BLOCK), 1)


def _mixer_fwd(proj_t, prm, dm, nseq, nb):
    gq, gk2, sinks, lng, lnb, ws, bs, goa, gog = prm
    t = proj_t.shape[1]
    cur, prev, full = _mixer_specs(dm, nb, lambda i: i)

    def body(p_ref, pkv_ref, gq_ref, gk_ref, sink_ref, lng_ref, lnb_ref, ws_ref, bs_ref, goa_ref, gog_ref, y_ref, att_scr):
        i = pl.program_id(1)
        masks = _band_masks(i == 0)
        gqv, gkv = gq_ref[...], gk_ref[...]
        for kh in range(dm.nkv):
            _, kn, v_tb, _ = _kv_band(dm, kh, p_ref, pkv_ref, gkv)
            o = _attn_group_fwd(dm, kh, p_ref, gqv, kn, v_tb, sink_ref, masks)[0]
            for g, h in enumerate(_group_heads(dm, kh)):
                att_scr[h * dm.hd:(h + 1) * dm.hd, :] = o[:, g * BLOCK:(g + 1) * BLOCK]
        att = att_scr[...]
        y_ref[:dm.aw, :] = (att * _col_rms(att) * goa_ref[...]).astype(y_ref.dtype)
        gt = _gate_fwd(dm, p_ref, lng_ref, lnb_ref, ws_ref, bs_ref, _tril())[0]
        y_ref[dm.aw:, :] = (gt * _col_rms(gt) * gog_ref[...]).astype(y_ref.dtype)

    return pl.pallas_call(
        body, name="mixer_fwd", grid=(nseq, nb),
        in_specs=[pl.BlockSpec((dm.inw, BLOCK), cur), pl.BlockSpec((2 * dm.kvw, BLOCK), prev),
                  full(gq.shape), full(gk2.shape), pl.BlockSpec(memory_space=pltpu.SMEM),
                  full(lng.shape), full(lnb.shape), full(ws.shape), full(bs.shape), full(goa.shape), full(gog.shape)],
        out_specs=pl.BlockSpec((dm.d, BLOCK), cur),
        out_shape=jax.ShapeDtypeStruct((dm.d, t), MXU),
        scratch_shapes=[pltpu.VMEM((dm.aw, BLOCK), F32)],
        compiler_params=_cparams(("parallel", "arbitrary"), 40),
    )(proj_t, proj_t, gq, gk2, sinks, lng, lnb, ws, bs, goa, gog)


def _mixer_bwd(proj_t, dy_t, prm, dm, nseq, nb):
    gq, gk2, sinks, lng, lnb, ws, bs, goa, gog = prm
    t = proj_t.shape[1]
    clamp = lambda i: jnp.minimum(i, nb - 1)
    cur, prev, full = _mixer_specs(dm, nb, clamp)
    kvw2 = 2 * dm.kvw

    def prev_kv_out(s, i):
        return (0, s * nb + jnp.maximum(i - 1, 0))

    def body(p_ref, pkv_ref, dy_ref, gq_ref, gk_ref, sink_ref, lng_ref, lnb_ref, ws_ref, bs_ref, goa_ref, gog_ref,
             dproj_ref, dkv_ref, dgq_ref, dgk_ref, dsink_ref, dlng_ref, dlnb_ref, dws_ref, dbs_ref, dgoa_ref, dgog_ref,
             att_scr, datt_scr, carry_scr, prevpart_scr, curpart_scr, kprev_scr,
             a_gq, a_gk, a_sink, a_lng, a_lnb, a_goa, a_gog):
        s_id, i = pl.program_id(0), pl.program_id(1)
        lane_accs = ((a_gq, dgq_ref), (a_gk, dgk_ref), (a_sink, dsink_ref), (a_lng, dlng_ref), (a_lnb, dlnb_ref),
                     (a_goa, dgoa_ref), (a_gog, dgog_ref))

        @pl.when((s_id == 0) & (i == 0))
        def _():
            for acc, _ in lane_accs:
                acc[...] = jnp.zeros_like(acc)
            dws_ref[...] = jnp.zeros_like(dws_ref)
            dbs_ref[...] = jnp.zeros_like(dbs_ref)

        gqv, gkv = gq_ref[...], gk_ref[...]

        @pl.when(i < nb)
        def _():
            masks = _band_masks(i == 0)
            upper = masks[0]
            kvs, fwd = [], []
            for kh in range(dm.nkv):
                kv = _kv_band(dm, kh, p_ref, pkv_ref, gkv)
                kvs.append(kv)
                fwd.append(_attn_group_fwd(dm, kh, p_ref, gqv, kv[1], kv[2], sink_ref, masks))
                for g, h in enumerate(_group_heads(dm, kh)):
                    att_scr[h * dm.hd:(h + 1) * dm.hd, :] = fwd[kh][0][:, g * BLOCK:(g + 1) * BLOCK]
            att = att_scr[...]
            dya = dy_ref[:dm.aw, :]
            ra = _col_rms(att)
            ah = att * ra
            a_goa[...] += dya * ah
            dah = dya * goa_ref[...]
            datt_scr[...] = ra * (dah - ah * jnp.mean(dah * ah, axis=0, keepdims=True))
            for kh in range(dm.nkv):
                kn_tb, kn, v_tb, vb = kvs[kh]
                _, probs, probs_b, sink_probs, rq, qh, qnb = fwd[kh]
                heads = _group_heads(dm, kh)
                do_b = jnp.concatenate([datt_scr[h * dm.hd:(h + 1) * dm.hd, :] for h in heads], axis=1).astype(MXU)
                dp = _dot(vb, do_b)
                ds = []
                for g, h in enumerate(heads):
                    p, dp_h = probs[g], _fold(dp[:, g * BLOCK:(g + 1) * BLOCK], upper)
                    delta = jnp.sum(p * dp_h, axis=0, keepdims=True)
                    ds.append(_unfold(p * (dp_h - delta) * (dm.hd ** -0.5), upper).astype(MXU))
                    a_sink[h:h + 1, :] += -(sink_probs[g] * delta)
                dsb = jnp.concatenate(ds, axis=1)
                dqn = _dot(kn_tb, dsb)
                dkn = _dot_nt(qnb, dsb)
                dvb = _dot_nt(do_b, probs_b)
                a_gq[...] += dqn * qh
                dqh = dqn * gqv
                dq = rq * (dqh - qh * jnp.mean(dqh * qh, axis=0, keepdims=True))
                for g, h in enumerate(heads):
                    dproj_ref[h * dm.hd:(h + 1) * dm.hd, :] = dq[:, g * BLOCK:(g + 1) * BLOCK].astype(dproj_ref.dtype)
                krows = slice(kh * dm.hd, (kh + 1) * dm.hd)
                vrows = slice(dm.kvw + kh * dm.hd, dm.kvw + (kh + 1) * dm.hd)
                prevpart_scr[krows, :] = dkn[:, :BLOCK]
                prevpart_scr[vrows, :] = dvb[:, :BLOCK]
                curpart_scr[krows, :] = dkn[:, BLOCK:]
                curpart_scr[vrows, :] = dvb[:, BLOCK:]
            dproj_ref[dm.aw:dm.zoff, :] = jnp.zeros((kvw2, BLOCK), dproj_ref.dtype)
            tril = _tril()
            gt, u, mixed, xh, rstd, vnb, wts, (zu, cu), (zv, cv) = _gate_fwd(dm, p_ref, lng_ref, lnb_ref, ws_ref, bs_ref, tril)
            dyg = dy_ref[dm.aw:, :]
            rg = _col_rms(gt)
            gh = gt * rg
            a_gog[...] += dyg * gh
            dgh = dyg * gog_ref[...]
            dgt = rg * (dgh - gh * jnp.mean(dgh * gh, axis=0, keepdims=True))
            du = dgt * mixed
            dmix = dgt * u
            dmixb = dmix.astype(MXU)
            dbs_ref[...] += jnp.sum(_by_group(dmix, dm.ng), axis=1)
            dvn = []
            for g in range(dm.ng):
                rows = slice(g * GROUP_DIM, (g + 1) * GROUP_DIM)
                dws_ref[g] += jnp.where(tril, _dot_tn(dmixb[rows], vnb[rows]), 0.0)
                dvn.append(_dot(dmixb[rows], wts[g]))
            dvn = jnp.concatenate(dvn, axis=0)
            a_lng[...] += dvn * xh
            a_lnb[...] += dvn
            dxh3, xh3 = _by_group(dvn * lng_ref[...], dm.ng), _by_group(xh, dm.ng)
            dv = (rstd * (dxh3 - jnp.mean(dxh3, axis=1, keepdims=True) - xh3 * jnp.mean(dxh3 * xh3, axis=1, keepdims=True))).reshape(dm.gw, BLOCK)
            dgu = cu + zu * (jnp.exp(-0.5 * zu * zu) * _INV_SQRT_2PI)
            dgv = cv + zv * (jnp.exp(-0.5 * zv * zv) * _INV_SQRT_2PI)
            dproj_ref[dm.zoff:dm.zoff + dm.gw, :] = (du * dgu).astype(dproj_ref.dtype)
            dproj_ref[dm.zoff + dm.gw:, :] = (dv * dgv).astype(dproj_ref.dtype)

        @pl.when(i == nb)
        def _():
            prevpart_scr[...] = jnp.zeros_like(prevpart_scr)

        @pl.when(i >= 1)
        def _():
            tot = carry_scr[...] + prevpart_scr[...]
            for kh in range(dm.nkv):
                krows = slice(kh * dm.hd, (kh + 1) * dm.hd)
                kraw = kprev_scr[krows, :]
                rk = _col_rms(kraw)
                khat = kraw * rk
                dkn = tot[krows, :]
                a_gk[...] += dkn * khat
                dkh = dkn * gkv[:, :BLOCK]
                dk = rk * (dkh - khat * jnp.mean(dkh * khat, axis=0, keepdims=True))
                dkv_ref[krows, :] = dk.astype(dkv_ref.dtype)
            dkv_ref[dm.kvw:, :] = tot[dm.kvw:, :].astype(dkv_ref.dtype)

        @pl.when(i < nb)
        def _():
            carry_scr[...] = curpart_scr[...]
            kprev_scr[...] = p_ref[dm.aw:dm.aw + dm.kvw, :]

        @pl.when((s_id == nseq - 1) & (i == nb))
        def _():
            for acc, out in lane_accs:
                out[...] = jnp.sum(acc[...], axis=1, keepdims=True)

    col = lambda rows: jax.ShapeDtypeStruct((rows, 1), F32)
    lane = lambda rows: pltpu.VMEM((rows, LANES), F32)
    return pl.pallas_call(
        body, name="mixer_bwd", grid=(nseq, nb + 1),
        in_specs=[pl.BlockSpec((dm.inw, BLOCK), cur), pl.BlockSpec((kvw2, BLOCK), prev), pl.BlockSpec((dm.d, BLOCK), cur),
                  full(gq.shape), full(gk2.shape), pl.BlockSpec(memory_space=pltpu.SMEM),
                  full(lng.shape), full(lnb.shape), full(ws.shape), full(bs.shape), full(goa.shape), full(gog.shape)],
        out_specs=[pl.BlockSpec((dm.inw, BLOCK), cur), pl.BlockSpec((kvw2, BLOCK), prev_kv_out),
                   full((dm.hd, 1)), full((dm.hd, 1)), full((dm.nh, 1)), full((dm.gw, 1)), full((dm.gw, 1)), full(ws.shape),
                   full(bs.shape), full((dm.aw, 1)), full((dm.gw, 1))],
        out_shape=[jax.ShapeDtypeStruct((dm.inw, t), MXU), jax.ShapeDtypeStruct((kvw2, t), MXU),
                   col(dm.hd), col(dm.hd), col(dm.nh), col(dm.gw), col(dm.gw), jax.ShapeDtypeStruct(ws.shape, F32),
                   jax.ShapeDtypeStruct(bs.shape, F32), col(dm.aw), col(dm.gw)],
        scratch_shapes=[pltpu.VMEM((dm.aw, BLOCK), F32), pltpu.VMEM((dm.aw, BLOCK), F32),
                        pltpu.VMEM((kvw2, BLOCK), F32), pltpu.VMEM((kvw2, BLOCK), F32), pltpu.VMEM((kvw2, BLOCK), F32),
                        pltpu.VMEM((dm.kvw, BLOCK), F32),
                        pltpu.VMEM((dm.hd, dm.grp * BLOCK), F32), lane(dm.hd), lane(dm.nh), lane(dm.gw), lane(dm.gw), lane(dm.aw), lane(dm.gw)],
        compiler_params=_cparams(("arbitrary", "arbitrary"), 48),
    )(proj_t, proj_t, dy_t, gq, gk2, sinks, lng, lnb, ws, bs, goa, gog)


def _patch_kv(dproj_t, dkv_t, dm):
    t = dproj_t.shape[1]
    tc = _pick(t, (1024, 512, 256, 128))
    kvw2 = 2 * dm.kvw
    kvblk = dm.aw // kvw2

    def body(dproj_hbm, dkv_ref, out_ref):
        del dproj_hbm
        out_ref[...] = dkv_ref[...]

    return pl.pallas_call(
        body, name="patch_kv", grid=(t // tc,),
        in_specs=[ANY, pl.BlockSpec((kvw2, tc), lambda i: (0, i))],
        out_specs=pl.BlockSpec((kvw2, tc), lambda i: (kvblk, i)),
        out_shape=jax.ShapeDtypeStruct(dproj_t.shape, dproj_t.dtype),
        input_output_aliases={0: 0},
        compiler_params=_cparams(("parallel",), 32),
    )(dproj_t, dkv_t)


def _place():
    x, y, c = lax.axis_index("x"), lax.axis_index("y"), lax.axis_index("c")
    return x, y, c


def _handshake(peers):
    barrier = pltpu.get_barrier_semaphore()
    for p in peers:
        pl.semaphore_signal(barrier, inc=1, device_id=p, device_id_type=MESH)
    pl.semaphore_wait(barrier, len(peers))


def _sequencer_mesh():
    return plsc.ScalarSubcoreMesh(axis_name="sequencer", num_cores=1)


GATHER_CHUNKS = 4
BF16_ROWS = 16


def _row_chunks(n, k):
    tiles = n // BF16_ROWS
    sizes = [(tiles // k + (1 if i < tiles % k else 0)) * BF16_ROWS for i in range(k)]
    return [(sum(sizes[:i]), sz) for i, sz in enumerate(sizes) if sz]


def _allgather_weight(name, collective_id, shard, after=None):
    n = shard.shape[0]
    assert n % BF16_ROWS == 0
    chunks = _row_chunks(n, GATHER_CHUNKS)
    nc = len(chunks)

    def body(*refs):
        src, out = refs[0], refs[-4]
        send_sems, recv_sems, local_sem = refs[-3:]
        x, y, c = _place()
        me, sib, xn, yn, diag = (x, y, c), (x, y, 1 - c), (1 - x, y, c), (x, 1 - y, c), (1 - x, 1 - y, c)
        relay_to = (x ^ c, y ^ (1 - c), c)
        relay_of = (x ^ (1 - c), y ^ c, c)
        _handshake([sib, xn, yn])

        def rows(place, ci):
            px, py, pc = place
            off, size = chunks[ci]
            return out.at[pl.ds(pl.multiple_of((4 * px + 2 * py + pc) * n + off, BF16_ROWS), size), :]

        def copy(k, ci, block, to, from_src=False):
            off, size = chunks[ci]
            return pltpu.make_async_remote_copy(
                src_ref=src.at[pl.ds(off, size), :] if from_src else rows(block, ci), dst_ref=rows(block, ci),
                send_sem=send_sems.at[ci, k], recv_sem=recv_sems.at[ci, k], device_id=to, device_id_type=MESH)

        mine = pltpu.make_async_copy(src, out.at[pl.ds(pl.multiple_of((4 * x + 2 * y + c) * n, BF16_ROWS), n), :], local_sem)
        mine.start()
        sent = []
        for ci in range(nc):
            sent += [copy(0, ci, me, sib, from_src=True), copy(1, ci, me, xn, from_src=True), copy(2, ci, me, yn, from_src=True)]
        for cp in sent:
            cp.start()
        for ci in range(nc):
            copy(1, ci, xn, me).wait_recv()
            copy(2, ci, yn, me).wait_recv()
            passed = [copy(3, ci, relay_of, relay_to), copy(4, ci, xn, sib), copy(5, ci, yn, sib)]
            for cp in passed:
                cp.start()
            sent += passed
        for ci in range(nc):
            copy(3, ci, diag, me).wait_recv()
            passed = copy(6, ci, diag, sib)
            passed.start()
            sent.append(passed)
        for ci in range(nc):
            copy(0, ci, sib, me).wait_recv()
            for k, block in ((4, (1 - x, y, 1 - c)), (5, (x, 1 - y, 1 - c)), (6, (1 - x, 1 - y, 1 - c))):
                copy(k, ci, block, me).wait_recv()
        for cp in sent:
            cp.wait_send()
        mine.wait()

    return pl.kernel(
        body, name=name,
        out_type=jax.ShapeDtypeStruct((N_DEV * n, shard.shape[1]), shard.dtype),
        mesh=_sequencer_mesh(),
        scratch_types=[pltpu.SemaphoreType.DMA((nc, 7)), pltpu.SemaphoreType.DMA((nc, 7)), pltpu.SemaphoreType.DMA],
        compiler_params=pltpu.CompilerParams(collective_id=collective_id),
    )(*([shard] if after is None else [shard, after]))


_FLIPS = [(0, 0, 1), (1, 0, 0), (0, 1, 0), (1, 1, 0), (1, 0, 1), (0, 1, 1), (1, 1, 1)]


def _scatter_grad(name, collective_id, grad):
    n = grad.shape[0] // N_DEV

    def body(src, out, send_sems, recv_sems, local_sem):
        x, y, c = _place()
        me_idx = 4 * x + 2 * y + c
        peers = [(x ^ fx, y ^ fy, c ^ fc) for (fx, fy, fc) in _FLIPS]
        _handshake(peers)

        def block(idx):
            return src.at[pl.ds(pl.multiple_of(idx * n, 16), n), :]

        copies = [pltpu.make_async_remote_copy(
            src_ref=block(4 * px + 2 * py + pc), dst_ref=out.at[me_idx], send_sem=send_sems.at[k], recv_sem=recv_sems.at[k],
            device_id=(px, py, pc), device_id_type=MESH) for k, (px, py, pc) in enumerate(peers)]
        mine = pltpu.make_async_copy(block(me_idx), out.at[me_idx], local_sem)
        mine.start()
        for cp in copies:
            cp.start()
        for cp in copies:
            cp.wait_recv()
        for cp in copies:
            cp.wait_send()
        mine.wait()

    return pl.kernel(
        body, name=name,
        out_type=jax.ShapeDtypeStruct((N_DEV, n, grad.shape[1]), grad.dtype),
        mesh=_sequencer_mesh(),
        scratch_types=[pltpu.SemaphoreType.DMA((7,)), pltpu.SemaphoreType.DMA((7,)), pltpu.SemaphoreType.DMA],
        compiler_params=pltpu.CompilerParams(collective_id=collective_id),
    )(grad)


def _allgather_rows(name, collective_id, part):
    def body(src, out, send_sems, recv_sems, local_sem):
        x, y, c = _place()
        me_idx = 4 * x + 2 * y + c
        peers = [(x ^ fx, y ^ fy, c ^ fc) for (fx, fy, fc) in _FLIPS]
        _handshake(peers)
        copies = [pltpu.make_async_remote_copy(
            src_ref=src, dst_ref=out.at[me_idx], send_sem=send_sems.at[k], recv_sem=recv_sems.at[k],
            device_id=peer, device_id_type=MESH) for k, peer in enumerate(peers)]
        mine = pltpu.make_async_copy(src, out.at[me_idx], local_sem)
        mine.start()
        for cp in copies:
            cp.start()
        for cp in copies:
            cp.wait_recv()
        for cp in copies:
            cp.wait_send()
        mine.wait()

    return pl.kernel(
        body, name=name,
        out_type=jax.ShapeDtypeStruct((N_DEV,) + part.shape, part.dtype),
        mesh=_sequencer_mesh(),
        scratch_types=[pltpu.SemaphoreType.DMA((7,)), pltpu.SemaphoreType.DMA((7,)), pltpu.SemaphoreType.DMA],
        compiler_params=pltpu.CompilerParams(collective_id=collective_id),
    )(part)


def _adamw_math(w, g, m, v):
    m = ADAM_B1 * m + (1.0 - ADAM_B1) * g
    v = ADAM_B2 * v + (1.0 - ADAM_B2) * (g * g)
    m_hat = m / (1.0 - ADAM_B1 ** ADAM_STEP)
    v_hat = v / (1.0 - ADAM_B2 ** ADAM_STEP)
    delta = -ADAM_LR * (m_hat / (jnp.sqrt(v_hat) + ADAM_EPS) + ADAM_WD * w)
    return delta, m, v


def _sum_adamw(name, slots, w, m, v, after):
    _, n, kk = slots.shape
    tr = _pick(n, (208, 176, 128, 96, 64, 32, 16))

    def body(s_ref, w_ref, m_ref, v_ref, after_ref, g_ref, d_ref, nm_ref, nv_ref):
        del after_ref
        g = s_ref[0].astype(F32)
        for p in range(1, N_DEV):
            g = g + s_ref[p].astype(F32)
        g_ref[...] = g
        d_ref[...], nm_ref[...], nv_ref[...] = _adamw_math(w_ref[...], g, m_ref[...], v_ref[...])

    row = pl.BlockSpec((tr, kk), lambda i: (i, 0))
    return pl.pallas_call(
        body, name=name, grid=(n // tr,),
        in_specs=[pl.BlockSpec((N_DEV, tr, kk), lambda i: (0, i, 0)), row, row, row, ANY],
        out_specs=[row] * 4,
        out_shape=[jax.ShapeDtypeStruct((n, kk), F32)] * 4,
        compiler_params=_cparams(("parallel",), 48),
    )(slots, w, m, v, after)


def _allreduce_small_adamw(early_slots, late, w, m, v, after):
    ra, rb = early_slots.shape[1], late.shape[0]

    def body(early_ref, late_ref, w_ref, m_ref, v_ref, after_ref, g_ref, d_ref, nm_ref, nv_ref, slots, send_sems, recv_sems):
        del after_ref
        x, y, c = _place()
        me_idx = 4 * x + 2 * y + c
        copies = []
        for k, (fx, fy, fc) in enumerate(_FLIPS):
            px, py, pc = x ^ fx, y ^ fy, c ^ fc
            copies.append(pltpu.make_async_remote_copy(
                src_ref=late_ref, dst_ref=slots.at[me_idx], send_sem=send_sems.at[k], recv_sem=recv_sems.at[k],
                device_id=(px, py, pc), device_id_type=MESH))
        for cp in copies:
            cp.start()
        slots[me_idx] = late_ref[...]
        g = early_ref[0]
        for p in range(1, N_DEV):
            g = g + early_ref[p]
        early = pl.ds(0, ra)
        g_ref[early, :] = g
        d_ref[early, :], nm_ref[early, :], nv_ref[early, :] = _adamw_math(w_ref[early, :], g, m_ref[early, :], v_ref[early, :])
        for cp in copies:
            cp.wait_recv()
        for cp in copies:
            cp.wait_send()
        g = slots[0]
        for p in range(1, N_DEV):
            g = g + slots[p]
        tail = pl.ds(ra, rb)
        g_ref[tail, :] = g
        d_ref[tail, :], nm_ref[tail, :], nv_ref[tail, :] = _adamw_math(w_ref[tail, :], g, m_ref[tail, :], v_ref[tail, :])

    vm = pl.BlockSpec(memory_space=pltpu.VMEM)
    return pl.pallas_call(
        body, name="allreduce_small_adamw",
        in_specs=[vm] * 5 + [ANY], out_specs=[vm] * 4,
        out_shape=[jax.ShapeDtypeStruct((ra + rb, LANES), F32)] * 4,
        scratch_shapes=[pltpu.VMEM((N_DEV, rb, LANES), F32), pltpu.SemaphoreType.DMA((7,)), pltpu.SemaphoreType.DMA((7,))],
        compiler_params=pltpu.CompilerParams(vmem_limit_bytes=48 * MIB),
    )(early_slots, late, w, m, v, after)


def _pack(arrs):
    parts, meta, off = [], [], 0
    for a in arrs:
        flat = a.reshape(-1).astype(F32)
        rows = -(-flat.shape[0] // LANES)
        rows8 = -(-rows // 8) * 8
        flat = jnp.pad(flat, (0, rows8 * LANES - flat.shape[0]))
        parts.append(flat.reshape(rows8, LANES))
        meta.append((off, a.shape, a.size))
        off += rows8
    return jnp.concatenate(parts, axis=0), meta


def _unpack(packed, meta):
    outs = []
    for off, shape, size in meta:
        rows = -(-size // LANES)
        outs.append(packed[off:off + rows].reshape(-1)[:size].reshape(shape))
    return outs


def _silu_parts(a):
    sg = 0.5 + 0.5 * jnp.tanh(0.5 * a)
    return a * sg, sg * (1.0 + a * (1.0 - sg))


def kernel(x, norm1_g, w_in, q_norm_g, k_norm_g, attn_sinks, gate_ln_g, gate_ln_b, w_spatial, b_spatial, out_norm_attn_g, out_norm_gate_g, w_out, norm2_g, w_ffn_gate, w_ffn_up, w_ffn_down, loss_target, m_norm1_g, m_w_in, m_q_norm_g, m_k_norm_g, m_attn_sinks, m_gate_ln_g, m_gate_ln_b, m_w_spatial, m_b_spatial, m_out_norm_attn_g, m_out_norm_gate_g, m_w_out, m_norm2_g, m_w_ffn_gate, m_w_ffn_up, m_w_ffn_down, v_norm1_g, v_w_in, v_q_norm_g, v_k_norm_g, v_attn_sinks, v_gate_ln_g, v_gate_ln_b, v_w_spatial, v_b_spatial, v_out_norm_attn_g, v_out_norm_gate_g, v_w_out, v_norm2_g, v_w_ffn_gate, v_w_ffn_up, v_w_ffn_down):
    nseq, seq, d = x.shape
    t = nseq * seq
    nb = seq // BLOCK
    inw = w_in.shape[2] * N_DEV
    dm = _Dims(d, inw, q_norm_g.shape[-1])
    xf = x.reshape(t, d)
    tgt = loss_target.reshape(t, d)

    rows = lambda wv, transposed: jnp.swapaxes(wv, 1, 2)[0] if transposed else wv[0]
    big = {"w_in": (w_in, m_w_in, v_w_in, True), "w_out": (w_out, m_w_out, v_w_out, False),
           "w_ffn_gate": (w_ffn_gate, m_w_ffn_gate, v_w_ffn_gate, True), "w_ffn_up": (w_ffn_up, m_w_ffn_up, v_w_ffn_up, True),
           "w_ffn_down": (w_ffn_down, m_w_ffn_down, v_w_ffn_down, False)}
    big_rows = {nm: tuple(rows(arr, tr) for arr in (wv, mv, vv)) for nm, (wv, mv, vv, tr) in big.items()}
    shard = lambda nm: big_rows[nm][0].astype(WIRE)
    win_t = _allgather_weight("gather_w_in", 1, shard("w_in"))
    wout = _allgather_weight("gather_w_out", 2, shard("w_out"), after=win_t)
    wg_t = _allgather_weight("gather_w_ffn_gate", 3, shard("w_ffn_gate"), after=win_t)
    wu_t = _allgather_weight("gather_w_ffn_up", 9, shard("w_ffn_up"), after=win_t)
    wd = _allgather_weight("gather_w_ffn_down", 10, shard("w_ffn_down"), after=win_t)

    lanes = lambda v, n=BLOCK: jnp.broadcast_to(v.reshape(-1, 1), (v.size, n))
    prm = (lanes(q_norm_g, dm.grp * BLOCK), lanes(k_norm_g, 2 * BLOCK), attn_sinks[0], lanes(gate_ln_g), lanes(gate_ln_b), w_spatial[0], b_spatial[0],
           lanes(out_norm_attn_g), lanes(out_norm_gate_g))

    h1 = _rms_fwd("rms1_fwd", xf, norm1_g)
    (proj_t,) = _matmul("mm_in", win_t, h1, "nt", [F32])
    y_t = _mixer_fwd(proj_t, prm, dm, nseq, nb)

    def residual_norm(acc, xr, g2):
        x2v = xr + acc
        return x2v, x2v * lax.rsqrt(jnp.mean(x2v * x2v, axis=-1, keepdims=True) + EPS) * g2

    x2, h2 = _matmul("mm_out", y_t, wout, "tn", [F32, MXU], epilogue=residual_norm, extras=[xf], rowvecs=[norm2_g], full_rows=True)
    a, b, s = _matmul("mm_gate_up", h2, wg_t, "nt", [MXU, MXU, MXU], b2=wu_t, epilogue=lambda ga, ub: (ga, ub, _silu_parts(ga)[0] * ub))

    def loss_epilogue(acc, x2v, tv):
        diff = (x2v + acc) - tv
        dx3 = diff * (1.0 / d)
        return dx3, dx3, jnp.sum(diff * diff)

    dx3, dx3b, lossp = _matmul("mm_down", s, wd, "nn", [F32, MXU], epilogue=loss_epilogue, extras=[x2, tgt], partial=True)
    loss_part = (0.5 / d) * jnp.sum(lossp[::8, ::LANES])

    def dswiglu(acc, av, bv):
        silu, dsilu = _silu_parts(av.astype(F32))
        return acc * bv.astype(F32) * dsilu, acc * silu

    da, db = _matmul("mm_d_down", dx3b, wd, "nt", [MXU, MXU], epilogue=dswiglu, extras=[a, b])
    (g_wd,) = _matmul("mm_gw_down", s, dx3b, "tn", [WIRE])
    sl_wd = _scatter_grad("scatter_w_ffn_down", 4, g_wd)
    (dh2a,) = _matmul("mm_dh2_gate", da, wg_t, "nn", [F32], after=[g_wd])
    (g_wg,) = _matmul("mm_gw_gate", da, h2, "tn", [WIRE], after=[dh2a])
    sl_wg = _scatter_grad("scatter_w_ffn_gate", 5, g_wg)
    (dh2,) = _matmul("mm_dh2_up", db, wu_t, "nn", [F32], epilogue=lambda acc, pv: (pv + acc,), extras=[dh2a], after=[g_wg])
    (g_wu,) = _matmul("mm_gw_up", db, h2, "tn", [WIRE], after=[dh2])
    sl_wu = _scatter_grad("scatter_w_ffn_up", 6, g_wu)

    dy_t, dx2, dx2b, dg2 = _norm_bwd_matmul("mm_d_out", wout, dh2, x2, norm2_g, dx3, after=g_wu)
    (g_wout,) = _matmul("mm_gw_out", y_t, dx2b, "nn", [WIRE], after=[dy_t])
    sl_wout = _scatter_grad("scatter_w_out", 7, g_wout)
    (dproj0, dkv, dgq, dgk, dsink, dlng, dlnb, dws, dbs, dgoa, dgog) = _mixer_bwd(proj_t, dy_t, prm, dm, nseq, nb)
    early_g = [dgq, dgk, dsink, dlng, dlnb, dws, dbs, dgoa, dgog, dg2, loss_part.reshape(1)]
    early_slots = _allgather_rows("gather_small_grads", 11, _pack(early_g)[0])
    dproj_t = _patch_kv(dproj0, dkv, dm)
    (g_win,) = _matmul("mm_gw_in", dproj_t, h1, "nn", [WIRE])
    sl_win = _scatter_grad("scatter_w_in", 8, g_win)

    def norm1_backward(dh1, xv, dx2v, g1):
        r = lax.rsqrt(jnp.mean(xv * xv, axis=-1, keepdims=True) + EPS)
        xh = xv * r
        dxh = dh1 * g1
        return dx2v + r * (dxh - xh * jnp.mean(dxh * xh, axis=-1, keepdims=True)), jnp.sum(dh1 * xh, axis=0, keepdims=True)

    dx, dg1 = _matmul("mm_d_in", dproj_t, win_t, "tn", [F32], epilogue=norm1_backward, extras=[xf, dx2], rowvecs=[norm1_g],
                      after=[g_win], col_sum=True, full_rows=True)

    big_out = {}
    last = dx

    def big_update(nm, sl, after):
        res = _sum_adamw("adamw_" + nm, sl, *big_rows[nm], after=after)
        big_out[nm] = tuple(jnp.swapaxes(r[None], 1, 2) if big[nm][3] else r[None] for r in res)
        return res[1]

    for nm, sl in (("w_ffn_down", sl_wd), ("w_ffn_gate", sl_wg), ("w_ffn_up", sl_wu), ("w_out", sl_wout)):
        last = big_update(nm, sl, last)

    zero = jnp.zeros((1,), F32)
    small_names = ["q_norm_g", "k_norm_g", "attn_sinks", "gate_ln_g", "gate_ln_b", "w_spatial", "b_spatial",
                   "out_norm_attn_g", "out_norm_gate_g", "norm2_g", "loss", "norm1_g"]
    small_w = [q_norm_g, k_norm_g, attn_sinks, gate_ln_g, gate_ln_b, w_spatial, b_spatial, out_norm_attn_g, out_norm_gate_g, norm2_g, zero, norm1_g]
    small_m = [m_q_norm_g, m_k_norm_g, m_attn_sinks, m_gate_ln_g, m_gate_ln_b, m_w_spatial, m_b_spatial, m_out_norm_attn_g, m_out_norm_gate_g, m_norm2_g, zero, m_norm1_g]
    small_v = [v_q_norm_g, v_k_norm_g, v_attn_sinks, v_gate_ln_g, v_gate_ln_b, v_w_spatial, v_b_spatial, v_out_norm_attn_g, v_out_norm_gate_g, v_norm2_g, zero, v_norm1_g]
    pw, meta = _pack(small_w)
    sg, sd, sm, sv = _allreduce_small_adamw(early_slots, _pack([dg1])[0], pw, _pack(small_m)[0], _pack(small_v)[0], after=last)
    big_update("w_in", sl_win, sd)
    ug, ud, um, uv = _unpack(sg, meta), _unpack(sd, meta), _unpack(sm, meta), _unpack(sv, meta)
    small_out = {nm: (ug[k], ud[k], um[k], uv[k]) for k, nm in enumerate(small_names)}
    loss = small_out["loss"][0].reshape(())

    order = ["norm1_g", "w_in", "q_norm_g", "k_norm_g", "attn_sinks", "gate_ln_g", "gate_ln_b", "w_spatial", "b_spatial",
             "out_norm_attn_g", "out_norm_gate_g", "w_out", "norm2_g", "w_ffn_gate", "w_ffn_up", "w_ffn_down"]
    allo = {**big_out, **small_out}
    outs = [loss, dx.reshape(nseq, seq, d)]
    for k in range(4):
        outs += [allo[nm][k] for nm in order]
    return tuple(outs)
```

```python
import math

import jax
import jax.numpy as jnp
from jax import lax
from jax.experimental import pallas as pl
from jax.experimental.pallas import tpu as pltpu
from jax.experimental.pallas import tpu_sc as plsc

F32 = jnp.float32
MXU = jnp.bfloat16
WIRE = jnp.bfloat16
EPS = 1e-6
BLOCK = 128
GROUP_DIM = 128
N_KV_HEADS = 2
NEG = -1e30
N_DEV = 8
LANES = 128
MIB = 1024 * 1024

ADAM_LR = 0.001
ADAM_B1 = 0.9
ADAM_B2 = 0.999
ADAM_EPS = 1e-08
ADAM_WD = 0.01
ADAM_STEP = 10

MESH = pl.DeviceIdType.MESH
ANY = pl.BlockSpec(memory_space=pl.ANY)


def _pick(n, cands):
    for c in cands:
        if n % c == 0:
            return c
    return n


def _cparams(sem, vmem_mb):
    return pltpu.CompilerParams(dimension_semantics=sem, vmem_limit_bytes=vmem_mb * MIB)


VMEM_TILE_BUDGET = 44 * MIB
HBM_BYTES_PER_US = 3.0e6
STEP_US = 0.4
MIN_TILE_N = 512


def _tile_candidates(n):
    return [c for c in range(min(n, 2048), 0, -LANES) if n % c == 0 and c % LANES == 0] or [n]


def _matmul_tiles(m, n, kk, esz, n_b, extra_sizes, out_sizes, full_rows):
    best = None
    wide = [n] if full_rows else [c for c in _tile_candidates(n) if c >= MIN_TILE_N] or _tile_candidates(n)
    for tm in _tile_candidates(m):
        for tn in wide:
            vmem = 2 * (tm + n_b * tn) * kk * esz + tm * tn * (4 * n_b + 2 * sum(extra_sizes) + 2 * sum(out_sizes))
            if vmem > VMEM_TILE_BUDGET:
                continue
            cost = (m // tm) * n_b * n * kk * esz / HBM_BYTES_PER_US + (m // tm) * (n // tn) * STEP_US
            if best is None or cost < best[0]:
                best = (cost, tm, tn, vmem)
    assert best is not None, (m, n, kk)
    return best[1:]


def _matmul(name, a, b, mode, out_dtypes, epilogue=None, extras=(), rowvecs=(), after=(), partial=False, col_sum=False, full_rows=False, b2=None):
    if mode == "nn":
        (m, kk), n = a.shape, b.shape[1]
        dn = (((1,), (0,)), ((), ()))
    elif mode == "nt":
        (m, kk), n = a.shape, b.shape[0]
        dn = (((1,), (1,)), ((), ()))
    else:
        (kk, m), n = a.shape, b.shape[1]
        dn = (((0,), (0,)), ((), ()))
    bs = [b] if b2 is None else [b, b2]
    tm, tn, vmem = _matmul_tiles(m, n, kk, a.dtype.itemsize, len(bs), [e.dtype.itemsize for e in extras],
                                 [jnp.dtype(dt).itemsize for dt in out_dtypes], full_rows)
    a_spec = pl.BlockSpec((kk, tm), lambda i, j: (0, i)) if mode == "tn" else pl.BlockSpec((tm, kk), lambda i, j: (i, 0))
    b_spec = pl.BlockSpec((tn, kk), lambda i, j: (j, 0)) if mode == "nt" else pl.BlockSpec((kk, tn), lambda i, j: (0, j))
    tile = pl.BlockSpec((tm, tn), lambda i, j: (i, j))
    row = pl.BlockSpec((1, tn), lambda i, j: (0, j))
    nb, ne, nr, na, no = len(bs), len(extras), len(rowvecs), len(after), len(out_dtypes)

    def body(a_ref, *rest):
        b_refs, in_refs, out_refs = rest[:nb], rest[nb:nb + ne + nr], rest[nb + ne + nr + na:]
        av = a_ref[...]
        accs = [lax.dot_general(av, b_ref[...], dn, preferred_element_type=F32) for b_ref in b_refs]
        vals = tuple(accs) if epilogue is None else epilogue(*accs, *[r[...] for r in in_refs])
        for o_ref, t in zip(out_refs[:no], vals[:no]):
            o_ref[...] = t.astype(o_ref.dtype)
        if partial:
            out_refs[no][...] = jnp.full((8, LANES), vals[no], F32)
        if col_sum:
            sum_ref = out_refs[-1]

            @pl.when(pl.program_id(0) == 0)
            def _():
                sum_ref[...] = jnp.zeros_like(sum_ref)

            sum_ref[...] += vals[-1]

    out_specs = [tile] * no
    out_shape = [jax.ShapeDtypeStruct((m, n), dt) for dt in out_dtypes]
    if partial:
        out_specs.append(pl.BlockSpec((8, LANES), lambda i, j: (i, j)))
        out_shape.append(jax.ShapeDtypeStruct((m // tm * 8, n // tn * LANES), F32))
    if col_sum:
        out_specs.append(row)
        out_shape.append(jax.ShapeDtypeStruct((1, n), F32))
    return pl.pallas_call(
        body, name=name, grid=(m // tm, n // tn),
        in_specs=[a_spec] + [b_spec] * nb + [tile] * ne + [row] * nr + [ANY] * na,
        out_specs=out_specs, out_shape=out_shape,
        compiler_params=_cparams(("arbitrary" if col_sum else "parallel", "arbitrary"), min(vmem // MIB + 8, 60)),
    )(a, *bs, *extras, *rowvecs, *after)


def _rms_fwd(name, x, g):
    t, d = x.shape
    tm = _pick(t, (512, 256, 128))

    def body(x_ref, g_ref, h_ref):
        xv = x_ref[...]
        r = lax.rsqrt(jnp.mean(xv * xv, axis=-1, keepdims=True) + EPS)
        h_ref[...] = (xv * r * g_ref[...]).astype(h_ref.dtype)

    return pl.pallas_call(
        body, name=name, grid=(t // tm,),
        in_specs=[pl.BlockSpec((tm, d), lambda i: (i, 0)), pl.BlockSpec((1, d), lambda i: (0, 0))],
        out_specs=pl.BlockSpec((tm, d), lambda i: (i, 0)),
        out_shape=jax.ShapeDtypeStruct((t, d), MXU),
        compiler_params=_cparams(("parallel",), 32),
    )(x, g)


def _norm_bwd_matmul(name, w, dh, x, g, res, after):
    t, d = x.shape
    m = w.shape[0]
    tn = _pick(t, (256, 128))

    def body(w_ref, dh_ref, x_ref, g_ref, res_ref, after_ref, out_ref, dx_ref, dxb_ref, dg_ref):
        del after_ref

        @pl.when(pl.program_id(0) == 0)
        def _():
            dg_ref[...] = jnp.zeros_like(dg_ref)

        xv, dhv = x_ref[...], dh_ref[...]
        r = lax.rsqrt(jnp.mean(xv * xv, axis=-1, keepdims=True) + EPS)
        xh = xv * r
        dg_ref[...] += jnp.sum(dhv * xh, axis=0, keepdims=True)
        dxh = dhv * g_ref[...]
        dx = res_ref[...] + r * (dxh - xh * jnp.mean(dxh * xh, axis=-1, keepdims=True))
        dx_ref[...] = dx
        dxb = dx.astype(MXU)
        dxb_ref[...] = dxb
        out_ref[...] = lax.dot_general(w_ref[...], dxb, (((1,), (1,)), ((), ())), preferred_element_type=F32)

    row = pl.BlockSpec((tn, d), lambda j: (j, 0))
    vec = pl.BlockSpec((1, d), lambda j: (0, 0))
    return pl.pallas_call(
        body, name=name, grid=(t // tn,),
        in_specs=[pl.BlockSpec((m, d), lambda j: (0, 0)), row, row, vec, row, ANY],
        out_specs=[pl.BlockSpec((m, tn), lambda j: (0, j)), row, row, vec],
        out_shape=[jax.ShapeDtypeStruct((m, t), F32), jax.ShapeDtypeStruct((t, d), F32), jax.ShapeDtypeStruct((t, d), MXU),
                   jax.ShapeDtypeStruct((1, d), F32)],
        compiler_params=_cparams(("arbitrary",), 52),
    )(w, dh, x, g, res, after)


_INV_SQRT2 = 0.7071067811865476
_INV_SQRT_2PI = 0.3989422804014327


def _dot_nt(a, b):
    return lax.dot_general(a, b, (((1,), (1,)), ((), ())), preferred_element_type=F32)


def _dot_tn(a, b):
    return lax.dot_general(a, b, (((0,), (0,)), ((), ())), preferred_element_type=F32)


def _dot(a, b):
    return jnp.dot(a, b, preferred_element_type=F32)


def _col_rms(v):
    return lax.rsqrt(jnp.mean(v * v, axis=0, keepdims=True) + EPS)


class _Dims:
    def __init__(self, d_model, in_width, head_dim):
        self.d = d_model
        self.aw = d_model // 2
        self.gw = d_model - self.aw
        self.kvw = (in_width - self.aw - 2 * self.gw) // 2
        self.hd = head_dim
        self.nh = self.aw // head_dim
        self.nkv = self.kvw // head_dim
        self.grp = self.nh // self.nkv
        self.ng = self.gw // GROUP_DIM
        self.inw = in_width
        self.zoff = self.aw + 2 * self.kvw
        assert self.nkv == N_KV_HEADS and self.zoff + 2 * self.gw == in_width and self.aw % (2 * self.kvw) == 0


def _band_masks(first):
    r = lax.broadcasted_iota(jnp.int32, (BLOCK, BLOCK), 0)
    t = lax.broadcasted_iota(jnp.int32, (BLOCK, BLOCK), 1)
    upper = r > t
    dist = t - r + jnp.where(upper, BLOCK, 0)
    return upper, jnp.logical_not(upper & first), dist.astype(F32)


def _fold(full, upper):
    return jnp.where(upper, full[:BLOCK], full[BLOCK:])


def _unfold(folded, upper):
    zero = jnp.zeros_like(folded)
    return jnp.concatenate([jnp.where(upper, folded, zero), jnp.where(upper, zero, folded)], axis=0)


def _kv_band(dm, kh, p_ref, pkv_ref, gk2):
    ko = dm.aw + kh * dm.hd
    vo = dm.aw + dm.kvw + kh * dm.hd
    k_t = jnp.concatenate([pkv_ref[kh * dm.hd:(kh + 1) * dm.hd, :], p_ref[ko:ko + dm.hd, :]], axis=1)
    v_t = jnp.concatenate([pkv_ref[dm.kvw + kh * dm.hd:dm.kvw + (kh + 1) * dm.hd, :], p_ref[vo:vo + dm.hd, :]], axis=1)
    kn_t = k_t * _col_rms(k_t) * gk2
    return kn_t.astype(MXU), kn_t.T.astype(MXU), v_t.astype(MXU), v_t.T.astype(MXU)


def _group_heads(dm, kh):
    return range(kh * dm.grp, (kh + 1) * dm.grp)


def _attn_group_fwd(dm, kh, p_ref, gq, kn, v_tb, sink_ref, masks):
    heads = _group_heads(dm, kh)
    q = jnp.concatenate([p_ref[h * dm.hd:(h + 1) * dm.hd, :] for h in heads], axis=1)
    rq = _col_rms(q)
    qh = q * rq
    qnb = (qh * gq).astype(MXU)
    upper, valid, dist = masks
    s = _dot(kn, qnb)
    probs, probs_b, sink_probs = [], [], []
    for g, h in enumerate(heads):
        slope, sink = math.pow(2.0, -8.0 * (h + 1) / dm.nh), sink_ref[h]
        logits = jnp.where(valid, _fold(s[:, g * BLOCK:(g + 1) * BLOCK], upper) * (dm.hd ** -0.5) - slope * dist, NEG)
        m = jnp.maximum(jnp.max(logits, axis=0, keepdims=True), sink)
        e = jnp.exp(logits - m)
        es = jnp.exp(sink - m)
        inv = 1.0 / (jnp.sum(e, axis=0, keepdims=True) + es)
        probs.append(e * inv)
        probs_b.append(_unfold(probs[g], upper).astype(MXU))
        sink_probs.append(es * inv)
    probs_b = jnp.concatenate(probs_b, axis=1)
    o = _dot(v_tb, probs_b)
    return o, probs, probs_b, sink_probs, rq, qh, qnb


def _gelu_cdf(z):
    return 0.5 * (1.0 + lax.erf(z * _INV_SQRT2))


def _by_group(v, ng):
    return v.reshape(ng, GROUP_DIM, v.shape[1])


def _gate_fwd(dm, p_ref, lng_ref, lnb_ref, ws_ref, bs_ref, tril):
    zu, zv = p_ref[dm.zoff:dm.zoff + dm.gw, :], p_ref[dm.zoff + dm.gw:dm.zoff + 2 * dm.gw, :]
    cu, cv = _gelu_cdf(zu), _gelu_cdf(zv)
    u, v = zu * cu, zv * cv
    v3 = _by_group(v, dm.ng)
    xc = v3 - jnp.mean(v3, axis=1, keepdims=True)
    rstd = lax.rsqrt(jnp.mean(xc * xc, axis=1, keepdims=True) + EPS)
    xh = (xc * rstd).reshape(dm.gw, BLOCK)
    vnb = (xh * lng_ref[...] + lnb_ref[...]).astype(MXU)
    wts = [jnp.where(tril, ws_ref[g], 0.0).astype(MXU) for g in range(dm.ng)]
    mixed = jnp.concatenate([_dot_nt(vnb[g * GROUP_DIM:(g + 1) * GROUP_DIM], wts[g]) + bs_ref[g:g + 1, :]
                             for g in range(dm.ng)], axis=0)
    return u * mixed, u, mixed, xh, rstd, vnb, wts, (zu, cu), (zv, cv)


def _mixer_specs(dm, nb, clamp):
    kvblk = dm.aw // (2 * dm.kvw)

    def cur(s, i):
        return (0, s * nb + clamp(i))

    def prev(s, i):
        return (kvblk, s * nb + jnp.maximum(clamp(i) - 1, 0))

    full = lambda shape: pl.BlockSpec(shape, lambda s, i: tuple(0 for _ in shape))
    return cur, prev, full


def _tril():
    return lax.broadcasted_iota(jnp.int32, (BLOCK, BLOCK), 0) >= lax.broadcasted_iota(jnp.int32, (BLOCK, BLOCK), 1)


def _mixer_fwd(proj_t, prm, dm, nseq, nb):
    gq, gk2, sinks, lng, lnb, ws, bs, goa, gog = prm
    t = proj_t.shape[1]
    cur, prev, full = _mixer_specs(dm, nb, lambda i: i)

    def body(p_ref, pkv_ref, gq_ref, gk_ref, sink_ref, lng_ref, lnb_ref, ws_ref, bs_ref, goa_ref, gog_ref, y_ref, att_scr):
        i = pl.program_id(1)
        masks = _band_masks(i == 0)
        gqv, gkv = gq_ref[...], gk_ref[...]
        for kh in range(dm.nkv):
            _, kn, v_tb, _ = _kv_band(dm, kh, p_ref, pkv_ref, gkv)
            o = _attn_group_fwd(dm, kh, p_ref, gqv, kn, v_tb, sink_ref, masks)[0]
            for g, h in enumerate(_group_heads(dm, kh)):
                att_scr[h * dm.hd:(h + 1) * dm.hd, :] = o[:, g * BLOCK:(g + 1) * BLOCK]
        att = att_scr[...]
        y_ref[:dm.aw, :] = (att * _col_rms(att) * goa_ref[...]).astype(y_ref.dtype)
        gt = _gate_fwd(dm, p_ref, lng_ref, lnb_ref, ws_ref, bs_ref, _tril())[0]
        y_ref[dm.aw:, :] = (gt * _col_rms(gt) * gog_ref[...]).astype(y_ref.dtype)

    return pl.pallas_call(
        body, name="mixer_fwd", grid=(nseq, nb),
        in_specs=[pl.BlockSpec((dm.inw, BLOCK), cur), pl.BlockSpec((2 * dm.kvw, BLOCK), prev),
                  full(gq.shape), full(gk2.shape), pl.BlockSpec(memory_space=pltpu.SMEM),
                  full(lng.shape), full(lnb.shape), full(ws.shape), full(bs.shape), full(goa.shape), full(gog.shape)],
        out_specs=pl.BlockSpec((dm.d, BLOCK), cur),
        out_shape=jax.ShapeDtypeStruct((dm.d, t), MXU),
        scratch_shapes=[pltpu.VMEM((dm.aw, BLOCK), F32)],
        compiler_params=_cparams(("parallel", "arbitrary"), 40),
    )(proj_t, proj_t, gq, gk2, sinks, lng, lnb, ws, bs, goa, gog)


def _mixer_bwd(proj_t, dy_t, prm, dm, nseq, nb):
    gq, gk2, sinks, lng, lnb, ws, bs, goa, gog = prm
    t = proj_t.shape[1]
    clamp = lambda i: jnp.minimum(i, nb - 1)
    cur, prev, full = _mixer_specs(dm, nb, clamp)
    kvw2 = 2 * dm.kvw

    def prev_kv_out(s, i):
        return (0, s * nb + jnp.maximum(i - 1, 0))

    def body(p_ref, pkv_ref, dy_ref, gq_ref, gk_ref, sink_ref, lng_ref, lnb_ref, ws_ref, bs_ref, goa_ref, gog_ref,
             dproj_ref, dkv_ref, dgq_ref, dgk_ref, dsink_ref, dlng_ref, dlnb_ref, dws_ref, dbs_ref, dgoa_ref, dgog_ref,
             att_scr, datt_scr, carry_scr, prevpart_scr, curpart_scr, kprev_scr,
             a_gq, a_gk, a_sink, a_lng, a_lnb, a_goa, a_gog):
        s_id, i = pl.program_id(0), pl.program_id(1)
        lane_accs = ((a_gq, dgq_ref), (a_gk, dgk_ref), (a_sink, dsink_ref), (a_lng, dlng_ref), (a_lnb, dlnb_ref),
                     (a_goa, dgoa_ref), (a_gog, dgog_ref))

        @pl.when((s_id == 0) & (i == 0))
        def _():
            for acc, _ in lane_accs:
                acc[...] = jnp.zeros_like(acc)
            dws_ref[...] = jnp.zeros_like(dws_ref)
            dbs_ref[...] = jnp.zeros_like(dbs_ref)

        gqv, gkv = gq_ref[...], gk_ref[...]

        @pl.when(i < nb)
        def _():
            masks = _band_masks(i == 0)
            upper = masks[0]
            kvs, fwd = [], []
            for kh in range(dm.nkv):
                kv = _kv_band(dm, kh, p_ref, pkv_ref, gkv)
                kvs.append(kv)
                fwd.append(_attn_group_fwd(dm, kh, p_ref, gqv, kv[1], kv[2], sink_ref, masks))
                for g, h in enumerate(_group_heads(dm, kh)):
                    att_scr[h * dm.hd:(h + 1) * dm.hd, :] = fwd[kh][0][:, g * BLOCK:(g + 1) * BLOCK]
            att = att_scr[...]
            dya = dy_ref[:dm.aw, :]
            ra = _col_rms(att)
            ah = att * ra
            a_goa[...] += dya * ah
            dah = dya * goa_ref[...]
            datt_scr[...] = ra * (dah - ah * jnp.mean(dah * ah, axis=0, keepdims=True))
            for kh in range(dm.nkv):
                kn_tb, kn, v_tb, vb = kvs[kh]
                _, probs, probs_b, sink_probs, rq, qh, qnb = fwd[kh]
                heads = _group_heads(dm, kh)
                do_b = jnp.concatenate([datt_scr[h * dm.hd:(h + 1) * dm.hd, :] for h in heads], axis=1).astype(MXU)
                dp = _dot(vb, do_b)
                ds = []
                for g, h in enumerate(heads):
                    p, dp_h = probs[g], _fold(dp[:, g * BLOCK:(g + 1) * BLOCK], upper)
                    delta = jnp.sum(p * dp_h, axis=0, keepdims=True)
                    ds.append(_unfold(p * (dp_h - delta) * (dm.hd ** -0.5), upper).astype(MXU))
                    a_sink[h:h + 1, :] += -(sink_probs[g] * delta)
                dsb = jnp.concatenate(ds, axis=1)
                dqn = _dot(kn_tb, dsb)
                dkn = _dot_nt(qnb, dsb)
                dvb = _dot_nt(do_b, probs_b)
                a_gq[...] += dqn * qh
                dqh = dqn * gqv
                dq = rq * (dqh - qh * jnp.mean(dqh * qh, axis=0, keepdims=True))
                for g, h in enumerate(heads):
                    dproj_ref[h * dm.hd:(h + 1) * dm.hd, :] = dq[:, g * BLOCK:(g + 1) * BLOCK].astype(dproj_ref.dtype)
                krows = slice(kh * dm.hd, (kh + 1) * dm.hd)
                vrows = slice(dm.kvw + kh * dm.hd, dm.kvw + (kh + 1) * dm.hd)
                prevpart_scr[krows, :] = dkn[:, :BLOCK]
                prevpart_scr[vrows, :] = dvb[:, :BLOCK]
                curpart_scr[krows, :] = dkn[:, BLOCK:]
                curpart_scr[vrows, :] = dvb[:, BLOCK:]
            dproj_ref[dm.aw:dm.zoff, :] = jnp.zeros((kvw2, BLOCK), dproj_ref.dtype)
            tril = _tril()
            gt, u, mixed, xh, rstd, vnb, wts, (zu, cu), (zv, cv) = _gate_fwd(dm, p_ref, lng_ref, lnb_ref, ws_ref, bs_ref, tril)
            dyg = dy_ref[dm.aw:, :]
            rg = _col_rms(gt)
            gh = gt * rg
            a_gog[...] += dyg * gh
            dgh = dyg * gog_ref[...]
            dgt = rg * (dgh - gh * jnp.mean(dgh * gh, axis=0, keepdims=True))
            du = dgt * mixed
            dmix = dgt * u
            dmixb = dmix.astype(MXU)
            dbs_ref[...] += jnp.sum(_by_group(dmix, dm.ng), axis=1)
            dvn = []
            for g in range(dm.ng):
                rows = slice(g * GROUP_DIM, (g + 1) * GROUP_DIM)
                dws_ref[g] += jnp.where(tril, _dot_tn(dmixb[rows], vnb[rows]), 0.0)
                dvn.append(_dot(dmixb[rows], wts[g]))
            dvn = jnp.concatenate(dvn, axis=0)
            a_lng[...] += dvn * xh
            a_lnb[...] += dvn
            dxh3, xh3 = _by_group(dvn * lng_ref[...], dm.ng), _by_group(xh, dm.ng)
            dv = (rstd * (dxh3 - jnp.mean(dxh3, axis=1, keepdims=True) - xh3 * jnp.mean(dxh3 * xh3, axis=1, keepdims=True))).reshape(dm.gw, BLOCK)
            dgu = cu + zu * (jnp.exp(-0.5 * zu * zu) * _INV_SQRT_2PI)
            dgv = cv + zv * (jnp.exp(-0.5 * zv * zv) * _INV_SQRT_2PI)
            dproj_ref[dm.zoff:dm.zoff + dm.gw, :] = (du * dgu).astype(dproj_ref.dtype)
            dproj_ref[dm.zoff + dm.gw:, :] = (dv * dgv).astype(dproj_ref.dtype)

        @pl.when(i == nb)
        def _():
            prevpart_scr[...] = jnp.zeros_like(prevpart_scr)

        @pl.when(i >= 1)
        def _():
            tot = carry_scr[...] + prevpart_scr[...]
            for kh in range(dm.nkv):
                krows = slice(kh * dm.hd, (kh + 1) * dm.hd)
                kraw = kprev_scr[krows, :]
                rk = _col_rms(kraw)
                khat = kraw * rk
                dkn = tot[krows, :]
                a_gk[...] += dkn * khat
                dkh = dkn * gkv[:, :BLOCK]
                dk = rk * (dkh - khat * jnp.mean(dkh * khat, axis=0, keepdims=True))
                dkv_ref[krows, :] = dk.astype(dkv_ref.dtype)
            dkv_ref[dm.kvw:, :] = tot[dm.kvw:, :].astype(dkv_ref.dtype)

        @pl.when(i < nb)
        def _():
            carry_scr[...] = curpart_scr[...]
            kprev_scr[...] = p_ref[dm.aw:dm.aw + dm.kvw, :]

        @pl.when((s_id == nseq - 1) & (i == nb))
        def _():
            for acc, out in lane_accs:
                out[...] = jnp.sum(acc[...], axis=1, keepdims=True)

    col = lambda rows: jax.ShapeDtypeStruct((rows, 1), F32)
    lane = lambda rows: pltpu.VMEM((rows, LANES), F32)
    return pl.pallas_call(
        body, name="mixer_bwd", grid=(nseq, nb + 1),
        in_specs=[pl.BlockSpec((dm.inw, BLOCK), cur), pl.BlockSpec((kvw2, BLOCK), prev), pl.BlockSpec((dm.d, BLOCK), cur),
                  full(gq.shape), full(gk2.shape), pl.BlockSpec(memory_space=pltpu.SMEM),
                  full(lng.shape), full(lnb.shape), full(ws.shape), full(bs.shape), full(goa.shape), full(gog.shape)],
        out_specs=[pl.BlockSpec((dm.inw, BLOCK), cur), pl.BlockSpec((kvw2, BLOCK), prev_kv_out),
                   full((dm.hd, 1)), full((dm.hd, 1)), full((dm.nh, 1)), full((dm.gw, 1)), full((dm.gw, 1)), full(ws.shape),
                   full(bs.shape), full((dm.aw, 1)), full((dm.gw, 1))],
        out_shape=[jax.ShapeDtypeStruct((dm.inw, t), MXU), jax.ShapeDtypeStruct((kvw2, t), MXU),
                   col(dm.hd), col(dm.hd), col(dm.nh), col(dm.gw), col(dm.gw), jax.ShapeDtypeStruct(ws.shape, F32),
                   jax.ShapeDtypeStruct(bs.shape, F32), col(dm.aw), col(dm.gw)],
        scratch_shapes=[pltpu.VMEM((dm.aw, BLOCK), F32), pltpu.VMEM((dm.aw, BLOCK), F32),
                        pltpu.VMEM((kvw2, BLOCK), F32), pltpu.VMEM((kvw2, BLOCK), F32), pltpu.VMEM((kvw2, BLOCK), F32),
                        pltpu.VMEM((dm.kvw, BLOCK), F32),
                        pltpu.VMEM((dm.hd, dm.grp * BLOCK), F32), lane(dm.hd), lane(dm.nh), lane(dm.gw), lane(dm.gw), lane(dm.aw), lane(dm.gw)],
        compiler_params=_cparams(("arbitrary", "arbitrary"), 48),
    )(proj_t, proj_t, dy_t, gq, gk2, sinks, lng, lnb, ws, bs, goa, gog)


def _patch_kv(dproj_t, dkv_t, dm):
    t = dproj_t.shape[1]
    tc = _pick(t, (1024, 512, 256, 128))
    kvw2 = 2 * dm.kvw
    kvblk = dm.aw // kvw2

    def body(dproj_hbm, dkv_ref, out_ref):
        del dproj_hbm
        out_ref[...] = dkv_ref[...]

    return pl.pallas_call(
        body, name="patch_kv", grid=(t // tc,),
        in_specs=[ANY, pl.BlockSpec((kvw2, tc), lambda i: (0, i))],
        out_specs=pl.BlockSpec((kvw2, tc), lambda i: (kvblk, i)),
        out_shape=jax.ShapeDtypeStruct(dproj_t.shape, dproj_t.dtype),
        input_output_aliases={0: 0},
        compiler_params=_cparams(("parallel",), 32),
    )(dproj_t, dkv_t)


def _place():
    x, y, c = lax.axis_index("x"), lax.axis_index("y"), lax.axis_index("c")
    return x, y, c


def _handshake(peers):
    barrier = pltpu.get_barrier_semaphore()
    for p in peers:
        pl.semaphore_signal(barrier, inc=1, device_id=p, device_id_type=MESH)
    pl.semaphore_wait(barrier, len(peers))


def _sequencer_mesh():
    return plsc.ScalarSubcoreMesh(axis_name="sequencer", num_cores=1)


GATHER_CHUNKS = 4
BF16_ROWS = 16


def _row_chunks(n, k):
    tiles = n // BF16_ROWS
    sizes = [(tiles // k + (1 if i < tiles % k else 0)) * BF16_ROWS for i in range(k)]
    return [(sum(sizes[:i]), sz) for i, sz in enumerate(sizes) if sz]


def _allgather_weight(name, collective_id, shard, after=None):
    n = shard.shape[0]
    assert n % BF16_ROWS == 0
    chunks = _row_chunks(n, GATHER_CHUNKS)
    nc = len(chunks)

    def body(*refs):
        src, out = refs[0], refs[-4]
        send_sems, recv_sems, local_sem = refs[-3:]
        x, y, c = _place()
        me, sib, xn, yn, diag = (x, y, c), (x, y, 1 - c), (1 - x, y, c), (x, 1 - y, c), (1 - x, 1 - y, c)
        relay_to = (x ^ c, y ^ (1 - c), c)
        relay_of = (x ^ (1 - c), y ^ c, c)
        _handshake([sib, xn, yn])

        def rows(place, ci):
            px, py, pc = place
            off, size = chunks[ci]
            return out.at[pl.ds(pl.multiple_of((4 * px + 2 * py + pc) * n + off, BF16_ROWS), size), :]

        def copy(k, ci, block, to, from_src=False):
            off, size = chunks[ci]
            return pltpu.make_async_remote_copy(
                src_ref=src.at[pl.ds(off, size), :] if from_src else rows(block, ci), dst_ref=rows(block, ci),
                send_sem=send_sems.at[ci, k], recv_sem=recv_sems.at[ci, k], device_id=to, device_id_type=MESH)

        mine = pltpu.make_async_copy(src, out.at[pl.ds(pl.multiple_of((4 * x + 2 * y + c) * n, BF16_ROWS), n), :], local_sem)
        mine.start()
        sent = []
        for ci in range(nc):
            sent += [copy(0, ci, me, sib, from_src=True), copy(1, ci, me, xn, from_src=True), copy(2, ci, me, yn, from_src=True)]
        for cp in sent:
            cp.start()
        for ci in range(nc):
            copy(1, ci, xn, me).wait_recv()
            copy(2, ci, yn, me).wait_recv()
            passed = [copy(3, ci, relay_of, relay_to), copy(4, ci, xn, sib), copy(5, ci, yn, sib)]
            for cp in passed:
                cp.start()
            sent += passed
        for ci in range(nc):
            copy(3, ci, diag, me).wait_recv()
            passed = copy(6, ci, diag, sib)
            passed.start()
            sent.append(passed)
        for ci in range(nc):
            copy(0, ci, sib, me).wait_recv()
            for k, block in ((4, (1 - x, y, 1 - c)), (5, (x, 1 - y, 1 - c)), (6, (1 - x, 1 - y, 1 - c))):
                copy(k, ci, block, me).wait_recv()
        for cp in sent:
            cp.wait_send()
        mine.wait()

    return pl.kernel(
        body, name=name,
        out_type=jax.ShapeDtypeStruct((N_DEV * n, shard.shape[1]), shard.dtype),
        mesh=_sequencer_mesh(),
        scratch_types=[pltpu.SemaphoreType.DMA((nc, 7)), pltpu.SemaphoreType.DMA((nc, 7)), pltpu.SemaphoreType.DMA],
        compiler_params=pltpu.CompilerParams(collective_id=collective_id),
    )(*([shard] if after is None else [shard, after]))


_FLIPS = [(0, 0, 1), (1, 0, 0), (0, 1, 0), (1, 1, 0), (1, 0, 1), (0, 1, 1), (1, 1, 1)]


def _scatter_grad(name, collective_id, grad):
    n = grad.shape[0] // N_DEV

    def body(src, out, send_sems, recv_sems, local_sem):
        x, y, c = _place()
        me_idx = 4 * x + 2 * y + c
        peers = [(x ^ fx, y ^ fy, c ^ fc) for (fx, fy, fc) in _FLIPS]
        _handshake(peers)

        def block(idx):
            return src.at[pl.ds(pl.multiple_of(idx * n, 16), n), :]

        copies = [pltpu.make_async_remote_copy(
            src_ref=block(4 * px + 2 * py + pc), dst_ref=out.at[me_idx], send_sem=send_sems.at[k], recv_sem=recv_sems.at[k],
            device_id=(px, py, pc), device_id_type=MESH) for k, (px, py, pc) in enumerate(peers)]
        mine = pltpu.make_async_copy(block(me_idx), out.at[me_idx], local_sem)
        mine.start()
        for cp in copies:
            cp.start()
        for cp in copies:
            cp.wait_recv()
        for cp in copies:
            cp.wait_send()
        mine.wait()

    return pl.kernel(
        body, name=name,
        out_type=jax.ShapeDtypeStruct((N_DEV, n, grad.shape[1]), grad.dtype),
        mesh=_sequencer_mesh(),
        scratch_types=[pltpu.SemaphoreType.DMA((7,)), pltpu.SemaphoreType.DMA((7,)), pltpu.SemaphoreType.DMA],
        compiler_params=pltpu.CompilerParams(collective_id=collective_id),
    )(grad)


def _allgather_rows(name, collective_id, part):
    def body(src, out, send_sems, recv_sems, local_sem):
        x, y, c = _place()
        me_idx = 4 * x + 2 * y + c
        peers = [(x ^ fx, y ^ fy, c ^ fc) for (fx, fy, fc) in _FLIPS]
        _handshake(peers)
        copies = [pltpu.make_async_remote_copy(
            src_ref=src, dst_ref=out.at[me_idx], send_sem=send_sems.at[k], recv_sem=recv_sems.at[k],
            device_id=peer, device_id_type=MESH) for k, peer in enumerate(peers)]
        mine = pltpu.make_async_copy(src, out.at[me_idx], local_sem)
        mine.start()
        for cp in copies:
            cp.start()
        for cp in copies:
            cp.wait_recv()
        for cp in copies:
            cp.wait_send()
        mine.wait()

    return pl.kernel(
        body, name=name,
        out_type=jax.ShapeDtypeStruct((N_DEV,) + part.shape, part.dtype),
        mesh=_sequencer_mesh(),
        scratch_types=[pltpu.SemaphoreType.DMA((7,)), pltpu.SemaphoreType.DMA((7,)), pltpu.SemaphoreType.DMA],
        compiler_params=pltpu.CompilerParams(collective_id=collective_id),
    )(part)


def _adamw_math(w, g, m, v):
    m = ADAM_B1 * m + (1.0 - ADAM_B1) * g
    v = ADAM_B2 * v + (1.0 - ADAM_B2) * (g * g)
    m_hat = m / (1.0 - ADAM_B1 ** ADAM_STEP)
    v_hat = v / (1.0 - ADAM_B2 ** ADAM_STEP)
    delta = -ADAM_LR * (m_hat / (jnp.sqrt(v_hat) + ADAM_EPS) + ADAM_WD * w)
    return delta, m, v


def _sum_adamw(name, slots, w, m, v, after):
    _, n, kk = slots.shape
    tr = _pick(n, (208, 176, 128, 96, 64, 32, 16))

    def body(s_ref, w_ref, m_ref, v_ref, after_ref, g_ref, d_ref, nm_ref, nv_ref):
        del after_ref
        g = s_ref[0].astype(F32)
        for p in range(1, N_DEV):
            g = g + s_ref[p].astype(F32)
        g_ref[...] = g
        d_ref[...], nm_ref[...], nv_ref[...] = _adamw_math(w_ref[...], g, m_ref[...], v_ref[...])

    row = pl.BlockSpec((tr, kk), lambda i: (i, 0))
    return pl.pallas_call(
        body, name=name, grid=(n // tr,),
        in_specs=[pl.BlockSpec((N_DEV, tr, kk), lambda i: (0, i, 0)), row, row, row, ANY],
        out_specs=[row] * 4,
        out_shape=[jax.ShapeDtypeStruct((n, kk), F32)] * 4,
        compiler_params=_cparams(("parallel",), 48),
    )(slots, w, m, v, after)


def _allreduce_small_adamw(early_slots, late, w, m, v, after):
    ra, rb = early_slots.shape[1], late.shape[0]

    def body(early_ref, late_ref, w_ref, m_ref, v_ref, after_ref, g_ref, d_ref, nm_ref, nv_ref, slots, send_sems, recv_sems):
        del after_ref
        x, y, c = _place()
        me_idx = 4 * x + 2 * y + c
        copies = []
        for k, (fx, fy, fc) in enumerate(_FLIPS):
            px, py, pc = x ^ fx, y ^ fy, c ^ fc
            copies.append(pltpu.make_async_remote_copy(
                src_ref=late_ref, dst_ref=slots.at[me_idx], send_sem=send_sems.at[k], recv_sem=recv_sems.at[k],
                device_id=(px, py, pc), device_id_type=MESH))
        for cp in copies:
            cp.start()
        slots[me_idx] = late_ref[...]
        g = early_ref[0]
        for p in range(1, N_DEV):
            g = g + early_ref[p]
        early = pl.ds(0, ra)
        g_ref[early, :] = g
        d_ref[early, :], nm_ref[early, :], nv_ref[early, :] = _adamw_math(w_ref[early, :], g, m_ref[early, :], v_ref[early, :])
        for cp in copies:
            cp.wait_recv()
        for cp in copies:
            cp.wait_send()
        g = slots[0]
        for p in range(1, N_DEV):
            g = g + slots[p]
        tail = pl.ds(ra, rb)
        g_ref[tail, :] = g
        d_ref[tail, :], nm_ref[tail, :], nv_ref[tail, :] = _adamw_math(w_ref[tail, :], g, m_ref[tail, :], v_ref[tail, :])

    vm = pl.BlockSpec(memory_space=pltpu.VMEM)
    return pl.pallas_call(
        body, name="allreduce_small_adamw",
        in_specs=[vm] * 5 + [ANY], out_specs=[vm] * 4,
        out_shape=[jax.ShapeDtypeStruct((ra + rb, LANES), F32)] * 4,
        scratch_shapes=[pltpu.VMEM((N_DEV, rb, LANES), F32), pltpu.SemaphoreType.DMA((7,)), pltpu.SemaphoreType.DMA((7,))],
        compiler_params=pltpu.CompilerParams(vmem_limit_bytes=48 * MIB),
    )(early_slots, late, w, m, v, after)


def _pack(arrs):
    parts, meta, off = [], [], 0
    for a in arrs:
        flat = a.reshape(-1).astype(F32)
        rows = -(-flat.shape[0] // LANES)
        rows8 = -(-rows // 8) * 8
        flat = jnp.pad(flat, (0, rows8 * LANES - flat.shape[0]))
        parts.append(flat.reshape(rows8, LANES))
        meta.append((off, a.shape, a.size))
        off += rows8
    return jnp.concatenate(parts, axis=0), meta


def _unpack(packed, meta):
    outs = []
    for off, shape, size in meta:
        rows = -(-size // LANES)
        outs.append(packed[off:off + rows].reshape(-1)[:size].reshape(shape))
    return outs


def _silu_parts(a):
    sg = 0.5 + 0.5 * jnp.tanh(0.5 * a)
    return a * sg, sg * (1.0 + a * (1.0 - sg))


def kernel(x, norm1_g, w_in, q_norm_g, k_norm_g, attn_sinks, gate_ln_g, gate_ln_b, w_spatial, b_spatial, out_norm_attn_g, out_norm_gate_g, w_out, norm2_g, w_ffn_gate, w_ffn_up, w_ffn_down, loss_target, m_norm1_g, m_w_in, m_q_norm_g, m_k_norm_g, m_attn_sinks, m_gate_ln_g, m_gate_ln_b, m_w_spatial, m_b_spatial, m_out_norm_attn_g, m_out_norm_gate_g, m_w_out, m_norm2_g, m_w_ffn_gate, m_w_ffn_up, m_w_ffn_down, v_norm1_g, v_w_in, v_q_norm_g, v_k_norm_g, v_attn_sinks, v_gate_ln_g, v_gate_ln_b, v_w_spatial, v_b_spatial, v_out_norm_attn_g, v_out_norm_gate_g, v_w_out, v_norm2_g, v_w_ffn_gate, v_w_ffn_up, v_w_ffn_down):
    nseq, seq, d = x.shape
    t = nseq * seq
    nb = seq // BLOCK
    inw = w_in.shape[2] * N_DEV
    dm = _Dims(d, inw, q_norm_g.shape[-1])
    xf = x.reshape(t, d)
    tgt = loss_target.reshape(t, d)

    rows = lambda wv, transposed: jnp.swapaxes(wv, 1, 2)[0] if transposed else wv[0]
    big = {"w_in": (w_in, m_w_in, v_w_in, True), "w_out": (w_out, m_w_out, v_w_out, False),
           "w_ffn_gate": (w_ffn_gate, m_w_ffn_gate, v_w_ffn_gate, True), "w_ffn_up": (w_ffn_up, m_w_ffn_up, v_w_ffn_up, True),
           "w_ffn_down": (w_ffn_down, m_w_ffn_down, v_w_ffn_down, False)}
    big_rows = {nm: tuple(rows(arr, tr) for arr in (wv, mv, vv)) for nm, (wv, mv, vv, tr) in big.items()}
    shard = lambda nm: big_rows[nm][0].astype(WIRE)
    win_t = _allgather_weight("gather_w_in", 1, shard("w_in"))
    wout = _allgather_weight("gather_w_out", 2, shard("w_out"), after=win_t)
    wg_t = _allgather_weight("gather_w_ffn_gate", 3, shard("w_ffn_gate"), after=win_t)
    wu_t = _allgather_weight("gather_w_ffn_up", 9, shard("w_ffn_up"), after=win_t)
    wd = _allgather_weight("gather_w_ffn_down", 10, shard("w_ffn_down"), after=win_t)

    lanes = lambda v, n=BLOCK: jnp.broadcast_to(v.reshape(-1, 1), (v.size, n))
    prm = (lanes(q_norm_g, dm.grp * BLOCK), lanes(k_norm_g, 2 * BLOCK), attn_sinks[0], lanes(gate_ln_g), lanes(gate_ln_b), w_spatial[0], b_spatial[0],
           lanes(out_norm_attn_g), lanes(out_norm_gate_g))

    h1 = _rms_fwd("rms1_fwd", xf, norm1_g)
    (proj_t,) = _matmul("mm_in", win_t, h1, "nt", [F32])
    y_t = _mixer_fwd(proj_t, prm, dm, nseq, nb)

    def residual_norm(acc, xr, g2):
        x2v = xr + acc
        return x2v, x2v * lax.rsqrt(jnp.mean(x2v * x2v, axis=-1, keepdims=True) + EPS) * g2

    x2, h2 = _matmul("mm_out", y_t, wout, "tn", [F32, MXU], epilogue=residual_norm, extras=[xf], rowvecs=[norm2_g], full_rows=True)
    a, b, s = _matmul("mm_gate_up", h2, wg_t, "nt", [MXU, MXU, MXU], b2=wu_t, epilogue=lambda ga, ub: (ga, ub, _silu_parts(ga)[0] * ub))

    def loss_epilogue(acc, x2v, tv):
        diff = (x2v + acc) - tv
        dx3 = diff * (1.0 / d)
        return dx3, dx3, jnp.sum(diff * diff)

    dx3, dx3b, lossp = _matmul("mm_down", s, wd, "nn", [F32, MXU], epilogue=loss_epilogue, extras=[x2, tgt], partial=True)
    loss_part = (0.5 / d) * jnp.sum(lossp[::8, ::LANES])

    def dswiglu(acc, av, bv):
        silu, dsilu = _silu_parts(av.astype(F32))
        return acc * bv.astype(F32) * dsilu, acc * silu

    (g_wd,) = _matmul("mm_gw_down", s, dx3b, "tn", [WIRE])
    sl_wd = _scatter_grad("scatter_w_ffn_down", 4, g_wd)
    da, db = _matmul("mm_d_down", dx3b, wd, "nt", [MXU, MXU], epilogue=dswiglu, extras=[a, b], after=[g_wd])
    (g_wg,) = _matmul("mm_gw_gate", da, h2, "tn", [WIRE])
    sl_wg = _scatter_grad("scatter_w_ffn_gate", 5, g_wg)
    (g_wu,) = _matmul("mm_gw_up", db, h2, "tn", [WIRE], after=[g_wg])
    sl_wu = _scatter_grad("scatter_w_ffn_up", 6, g_wu)
    (dh2a,) = _matmul("mm_dh2_gate", da, wg_t, "nn", [F32], after=[g_wu])
    (dh2,) = _matmul("mm_dh2_up", db, wu_t, "nn", [F32], epilogue=lambda acc, pv: (pv + acc,), extras=[dh2a])

    dy_t, dx2, dx2b, dg2 = _norm_bwd_matmul("mm_d_out", wout, dh2, x2, norm2_g, dx3, after=dh2a)
    (g_wout,) = _matmul("mm_gw_out", y_t, dx2b, "nn", [WIRE], after=[dy_t])
    sl_wout = _scatter_grad("scatter_w_out", 7, g_wout)
    (dproj0, dkv, dgq, dgk, dsink, dlng, dlnb, dws, dbs, dgoa, dgog) = _mixer_bwd(proj_t, dy_t, prm, dm, nseq, nb)
    early_g = [dgq, dgk, dsink, dlng, dlnb, dws, dbs, dgoa, dgog, dg2, loss_part.reshape(1)]
    early_slots = _allgather_rows("gather_small_grads", 11, _pack(early_g)[0])
    dproj_t = _patch_kv(dproj0, dkv, dm)
    (g_win,) = _matmul("mm_gw_in", dproj_t, h1, "nn", [WIRE])
    sl_win = _scatter_grad("scatter_w_in", 8, g_win)

    def norm1_backward(dh1, xv, dx2v, g1):
        r = lax.rsqrt(jnp.mean(xv * xv, axis=-1, keepdims=True) + EPS)
        xh = xv * r
        dxh = dh1 * g1
        return dx2v + r * (dxh - xh * jnp.mean(dxh * xh, axis=-1, keepdims=True)), jnp.sum(dh1 * xh, axis=0, keepdims=True)

    dx, dg1 = _matmul("mm_d_in", dproj_t, win_t, "tn", [F32], epilogue=norm1_backward, extras=[xf, dx2], rowvecs=[norm1_g],
                      after=[g_win], col_sum=True, full_rows=True)

    big_out = {}
    last = dx

    def big_update(nm, sl, after):
        res = _sum_adamw("adamw_" + nm, sl, *big_rows[nm], after=after)
        big_out[nm] = tuple(jnp.swapaxes(r[None], 1, 2) if big[nm][3] else r[None] for r in res)
        return res[1]

    for nm, sl in (("w_ffn_down", sl_wd), ("w_ffn_gate", sl_wg), ("w_ffn_up", sl_wu), ("w_out", sl_wout)):
        last = big_update(nm, sl, last)

    zero = jnp.zeros((1,), F32)
    small_names = ["q_norm_g", "k_norm_g", "attn_sinks", "gate_ln_g", "gate_ln_b", "w_spatial", "b_spatial",
                   "out_norm_attn_g", "out_norm_gate_g", "norm2_g", "loss", "norm1_g"]
    small_w = [q_norm_g, k_norm_g, attn_sinks, gate_ln_g, gate_ln_b, w_spatial, b_spatial, out_norm_attn_g, out_norm_gate_g, norm2_g, zero, norm1_g]
    small_m = [m_q_norm_g, m_k_norm_g, m_attn_sinks, m_gate_ln_g, m_gate_ln_b, m_w_spatial, m_b_spatial, m_out_norm_attn_g, m_out_norm_gate_g, m_norm2_g, zero, m_norm1_g]
    small_v = [v_q_norm_g, v_k_norm_g, v_attn_sinks, v_gate_ln_g, v_gate_ln_b, v_w_spatial, v_b_spatial, v_out_norm_attn_g, v_out_norm_gate_g, v_norm2_g, zero, v_norm1_g]
    pw, meta = _pack(small_w)
    sg, sd, sm, sv = _allreduce_small_adamw(early_slots, _pack([dg1])[0], pw, _pack(small_m)[0], _pack(small_v)[0], after=last)
    big_update("w_in", sl_win, sd)
    ug, ud, um, uv = _unpack(sg, meta), _unpack(sd, meta), _unpack(sm, meta), _unpack(sv, meta)
    small_out = {nm: (ug[k], ud[k], um[k], uv[k]) for k, nm in enumerate(small_names)}
    loss = small_out["loss"][0].reshape(())

    order = ["norm1_g", "w_in", "q_norm_g", "k_norm_g", "attn_sinks", "gate_ln_g", "gate_ln_b", "w_spatial", "b_spatial",
             "out_norm_attn_g", "out_norm_gate_g", "w_out", "norm2_g", "w_ffn_gate", "w_ffn_up", "w_ffn_down"]
    allo = {**big_out, **small_out}
    outs = [loss, dx.reshape(nseq, seq, d)]
    for k in range(4):
        outs += [allo[nm][k] for nm in order]
    return tuple(outs)
```

```python
import math

import jax
import jax.numpy as jnp
from jax import lax
from jax.experimental import pallas as pl
from jax.experimental.pallas import tpu as pltpu
from jax.experimental.pallas import tpu_sc as plsc

F32 = jnp.float32
MXU = jnp.bfloat16
WIRE = jnp.bfloat16
EPS = 1e-6
BLOCK = 128
GROUP_DIM = 128
N_KV_HEADS = 2
NEG = -1e30
N_DEV = 8
LANES = 128
MIB = 1024 * 1024

ADAM_LR = 0.001
ADAM_B1 = 0.9
ADAM_B2 = 0.999
ADAM_EPS = 1e-08
ADAM_WD = 0.01
ADAM_STEP = 10

MESH = pl.DeviceIdType.MESH
ANY = pl.BlockSpec(memory_space=pl.ANY)


def _pick(n, cands):
    for c in cands:
        if n % c == 0:
            return c
    return n


def _cparams(sem, vmem_mb):
    return pltpu.CompilerParams(dimension_semantics=sem, vmem_limit_bytes=vmem_mb * MIB)


VMEM_TILE_BUDGET = 44 * MIB
HBM_BYTES_PER_US = 3.0e6
STEP_US = 0.4
MIN_TILE_N = 512


def _tile_candidates(n):
    return [c for c in range(min(n, 2048), 0, -LANES) if n % c == 0 and c % LANES == 0] or [n]


def _matmul_tiles(m, n, kk, esz, n_b, extra_sizes, out_sizes, full_rows):
    best = None
    wide = [n] if full_rows else [c for c in _tile_candidates(n) if c >= MIN_TILE_N] or _tile_candidates(n)
    for tm in _tile_candidates(m):
        for tn in wide:
            vmem = 2 * (tm + n_b * tn) * kk * esz + tm * tn * (4 * n_b + 2 * sum(extra_sizes) + 2 * sum(out_sizes))
            if vmem > VMEM_TILE_BUDGET:
                continue
            cost = (m // tm) * n_b * n * kk * esz / HBM_BYTES_PER_US + (m // tm) * (n // tn) * STEP_US
            if best is None or cost < best[0]:
                best = (cost, tm, tn, vmem)
    assert best is not None, (m, n, kk)
    return best[1:]


def _matmul(name, a, b, mode, out_dtypes, epilogue=None, extras=(), rowvecs=(), after=(), partial=False, col_sum=False, full_rows=False, b2=None):
    if mode == "nn":
        (m, kk), n = a.shape, b.shape[1]
        dn = (((1,), (0,)), ((), ()))
    elif mode == "nt":
        (m, kk), n = a.shape, b.shape[0]
        dn = (((1,), (1,)), ((), ()))
    else:
        (kk, m), n = a.shape, b.shape[1]
        dn = (((0,), (0,)), ((), ()))
    bs = [b] if b2 is None else [b, b2]
    tm, tn, vmem = _matmul_tiles(m, n, kk, a.dtype.itemsize, len(bs), [e.dtype.itemsize for e in extras],
                                 [jnp.dtype(dt).itemsize for dt in out_dtypes], full_rows)
    a_spec = pl.BlockSpec((kk, tm), lambda i, j: (0, i)) if mode == "tn" else pl.BlockSpec((tm, kk), lambda i, j: (i, 0))
    b_spec = pl.BlockSpec((tn, kk), lambda i, j: (j, 0)) if mode == "nt" else pl.BlockSpec((kk, tn), lambda i, j: (0, j))
    tile = pl.BlockSpec((tm, tn), lambda i, j: (i, j))
    row = pl.BlockSpec((1, tn), lambda i, j: (0, j))
    nb, ne, nr, na, no = len(bs), len(extras), len(rowvecs), len(after), len(out_dtypes)

    def body(a_ref, *rest):
        b_refs, in_refs, out_refs = rest[:nb], rest[nb:nb + ne + nr], rest[nb + ne + nr + na:]
        av = a_ref[...]
        accs = [lax.dot_general(av, b_ref[...], dn, preferred_element_type=F32) for b_ref in b_refs]
        vals = tuple(accs) if epilogue is None else epilogue(*accs, *[r[...] for r in in_refs])
        for o_ref, t in zip(out_refs[:no], vals[:no]):
            o_ref[...] = t.astype(o_ref.dtype)
        if partial:
            out_refs[no][...] = jnp.full((8, LANES), vals[no], F32)
        if col_sum:
            sum_ref = out_refs[-1]

            @pl.when(pl.program_id(0) == 0)
            def _():
                sum_ref[...] = jnp.zeros_like(sum_ref)

            sum_ref[...] += vals[-1]

    out_specs = [tile] * no
    out_shape = [jax.ShapeDtypeStruct((m, n), dt) for dt in out_dtypes]
    if partial:
        out_specs.append(pl.BlockSpec((8, LANES), lambda i, j: (i, j)))
        out_shape.append(jax.ShapeDtypeStruct((m // tm * 8, n // tn * LANES), F32))
    if col_sum:
        out_specs.append(row)
        out_shape.append(jax.ShapeDtypeStruct((1, n), F32))
    return pl.pallas_call(
        body, name=name, grid=(m // tm, n // tn),
        in_specs=[a_spec] + [b_spec] * nb + [tile] * ne + [row] * nr + [ANY] * na,
        out_specs=out_specs, out_shape=out_shape,
        compiler_params=_cparams(("arbitrary" if col_sum else "parallel", "arbitrary"), min(vmem // MIB + 8, 60)),
    )(a, *bs, *extras, *rowvecs, *after)


def _rms_fwd(name, x, g):
    t, d = x.shape
    tm = _pick(t, (512, 256, 128))

    def body(x_ref, g_ref, h_ref):
        xv = x_ref[...]
        r = lax.rsqrt(jnp.mean(xv * xv, axis=-1, keepdims=True) + EPS)
        h_ref[...] = (xv * r * g_ref[...]).astype(h_ref.dtype)

    return pl.pallas_call(
        body, name=name, grid=(t // tm,),
        in_specs=[pl.BlockSpec((tm, d), lambda i: (i, 0)), pl.BlockSpec((1, d), lambda i: (0, 0))],
        out_specs=pl.BlockSpec((tm, d), lambda i: (i, 0)),
        out_shape=jax.ShapeDtypeStruct((t, d), MXU),
        compiler_params=_cparams(("parallel",), 32),
    )(x, g)


def _norm_bwd_matmul(name, w, dh, x, g, res, after):
    t, d = x.shape
    m = w.shape[0]
    tn = _pick(t, (256, 128))

    def body(w_ref, dh_ref, x_ref, g_ref, res_ref, after_ref, out_ref, dx_ref, dxb_ref, dg_ref):
        del after_ref

        @pl.when(pl.program_id(0) == 0)
        def _():
            dg_ref[...] = jnp.zeros_like(dg_ref)

        xv, dhv = x_ref[...], dh_ref[...]
        r = lax.rsqrt(jnp.mean(xv * xv, axis=-1, keepdims=True) + EPS)
        xh = xv * r
        dg_ref[...] += jnp.sum(dhv * xh, axis=0, keepdims=True)
        dxh = dhv * g_ref[...]
        dx = res_ref[...] + r * (dxh - xh * jnp.mean(dxh * xh, axis=-1, keepdims=True))
        dx_ref[...] = dx
        dxb = dx.astype(MXU)
        dxb_ref[...] = dxb
        out_ref[...] = lax.dot_general(w_ref[...], dxb, (((1,), (1,)), ((), ())), preferred_element_type=F32)

    row = pl.BlockSpec((tn, d), lambda j: (j, 0))
    vec = pl.BlockSpec((1, d), lambda j: (0, 0))
    return pl.pallas_call(
        body, name=name, grid=(t // tn,),
        in_specs=[pl.BlockSpec((m, d), lambda j: (0, 0)), row, row, vec, row, ANY],
        out_specs=[pl.BlockSpec((m, tn), lambda j: (0, j)), row, row, vec],
        out_shape=[jax.ShapeDtypeStruct((m, t), F32), jax.ShapeDtypeStruct((t, d), F32), jax.ShapeDtypeStruct((t, d), MXU),
                   jax.ShapeDtypeStruct((1, d), F32)],
        compiler_params=_cparams(("arbitrary",), 52),
    )(w, dh, x, g, res, after)


_INV_SQRT2 = 0.7071067811865476
_INV_SQRT_2PI = 0.3989422804014327


def _dot_nt(a, b):
    return lax.dot_general(a, b, (((1,), (1,)), ((), ())), preferred_element_type=F32)


def _dot_tn(a, b):
    return lax.dot_general(a, b, (((0,), (0,)), ((), ())), preferred_element_type=F32)


def _dot(a, b):
    return jnp.dot(a, b, preferred_element_type=F32)


def _col_rms(v):
    return lax.rsqrt(jnp.mean(v * v, axis=0, keepdims=True) + EPS)


class _Dims:
    def __init__(self, d_model, in_width, head_dim):
        self.d = d_model
        self.aw = d_model // 2
        self.gw = d_model - self.aw
        self.kvw = (in_width - self.aw - 2 * self.gw) // 2
        self.hd = head_dim
        self.nh = self.aw // head_dim
        self.nkv = self.kvw // head_dim
        self.grp = self.nh // self.nkv
        self.ng = self.gw // GROUP_DIM
        self.inw = in_width
        self.zoff = self.aw + 2 * self.kvw
        assert self.nkv == N_KV_HEADS and self.zoff + 2 * self.gw == in_width and self.aw % (2 * self.kvw) == 0


def _band_masks(first):
    r = lax.broadcasted_iota(jnp.int32, (BLOCK, BLOCK), 0)
    t = lax.broadcasted_iota(jnp.int32, (BLOCK, BLOCK), 1)
    upper = r > t
    dist = t - r + jnp.where(upper, BLOCK, 0)
    return upper, jnp.logical_not(upper & first), dist.astype(F32)


def _fold(full, upper):
    return jnp.where(upper, full[:BLOCK], full[BLOCK:])


def _unfold(folded, upper):
    zero = jnp.zeros_like(folded)
    return jnp.concatenate([jnp.where(upper, folded, zero), jnp.where(upper, zero, folded)], axis=0)


def _kv_band(dm, kh, p_ref, pkv_ref, gk2):
    ko = dm.aw + kh * dm.hd
    vo = dm.aw + dm.kvw + kh * dm.hd
    k_t = jnp.concatenate([pkv_ref[kh * dm.hd:(kh + 1) * dm.hd, :], p_ref[ko:ko + dm.hd, :]], axis=1)
    v_t = jnp.concatenate([pkv_ref[dm.kvw + kh * dm.hd:dm.kvw + (kh + 1) * dm.hd, :], p_ref[vo:vo + dm.hd, :]], axis=1)
    kn_t = k_t * _col_rms(k_t) * gk2
    return kn_t.astype(MXU), kn_t.T.astype(MXU), v_t.astype(MXU), v_t.T.astype(MXU)


def _group_heads(dm, kh):
    return range(kh * dm.grp, (kh + 1) * dm.grp)


def _attn_group_fwd(dm, kh, p_ref, gq, kn, v_tb, sink_ref, masks):
    heads = _group_heads(dm, kh)
    q = jnp.concatenate([p_ref[h * dm.hd:(h + 1) * dm.hd, :] for h in heads], axis=1)
    rq = _col_rms(q)
    qh = q * rq
    qnb = (qh * gq).astype(MXU)
    upper, valid, dist = masks
    s = _dot(kn, qnb)
    probs, probs_b, sink_probs = [], [], []
    for g, h in enumerate(heads):
        slope, sink = math.pow(2.0, -8.0 * (h + 1) / dm.nh), sink_ref[h]
        logits = jnp.where(valid, _fold(s[:, g * BLOCK:(g + 1) * BLOCK], upper) * (dm.hd ** -0.5) - slope * dist, NEG)
        m = jnp.maximum(jnp.max(logits, axis=0, keepdims=True), sink)
        e = jnp.exp(logits - m)
        es = jnp.exp(sink - m)
        inv = 1.0 / (jnp.sum(e, axis=0, keepdims=True) + es)
        probs.append(e * inv)
        probs_b.append(_unfold(probs[g], upper).astype(MXU))
        sink_probs.append(es * inv)
    probs_b = jnp.concatenate(probs_b, axis=1)
    o = _dot(v_tb, probs_b)
    return o, probs, probs_b, sink_probs, rq, qh, qnb


def _gelu_cdf(z):
    return 0.5 * (1.0 + lax.erf(z * _INV_SQRT2))


def _by_group(v, ng):
    return v.reshape(ng, GROUP_DIM, v.shape[1])


def _gate_fwd(dm, p_ref, lng_ref, lnb_ref, ws_ref, bs_ref, tril):
    zu, zv = p_ref[dm.zoff:dm.zoff + dm.gw, :], p_ref[dm.zoff + dm.gw:dm.zoff + 2 * dm.gw, :]
    cu, cv = _gelu_cdf(zu), _gelu_cdf(zv)
    u, v = zu * cu, zv * cv
    v3 = _by_group(v, dm.ng)
    xc = v3 - jnp.mean(v3, axis=1, keepdims=True)
    rstd = lax.rsqrt(jnp.mean(xc * xc, axis=1, keepdims=True) + EPS)
    xh = (xc * rstd).reshape(dm.gw, BLOCK)
    vnb = (xh * lng_ref[...] + lnb_ref[...]).astype(MXU)
    wts = [jnp.where(tril, ws_ref[g], 0.0).astype(MXU) for g in range(dm.ng)]
    mixed = jnp.concatenate([_dot_nt(vnb[g * GROUP_DIM:(g + 1) * GROUP_DIM], wts[g]) + bs_ref[g:g + 1, :]
                             for g in range(dm.ng)], axis=0)
    return u * mixed, u, mixed, xh, rstd, vnb, wts, (zu, cu), (zv, cv)


def _mixer_specs(dm, nb, clamp):
    kvblk = dm.aw // (2 * dm.kvw)

    def cur(s, i):
        return (0, s * nb + clamp(i))

    def prev(s, i):
        return (kvblk, s * nb + jnp.maximum(clamp(i) - 1, 0))

    full = lambda shape: pl.BlockSpec(shape, lambda s, i: tuple(0 for _ in shape))
    return cur, prev, full


def _tril():
    return lax.broadcasted_iota(jnp.int32, (BLOCK, BLOCK), 0) >= lax.broadcasted_iota(jnp.int32, (BLOCK, BLOCK), 1)


def _mixer_fwd(proj_t, prm, dm, nseq, nb):
    gq, gk2, sinks, lng, lnb, ws, bs, goa, gog = prm
    t = proj_t.shape[1]
    cur, prev, full = _mixer_specs(dm, nb, lambda i: i)

    def body(p_ref, pkv_ref, gq_ref, gk_ref, sink_ref, lng_ref, lnb_ref, ws_ref, bs_ref, goa_ref, gog_ref, y_ref, att_scr):
        i = pl.program_id(1)
        masks = _band_masks(i == 0)
        gqv, gkv = gq_ref[...], gk_ref[...]
        for kh in range(dm.nkv):
            _, kn, v_tb, _ = _kv_band(dm, kh, p_ref, pkv_ref, gkv)
            o = _attn_group_fwd(dm, kh, p_ref, gqv, kn, v_tb, sink_ref, masks)[0]
            for g, h in enumerate(_group_heads(dm, kh)):
                att_scr[h * dm.hd:(h + 1) * dm.hd, :] = o[:, g * BLOCK:(g + 1) * BLOCK]
        att = att_scr[...]
        y_ref[:dm.aw, :] = (att * _col_rms(att) * goa_ref[...]).astype(y_ref.dtype)
        gt = _gate_fwd(dm, p_ref, lng_ref, lnb_ref, ws_ref, bs_ref, _tril())[0]
        y_ref[dm.aw:, :] = (gt * _col_rms(gt) * gog_ref[...]).astype(y_ref.dtype)

    return pl.pallas_call(
        body, name="mixer_fwd", grid=(nseq, nb),
        in_specs=[pl.BlockSpec((dm.inw, BLOCK), cur), pl.BlockSpec((2 * dm.kvw, BLOCK), prev),
                  full(gq.shape), full(gk2.shape), pl.BlockSpec(memory_space=pltpu.SMEM),
                  full(lng.shape), full(lnb.shape), full(ws.shape), full(bs.shape), full(goa.shape), full(gog.shape)],
        out_specs=pl.BlockSpec((dm.d, BLOCK), cur),
        out_shape=jax.ShapeDtypeStruct((dm.d, t), MXU),
        scratch_shapes=[pltpu.VMEM((dm.aw, BLOCK), F32)],
        compiler_params=_cparams(("parallel", "arbitrary"), 40),
    )(proj_t, proj_t, gq, gk2, sinks, lng, lnb, ws, bs, goa, gog)


def _mixer_bwd(proj_t, dy_t, prm, dm, nseq, nb):
    gq, gk2, sinks, lng, lnb, ws, bs, goa, gog = prm
    t = proj_t.shape[1]
    clamp = lambda i: jnp.minimum(i, nb - 1)
    cur, prev, full = _mixer_specs(dm, nb, clamp)
    kvw2 = 2 * dm.kvw

    def prev_kv_out(s, i):
        return (0, s * nb + jnp.maximum(i - 1, 0))

    def body(p_ref, pkv_ref, dy_ref, gq_ref, gk_ref, sink_ref, lng_ref, lnb_ref, ws_ref, bs_ref, goa_ref, gog_ref,
             dproj_ref, dkv_ref, dgq_ref, dgk_ref, dsink_ref, dlng_ref, dlnb_ref, dws_ref, dbs_ref, dgoa_ref, dgog_ref,
             att_scr, datt_scr, carry_scr, prevpart_scr, curpart_scr, kprev_scr,
             a_gq, a_gk, a_sink, a_lng, a_lnb, a_goa, a_gog):
        s_id, i = pl.program_id(0), pl.program_id(1)
        lane_accs = ((a_gq, dgq_ref), (a_gk, dgk_ref), (a_sink, dsink_ref), (a_lng, dlng_ref), (a_lnb, dlnb_ref),
                     (a_goa, dgoa_ref), (a_gog, dgog_ref))

        @pl.when((s_id == 0) & (i == 0))
        def _():
            for acc, _ in lane_accs:
                acc[...] = jnp.zeros_like(acc)
            dws_ref[...] = jnp.zeros_like(dws_ref)
            dbs_ref[...] = jnp.zeros_like(dbs_ref)

        gqv, gkv = gq_ref[...], gk_ref[...]

        @pl.when(i < nb)
        def _():
            masks = _band_masks(i == 0)
            upper = masks[0]
            kvs, fwd = [], []
            for kh in range(dm.nkv):
                kv = _kv_band(dm, kh, p_ref, pkv_ref, gkv)
                kvs.append(kv)
                fwd.append(_attn_group_fwd(dm, kh, p_ref, gqv, kv[1], kv[2], sink_ref, masks))
                for g, h in enumerate(_group_heads(dm, kh)):
                    att_scr[h * dm.hd:(h + 1) * dm.hd, :] = fwd[kh][0][:, g * BLOCK:(g + 1) * BLOCK]
            att = att_scr[...]
            dya = dy_ref[:dm.aw, :]
            ra = _col_rms(att)
            ah = att * ra
            a_goa[...] += dya * ah
            dah = dya * goa_ref[...]
            datt_scr[...] = ra * (dah - ah * jnp.mean(dah * ah, axis=0, keepdims=True))
            for kh in range(dm.nkv):
                kn_tb, kn, v_tb, vb = kvs[kh]
                _, probs, probs_b, sink_probs, rq, qh, qnb = fwd[kh]
                heads = _group_heads(dm, kh)
                do_b = jnp.concatenate([datt_scr[h * dm.hd:(h + 1) * dm.hd, :] for h in heads], axis=1).astype(MXU)
                dp = _dot(vb, do_b)
                ds = []
                for g, h in enumerate(heads):
                    p, dp_h = probs[g], _fold(dp[:, g * BLOCK:(g + 1) * BLOCK], upper)
                    delta = jnp.sum(p * dp_h, axis=0, keepdims=True)
                    ds.append(_unfold(p * (dp_h - delta) * (dm.hd ** -0.5), upper).astype(MXU))
                    a_sink[h:h + 1, :] += -(sink_probs[g] * delta)
                dsb = jnp.concatenate(ds, axis=1)
                dqn = _dot(kn_tb, dsb)
                dkn = _dot_nt(qnb, dsb)
                dvb = _dot_nt(do_b, probs_b)
                a_gq[...] += dqn * qh
                dqh = dqn * gqv
                dq = rq * (dqh - qh * jnp.mean(dqh * qh, axis=0, keepdims=True))
                for g, h in enumerate(heads):
                    dproj_ref[h * dm.hd:(h + 1) * dm.hd, :] = dq[:, g * BLOCK:(g + 1) * BLOCK].astype(dproj_ref.dtype)
                krows = slice(kh * dm.hd, (kh + 1) * dm.hd)
                vrows = slice(dm.kvw + kh * dm.hd, dm.kvw + (kh + 1) * dm.hd)
                prevpart_scr[krows, :] = dkn[:, :BLOCK]
                prevpart_scr[vrows, :] = dvb[:, :BLOCK]
                curpart_scr[krows, :] = dkn[:, BLOCK:]
                curpart_scr[vrows, :] = dvb[:, BLOCK:]
            dproj_ref[dm.aw:dm.zoff, :] = jnp.zeros((kvw2, BLOCK), dproj_ref.dtype)
            tril = _tril()
            gt, u, mixed, xh, rstd, vnb, wts, (zu, cu), (zv, cv) = _gate_fwd(dm, p_ref, lng_ref, lnb_ref, ws_ref, bs_ref, tril)
            dyg = dy_ref[dm.aw:, :]
            rg = _col_rms(gt)
            gh = gt * rg
            a_gog[...] += dyg * gh
            dgh = dyg * gog_ref[...]
            dgt = rg * (dgh - gh * jnp.mean(dgh * gh, axis=0, keepdims=True))
            du = dgt * mixed
            dmix = dgt * u
            dmixb = dmix.astype(MXU)
            dbs_ref[...] += jnp.sum(_by_group(dmix, dm.ng), axis=1)
            dvn = []
            for g in range(dm.ng):
                rows = slice(g * GROUP_DIM, (g + 1) * GROUP_DIM)
                dws_ref[g] += jnp.where(tril, _dot_tn(dmixb[rows], vnb[rows]), 0.0)
                dvn.append(_dot(dmixb[rows], wts[g]))
            dvn = jnp.concatenate(dvn, axis=0)
            a_lng[...] += dvn * xh
            a_lnb[...] += dvn
            dxh3, xh3 = _by_group(dvn * lng_ref[...], dm.ng), _by_group(xh, dm.ng)
            dv = (rstd * (dxh3 - jnp.mean(dxh3, axis=1, keepdims=True) - xh3 * jnp.mean(dxh3 * xh3, axis=1, keepdims=True))).reshape(dm.gw, BLOCK)
            dgu = cu + zu * (jnp.exp(-0.5 * zu * zu) * _INV_SQRT_2PI)
            dgv = cv + zv * (jnp.exp(-0.5 * zv * zv) * _INV_SQRT_2PI)
            dproj_ref[dm.zoff:dm.zoff + dm.gw, :] = (du * dgu).astype(dproj_ref.dtype)
            dproj_ref[dm.zoff + dm.gw:, :] = (dv * dgv).astype(dproj_ref.dtype)

        @pl.when(i == nb)
        def _():
            prevpart_scr[...] = jnp.zeros_like(prevpart_scr)

        @pl.when(i >= 1)
        def _():
            tot = carry_scr[...] + prevpart_scr[...]
            for kh in range(dm.nkv):
                krows = slice(kh * dm.hd, (kh + 1) * dm.hd)
                kraw = kprev_scr[krows, :]
                rk = _col_rms(kraw)
                khat = kraw * rk
                dkn = tot[krows, :]
                a_gk[...] += dkn * khat
                dkh = dkn * gkv[:, :BLOCK]
                dk = rk * (dkh - khat * jnp.mean(dkh * khat, axis=0, keepdims=True))
                dkv_ref[krows, :] = dk.astype(dkv_ref.dtype)
            dkv_ref[dm.kvw:, :] = tot[dm.kvw:, :].astype(dkv_ref.dtype)

        @pl.when(i < nb)
        def _():
            carry_scr[...] = curpart_scr[...]
            kprev_scr[...] = p_ref[dm.aw:dm.aw + dm.kvw, :]

        @pl.when((s_id == nseq - 1) & (i == nb))
        def _():
            for acc, out in lane_accs:
                out[...] = jnp.sum(acc[...], axis=1, keepdims=True)

    col = lambda rows: jax.ShapeDtypeStruct((rows, 1), F32)
    lane = lambda rows: pltpu.VMEM((rows, LANES), F32)
    return pl.pallas_call(
        body, name="mixer_bwd", grid=(nseq, nb + 1),
        in_specs=[pl.BlockSpec((dm.inw, BLOCK), cur), pl.BlockSpec((kvw2, BLOCK), prev), pl.BlockSpec((dm.d, BLOCK), cur),
                  full(gq.shape), full(gk2.shape), pl.BlockSpec(memory_space=pltpu.SMEM),
                  full(lng.shape), full(lnb.shape), full(ws.shape), full(bs.shape), full(goa.shape), full(gog.shape)],
        out_specs=[pl.BlockSpec((dm.inw, BLOCK), cur), pl.BlockSpec((kvw2, BLOCK), prev_kv_out),
                   full((dm.hd, 1)), full((dm.hd, 1)), full((dm.nh, 1)), full((dm.gw, 1)), full((dm.gw, 1)), full(ws.shape),
                   full(bs.shape), full((dm.aw, 1)), full((dm.gw, 1))],
        out_shape=[jax.ShapeDtypeStruct((dm.inw, t), MXU), jax.ShapeDtypeStruct((kvw2, t), MXU),
                   col(dm.hd), col(dm.hd), col(dm.nh), col(dm.gw), col(dm.gw), jax.ShapeDtypeStruct(ws.shape, F32),
                   jax.ShapeDtypeStruct(bs.shape, F32), col(dm.aw), col(dm.gw)],
        scratch_shapes=[pltpu.VMEM((dm.aw, BLOCK), F32), pltpu.VMEM((dm.aw, BLOCK), F32),
                        pltpu.VMEM((kvw2, BLOCK), F32), pltpu.VMEM((kvw2, BLOCK), F32), pltpu.VMEM((kvw2, BLOCK), F32),
                        pltpu.VMEM((dm.kvw, BLOCK), F32),
                        pltpu.VMEM((dm.hd, dm.grp * BLOCK), F32), lane(dm.hd), lane(dm.nh), lane(dm.gw), lane(dm.gw), lane(dm.aw), lane(dm.gw)],
        compiler_params=_cparams(("arbitrary", "arbitrary"), 48),
    )(proj_t, proj_t, dy_t, gq, gk2, sinks, lng, lnb, ws, bs, goa, gog)


def _patch_kv(dproj_t, dkv_t, dm):
    t = dproj_t.shape[1]
    tc = _pick(t, (1024, 512, 256, 128))
    kvw2 = 2 * dm.kvw
    kvblk = dm.aw // kvw2

    def body(dproj_hbm, dkv_ref, out_ref):
        del dproj_hbm
        out_ref[...] = dkv_ref[...]

    return pl.pallas_call(
        body, name="patch_kv", grid=(t // tc,),
        in_specs=[ANY, pl.BlockSpec((kvw2, tc), lambda i: (0, i))],
        out_specs=pl.BlockSpec((kvw2, tc), lambda i: (kvblk, i)),
        out_shape=jax.ShapeDtypeStruct(dproj_t.shape, dproj_t.dtype),
        input_output_aliases={0: 0},
        compiler_params=_cparams(("parallel",), 32),
    )(dproj_t, dkv_t)


def _place():
    x, y, c = lax.axis_index("x"), lax.axis_index("y"), lax.axis_index("c")
    return x, y, c


def _handshake(peers):
    barrier = pltpu.get_barrier_semaphore()
    for p in peers:
        pl.semaphore_signal(barrier, inc=1, device_id=p, device_id_type=MESH)
    pl.semaphore_wait(barrier, len(peers))


def _sequencer_mesh():
    return plsc.ScalarSubcoreMesh(axis_name="sequencer", num_cores=1)


GATHER_CHUNKS = 4
BF16_ROWS = 16


def _row_chunks(n, k):
    tiles = n // BF16_ROWS
    sizes = [(tiles // k + (1 if i < tiles % k else 0)) * BF16_ROWS for i in range(k)]
    return [(sum(sizes[:i]), sz) for i, sz in enumerate(sizes) if sz]


def _allgather_weight(name, collective_id, shard):
    n = shard.shape[0]
    assert n % BF16_ROWS == 0
    chunks = _row_chunks(n, GATHER_CHUNKS)
    nc = len(chunks)

    def body(src, out, send_sems, recv_sems, local_sem):
        x, y, c = _place()
        me, sib, xn, yn, diag = (x, y, c), (x, y, 1 - c), (1 - x, y, c), (x, 1 - y, c), (1 - x, 1 - y, c)
        relay_to = (x ^ c, y ^ (1 - c), c)
        relay_of = (x ^ (1 - c), y ^ c, c)
        _handshake([sib, xn, yn])

        def rows(place, ci):
            px, py, pc = place
            off, size = chunks[ci]
            return out.at[pl.ds(pl.multiple_of((4 * px + 2 * py + pc) * n + off, BF16_ROWS), size), :]

        def copy(k, ci, block, to, from_src=False):
            off, size = chunks[ci]
            return pltpu.make_async_remote_copy(
                src_ref=src.at[pl.ds(off, size), :] if from_src else rows(block, ci), dst_ref=rows(block, ci),
                send_sem=send_sems.at[ci, k], recv_sem=recv_sems.at[ci, k], device_id=to, device_id_type=MESH)

        mine = pltpu.make_async_copy(src, out.at[pl.ds(pl.multiple_of((4 * x + 2 * y + c) * n, BF16_ROWS), n), :], local_sem)
        mine.start()
        sent = []
        for ci in range(nc):
            sent += [copy(0, ci, me, sib, from_src=True), copy(1, ci, me, xn, from_src=True), copy(2, ci, me, yn, from_src=True)]
        for cp in sent:
            cp.start()
        for ci in range(nc):
            copy(1, ci, xn, me).wait_recv()
            copy(2, ci, yn, me).wait_recv()
            passed = [copy(3, ci, relay_of, relay_to), copy(4, ci, xn, sib), copy(5, ci, yn, sib)]
            for cp in passed:
                cp.start()
            sent += passed
        for ci in range(nc):
            copy(3, ci, diag, me).wait_recv()
            passed = copy(6, ci, diag, sib)
            passed.start()
            sent.append(passed)
        for ci in range(nc):
            copy(0, ci, sib, me).wait_recv()
            for k, block in ((4, (1 - x, y, 1 - c)), (5, (x, 1 - y, 1 - c)), (6, (1 - x, 1 - y, 1 - c))):
                copy(k, ci, block, me).wait_recv()
        for cp in sent:
            cp.wait_send()
        mine.wait()

    return pl.kernel(
        body, name=name,
        out_type=jax.ShapeDtypeStruct((N_DEV * n, shard.shape[1]), shard.dtype),
        mesh=_sequencer_mesh(),
        scratch_types=[pltpu.SemaphoreType.DMA((nc, 7)), pltpu.SemaphoreType.DMA((nc, 7)), pltpu.SemaphoreType.DMA],
        compiler_params=pltpu.CompilerParams(collective_id=collective_id),
    )(shard)


_FLIPS = [(0, 0, 1), (1, 0, 0), (0, 1, 0), (1, 1, 0), (1, 0, 1), (0, 1, 1), (1, 1, 1)]


def _scatter_grad(name, collective_id, grad):
    n = grad.shape[0] // N_DEV

    def body(src, out, send_sems, recv_sems, local_sem):
        x, y, c = _place()
        me_idx = 4 * x + 2 * y + c
        peers = [(x ^ fx, y ^ fy, c ^ fc) for (fx, fy, fc) in _FLIPS]
        _handshake(peers)

        def block(idx):
            return src.at[pl.ds(pl.multiple_of(idx * n, 16), n), :]

        copies = [pltpu.make_async_remote_copy(
            src_ref=block(4 * px + 2 * py + pc), dst_ref=out.at[me_idx], send_sem=send_sems.at[k], recv_sem=recv_sems.at[k],
            device_id=(px, py, pc), device_id_type=MESH) for k, (px, py, pc) in enumerate(peers)]
        mine = pltpu.make_async_copy(block(me_idx), out.at[me_idx], local_sem)
        mine.start()
        for cp in copies:
            cp.start()
        for cp in copies:
            cp.wait_recv()
        for cp in copies:
            cp.wait_send()
        mine.wait()

    return pl.kernel(
        body, name=name,
        out_type=jax.ShapeDtypeStruct((N_DEV, n, grad.shape[1]), grad.dtype),
        mesh=_sequencer_mesh(),
        scratch_types=[pltpu.SemaphoreType.DMA((7,)), pltpu.SemaphoreType.DMA((7,)), pltpu.SemaphoreType.DMA],
        compiler_params=pltpu.CompilerParams(collective_id=collective_id),
    )(grad)


def _allgather_rows(name, collective_id, part):
    def body(src, out, send_sems, recv_sems, local_sem):
        x, y, c = _place()
        me_idx = 4 * x + 2 * y + c
        peers = [(x ^ fx, y ^ fy, c ^ fc) for (fx, fy, fc) in _FLIPS]
        _handshake(peers)
        copies = [pltpu.make_async_remote_copy(
            src_ref=src, dst_ref=out.at[me_idx], send_sem=send_sems.at[k], recv_sem=recv_sems.at[k],
            device_id=peer, device_id_type=MESH) for k, peer in enumerate(peers)]
        mine = pltpu.make_async_copy(src, out.at[me_idx], local_sem)
        mine.start()
        for cp in copies:
            cp.start()
        for cp in copies:
            cp.wait_recv()
        for cp in copies:
            cp.wait_send()
        mine.wait()

    return pl.kernel(
        body, name=name,
        out_type=jax.ShapeDtypeStruct((N_DEV,) + part.shape, part.dtype),
        mesh=_sequencer_mesh(),
        scratch_types=[pltpu.SemaphoreType.DMA((7,)), pltpu.SemaphoreType.DMA((7,)), pltpu.SemaphoreType.DMA],
        compiler_params=pltpu.CompilerParams(collective_id=collective_id),
    )(part)


def _adamw_math(w, g, m, v):
    m = ADAM_B1 * m + (1.0 - ADAM_B1) * g
    v = ADAM_B2 * v + (1.0 - ADAM_B2) * (g * g)
    m_hat = m / (1.0 - ADAM_B1 ** ADAM_STEP)
    v_hat = v / (1.0 - ADAM_B2 ** ADAM_STEP)
    delta = -ADAM_LR * (m_hat / (jnp.sqrt(v_hat) + ADAM_EPS) + ADAM_WD * w)
    return delta, m, v


def _sum_adamw(name, slots, w, m, v, after):
    _, n, kk = slots.shape
    tr = _pick(n, (208, 176, 128, 96, 64, 32, 16))

    def body(s_ref, w_ref, m_ref, v_ref, after_ref, g_ref, d_ref, nm_ref, nv_ref):
        del after_ref
        g = s_ref[0].astype(F32)
        for p in range(1, N_DEV):
            g = g + s_ref[p].astype(F32)
        g_ref[...] = g
        d_ref[...], nm_ref[...], nv_ref[...] = _adamw_math(w_ref[...], g, m_ref[...], v_ref[...])

    row = pl.BlockSpec((tr, kk), lambda i: (i, 0))
    return pl.pallas_call(
        body, name=name, grid=(n // tr,),
        in_specs=[pl.BlockSpec((N_DEV, tr, kk), lambda i: (0, i, 0)), row, row, row, ANY],
        out_specs=[row] * 4,
        out_shape=[jax.ShapeDtypeStruct((n, kk), F32)] * 4,
        compiler_params=_cparams(("parallel",), 48),
    )(slots, w, m, v, after)


def _allreduce_small_adamw(early_slots, late, w, m, v, after):
    ra, rb = early_slots.shape[1], late.shape[0]

    def body(early_ref, late_ref, w_ref, m_ref, v_ref, after_ref, g_ref, d_ref, nm_ref, nv_ref, slots, send_sems, recv_sems):
        del after_ref
        x, y, c = _place()
        me_idx = 4 * x + 2 * y + c
        copies = []
        for k, (fx, fy, fc) in enumerate(_FLIPS):
            px, py, pc = x ^ fx, y ^ fy, c ^ fc
            copies.append(pltpu.make_async_remote_copy(
                src_ref=late_ref, dst_ref=slots.at[me_idx], send_sem=send_sems.at[k], recv_sem=recv_sems.at[k],
                device_id=(px, py, pc), device_id_type=MESH))
        for cp in copies:
            cp.start()
        slots[me_idx] = late_ref[...]
        g = early_ref[0]
        for p in range(1, N_DEV):
            g = g + early_ref[p]
        early = pl.ds(0, ra)
        g_ref[early, :] = g
        d_ref[early, :], nm_ref[early, :], nv_ref[early, :] = _adamw_math(w_ref[early, :], g, m_ref[early, :], v_ref[early, :])
        for cp in copies:
            cp.wait_recv()
        for cp in copies:
            cp.wait_send()
        g = slots[0]
        for p in range(1, N_DEV):
            g = g + slots[p]
        tail = pl.ds(ra, rb)
        g_ref[tail, :] = g
        d_ref[tail, :], nm_ref[tail, :], nv_ref[tail, :] = _adamw_math(w_ref[tail, :], g, m_ref[tail, :], v_ref[tail, :])

    vm = pl.BlockSpec(memory_space=pltpu.VMEM)
    return pl.pallas_call(
        body, name="allreduce_small_adamw",
        in_specs=[vm] * 5 + [ANY], out_specs=[vm] * 4,
        out_shape=[jax.ShapeDtypeStruct((ra + rb, LANES), F32)] * 4,
        scratch_shapes=[pltpu.VMEM((N_DEV, rb, LANES), F32), pltpu.SemaphoreType.DMA((7,)), pltpu.SemaphoreType.DMA((7,))],
        compiler_params=pltpu.CompilerParams(vmem_limit_bytes=48 * MIB),
    )(early_slots, late, w, m, v, after)


def _pack(arrs):
    parts, meta, off = [], [], 0
    for a in arrs:
        flat = a.reshape(-1).astype(F32)
        rows = -(-flat.shape[0] // LANES)
        rows8 = -(-rows // 8) * 8
        flat = jnp.pad(flat, (0, rows8 * LANES - flat.shape[0]))
        parts.append(flat.reshape(rows8, LANES))
        meta.append((off, a.shape, a.size))
        off += rows8
    return jnp.concatenate(parts, axis=0), meta


def _unpack(packed, meta):
    outs = []
    for off, shape, size in meta:
        rows = -(-size // LANES)
        outs.append(packed[off:off + rows].reshape(-1)[:size].reshape(shape))
    return outs


def _silu_parts(a):
    sg = 0.5 + 0.5 * jnp.tanh(0.5 * a)
    return a * sg, sg * (1.0 + a * (1.0 - sg))


def kernel(x, norm1_g, w_in, q_norm_g, k_norm_g, attn_sinks, gate_ln_g, gate_ln_b, w_spatial, b_spatial, out_norm_attn_g, out_norm_gate_g, w_out, norm2_g, w_ffn_gate, w_ffn_up, w_ffn_down, loss_target, m_norm1_g, m_w_in, m_q_norm_g, m_k_norm_g, m_attn_sinks, m_gate_ln_g, m_gate_ln_b, m_w_spatial, m_b_spatial, m_out_norm_attn_g, m_out_norm_gate_g, m_w_out, m_norm2_g, m_w_ffn_gate, m_w_ffn_up, m_w_ffn_down, v_norm1_g, v_w_in, v_q_norm_g, v_k_norm_g, v_attn_sinks, v_gate_ln_g, v_gate_ln_b, v_w_spatial, v_b_spatial, v_out_norm_attn_g, v_out_norm_gate_g, v_w_out, v_norm2_g, v_w_ffn_gate, v_w_ffn_up, v_w_ffn_down):
    nseq, seq, d = x.shape
    t = nseq * seq
    nb = seq // BLOCK
    inw = w_in.shape[2] * N_DEV
    dm = _Dims(d, inw, q_norm_g.shape[-1])
    xf = x.reshape(t, d)
    tgt = loss_target.reshape(t, d)

    rows = lambda wv, transposed: jnp.swapaxes(wv, 1, 2)[0] if transposed else wv[0]
    big = {"w_in": (w_in, m_w_in, v_w_in, True), "w_out": (w_out, m_w_out, v_w_out, False),
           "w_ffn_gate": (w_ffn_gate, m_w_ffn_gate, v_w_ffn_gate, True), "w_ffn_up": (w_ffn_up, m_w_ffn_up, v_w_ffn_up, True),
           "w_ffn_down": (w_ffn_down, m_w_ffn_down, v_w_ffn_down, False)}
    big_rows = {nm: tuple(rows(arr, tr) for arr in (wv, mv, vv)) for nm, (wv, mv, vv, tr) in big.items()}
    shard = lambda nm: big_rows[nm][0].astype(WIRE)
    win_t = _allgather_weight("gather_w_in", 1, shard("w_in"))
    wout = _allgather_weight("gather_w_out", 2, shard("w_out"))
    wg_t = _allgather_weight("gather_w_ffn_gate", 3, shard("w_ffn_gate"))
    wu_t = _allgather_weight("gather_w_ffn_up", 9, shard("w_ffn_up"))
    wd = _allgather_weight("gather_w_ffn_down", 10, shard("w_ffn_down"))

    lanes = lambda v, n=BLOCK: jnp.broadcast_to(v.reshape(-1, 1), (v.size, n))
    prm = (lanes(q_norm_g, dm.grp * BLOCK), lanes(k_norm_g, 2 * BLOCK), attn_sinks[0], lanes(gate_ln_g), lanes(gate_ln_b), w_spatial[0], b_spatial[0],
           lanes(out_norm_attn_g), lanes(out_norm_gate_g))

    h1 = _rms_fwd("rms1_fwd", xf, norm1_g)
    (proj_t,) = _matmul("mm_in", win_t, h1, "nt", [F32])
    y_t = _mixer_fwd(proj_t, prm, dm, nseq, nb)

    def residual_norm(acc, xr, g2):
        x2v = xr + acc
        return x2v, x2v * lax.rsqrt(jnp.mean(x2v * x2v, axis=-1, keepdims=True) + EPS) * g2

    x2, h2 = _matmul("mm_out", y_t, wout, "tn", [F32, MXU], epilogue=residual_norm, extras=[xf], rowvecs=[norm2_g], full_rows=True)
    a, b, s = _matmul("mm_gate_up", h2, wg_t, "nt", [MXU, MXU, MXU], b2=wu_t, epilogue=lambda ga, ub: (ga, ub, _silu_parts(ga)[0] * ub))

    def loss_epilogue(acc, x2v, tv):
        diff = (x2v + acc) - tv
        dx3 = diff * (1.0 / d)
        return dx3, dx3, jnp.sum(diff * diff)

    dx3, dx3b, lossp = _matmul("mm_down", s, wd, "nn", [F32, MXU], epilogue=loss_epilogue, extras=[x2, tgt], partial=True)
    loss_part = (0.5 / d) * jnp.sum(lossp[::8, ::LANES])

    def dswiglu(acc, av, bv):
        silu, dsilu = _silu_parts(av.astype(F32))
        return acc * bv.astype(F32) * dsilu, acc * silu

    (g_wd,) = _matmul("mm_gw_down", s, dx3b, "tn", [WIRE])
    sl_wd = _scatter_grad("scatter_w_ffn_down", 4, g_wd)
    da, db = _matmul("mm_d_down", dx3b, wd, "nt", [MXU, MXU], epilogue=dswiglu, extras=[a, b], after=[g_wd])
    (g_wg,) = _matmul("mm_gw_gate", da, h2, "tn", [WIRE])
    sl_wg = _scatter_grad("scatter_w_ffn_gate", 5, g_wg)
    (g_wu,) = _matmul("mm_gw_up", db, h2, "tn", [WIRE], after=[g_wg])
    sl_wu = _scatter_grad("scatter_w_ffn_up", 6, g_wu)
    (dh2a,) = _matmul("mm_dh2_gate", da, wg_t, "nn", [F32], after=[g_wu])
    (dh2,) = _matmul("mm_dh2_up", db, wu_t, "nn", [F32], epilogue=lambda acc, pv: (pv + acc,), extras=[dh2a])

    dy_t, dx2, dx2b, dg2 = _norm_bwd_matmul("mm_d_out", wout, dh2, x2, norm2_g, dx3, after=dh2a)
    (g_wout,) = _matmul("mm_gw_out", y_t, dx2b, "nn", [WIRE], after=[dy_t])
    sl_wout = _scatter_grad("scatter_w_out", 7, g_wout)
    (dproj0, dkv, dgq, dgk, dsink, dlng, dlnb, dws, dbs, dgoa, dgog) = _mixer_bwd(proj_t, dy_t, prm, dm, nseq, nb)
    early_g = [dgq, dgk, dsink, dlng, dlnb, dws, dbs, dgoa, dgog, dg2, loss_part.reshape(1)]
    early_slots = _allgather_rows("gather_small_grads", 11, _pack(early_g)[0])
    dproj_t = _patch_kv(dproj0, dkv, dm)
    (g_win,) = _matmul("mm_gw_in", dproj_t, h1, "nn", [WIRE])
    sl_win = _scatter_grad("scatter_w_in", 8, g_win)

    def norm1_backward(dh1, xv, dx2v, g1):
        r = lax.rsqrt(jnp.mean(xv * xv, axis=-1, keepdims=True) + EPS)
        xh = xv * r
        dxh = dh1 * g1
        return dx2v + r * (dxh - xh * jnp.mean(dxh * xh, axis=-1, keepdims=True)), jnp.sum(dh1 * xh, axis=0, keepdims=True)

    dx, dg1 = _matmul("mm_d_in", dproj_t, win_t, "tn", [F32], epilogue=norm1_backward, extras=[xf, dx2], rowvecs=[norm1_g],
                      after=[g_win], col_sum=True, full_rows=True)

    big_out = {}
    last = dx

    def big_update(nm, sl, after):
        res = _sum_adamw("adamw_" + nm, sl, *big_rows[nm], after=after)
        big_out[nm] = tuple(jnp.swapaxes(r[None], 1, 2) if big[nm][3] else r[None] for r in res)
        return res[1]

    for nm, sl in (("w_ffn_down", sl_wd), ("w_ffn_gate", sl_wg), ("w_ffn_up", sl_wu), ("w_out", sl_wout)):
        last = big_update(nm, sl, last)

    zero = jnp.zeros((1,), F32)
    small_names = ["q_norm_g", "k_norm_g", "attn_sinks", "gate_ln_g", "gate_ln_b", "w_spatial", "b_spatial",
                   "out_norm_attn_g", "out_norm_gate_g", "norm2_g", "loss", "norm1_g"]
    small_w = [q_norm_g, k_norm_g, attn_sinks, gate_ln_g, gate_ln_b, w_spatial, b_spatial, out_norm_attn_g, out_norm_gate_g, norm2_g, zero, norm1_g]
    small_m = [m_q_norm_g, m_k_norm_g, m_attn_sinks, m_gate_ln_g, m_gate_ln_b, m_w_spatial, m_b_spatial, m_out_norm_attn_g, m_out_norm_gate_g, m_norm2_g, zero, m_norm1_g]
    small_v = [v_q_norm_g, v_k_norm_g, v_attn_sinks, v_gate_ln_g, v_gate_ln_b, v_w_spatial, v_b_spatial, v_out_norm_attn_g, v_out_norm_gate_g, v_norm2_g, zero, v_norm1_g]
    pw, meta = _pack(small_w)
    sg, sd, sm, sv = _allreduce_small_adamw(early_slots, _pack([dg1])[0], pw, _pack(small_m)[0], _pack(small_v)[0], after=last)
    big_update("w_in", sl_win, sd)
    ug, ud, um, uv = _unpack(sg, meta), _unpack(sd, meta), _unpack(sm, meta), _unpack(sv, meta)
    small_out = {nm: (ug[k], ud[k], um[k], uv[k]) for k, nm in enumerate(small_names)}
    loss = small_out["loss"][0].reshape(())

    order = ["norm1_g", "w_in", "q_norm_g", "k_norm_g", "attn_sinks", "gate_ln_g", "gate_ln_b", "w_spatial", "b_spatial",
             "out_norm_attn_g", "out_norm_gate_g", "w_out", "norm2_g", "w_ffn_gate", "w_ffn_up", "w_ffn_down"]
    allo = {**big_out, **small_out}
    outs = [loss, dx.reshape(nseq, seq, d)]
    for k in range(4):
        outs += [allo[nm][k] for nm in order]
    return tuple(outs)
```

```python
import math

import jax
import jax.numpy as jnp
from jax import lax
from jax.experimental import pallas as pl
from jax.experimental.pallas import tpu as pltpu
from jax.experimental.pallas import tpu_sc as plsc

F32 = jnp.float32
MXU = jnp.bfloat16
WIRE = jnp.bfloat16
EPS = 1e-6
BLOCK = 128
GROUP_DIM = 128
N_KV_HEADS = 2
NEG = -1e30
N_DEV = 8
LANES = 128
MIB = 1024 * 1024

ADAM_LR = 0.001
ADAM_B1 = 0.9
ADAM_B2 = 0.999
ADAM_EPS = 1e-08
ADAM_WD = 0.01
ADAM_STEP = 10

MESH = pl.DeviceIdType.MESH
ANY = pl.BlockSpec(memory_space=pl.ANY)


def _pick(n, cands):
    for c in cands:
        if n % c == 0:
            return c
    return n


def _cparams(sem, vmem_mb):
    return pltpu.CompilerParams(dimension_semantics=sem, vmem_limit_bytes=vmem_mb * MIB)


VMEM_TILE_BUDGET = 44 * MIB
HBM_BYTES_PER_US = 3.0e6
STEP_US = 0.4
MIN_TILE_N = 512


def _tile_candidates(n):
    return [c for c in range(min(n, 2048), 0, -LANES) if n % c == 0 and c % LANES == 0] or [n]


def _matmul_tiles(m, n, kk, esz, n_b, extra_sizes, out_sizes, full_rows):
    best = None
    wide = [n] if full_rows else [c for c in _tile_candidates(n) if c >= MIN_TILE_N] or _tile_candidates(n)
    for tm in _tile_candidates(m):
        for tn in wide:
            vmem = 2 * (tm + n_b * tn) * kk * esz + tm * tn * (4 * n_b + 2 * sum(extra_sizes) + 2 * sum(out_sizes))
            if vmem > VMEM_TILE_BUDGET:
                continue
            cost = (m // tm) * n_b * n * kk * esz / HBM_BYTES_PER_US + (m // tm) * (n // tn) * STEP_US
            if best is None or cost < best[0]:
                best = (cost, tm, tn, vmem)
    assert best is not None, (m, n, kk)
    return best[1:]


def _matmul(name, a, b, mode, out_dtypes, epilogue=None, extras=(), rowvecs=(), after=(), partial=False, col_sum=False, full_rows=False, b2=None):
    if mode == "nn":
        (m, kk), n = a.shape, b.shape[1]
        dn = (((1,), (0,)), ((), ()))
    elif mode == "nt":
        (m, kk), n = a.shape, b.shape[0]
        dn = (((1,), (1,)), ((), ()))
    else:
        (kk, m), n = a.shape, b.shape[1]
        dn = (((0,), (0,)), ((), ()))
    bs = [b] if b2 is None else [b, b2]
    tm, tn, vmem = _matmul_tiles(m, n, kk, a.dtype.itemsize, len(bs), [e.dtype.itemsize for e in extras],
                                 [jnp.dtype(dt).itemsize for dt in out_dtypes], full_rows)
    a_spec = pl.BlockSpec((kk, tm), lambda i, j: (0, i)) if mode == "tn" else pl.BlockSpec((tm, kk), lambda i, j: (i, 0))
    b_spec = pl.BlockSpec((tn, kk), lambda i, j: (j, 0)) if mode == "nt" else pl.BlockSpec((kk, tn), lambda i, j: (0, j))
    tile = pl.BlockSpec((tm, tn), lambda i, j: (i, j))
    row = pl.BlockSpec((1, tn), lambda i, j: (0, j))
    nb, ne, nr, na, no = len(bs), len(extras), len(rowvecs), len(after), len(out_dtypes)

    def body(a_ref, *rest):
        b_refs, in_refs, out_refs = rest[:nb], rest[nb:nb + ne + nr], rest[nb + ne + nr + na:]
        av = a_ref[...]
        accs = [lax.dot_general(av, b_ref[...], dn, preferred_element_type=F32) for b_ref in b_refs]
        vals = tuple(accs) if epilogue is None else epilogue(*accs, *[r[...] for r in in_refs])
        for o_ref, t in zip(out_refs[:no], vals[:no]):
            o_ref[...] = t.astype(o_ref.dtype)
        if partial:
            out_refs[no][...] = jnp.full((8, LANES), vals[no], F32)
        if col_sum:
            sum_ref = out_refs[-1]

            @pl.when(pl.program_id(0) == 0)
            def _():
                sum_ref[...] = jnp.zeros_like(sum_ref)

            sum_ref[...] += vals[-1]

    out_specs = [tile] * no
    out_shape = [jax.ShapeDtypeStruct((m, n), dt) for dt in out_dtypes]
    if partial:
        out_specs.append(pl.BlockSpec((8, LANES), lambda i, j: (i, j)))
        out_shape.append(jax.ShapeDtypeStruct((m // tm * 8, n // tn * LANES), F32))
    if col_sum:
        out_specs.append(row)
        out_shape.append(jax.ShapeDtypeStruct((1, n), F32))
    return pl.pallas_call(
        body, name=name, grid=(m // tm, n // tn),
        in_specs=[a_spec] + [b_spec] * nb + [tile] * ne + [row] * nr + [ANY] * na,
        out_specs=out_specs, out_shape=out_shape,
        compiler_params=_cparams(("arbitrary" if col_sum else "parallel", "arbitrary"), min(vmem // MIB + 8, 60)),
    )(a, *bs, *extras, *rowvecs, *after)


def _rms_fwd(name, x, g):
    t, d = x.shape
    tm = _pick(t, (512, 256, 128))

    def body(x_ref, g_ref, h_ref):
        xv = x_ref[...]
        r = lax.rsqrt(jnp.mean(xv * xv, axis=-1, keepdims=True) + EPS)
        h_ref[...] = (xv * r * g_ref[...]).astype(h_ref.dtype)

    return pl.pallas_call(
        body, name=name, grid=(t // tm,),
        in_specs=[pl.BlockSpec((tm, d), lambda i: (i, 0)), pl.BlockSpec((1, d), lambda i: (0, 0))],
        out_specs=pl.BlockSpec((tm, d), lambda i: (i, 0)),
        out_shape=jax.ShapeDtypeStruct((t, d), MXU),
        compiler_params=_cparams(("parallel",), 32),
    )(x, g)


def _norm_bwd_matmul(name, w, dh, x, g, res, after):
    t, d = x.shape
    m = w.shape[0]
    tn = _pick(t, (256, 128))

    def body(w_ref, dh_ref, x_ref, g_ref, res_ref, after_ref, out_ref, dx_ref, dxb_ref, dg_ref):
        del after_ref

        @pl.when(pl.program_id(0) == 0)
        def _():
            dg_ref[...] = jnp.zeros_like(dg_ref)

        xv, dhv = x_ref[...], dh_ref[...]
        r = lax.rsqrt(jnp.mean(xv * xv, axis=-1, keepdims=True) + EPS)
        xh = xv * r
        dg_ref[...] += jnp.sum(dhv * xh, axis=0, keepdims=True)
        dxh = dhv * g_ref[...]
        dx = res_ref[...] + r * (dxh - xh * jnp.mean(dxh * xh, axis=-1, keepdims=True))
        dx_ref[...] = dx
        dxb = dx.astype(MXU)
        dxb_ref[...] = dxb
        out_ref[...] = lax.dot_general(w_ref[...], dxb, (((1,), (1,)), ((), ())), preferred_element_type=F32)

    row = pl.BlockSpec((tn, d), lambda j: (j, 0))
    vec = pl.BlockSpec((1, d), lambda j: (0, 0))
    return pl.pallas_call(
        body, name=name, grid=(t // tn,),
        in_specs=[pl.BlockSpec((m, d), lambda j: (0, 0)), row, row, vec, row, ANY],
        out_specs=[pl.BlockSpec((m, tn), lambda j: (0, j)), row, row, vec],
        out_shape=[jax.ShapeDtypeStruct((m, t), F32), jax.ShapeDtypeStruct((t, d), F32), jax.ShapeDtypeStruct((t, d), MXU),
                   jax.ShapeDtypeStruct((1, d), F32)],
        compiler_params=_cparams(("arbitrary",), 52),
    )(w, dh, x, g, res, after)


_INV_SQRT2 = 0.7071067811865476
_INV_SQRT_2PI = 0.3989422804014327


def _dot_nt(a, b):
    return lax.dot_general(a, b, (((1,), (1,)), ((), ())), preferred_element_type=F32)


def _dot_tn(a, b):
    return lax.dot_general(a, b, (((0,), (0,)), ((), ())), preferred_element_type=F32)


def _dot(a, b):
    return jnp.dot(a, b, preferred_element_type=F32)


def _col_rms(v):
    return lax.rsqrt(jnp.mean(v * v, axis=0, keepdims=True) + EPS)


class _Dims:
    def __init__(self, d_model, in_width, head_dim):
        self.d = d_model
        self.aw = d_model // 2
        self.gw = d_model - self.aw
        self.kvw = (in_width - self.aw - 2 * self.gw) // 2
        self.hd = head_dim
        self.nh = self.aw // head_dim
        self.nkv = self.kvw // head_dim
        self.grp = self.nh // self.nkv
        self.ng = self.gw // GROUP_DIM
        self.inw = in_width
        self.zoff = self.aw + 2 * self.kvw
        assert self.nkv == N_KV_HEADS and self.zoff + 2 * self.gw == in_width and self.aw % (2 * self.kvw) == 0


def _band_masks(first):
    r = lax.broadcasted_iota(jnp.int32, (BLOCK, BLOCK), 0)
    t = lax.broadcasted_iota(jnp.int32, (BLOCK, BLOCK), 1)
    upper = r > t
    dist = t - r + jnp.where(upper, BLOCK, 0)
    return upper, jnp.logical_not(upper & first), dist.astype(F32)


def _fold(full, upper):
    return jnp.where(upper, full[:BLOCK], full[BLOCK:])


def _unfold(folded, upper):
    zero = jnp.zeros_like(folded)
    return jnp.concatenate([jnp.where(upper, folded, zero), jnp.where(upper, zero, folded)], axis=0)


def _kv_band(dm, kh, p_ref, pkv_ref, gk2):
    ko = dm.aw + kh * dm.hd
    vo = dm.aw + dm.kvw + kh * dm.hd
    k_t = jnp.concatenate([pkv_ref[kh * dm.hd:(kh + 1) * dm.hd, :], p_ref[ko:ko + dm.hd, :]], axis=1)
    v_t = jnp.concatenate([pkv_ref[dm.kvw + kh * dm.hd:dm.kvw + (kh + 1) * dm.hd, :], p_ref[vo:vo + dm.hd, :]], axis=1)
    kn_t = k_t * _col_rms(k_t) * gk2
    return kn_t.astype(MXU), kn_t.T.astype(MXU), v_t.astype(MXU), v_t.T.astype(MXU)


def _group_heads(dm, kh):
    return range(kh * dm.grp, (kh + 1) * dm.grp)


def _attn_group_fwd(dm, kh, p_ref, gq, kn, v_tb, sink_ref, masks):
    heads = _group_heads(dm, kh)
    q = jnp.concatenate([p_ref[h * dm.hd:(h + 1) * dm.hd, :] for h in heads], axis=1)
    rq = _col_rms(q)
    qh = q * rq
    qnb = (qh * gq).astype(MXU)
    upper, valid, dist = masks
    s = _dot(kn, qnb)
    probs, probs_b, sink_probs = [], [], []
    for g, h in enumerate(heads):
        slope, sink = math.pow(2.0, -8.0 * (h + 1) / dm.nh), sink_ref[h]
        logits = jnp.where(valid, _fold(s[:, g * BLOCK:(g + 1) * BLOCK], upper) * (dm.hd ** -0.5) - slope * dist, NEG)
        m = jnp.maximum(jnp.max(logits, axis=0, keepdims=True), sink)
        e = jnp.exp(logits - m)
        es = jnp.exp(sink - m)
        inv = 1.0 / (jnp.sum(e, axis=0, keepdims=True) + es)
        probs.append(e * inv)
        probs_b.append(_unfold(probs[g], upper).astype(MXU))
        sink_probs.append(es * inv)
    probs_b = jnp.concatenate(probs_b, axis=1)
    o = _dot(v_tb, probs_b)
    return o, probs, probs_b, sink_probs, rq, qh, qnb


def _gelu_cdf(z):
    return 0.5 * (1.0 + lax.erf(z * _INV_SQRT2))


def _by_group(v, ng):
    return v.reshape(ng, GROUP_DIM, v.shape[1])


def _gate_fwd(dm, p_ref, lng_ref, lnb_ref, ws_ref, bs_ref, tril):
    zu, zv = p_ref[dm.zoff:dm.zoff + dm.gw, :], p_ref[dm.zoff + dm.gw:dm.zoff + 2 * dm.gw, :]
    cu, cv = _gelu_cdf(zu), _gelu_cdf(zv)
    u, v = zu * cu, zv * cv
    v3 = _by_group(v, dm.ng)
    xc = v3 - jnp.mean(v3, axis=1, keepdims=True)
    rstd = lax.rsqrt(jnp.mean(xc * xc, axis=1, keepdims=True) + EPS)
    xh = (xc * rstd).reshape(dm.gw, BLOCK)
    vnb = (xh * lng_ref[...] + lnb_ref[...]).astype(MXU)
    wts = [jnp.where(tril, ws_ref[g], 0.0).astype(MXU) for g in range(dm.ng)]
    mixed = jnp.concatenate([_dot_nt(vnb[g * GROUP_DIM:(g + 1) * GROUP_DIM], wts[g]) + bs_ref[g:g + 1, :]
                             for g in range(dm.ng)], axis=0)
    return u * mixed, u, mixed, xh, rstd, vnb, wts, (zu, cu), (zv, cv)


def _mixer_specs(dm, nb, clamp):
    kvblk = dm.aw // (2 * dm.kvw)

    def cur(s, i):
        return (0, s * nb + clamp(i))

    def prev(s, i):
        return (kvblk, s * nb + jnp.maximum(clamp(i) - 1, 0))

    full = lambda shape: pl.BlockSpec(shape, lambda s, i: tuple(0 for _ in shape))
    return cur, prev, full


def _tril():
    return lax.broadcasted_iota(jnp.int32, (BLOCK, BLOCK), 0) >= lax.broadcasted_iota(jnp.int32, (BLOCK, BLOCK), 1)


def _mixer_fwd(proj_t, prm, dm, nseq, nb):
    gq, gk2, sinks, lng, lnb, ws, bs, goa, gog = prm
    t = proj_t.shape[1]
    cur, prev, full = _mixer_specs(dm, nb, lambda i: i)

    def body(p_ref, pkv_ref, gq_ref, gk_ref, sink_ref, lng_ref, lnb_ref, ws_ref, bs_ref, goa_ref, gog_ref, y_ref, att_scr):
        i = pl.program_id(1)
        masks = _band_masks(i == 0)
        gqv, gkv = gq_ref[...], gk_ref[...]
        for kh in range(dm.nkv):
            _, kn, v_tb, _ = _kv_band(dm, kh, p_ref, pkv_ref, gkv)
            o = _attn_group_fwd(dm, kh, p_ref, gqv, kn, v_tb, sink_ref, masks)[0]
            for g, h in enumerate(_group_heads(dm, kh)):
                att_scr[h * dm.hd:(h + 1) * dm.hd, :] = o[:, g * BLOCK:(g + 1) * BLOCK]
        att = att_scr[...]
        y_ref[:dm.aw, :] = (att * _col_rms(att) * goa_ref[...]).astype(y_ref.dtype)
        gt = _gate_fwd(dm, p_ref, lng_ref, lnb_ref, ws_ref, bs_ref, _tril())[0]
        y_ref[dm.aw:, :] = (gt * _col_rms(gt) * gog_ref[...]).astype(y_ref.dtype)

    return pl.pallas_call(
        body, name="mixer_fwd", grid=(nseq, nb),
        in_specs=[pl.BlockSpec((dm.inw, BLOCK), cur), pl.BlockSpec((2 * dm.kvw, BLOCK), prev),
                  full(gq.shape), full(gk2.shape), pl.BlockSpec(memory_space=pltpu.SMEM),
                  full(lng.shape), full(lnb.shape), full(ws.shape), full(bs.shape), full(goa.shape), full(gog.shape)],
        out_specs=pl.BlockSpec((dm.d, BLOCK), cur),
        out_shape=jax.ShapeDtypeStruct((dm.d, t), MXU),
        scratch_shapes=[pltpu.VMEM((dm.aw, BLOCK), F32)],
        compiler_params=_cparams(("parallel", "arbitrary"), 40),
    )(proj_t, proj_t, gq, gk2, sinks, lng, lnb, ws, bs, goa, gog)


def _mixer_bwd(proj_t, dy_t, prm, dm, nseq, nb):
    gq, gk2, sinks, lng, lnb, ws, bs, goa, gog = prm
    t = proj_t.shape[1]
    clamp = lambda i: jnp.minimum(i, nb - 1)
    cur, prev, full = _mixer_specs(dm, nb, clamp)
    kvw2 = 2 * dm.kvw

    def prev_kv_out(s, i):
        return (0, s * nb + jnp.maximum(i - 1, 0))

    def body(p_ref, pkv_ref, dy_ref, gq_ref, gk_ref, sink_ref, lng_ref, lnb_ref, ws_ref, bs_ref, goa_ref, gog_ref,
             dproj_ref, dkv_ref, dgq_ref, dgk_ref, dsink_ref, dlng_ref, dlnb_ref, dws_ref, dbs_ref, dgoa_ref, dgog_ref,
             att_scr, datt_scr, carry_scr, prevpart_scr, curpart_scr, kprev_scr,
             a_gq, a_gk, a_sink, a_lng, a_lnb, a_goa, a_gog):
        s_id, i = pl.program_id(0), pl.program_id(1)
        lane_accs = ((a_gq, dgq_ref), (a_gk, dgk_ref), (a_sink, dsink_ref), (a_lng, dlng_ref), (a_lnb, dlnb_ref),
                     (a_goa, dgoa_ref), (a_gog, dgog_ref))

        @pl.when((s_id == 0) & (i == 0))
        def _():
            for acc, _ in lane_accs:
                acc[...] = jnp.zeros_like(acc)
            dws_ref[...] = jnp.zeros_like(dws_ref)
            dbs_ref[...] = jnp.zeros_like(dbs_ref)

        gqv, gkv = gq_ref[...], gk_ref[...]

        @pl.when(i < nb)
        def _():
            masks = _band_masks(i == 0)
            upper = masks[0]
            kvs, fwd = [], []
            for kh in range(dm.nkv):
                kv = _kv_band(dm, kh, p_ref, pkv_ref, gkv)
                kvs.append(kv)
                fwd.append(_attn_group_fwd(dm, kh, p_ref, gqv, kv[1], kv[2], sink_ref, masks))
                for g, h in enumerate(_group_heads(dm, kh)):
                    att_scr[h * dm.hd:(h + 1) * dm.hd, :] = fwd[kh][0][:, g * BLOCK:(g + 1) * BLOCK]
            att = att_scr[...]
            dya = dy_ref[:dm.aw, :]
            ra = _col_rms(att)
            ah = att * ra
            a_goa[...] += dya * ah
            dah = dya * goa_ref[...]
            datt_scr[...] = ra * (dah - ah * jnp.mean(dah * ah, axis=0, keepdims=True))
            for kh in range(dm.nkv):
                kn_tb, kn, v_tb, vb = kvs[kh]
                _, probs, probs_b, sink_probs, rq, qh, qnb = fwd[kh]
                heads = _group_heads(dm, kh)
                do_b = jnp.concatenate([datt_scr[h * dm.hd:(h + 1) * dm.hd, :] for h in heads], axis=1).astype(MXU)
                dp = _dot(vb, do_b)
                ds = []
                for g, h in enumerate(heads):
                    p, dp_h = probs[g], _fold(dp[:, g * BLOCK:(g + 1) * BLOCK], upper)
                    delta = jnp.sum(p * dp_h, axis=0, keepdims=True)
                    ds.append(_unfold(p * (dp_h - delta) * (dm.hd ** -0.5), upper).astype(MXU))
                    a_sink[h:h + 1, :] += -(sink_probs[g] * delta)
                dsb = jnp.concatenate(ds, axis=1)
                dqn = _dot(kn_tb, dsb)
                dkn = _dot_nt(qnb, dsb)
                dvb = _dot_nt(do_b, probs_b)
                a_gq[...] += dqn * qh
                dqh = dqn * gqv
                dq = rq * (dqh - qh * jnp.mean(dqh * qh, axis=0, keepdims=True))
                for g, h in enumerate(heads):
                    dproj_ref[h * dm.hd:(h + 1) * dm.hd, :] = dq[:, g * BLOCK:(g + 1) * BLOCK].astype(dproj_ref.dtype)
                krows = slice(kh * dm.hd, (kh + 1) * dm.hd)
                vrows = slice(dm.kvw + kh * dm.hd, dm.kvw + (kh + 1) * dm.hd)
                prevpart_scr[krows, :] = dkn[:, :BLOCK]
                prevpart_scr[vrows, :] = dvb[:, :BLOCK]
                curpart_scr[krows, :] = dkn[:, BLOCK:]
                curpart_scr[vrows, :] = dvb[:, BLOCK:]
            dproj_ref[dm.aw:dm.zoff, :] = jnp.zeros((kvw2, BLOCK), dproj_ref.dtype)
            tril = _tril()
            gt, u, mixed, xh, rstd, vnb, wts, (zu, cu), (zv, cv) = _gate_fwd(dm, p_ref, lng_ref, lnb_ref, ws_ref, bs_ref, tril)
            dyg = dy_ref[dm.aw:, :]
            rg = _col_rms(gt)
            gh = gt * rg
            a_gog[...] += dyg * gh
            dgh = dyg * gog_ref[...]
            dgt = rg * (dgh - gh * jnp.mean(dgh * gh, axis=0, keepdims=True))
            du = dgt * mixed
            dmix = dgt * u
            dmixb = dmix.astype(MXU)
            dbs_ref[...] += jnp.sum(_by_group(dmix, dm.ng), axis=1)
            dvn = []
            for g in range(dm.ng):
                rows = slice(g * GROUP_DIM, (g + 1) * GROUP_DIM)
                dws_ref[g] += jnp.where(tril, _dot_tn(dmixb[rows], vnb[rows]), 0.0)
                dvn.append(_dot(dmixb[rows], wts[g]))
            dvn = jnp.concatenate(dvn, axis=0)
            a_lng[...] += dvn * xh
            a_lnb[...] += dvn
            dxh3, xh3 = _by_group(dvn * lng_ref[...], dm.ng), _by_group(xh, dm.ng)
            dv = (rstd * (dxh3 - jnp.mean(dxh3, axis=1, keepdims=True) - xh3 * jnp.mean(dxh3 * xh3, axis=1, keepdims=True))).reshape(dm.gw, BLOCK)
            dgu = cu + zu * (jnp.exp(-0.5 * zu * zu) * _INV_SQRT_2PI)
            dgv = cv + zv * (jnp.exp(-0.5 * zv * zv) * _INV_SQRT_2PI)
            dproj_ref[dm.zoff:dm.zoff + dm.gw, :] = (du * dgu).astype(dproj_ref.dtype)
            dproj_ref[dm.zoff + dm.gw:, :] = (dv * dgv).astype(dproj_ref.dtype)

        @pl.when(i == nb)
        def _():
            prevpart_scr[...] = jnp.zeros_like(prevpart_scr)

        @pl.when(i >= 1)
        def _():
            tot = carry_scr[...] + prevpart_scr[...]
            for kh in range(dm.nkv):
                krows = slice(kh * dm.hd, (kh + 1) * dm.hd)
                kraw = kprev_scr[krows, :]
                rk = _col_rms(kraw)
                khat = kraw * rk
                dkn = tot[krows, :]
                a_gk[...] += dkn * khat
                dkh = dkn * gkv[:, :BLOCK]
                dk = rk * (dkh - khat * jnp.mean(dkh * khat, axis=0, keepdims=True))
                dkv_ref[krows, :] = dk.astype(dkv_ref.dtype)
            dkv_ref[dm.kvw:, :] = tot[dm.kvw:, :].astype(dkv_ref.dtype)

        @pl.when(i < nb)
        def _():
            carry_scr[...] = curpart_scr[...]
            kprev_scr[...] = p_ref[dm.aw:dm.aw + dm.kvw, :]

        @pl.when((s_id == nseq - 1) & (i == nb))
        def _():
            for acc, out in lane_accs:
                out[...] = jnp.sum(acc[...], axis=1, keepdims=True)

    col = lambda rows: jax.ShapeDtypeStruct((rows, 1), F32)
    lane = lambda rows: pltpu.VMEM((rows, LANES), F32)
    return pl.pallas_call(
        body, name="mixer_bwd", grid=(nseq, nb + 1),
        in_specs=[pl.BlockSpec((dm.inw, BLOCK), cur), pl.BlockSpec((kvw2, BLOCK), prev), pl.BlockSpec((dm.d, BLOCK), cur),
                  full(gq.shape), full(gk2.shape), pl.BlockSpec(memory_space=pltpu.SMEM),
                  full(lng.shape), full(lnb.shape), full(ws.shape), full(bs.shape), full(goa.shape), full(gog.shape)],
        out_specs=[pl.BlockSpec((dm.inw, BLOCK), cur), pl.BlockSpec((kvw2, BLOCK), prev_kv_out),
                   full((dm.hd, 1)), full((dm.hd, 1)), full((dm.nh, 1)), full((dm.gw, 1)), full((dm.gw, 1)), full(ws.shape),
                   full(bs.shape), full((dm.aw, 1)), full((dm.gw, 1))],
        out_shape=[jax.ShapeDtypeStruct((dm.inw, t), MXU), jax.ShapeDtypeStruct((kvw2, t), MXU),
                   col(dm.hd), col(dm.hd), col(dm.nh), col(dm.gw), col(dm.gw), jax.ShapeDtypeStruct(ws.shape, F32),
                   jax.ShapeDtypeStruct(bs.shape, F32), col(dm.aw), col(dm.gw)],
        scratch_shapes=[pltpu.VMEM((dm.aw, BLOCK), F32), pltpu.VMEM((dm.aw, BLOCK), F32),
                        pltpu.VMEM((kvw2, BLOCK), F32), pltpu.VMEM((kvw2, BLOCK), F32), pltpu.VMEM((kvw2, BLOCK), F32),
                        pltpu.VMEM((dm.kvw, BLOCK), F32),
                        pltpu.VMEM((dm.hd, dm.grp * BLOCK), F32), lane(dm.hd), lane(dm.nh), lane(dm.gw), lane(dm.gw), lane(dm.aw), lane(dm.gw)],
        compiler_params=_cparams(("arbitrary", "arbitrary"), 48),
    )(proj_t, proj_t, dy_t, gq, gk2, sinks, lng, lnb, ws, bs, goa, gog)


def _patch_kv(dproj_t, dkv_t, dm):
    t = dproj_t.shape[1]
    tc = _pick(t, (1024, 512, 256, 128))
    kvw2 = 2 * dm.kvw
    kvblk = dm.aw // kvw2

    def body(dproj_hbm, dkv_ref, out_ref):
        del dproj_hbm
        out_ref[...] = dkv_ref[...]

    return pl.pallas_call(
        body, name="patch_kv", grid=(t // tc,),
        in_specs=[ANY, pl.BlockSpec((kvw2, tc), lambda i: (0, i))],
        out_specs=pl.BlockSpec((kvw2, tc), lambda i: (kvblk, i)),
        out_shape=jax.ShapeDtypeStruct(dproj_t.shape, dproj_t.dtype),
        input_output_aliases={0: 0},
        compiler_params=_cparams(("parallel",), 32),
    )(dproj_t, dkv_t)


def _place():
    x, y, c = lax.axis_index("x"), lax.axis_index("y"), lax.axis_index("c")
    return x, y, c


def _handshake(peers):
    barrier = pltpu.get_barrier_semaphore()
    for p in peers:
        pl.semaphore_signal(barrier, inc=1, device_id=p, device_id_type=MESH)
    pl.semaphore_wait(barrier, len(peers))


def _sequencer_mesh():
    return plsc.ScalarSubcoreMesh(axis_name="sequencer", num_cores=1)


GATHER_CHUNKS = 4
BF16_ROWS = 16


def _row_chunks(n, k):
    tiles = n // BF16_ROWS
    sizes = [(tiles // k + (1 if i < tiles % k else 0)) * BF16_ROWS for i in range(k)]
    return [(sum(sizes[:i]), sz) for i, sz in enumerate(sizes) if sz]


def _allgather_weight(name, collective_id, shard):
    n = shard.shape[0]
    assert n % BF16_ROWS == 0
    chunks = _row_chunks(n, GATHER_CHUNKS)
    nc = len(chunks)

    def body(src, out, send_sems, recv_sems, local_sem):
        x, y, c = _place()
        me, sib, xn, yn, diag = (x, y, c), (x, y, 1 - c), (1 - x, y, c), (x, 1 - y, c), (1 - x, 1 - y, c)
        relay_to = (x ^ c, y ^ (1 - c), c)
        relay_of = (x ^ (1 - c), y ^ c, c)
        _handshake([sib, xn, yn])

        def rows(place, ci):
            px, py, pc = place
            off, size = chunks[ci]
            return out.at[pl.ds(pl.multiple_of((4 * px + 2 * py + pc) * n + off, BF16_ROWS), size), :]

        def copy(k, ci, block, to, from_src=False):
            off, size = chunks[ci]
            return pltpu.make_async_remote_copy(
                src_ref=src.at[pl.ds(off, size), :] if from_src else rows(block, ci), dst_ref=rows(block, ci),
                send_sem=send_sems.at[ci, k], recv_sem=recv_sems.at[ci, k], device_id=to, device_id_type=MESH)

        mine = pltpu.make_async_copy(src, out.at[pl.ds(pl.multiple_of((4 * x + 2 * y + c) * n, BF16_ROWS), n), :], local_sem)
        mine.start()
        sent = []
        for ci in range(nc):
            sent += [copy(0, ci, me, sib, from_src=True), copy(1, ci, me, xn, from_src=True), copy(2, ci, me, yn, from_src=True)]
        for cp in sent:
            cp.start()
        for ci in range(nc):
            copy(1, ci, xn, me).wait_recv()
            copy(2, ci, yn, me).wait_recv()
            passed = [copy(3, ci, relay_of, relay_to), copy(4, ci, xn, sib), copy(5, ci, yn, sib)]
            for cp in passed:
                cp.start()
            sent += passed
        for ci in range(nc):
            copy(3, ci, diag, me).wait_recv()
            passed = copy(6, ci, diag, sib)
            passed.start()
            sent.append(passed)
        for ci in range(nc):
            copy(0, ci, sib, me).wait_recv()
            for k, block in ((4, (1 - x, y, 1 - c)), (5, (x, 1 - y, 1 - c)), (6, (1 - x, 1 - y, 1 - c))):
                copy(k, ci, block, me).wait_recv()
        for cp in sent:
            cp.wait_send()
        mine.wait()

    return pl.kernel(
        body, name=name,
        out_type=jax.ShapeDtypeStruct((N_DEV * n, shard.shape[1]), shard.dtype),
        mesh=_sequencer_mesh(),
        scratch_types=[pltpu.SemaphoreType.DMA((nc, 7)), pltpu.SemaphoreType.DMA((nc, 7)), pltpu.SemaphoreType.DMA],
        compiler_params=pltpu.CompilerParams(collective_id=collective_id),
    )(shard)


_FLIPS = [(0, 0, 1), (1, 0, 0), (0, 1, 0), (1, 1, 0), (1, 0, 1), (0, 1, 1), (1, 1, 1)]


def _scatter_grad(name, collective_id, grad):
    n = grad.shape[0] // N_DEV

    def body(src, out, send_sems, recv_sems, local_sem):
        x, y, c = _place()
        me_idx = 4 * x + 2 * y + c
        peers = [(x ^ fx, y ^ fy, c ^ fc) for (fx, fy, fc) in _FLIPS]
        _handshake(peers)

        def block(idx):
            return src.at[pl.ds(pl.multiple_of(idx * n, 16), n), :]

        copies = [pltpu.make_async_remote_copy(
            src_ref=block(4 * px + 2 * py + pc), dst_ref=out.at[me_idx], send_sem=send_sems.at[k], recv_sem=recv_sems.at[k],
            device_id=(px, py, pc), device_id_type=MESH) for k, (px, py, pc) in enumerate(peers)]
        mine = pltpu.make_async_copy(block(me_idx), out.at[me_idx], local_sem)
        mine.start()
        for cp in copies:
            cp.start()
        for cp in copies:
            cp.wait_recv()
        for cp in copies:
            cp.wait_send()
        mine.wait()

    return pl.kernel(
        body, name=name,
        out_type=jax.ShapeDtypeStruct((N_DEV, n, grad.shape[1]), grad.dtype),
        mesh=_sequencer_mesh(),
        scratch_types=[pltpu.SemaphoreType.DMA((7,)), pltpu.SemaphoreType.DMA((7,)), pltpu.SemaphoreType.DMA],
        compiler_params=pltpu.CompilerParams(collective_id=collective_id),
    )(grad)


def _allgather_rows(name, collective_id, part):
    def body(src, out, send_sems, recv_sems, local_sem):
        x, y, c = _place()
        me_idx = 4 * x + 2 * y + c
        peers = [(x ^ fx, y ^ fy, c ^ fc) for (fx, fy, fc) in _FLIPS]
        _handshake(peers)
        copies = [pltpu.make_async_remote_copy(
            src_ref=src, dst_ref=out.at[me_idx], send_sem=send_sems.at[k], recv_sem=recv_sems.at[k],
            device_id=peer, device_id_type=MESH) for k, peer in enumerate(peers)]
        mine = pltpu.make_async_copy(src, out.at[me_idx], local_sem)
        mine.start()
        for cp in copies:
            cp.start()
        for cp in copies:
            cp.wait_recv()
        for cp in copies:
            cp.wait_send()
        mine.wait()

    return pl.kernel(
        body, name=name,
        out_type=jax.ShapeDtypeStruct((N_DEV,) + part.shape, part.dtype),
        mesh=_sequencer_mesh(),
        scratch_types=[pltpu.SemaphoreType.DMA((7,)), pltpu.SemaphoreType.DMA((7,)), pltpu.SemaphoreType.DMA],
        compiler_params=pltpu.CompilerParams(collective_id=collective_id),
    )(part)


def _adamw_math(w, g, m, v):
    m = ADAM_B1 * m + (1.0 - ADAM_B1) * g
    v = ADAM_B2 * v + (1.0 - ADAM_B2) * (g * g)
    m_hat = m / (1.0 - ADAM_B1 ** ADAM_STEP)
    v_hat = v / (1.0 - ADAM_B2 ** ADAM_STEP)
    delta = -ADAM_LR * (m_hat / (jnp.sqrt(v_hat) + ADAM_EPS) + ADAM_WD * w)
    return delta, m, v


def _sum_adamw(name, slots, w, m, v, after):
    _, n, kk = slots.shape
    tr = _pick(n, (208, 176, 128, 96, 64, 32, 16))

    def body(s_ref, w_ref, m_ref, v_ref, after_ref, g_ref, d_ref, nm_ref, nv_ref):
        del after_ref
        g = s_ref[0].astype(F32)
        for p in range(1, N_DEV):
            g = g + s_ref[p].astype(F32)
        g_ref[...] = g
        d_ref[...], nm_ref[...], nv_ref[...] = _adamw_math(w_ref[...], g, m_ref[...], v_ref[...])

    row = pl.BlockSpec((tr, kk), lambda i: (i, 0))
    return pl.pallas_call(
        body, name=name, grid=(n // tr,),
        in_specs=[pl.BlockSpec((N_DEV, tr, kk), lambda i: (0, i, 0)), row, row, row, ANY],
        out_specs=[row] * 4,
        out_shape=[jax.ShapeDtypeStruct((n, kk), F32)] * 4,
        compiler_params=_cparams(("parallel",), 48),
    )(slots, w, m, v, after)


def _allreduce_small_adamw(early_slots, late, w, m, v, after):
    ra, rb = early_slots.shape[1], late.shape[0]

    def body(early_ref, late_ref, w_ref, m_ref, v_ref, after_ref, g_ref, d_ref, nm_ref, nv_ref, slots, send_sems, recv_sems):
        del after_ref
        x, y, c = _place()
        me_idx = 4 * x + 2 * y + c
        copies = []
        for k, (fx, fy, fc) in enumerate(_FLIPS):
            px, py, pc = x ^ fx, y ^ fy, c ^ fc
            copies.append(pltpu.make_async_remote_copy(
                src_ref=late_ref, dst_ref=slots.at[me_idx], send_sem=send_sems.at[k], recv_sem=recv_sems.at[k],
                device_id=(px, py, pc), device_id_type=MESH))
        for cp in copies:
            cp.start()
        slots[me_idx] = late_ref[...]
        g = early_ref[0]
        for p in range(1, N_DEV):
            g = g + early_ref[p]
        early = pl.ds(0, ra)
        g_ref[early, :] = g
        d_ref[early, :], nm_ref[early, :], nv_ref[early, :] = _adamw_math(w_ref[early, :], g, m_ref[early, :], v_ref[early, :])
        for cp in copies:
            cp.wait_recv()
        for cp in copies:
            cp.wait_send()
        g = slots[0]
        for p in range(1, N_DEV):
            g = g + slots[p]
        tail = pl.ds(ra, rb)
        g_ref[tail, :] = g
        d_ref[tail, :], nm_ref[tail, :], nv_ref[tail, :] = _adamw_math(w_ref[tail, :], g, m_ref[tail, :], v_ref[tail, :])

    vm = pl.BlockSpec(memory_space=pltpu.VMEM)
    return pl.pallas_call(
        body, name="allreduce_small_adamw",
        in_specs=[vm] * 5 + [ANY], out_specs=[vm] * 4,
        out_shape=[jax.ShapeDtypeStruct((ra + rb, LANES), F32)] * 4,
        scratch_shapes=[pltpu.VMEM((N_DEV, rb, LANES), F32), pltpu.SemaphoreType.DMA((7,)), pltpu.SemaphoreType.DMA((7,))],
        compiler_params=pltpu.CompilerParams(vmem_limit_bytes=48 * MIB),
    )(early_slots, late, w, m, v, after)


def _pack(arrs):
    parts, meta, off = [], [], 0
    for a in arrs:
        flat = a.reshape(-1).astype(F32)
        rows = -(-flat.shape[0] // LANES)
        rows8 = -(-rows // 8) * 8
        flat = jnp.pad(flat, (0, rows8 * LANES - flat.shape[0]))
        parts.append(flat.reshape(rows8, LANES))
        meta.append((off, a.shape, a.size))
        off += rows8
    return jnp.concatenate(parts, axis=0), meta


def _unpack(packed, meta):
    outs = []
    for off, shape, size in meta:
        rows = -(-size // LANES)
        outs.append(packed[off:off + rows].reshape(-1)[:size].reshape(shape))
    return outs


def _silu_parts(a):
    sg = 0.5 + 0.5 * jnp.tanh(0.5 * a)
    return a * sg, sg * (1.0 + a * (1.0 - sg))


def kernel(x, norm1_g, w_in, q_norm_g, k_norm_g, attn_sinks, gate_ln_g, gate_ln_b, w_spatial, b_spatial, out_norm_attn_g, out_norm_gate_g, w_out, norm2_g, w_ffn_gate, w_ffn_up, w_ffn_down, loss_target, m_norm1_g, m_w_in, m_q_norm_g, m_k_norm_g, m_attn_sinks, m_gate_ln_g, m_gate_ln_b, m_w_spatial, m_b_spatial, m_out_norm_attn_g, m_out_norm_gate_g, m_w_out, m_norm2_g, m_w_ffn_gate, m_w_ffn_up, m_w_ffn_down, v_norm1_g, v_w_in, v_q_norm_g, v_k_norm_g, v_attn_sinks, v_gate_ln_g, v_gate_ln_b, v_w_spatial, v_b_spatial, v_out_norm_attn_g, v_out_norm_gate_g, v_w_out, v_norm2_g, v_w_ffn_gate, v_w_ffn_up, v_w_ffn_down):
    nseq, seq, d = x.shape
    t = nseq * seq
    nb = seq // BLOCK
    inw = w_in.shape[2] * N_DEV
    dm = _Dims(d, inw, q_norm_g.shape[-1])
    xf = x.reshape(t, d)
    tgt = loss_target.reshape(t, d)

    rows = lambda wv, transposed: jnp.swapaxes(wv, 1, 2)[0] if transposed else wv[0]
    big = {"w_in": (w_in, m_w_in, v_w_in, True), "w_out": (w_out, m_w_out, v_w_out, False),
           "w_ffn_gate": (w_ffn_gate, m_w_ffn_gate, v_w_ffn_gate, True), "w_ffn_up": (w_ffn_up, m_w_ffn_up, v_w_ffn_up, True),
           "w_ffn_down": (w_ffn_down, m_w_ffn_down, v_w_ffn_down, False)}
    big_rows = {nm: tuple(rows(arr, tr) for arr in (wv, mv, vv)) for nm, (wv, mv, vv, tr) in big.items()}
    shard = lambda nm: big_rows[nm][0].astype(WIRE)
    win_t = _allgather_weight("gather_w_in", 1, shard("w_in"))
    wout = _allgather_weight("gather_w_out", 2, shard("w_out"))
    wu_t = _allgather_weight("gather_w_ffn_up", 9, shard("w_ffn_up"))
    wg_t = _allgather_weight("gather_w_ffn_gate", 3, shard("w_ffn_gate"))
    wd = _allgather_weight("gather_w_ffn_down", 10, shard("w_ffn_down"))

    lanes = lambda v, n=BLOCK: jnp.broadcast_to(v.reshape(-1, 1), (v.size, n))
    prm = (lanes(q_norm_g, dm.grp * BLOCK), lanes(k_norm_g, 2 * BLOCK), attn_sinks[0], lanes(gate_ln_g), lanes(gate_ln_b), w_spatial[0], b_spatial[0],
           lanes(out_norm_attn_g), lanes(out_norm_gate_g))

    h1 = _rms_fwd("rms1_fwd", xf, norm1_g)
    (proj_t,) = _matmul("mm_in", win_t, h1, "nt", [F32])
    y_t = _mixer_fwd(proj_t, prm, dm, nseq, nb)

    def residual_norm(acc, xr, g2):
        x2v = xr + acc
        return x2v, x2v * lax.rsqrt(jnp.mean(x2v * x2v, axis=-1, keepdims=True) + EPS) * g2

    x2, h2 = _matmul("mm_out", y_t, wout, "tn", [F32, MXU], epilogue=residual_norm, extras=[xf], rowvecs=[norm2_g], full_rows=True)
    (b,) = _matmul("mm_up", h2, wu_t, "nt", [MXU])
    a, s = _matmul("mm_gate", h2, wg_t, "nt", [MXU, MXU], epilogue=lambda ga, ub: (ga, _silu_parts(ga)[0] * ub.astype(F32)), extras=[b])

    def loss_epilogue(acc, x2v, tv):
        diff = (x2v + acc) - tv
        dx3 = diff * (1.0 / d)
        return dx3, dx3, jnp.sum(diff * diff)

    dx3, dx3b, lossp = _matmul("mm_down", s, wd, "nn", [F32, MXU], epilogue=loss_epilogue, extras=[x2, tgt], partial=True)
    loss_part = (0.5 / d) * jnp.sum(lossp[::8, ::LANES])

    def dswiglu(acc, av, bv):
        silu, dsilu = _silu_parts(av.astype(F32))
        return acc * bv.astype(F32) * dsilu, acc * silu

    (g_wd,) = _matmul("mm_gw_down", s, dx3b, "tn", [WIRE])
    sl_wd = _scatter_grad("scatter_w_ffn_down", 4, g_wd)
    da, db = _matmul("mm_d_down", dx3b, wd, "nt", [MXU, MXU], epilogue=dswiglu, extras=[a, b], after=[g_wd])
    (g_wg,) = _matmul("mm_gw_gate", da, h2, "tn", [WIRE])
    sl_wg = _scatter_grad("scatter_w_ffn_gate", 5, g_wg)
    (g_wu,) = _matmul("mm_gw_up", db, h2, "tn", [WIRE], after=[g_wg])
    sl_wu = _scatter_grad("scatter_w_ffn_up", 6, g_wu)
    (dh2a,) = _matmul("mm_dh2_gate", da, wg_t, "nn", [F32], after=[g_wu])
    (dh2,) = _matmul("mm_dh2_up", db, wu_t, "nn", [F32], epilogue=lambda acc, pv: (pv + acc,), extras=[dh2a])

    dy_t, dx2, dx2b, dg2 = _norm_bwd_matmul("mm_d_out", wout, dh2, x2, norm2_g, dx3, after=dh2a)
    (g_wout,) = _matmul("mm_gw_out", y_t, dx2b, "nn", [WIRE], after=[dy_t])
    sl_wout = _scatter_grad("scatter_w_out", 7, g_wout)
    (dproj0, dkv, dgq, dgk, dsink, dlng, dlnb, dws, dbs, dgoa, dgog) = _mixer_bwd(proj_t, dy_t, prm, dm, nseq, nb)
    early_g = [dgq, dgk, dsink, dlng, dlnb, dws, dbs, dgoa, dgog, dg2, loss_part.reshape(1)]
    early_slots = _allgather_rows("gather_small_grads", 11, _pack(early_g)[0])
    dproj_t = _patch_kv(dproj0, dkv, dm)
    (g_win,) = _matmul("mm_gw_in", dproj_t, h1, "nn", [WIRE])
    sl_win = _scatter_grad("scatter_w_in", 8, g_win)

    def norm1_backward(dh1, xv, dx2v, g1):
        r = lax.rsqrt(jnp.mean(xv * xv, axis=-1, keepdims=True) + EPS)
        xh = xv * r
        dxh = dh1 * g1
        return dx2v + r * (dxh - xh * jnp.mean(dxh * xh, axis=-1, keepdims=True)), jnp.sum(dh1 * xh, axis=0, keepdims=True)

    dx, dg1 = _matmul("mm_d_in", dproj_t, win_t, "tn", [F32], epilogue=norm1_backward, extras=[xf, dx2], rowvecs=[norm1_g],
                      after=[g_win], col_sum=True, full_rows=True)

    big_out = {}
    last = dx

    def big_update(nm, sl, after):
        res = _sum_adamw("adamw_" + nm, sl, *big_rows[nm], after=after)
        big_out[nm] = tuple(jnp.swapaxes(r[None], 1, 2) if big[nm][3] else r[None] for r in res)
        return res[1]

    for nm, sl in (("w_ffn_down", sl_wd), ("w_ffn_gate", sl_wg), ("w_ffn_up", sl_wu), ("w_out", sl_wout)):
        last = big_update(nm, sl, last)

    zero = jnp.zeros((1,), F32)
    small_names = ["q_norm_g", "k_norm_g", "attn_sinks", "gate_ln_g", "gate_ln_b", "w_spatial", "b_spatial",
                   "out_norm_attn_g", "out_norm_gate_g", "norm2_g", "loss", "norm1_g"]
    small_w = [q_norm_g, k_norm_g, attn_sinks, gate_ln_g, gate_ln_b, w_spatial, b_spatial, out_norm_attn_g, out_norm_gate_g, norm2_g, zero, norm1_g]
    small_m = [m_q_norm_g, m_k_norm_g, m_attn_sinks, m_gate_ln_g, m_gate_ln_b, m_w_spatial, m_b_spatial, m_out_norm_attn_g, m_out_norm_gate_g, m_norm2_g, zero, m_norm1_g]
    small_v = [v_q_norm_g, v_k_norm_g, v_attn_sinks, v_gate_ln_g, v_gate_ln_b, v_w_spatial, v_b_spatial, v_out_norm_attn_g, v_out_norm_gate_g, v_norm2_g, zero, v_norm1_g]
    pw, meta = _pack(small_w)
    sg, sd, sm, sv = _allreduce_small_adamw(early_slots, _pack([dg1])[0], pw, _pack(small_m)[0], _pack(small_v)[0], after=last)
    big_update("w_in", sl_win, sd)
    ug, ud, um, uv = _unpack(sg, meta), _unpack(sd, meta), _unpack(sm, meta), _unpack(sv, meta)
    small_out = {nm: (ug[k], ud[k], um[k], uv[k]) for k, nm in enumerate(small_names)}
    loss = small_out["loss"][0].reshape(())

    order = ["norm1_g", "w_in", "q_norm_g", "k_norm_g", "attn_sinks", "gate_ln_g", "gate_ln_b", "w_spatial", "b_spatial",
             "out_norm_attn_g", "out_norm_gate_g", "w_out", "norm2_g", "w_ffn_gate", "w_ffn_up", "w_ffn_down"]
    allo = {**big_out, **small_out}
    outs = [loss, dx.reshape(nseq, seq, d)]
    for k in range(4):
        outs += [allo[nm][k] for nm in order]
    return tuple(outs)
```

```python
import math

import jax
import jax.numpy as jnp
from jax import lax
from jax.experimental import pallas as pl
from jax.experimental.pallas import tpu as pltpu
from jax.experimental.pallas import tpu_sc as plsc

F32 = jnp.float32
MXU = jnp.bfloat16
WIRE = jnp.bfloat16
EPS = 1e-6
BLOCK = 128
GROUP_DIM = 128
N_KV_HEADS = 2
NEG = -1e30
N_DEV = 8
LANES = 128
MIB = 1024 * 1024

ADAM_LR = 0.001
ADAM_B1 = 0.9
ADAM_B2 = 0.999
ADAM_EPS = 1e-08
ADAM_WD = 0.01
ADAM_STEP = 10

MESH = pl.DeviceIdType.MESH
ANY = pl.BlockSpec(memory_space=pl.ANY)


def _pick(n, cands):
    for c in cands:
        if n % c == 0:
            return c
    return n


def _cparams(sem, vmem_mb):
    return pltpu.CompilerParams(dimension_semantics=sem, vmem_limit_bytes=vmem_mb * MIB)


VMEM_TILE_BUDGET = 44 * MIB
HBM_BYTES_PER_US = 3.0e6
STEP_US = 0.4
MIN_TILE_N = 512


def _tile_candidates(n):
    return [c for c in range(min(n, 2048), 0, -LANES) if n % c == 0 and c % LANES == 0] or [n]


def _matmul_tiles(m, n, kk, esz, extra_sizes, out_sizes, full_rows):
    best = None
    wide = [n] if full_rows else [c for c in _tile_candidates(n) if c >= MIN_TILE_N] or _tile_candidates(n)
    for tm in _tile_candidates(m):
        for tn in wide:
            vmem = 2 * (tm + tn) * kk * esz + tm * tn * (4 + 2 * sum(extra_sizes) + 2 * sum(out_sizes))
            if vmem > VMEM_TILE_BUDGET:
                continue
            cost = (m // tm) * n * kk * esz / HBM_BYTES_PER_US + (m // tm) * (n // tn) * STEP_US
            if best is None or cost < best[0]:
                best = (cost, tm, tn, vmem)
    assert best is not None, (m, n, kk)
    return best[1:]


def _matmul(name, a, b, mode, out_dtypes, epilogue=None, extras=(), rowvecs=(), after=(), partial=False, col_sum=False, full_rows=False):
    if mode == "nn":
        (m, kk), n = a.shape, b.shape[1]
        dn = (((1,), (0,)), ((), ()))
    elif mode == "nt":
        (m, kk), n = a.shape, b.shape[0]
        dn = (((1,), (1,)), ((), ()))
    else:
        (kk, m), n = a.shape, b.shape[1]
        dn = (((0,), (0,)), ((), ()))
    tm, tn, vmem = _matmul_tiles(m, n, kk, a.dtype.itemsize, [e.dtype.itemsize for e in extras],
                                 [jnp.dtype(dt).itemsize for dt in out_dtypes], full_rows)
    a_spec = pl.BlockSpec((kk, tm), lambda i, j: (0, i)) if mode == "tn" else pl.BlockSpec((tm, kk), lambda i, j: (i, 0))
    b_spec = pl.BlockSpec((tn, kk), lambda i, j: (j, 0)) if mode == "nt" else pl.BlockSpec((kk, tn), lambda i, j: (0, j))
    tile = pl.BlockSpec((tm, tn), lambda i, j: (i, j))
    row = pl.BlockSpec((1, tn), lambda i, j: (0, j))
    ne, nr, na, no = len(extras), len(rowvecs), len(after), len(out_dtypes)

    def body(a_ref, b_ref, *rest):
        in_refs, out_refs = rest[:ne + nr], rest[ne + nr + na:]
        acc = lax.dot_general(a_ref[...], b_ref[...], dn, preferred_element_type=F32)
        vals = (acc,) if epilogue is None else epilogue(acc, *[r[...] for r in in_refs])
        for o_ref, t in zip(out_refs[:no], vals[:no]):
            o_ref[...] = t.astype(o_ref.dtype)
        if partial:
            out_refs[no][...] = jnp.full((8, LANES), vals[no], F32)
        if col_sum:
            sum_ref = out_refs[-1]

            @pl.when(pl.program_id(0) == 0)
            def _():
                sum_ref[...] = jnp.zeros_like(sum_ref)

            sum_ref[...] += vals[-1]

    out_specs = [tile] * no
    out_shape = [jax.ShapeDtypeStruct((m, n), dt) for dt in out_dtypes]
    if partial:
        out_specs.append(pl.BlockSpec((8, LANES), lambda i, j: (i, j)))
        out_shape.append(jax.ShapeDtypeStruct((m // tm * 8, n // tn * LANES), F32))
    if col_sum:
        out_specs.append(row)
        out_shape.append(jax.ShapeDtypeStruct((1, n), F32))
    return pl.pallas_call(
        body, name=name, grid=(m // tm, n // tn),
        in_specs=[a_spec, b_spec] + [tile] * ne + [row] * nr + [ANY] * na,
        out_specs=out_specs, out_shape=out_shape,
        compiler_params=_cparams(("arbitrary" if col_sum else "parallel", "arbitrary"), min(vmem // MIB + 8, 60)),
    )(a, b, *extras, *rowvecs, *after)


def _rms_fwd(name, x, g):
    t, d = x.shape
    tm = _pick(t, (512, 256, 128))

    def body(x_ref, g_ref, h_ref):
        xv = x_ref[...]
        r = lax.rsqrt(jnp.mean(xv * xv, axis=-1, keepdims=True) + EPS)
        h_ref[...] = (xv * r * g_ref[...]).astype(h_ref.dtype)

    return pl.pallas_call(
        body, name=name, grid=(t // tm,),
        in_specs=[pl.BlockSpec((tm, d), lambda i: (i, 0)), pl.BlockSpec((1, d), lambda i: (0, 0))],
        out_specs=pl.BlockSpec((tm, d), lambda i: (i, 0)),
        out_shape=jax.ShapeDtypeStruct((t, d), MXU),
        compiler_params=_cparams(("parallel",), 32),
    )(x, g)


def _norm_bwd_matmul(name, w, dh, x, g, res, after):
    t, d = x.shape
    m = w.shape[0]
    tn = _pick(t, (256, 128))

    def body(w_ref, dh_ref, x_ref, g_ref, res_ref, after_ref, out_ref, dx_ref, dxb_ref, dg_ref):
        del after_ref

        @pl.when(pl.program_id(0) == 0)
        def _():
            dg_ref[...] = jnp.zeros_like(dg_ref)

        xv, dhv = x_ref[...], dh_ref[...]
        r = lax.rsqrt(jnp.mean(xv * xv, axis=-1, keepdims=True) + EPS)
        xh = xv * r
        dg_ref[...] += jnp.sum(dhv * xh, axis=0, keepdims=True)
        dxh = dhv * g_ref[...]
        dx = res_ref[...] + r * (dxh - xh * jnp.mean(dxh * xh, axis=-1, keepdims=True))
        dx_ref[...] = dx
        dxb = dx.astype(MXU)
        dxb_ref[...] = dxb
        out_ref[...] = lax.dot_general(w_ref[...], dxb, (((1,), (1,)), ((), ())), preferred_element_type=F32)

    row = pl.BlockSpec((tn, d), lambda j: (j, 0))
    vec = pl.BlockSpec((1, d), lambda j: (0, 0))
    return pl.pallas_call(
        body, name=name, grid=(t // tn,),
        in_specs=[pl.BlockSpec((m, d), lambda j: (0, 0)), row, row, vec, row, ANY],
        out_specs=[pl.BlockSpec((m, tn), lambda j: (0, j)), row, row, vec],
        out_shape=[jax.ShapeDtypeStruct((m, t), F32), jax.ShapeDtypeStruct((t, d), F32), jax.ShapeDtypeStruct((t, d), MXU),
                   jax.ShapeDtypeStruct((1, d), F32)],
        compiler_params=_cparams(("arbitrary",), 52),
    )(w, dh, x, g, res, after)


_INV_SQRT2 = 0.7071067811865476
_INV_SQRT_2PI = 0.3989422804014327


def _dot_nt(a, b):
    return lax.dot_general(a, b, (((1,), (1,)), ((), ())), preferred_element_type=F32)


def _dot_tn(a, b):
    return lax.dot_general(a, b, (((0,), (0,)), ((), ())), preferred_element_type=F32)


def _dot(a, b):
    return jnp.dot(a, b, preferred_element_type=F32)


def _col_rms(v):
    return lax.rsqrt(jnp.mean(v * v, axis=0, keepdims=True) + EPS)


class _Dims:
    def __init__(self, d_model, in_width, head_dim):
        self.d = d_model
        self.aw = d_model // 2
        self.gw = d_model - self.aw
        self.kvw = (in_width - self.aw - 2 * self.gw) // 2
        self.hd = head_dim
        self.nh = self.aw // head_dim
        self.nkv = self.kvw // head_dim
        self.grp = self.nh // self.nkv
        self.ng = self.gw // GROUP_DIM
        self.inw = in_width
        self.zoff = self.aw + 2 * self.kvw
        assert self.nkv == N_KV_HEADS and self.zoff + 2 * self.gw == in_width and self.aw % (2 * self.kvw) == 0


def _band_masks(first):
    r = lax.broadcasted_iota(jnp.int32, (BLOCK, BLOCK), 0)
    t = lax.broadcasted_iota(jnp.int32, (BLOCK, BLOCK), 1)
    upper = r > t
    dist = t - r + jnp.where(upper, BLOCK, 0)
    return upper, jnp.logical_not(upper & first), dist.astype(F32)


def _fold(full, upper):
    return jnp.where(upper, full[:BLOCK], full[BLOCK:])


def _unfold(folded, upper):
    zero = jnp.zeros_like(folded)
    return jnp.concatenate([jnp.where(upper, folded, zero), jnp.where(upper, zero, folded)], axis=0)


def _kv_band(dm, kh, p_ref, pkv_ref, gk2):
    ko = dm.aw + kh * dm.hd
    vo = dm.aw + dm.kvw + kh * dm.hd
    k_t = jnp.concatenate([pkv_ref[kh * dm.hd:(kh + 1) * dm.hd, :], p_ref[ko:ko + dm.hd, :]], axis=1)
    v_t = jnp.concatenate([pkv_ref[dm.kvw + kh * dm.hd:dm.kvw + (kh + 1) * dm.hd, :], p_ref[vo:vo + dm.hd, :]], axis=1)
    kn_t = k_t * _col_rms(k_t) * gk2
    return kn_t.astype(MXU), kn_t.T.astype(MXU), v_t.astype(MXU), v_t.T.astype(MXU)


def _group_heads(dm, kh):
    return range(kh * dm.grp, (kh + 1) * dm.grp)


def _attn_group_fwd(dm, kh, p_ref, gq, kn, v_tb, sink_ref, masks):
    heads = _group_heads(dm, kh)
    q = jnp.concatenate([p_ref[h * dm.hd:(h + 1) * dm.hd, :] for h in heads], axis=1)
    rq = _col_rms(q)
    qh = q * rq
    qnb = (qh * gq).astype(MXU)
    upper, valid, dist = masks
    s = _dot(kn, qnb)
    probs, probs_b, sink_probs = [], [], []
    for g, h in enumerate(heads):
        slope, sink = math.pow(2.0, -8.0 * (h + 1) / dm.nh), sink_ref[h]
        logits = jnp.where(valid, _fold(s[:, g * BLOCK:(g + 1) * BLOCK], upper) * (dm.hd ** -0.5) - slope * dist, NEG)
        m = jnp.maximum(jnp.max(logits, axis=0, keepdims=True), sink)
        e = jnp.exp(logits - m)
        es = jnp.exp(sink - m)
        inv = 1.0 / (jnp.sum(e, axis=0, keepdims=True) + es)
        probs.append(e * inv)
        probs_b.append(_unfold(probs[g], upper).astype(MXU))
        sink_probs.append(es * inv)
    probs_b = jnp.concatenate(probs_b, axis=1)
    o = _dot(v_tb, probs_b)
    return o, probs, probs_b, sink_probs, rq, qh, qnb


def _gelu_cdf(z):
    return 0.5 * (1.0 + lax.erf(z * _INV_SQRT2))


def _by_group(v, ng):
    return v.reshape(ng, GROUP_DIM, v.shape[1])


def _gate_fwd(dm, p_ref, lng_ref, lnb_ref, ws_ref, bs_ref, tril):
    zu, zv = p_ref[dm.zoff:dm.zoff + dm.gw, :], p_ref[dm.zoff + dm.gw:dm.zoff + 2 * dm.gw, :]
    cu, cv = _gelu_cdf(zu), _gelu_cdf(zv)
    u, v = zu * cu, zv * cv
    v3 = _by_group(v, dm.ng)
    xc = v3 - jnp.mean(v3, axis=1, keepdims=True)
    rstd = lax.rsqrt(jnp.mean(xc * xc, axis=1, keepdims=True) + EPS)
    xh = (xc * rstd).reshape(dm.gw, BLOCK)
    vnb = (xh * lng_ref[...] + lnb_ref[...]).astype(MXU)
    wts = [jnp.where(tril, ws_ref[g], 0.0).astype(MXU) for g in range(dm.ng)]
    mixed = jnp.concatenate([_dot_nt(vnb[g * GROUP_DIM:(g + 1) * GROUP_DIM], wts[g]) + bs_ref[g:g + 1, :]
                             for g in range(dm.ng)], axis=0)
    return u * mixed, u, mixed, xh, rstd, vnb, wts, (zu, cu), (zv, cv)


def _mixer_specs(dm, nb, clamp):
    kvblk = dm.aw // (2 * dm.kvw)

    def cur(s, i):
        return (0, s * nb + clamp(i))

    def prev(s, i):
        return (kvblk, s * nb + jnp.maximum(clamp(i) - 1, 0))

    full = lambda shape: pl.BlockSpec(shape, lambda s, i: tuple(0 for _ in shape))
    return cur, prev, full


def _tril():
    return lax.broadcasted_iota(jnp.int32, (BLOCK, BLOCK), 0) >= lax.broadcasted_iota(jnp.int32, (BLOCK, BLOCK), 1)


def _mixer_fwd(proj_t, prm, dm, nseq, nb):
    gq, gk2, sinks, lng, lnb, ws, bs, goa, gog = prm
    t = proj_t.shape[1]
    cur, prev, full = _mixer_specs(dm, nb, lambda i: i)

    def body(p_ref, pkv_ref, gq_ref, gk_ref, sink_ref, lng_ref, lnb_ref, ws_ref, bs_ref, goa_ref, gog_ref, y_ref, att_scr):
        i = pl.program_id(1)
        masks = _band_masks(i == 0)
        gqv, gkv = gq_ref[...], gk_ref[...]
        for kh in range(dm.nkv):
            _, kn, v_tb, _ = _kv_band(dm, kh, p_ref, pkv_ref, gkv)
            o = _attn_group_fwd(dm, kh, p_ref, gqv, kn, v_tb, sink_ref, masks)[0]
            for g, h in enumerate(_group_heads(dm, kh)):
                att_scr[h * dm.hd:(h + 1) * dm.hd, :] = o[:, g * BLOCK:(g + 1) * BLOCK]
        att = att_scr[...]
        y_ref[:dm.aw, :] = (att * _col_rms(att) * goa_ref[...]).astype(y_ref.dtype)
        gt = _gate_fwd(dm, p_ref, lng_ref, lnb_ref, ws_ref, bs_ref, _tril())[0]
        y_ref[dm.aw:, :] = (gt * _col_rms(gt) * gog_ref[...]).astype(y_ref.dtype)

    return pl.pallas_call(
        body, name="mixer_fwd", grid=(nseq, nb),
        in_specs=[pl.BlockSpec((dm.inw, BLOCK), cur), pl.BlockSpec((2 * dm.kvw, BLOCK), prev),
                  full(gq.shape), full(gk2.shape), pl.BlockSpec(memory_space=pltpu.SMEM),
                  full(lng.shape), full(lnb.shape), full(ws.shape), full(bs.shape), full(goa.shape), full(gog.shape)],
        out_specs=pl.BlockSpec((dm.d, BLOCK), cur),
        out_shape=jax.ShapeDtypeStruct((dm.d, t), MXU),
        scratch_shapes=[pltpu.VMEM((dm.aw, BLOCK), F32)],
        compiler_params=_cparams(("parallel", "arbitrary"), 40),
    )(proj_t, proj_t, gq, gk2, sinks, lng, lnb, ws, bs, goa, gog)


def _mixer_bwd(proj_t, dy_t, prm, dm, nseq, nb):
    gq, gk2, sinks, lng, lnb, ws, bs, goa, gog = prm
    t = proj_t.shape[1]
    clamp = lambda i: jnp.minimum(i, nb - 1)
    cur, prev, full = _mixer_specs(dm, nb, clamp)
    kvw2 = 2 * dm.kvw

    def prev_kv_out(s, i):
        return (0, s * nb + jnp.maximum(i - 1, 0))

    def body(p_ref, pkv_ref, dy_ref, gq_ref, gk_ref, sink_ref, lng_ref, lnb_ref, ws_ref, bs_ref, goa_ref, gog_ref,
             dproj_ref, dkv_ref, dgq_ref, dgk_ref, dsink_ref, dlng_ref, dlnb_ref, dws_ref, dbs_ref, dgoa_ref, dgog_ref,
             att_scr, datt_scr, carry_scr, prevpart_scr, curpart_scr, kprev_scr,
             a_gq, a_gk, a_sink, a_lng, a_lnb, a_goa, a_gog):
        s_id, i = pl.program_id(0), pl.program_id(1)
        lane_accs = ((a_gq, dgq_ref), (a_gk, dgk_ref), (a_sink, dsink_ref), (a_lng, dlng_ref), (a_lnb, dlnb_ref),
                     (a_goa, dgoa_ref), (a_gog, dgog_ref))

        @pl.when((s_id == 0) & (i == 0))
        def _():
            for acc, _ in lane_accs:
                acc[...] = jnp.zeros_like(acc)
            dws_ref[...] = jnp.zeros_like(dws_ref)
            dbs_ref[...] = jnp.zeros_like(dbs_ref)

        gqv, gkv = gq_ref[...], gk_ref[...]

        @pl.when(i < nb)
        def _():
            masks = _band_masks(i == 0)
            upper = masks[0]
            kvs, fwd = [], []
            for kh in range(dm.nkv):
                kv = _kv_band(dm, kh, p_ref, pkv_ref, gkv)
                kvs.append(kv)
                fwd.append(_attn_group_fwd(dm, kh, p_ref, gqv, kv[1], kv[2], sink_ref, masks))
                for g, h in enumerate(_group_heads(dm, kh)):
                    att_scr[h * dm.hd:(h + 1) * dm.hd, :] = fwd[kh][0][:, g * BLOCK:(g + 1) * BLOCK]
            att = att_scr[...]
            dya = dy_ref[:dm.aw, :]
            ra = _col_rms(att)
            ah = att * ra
            a_goa[...] += dya * ah
            dah = dya * goa_ref[...]
            datt_scr[...] = ra * (dah - ah * jnp.mean(dah * ah, axis=0, keepdims=True))
            for kh in range(dm.nkv):
                kn_tb, kn, v_tb, vb = kvs[kh]
                _, probs, probs_b, sink_probs, rq, qh, qnb = fwd[kh]
                heads = _group_heads(dm, kh)
                do_b = jnp.concatenate([datt_scr[h * dm.hd:(h + 1) * dm.hd, :] for h in heads], axis=1).astype(MXU)
                dp = _dot(vb, do_b)
                ds = []
                for g, h in enumerate(heads):
                    p, dp_h = probs[g], _fold(dp[:, g * BLOCK:(g + 1) * BLOCK], upper)
                    delta = jnp.sum(p * dp_h, axis=0, keepdims=True)
                    ds.append(_unfold(p * (dp_h - delta) * (dm.hd ** -0.5), upper).astype(MXU))
                    a_sink[h:h + 1, :] += -(sink_probs[g] * delta)
                dsb = jnp.concatenate(ds, axis=1)
                dqn = _dot(kn_tb, dsb)
                dkn = _dot_nt(qnb, dsb)
                dvb = _dot_nt(do_b, probs_b)
                a_gq[...] += dqn * qh
                dqh = dqn * gqv
                dq = rq * (dqh - qh * jnp.mean(dqh * qh, axis=0, keepdims=True))
                for g, h in enumerate(heads):
                    dproj_ref[h * dm.hd:(h + 1) * dm.hd, :] = dq[:, g * BLOCK:(g + 1) * BLOCK].astype(dproj_ref.dtype)
                krows = slice(kh * dm.hd, (kh + 1) * dm.hd)
                vrows = slice(dm.kvw + kh * dm.hd, dm.kvw + (kh + 1) * dm.hd)
                prevpart_scr[krows, :] = dkn[:, :BLOCK]
                prevpart_scr[vrows, :] = dvb[:, :BLOCK]
                curpart_scr[krows, :] = dkn[:, BLOCK:]
                curpart_scr[vrows, :] = dvb[:, BLOCK:]
            dproj_ref[dm.aw:dm.zoff, :] = jnp.zeros((kvw2, BLOCK), dproj_ref.dtype)
            tril = _tril()
            gt, u, mixed, xh, rstd, vnb, wts, (zu, cu), (zv, cv) = _gate_fwd(dm, p_ref, lng_ref, lnb_ref, ws_ref, bs_ref, tril)
            dyg = dy_ref[dm.aw:, :]
            rg = _col_rms(gt)
            gh = gt * rg
            a_gog[...] += dyg * gh
            dgh = dyg * gog_ref[...]
            dgt = rg * (dgh - gh * jnp.mean(dgh * gh, axis=0, keepdims=True))
            du = dgt * mixed
            dmix = dgt * u
            dmixb = dmix.astype(MXU)
            dbs_ref[...] += jnp.sum(_by_group(dmix, dm.ng), axis=1)
            dvn = []
            for g in range(dm.ng):
                rows = slice(g * GROUP_DIM, (g + 1) * GROUP_DIM)
                dws_ref[g] += jnp.where(tril, _dot_tn(dmixb[rows], vnb[rows]), 0.0)
                dvn.append(_dot(dmixb[rows], wts[g]))
            dvn = jnp.concatenate(dvn, axis=0)
            a_lng[...] += dvn * xh
            a_lnb[...] += dvn
            dxh3, xh3 = _by_group(dvn * lng_ref[...], dm.ng), _by_group(xh, dm.ng)
            dv = (rstd * (dxh3 - jnp.mean(dxh3, axis=1, keepdims=True) - xh3 * jnp.mean(dxh3 * xh3, axis=1, keepdims=True))).reshape(dm.gw, BLOCK)
            dgu = cu + zu * (jnp.exp(-0.5 * zu * zu) * _INV_SQRT_2PI)
            dgv = cv + zv * (jnp.exp(-0.5 * zv * zv) * _INV_SQRT_2PI)
            dproj_ref[dm.zoff:dm.zoff + dm.gw, :] = (du * dgu).astype(dproj_ref.dtype)
            dproj_ref[dm.zoff + dm.gw:, :] = (dv * dgv).astype(dproj_ref.dtype)

        @pl.when(i == nb)
        def _():
            prevpart_scr[...] = jnp.zeros_like(prevpart_scr)

        @pl.when(i >= 1)
        def _():
            tot = carry_scr[...] + prevpart_scr[...]
            for kh in range(dm.nkv):
                krows = slice(kh * dm.hd, (kh + 1) * dm.hd)
                kraw = kprev_scr[krows, :]
                rk = _col_rms(kraw)
                khat = kraw * rk
                dkn = tot[krows, :]
                a_gk[...] += dkn * khat
                dkh = dkn * gkv[:, :BLOCK]
                dk = rk * (dkh - khat * jnp.mean(dkh * khat, axis=0, keepdims=True))
                dkv_ref[krows, :] = dk.astype(dkv_ref.dtype)
            dkv_ref[dm.kvw:, :] = tot[dm.kvw:, :].astype(dkv_ref.dtype)

        @pl.when(i < nb)
        def _():
            carry_scr[...] = curpart_scr[...]
            kprev_scr[...] = p_ref[dm.aw:dm.aw + dm.kvw, :]

        @pl.when((s_id == nseq - 1) & (i == nb))
        def _():
            for acc, out in lane_accs:
                out[...] = jnp.sum(acc[...], axis=1, keepdims=True)

    col = lambda rows: jax.ShapeDtypeStruct((rows, 1), F32)
    lane = lambda rows: pltpu.VMEM((rows, LANES), F32)
    return pl.pallas_call(
        body, name="mixer_bwd", grid=(nseq, nb + 1),
        in_specs=[pl.BlockSpec((dm.inw, BLOCK), cur), pl.BlockSpec((kvw2, BLOCK), prev), pl.BlockSpec((dm.d, BLOCK), cur),
                  full(gq.shape), full(gk2.shape), pl.BlockSpec(memory_space=pltpu.SMEM),
                  full(lng.shape), full(lnb.shape), full(ws.shape), full(bs.shape), full(goa.shape), full(gog.shape)],
        out_specs=[pl.BlockSpec((dm.inw, BLOCK), cur), pl.BlockSpec((kvw2, BLOCK), prev_kv_out),
                   full((dm.hd, 1)), full((dm.hd, 1)), full((dm.nh, 1)), full((dm.gw, 1)), full((dm.gw, 1)), full(ws.shape),
                   full(bs.shape), full((dm.aw, 1)), full((dm.gw, 1))],
        out_shape=[jax.ShapeDtypeStruct((dm.inw, t), MXU), jax.ShapeDtypeStruct((kvw2, t), MXU),
                   col(dm.hd), col(dm.hd), col(dm.nh), col(dm.gw), col(dm.gw), jax.ShapeDtypeStruct(ws.shape, F32),
                   jax.ShapeDtypeStruct(bs.shape, F32), col(dm.aw), col(dm.gw)],
        scratch_shapes=[pltpu.VMEM((dm.aw, BLOCK), F32), pltpu.VMEM((dm.aw, BLOCK), F32),
                        pltpu.VMEM((kvw2, BLOCK), F32), pltpu.VMEM((kvw2, BLOCK), F32), pltpu.VMEM((kvw2, BLOCK), F32),
                        pltpu.VMEM((dm.kvw, BLOCK), F32),
                        pltpu.VMEM((dm.hd, dm.grp * BLOCK), F32), lane(dm.hd), lane(dm.nh), lane(dm.gw), lane(dm.gw), lane(dm.aw), lane(dm.gw)],
        compiler_params=_cparams(("arbitrary", "arbitrary"), 48),
    )(proj_t, proj_t, dy_t, gq, gk2, sinks, lng, lnb, ws, bs, goa, gog)


def _patch_kv(dproj_t, dkv_t, dm):
    t = dproj_t.shape[1]
    tc = _pick(t, (1024, 512, 256, 128))
    kvw2 = 2 * dm.kvw
    kvblk = dm.aw // kvw2

    def body(dproj_hbm, dkv_ref, out_ref):
        del dproj_hbm
        out_ref[...] = dkv_ref[...]

    return pl.pallas_call(
        body, name="patch_kv", grid=(t // tc,),
        in_specs=[ANY, pl.BlockSpec((kvw2, tc), lambda i: (0, i))],
        out_specs=pl.BlockSpec((kvw2, tc), lambda i: (kvblk, i)),
        out_shape=jax.ShapeDtypeStruct(dproj_t.shape, dproj_t.dtype),
        input_output_aliases={0: 0},
        compiler_params=_cparams(("parallel",), 32),
    )(dproj_t, dkv_t)


def _place():
    x, y, c = lax.axis_index("x"), lax.axis_index("y"), lax.axis_index("c")
    return x, y, c


def _handshake(peers):
    barrier = pltpu.get_barrier_semaphore()
    for p in peers:
        pl.semaphore_signal(barrier, inc=1, device_id=p, device_id_type=MESH)
    pl.semaphore_wait(barrier, len(peers))


def _sequencer_mesh():
    return plsc.ScalarSubcoreMesh(axis_name="sequencer", num_cores=1)


GATHER_CHUNKS = 4
BF16_ROWS = 16


def _row_chunks(n, k):
    tiles = n // BF16_ROWS
    sizes = [(tiles // k + (1 if i < tiles % k else 0)) * BF16_ROWS for i in range(k)]
    return [(sum(sizes[:i]), sz) for i, sz in enumerate(sizes) if sz]


def _allgather_weight(name, collective_id, shard):
    n = shard.shape[0]
    assert n % BF16_ROWS == 0
    chunks = _row_chunks(n, GATHER_CHUNKS)
    nc = len(chunks)

    def body(src, out, send_sems, recv_sems, local_sem):
        x, y, c = _place()
        me, sib, xn, yn, diag = (x, y, c), (x, y, 1 - c), (1 - x, y, c), (x, 1 - y, c), (1 - x, 1 - y, c)
        relay_to = (x ^ c, y ^ (1 - c), c)
        relay_of = (x ^ (1 - c), y ^ c, c)
        _handshake([sib, xn, yn])

        def rows(place, ci):
            px, py, pc = place
            off, size = chunks[ci]
            return out.at[pl.ds(pl.multiple_of((4 * px + 2 * py + pc) * n + off, BF16_ROWS), size), :]

        def copy(k, ci, block, to, from_src=False):
            off, size = chunks[ci]
            return pltpu.make_async_remote_copy(
                src_ref=src.at[pl.ds(off, size), :] if from_src else rows(block, ci), dst_ref=rows(block, ci),
                send_sem=send_sems.at[ci, k], recv_sem=recv_sems.at[ci, k], device_id=to, device_id_type=MESH)

        mine = pltpu.make_async_copy(src, out.at[pl.ds(pl.multiple_of((4 * x + 2 * y + c) * n, BF16_ROWS), n), :], local_sem)
        mine.start()
        sent = []
        for ci in range(nc):
            sent += [copy(0, ci, me, sib, from_src=True), copy(1, ci, me, xn, from_src=True), copy(2, ci, me, yn, from_src=True)]
        for cp in sent:
            cp.start()
        for ci in range(nc):
            copy(1, ci, xn, me).wait_recv()
            copy(2, ci, yn, me).wait_recv()
            passed = [copy(3, ci, relay_of, relay_to), copy(4, ci, xn, sib), copy(5, ci, yn, sib)]
            for cp in passed:
                cp.start()
            sent += passed
        for ci in range(nc):
            copy(3, ci, diag, me).wait_recv()
            passed = copy(6, ci, diag, sib)
            passed.start()
            sent.append(passed)
        for ci in range(nc):
            copy(0, ci, sib, me).wait_recv()
            for k, block in ((4, (1 - x, y, 1 - c)), (5, (x, 1 - y, 1 - c)), (6, (1 - x, 1 - y, 1 - c))):
                copy(k, ci, block, me).wait_recv()
        for cp in sent:
            cp.wait_send()
        mine.wait()

    return pl.kernel(
        body, name=name,
        out_type=jax.ShapeDtypeStruct((N_DEV * n, shard.shape[1]), shard.dtype),
        mesh=_sequencer_mesh(),
        scratch_types=[pltpu.SemaphoreType.DMA((nc, 7)), pltpu.SemaphoreType.DMA((nc, 7)), pltpu.SemaphoreType.DMA],
        compiler_params=pltpu.CompilerParams(collective_id=collective_id),
    )(shard)


_FLIPS = [(0, 0, 1), (1, 0, 0), (0, 1, 0), (1, 1, 0), (1, 0, 1), (0, 1, 1), (1, 1, 1)]


def _scatter_grad(name, collective_id, grad):
    n = grad.shape[0] // N_DEV

    def body(src, out, send_sems, recv_sems, local_sem):
        x, y, c = _place()
        me_idx = 4 * x + 2 * y + c
        peers = [(x ^ fx, y ^ fy, c ^ fc) for (fx, fy, fc) in _FLIPS]
        _handshake(peers)

        def block(idx):
            return src.at[pl.ds(pl.multiple_of(idx * n, 16), n), :]

        copies = [pltpu.make_async_remote_copy(
            src_ref=block(4 * px + 2 * py + pc), dst_ref=out.at[me_idx], send_sem=send_sems.at[k], recv_sem=recv_sems.at[k],
            device_id=(px, py, pc), device_id_type=MESH) for k, (px, py, pc) in enumerate(peers)]
        mine = pltpu.make_async_copy(block(me_idx), out.at[me_idx], local_sem)
        mine.start()
        for cp in copies:
            cp.start()
        for cp in copies:
            cp.wait_recv()
        for cp in copies:
            cp.wait_send()
        mine.wait()

    return pl.kernel(
        body, name=name,
        out_type=jax.ShapeDtypeStruct((N_DEV, n, grad.shape[1]), grad.dtype),
        mesh=_sequencer_mesh(),
        scratch_types=[pltpu.SemaphoreType.DMA((7,)), pltpu.SemaphoreType.DMA((7,)), pltpu.SemaphoreType.DMA],
        compiler_params=pltpu.CompilerParams(collective_id=collective_id),
    )(grad)


def _allgather_rows(name, collective_id, part):
    def body(src, out, send_sems, recv_sems, local_sem):
        x, y, c = _place()
        me_idx = 4 * x + 2 * y + c
        peers = [(x ^ fx, y ^ fy, c ^ fc) for (fx, fy, fc) in _FLIPS]
        _handshake(peers)
        copies = [pltpu.make_async_remote_copy(
            src_ref=src, dst_ref=out.at[me_idx], send_sem=send_sems.at[k], recv_sem=recv_sems.at[k],
            device_id=peer, device_id_type=MESH) for k, peer in enumerate(peers)]
        mine = pltpu.make_async_copy(src, out.at[me_idx], local_sem)
        mine.start()
        for cp in copies:
            cp.start()
        for cp in copies:
            cp.wait_recv()
        for cp in copies:
            cp.wait_send()
        mine.wait()

    return pl.kernel(
        body, name=name,
        out_type=jax.ShapeDtypeStruct((N_DEV,) + part.shape, part.dtype),
        mesh=_sequencer_mesh(),
        scratch_types=[pltpu.SemaphoreType.DMA((7,)), pltpu.SemaphoreType.DMA((7,)), pltpu.SemaphoreType.DMA],
        compiler_params=pltpu.CompilerParams(collective_id=collective_id),
    )(part)


def _adamw_math(w, g, m, v):
    m = ADAM_B1 * m + (1.0 - ADAM_B1) * g
    v = ADAM_B2 * v + (1.0 - ADAM_B2) * (g * g)
    m_hat = m / (1.0 - ADAM_B1 ** ADAM_STEP)
    v_hat = v / (1.0 - ADAM_B2 ** ADAM_STEP)
    delta = -ADAM_LR * (m_hat / (jnp.sqrt(v_hat) + ADAM_EPS) + ADAM_WD * w)
    return delta, m, v


def _sum_adamw(name, slots, w, m, v, after):
    _, n, kk = slots.shape
    tr = _pick(n, (208, 176, 128, 96, 64, 32, 16))

    def body(s_ref, w_ref, m_ref, v_ref, after_ref, g_ref, d_ref, nm_ref, nv_ref):
        del after_ref
        g = s_ref[0].astype(F32)
        for p in range(1, N_DEV):
            g = g + s_ref[p].astype(F32)
        g_ref[...] = g
        d_ref[...], nm_ref[...], nv_ref[...] = _adamw_math(w_ref[...], g, m_ref[...], v_ref[...])

    row = pl.BlockSpec((tr, kk), lambda i: (i, 0))
    return pl.pallas_call(
        body, name=name, grid=(n // tr,),
        in_specs=[pl.BlockSpec((N_DEV, tr, kk), lambda i: (0, i, 0)), row, row, row, ANY],
        out_specs=[row] * 4,
        out_shape=[jax.ShapeDtypeStruct((n, kk), F32)] * 4,
        compiler_params=_cparams(("parallel",), 48),
    )(slots, w, m, v, after)


def _allreduce_small_adamw(early_slots, late, w, m, v, after):
    ra, rb = early_slots.shape[1], late.shape[0]

    def body(early_ref, late_ref, w_ref, m_ref, v_ref, after_ref, g_ref, d_ref, nm_ref, nv_ref, slots, send_sems, recv_sems):
        del after_ref
        x, y, c = _place()
        me_idx = 4 * x + 2 * y + c
        copies = []
        for k, (fx, fy, fc) in enumerate(_FLIPS):
            px, py, pc = x ^ fx, y ^ fy, c ^ fc
            copies.append(pltpu.make_async_remote_copy(
                src_ref=late_ref, dst_ref=slots.at[me_idx], send_sem=send_sems.at[k], recv_sem=recv_sems.at[k],
                device_id=(px, py, pc), device_id_type=MESH))
        for cp in copies:
            cp.start()
        slots[me_idx] = late_ref[...]
        g = early_ref[0]
        for p in range(1, N_DEV):
            g = g + early_ref[p]
        early = pl.ds(0, ra)
        g_ref[early, :] = g
        d_ref[early, :], nm_ref[early, :], nv_ref[early, :] = _adamw_math(w_ref[early, :], g, m_ref[early, :], v_ref[early, :])
        for cp in copies:
            cp.wait_recv()
        for cp in copies:
            cp.wait_send()
        g = slots[0]
        for p in range(1, N_DEV):
            g = g + slots[p]
        tail = pl.ds(ra, rb)
        g_ref[tail, :] = g
        d_ref[tail, :], nm_ref[tail, :], nv_ref[tail, :] = _adamw_math(w_ref[tail, :], g, m_ref[tail, :], v_ref[tail, :])

    vm = pl.BlockSpec(memory_space=pltpu.VMEM)
    return pl.pallas_call(
        body, name="allreduce_small_adamw",
        in_specs=[vm] * 5 + [ANY], out_specs=[vm] * 4,
        out_shape=[jax.ShapeDtypeStruct((ra + rb, LANES), F32)] * 4,
        scratch_shapes=[pltpu.VMEM((N_DEV, rb, LANES), F32), pltpu.SemaphoreType.DMA((7,)), pltpu.SemaphoreType.DMA((7,))],
        compiler_params=pltpu.CompilerParams(vmem_limit_bytes=48 * MIB),
    )(early_slots, late, w, m, v, after)


def _pack(arrs):
    parts, meta, off = [], [], 0
    for a in arrs:
        flat = a.reshape(-1).astype(F32)
        rows = -(-flat.shape[0] // LANES)
        rows8 = -(-rows // 8) * 8
        flat = jnp.pad(flat, (0, rows8 * LANES - flat.shape[0]))
        parts.append(flat.reshape(rows8, LANES))
        meta.append((off, a.shape, a.size))
        off += rows8
    return jnp.concatenate(parts, axis=0), meta


def _unpack(packed, meta):
    outs = []
    for off, shape, size in meta:
        rows = -(-size // LANES)
        outs.append(packed[off:off + rows].reshape(-1)[:size].reshape(shape))
    return outs


def _silu_parts(a):
    sg = 0.5 + 0.5 * jnp.tanh(0.5 * a)
    return a * sg, sg * (1.0 + a * (1.0 - sg))


def kernel(x, norm1_g, w_in, q_norm_g, k_norm_g, attn_sinks, gate_ln_g, gate_ln_b, w_spatial, b_spatial, out_norm_attn_g, out_norm_gate_g, w_out, norm2_g, w_ffn_gate, w_ffn_up, w_ffn_down, loss_target, m_norm1_g, m_w_in, m_q_norm_g, m_k_norm_g, m_attn_sinks, m_gate_ln_g, m_gate_ln_b, m_w_spatial, m_b_spatial, m_out_norm_attn_g, m_out_norm_gate_g, m_w_out, m_norm2_g, m_w_ffn_gate, m_w_ffn_up, m_w_ffn_down, v_norm1_g, v_w_in, v_q_norm_g, v_k_norm_g, v_attn_sinks, v_gate_ln_g, v_gate_ln_b, v_w_spatial, v_b_spatial, v_out_norm_attn_g, v_out_norm_gate_g, v_w_out, v_norm2_g, v_w_ffn_gate, v_w_ffn_up, v_w_ffn_down):
    nseq, seq, d = x.shape
    t = nseq * seq
    nb = seq // BLOCK
    inw = w_in.shape[2] * N_DEV
    dm = _Dims(d, inw, q_norm_g.shape[-1])
    xf = x.reshape(t, d)
    tgt = loss_target.reshape(t, d)

    rows = lambda wv, transposed: jnp.swapaxes(wv, 1, 2)[0] if transposed else wv[0]
    big = {"w_in": (w_in, m_w_in, v_w_in, True), "w_out": (w_out, m_w_out, v_w_out, False),
           "w_ffn_gate": (w_ffn_gate, m_w_ffn_gate, v_w_ffn_gate, True), "w_ffn_up": (w_ffn_up, m_w_ffn_up, v_w_ffn_up, True),
           "w_ffn_down": (w_ffn_down, m_w_ffn_down, v_w_ffn_down, False)}
    big_rows = {nm: tuple(rows(arr, tr) for arr in (wv, mv, vv)) for nm, (wv, mv, vv, tr) in big.items()}
    shard = lambda nm: big_rows[nm][0].astype(WIRE)
    win_t = _allgather_weight("gather_w_in", 1, shard("w_in"))
    wout = _allgather_weight("gather_w_out", 2, shard("w_out"))
    wu_t = _allgather_weight("gather_w_ffn_up", 9, shard("w_ffn_up"))
    wg_t = _allgather_weight("gather_w_ffn_gate", 3, shard("w_ffn_gate"))
    wd = _allgather_weight("gather_w_ffn_down", 10, shard("w_ffn_down"))

    lanes = lambda v, n=BLOCK: jnp.broadcast_to(v.reshape(-1, 1), (v.size, n))
    prm = (lanes(q_norm_g, dm.grp * BLOCK), lanes(k_norm_g, 2 * BLOCK), attn_sinks[0], lanes(gate_ln_g), lanes(gate_ln_b), w_spatial[0], b_spatial[0],
           lanes(out_norm_attn_g), lanes(out_norm_gate_g))

    h1 = _rms_fwd("rms1_fwd", xf, norm1_g)
    (proj_t,) = _matmul("mm_in", win_t, h1, "nt", [F32])
    y_t = _mixer_fwd(proj_t, prm, dm, nseq, nb)

    def residual_norm(acc, xr, g2):
        x2v = xr + acc
        return x2v, x2v * lax.rsqrt(jnp.mean(x2v * x2v, axis=-1, keepdims=True) + EPS) * g2

    x2, h2 = _matmul("mm_out", y_t, wout, "tn", [F32, MXU], epilogue=residual_norm, extras=[xf], rowvecs=[norm2_g], full_rows=True)
    (b,) = _matmul("mm_up", h2, wu_t, "nt", [MXU])

    def swiglu(ga, ub):
        silu, dsilu = _silu_parts(ga)
        ub = ub.astype(F32)
        return silu * ub, silu, dsilu * ub

    s, silu_a, dsilu_a_b = _matmul("mm_gate", h2, wg_t, "nt", [MXU, MXU, MXU], epilogue=swiglu, extras=[b])

    def loss_epilogue(acc, x2v, tv):
        diff = (x2v + acc) - tv
        dx3 = diff * (1.0 / d)
        return dx3, dx3, jnp.sum(diff * diff)

    dx3, dx3b, lossp = _matmul("mm_down", s, wd, "nn", [F32, MXU], epilogue=loss_epilogue, extras=[x2, tgt], partial=True)
    loss_part = (0.5 / d) * jnp.sum(lossp[::8, ::LANES])

    def dswiglu(ds, dsilu_b, silu):
        return ds * dsilu_b.astype(F32), ds * silu.astype(F32)

    (g_wd,) = _matmul("mm_gw_down", s, dx3b, "tn", [WIRE])
    sl_wd = _scatter_grad("scatter_w_ffn_down", 4, g_wd)
    da, db = _matmul("mm_d_down", dx3b, wd, "nt", [MXU, MXU], epilogue=dswiglu, extras=[dsilu_a_b, silu_a], after=[g_wd])
    (g_wg,) = _matmul("mm_gw_gate", da, h2, "tn", [WIRE])
    sl_wg = _scatter_grad("scatter_w_ffn_gate", 5, g_wg)
    (g_wu,) = _matmul("mm_gw_up", db, h2, "tn", [WIRE], after=[g_wg])
    sl_wu = _scatter_grad("scatter_w_ffn_up", 6, g_wu)
    (dh2a,) = _matmul("mm_dh2_gate", da, wg_t, "nn", [F32], after=[g_wu])
    (dh2,) = _matmul("mm_dh2_up", db, wu_t, "nn", [F32], epilogue=lambda acc, pv: (pv + acc,), extras=[dh2a])

    dy_t, dx2, dx2b, dg2 = _norm_bwd_matmul("mm_d_out", wout, dh2, x2, norm2_g, dx3, after=dh2a)
    (g_wout,) = _matmul("mm_gw_out", y_t, dx2b, "nn", [WIRE], after=[dy_t])
    sl_wout = _scatter_grad("scatter_w_out", 7, g_wout)
    (dproj0, dkv, dgq, dgk, dsink, dlng, dlnb, dws, dbs, dgoa, dgog) = _mixer_bwd(proj_t, dy_t, prm, dm, nseq, nb)
    early_g = [dgq, dgk, dsink, dlng, dlnb, dws, dbs, dgoa, dgog, dg2, loss_part.reshape(1)]
    early_slots = _allgather_rows("gather_small_grads", 11, _pack(early_g)[0])
    dproj_t = _patch_kv(dproj0, dkv, dm)
    (g_win,) = _matmul("mm_gw_in", dproj_t, h1, "nn", [WIRE])
    sl_win = _scatter_grad("scatter_w_in", 8, g_win)

    def norm1_backward(dh1, xv, dx2v, g1):
        r = lax.rsqrt(jnp.mean(xv * xv, axis=-1, keepdims=True) + EPS)
        xh = xv * r
        dxh = dh1 * g1
        return dx2v + r * (dxh - xh * jnp.mean(dxh * xh, axis=-1, keepdims=True)), jnp.sum(dh1 * xh, axis=0, keepdims=True)

    dx, dg1 = _matmul("mm_d_in", dproj_t, win_t, "tn", [F32], epilogue=norm1_backward, extras=[xf, dx2], rowvecs=[norm1_g],
                      after=[g_win], col_sum=True, full_rows=True)

    big_out = {}
    last = dx

    def big_update(nm, sl, after):
        res = _sum_adamw("adamw_" + nm, sl, *big_rows[nm], after=after)
        big_out[nm] = tuple(jnp.swapaxes(r[None], 1, 2) if big[nm][3] else r[None] for r in res)
        return res[1]

    for nm, sl in (("w_ffn_down", sl_wd), ("w_ffn_gate", sl_wg), ("w_ffn_up", sl_wu), ("w_out", sl_wout)):
        last = big_update(nm, sl, last)

    zero = jnp.zeros((1,), F32)
    small_names = ["q_norm_g", "k_norm_g", "attn_sinks", "gate_ln_g", "gate_ln_b", "w_spatial", "b_spatial",
                   "out_norm_attn_g", "out_norm_gate_g", "norm2_g", "loss", "norm1_g"]
    small_w = [q_norm_g, k_norm_g, attn_sinks, gate_ln_g, gate_ln_b, w_spatial, b_spatial, out_norm_attn_g, out_norm_gate_g, norm2_g, zero, norm1_g]
    small_m = [m_q_norm_g, m_k_norm_g, m_attn_sinks, m_gate_ln_g, m_gate_ln_b, m_w_spatial, m_b_spatial, m_out_norm_attn_g, m_out_norm_gate_g, m_norm2_g, zero, m_norm1_g]
    small_v = [v_q_norm_g, v_k_norm_g, v_attn_sinks, v_gate_ln_g, v_gate_ln_b, v_w_spatial, v_b_spatial, v_out_norm_attn_g, v_out_norm_gate_g, v_norm2_g, zero, v_norm1_g]
    pw, meta = _pack(small_w)
    sg, sd, sm, sv = _allreduce_small_adamw(early_slots, _pack([dg1])[0], pw, _pack(small_m)[0], _pack(small_v)[0], after=last)
    big_update("w_in", sl_win, sd)
    ug, ud, um, uv = _unpack(sg, meta), _unpack(sd, meta), _unpack(sm, meta), _unpack(sv, meta)
    small_out = {nm: (ug[k], ud[k], um[k], uv[k]) for k, nm in enumerate(small_names)}
    loss = small_out["loss"][0].reshape(())

    order = ["norm1_g", "w_in", "q_norm_g", "k_norm_g", "attn_sinks", "gate_ln_g", "gate_ln_b", "w_spatial", "b_spatial",
             "out_norm_attn_g", "out_norm_gate_g", "w_out", "norm2_g", "w_ffn_gate", "w_ffn_up", "w_ffn_down"]
    allo = {**big_out, **small_out}
    outs = [loss, dx.reshape(nseq, seq, d)]
    for k in range(4):
        outs += [allo[nm][k] for nm in order]
    return tuple(outs)
```

```python
import math

import jax
import jax.numpy as jnp
from jax import lax
from jax.experimental import pallas as pl
from jax.experimental.pallas import tpu as pltpu
from jax.experimental.pallas import tpu_sc as plsc

F32 = jnp.float32
MXU = jnp.bfloat16
WIRE = jnp.bfloat16
EPS = 1e-6
BLOCK = 128
GROUP_DIM = 128
N_KV_HEADS = 2
NEG = -1e30
N_DEV = 8
LANES = 128
MIB = 1024 * 1024

ADAM_LR = 0.001
ADAM_B1 = 0.9
ADAM_B2 = 0.999
ADAM_EPS = 1e-08
ADAM_WD = 0.01
ADAM_STEP = 10

MESH = pl.DeviceIdType.MESH
ANY = pl.BlockSpec(memory_space=pl.ANY)


def _pick(n, cands):
    for c in cands:
        if n % c == 0:
            return c
    return n


def _cparams(sem, vmem_mb):
    return pltpu.CompilerParams(dimension_semantics=sem, vmem_limit_bytes=vmem_mb * MIB)


VMEM_TILE_BUDGET = 44 * MIB
HBM_BYTES_PER_US = 3.0e6
STEP_US = 0.4
MIN_TILE_N = 512


def _tile_candidates(n):
    return [c for c in range(min(n, 2048), 0, -LANES) if n % c == 0 and c % LANES == 0] or [n]


def _matmul_tiles(m, n, kk, esz, extra_sizes, out_sizes, full_rows):
    best = None
    wide = [n] if full_rows else [c for c in _tile_candidates(n) if c >= MIN_TILE_N] or _tile_candidates(n)
    for tm in _tile_candidates(m):
        for tn in wide:
            b_buffers = 1 if full_rows else 2
            vmem = (2 * tm + b_buffers * tn) * kk * esz + tm * tn * (4 + 2 * sum(extra_sizes) + 2 * sum(out_sizes))
            if vmem > VMEM_TILE_BUDGET:
                continue
            cost = (m // tm) * n * kk * esz / HBM_BYTES_PER_US + (m // tm) * (n // tn) * STEP_US
            if best is None or cost < best[0]:
                best = (cost, tm, tn, vmem)
    assert best is not None, (m, n, kk)
    return best[1:]


def _matmul(name, a, b, mode, out_dtypes, epilogue=None, extras=(), rowvecs=(), after=(), partial=False, col_sum=False, full_rows=False):
    if mode == "nn":
        (m, kk), n = a.shape, b.shape[1]
        dn = (((1,), (0,)), ((), ()))
    elif mode == "nt":
        (m, kk), n = a.shape, b.shape[0]
        dn = (((1,), (1,)), ((), ()))
    else:
        (kk, m), n = a.shape, b.shape[1]
        dn = (((0,), (0,)), ((), ()))
    tm, tn, vmem = _matmul_tiles(m, n, kk, a.dtype.itemsize, [e.dtype.itemsize for e in extras],
                                 [jnp.dtype(dt).itemsize for dt in out_dtypes], full_rows)
    a_spec = pl.BlockSpec((kk, tm), lambda i, j: (0, i)) if mode == "tn" else pl.BlockSpec((tm, kk), lambda i, j: (i, 0))
    resident = dict(pipeline_mode=pl.Buffered(1)) if full_rows else {}
    b_spec = pl.BlockSpec((tn, kk), lambda i, j: (j, 0), **resident) if mode == "nt" else pl.BlockSpec((kk, tn), lambda i, j: (0, j), **resident)
    tile = pl.BlockSpec((tm, tn), lambda i, j: (i, j))
    row = pl.BlockSpec((1, tn), lambda i, j: (0, j))
    ne, nr, na, no = len(extras), len(rowvecs), len(after), len(out_dtypes)

    def body(a_ref, b_ref, *rest):
        in_refs, out_refs = rest[:ne + nr], rest[ne + nr + na:]
        acc = lax.dot_general(a_ref[...], b_ref[...], dn, preferred_element_type=F32)
        vals = (acc,) if epilogue is None else epilogue(acc, *[r[...] for r in in_refs])
        for o_ref, t in zip(out_refs[:no], vals[:no]):
            o_ref[...] = t.astype(o_ref.dtype)
        if partial:
            out_refs[no][...] = jnp.full((8, LANES), vals[no], F32)
        if col_sum:
            sum_ref = out_refs[-1]

            @pl.when(pl.program_id(0) == 0)
            def _():
                sum_ref[...] = jnp.zeros_like(sum_ref)

            sum_ref[...] += vals[-1]

    out_specs = [tile] * no
    out_shape = [jax.ShapeDtypeStruct((m, n), dt) for dt in out_dtypes]
    if partial:
        out_specs.append(pl.BlockSpec((8, LANES), lambda i, j: (i, j)))
        out_shape.append(jax.ShapeDtypeStruct((m // tm * 8, n // tn * LANES), F32))
    if col_sum:
        out_specs.append(row)
        out_shape.append(jax.ShapeDtypeStruct((1, n), F32))
    return pl.pallas_call(
        body, name=name, grid=(m // tm, n // tn),
        in_specs=[a_spec, b_spec] + [tile] * ne + [row] * nr + [ANY] * na,
        out_specs=out_specs, out_shape=out_shape,
        compiler_params=_cparams(("arbitrary" if col_sum else "parallel", "arbitrary"), min(vmem // MIB + 8, 60)),
    )(a, b, *extras, *rowvecs, *after)


def _rms_fwd(name, x, g):
    t, d = x.shape
    tm = _pick(t, (512, 256, 128))

    def body(x_ref, g_ref, h_ref):
        xv = x_ref[...]
        r = lax.rsqrt(jnp.mean(xv * xv, axis=-1, keepdims=True) + EPS)
        h_ref[...] = (xv * r * g_ref[...]).astype(h_ref.dtype)

    return pl.pallas_call(
        body, name=name, grid=(t // tm,),
        in_specs=[pl.BlockSpec((tm, d), lambda i: (i, 0)), pl.BlockSpec((1, d), lambda i: (0, 0))],
        out_specs=pl.BlockSpec((tm, d), lambda i: (i, 0)),
        out_shape=jax.ShapeDtypeStruct((t, d), MXU),
        compiler_params=_cparams(("parallel",), 32),
    )(x, g)


def _norm_bwd_matmul(name, w, dh, x, g, res, after):
    t, d = x.shape
    m = w.shape[0]
    tn = _pick(t, (256, 128))

    def body(w_ref, dh_ref, x_ref, g_ref, res_ref, after_ref, out_ref, dx_ref, dxb_ref, dg_ref):
        del after_ref

        @pl.when(pl.program_id(0) == 0)
        def _():
            dg_ref[...] = jnp.zeros_like(dg_ref)

        xv, dhv = x_ref[...], dh_ref[...]
        r = lax.rsqrt(jnp.mean(xv * xv, axis=-1, keepdims=True) + EPS)
        xh = xv * r
        dg_ref[...] += jnp.sum(dhv * xh, axis=0, keepdims=True)
        dxh = dhv * g_ref[...]
        dx = res_ref[...] + r * (dxh - xh * jnp.mean(dxh * xh, axis=-1, keepdims=True))
        dx_ref[...] = dx
        dxb = dx.astype(MXU)
        dxb_ref[...] = dxb
        out_ref[...] = lax.dot_general(w_ref[...], dxb, (((1,), (1,)), ((), ())), preferred_element_type=F32)

    row = pl.BlockSpec((tn, d), lambda j: (j, 0))
    vec = pl.BlockSpec((1, d), lambda j: (0, 0))
    return pl.pallas_call(
        body, name=name, grid=(t // tn,),
        in_specs=[pl.BlockSpec((m, d), lambda j: (0, 0), pipeline_mode=pl.Buffered(1)), row, row, vec, row, ANY],
        out_specs=[pl.BlockSpec((m, tn), lambda j: (0, j)), row, row, vec],
        out_shape=[jax.ShapeDtypeStruct((m, t), F32), jax.ShapeDtypeStruct((t, d), F32), jax.ShapeDtypeStruct((t, d), MXU),
                   jax.ShapeDtypeStruct((1, d), F32)],
        compiler_params=_cparams(("arbitrary",), 52),
    )(w, dh, x, g, res, after)


_INV_SQRT2 = 0.7071067811865476
_INV_SQRT_2PI = 0.3989422804014327


def _dot_nt(a, b):
    return lax.dot_general(a, b, (((1,), (1,)), ((), ())), preferred_element_type=F32)


def _dot_tn(a, b):
    return lax.dot_general(a, b, (((0,), (0,)), ((), ())), preferred_element_type=F32)


def _dot(a, b):
    return jnp.dot(a, b, preferred_element_type=F32)


def _col_rms(v):
    return lax.rsqrt(jnp.mean(v * v, axis=0, keepdims=True) + EPS)


class _Dims:
    def __init__(self, d_model, in_width, head_dim):
        self.d = d_model
        self.aw = d_model // 2
        self.gw = d_model - self.aw
        self.kvw = (in_width - self.aw - 2 * self.gw) // 2
        self.hd = head_dim
        self.nh = self.aw // head_dim
        self.nkv = self.kvw // head_dim
        self.grp = self.nh // self.nkv
        self.ng = self.gw // GROUP_DIM
        self.inw = in_width
        self.zoff = self.aw + 2 * self.kvw
        assert self.nkv == N_KV_HEADS and self.zoff + 2 * self.gw == in_width and self.aw % (2 * self.kvw) == 0


def _band_masks(first):
    r = lax.broadcasted_iota(jnp.int32, (BLOCK, BLOCK), 0)
    t = lax.broadcasted_iota(jnp.int32, (BLOCK, BLOCK), 1)
    upper = r > t
    dist = t - r + jnp.where(upper, BLOCK, 0)
    return upper, jnp.logical_not(upper & first), dist.astype(F32)


def _fold(full, upper):
    return jnp.where(upper, full[:BLOCK], full[BLOCK:])


def _unfold(folded, upper):
    zero = jnp.zeros_like(folded)
    return jnp.concatenate([jnp.where(upper, folded, zero), jnp.where(upper, zero, folded)], axis=0)


def _kv_band(dm, kh, p_ref, pkv_ref, gk2):
    ko = dm.aw + kh * dm.hd
    vo = dm.aw + dm.kvw + kh * dm.hd
    k_t = jnp.concatenate([pkv_ref[kh * dm.hd:(kh + 1) * dm.hd, :], p_ref[ko:ko + dm.hd, :]], axis=1)
    v_t = jnp.concatenate([pkv_ref[dm.kvw + kh * dm.hd:dm.kvw + (kh + 1) * dm.hd, :], p_ref[vo:vo + dm.hd, :]], axis=1)
    kn_t = k_t * _col_rms(k_t) * gk2
    return kn_t.astype(MXU), kn_t.T.astype(MXU), v_t.astype(MXU), v_t.T.astype(MXU)


def _group_heads(dm, kh):
    return range(kh * dm.grp, (kh + 1) * dm.grp)


def _attn_group_fwd(dm, kh, p_ref, gq, kn, v_tb, sink_ref, masks):
    heads = _group_heads(dm, kh)
    q = jnp.concatenate([p_ref[h * dm.hd:(h + 1) * dm.hd, :] for h in heads], axis=1)
    rq = _col_rms(q)
    qh = q * rq
    qnb = (qh * gq).astype(MXU)
    upper, valid, dist = masks
    s = _dot(kn, qnb)
    probs, probs_b, sink_probs = [], [], []
    for g, h in enumerate(heads):
        slope, sink = math.pow(2.0, -8.0 * (h + 1) / dm.nh), sink_ref[h]
        logits = jnp.where(valid, _fold(s[:, g * BLOCK:(g + 1) * BLOCK], upper) * (dm.hd ** -0.5) - slope * dist, NEG)
        m = jnp.maximum(jnp.max(logits, axis=0, keepdims=True), sink)
        e = jnp.exp(logits - m)
        es = jnp.exp(sink - m)
        inv = 1.0 / (jnp.sum(e, axis=0, keepdims=True) + es)
        probs.append(e * inv)
        probs_b.append(_unfold(probs[g], upper).astype(MXU))
        sink_probs.append(es * inv)
    probs_b = jnp.concatenate(probs_b, axis=1)
    o = _dot(v_tb, probs_b)
    return o, probs, probs_b, sink_probs, rq, qh, qnb


def _gelu_cdf(z):
    return 0.5 * (1.0 + lax.erf(z * _INV_SQRT2))


def _by_group(v, ng):
    return v.reshape(ng, GROUP_DIM, v.shape[1])


def _gate_fwd(dm, p_ref, lng_ref, lnb_ref, ws_ref, bs_ref, tril):
    zu, zv = p_ref[dm.zoff:dm.zoff + dm.gw, :], p_ref[dm.zoff + dm.gw:dm.zoff + 2 * dm.gw, :]
    cu, cv = _gelu_cdf(zu), _gelu_cdf(zv)
    u, v = zu * cu, zv * cv
    v3 = _by_group(v, dm.ng)
    xc = v3 - jnp.mean(v3, axis=1, keepdims=True)
    rstd = lax.rsqrt(jnp.mean(xc * xc, axis=1, keepdims=True) + EPS)
    xh = (xc * rstd).reshape(dm.gw, BLOCK)
    vnb = (xh * lng_ref[...] + lnb_ref[...]).astype(MXU)
    wts = [jnp.where(tril, ws_ref[g], 0.0).astype(MXU) for g in range(dm.ng)]
    mixed = jnp.concatenate([_dot_nt(vnb[g * GROUP_DIM:(g + 1) * GROUP_DIM], wts[g]) + bs_ref[g:g + 1, :]
                             for g in range(dm.ng)], axis=0)
    return u * mixed, u, mixed, xh, rstd, vnb, wts, (zu, cu), (zv, cv)


def _mixer_specs(dm, nb, clamp):
    kvblk = dm.aw // (2 * dm.kvw)

    def cur(s, i):
        return (0, s * nb + clamp(i))

    def prev(s, i):
        return (kvblk, s * nb + jnp.maximum(clamp(i) - 1, 0))

    full = lambda shape: pl.BlockSpec(shape, lambda s, i: tuple(0 for _ in shape))
    return cur, prev, full


def _tril():
    return lax.broadcasted_iota(jnp.int32, (BLOCK, BLOCK), 0) >= lax.broadcasted_iota(jnp.int32, (BLOCK, BLOCK), 1)


def _mixer_fwd(proj_t, prm, dm, nseq, nb):
    gq, gk2, sinks, lng, lnb, ws, bs, goa, gog = prm
    t = proj_t.shape[1]
    cur, prev, full = _mixer_specs(dm, nb, lambda i: i)

    def body(p_ref, pkv_ref, gq_ref, gk_ref, sink_ref, lng_ref, lnb_ref, ws_ref, bs_ref, goa_ref, gog_ref, y_ref, att_scr):
        i = pl.program_id(1)
        masks = _band_masks(i == 0)
        gqv, gkv = gq_ref[...], gk_ref[...]
        for kh in range(dm.nkv):
            _, kn, v_tb, _ = _kv_band(dm, kh, p_ref, pkv_ref, gkv)
            o = _attn_group_fwd(dm, kh, p_ref, gqv, kn, v_tb, sink_ref, masks)[0]
            for g, h in enumerate(_group_heads(dm, kh)):
                att_scr[h * dm.hd:(h + 1) * dm.hd, :] = o[:, g * BLOCK:(g + 1) * BLOCK]
        att = att_scr[...]
        y_ref[:dm.aw, :] = (att * _col_rms(att) * goa_ref[...]).astype(y_ref.dtype)
        gt = _gate_fwd(dm, p_ref, lng_ref, lnb_ref, ws_ref, bs_ref, _tril())[0]
        y_ref[dm.aw:, :] = (gt * _col_rms(gt) * gog_ref[...]).astype(y_ref.dtype)

    return pl.pallas_call(
        body, name="mixer_fwd", grid=(nseq, nb),
        in_specs=[pl.BlockSpec((dm.inw, BLOCK), cur), pl.BlockSpec((2 * dm.kvw, BLOCK), prev),
                  full(gq.shape), full(gk2.shape), pl.BlockSpec(memory_space=pltpu.SMEM),
                  full(lng.shape), full(lnb.shape), full(ws.shape), full(bs.shape), full(goa.shape), full(gog.shape)],
        out_specs=pl.BlockSpec((dm.d, BLOCK), cur),
        out_shape=jax.ShapeDtypeStruct((dm.d, t), MXU),
        scratch_shapes=[pltpu.VMEM((dm.aw, BLOCK), F32)],
        compiler_params=_cparams(("parallel", "arbitrary"), 40),
    )(proj_t, proj_t, gq, gk2, sinks, lng, lnb, ws, bs, goa, gog)


def _mixer_bwd(proj_t, dy_t, prm, dm, nseq, nb):
    gq, gk2, sinks, lng, lnb, ws, bs, goa, gog = prm
    t = proj_t.shape[1]
    clamp = lambda i: jnp.minimum(i, nb - 1)
    cur, prev, full = _mixer_specs(dm, nb, clamp)
    kvw2 = 2 * dm.kvw

    def prev_kv_out(s, i):
        return (0, s * nb + jnp.maximum(i - 1, 0))

    def body(p_ref, pkv_ref, dy_ref, gq_ref, gk_ref, sink_ref, lng_ref, lnb_ref, ws_ref, bs_ref, goa_ref, gog_ref,
             dproj_ref, dkv_ref, dgq_ref, dgk_ref, dsink_ref, dlng_ref, dlnb_ref, dws_ref, dbs_ref, dgoa_ref, dgog_ref,
             att_scr, datt_scr, carry_scr, prevpart_scr, curpart_scr, kprev_scr,
             a_gq, a_gk, a_sink, a_lng, a_lnb, a_goa, a_gog):
        s_id, i = pl.program_id(0), pl.program_id(1)
        lane_accs = ((a_gq, dgq_ref), (a_gk, dgk_ref), (a_sink, dsink_ref), (a_lng, dlng_ref), (a_lnb, dlnb_ref),
                     (a_goa, dgoa_ref), (a_gog, dgog_ref))

        @pl.when((s_id == 0) & (i == 0))
        def _():
            for acc, _ in lane_accs:
                acc[...] = jnp.zeros_like(acc)
            dws_ref[...] = jnp.zeros_like(dws_ref)
            dbs_ref[...] = jnp.zeros_like(dbs_ref)

        gqv, gkv = gq_ref[...], gk_ref[...]

        @pl.when(i < nb)
        def _():
            masks = _band_masks(i == 0)
            upper = masks[0]
            kvs, fwd = [], []
            for kh in range(dm.nkv):
                kv = _kv_band(dm, kh, p_ref, pkv_ref, gkv)
                kvs.append(kv)
                fwd.append(_attn_group_fwd(dm, kh, p_ref, gqv, kv[1], kv[2], sink_ref, masks))
                for g, h in enumerate(_group_heads(dm, kh)):
                    att_scr[h * dm.hd:(h + 1) * dm.hd, :] = fwd[kh][0][:, g * BLOCK:(g + 1) * BLOCK]
            att = att_scr[...]
            dya = dy_ref[:dm.aw, :]
            ra = _col_rms(att)
            ah = att * ra
            a_goa[...] += dya * ah
            dah = dya * goa_ref[...]
            datt_scr[...] = ra * (dah - ah * jnp.mean(dah * ah, axis=0, keepdims=True))
            for kh in range(dm.nkv):
                kn_tb, kn, v_tb, vb = kvs[kh]
                _, probs, probs_b, sink_probs, rq, qh, qnb = fwd[kh]
                heads = _group_heads(dm, kh)
                do_b = jnp.concatenate([datt_scr[h * dm.hd:(h + 1) * dm.hd, :] for h in heads], axis=1).astype(MXU)
                dp = _dot(vb, do_b)
                ds = []
                for g, h in enumerate(heads):
                    p, dp_h = probs[g], _fold(dp[:, g * BLOCK:(g + 1) * BLOCK], upper)
                    delta = jnp.sum(p * dp_h, axis=0, keepdims=True)
                    ds.append(_unfold(p * (dp_h - delta) * (dm.hd ** -0.5), upper).astype(MXU))
                    a_sink[h:h + 1, :] += -(sink_probs[g] * delta)
                dsb = jnp.concatenate(ds, axis=1)
                dqn = _dot(kn_tb, dsb)
                dkn = _dot_nt(qnb, dsb)
                dvb = _dot_nt(do_b, probs_b)
                a_gq[...] += dqn * qh
                dqh = dqn * gqv
                dq = rq * (dqh - qh * jnp.mean(dqh * qh, axis=0, keepdims=True))
                for g, h in enumerate(heads):
                    dproj_ref[h * dm.hd:(h + 1) * dm.hd, :] = dq[:, g * BLOCK:(g + 1) * BLOCK].astype(dproj_ref.dtype)
                krows = slice(kh * dm.hd, (kh + 1) * dm.hd)
                vrows = slice(dm.kvw + kh * dm.hd, dm.kvw + (kh + 1) * dm.hd)
                prevpart_scr[krows, :] = dkn[:, :BLOCK]
                prevpart_scr[vrows, :] = dvb[:, :BLOCK]
                curpart_scr[krows, :] = dkn[:, BLOCK:]
                curpart_scr[vrows, :] = dvb[:, BLOCK:]
            dproj_ref[dm.aw:dm.zoff, :] = jnp.zeros((kvw2, BLOCK), dproj_ref.dtype)
            tril = _tril()
            gt, u, mixed, xh, rstd, vnb, wts, (zu, cu), (zv, cv) = _gate_fwd(dm, p_ref, lng_ref, lnb_ref, ws_ref, bs_ref, tril)
            dyg = dy_ref[dm.aw:, :]
            rg = _col_rms(gt)
            gh = gt * rg
            a_gog[...] += dyg * gh
            dgh = dyg * gog_ref[...]
            dgt = rg * (dgh - gh * jnp.mean(dgh * gh, axis=0, keepdims=True))
            du = dgt * mixed
            dmix = dgt * u
            dmixb = dmix.astype(MXU)
            dbs_ref[...] += jnp.sum(_by_group(dmix, dm.ng), axis=1)
            dvn = []
            for g in range(dm.ng):
                rows = slice(g * GROUP_DIM, (g + 1) * GROUP_DIM)
                dws_ref[g] += jnp.where(tril, _dot_tn(dmixb[rows], vnb[rows]), 0.0)
                dvn.append(_dot(dmixb[rows], wts[g]))
            dvn = jnp.concatenate(dvn, axis=0)
            a_lng[...] += dvn * xh
            a_lnb[...] += dvn
            dxh3, xh3 = _by_group(dvn * lng_ref[...], dm.ng), _by_group(xh, dm.ng)
            dv = (rstd * (dxh3 - jnp.mean(dxh3, axis=1, keepdims=True) - xh3 * jnp.mean(dxh3 * xh3, axis=1, keepdims=True))).reshape(dm.gw, BLOCK)
            dgu = cu + zu * (jnp.exp(-0.5 * zu * zu) * _INV_SQRT_2PI)
            dgv = cv + zv * (jnp.exp(-0.5 * zv * zv) * _INV_SQRT_2PI)
            dproj_ref[dm.zoff:dm.zoff + dm.gw, :] = (du * dgu).astype(dproj_ref.dtype)
            dproj_ref[dm.zoff + dm.gw:, :] = (dv * dgv).astype(dproj_ref.dtype)

        @pl.when(i == nb)
        def _():
            prevpart_scr[...] = jnp.zeros_like(prevpart_scr)

        @pl.when(i >= 1)
        def _():
            tot = carry_scr[...] + prevpart_scr[...]
            for kh in range(dm.nkv):
                krows = slice(kh * dm.hd, (kh + 1) * dm.hd)
                kraw = kprev_scr[krows, :]
                rk = _col_rms(kraw)
                khat = kraw * rk
                dkn = tot[krows, :]
                a_gk[...] += dkn * khat
                dkh = dkn * gkv[:, :BLOCK]
                dk = rk * (dkh - khat * jnp.mean(dkh * khat, axis=0, keepdims=True))
                dkv_ref[krows, :] = dk.astype(dkv_ref.dtype)
            dkv_ref[dm.kvw:, :] = tot[dm.kvw:, :].astype(dkv_ref.dtype)

        @pl.when(i < nb)
        def _():
            carry_scr[...] = curpart_scr[...]
            kprev_scr[...] = p_ref[dm.aw:dm.aw + dm.kvw, :]

        @pl.when((s_id == nseq - 1) & (i == nb))
        def _():
            for acc, out in lane_accs:
                out[...] = jnp.sum(acc[...], axis=1, keepdims=True)

    col = lambda rows: jax.ShapeDtypeStruct((rows, 1), F32)
    lane = lambda rows: pltpu.VMEM((rows, LANES), F32)
    return pl.pallas_call(
        body, name="mixer_bwd", grid=(nseq, nb + 1),
        in_specs=[pl.BlockSpec((dm.inw, BLOCK), cur), pl.BlockSpec((kvw2, BLOCK), prev), pl.BlockSpec((dm.d, BLOCK), cur),
                  full(gq.shape), full(gk2.shape), pl.BlockSpec(memory_space=pltpu.SMEM),
                  full(lng.shape), full(lnb.shape), full(ws.shape), full(bs.shape), full(goa.shape), full(gog.shape)],
        out_specs=[pl.BlockSpec((dm.inw, BLOCK), cur), pl.BlockSpec((kvw2, BLOCK), prev_kv_out),
                   full((dm.hd, 1)), full((dm.hd, 1)), full((dm.nh, 1)), full((dm.gw, 1)), full((dm.gw, 1)), full(ws.shape),
                   full(bs.shape), full((dm.aw, 1)), full((dm.gw, 1))],
        out_shape=[jax.ShapeDtypeStruct((dm.inw, t), MXU), jax.ShapeDtypeStruct((kvw2, t), MXU),
                   col(dm.hd), col(dm.hd), col(dm.nh), col(dm.gw), col(dm.gw), jax.ShapeDtypeStruct(ws.shape, F32),
                   jax.ShapeDtypeStruct(bs.shape, F32), col(dm.aw), col(dm.gw)],
        scratch_shapes=[pltpu.VMEM((dm.aw, BLOCK), F32), pltpu.VMEM((dm.aw, BLOCK), F32),
                        pltpu.VMEM((kvw2, BLOCK), F32), pltpu.VMEM((kvw2, BLOCK), F32), pltpu.VMEM((kvw2, BLOCK), F32),
                        pltpu.VMEM((dm.kvw, BLOCK), F32),
                        pltpu.VMEM((dm.hd, dm.grp * BLOCK), F32), lane(dm.hd), lane(dm.nh), lane(dm.gw), lane(dm.gw), lane(dm.aw), lane(dm.gw)],
        compiler_params=_cparams(("arbitrary", "arbitrary"), 48),
    )(proj_t, proj_t, dy_t, gq, gk2, sinks, lng, lnb, ws, bs, goa, gog)


def _patch_kv(dproj_t, dkv_t, dm):
    t = dproj_t.shape[1]
    tc = _pick(t, (1024, 512, 256, 128))
    kvw2 = 2 * dm.kvw
    kvblk = dm.aw // kvw2

    def body(dproj_hbm, dkv_ref, out_ref):
        del dproj_hbm
        out_ref[...] = dkv_ref[...]

    return pl.pallas_call(
        body, name="patch_kv", grid=(t // tc,),
        in_specs=[ANY, pl.BlockSpec((kvw2, tc), lambda i: (0, i))],
        out_specs=pl.BlockSpec((kvw2, tc), lambda i: (kvblk, i)),
        out_shape=jax.ShapeDtypeStruct(dproj_t.shape, dproj_t.dtype),
        input_output_aliases={0: 0},
        compiler_params=_cparams(("parallel",), 32),
    )(dproj_t, dkv_t)


def _place():
    x, y, c = lax.axis_index("x"), lax.axis_index("y"), lax.axis_index("c")
    return x, y, c


def _handshake(peers):
    barrier = pltpu.get_barrier_semaphore()
    for p in peers:
        pl.semaphore_signal(barrier, inc=1, device_id=p, device_id_type=MESH)
    pl.semaphore_wait(barrier, len(peers))


def _sequencer_mesh():
    return plsc.ScalarSubcoreMesh(axis_name="sequencer", num_cores=1)


GATHER_CHUNKS = 4
BF16_ROWS = 16


def _row_chunks(n, k):
    tiles = n // BF16_ROWS
    sizes = [(tiles // k + (1 if i < tiles % k else 0)) * BF16_ROWS for i in range(k)]
    return [(sum(sizes[:i]), sz) for i, sz in enumerate(sizes) if sz]


def _allgather_weight(name, collective_id, shard):
    n = shard.shape[0]
    assert n % BF16_ROWS == 0
    chunks = _row_chunks(n, GATHER_CHUNKS)
    nc = len(chunks)

    def body(src, out, send_sems, recv_sems, local_sem):
        x, y, c = _place()
        me, sib, xn, yn, diag = (x, y, c), (x, y, 1 - c), (1 - x, y, c), (x, 1 - y, c), (1 - x, 1 - y, c)
        relay_to = (x ^ c, y ^ (1 - c), c)
        relay_of = (x ^ (1 - c), y ^ c, c)
        _handshake([sib, xn, yn])

        def rows(place, ci):
            px, py, pc = place
            off, size = chunks[ci]
            return out.at[pl.ds(pl.multiple_of((4 * px + 2 * py + pc) * n + off, BF16_ROWS), size), :]

        def copy(k, ci, block, to, from_src=False):
            off, size = chunks[ci]
            return pltpu.make_async_remote_copy(
                src_ref=src.at[pl.ds(off, size), :] if from_src else rows(block, ci), dst_ref=rows(block, ci),
                send_sem=send_sems.at[ci, k], recv_sem=recv_sems.at[ci, k], device_id=to, device_id_type=MESH)

        mine = pltpu.make_async_copy(src, out.at[pl.ds(pl.multiple_of((4 * x + 2 * y + c) * n, BF16_ROWS), n), :], local_sem)
        mine.start()
        sent = []
        for ci in range(nc):
            sent += [copy(0, ci, me, sib, from_src=True), copy(1, ci, me, xn, from_src=True), copy(2, ci, me, yn, from_src=True)]
        for cp in sent:
            cp.start()
        for ci in range(nc):
            copy(1, ci, xn, me).wait_recv()
            copy(2, ci, yn, me).wait_recv()
            passed = [copy(3, ci, relay_of, relay_to), copy(4, ci, xn, sib), copy(5, ci, yn, sib)]
            for cp in passed:
                cp.start()
            sent += passed
        for ci in range(nc):
            copy(3, ci, diag, me).wait_recv()
            passed = copy(6, ci, diag, sib)
            passed.start()
            sent.append(passed)
        for ci in range(nc):
            copy(0, ci, sib, me).wait_recv()
            for k, block in ((4, (1 - x, y, 1 - c)), (5, (x, 1 - y, 1 - c)), (6, (1 - x, 1 - y, 1 - c))):
                copy(k, ci, block, me).wait_recv()
        for cp in sent:
            cp.wait_send()
        mine.wait()

    return pl.kernel(
        body, name=name,
        out_type=jax.ShapeDtypeStruct((N_DEV * n, shard.shape[1]), shard.dtype),
        mesh=_sequencer_mesh(),
        scratch_types=[pltpu.SemaphoreType.DMA((nc, 7)), pltpu.SemaphoreType.DMA((nc, 7)), pltpu.SemaphoreType.DMA],
        compiler_params=pltpu.CompilerParams(collective_id=collective_id),
    )(shard)


_FLIPS = [(0, 0, 1), (1, 0, 0), (0, 1, 0), (1, 1, 0), (1, 0, 1), (0, 1, 1), (1, 1, 1)]


def _scatter_grad(name, collective_id, grad):
    n = grad.shape[0] // N_DEV

    def body(src, out, send_sems, recv_sems, local_sem):
        x, y, c = _place()
        me_idx = 4 * x + 2 * y + c
        peers = [(x ^ fx, y ^ fy, c ^ fc) for (fx, fy, fc) in _FLIPS]
        _handshake(peers)

        def block(idx):
            return src.at[pl.ds(pl.multiple_of(idx * n, 16), n), :]

        copies = [pltpu.make_async_remote_copy(
            src_ref=block(4 * px + 2 * py + pc), dst_ref=out.at[me_idx], send_sem=send_sems.at[k], recv_sem=recv_sems.at[k],
            device_id=(px, py, pc), device_id_type=MESH) for k, (px, py, pc) in enumerate(peers)]
        mine = pltpu.make_async_copy(block(me_idx), out.at[me_idx], local_sem)
        mine.start()
        for cp in copies:
            cp.start()
        for cp in copies:
            cp.wait_recv()
        for cp in copies:
            cp.wait_send()
        mine.wait()

    return pl.kernel(
        body, name=name,
        out_type=jax.ShapeDtypeStruct((N_DEV, n, grad.shape[1]), grad.dtype),
        mesh=_sequencer_mesh(),
        scratch_types=[pltpu.SemaphoreType.DMA((7,)), pltpu.SemaphoreType.DMA((7,)), pltpu.SemaphoreType.DMA],
        compiler_params=pltpu.CompilerParams(collective_id=collective_id),
    )(grad)


def _allgather_rows(name, collective_id, part):
    def body(src, out, send_sems, recv_sems, local_sem):
        x, y, c = _place()
        me_idx = 4 * x + 2 * y + c
        peers = [(x ^ fx, y ^ fy, c ^ fc) for (fx, fy, fc) in _FLIPS]
        _handshake(peers)
        copies = [pltpu.make_async_remote_copy(
            src_ref=src, dst_ref=out.at[me_idx], send_sem=send_sems.at[k], recv_sem=recv_sems.at[k],
            device_id=peer, device_id_type=MESH) for k, peer in enumerate(peers)]
        mine = pltpu.make_async_copy(src, out.at[me_idx], local_sem)
        mine.start()
        for cp in copies:
            cp.start()
        for cp in copies:
            cp.wait_recv()
        for cp in copies:
            cp.wait_send()
        mine.wait()

    return pl.kernel(
        body, name=name,
        out_type=jax.ShapeDtypeStruct((N_DEV,) + part.shape, part.dtype),
        mesh=_sequencer_mesh(),
        scratch_types=[pltpu.SemaphoreType.DMA((7,)), pltpu.SemaphoreType.DMA((7,)), pltpu.SemaphoreType.DMA],
        compiler_params=pltpu.CompilerParams(collective_id=collective_id),
    )(part)


def _adamw_math(w, g, m, v):
    m = ADAM_B1 * m + (1.0 - ADAM_B1) * g
    v = ADAM_B2 * v + (1.0 - ADAM_B2) * (g * g)
    m_hat = m / (1.0 - ADAM_B1 ** ADAM_STEP)
    v_hat = v / (1.0 - ADAM_B2 ** ADAM_STEP)
    delta = -ADAM_LR * (m_hat / (jnp.sqrt(v_hat) + ADAM_EPS) + ADAM_WD * w)
    return delta, m, v


def _sum_adamw(name, slots, w, m, v, after):
    _, n, kk = slots.shape
    tr = _pick(n, (208, 176, 128, 96, 64, 32, 16))

    def body(s_ref, w_ref, m_ref, v_ref, after_ref, g_ref, d_ref, nm_ref, nv_ref):
        del after_ref
        g = s_ref[0].astype(F32)
        for p in range(1, N_DEV):
            g = g + s_ref[p].astype(F32)
        g_ref[...] = g
        d_ref[...], nm_ref[...], nv_ref[...] = _adamw_math(w_ref[...], g, m_ref[...], v_ref[...])

    row = pl.BlockSpec((tr, kk), lambda i: (i, 0))
    return pl.pallas_call(
        body, name=name, grid=(n // tr,),
        in_specs=[pl.BlockSpec((N_DEV, tr, kk), lambda i: (0, i, 0)), row, row, row, ANY],
        out_specs=[row] * 4,
        out_shape=[jax.ShapeDtypeStruct((n, kk), F32)] * 4,
        compiler_params=_cparams(("parallel",), 48),
    )(slots, w, m, v, after)


def _allreduce_small_adamw(early_slots, late, w, m, v, after):
    ra, rb = early_slots.shape[1], late.shape[0]

    def body(early_ref, late_ref, w_ref, m_ref, v_ref, after_ref, g_ref, d_ref, nm_ref, nv_ref, slots, send_sems, recv_sems):
        del after_ref
        x, y, c = _place()
        me_idx = 4 * x + 2 * y + c
        copies = []
        for k, (fx, fy, fc) in enumerate(_FLIPS):
            px, py, pc = x ^ fx, y ^ fy, c ^ fc
            copies.append(pltpu.make_async_remote_copy(
                src_ref=late_ref, dst_ref=slots.at[me_idx], send_sem=send_sems.at[k], recv_sem=recv_sems.at[k],
                device_id=(px, py, pc), device_id_type=MESH))
        for cp in copies:
            cp.start()
        slots[me_idx] = late_ref[...]
        g = early_ref[0]
        for p in range(1, N_DEV):
            g = g + early_ref[p]
        early = pl.ds(0, ra)
        g_ref[early, :] = g
        d_ref[early, :], nm_ref[early, :], nv_ref[early, :] = _adamw_math(w_ref[early, :], g, m_ref[early, :], v_ref[early, :])
        for cp in copies:
            cp.wait_recv()
        for cp in copies:
            cp.wait_send()
        g = slots[0]
        for p in range(1, N_DEV):
            g = g + slots[p]
        tail = pl.ds(ra, rb)
        g_ref[tail, :] = g
        d_ref[tail, :], nm_ref[tail, :], nv_ref[tail, :] = _adamw_math(w_ref[tail, :], g, m_ref[tail, :], v_ref[tail, :])

    vm = pl.BlockSpec(memory_space=pltpu.VMEM)
    return pl.pallas_call(
        body, name="allreduce_small_adamw",
        in_specs=[vm] * 5 + [ANY], out_specs=[vm] * 4,
        out_shape=[jax.ShapeDtypeStruct((ra + rb, LANES), F32)] * 4,
        scratch_shapes=[pltpu.VMEM((N_DEV, rb, LANES), F32), pltpu.SemaphoreType.DMA((7,)), pltpu.SemaphoreType.DMA((7,))],
        compiler_params=pltpu.CompilerParams(vmem_limit_bytes=48 * MIB),
    )(early_slots, late, w, m, v, after)


def _pack(arrs):
    parts, meta, off = [], [], 0
    for a in arrs:
        flat = a.reshape(-1).astype(F32)
        rows = -(-flat.shape[0] // LANES)
        rows8 = -(-rows // 8) * 8
        flat = jnp.pad(flat, (0, rows8 * LANES - flat.shape[0]))
        parts.append(flat.reshape(rows8, LANES))
        meta.append((off, a.shape, a.size))
        off += rows8
    return jnp.concatenate(parts, axis=0), meta


def _unpack(packed, meta):
    outs = []
    for off, shape, size in meta:
        rows = -(-size // LANES)
        outs.append(packed[off:off + rows].reshape(-1)[:size].reshape(shape))
    return outs


def _silu_parts(a):
    sg = 0.5 + 0.5 * jnp.tanh(0.5 * a)
    return a * sg, sg * (1.0 + a * (1.0 - sg))


def kernel(x, norm1_g, w_in, q_norm_g, k_norm_g, attn_sinks, gate_ln_g, gate_ln_b, w_spatial, b_spatial, out_norm_attn_g, out_norm_gate_g, w_out, norm2_g, w_ffn_gate, w_ffn_up, w_ffn_down, loss_target, m_norm1_g, m_w_in, m_q_norm_g, m_k_norm_g, m_attn_sinks, m_gate_ln_g, m_gate_ln_b, m_w_spatial, m_b_spatial, m_out_norm_attn_g, m_out_norm_gate_g, m_w_out, m_norm2_g, m_w_ffn_gate, m_w_ffn_up, m_w_ffn_down, v_norm1_g, v_w_in, v_q_norm_g, v_k_norm_g, v_attn_sinks, v_gate_ln_g, v_gate_ln_b, v_w_spatial, v_b_spatial, v_out_norm_attn_g, v_out_norm_gate_g, v_w_out, v_norm2_g, v_w_ffn_gate, v_w_ffn_up, v_w_ffn_down):
    nseq, seq, d = x.shape
    t = nseq * seq
    nb = seq // BLOCK
    inw = w_in.shape[2] * N_DEV
    dm = _Dims(d, inw, q_norm_g.shape[-1])
    xf = x.reshape(t, d)
    tgt = loss_target.reshape(t, d)

    rows = lambda wv, transposed: jnp.swapaxes(wv, 1, 2)[0] if transposed else wv[0]
    big = {"w_in": (w_in, m_w_in, v_w_in, True), "w_out": (w_out, m_w_out, v_w_out, False),
           "w_ffn_gate": (w_ffn_gate, m_w_ffn_gate, v_w_ffn_gate, True), "w_ffn_up": (w_ffn_up, m_w_ffn_up, v_w_ffn_up, True),
           "w_ffn_down": (w_ffn_down, m_w_ffn_down, v_w_ffn_down, False)}
    big_rows = {nm: tuple(rows(arr, tr) for arr in (wv, mv, vv)) for nm, (wv, mv, vv, tr) in big.items()}
    shard = lambda nm: big_rows[nm][0].astype(WIRE)
    win_t = _allgather_weight("gather_w_in", 1, shard("w_in"))
    wout = _allgather_weight("gather_w_out", 2, shard("w_out"))
    wu_t = _allgather_weight("gather_w_ffn_up", 9, shard("w_ffn_up"))
    wg_t = _allgather_weight("gather_w_ffn_gate", 3, shard("w_ffn_gate"))
    wd = _allgather_weight("gather_w_ffn_down", 10, shard("w_ffn_down"))

    lanes = lambda v, n=BLOCK: jnp.broadcast_to(v.reshape(-1, 1), (v.size, n))
    prm = (lanes(q_norm_g, dm.grp * BLOCK), lanes(k_norm_g, 2 * BLOCK), attn_sinks[0], lanes(gate_ln_g), lanes(gate_ln_b), w_spatial[0], b_spatial[0],
           lanes(out_norm_attn_g), lanes(out_norm_gate_g))

    h1 = _rms_fwd("rms1_fwd", xf, norm1_g)
    (proj_t,) = _matmul("mm_in", win_t, h1, "nt", [F32])
    y_t = _mixer_fwd(proj_t, prm, dm, nseq, nb)

    def residual_norm(acc, xr, g2):
        x2v = xr + acc
        return x2v, x2v * lax.rsqrt(jnp.mean(x2v * x2v, axis=-1, keepdims=True) + EPS) * g2

    x2, h2 = _matmul("mm_out", y_t, wout, "tn", [F32, MXU], epilogue=residual_norm, extras=[xf], rowvecs=[norm2_g], full_rows=True)
    (b,) = _matmul("mm_up", h2, wu_t, "nt", [MXU])

    def swiglu(ga, ub):
        silu, dsilu = _silu_parts(ga)
        ub = ub.astype(F32)
        return silu * ub, silu, dsilu * ub

    s, silu_a, dsilu_a_b = _matmul("mm_gate", h2, wg_t, "nt", [MXU, MXU, MXU], epilogue=swiglu, extras=[b])

    def loss_epilogue(acc, x2v, tv):
        diff = (x2v + acc) - tv
        dx3 = diff * (1.0 / d)
        return dx3, dx3, jnp.sum(diff * diff)

    dx3, dx3b, lossp = _matmul("mm_down", s, wd, "nn", [F32, MXU], epilogue=loss_epilogue, extras=[x2, tgt], partial=True)
    loss_part = (0.5 / d) * jnp.sum(lossp[::8, ::LANES])

    def dswiglu(ds, dsilu_b, silu):
        return ds * dsilu_b.astype(F32), ds * silu.astype(F32)

    (g_wd,) = _matmul("mm_gw_down", s, dx3b, "tn", [WIRE])
    sl_wd = _scatter_grad("scatter_w_ffn_down", 4, g_wd)
    da, db = _matmul("mm_d_down", dx3b, wd, "nt", [MXU, MXU], epilogue=dswiglu, extras=[dsilu_a_b, silu_a], after=[g_wd])
    (g_wg,) = _matmul("mm_gw_gate", da, h2, "tn", [WIRE])
    sl_wg = _scatter_grad("scatter_w_ffn_gate", 5, g_wg)
    (g_wu,) = _matmul("mm_gw_up", db, h2, "tn", [WIRE], after=[g_wg])
    sl_wu = _scatter_grad("scatter_w_ffn_up", 6, g_wu)
    (dh2a,) = _matmul("mm_dh2_gate", da, wg_t, "nn", [F32], after=[g_wu])
    (dh2,) = _matmul("mm_dh2_up", db, wu_t, "nn", [F32], epilogue=lambda acc, pv: (pv + acc,), extras=[dh2a])

    dy_t, dx2, dx2b, dg2 = _norm_bwd_matmul("mm_d_out", wout, dh2, x2, norm2_g, dx3, after=dh2a)
    (g_wout,) = _matmul("mm_gw_out", y_t, dx2b, "nn", [WIRE], after=[dy_t])
    sl_wout = _scatter_grad("scatter_w_out", 7, g_wout)
    (dproj0, dkv, dgq, dgk, dsink, dlng, dlnb, dws, dbs, dgoa, dgog) = _mixer_bwd(proj_t, dy_t, prm, dm, nseq, nb)
    early_g = [dgq, dgk, dsink, dlng, dlnb, dws, dbs, dgoa, dgog, dg2, loss_part.reshape(1)]
    early_slots = _allgather_rows("gather_small_grads", 11, _pack(early_g)[0])
    dproj_t = _patch_kv(dproj0, dkv, dm)
    (g_win,) = _matmul("mm_gw_in", dproj_t, h1, "nn", [WIRE])
    sl_win = _scatter_grad("scatter_w_in", 8, g_win)

    def norm1_backward(dh1, xv, dx2v, g1):
        r = lax.rsqrt(jnp.mean(xv * xv, axis=-1, keepdims=True) + EPS)
        xh = xv * r
        dxh = dh1 * g1
        return dx2v + r * (dxh - xh * jnp.mean(dxh * xh, axis=-1, keepdims=True)), jnp.sum(dh1 * xh, axis=0, keepdims=True)

    dx, dg1 = _matmul("mm_d_in", dproj_t, win_t, "tn", [F32], epilogue=norm1_backward, extras=[xf, dx2], rowvecs=[norm1_g],
                      after=[g_win], col_sum=True, full_rows=True)

    big_out = {}
    last = dx

    def big_update(nm, sl, after):
        res = _sum_adamw("adamw_" + nm, sl, *big_rows[nm], after=after)
        big_out[nm] = tuple(jnp.swapaxes(r[None], 1, 2) if big[nm][3] else r[None] for r in res)
        return res[1]

    for nm, sl in (("w_ffn_down", sl_wd), ("w_ffn_gate", sl_wg), ("w_ffn_up", sl_wu), ("w_out", sl_wout)):
        last = big_update(nm, sl, last)

    zero = jnp.zeros((1,), F32)
    small_names = ["q_norm_g", "k_norm_g", "attn_sinks", "gate_ln_g", "gate_ln_b", "w_spatial", "b_spatial",
                   "out_norm_attn_g", "out_norm_gate_g", "norm2_g", "loss", "norm1_g"]
    small_w = [q_norm_g, k_norm_g, attn_sinks, gate_ln_g, gate_ln_b, w_spatial, b_spatial, out_norm_attn_g, out_norm_gate_g, norm2_g, zero, norm1_g]
    small_m = [m_q_norm_g, m_k_norm_g, m_attn_sinks, m_gate_ln_g, m_gate_ln_b, m_w_spatial, m_b_spatial, m_out_norm_attn_g, m_out_norm_gate_g, m_norm2_g, zero, m_norm1_g]
    small_v = [v_q_norm_g, v_k_norm_g, v_attn_sinks, v_gate_ln_g, v_gate_ln_b, v_w_spatial, v_b_spatial, v_out_norm_attn_g, v_out_norm_gate_g, v_norm2_g, zero, v_norm1_g]
    pw, meta = _pack(small_w)
    sg, sd, sm, sv = _allreduce_small_adamw(early_slots, _pack([dg1])[0], pw, _pack(small_m)[0], _pack(small_v)[0], after=last)
    big_update("w_in", sl_win, sd)
    ug, ud, um, uv = _unpack(sg, meta), _unpack(sd, meta), _unpack(sm, meta), _unpack(sv, meta)
    small_out = {nm: (ug[k], ud[k], um[k], uv[k]) for k, nm in enumerate(small_names)}
    loss = small_out["loss"][0].reshape(())

    order = ["norm1_g", "w_in", "q_norm_g", "k_norm_g", "attn_sinks", "gate_ln_g", "gate_ln_b", "w_spatial", "b_spatial",
             "out_norm_attn_g", "out_norm_gate_g", "w_out", "norm2_g", "w_ffn_gate", "w_ffn_up", "w_ffn_down"]
    allo = {**big_out, **small_out}
    outs = [loss, dx.reshape(nseq, seq, d)]
    for k in range(4):
        outs += [allo[nm][k] for nm in order]
    return tuple(outs)
```

```python
import math

import jax
import jax.numpy as jnp
from jax import lax
from jax.experimental import pallas as pl
from jax.experimental.pallas import tpu as pltpu
from jax.experimental.pallas import tpu_sc as plsc

F32 = jnp.float32
MXU = jnp.bfloat16
WIRE = jnp.bfloat16
EPS = 1e-6
BLOCK = 128
GROUP_DIM = 128
N_KV_HEADS = 2
NEG = -1e30
N_DEV = 8
LANES = 128
MIB = 1024 * 1024

ADAM_LR = 0.001
ADAM_B1 = 0.9
ADAM_B2 = 0.999
ADAM_EPS = 1e-08
ADAM_WD = 0.01
ADAM_STEP = 10

MESH = pl.DeviceIdType.MESH
ANY = pl.BlockSpec(memory_space=pl.ANY)


def _pick(n, cands):
    for c in cands:
        if n % c == 0:
            return c
    return n


def _cparams(sem, vmem_mb):
    return pltpu.CompilerParams(dimension_semantics=sem, vmem_limit_bytes=vmem_mb * MIB)


VMEM_TILE_BUDGET = 50 * MIB
HBM_BYTES_PER_US = 3.0e6
STEP_US = 0.4
MXU_COLS = 256
MIN_TILE_N = 2 * MXU_COLS


def _tile_candidates(n):
    return [c for c in range(min(n, 2048), 0, -LANES) if n % c == 0 and c % LANES == 0] or [n]


def _matmul_tiles(m, n, kk, esz, extra_sizes, out_sizes, full_rows):
    best = None
    wide = [n] if full_rows else [c for c in _tile_candidates(n) if c >= MIN_TILE_N and c % MXU_COLS == 0] or _tile_candidates(n)
    for tm in _tile_candidates(m):
        for tn in wide:
            vmem = 2 * (tm + tn) * kk * esz + tm * tn * (4 + 2 * sum(extra_sizes) + 2 * sum(out_sizes))
            if vmem > VMEM_TILE_BUDGET:
                continue
            cost = (m // tm) * n * kk * esz / HBM_BYTES_PER_US + (m // tm) * (n // tn) * STEP_US
            if best is None or cost < best[0]:
                best = (cost, tm, tn, vmem)
    assert best is not None, (m, n, kk)
    return best[1:]


def _matmul(name, a, b, mode, out_dtypes, epilogue=None, extras=(), rowvecs=(), after=(), partial=False, col_sum=False, full_rows=False):
    if mode == "nn":
        (m, kk), n = a.shape, b.shape[1]
        dn = (((1,), (0,)), ((), ()))
    elif mode == "nt":
        (m, kk), n = a.shape, b.shape[0]
        dn = (((1,), (1,)), ((), ()))
    else:
        (kk, m), n = a.shape, b.shape[1]
        dn = (((0,), (0,)), ((), ()))
    tm, tn, vmem = _matmul_tiles(m, n, kk, a.dtype.itemsize, [e.dtype.itemsize for e in extras],
                                 [jnp.dtype(dt).itemsize for dt in out_dtypes], full_rows)
    a_spec = pl.BlockSpec((kk, tm), lambda i, j: (0, i)) if mode == "tn" else pl.BlockSpec((tm, kk), lambda i, j: (i, 0))
    b_spec = pl.BlockSpec((tn, kk), lambda i, j: (j, 0)) if mode == "nt" else pl.BlockSpec((kk, tn), lambda i, j: (0, j))
    tile = pl.BlockSpec((tm, tn), lambda i, j: (i, j))
    row = pl.BlockSpec((1, tn), lambda i, j: (0, j))
    ne, nr, na, no = len(extras), len(rowvecs), len(after), len(out_dtypes)

    def body(a_ref, b_ref, *rest):
        in_refs, out_refs = rest[:ne + nr], rest[ne + nr + na:]
        acc = lax.dot_general(a_ref[...], b_ref[...], dn, preferred_element_type=F32)
        vals = (acc,) if epilogue is None else epilogue(acc, *[r[...] for r in in_refs])
        for o_ref, t in zip(out_refs[:no], vals[:no]):
            o_ref[...] = t.astype(o_ref.dtype)
        if partial:
            out_refs[no][...] = jnp.full((8, LANES), vals[no], F32)
        if col_sum:
            sum_ref = out_refs[-1]

            @pl.when(pl.program_id(0) == 0)
            def _():
                sum_ref[...] = jnp.zeros_like(sum_ref)

            sum_ref[...] += vals[-1]

    out_specs = [tile] * no
    out_shape = [jax.ShapeDtypeStruct((m, n), dt) for dt in out_dtypes]
    if partial:
        out_specs.append(pl.BlockSpec((8, LANES), lambda i, j: (i, j)))
        out_shape.append(jax.ShapeDtypeStruct((m // tm * 8, n // tn * LANES), F32))
    if col_sum:
        out_specs.append(row)
        out_shape.append(jax.ShapeDtypeStruct((1, n), F32))
    return pl.pallas_call(
        body, name=name, grid=(m // tm, n // tn),
        in_specs=[a_spec, b_spec] + [tile] * ne + [row] * nr + [ANY] * na,
        out_specs=out_specs, out_shape=out_shape,
        compiler_params=_cparams(("arbitrary" if col_sum else "parallel", "arbitrary"), min(vmem // MIB + 8, 58)),
    )(a, b, *extras, *rowvecs, *after)


def _rms_fwd(name, x, g):
    t, d = x.shape
    tm = _pick(t, (512, 256, 128))

    def body(x_ref, g_ref, h_ref):
        xv = x_ref[...]
        r = lax.rsqrt(jnp.mean(xv * xv, axis=-1, keepdims=True) + EPS)
        h_ref[...] = (xv * r * g_ref[...]).astype(h_ref.dtype)

    return pl.pallas_call(
        body, name=name, grid=(t // tm,),
        in_specs=[pl.BlockSpec((tm, d), lambda i: (i, 0)), pl.BlockSpec((1, d), lambda i: (0, 0))],
        out_specs=pl.BlockSpec((tm, d), lambda i: (i, 0)),
        out_shape=jax.ShapeDtypeStruct((t, d), MXU),
        compiler_params=_cparams(("parallel",), 32),
    )(x, g)


def _norm_bwd_matmul(name, w, dh, x, g, res, after):
    t, d = x.shape
    m = w.shape[0]
    tn = _pick(t, (256, 128))

    def body(w_ref, dh_ref, x_ref, g_ref, res_ref, after_ref, out_ref, dx_ref, dxb_ref, dg_ref):
        del after_ref

        @pl.when(pl.program_id(0) == 0)
        def _():
            dg_ref[...] = jnp.zeros_like(dg_ref)

        xv, dhv = x_ref[...], dh_ref[...]
        r = lax.rsqrt(jnp.mean(xv * xv, axis=-1, keepdims=True) + EPS)
        xh = xv * r
        dg_ref[...] += jnp.sum(dhv * xh, axis=0, keepdims=True)
        dxh = dhv * g_ref[...]
        dx = res_ref[...] + r * (dxh - xh * jnp.mean(dxh * xh, axis=-1, keepdims=True))
        dx_ref[...] = dx
        dxb = dx.astype(MXU)
        dxb_ref[...] = dxb
        out_ref[...] = lax.dot_general(w_ref[...], dxb, (((1,), (1,)), ((), ())), preferred_element_type=F32)

    row = pl.BlockSpec((tn, d), lambda j: (j, 0))
    vec = pl.BlockSpec((1, d), lambda j: (0, 0))
    return pl.pallas_call(
        body, name=name, grid=(t // tn,),
        in_specs=[pl.BlockSpec((m, d), lambda j: (0, 0)), row, row, vec, row, ANY],
        out_specs=[pl.BlockSpec((m, tn), lambda j: (0, j)), row, row, vec],
        out_shape=[jax.ShapeDtypeStruct((m, t), F32), jax.ShapeDtypeStruct((t, d), F32), jax.ShapeDtypeStruct((t, d), MXU),
                   jax.ShapeDtypeStruct((1, d), F32)],
        compiler_params=_cparams(("arbitrary",), 52),
    )(w, dh, x, g, res, after)


_INV_SQRT2 = 0.7071067811865476
_INV_SQRT_2PI = 0.3989422804014327


def _dot_nt(a, b):
    return lax.dot_general(a, b, (((1,), (1,)), ((), ())), preferred_element_type=F32)


def _dot_tn(a, b):
    return lax.dot_general(a, b, (((0,), (0,)), ((), ())), preferred_element_type=F32)


def _dot(a, b):
    return jnp.dot(a, b, preferred_element_type=F32)


def _col_rms(v):
    return lax.rsqrt(jnp.mean(v * v, axis=0, keepdims=True) + EPS)


class _Dims:
    def __init__(self, d_model, in_width, head_dim):
        self.d = d_model
        self.aw = d_model // 2
        self.gw = d_model - self.aw
        self.kvw = (in_width - self.aw - 2 * self.gw) // 2
        self.hd = head_dim
        self.nh = self.aw // head_dim
        self.nkv = self.kvw // head_dim
        self.grp = self.nh // self.nkv
        self.ng = self.gw // GROUP_DIM
        self.inw = in_width
        self.zoff = self.aw + 2 * self.kvw
        assert self.nkv == N_KV_HEADS and self.zoff + 2 * self.gw == in_width and self.aw % (2 * self.kvw) == 0


def _band_masks(first):
    r = lax.broadcasted_iota(jnp.int32, (BLOCK, BLOCK), 0)
    t = lax.broadcasted_iota(jnp.int32, (BLOCK, BLOCK), 1)
    upper = r > t
    dist = t - r + jnp.where(upper, BLOCK, 0)
    return upper, jnp.logical_not(upper & first), dist.astype(F32)


def _fold(full, upper):
    return jnp.where(upper, full[:BLOCK], full[BLOCK:])


def _unfold(folded, upper):
    zero = jnp.zeros_like(folded)
    return jnp.concatenate([jnp.where(upper, folded, zero), jnp.where(upper, zero, folded)], axis=0)


def _kv_band(dm, kh, p_ref, pkv_ref, gk2):
    ko = dm.aw + kh * dm.hd
    vo = dm.aw + dm.kvw + kh * dm.hd
    k_t = jnp.concatenate([pkv_ref[kh * dm.hd:(kh + 1) * dm.hd, :], p_ref[ko:ko + dm.hd, :]], axis=1)
    v_t = jnp.concatenate([pkv_ref[dm.kvw + kh * dm.hd:dm.kvw + (kh + 1) * dm.hd, :], p_ref[vo:vo + dm.hd, :]], axis=1)
    kn_t = k_t * _col_rms(k_t) * gk2
    return kn_t.astype(MXU), kn_t.T.astype(MXU), v_t.astype(MXU), v_t.T.astype(MXU)


def _group_heads(dm, kh):
    return range(kh * dm.grp, (kh + 1) * dm.grp)


def _attn_group_fwd(dm, kh, p_ref, gq, kn, v_tb, sink_ref, masks):
    heads = _group_heads(dm, kh)
    q = jnp.concatenate([p_ref[h * dm.hd:(h + 1) * dm.hd, :] for h in heads], axis=1)
    rq = _col_rms(q)
    qh = q * rq
    qnb = (qh * gq).astype(MXU)
    upper, valid, dist = masks
    s = _dot(kn, qnb)
    probs, probs_b, sink_probs = [], [], []
    for g, h in enumerate(heads):
        slope, sink = math.pow(2.0, -8.0 * (h + 1) / dm.nh), sink_ref[h]
        logits = jnp.where(valid, _fold(s[:, g * BLOCK:(g + 1) * BLOCK], upper) * (dm.hd ** -0.5) - slope * dist, NEG)
        m = jnp.maximum(jnp.max(logits, axis=0, keepdims=True), sink)
        e = jnp.exp(logits - m)
        es = jnp.exp(sink - m)
        inv = 1.0 / (jnp.sum(e, axis=0, keepdims=True) + es)
        probs.append(e * inv)
        probs_b.append(_unfold(probs[g], upper).astype(MXU))
        sink_probs.append(es * inv)
    probs_b = jnp.concatenate(probs_b, axis=1)
    o = _dot(v_tb, probs_b)
    return o, probs, probs_b, sink_probs, rq, qh, qnb


def _gelu_cdf(z):
    return 0.5 * (1.0 + lax.erf(z * _INV_SQRT2))


def _by_group(v, ng):
    return v.reshape(ng, GROUP_DIM, v.shape[1])


def _gate_fwd(dm, p_ref, lng_ref, lnb_ref, ws_ref, bs_ref, tril):
    zu, zv = p_ref[dm.zoff:dm.zoff + dm.gw, :], p_ref[dm.zoff + dm.gw:dm.zoff + 2 * dm.gw, :]
    cu, cv = _gelu_cdf(zu), _gelu_cdf(zv)
    u, v = zu * cu, zv * cv
    v3 = _by_group(v, dm.ng)
    xc = v3 - jnp.mean(v3, axis=1, keepdims=True)
    rstd = lax.rsqrt(jnp.mean(xc * xc, axis=1, keepdims=True) + EPS)
    xh = (xc * rstd).reshape(dm.gw, BLOCK)
    vnb = (xh * lng_ref[...] + lnb_ref[...]).astype(MXU)
    wts = [jnp.where(tril, ws_ref[g], 0.0).astype(MXU) for g in range(dm.ng)]
    mixed = jnp.concatenate([_dot_nt(vnb[g * GROUP_DIM:(g + 1) * GROUP_DIM], wts[g]) + bs_ref[g:g + 1, :]
                             for g in range(dm.ng)], axis=0)
    return u * mixed, u, mixed, xh, rstd, vnb, wts, (zu, cu), (zv, cv)


def _mixer_specs(dm, nb, clamp):
    kvblk = dm.aw // (2 * dm.kvw)

    def cur(s, i):
        return (0, s * nb + clamp(i))

    def prev(s, i):
        return (kvblk, s * nb + jnp.maximum(clamp(i) - 1, 0))

    full = lambda shape: pl.BlockSpec(shape, lambda s, i: tuple(0 for _ in shape))
    return cur, prev, full


def _tril():
    return lax.broadcasted_iota(jnp.int32, (BLOCK, BLOCK), 0) >= lax.broadcasted_iota(jnp.int32, (BLOCK, BLOCK), 1)


def _mixer_fwd(proj_t, prm, dm, nseq, nb):
    gq, gk2, sinks, lng, lnb, ws, bs, goa, gog = prm
    t = proj_t.shape[1]
    cur, prev, full = _mixer_specs(dm, nb, lambda i: i)

    def body(p_ref, pkv_ref, gq_ref, gk_ref, sink_ref, lng_ref, lnb_ref, ws_ref, bs_ref, goa_ref, gog_ref, y_ref, att_scr):
        i = pl.program_id(1)
        masks = _band_masks(i == 0)
        gqv, gkv = gq_ref[...], gk_ref[...]
        for kh in range(dm.nkv):
            _, kn, v_tb, _ = _kv_band(dm, kh, p_ref, pkv_ref, gkv)
            o = _attn_group_fwd(dm, kh, p_ref, gqv, kn, v_tb, sink_ref, masks)[0]
            for g, h in enumerate(_group_heads(dm, kh)):
                att_scr[h * dm.hd:(h + 1) * dm.hd, :] = o[:, g * BLOCK:(g + 1) * BLOCK]
        att = att_scr[...]
        y_ref[:dm.aw, :] = (att * _col_rms(att) * goa_ref[...]).astype(y_ref.dtype)
        gt = _gate_fwd(dm, p_ref, lng_ref, lnb_ref, ws_ref, bs_ref, _tril())[0]
        y_ref[dm.aw:, :] = (gt * _col_rms(gt) * gog_ref[...]).astype(y_ref.dtype)

    return pl.pallas_call(
        body, name="mixer_fwd", grid=(nseq, nb),
        in_specs=[pl.BlockSpec((dm.inw, BLOCK), cur), pl.BlockSpec((2 * dm.kvw, BLOCK), prev),
                  full(gq.shape), full(gk2.shape), pl.BlockSpec(memory_space=pltpu.SMEM),
                  full(lng.shape), full(lnb.shape), full(ws.shape), full(bs.shape), full(goa.shape), full(gog.shape)],
        out_specs=pl.BlockSpec((dm.d, BLOCK), cur),
        out_shape=jax.ShapeDtypeStruct((dm.d, t), MXU),
        scratch_shapes=[pltpu.VMEM((dm.aw, BLOCK), F32)],
        compiler_params=_cparams(("parallel", "arbitrary"), 40),
    )(proj_t, proj_t, gq, gk2, sinks, lng, lnb, ws, bs, goa, gog)


def _mixer_bwd(proj_t, dy_t, prm, dm, nseq, nb):
    gq, gk2, sinks, lng, lnb, ws, bs, goa, gog = prm
    t = proj_t.shape[1]
    clamp = lambda i: jnp.minimum(i, nb - 1)
    cur, prev, full = _mixer_specs(dm, nb, clamp)
    kvw2 = 2 * dm.kvw

    def prev_kv_out(s, i):
        return (0, s * nb + jnp.maximum(i - 1, 0))

    def body(p_ref, pkv_ref, dy_ref, gq_ref, gk_ref, sink_ref, lng_ref, lnb_ref, ws_ref, bs_ref, goa_ref, gog_ref,
             dproj_ref, dkv_ref, dgq_ref, dgk_ref, dsink_ref, dlng_ref, dlnb_ref, dws_ref, dbs_ref, dgoa_ref, dgog_ref,
             att_scr, datt_scr, carry_scr, prevpart_scr, curpart_scr, kprev_scr,
             a_gq, a_gk, a_sink, a_lng, a_lnb, a_goa, a_gog):
        s_id, i = pl.program_id(0), pl.program_id(1)
        lane_accs = ((a_gq, dgq_ref), (a_gk, dgk_ref), (a_sink, dsink_ref), (a_lng, dlng_ref), (a_lnb, dlnb_ref),
                     (a_goa, dgoa_ref), (a_gog, dgog_ref))

        @pl.when((s_id == 0) & (i == 0))
        def _():
            for acc, _ in lane_accs:
                acc[...] = jnp.zeros_like(acc)
            dws_ref[...] = jnp.zeros_like(dws_ref)
            dbs_ref[...] = jnp.zeros_like(dbs_ref)

        gqv, gkv = gq_ref[...], gk_ref[...]

        @pl.when(i < nb)
        def _():
            masks = _band_masks(i == 0)
            upper = masks[0]
            kvs, fwd = [], []
            for kh in range(dm.nkv):
                kv = _kv_band(dm, kh, p_ref, pkv_ref, gkv)
                kvs.append(kv)
                fwd.append(_attn_group_fwd(dm, kh, p_ref, gqv, kv[1], kv[2], sink_ref, masks))
                for g, h in enumerate(_group_heads(dm, kh)):
                    att_scr[h * dm.hd:(h + 1) * dm.hd, :] = fwd[kh][0][:, g * BLOCK:(g + 1) * BLOCK]
            att = att_scr[...]
            dya = dy_ref[:dm.aw, :]
            ra = _col_rms(att)
            ah = att * ra
            a_goa[...] += dya * ah
            dah = dya * goa_ref[...]
            datt_scr[...] = ra * (dah - ah * jnp.mean(dah * ah, axis=0, keepdims=True))
            for kh in range(dm.nkv):
                kn_tb, kn, v_tb, vb = kvs[kh]
                _, probs, probs_b, sink_probs, rq, qh, qnb = fwd[kh]
                heads = _group_heads(dm, kh)
                do_b = jnp.concatenate([datt_scr[h * dm.hd:(h + 1) * dm.hd, :] for h in heads], axis=1).astype(MXU)
                dp = _dot(vb, do_b)
                ds = []
                for g, h in enumerate(heads):
                    p, dp_h = probs[g], _fold(dp[:, g * BLOCK:(g + 1) * BLOCK], upper)
                    delta = jnp.sum(p * dp_h, axis=0, keepdims=True)
                    ds.append(_unfold(p * (dp_h - delta) * (dm.hd ** -0.5), upper).astype(MXU))
                    a_sink[h:h + 1, :] += -(sink_probs[g] * delta)
                dsb = jnp.concatenate(ds, axis=1)
                dqn = _dot(kn_tb, dsb)
                dkn = _dot_nt(qnb, dsb)
                dvb = _dot_nt(do_b, probs_b)
                a_gq[...] += dqn * qh
                dqh = dqn * gqv
                dq = rq * (dqh - qh * jnp.mean(dqh * qh, axis=0, keepdims=True))
                for g, h in enumerate(heads):
                    dproj_ref[h * dm.hd:(h + 1) * dm.hd, :] = dq[:, g * BLOCK:(g + 1) * BLOCK].astype(dproj_ref.dtype)
                krows = slice(kh * dm.hd, (kh + 1) * dm.hd)
                vrows = slice(dm.kvw + kh * dm.hd, dm.kvw + (kh + 1) * dm.hd)
                prevpart_scr[krows, :] = dkn[:, :BLOCK]
                prevpart_scr[vrows, :] = dvb[:, :BLOCK]
                curpart_scr[krows, :] = dkn[:, BLOCK:]
                curpart_scr[vrows, :] = dvb[:, BLOCK:]
            dproj_ref[dm.aw:dm.zoff, :] = jnp.zeros((kvw2, BLOCK), dproj_ref.dtype)
            tril = _tril()
            gt, u, mixed, xh, rstd, vnb, wts, (zu, cu), (zv, cv) = _gate_fwd(dm, p_ref, lng_ref, lnb_ref, ws_ref, bs_ref, tril)
            dyg = dy_ref[dm.aw:, :]
            rg = _col_rms(gt)
            gh = gt * rg
            a_gog[...] += dyg * gh
            dgh = dyg * gog_ref[...]
            dgt = rg * (dgh - gh * jnp.mean(dgh * gh, axis=0, keepdims=True))
            du = dgt * mixed
            dmix = dgt * u
            dmixb = dmix.astype(MXU)
            dbs_ref[...] += jnp.sum(_by_group(dmix, dm.ng), axis=1)
            dvn = []
            for g in range(dm.ng):
                rows = slice(g * GROUP_DIM, (g + 1) * GROUP_DIM)
                dws_ref[g] += jnp.where(tril, _dot_tn(dmixb[rows], vnb[rows]), 0.0)
                dvn.append(_dot(dmixb[rows], wts[g]))
            dvn = jnp.concatenate(dvn, axis=0)
            a_lng[...] += dvn * xh
            a_lnb[...] += dvn
            dxh3, xh3 = _by_group(dvn * lng_ref[...], dm.ng), _by_group(xh, dm.ng)
            dv = (rstd * (dxh3 - jnp.mean(dxh3, axis=1, keepdims=True) - xh3 * jnp.mean(dxh3 * xh3, axis=1, keepdims=True))).reshape(dm.gw, BLOCK)
            dgu = cu + zu * (jnp.exp(-0.5 * zu * zu) * _INV_SQRT_2PI)
            dgv = cv + zv * (jnp.exp(-0.5 * zv * zv) * _INV_SQRT_2PI)
            dproj_ref[dm.zoff:dm.zoff + dm.gw, :] = (du * dgu).astype(dproj_ref.dtype)
            dproj_ref[dm.zoff + dm.gw:, :] = (dv * dgv).astype(dproj_ref.dtype)

        @pl.when(i == nb)
        def _():
            prevpart_scr[...] = jnp.zeros_like(prevpart_scr)

        @pl.when(i >= 1)
        def _():
            tot = carry_scr[...] + prevpart_scr[...]
            for kh in range(dm.nkv):
                krows = slice(kh * dm.hd, (kh + 1) * dm.hd)
                kraw = kprev_scr[krows, :]
                rk = _col_rms(kraw)
                khat = kraw * rk
                dkn = tot[krows, :]
                a_gk[...] += dkn * khat
                dkh = dkn * gkv[:, :BLOCK]
                dk = rk * (dkh - khat * jnp.mean(dkh * khat, axis=0, keepdims=True))
                dkv_ref[krows, :] = dk.astype(dkv_ref.dtype)
            dkv_ref[dm.kvw:, :] = tot[dm.kvw:, :].astype(dkv_ref.dtype)

        @pl.when(i < nb)
        def _():
            carry_scr[...] = curpart_scr[...]
            kprev_scr[...] = p_ref[dm.aw:dm.aw + dm.kvw, :]

        @pl.when((s_id == nseq - 1) & (i == nb))
        def _():
            for acc, out in lane_accs:
                out[...] = jnp.sum(acc[...], axis=1, keepdims=True)

    col = lambda rows: jax.ShapeDtypeStruct((rows, 1), F32)
    lane = lambda rows: pltpu.VMEM((rows, LANES), F32)
    return pl.pallas_call(
        body, name="mixer_bwd", grid=(nseq, nb + 1),
        in_specs=[pl.BlockSpec((dm.inw, BLOCK), cur), pl.BlockSpec((kvw2, BLOCK), prev), pl.BlockSpec((dm.d, BLOCK), cur),
                  full(gq.shape), full(gk2.shape), pl.BlockSpec(memory_space=pltpu.SMEM),
                  full(lng.shape), full(lnb.shape), full(ws.shape), full(bs.shape), full(goa.shape), full(gog.shape)],
        out_specs=[pl.BlockSpec((dm.inw, BLOCK), cur), pl.BlockSpec((kvw2, BLOCK), prev_kv_out),
                   full((dm.hd, 1)), full((dm.hd, 1)), full((dm.nh, 1)), full((dm.gw, 1)), full((dm.gw, 1)), full(ws.shape),
                   full(bs.shape), full((dm.aw, 1)), full((dm.gw, 1))],
        out_shape=[jax.ShapeDtypeStruct((dm.inw, t), MXU), jax.ShapeDtypeStruct((kvw2, t), MXU),
                   col(dm.hd), col(dm.hd), col(dm.nh), col(dm.gw), col(dm.gw), jax.ShapeDtypeStruct(ws.shape, F32),
                   jax.ShapeDtypeStruct(bs.shape, F32), col(dm.aw), col(dm.gw)],
        scratch_shapes=[pltpu.VMEM((dm.aw, BLOCK), F32), pltpu.VMEM((dm.aw, BLOCK), F32),
                        pltpu.VMEM((kvw2, BLOCK), F32), pltpu.VMEM((kvw2, BLOCK), F32), pltpu.VMEM((kvw2, BLOCK), F32),
                        pltpu.VMEM((dm.kvw, BLOCK), F32),
                        pltpu.VMEM((dm.hd, dm.grp * BLOCK), F32), lane(dm.hd), lane(dm.nh), lane(dm.gw), lane(dm.gw), lane(dm.aw), lane(dm.gw)],
        compiler_params=_cparams(("arbitrary", "arbitrary"), 48),
    )(proj_t, proj_t, dy_t, gq, gk2, sinks, lng, lnb, ws, bs, goa, gog)


def _patch_kv(dproj_t, dkv_t, dm):
    t = dproj_t.shape[1]
    tc = _pick(t, (1024, 512, 256, 128))
    kvw2 = 2 * dm.kvw
    kvblk = dm.aw // kvw2

    def body(dproj_hbm, dkv_ref, out_ref):
        del dproj_hbm
        out_ref[...] = dkv_ref[...]

    return pl.pallas_call(
        body, name="patch_kv", grid=(t // tc,),
        in_specs=[ANY, pl.BlockSpec((kvw2, tc), lambda i: (0, i))],
        out_specs=pl.BlockSpec((kvw2, tc), lambda i: (kvblk, i)),
        out_shape=jax.ShapeDtypeStruct(dproj_t.shape, dproj_t.dtype),
        input_output_aliases={0: 0},
        compiler_params=_cparams(("parallel",), 32),
    )(dproj_t, dkv_t)


def _place():
    x, y, c = lax.axis_index("x"), lax.axis_index("y"), lax.axis_index("c")
    return x, y, c


def _handshake(peers):
    barrier = pltpu.get_barrier_semaphore()
    for p in peers:
        pl.semaphore_signal(barrier, inc=1, device_id=p, device_id_type=MESH)
    pl.semaphore_wait(barrier, len(peers))


def _sequencer_mesh():
    return plsc.ScalarSubcoreMesh(axis_name="sequencer", num_cores=1)


GATHER_CHUNKS = 4
BF16_ROWS = 16


def _row_chunks(n, k):
    tiles = n // BF16_ROWS
    sizes = [(tiles // k + (1 if i < tiles % k else 0)) * BF16_ROWS for i in range(k)]
    return [(sum(sizes[:i]), sz) for i, sz in enumerate(sizes) if sz]


def _allgather_weight(name, collective_id, shard):
    n = shard.shape[0]
    assert n % BF16_ROWS == 0
    chunks = _row_chunks(n, GATHER_CHUNKS)
    nc = len(chunks)

    def body(src, out, send_sems, recv_sems, local_sem):
        x, y, c = _place()
        me, sib, xn, yn, diag = (x, y, c), (x, y, 1 - c), (1 - x, y, c), (x, 1 - y, c), (1 - x, 1 - y, c)
        relay_to = (x ^ c, y ^ (1 - c), c)
        relay_of = (x ^ (1 - c), y ^ c, c)
        _handshake([sib, xn, yn])

        def rows(place, ci):
            px, py, pc = place
            off, size = chunks[ci]
            return out.at[pl.ds(pl.multiple_of((4 * px + 2 * py + pc) * n + off, BF16_ROWS), size), :]

        def copy(k, ci, block, to, from_src=False):
            off, size = chunks[ci]
            return pltpu.make_async_remote_copy(
                src_ref=src.at[pl.ds(off, size), :] if from_src else rows(block, ci), dst_ref=rows(block, ci),
                send_sem=send_sems.at[ci, k], recv_sem=recv_sems.at[ci, k], device_id=to, device_id_type=MESH)

        mine = pltpu.make_async_copy(src, out.at[pl.ds(pl.multiple_of((4 * x + 2 * y + c) * n, BF16_ROWS), n), :], local_sem)
        mine.start()
        sent = []
        for ci in range(nc):
            sent += [copy(0, ci, me, sib, from_src=True), copy(1, ci, me, xn, from_src=True), copy(2, ci, me, yn, from_src=True)]
        for cp in sent:
            cp.start()
        for ci in range(nc):
            copy(1, ci, xn, me).wait_recv()
            copy(2, ci, yn, me).wait_recv()
            passed = [copy(3, ci, relay_of, relay_to), copy(4, ci, xn, sib), copy(5, ci, yn, sib)]
            for cp in passed:
                cp.start()
            sent += passed
        for ci in range(nc):
            copy(3, ci, diag, me).wait_recv()
            passed = copy(6, ci, diag, sib)
            passed.start()
            sent.append(passed)
        for ci in range(nc):
            copy(0, ci, sib, me).wait_recv()
            for k, block in ((4, (1 - x, y, 1 - c)), (5, (x, 1 - y, 1 - c)), (6, (1 - x, 1 - y, 1 - c))):
                copy(k, ci, block, me).wait_recv()
        for cp in sent:
            cp.wait_send()
        mine.wait()

    return pl.kernel(
        body, name=name,
        out_type=jax.ShapeDtypeStruct((N_DEV * n, shard.shape[1]), shard.dtype),
        mesh=_sequencer_mesh(),
        scratch_types=[pltpu.SemaphoreType.DMA((nc, 7)), pltpu.SemaphoreType.DMA((nc, 7)), pltpu.SemaphoreType.DMA],
        compiler_params=pltpu.CompilerParams(collective_id=collective_id),
    )(shard)


_FLIPS = [(0, 0, 1), (1, 0, 0), (0, 1, 0), (1, 1, 0), (1, 0, 1), (0, 1, 1), (1, 1, 1)]


def _scatter_grad(name, collective_id, grad):
    n = grad.shape[0] // N_DEV

    def body(src, out, send_sems, recv_sems, local_sem):
        x, y, c = _place()
        me_idx = 4 * x + 2 * y + c
        peers = [(x ^ fx, y ^ fy, c ^ fc) for (fx, fy, fc) in _FLIPS]
        _handshake(peers)

        def block(idx):
            return src.at[pl.ds(pl.multiple_of(idx * n, 16), n), :]

        copies = [pltpu.make_async_remote_copy(
            src_ref=block(4 * px + 2 * py + pc), dst_ref=out.at[me_idx], send_sem=send_sems.at[k], recv_sem=recv_sems.at[k],
            device_id=(px, py, pc), device_id_type=MESH) for k, (px, py, pc) in enumerate(peers)]
        mine = pltpu.make_async_copy(block(me_idx), out.at[me_idx], local_sem)
        mine.start()
        for cp in copies:
            cp.start()
        for cp in copies:
            cp.wait_recv()
        for cp in copies:
            cp.wait_send()
        mine.wait()

    return pl.kernel(
        body, name=name,
        out_type=jax.ShapeDtypeStruct((N_DEV, n, grad.shape[1]), grad.dtype),
        mesh=_sequencer_mesh(),
        scratch_types=[pltpu.SemaphoreType.DMA((7,)), pltpu.SemaphoreType.DMA((7,)), pltpu.SemaphoreType.DMA],
        compiler_params=pltpu.CompilerParams(collective_id=collective_id),
    )(grad)


def _allgather_rows(name, collective_id, part):
    def body(src, out, send_sems, recv_sems, local_sem):
        x, y, c = _place()
        me_idx = 4 * x + 2 * y + c
        peers = [(x ^ fx, y ^ fy, c ^ fc) for (fx, fy, fc) in _FLIPS]
        _handshake(peers)
        copies = [pltpu.make_async_remote_copy(
            src_ref=src, dst_ref=out.at[me_idx], send_sem=send_sems.at[k], recv_sem=recv_sems.at[k],
            device_id=peer, device_id_type=MESH) for k, peer in enumerate(peers)]
        mine = pltpu.make_async_copy(src, out.at[me_idx], local_sem)
        mine.start()
        for cp in copies:
            cp.start()
        for cp in copies:
            cp.wait_recv()
        for cp in copies:
            cp.wait_send()
        mine.wait()

    return pl.kernel(
        body, name=name,
        out_type=jax.ShapeDtypeStruct((N_DEV,) + part.shape, part.dtype),
        mesh=_sequencer_mesh(),
        scratch_types=[pltpu.SemaphoreType.DMA((7,)), pltpu.SemaphoreType.DMA((7,)), pltpu.SemaphoreType.DMA],
        compiler_params=pltpu.CompilerParams(collective_id=collective_id),
    )(part)


def _adamw_math(w, g, m, v):
    m = ADAM_B1 * m + (1.0 - ADAM_B1) * g
    v = ADAM_B2 * v + (1.0 - ADAM_B2) * (g * g)
    m_hat = m / (1.0 - ADAM_B1 ** ADAM_STEP)
    v_hat = v / (1.0 - ADAM_B2 ** ADAM_STEP)
    delta = -ADAM_LR * (m_hat / (jnp.sqrt(v_hat) + ADAM_EPS) + ADAM_WD * w)
    return delta, m, v


def _sum_adamw(name, slots, w, m, v, after):
    _, n, kk = slots.shape
    tr = _pick(n, (208, 176, 128, 96, 64, 32, 16))

    def body(s_ref, w_ref, m_ref, v_ref, after_ref, g_ref, d_ref, nm_ref, nv_ref):
        del after_ref
        g = s_ref[0].astype(F32)
        for p in range(1, N_DEV):
            g = g + s_ref[p].astype(F32)
        g_ref[...] = g
        d_ref[...], nm_ref[...], nv_ref[...] = _adamw_math(w_ref[...], g, m_ref[...], v_ref[...])

    row = pl.BlockSpec((tr, kk), lambda i: (i, 0))
    return pl.pallas_call(
        body, name=name, grid=(n // tr,),
        in_specs=[pl.BlockSpec((N_DEV, tr, kk), lambda i: (0, i, 0)), row, row, row, ANY],
        out_specs=[row] * 4,
        out_shape=[jax.ShapeDtypeStruct((n, kk), F32)] * 4,
        compiler_params=_cparams(("parallel",), 48),
    )(slots, w, m, v, after)


def _allreduce_small_adamw(early_slots, late, w, m, v, after):
    ra, rb = early_slots.shape[1], late.shape[0]

    def body(early_ref, late_ref, w_ref, m_ref, v_ref, after_ref, g_ref, d_ref, nm_ref, nv_ref, slots, send_sems, recv_sems):
        del after_ref
        x, y, c = _place()
        me_idx = 4 * x + 2 * y + c
        copies = []
        for k, (fx, fy, fc) in enumerate(_FLIPS):
            px, py, pc = x ^ fx, y ^ fy, c ^ fc
            copies.append(pltpu.make_async_remote_copy(
                src_ref=late_ref, dst_ref=slots.at[me_idx], send_sem=send_sems.at[k], recv_sem=recv_sems.at[k],
                device_id=(px, py, pc), device_id_type=MESH))
        for cp in copies:
            cp.start()
        slots[me_idx] = late_ref[...]
        g = early_ref[0]
        for p in range(1, N_DEV):
            g = g + early_ref[p]
        early = pl.ds(0, ra)
        g_ref[early, :] = g
        d_ref[early, :], nm_ref[early, :], nv_ref[early, :] = _adamw_math(w_ref[early, :], g, m_ref[early, :], v_ref[early, :])
        for cp in copies:
            cp.wait_recv()
        for cp in copies:
            cp.wait_send()
        g = slots[0]
        for p in range(1, N_DEV):
            g = g + slots[p]
        tail = pl.ds(ra, rb)
        g_ref[tail, :] = g
        d_ref[tail, :], nm_ref[tail, :], nv_ref[tail, :] = _adamw_math(w_ref[tail, :], g, m_ref[tail, :], v_ref[tail, :])

    vm = pl.BlockSpec(memory_space=pltpu.VMEM)
    return pl.pallas_call(
        body, name="allreduce_small_adamw",
        in_specs=[vm] * 5 + [ANY], out_specs=[vm] * 4,
        out_shape=[jax.ShapeDtypeStruct((ra + rb, LANES), F32)] * 4,
        scratch_shapes=[pltpu.VMEM((N_DEV, rb, LANES), F32), pltpu.SemaphoreType.DMA((7,)), pltpu.SemaphoreType.DMA((7,))],
        compiler_params=pltpu.CompilerParams(vmem_limit_bytes=48 * MIB),
    )(early_slots, late, w, m, v, after)


def _pack(arrs):
    parts, meta, off = [], [], 0
    for a in arrs:
        flat = a.reshape(-1).astype(F32)
        rows = -(-flat.shape[0] // LANES)
        rows8 = -(-rows // 8) * 8
        flat = jnp.pad(flat, (0, rows8 * LANES - flat.shape[0]))
        parts.append(flat.reshape(rows8, LANES))
        meta.append((off, a.shape, a.size))
        off += rows8
    return jnp.concatenate(parts, axis=0), meta


def _unpack(packed, meta):
    outs = []
    for off, shape, size in meta:
        rows = -(-size // LANES)
        outs.append(packed[off:off + rows].reshape(-1)[:size].reshape(shape))
    return outs


def _silu_parts(a):
    sg = 0.5 + 0.5 * jnp.tanh(0.5 * a)
    return a * sg, sg * (1.0 + a * (1.0 - sg))


def kernel(x, norm1_g, w_in, q_norm_g, k_norm_g, attn_sinks, gate_ln_g, gate_ln_b, w_spatial, b_spatial, out_norm_attn_g, out_norm_gate_g, w_out, norm2_g, w_ffn_gate, w_ffn_up, w_ffn_down, loss_target, m_norm1_g, m_w_in, m_q_norm_g, m_k_norm_g, m_attn_sinks, m_gate_ln_g, m_gate_ln_b, m_w_spatial, m_b_spatial, m_out_norm_attn_g, m_out_norm_gate_g, m_w_out, m_norm2_g, m_w_ffn_gate, m_w_ffn_up, m_w_ffn_down, v_norm1_g, v_w_in, v_q_norm_g, v_k_norm_g, v_attn_sinks, v_gate_ln_g, v_gate_ln_b, v_w_spatial, v_b_spatial, v_out_norm_attn_g, v_out_norm_gate_g, v_w_out, v_norm2_g, v_w_ffn_gate, v_w_ffn_up, v_w_ffn_down):
    nseq, seq, d = x.shape
    t = nseq * seq
    nb = seq // BLOCK
    inw = w_in.shape[2] * N_DEV
    dm = _Dims(d, inw, q_norm_g.shape[-1])
    xf = x.reshape(t, d)
    tgt = loss_target.reshape(t, d)

    rows = lambda wv, transposed: jnp.swapaxes(wv, 1, 2)[0] if transposed else wv[0]
    big = {"w_in": (w_in, m_w_in, v_w_in, True), "w_out": (w_out, m_w_out, v_w_out, False),
           "w_ffn_gate": (w_ffn_gate, m_w_ffn_gate, v_w_ffn_gate, True), "w_ffn_up": (w_ffn_up, m_w_ffn_up, v_w_ffn_up, True),
           "w_ffn_down": (w_ffn_down, m_w_ffn_down, v_w_ffn_down, False)}
    big_rows = {nm: tuple(rows(arr, tr) for arr in (wv, mv, vv)) for nm, (wv, mv, vv, tr) in big.items()}
    shard = lambda nm: big_rows[nm][0].astype(WIRE)
    win_t = _allgather_weight("gather_w_in", 1, shard("w_in"))
    wout = _allgather_weight("gather_w_out", 2, shard("w_out"))
    wu_t = _allgather_weight("gather_w_ffn_up", 9, shard("w_ffn_up"))
    wg_t = _allgather_weight("gather_w_ffn_gate", 3, shard("w_ffn_gate"))
    wd = _allgather_weight("gather_w_ffn_down", 10, shard("w_ffn_down"))

    lanes = lambda v, n=BLOCK: jnp.broadcast_to(v.reshape(-1, 1), (v.size, n))
    prm = (lanes(q_norm_g, dm.grp * BLOCK), lanes(k_norm_g, 2 * BLOCK), attn_sinks[0], lanes(gate_ln_g), lanes(gate_ln_b), w_spatial[0], b_spatial[0],
           lanes(out_norm_attn_g), lanes(out_norm_gate_g))

    h1 = _rms_fwd("rms1_fwd", xf, norm1_g)
    (proj_t,) = _matmul("mm_in", win_t, h1, "nt", [F32])
    y_t = _mixer_fwd(proj_t, prm, dm, nseq, nb)

    def residual_norm(acc, xr, g2):
        x2v = xr + acc
        return x2v, x2v * lax.rsqrt(jnp.mean(x2v * x2v, axis=-1, keepdims=True) + EPS) * g2

    x2, h2 = _matmul("mm_out", y_t, wout, "tn", [F32, MXU], epilogue=residual_norm, extras=[xf], rowvecs=[norm2_g], full_rows=True)
    (b,) = _matmul("mm_up", h2, wu_t, "nt", [MXU])

    def swiglu(ga, ub):
        silu, dsilu = _silu_parts(ga)
        ub = ub.astype(F32)
        return silu * ub, silu, dsilu * ub

    s, silu_a, dsilu_a_b = _matmul("mm_gate", h2, wg_t, "nt", [MXU, MXU, MXU], epilogue=swiglu, extras=[b])

    def loss_epilogue(acc, x2v, tv):
        diff = (x2v + acc) - tv
        dx3 = diff * (1.0 / d)
        return dx3, dx3, jnp.sum(diff * diff)

    dx3, dx3b, lossp = _matmul("mm_down", s, wd, "nn", [F32, MXU], epilogue=loss_epilogue, extras=[x2, tgt], partial=True)
    loss_part = (0.5 / d) * jnp.sum(lossp[::8, ::LANES])

    def dswiglu(ds, dsilu_b, silu):
        return ds * dsilu_b.astype(F32), ds * silu.astype(F32)

    (g_wd,) = _matmul("mm_gw_down", s, dx3b, "tn", [WIRE])
    sl_wd = _scatter_grad("scatter_w_ffn_down", 4, g_wd)
    da, db = _matmul("mm_d_down", dx3b, wd, "nt", [MXU, MXU], epilogue=dswiglu, extras=[dsilu_a_b, silu_a], after=[g_wd])
    (g_wg,) = _matmul("mm_gw_gate", da, h2, "tn", [WIRE])
    sl_wg = _scatter_grad("scatter_w_ffn_gate", 5, g_wg)
    (g_wu,) = _matmul("mm_gw_up", db, h2, "tn", [WIRE], after=[g_wg])
    sl_wu = _scatter_grad("scatter_w_ffn_up", 6, g_wu)
    (dh2a,) = _matmul("mm_dh2_gate", da, wg_t, "nn", [F32], after=[g_wu])
    (dh2,) = _matmul("mm_dh2_up", db, wu_t, "nn", [F32], epilogue=lambda acc, pv: (pv + acc,), extras=[dh2a])

    dy_t, dx2, dx2b, dg2 = _norm_bwd_matmul("mm_d_out", wout, dh2, x2, norm2_g, dx3, after=dh2a)
    (g_wout,) = _matmul("mm_gw_out", y_t, dx2b, "nn", [WIRE], after=[dy_t])
    sl_wout = _scatter_grad("scatter_w_out", 7, g_wout)
    (dproj0, dkv, dgq, dgk, dsink, dlng, dlnb, dws, dbs, dgoa, dgog) = _mixer_bwd(proj_t, dy_t, prm, dm, nseq, nb)
    early_g = [dgq, dgk, dsink, dlng, dlnb, dws, dbs, dgoa, dgog, dg2, loss_part.reshape(1)]
    early_slots = _allgather_rows("gather_small_grads", 11, _pack(early_g)[0])
    dproj_t = _patch_kv(dproj0, dkv, dm)
    (g_win,) = _matmul("mm_gw_in", dproj_t, h1, "nn", [WIRE])
    sl_win = _scatter_grad("scatter_w_in", 8, g_win)

    def norm1_backward(dh1, xv, dx2v, g1):
        r = lax.rsqrt(jnp.mean(xv * xv, axis=-1, keepdims=True) + EPS)
        xh = xv * r
        dxh = dh1 * g1
        return dx2v + r * (dxh - xh * jnp.mean(dxh * xh, axis=-1, keepdims=True)), jnp.sum(dh1 * xh, axis=0, keepdims=True)

    dx, dg1 = _matmul("mm_d_in", dproj_t, win_t, "tn", [F32], epilogue=norm1_backward, extras=[xf, dx2], rowvecs=[norm1_g],
                      after=[g_win], col_sum=True, full_rows=True)

    big_out = {}
    last = dx

    def big_update(nm, sl, after):
        res = _sum_adamw("adamw_" + nm, sl, *big_rows[nm], after=after)
        big_out[nm] = tuple(jnp.swapaxes(r[None], 1, 2) if big[nm][3] else r[None] for r in res)
        return res[1]

    for nm, sl in (("w_ffn_down", sl_wd), ("w_ffn_gate", sl_wg), ("w_ffn_up", sl_wu), ("w_out", sl_wout)):
        last = big_update(nm, sl, last)

    zero = jnp.zeros((1,), F32)
    small_names = ["q_norm_g", "k_norm_g", "attn_sinks", "gate_ln_g", "gate_ln_b", "w_spatial", "b_spatial",
                   "out_norm_attn_g", "out_norm_gate_g", "norm2_g", "loss", "norm1_g"]
    small_w = [q_norm_g, k_norm_g, attn_sinks, gate_ln_g, gate_ln_b, w_spatial, b_spatial, out_norm_attn_g, out_norm_gate_g, norm2_g, zero, norm1_g]
    small_m = [m_q_norm_g, m_k_norm_g, m_attn_sinks, m_gate_ln_g, m_gate_ln_b, m_w_spatial, m_b_spatial, m_out_norm_attn_g, m_out_norm_gate_g, m_norm2_g, zero, m_norm1_g]
    small_v = [v_q_norm_g, v_k_norm_g, v_attn_sinks, v_gate_ln_g, v_gate_ln_b, v_w_spatial, v_b_spatial, v_out_norm_attn_g, v_out_norm_gate_g, v_norm2_g, zero, v_norm1_g]
    pw, meta = _pack(small_w)
    sg, sd, sm, sv = _allreduce_small_adamw(early_slots, _pack([dg1])[0], pw, _pack(small_m)[0], _pack(small_v)[0], after=last)
    big_update("w_in", sl_win, sd)
    ug, ud, um, uv = _unpack(sg, meta), _unpack(sd, meta), _unpack(sm, meta), _unpack(sv, meta)
    small_out = {nm: (ug[k], ud[k], um[k], uv[k]) for k, nm in enumerate(small_names)}
    loss = small_out["loss"][0].reshape(())

    order = ["norm1_g", "w_in", "q_norm_g", "k_norm_g", "attn_sinks", "gate_ln_g", "gate_ln_b", "w_spatial", "b_spatial",
             "out_norm_attn_g", "out_norm_gate_g", "w_out", "norm2_g", "w_ffn_gate", "w_ffn_up", "w_ffn_down"]
    allo = {**big_out, **small_out}
    outs = [loss, dx.reshape(nseq, seq, d)]
    for k in range(4):
        outs += [allo[nm][k] for nm in order]
    return tuple(outs)
```

```python
import math

import jax
import jax.numpy as jnp
from jax import lax
from jax.experimental import pallas as pl
from jax.experimental.pallas import tpu as pltpu
from jax.experimental.pallas import tpu_sc as plsc

F32 = jnp.float32
MXU = jnp.bfloat16
WIRE = jnp.bfloat16
EPS = 1e-6
BLOCK = 128
GROUP_DIM = 128
N_KV_HEADS = 2
NEG = -1e30
N_DEV = 8
LANES = 128
MIB = 1024 * 1024

ADAM_LR = 0.001
ADAM_B1 = 0.9
ADAM_B2 = 0.999
ADAM_EPS = 1e-08
ADAM_WD = 0.01
ADAM_STEP = 10

MESH = pl.DeviceIdType.MESH
ANY = pl.BlockSpec(memory_space=pl.ANY)


def _pick(n, cands):
    for c in cands:
        if n % c == 0:
            return c
    return n


def _cparams(sem, vmem_mb):
    return pltpu.CompilerParams(dimension_semantics=sem, vmem_limit_bytes=vmem_mb * MIB)


VMEM_TILE_BUDGET = 50 * MIB
HBM_BYTES_PER_US = 3.0e6
STEP_US = 0.4
MXU_COLS = 256
MIN_TILE_N = 2 * MXU_COLS


def _tile_candidates(n):
    return [c for c in range(min(n, 2048), 0, -LANES) if n % c == 0 and c % LANES == 0] or [n]


def _matmul_tiles(m, n, kk, esz, extra_sizes, out_sizes, full_rows):
    best = None
    wide = [n] if full_rows else [c for c in _tile_candidates(n) if c >= MIN_TILE_N and c % MXU_COLS == 0] or _tile_candidates(n)
    for tm in _tile_candidates(m):
        for tn in wide:
            vmem = 2 * (tm + tn) * kk * esz + tm * tn * (4 + 2 * sum(extra_sizes) + 2 * sum(out_sizes))
            if vmem > VMEM_TILE_BUDGET:
                continue
            cost = (m // tm) * n * kk * esz / HBM_BYTES_PER_US + (m // tm) * (n // tn) * STEP_US
            if best is None or cost < best[0]:
                best = (cost, tm, tn, vmem)
    assert best is not None, (m, n, kk)
    return best[1:]


def _matmul(name, a, b, mode, out_dtypes, epilogue=None, extras=(), rowvecs=(), after=(), partial=False, col_sum=False, full_rows=False):
    if mode == "nn":
        (m, kk), n = a.shape, b.shape[1]
        dn = (((1,), (0,)), ((), ()))
    elif mode == "nt":
        (m, kk), n = a.shape, b.shape[0]
        dn = (((1,), (1,)), ((), ()))
    else:
        (kk, m), n = a.shape, b.shape[1]
        dn = (((0,), (0,)), ((), ()))
    tm, tn, vmem = _matmul_tiles(m, n, kk, a.dtype.itemsize, [e.dtype.itemsize for e in extras],
                                 [jnp.dtype(dt).itemsize for dt in out_dtypes], full_rows)
    a_spec = pl.BlockSpec((kk, tm), lambda i, j: (0, i)) if mode == "tn" else pl.BlockSpec((tm, kk), lambda i, j: (i, 0))
    b_spec = pl.BlockSpec((tn, kk), lambda i, j: (j, 0)) if mode == "nt" else pl.BlockSpec((kk, tn), lambda i, j: (0, j))
    tile = pl.BlockSpec((tm, tn), lambda i, j: (i, j))
    row = pl.BlockSpec((1, tn), lambda i, j: (0, j))
    ne, nr, na, no = len(extras), len(rowvecs), len(after), len(out_dtypes)

    def body(a_ref, b_ref, *rest):
        in_refs, out_refs = rest[:ne + nr], rest[ne + nr + na:]
        acc = lax.dot_general(a_ref[...], b_ref[...], dn, preferred_element_type=F32)
        vals = (acc,) if epilogue is None else epilogue(acc, *[r[...] for r in in_refs])
        for o_ref, t in zip(out_refs[:no], vals[:no]):
            o_ref[...] = t.astype(o_ref.dtype)
        if partial:
            out_refs[no][...] = jnp.full((8, LANES), vals[no], F32)
        if col_sum:
            sum_ref = out_refs[-1]

            @pl.when(pl.program_id(0) == 0)
            def _():
                sum_ref[...] = jnp.zeros_like(sum_ref)

            sum_ref[...] += vals[-1]

    out_specs = [tile] * no
    out_shape = [jax.ShapeDtypeStruct((m, n), dt) for dt in out_dtypes]
    if partial:
        out_specs.append(pl.BlockSpec((8, LANES), lambda i, j: (i, j)))
        out_shape.append(jax.ShapeDtypeStruct((m // tm * 8, n // tn * LANES), F32))
    if col_sum:
        out_specs.append(row)
        out_shape.append(jax.ShapeDtypeStruct((1, n), F32))
    return pl.pallas_call(
        body, name=name, grid=(m // tm, n // tn),
        in_specs=[a_spec, b_spec] + [tile] * ne + [row] * nr + [ANY] * na,
        out_specs=out_specs, out_shape=out_shape,
        compiler_params=_cparams(("arbitrary" if col_sum else "parallel", "arbitrary"), min(vmem // MIB + 8, 58)),
    )(a, b, *extras, *rowvecs, *after)


def _rms_fwd(name, x, g):
    t, d = x.shape
    tm = _pick(t, (512, 256, 128))

    def body(x_ref, g_ref, h_ref):
        xv = x_ref[...]
        r = lax.rsqrt(jnp.mean(xv * xv, axis=-1, keepdims=True) + EPS)
        h_ref[...] = (xv * r * g_ref[...]).astype(h_ref.dtype)

    return pl.pallas_call(
        body, name=name, grid=(t // tm,),
        in_specs=[pl.BlockSpec((tm, d), lambda i: (i, 0)), pl.BlockSpec((1, d), lambda i: (0, 0))],
        out_specs=pl.BlockSpec((tm, d), lambda i: (i, 0)),
        out_shape=jax.ShapeDtypeStruct((t, d), MXU),
        compiler_params=_cparams(("parallel",), 32),
    )(x, g)


def _norm_bwd_matmul(name, w, dh, x, g, res, after):
    t, d = x.shape
    m = w.shape[0]
    tn = _pick(t, (256, 128))

    def body(w_ref, dh_ref, x_ref, g_ref, res_ref, after_ref, out_ref, dx_ref, dxb_ref, dg_ref):
        del after_ref

        @pl.when(pl.program_id(0) == 0)
        def _():
            dg_ref[...] = jnp.zeros_like(dg_ref)

        xv, dhv = x_ref[...], dh_ref[...]
        r = lax.rsqrt(jnp.mean(xv * xv, axis=-1, keepdims=True) + EPS)
        xh = xv * r
        dg_ref[...] += jnp.sum(dhv * xh, axis=0, keepdims=True)
        dxh = dhv * g_ref[...]
        dx = res_ref[...] + r * (dxh - xh * jnp.mean(dxh * xh, axis=-1, keepdims=True))
        dx_ref[...] = dx
        dxb = dx.astype(MXU)
        dxb_ref[...] = dxb
        out_ref[...] = lax.dot_general(w_ref[...], dxb, (((1,), (1,)), ((), ())), preferred_element_type=F32)

    row = pl.BlockSpec((tn, d), lambda j: (j, 0))
    vec = pl.BlockSpec((1, d), lambda j: (0, 0))
    return pl.pallas_call(
        body, name=name, grid=(t // tn,),
        in_specs=[pl.BlockSpec((m, d), lambda j: (0, 0)), row, row, vec, row, ANY],
        out_specs=[pl.BlockSpec((m, tn), lambda j: (0, j)), row, row, vec],
        out_shape=[jax.ShapeDtypeStruct((m, t), F32), jax.ShapeDtypeStruct((t, d), F32), jax.ShapeDtypeStruct((t, d), MXU),
                   jax.ShapeDtypeStruct((1, d), F32)],
        compiler_params=_cparams(("arbitrary",), 52),
    )(w, dh, x, g, res, after)


_INV_SQRT2 = 0.7071067811865476
_INV_SQRT_2PI = 0.3989422804014327


def _dot_nt(a, b):
    return lax.dot_general(a, b, (((1,), (1,)), ((), ())), preferred_element_type=F32)


def _dot_tn(a, b):
    return lax.dot_general(a, b, (((0,), (0,)), ((), ())), preferred_element_type=F32)


def _dot(a, b):
    return jnp.dot(a, b, preferred_element_type=F32)


def _col_rms(v):
    return lax.rsqrt(jnp.mean(v * v, axis=0, keepdims=True) + EPS)


class _Dims:
    def __init__(self, d_model, in_width, head_dim):
        self.d = d_model
        self.aw = d_model // 2
        self.gw = d_model - self.aw
        self.kvw = (in_width - self.aw - 2 * self.gw) // 2
        self.hd = head_dim
        self.nh = self.aw // head_dim
        self.nkv = self.kvw // head_dim
        self.grp = self.nh // self.nkv
        self.ng = self.gw // GROUP_DIM
        self.inw = in_width
        self.zoff = self.aw + 2 * self.kvw
        assert self.nkv == N_KV_HEADS and self.zoff + 2 * self.gw == in_width and self.aw % (2 * self.kvw) == 0


def _band_masks(first):
    r = lax.broadcasted_iota(jnp.int32, (BLOCK, BLOCK), 0)
    t = lax.broadcasted_iota(jnp.int32, (BLOCK, BLOCK), 1)
    upper = r > t
    dist = t - r + jnp.where(upper, BLOCK, 0)
    return upper, jnp.logical_not(upper & first), dist.astype(F32)


def _fold(full, upper):
    return jnp.where(upper, full[:BLOCK], full[BLOCK:])


def _unfold(folded, upper):
    zero = jnp.zeros_like(folded)
    return jnp.concatenate([jnp.where(upper, folded, zero), jnp.where(upper, zero, folded)], axis=0)


def _kv_band(dm, kh, p_ref, pkv_ref, gk2):
    ko = dm.aw + kh * dm.hd
    vo = dm.aw + dm.kvw + kh * dm.hd
    k_t = jnp.concatenate([pkv_ref[kh * dm.hd:(kh + 1) * dm.hd, :], p_ref[ko:ko + dm.hd, :]], axis=1)
    v_t = jnp.concatenate([pkv_ref[dm.kvw + kh * dm.hd:dm.kvw + (kh + 1) * dm.hd, :], p_ref[vo:vo + dm.hd, :]], axis=1)
    kn_t = k_t * _col_rms(k_t) * gk2
    return kn_t.astype(MXU), kn_t.T.astype(MXU), v_t.astype(MXU), v_t.T.astype(MXU)


def _group_heads(dm, kh):
    return range(kh * dm.grp, (kh + 1) * dm.grp)


def _attn_group_fwd(dm, kh, p_ref, gq, kn, v_tb, sink_ref, masks):
    heads = _group_heads(dm, kh)
    q = jnp.concatenate([p_ref[h * dm.hd:(h + 1) * dm.hd, :] for h in heads], axis=1)
    rq = _col_rms(q)
    qh = q * rq
    qnb = (qh * gq).astype(MXU)
    upper, valid, dist = masks
    s = _dot(kn, qnb)
    probs, probs_b, sink_probs = [], [], []
    for g, h in enumerate(heads):
        slope, sink = math.pow(2.0, -8.0 * (h + 1) / dm.nh), sink_ref[h]
        logits = jnp.where(valid, _fold(s[:, g * BLOCK:(g + 1) * BLOCK], upper) * (dm.hd ** -0.5) - slope * dist, NEG)
        m = jnp.maximum(jnp.max(logits, axis=0, keepdims=True), sink)
        e = jnp.exp(logits - m)
        es = jnp.exp(sink - m)
        inv = 1.0 / (jnp.sum(e, axis=0, keepdims=True) + es)
        probs.append(e * inv)
        probs_b.append(_unfold(probs[g], upper).astype(MXU))
        sink_probs.append(es * inv)
    probs_b = jnp.concatenate(probs_b, axis=1)
    o = _dot(v_tb, probs_b)
    return o, probs, probs_b, sink_probs, rq, qh, qnb


def _gelu_cdf(z):
    return 0.5 * (1.0 + lax.erf(z * _INV_SQRT2))


def _by_group(v, ng):
    return v.reshape(ng, GROUP_DIM, v.shape[1])


def _gate_fwd(dm, p_ref, lng_ref, lnb_ref, ws_ref, bs_ref, tril):
    zu, zv = p_ref[dm.zoff:dm.zoff + dm.gw, :], p_ref[dm.zoff + dm.gw:dm.zoff + 2 * dm.gw, :]
    cu, cv = _gelu_cdf(zu), _gelu_cdf(zv)
    u, v = zu * cu, zv * cv
    v3 = _by_group(v, dm.ng)
    xc = v3 - jnp.mean(v3, axis=1, keepdims=True)
    rstd = lax.rsqrt(jnp.mean(xc * xc, axis=1, keepdims=True) + EPS)
    xh = (xc * rstd).reshape(dm.gw, BLOCK)
    vnb = (xh * lng_ref[...] + lnb_ref[...]).astype(MXU)
    wts = [jnp.where(tril, ws_ref[g], 0.0).astype(MXU) for g in range(dm.ng)]
    mixed = jnp.concatenate([_dot_nt(vnb[g * GROUP_DIM:(g + 1) * GROUP_DIM], wts[g]) + bs_ref[g:g + 1, :]
                             for g in range(dm.ng)], axis=0)
    return u * mixed, u, mixed, xh, rstd, vnb, wts, (zu, cu), (zv, cv)


def _mixer_specs(dm, nb, clamp):
    kvblk = dm.aw // (2 * dm.kvw)

    def cur(s, i):
        return (0, s * nb + clamp(i))

    def prev(s, i):
        return (kvblk, s * nb + jnp.maximum(clamp(i) - 1, 0))

    full = lambda shape: pl.BlockSpec(shape, lambda s, i: tuple(0 for _ in shape))
    return cur, prev, full


def _tril():
    return lax.broadcasted_iota(jnp.int32, (BLOCK, BLOCK), 0) >= lax.broadcasted_iota(jnp.int32, (BLOCK, BLOCK), 1)


def _mixer_fwd(proj_t, prm, dm, nseq, nb):
    gq, gk2, sinks, lng, lnb, ws, bs, goa, gog = prm
    t = proj_t.shape[1]
    cur, prev, full = _mixer_specs(dm, nb, lambda i: i)

    def body(p_ref, pkv_ref, gq_ref, gk_ref, sink_ref, lng_ref, lnb_ref, ws_ref, bs_ref, goa_ref, gog_ref, y_ref, att_scr):
        i = pl.program_id(1)
        masks = _band_masks(i == 0)
        gqv, gkv = gq_ref[...], gk_ref[...]
        for kh in range(dm.nkv):
            _, kn, v_tb, _ = _kv_band(dm, kh, p_ref, pkv_ref, gkv)
            o = _attn_group_fwd(dm, kh, p_ref, gqv, kn, v_tb, sink_ref, masks)[0]
            for g, h in enumerate(_group_heads(dm, kh)):
                att_scr[h * dm.hd:(h + 1) * dm.hd, :] = o[:, g * BLOCK:(g + 1) * BLOCK]
        att = att_scr[...]
        y_ref[:dm.aw, :] = (att * _col_rms(att) * goa_ref[...]).astype(y_ref.dtype)
        gt = _gate_fwd(dm, p_ref, lng_ref, lnb_ref, ws_ref, bs_ref, _tril())[0]
        y_ref[dm.aw:, :] = (gt * _col_rms(gt) * gog_ref[...]).astype(y_ref.dtype)

    return pl.pallas_call(
        body, name="mixer_fwd", grid=(nseq, nb),
        in_specs=[pl.BlockSpec((dm.inw, BLOCK), cur), pl.BlockSpec((2 * dm.kvw, BLOCK), prev),
                  full(gq.shape), full(gk2.shape), pl.BlockSpec(memory_space=pltpu.SMEM),
                  full(lng.shape), full(lnb.shape), full(ws.shape), full(bs.shape), full(goa.shape), full(gog.shape)],
        out_specs=pl.BlockSpec((dm.d, BLOCK), cur),
        out_shape=jax.ShapeDtypeStruct((dm.d, t), MXU),
        scratch_shapes=[pltpu.VMEM((dm.aw, BLOCK), F32)],
        compiler_params=_cparams(("parallel", "arbitrary"), 40),
    )(proj_t, proj_t, gq, gk2, sinks, lng, lnb, ws, bs, goa, gog)


def _mixer_bwd(proj_t, dy_t, prm, dm, nseq, nb):
    gq, gk2, sinks, lng, lnb, ws, bs, goa, gog = prm
    t = proj_t.shape[1]
    clamp = lambda i: jnp.minimum(i, nb - 1)
    cur, prev, full = _mixer_specs(dm, nb, clamp)
    kvw2 = 2 * dm.kvw

    def prev_kv_out(s, i):
        return (0, s * nb + jnp.maximum(i - 1, 0))

    def body(p_ref, pkv_ref, dy_ref, gq_ref, gk_ref, sink_ref, lng_ref, lnb_ref, ws_ref, bs_ref, goa_ref, gog_ref,
             dproj_ref, dkv_ref, dgq_ref, dgk_ref, dsink_ref, dlng_ref, dlnb_ref, dws_ref, dbs_ref, dgoa_ref, dgog_ref,
             att_scr, datt_scr, carry_scr, prevpart_scr, curpart_scr, kprev_scr,
             a_gq, a_gk, a_sink, a_lng, a_lnb, a_goa, a_gog):
        s_id, i = pl.program_id(0), pl.program_id(1)
        lane_accs = ((a_gq, dgq_ref), (a_gk, dgk_ref), (a_sink, dsink_ref), (a_lng, dlng_ref), (a_lnb, dlnb_ref),
                     (a_goa, dgoa_ref), (a_gog, dgog_ref))

        @pl.when((s_id == 0) & (i == 0))
        def _():
            for acc, _ in lane_accs:
                acc[...] = jnp.zeros_like(acc)
            dws_ref[...] = jnp.zeros_like(dws_ref)
            dbs_ref[...] = jnp.zeros_like(dbs_ref)

        gqv, gkv = gq_ref[...], gk_ref[...]

        @pl.when(i < nb)
        def _():
            masks = _band_masks(i == 0)
            upper = masks[0]
            kvs, fwd = [], []
            for kh in range(dm.nkv):
                kv = _kv_band(dm, kh, p_ref, pkv_ref, gkv)
                kvs.append(kv)
                fwd.append(_attn_group_fwd(dm, kh, p_ref, gqv, kv[1], kv[2], sink_ref, masks))
                for g, h in enumerate(_group_heads(dm, kh)):
                    att_scr[h * dm.hd:(h + 1) * dm.hd, :] = fwd[kh][0][:, g * BLOCK:(g + 1) * BLOCK]
            att = att_scr[...]
            dya = dy_ref[:dm.aw, :]
            ra = _col_rms(att)
            ah = att * ra
            a_goa[...] += dya * ah
            dah = dya * goa_ref[...]
            datt_scr[...] = ra * (dah - ah * jnp.mean(dah * ah, axis=0, keepdims=True))
            for kh in range(dm.nkv):
                kn_tb, kn, v_tb, vb = kvs[kh]
                _, probs, probs_b, sink_probs, rq, qh, qnb = fwd[kh]
                heads = _group_heads(dm, kh)
                do_b = jnp.concatenate([datt_scr[h * dm.hd:(h + 1) * dm.hd, :] for h in heads], axis=1).astype(MXU)
                dp = _dot(vb, do_b)
                ds = []
                for g, h in enumerate(heads):
                    p, dp_h = probs[g], _fold(dp[:, g * BLOCK:(g + 1) * BLOCK], upper)
                    delta = jnp.sum(p * dp_h, axis=0, keepdims=True)
                    ds.append(_unfold(p * (dp_h - delta) * (dm.hd ** -0.5), upper).astype(MXU))
                    a_sink[h:h + 1, :] += -(sink_probs[g] * delta)
                dsb = jnp.concatenate(ds, axis=1)
                dqn = _dot(kn_tb, dsb)
                dkn = _dot_nt(qnb, dsb)
                dvb = _dot_nt(do_b, probs_b)
                a_gq[...] += dqn * qh
                dqh = dqn * gqv
                dq = rq * (dqh - qh * jnp.mean(dqh * qh, axis=0, keepdims=True))
                for g, h in enumerate(heads):
                    dproj_ref[h * dm.hd:(h + 1) * dm.hd, :] = dq[:, g * BLOCK:(g + 1) * BLOCK].astype(dproj_ref.dtype)
                krows = slice(kh * dm.hd, (kh + 1) * dm.hd)
                vrows = slice(dm.kvw + kh * dm.hd, dm.kvw + (kh + 1) * dm.hd)
                prevpart_scr[krows, :] = dkn[:, :BLOCK]
                prevpart_scr[vrows, :] = dvb[:, :BLOCK]
                curpart_scr[krows, :] = dkn[:, BLOCK:]
                curpart_scr[vrows, :] = dvb[:, BLOCK:]
            dproj_ref[dm.aw:dm.zoff, :] = jnp.zeros((kvw2, BLOCK), dproj_ref.dtype)
            tril = _tril()
            gt, u, mixed, xh, rstd, vnb, wts, (zu, cu), (zv, cv) = _gate_fwd(dm, p_ref, lng_ref, lnb_ref, ws_ref, bs_ref, tril)
            dyg = dy_ref[dm.aw:, :]
            rg = _col_rms(gt)
            gh = gt * rg
            a_gog[...] += dyg * gh
            dgh = dyg * gog_ref[...]
            dgt = rg * (dgh - gh * jnp.mean(dgh * gh, axis=0, keepdims=True))
            du = dgt * mixed
            dmix = dgt * u
            dmixb = dmix.astype(MXU)
            dbs_ref[...] += jnp.sum(_by_group(dmix, dm.ng), axis=1)
            dvn = []
            for g in range(dm.ng):
                rows = slice(g * GROUP_DIM, (g + 1) * GROUP_DIM)
                dws_ref[g] += jnp.where(tril, _dot_tn(dmixb[rows], vnb[rows]), 0.0)
                dvn.append(_dot(dmixb[rows], wts[g]))
            dvn = jnp.concatenate(dvn, axis=0)
            a_lng[...] += dvn * xh
            a_lnb[...] += dvn
            dxh3, xh3 = _by_group(dvn * lng_ref[...], dm.ng), _by_group(xh, dm.ng)
            dv = (rstd * (dxh3 - jnp.mean(dxh3, axis=1, keepdims=True) - xh3 * jnp.mean(dxh3 * xh3, axis=1, keepdims=True))).reshape(dm.gw, BLOCK)
            dgu = cu + zu * (jnp.exp(-0.5 * zu * zu) * _INV_SQRT_2PI)
            dgv = cv + zv * (jnp.exp(-0.5 * zv * zv) * _INV_SQRT_2PI)
            dproj_ref[dm.zoff:dm.zoff + dm.gw, :] = (du * dgu).astype(dproj_ref.dtype)
            dproj_ref[dm.zoff + dm.gw:, :] = (dv * dgv).astype(dproj_ref.dtype)

        @pl.when(i == nb)
        def _():
            prevpart_scr[...] = jnp.zeros_like(prevpart_scr)

        @pl.when(i >= 1)
        def _():
            tot = carry_scr[...] + prevpart_scr[...]
            for kh in range(dm.nkv):
                krows = slice(kh * dm.hd, (kh + 1) * dm.hd)
                kraw = kprev_scr[krows, :]
                rk = _col_rms(kraw)
                khat = kraw * rk
                dkn = tot[krows, :]
                a_gk[...] += dkn * khat
                dkh = dkn * gkv[:, :BLOCK]
                dk = rk * (dkh - khat * jnp.mean(dkh * khat, axis=0, keepdims=True))
                dkv_ref[krows, :] = dk.astype(dkv_ref.dtype)
            dkv_ref[dm.kvw:, :] = tot[dm.kvw:, :].astype(dkv_ref.dtype)

        @pl.when(i < nb)
        def _():
            carry_scr[...] = curpart_scr[...]
            kprev_scr[...] = p_ref[dm.aw:dm.aw + dm.kvw, :]

        @pl.when((s_id == nseq - 1) & (i == nb))
        def _():
            for acc, out in lane_accs:
                out[...] = jnp.sum(acc[...], axis=1, keepdims=True)

    col = lambda rows: jax.ShapeDtypeStruct((rows, 1), F32)
    lane = lambda rows: pltpu.VMEM((rows, LANES), F32)
    return pl.pallas_call(
        body, name="mixer_bwd", grid=(nseq, nb + 1),
        in_specs=[pl.BlockSpec((dm.inw, BLOCK), cur), pl.BlockSpec((kvw2, BLOCK), prev), pl.BlockSpec((dm.d, BLOCK), cur),
                  full(gq.shape), full(gk2.shape), pl.BlockSpec(memory_space=pltpu.SMEM),
                  full(lng.shape), full(lnb.shape), full(ws.shape), full(bs.shape), full(goa.shape), full(gog.shape)],
        out_specs=[pl.BlockSpec((dm.inw, BLOCK), cur), pl.BlockSpec((kvw2, BLOCK), prev_kv_out),
                   full((dm.hd, 1)), full((dm.hd, 1)), full((dm.nh, 1)), full((dm.gw, 1)), full((dm.gw, 1)), full(ws.shape),
                   full(bs.shape), full((dm.aw, 1)), full((dm.gw, 1))],
        out_shape=[jax.ShapeDtypeStruct((dm.inw, t), MXU), jax.ShapeDtypeStruct((kvw2, t), MXU),
                   col(dm.hd), col(dm.hd), col(dm.nh), col(dm.gw), col(dm.gw), jax.ShapeDtypeStruct(ws.shape, F32),
                   jax.ShapeDtypeStruct(bs.shape, F32), col(dm.aw), col(dm.gw)],
        scratch_shapes=[pltpu.VMEM((dm.aw, BLOCK), F32), pltpu.VMEM((dm.aw, BLOCK), F32),
                        pltpu.VMEM((kvw2, BLOCK), F32), pltpu.VMEM((kvw2, BLOCK), F32), pltpu.VMEM((kvw2, BLOCK), F32),
                        pltpu.VMEM((dm.kvw, BLOCK), F32),
                        pltpu.VMEM((dm.hd, dm.grp * BLOCK), F32), lane(dm.hd), lane(dm.nh), lane(dm.gw), lane(dm.gw), lane(dm.aw), lane(dm.gw)],
        compiler_params=_cparams(("arbitrary", "arbitrary"), 48),
    )(proj_t, proj_t, dy_t, gq, gk2, sinks, lng, lnb, ws, bs, goa, gog)


def _patch_kv(dproj_t, dkv_t, dm):
    t = dproj_t.shape[1]
    tc = _pick(t, (1024, 512, 256, 128))
    kvw2 = 2 * dm.kvw
    kvblk = dm.aw // kvw2

    def body(dproj_hbm, dkv_ref, out_ref):
        del dproj_hbm
        out_ref[...] = dkv_ref[...]

    return pl.pallas_call(
        body, name="patch_kv", grid=(t // tc,),
        in_specs=[ANY, pl.BlockSpec((kvw2, tc), lambda i: (0, i))],
        out_specs=pl.BlockSpec((kvw2, tc), lambda i: (kvblk, i)),
        out_shape=jax.ShapeDtypeStruct(dproj_t.shape, dproj_t.dtype),
        input_output_aliases={0: 0},
        compiler_params=_cparams(("parallel",), 32),
    )(dproj_t, dkv_t)


def _place():
    x, y, c = lax.axis_index("x"), lax.axis_index("y"), lax.axis_index("c")
    return x, y, c


def _handshake(peers):
    barrier = pltpu.get_barrier_semaphore()
    for p in peers:
        pl.semaphore_signal(barrier, inc=1, device_id=p, device_id_type=MESH)
    pl.semaphore_wait(barrier, len(peers))


def _sequencer_mesh():
    return plsc.ScalarSubcoreMesh(axis_name="sequencer", num_cores=1)


GATHER_CHUNKS = 4
BF16_ROWS = 16


def _row_chunks(n, k):
    tiles = n // BF16_ROWS
    sizes = [(tiles // k + (1 if i < tiles % k else 0)) * BF16_ROWS for i in range(k)]
    return [(sum(sizes[:i]), sz) for i, sz in enumerate(sizes) if sz]


def _allgather_weight(name, collective_id, shard):
    n = shard.shape[0]
    assert n % BF16_ROWS == 0
    chunks = _row_chunks(n, GATHER_CHUNKS)
    nc = len(chunks)

    def body(src, out, send_sems, recv_sems, local_sem):
        x, y, c = _place()
        me, sib, xn, yn, diag = (x, y, c), (x, y, 1 - c), (1 - x, y, c), (x, 1 - y, c), (1 - x, 1 - y, c)
        relay_to = (x ^ c, y ^ (1 - c), c)
        relay_of = (x ^ (1 - c), y ^ c, c)
        _handshake([sib, xn, yn])

        def rows(place, ci):
            px, py, pc = place
            off, size = chunks[ci]
            return out.at[pl.ds(pl.multiple_of((4 * px + 2 * py + pc) * n + off, BF16_ROWS), size), :]

        def copy(k, ci, block, to, from_src=False):
            off, size = chunks[ci]
            return pltpu.make_async_remote_copy(
                src_ref=src.at[pl.ds(off, size), :] if from_src else rows(block, ci), dst_ref=rows(block, ci),
                send_sem=send_sems.at[ci, k], recv_sem=recv_sems.at[ci, k], device_id=to, device_id_type=MESH)

        mine = pltpu.make_async_copy(src, out.at[pl.ds(pl.multiple_of((4 * x + 2 * y + c) * n, BF16_ROWS), n), :], local_sem)
        mine.start()
        sent = []
        for ci in range(nc):
            sent += [copy(0, ci, me, sib, from_src=True), copy(1, ci, me, xn, from_src=True), copy(2, ci, me, yn, from_src=True)]
        for cp in sent:
            cp.start()
        for ci in range(nc):
            copy(1, ci, xn, me).wait_recv()
            copy(2, ci, yn, me).wait_recv()
            passed = [copy(3, ci, relay_of, relay_to), copy(4, ci, xn, sib), copy(5, ci, yn, sib)]
            for cp in passed:
                cp.start()
            sent += passed
        for ci in range(nc):
            copy(3, ci, diag, me).wait_recv()
            passed = copy(6, ci, diag, sib)
            passed.start()
            sent.append(passed)
        for ci in range(nc):
            copy(0, ci, sib, me).wait_recv()
            for k, block in ((4, (1 - x, y, 1 - c)), (5, (x, 1 - y, 1 - c)), (6, (1 - x, 1 - y, 1 - c))):
                copy(k, ci, block, me).wait_recv()
        for cp in sent:
            cp.wait_send()
        mine.wait()

    return pl.kernel(
        body, name=name,
        out_type=jax.ShapeDtypeStruct((N_DEV * n, shard.shape[1]), shard.dtype),
        mesh=_sequencer_mesh(),
        scratch_types=[pltpu.SemaphoreType.DMA((nc, 7)), pltpu.SemaphoreType.DMA((nc, 7)), pltpu.SemaphoreType.DMA],
        compiler_params=pltpu.CompilerParams(collective_id=collective_id),
    )(shard)


_FLIPS = [(0, 0, 1), (1, 0, 0), (0, 1, 0), (1, 1, 0), (1, 0, 1), (0, 1, 1), (1, 1, 1)]


def _scatter_grad(name, collective_id, grad):
    n = grad.shape[0] // N_DEV

    def body(src, out, send_sems, recv_sems, local_sem):
        x, y, c = _place()
        me_idx = 4 * x + 2 * y + c
        peers = [(x ^ fx, y ^ fy, c ^ fc) for (fx, fy, fc) in _FLIPS]
        _handshake(peers)

        def block(idx):
            return src.at[pl.ds(pl.multiple_of(idx * n, 16), n), :]

        copies = [pltpu.make_async_remote_copy(
            src_ref=block(4 * px + 2 * py + pc), dst_ref=out.at[me_idx], send_sem=send_sems.at[k], recv_sem=recv_sems.at[k],
            device_id=(px, py, pc), device_id_type=MESH) for k, (px, py, pc) in enumerate(peers)]
        mine = pltpu.make_async_copy(block(me_idx), out.at[me_idx], local_sem)
        mine.start()
        for cp in copies:
            cp.start()
        for cp in copies:
            cp.wait_recv()
        for cp in copies:
            cp.wait_send()
        mine.wait()

    return pl.kernel(
        body, name=name,
        out_type=jax.ShapeDtypeStruct((N_DEV, n, grad.shape[1]), grad.dtype),
        mesh=_sequencer_mesh(),
        scratch_types=[pltpu.SemaphoreType.DMA((7,)), pltpu.SemaphoreType.DMA((7,)), pltpu.SemaphoreType.DMA],
        compiler_params=pltpu.CompilerParams(collective_id=collective_id),
    )(grad)


def _allgather_rows(name, collective_id, part):
    def body(src, out, send_sems, recv_sems, local_sem):
        x, y, c = _place()
        me_idx = 4 * x + 2 * y + c
        peers = [(x ^ fx, y ^ fy, c ^ fc) for (fx, fy, fc) in _FLIPS]
        _handshake(peers)
        copies = [pltpu.make_async_remote_copy(
            src_ref=src, dst_ref=out.at[me_idx], send_sem=send_sems.at[k], recv_sem=recv_sems.at[k],
            device_id=peer, device_id_type=MESH) for k, peer in enumerate(peers)]
        mine = pltpu.make_async_copy(src, out.at[me_idx], local_sem)
        mine.start()
        for cp in copies:
            cp.start()
        for cp in copies:
            cp.wait_recv()
        for cp in copies:
            cp.wait_send()
        mine.wait()

    return pl.kernel(
        body, name=name,
        out_type=jax.ShapeDtypeStruct((N_DEV,) + part.shape, part.dtype),
        mesh=_sequencer_mesh(),
        scratch_types=[pltpu.SemaphoreType.DMA((7,)), pltpu.SemaphoreType.DMA((7,)), pltpu.SemaphoreType.DMA],
        compiler_params=pltpu.CompilerParams(collective_id=collective_id),
    )(part)


def _adamw_math(w, g, m, v):
    m = ADAM_B1 * m + (1.0 - ADAM_B1) * g
    v = ADAM_B2 * v + (1.0 - ADAM_B2) * (g * g)
    m_hat = m / (1.0 - ADAM_B1 ** ADAM_STEP)
    v_hat = v / (1.0 - ADAM_B2 ** ADAM_STEP)
    delta = -ADAM_LR * (m_hat / (jnp.sqrt(v_hat) + ADAM_EPS) + ADAM_WD * w)
    return delta, m, v


def _sum_adamw(name, slots, w, m, v, after):
    _, n, kk = slots.shape
    tr = _pick(n, (208, 176, 128, 96, 64, 32, 16))

    def body(s_ref, w_ref, m_ref, v_ref, after_ref, g_ref, d_ref, nm_ref, nv_ref):
        del after_ref
        g = s_ref[0].astype(F32)
        for p in range(1, N_DEV):
            g = g + s_ref[p].astype(F32)
        g_ref[...] = g
        d_ref[...], nm_ref[...], nv_ref[...] = _adamw_math(w_ref[...], g, m_ref[...], v_ref[...])

    row = pl.BlockSpec((tr, kk), lambda i: (i, 0))
    return pl.pallas_call(
        body, name=name, grid=(n // tr,),
        in_specs=[pl.BlockSpec((N_DEV, tr, kk), lambda i: (0, i, 0)), row, row, row, ANY],
        out_specs=[row] * 4,
        out_shape=[jax.ShapeDtypeStruct((n, kk), F32)] * 4,
        compiler_params=_cparams(("parallel",), 48),
    )(slots, w, m, v, after)


def _allreduce_small_adamw(early_slots, late, w, m, v, after):
    ra, rb = early_slots.shape[1], late.shape[0]

    def body(early_ref, late_ref, w_ref, m_ref, v_ref, after_ref, g_ref, d_ref, nm_ref, nv_ref, slots, send_sems, recv_sems):
        del after_ref
        x, y, c = _place()
        me_idx = 4 * x + 2 * y + c
        copies = []
        for k, (fx, fy, fc) in enumerate(_FLIPS):
            px, py, pc = x ^ fx, y ^ fy, c ^ fc
            copies.append(pltpu.make_async_remote_copy(
                src_ref=late_ref, dst_ref=slots.at[me_idx], send_sem=send_sems.at[k], recv_sem=recv_sems.at[k],
                device_id=(px, py, pc), device_id_type=MESH))
        for cp in copies:
            cp.start()
        slots[me_idx] = late_ref[...]
        g = early_ref[0]
        for p in range(1, N_DEV):
            g = g + early_ref[p]
        early = pl.ds(0, ra)
        g_ref[early, :] = g
        d_ref[early, :], nm_ref[early, :], nv_ref[early, :] = _adamw_math(w_ref[early, :], g, m_ref[early, :], v_ref[early, :])
        for cp in copies:
            cp.wait_recv()
        for cp in copies:
            cp.wait_send()
        g = slots[0]
        for p in range(1, N_DEV):
            g = g + slots[p]
        tail = pl.ds(ra, rb)
        g_ref[tail, :] = g
        d_ref[tail, :], nm_ref[tail, :], nv_ref[tail, :] = _adamw_math(w_ref[tail, :], g, m_ref[tail, :], v_ref[tail, :])

    vm = pl.BlockSpec(memory_space=pltpu.VMEM)
    return pl.pallas_call(
        body, name="allreduce_small_adamw",
        in_specs=[vm] * 5 + [ANY], out_specs=[vm] * 4,
        out_shape=[jax.ShapeDtypeStruct((ra + rb, LANES), F32)] * 4,
        scratch_shapes=[pltpu.VMEM((N_DEV, rb, LANES), F32), pltpu.SemaphoreType.DMA((7,)), pltpu.SemaphoreType.DMA((7,))],
        compiler_params=pltpu.CompilerParams(vmem_limit_bytes=48 * MIB),
    )(early_slots, late, w, m, v, after)


def _pack(arrs):
    parts, meta, off = [], [], 0
    for a in arrs:
        flat = a.reshape(-1).astype(F32)
        rows = -(-flat.shape[0] // LANES)
        rows8 = -(-rows // 8) * 8
        flat = jnp.pad(flat, (0, rows8 * LANES - flat.shape[0]))
        parts.append(flat.reshape(rows8, LANES))
        meta.append((off, a.shape, a.size))
        off += rows8
    return jnp.concatenate(parts, axis=0), meta


def _unpack(packed, meta):
    outs = []
    for off, shape, size in meta:
        rows = -(-size // LANES)
        outs.append(packed[off:off + rows].reshape(-1)[:size].reshape(shape))
    return outs


def _silu_parts(a):
    sg = 0.5 + 0.5 * jnp.tanh(0.5 * a)
    return a * sg, sg * (1.0 + a * (1.0 - sg))


def kernel(x, norm1_g, w_in, q_norm_g, k_norm_g, attn_sinks, gate_ln_g, gate_ln_b, w_spatial, b_spatial, out_norm_attn_g, out_norm_gate_g, w_out, norm2_g, w_ffn_gate, w_ffn_up, w_ffn_down, loss_target, m_norm1_g, m_w_in, m_q_norm_g, m_k_norm_g, m_attn_sinks, m_gate_ln_g, m_gate_ln_b, m_w_spatial, m_b_spatial, m_out_norm_attn_g, m_out_norm_gate_g, m_w_out, m_norm2_g, m_w_ffn_gate, m_w_ffn_up, m_w_ffn_down, v_norm1_g, v_w_in, v_q_norm_g, v_k_norm_g, v_attn_sinks, v_gate_ln_g, v_gate_ln_b, v_w_spatial, v_b_spatial, v_out_norm_attn_g, v_out_norm_gate_g, v_w_out, v_norm2_g, v_w_ffn_gate, v_w_ffn_up, v_w_ffn_down):
    nseq, seq, d = x.shape
    t = nseq * seq
    nb = seq // BLOCK
    inw = w_in.shape[2] * N_DEV
    dm = _Dims(d, inw, q_norm_g.shape[-1])
    xf = x.reshape(t, d)
    tgt = loss_target.reshape(t, d)

    rows = lambda wv, transposed: jnp.swapaxes(wv, 1, 2)[0] if transposed else wv[0]
    big = {"w_in": (w_in, m_w_in, v_w_in, True), "w_out": (w_out, m_w_out, v_w_out, False),
           "w_ffn_gate": (w_ffn_gate, m_w_ffn_gate, v_w_ffn_gate, True), "w_ffn_up": (w_ffn_up, m_w_ffn_up, v_w_ffn_up, True),
           "w_ffn_down": (w_ffn_down, m_w_ffn_down, v_w_ffn_down, False)}
    big_rows = {nm: tuple(rows(arr, tr) for arr in (wv, mv, vv)) for nm, (wv, mv, vv, tr) in big.items()}
    shard = lambda nm: big_rows[nm][0].astype(WIRE)
    win_t = _allgather_weight("gather_w_in", 1, shard("w_in"))
    wout = _allgather_weight("gather_w_out", 2, shard("w_out"))
    wg_t = _allgather_weight("gather_w_ffn_gate", 3, shard("w_ffn_gate"))
    wu_t = _allgather_weight("gather_w_ffn_up", 9, shard("w_ffn_up"))
    wd = _allgather_weight("gather_w_ffn_down", 10, shard("w_ffn_down"))

    lanes = lambda v, n=BLOCK: jnp.broadcast_to(v.reshape(-1, 1), (v.size, n))
    prm = (lanes(q_norm_g, dm.grp * BLOCK), lanes(k_norm_g, 2 * BLOCK), attn_sinks[0], lanes(gate_ln_g), lanes(gate_ln_b), w_spatial[0], b_spatial[0],
           lanes(out_norm_attn_g), lanes(out_norm_gate_g))

    h1 = _rms_fwd("rms1_fwd", xf, norm1_g)
    (proj_t,) = _matmul("mm_in", win_t, h1, "nt", [F32])
    y_t = _mixer_fwd(proj_t, prm, dm, nseq, nb)

    def residual_norm(acc, xr, g2):
        x2v = xr + acc
        return x2v, x2v * lax.rsqrt(jnp.mean(x2v * x2v, axis=-1, keepdims=True) + EPS) * g2

    x2, h2 = _matmul("mm_out", y_t, wout, "tn", [F32, MXU], epilogue=residual_norm, extras=[xf], rowvecs=[norm2_g], full_rows=True)
    silu_a, dsilu_a = _matmul("mm_gate", h2, wg_t, "nt", [MXU, MXU], epilogue=_silu_parts)
    s, dsilu_a_b = _matmul("mm_up", h2, wu_t, "nt", [MXU, MXU], epilogue=lambda ub, sv, dv: (sv.astype(F32) * ub, dv.astype(F32) * ub),
                           extras=[silu_a, dsilu_a])

    def loss_epilogue(acc, x2v, tv):
        diff = (x2v + acc) - tv
        dx3 = diff * (1.0 / d)
        return dx3, dx3, jnp.sum(diff * diff)

    dx3, dx3b, lossp = _matmul("mm_down", s, wd, "nn", [F32, MXU], epilogue=loss_epilogue, extras=[x2, tgt], partial=True)
    loss_part = (0.5 / d) * jnp.sum(lossp[::8, ::LANES])

    def dswiglu(ds, dsilu_b, silu):
        return ds * dsilu_b.astype(F32), ds * silu.astype(F32)

    (g_wd,) = _matmul("mm_gw_down", s, dx3b, "tn", [WIRE])
    sl_wd = _scatter_grad("scatter_w_ffn_down", 4, g_wd)
    da, db = _matmul("mm_d_down", dx3b, wd, "nt", [MXU, MXU], epilogue=dswiglu, extras=[dsilu_a_b, silu_a], after=[g_wd])
    (g_wg,) = _matmul("mm_gw_gate", da, h2, "tn", [WIRE])
    sl_wg = _scatter_grad("scatter_w_ffn_gate", 5, g_wg)
    (g_wu,) = _matmul("mm_gw_up", db, h2, "tn", [WIRE], after=[g_wg])
    sl_wu = _scatter_grad("scatter_w_ffn_up", 6, g_wu)
    (dh2a,) = _matmul("mm_dh2_gate", da, wg_t, "nn", [F32], after=[g_wu])
    (dh2,) = _matmul("mm_dh2_up", db, wu_t, "nn", [F32], epilogue=lambda acc, pv: (pv + acc,), extras=[dh2a])

    dy_t, dx2, dx2b, dg2 = _norm_bwd_matmul("mm_d_out", wout, dh2, x2, norm2_g, dx3, after=dh2a)
    (g_wout,) = _matmul("mm_gw_out", y_t, dx2b, "nn", [WIRE], after=[dy_t])
    sl_wout = _scatter_grad("scatter_w_out", 7, g_wout)
    (dproj0, dkv, dgq, dgk, dsink, dlng, dlnb, dws, dbs, dgoa, dgog) = _mixer_bwd(proj_t, dy_t, prm, dm, nseq, nb)
    early_g = [dgq, dgk, dsink, dlng, dlnb, dws, dbs, dgoa, dgog, dg2, loss_part.reshape(1)]
    early_slots = _allgather_rows("gather_small_grads", 11, _pack(early_g)[0])
    dproj_t = _patch_kv(dproj0, dkv, dm)
    (g_win,) = _matmul("mm_gw_in", dproj_t, h1, "nn", [WIRE])
    sl_win = _scatter_grad("scatter_w_in", 8, g_win)

    def norm1_backward(dh1, xv, dx2v, g1):
        r = lax.rsqrt(jnp.mean(xv * xv, axis=-1, keepdims=True) + EPS)
        xh = xv * r
        dxh = dh1 * g1
        return dx2v + r * (dxh - xh * jnp.mean(dxh * xh, axis=-1, keepdims=True)), jnp.sum(dh1 * xh, axis=0, keepdims=True)

    dx, dg1 = _matmul("mm_d_in", dproj_t, win_t, "tn", [F32], epilogue=norm1_backward, extras=[xf, dx2], rowvecs=[norm1_g],
                      after=[g_win], col_sum=True, full_rows=True)

    big_out = {}
    last = dx

    def big_update(nm, sl, after):
        res = _sum_adamw("adamw_" + nm, sl, *big_rows[nm], after=after)
        big_out[nm] = tuple(jnp.swapaxes(r[None], 1, 2) if big[nm][3] else r[None] for r in res)
        return res[1]

    for nm, sl in (("w_ffn_down", sl_wd), ("w_ffn_gate", sl_wg), ("w_ffn_up", sl_wu), ("w_out", sl_wout)):
        last = big_update(nm, sl, last)

    zero = jnp.zeros((1,), F32)
    small_names = ["q_norm_g", "k_norm_g", "attn_sinks", "gate_ln_g", "gate_ln_b", "w_spatial", "b_spatial",
                   "out_norm_attn_g", "out_norm_gate_g", "norm2_g", "loss", "norm1_g"]
    small_w = [q_norm_g, k_norm_g, attn_sinks, gate_ln_g, gate_ln_b, w_spatial, b_spatial, out_norm_attn_g, out_norm_gate_g, norm2_g, zero, norm1_g]
    small_m = [m_q_norm_g, m_k_norm_g, m_attn_sinks, m_gate_ln_g, m_gate_ln_b, m_w_spatial, m_b_spatial, m_out_norm_attn_g, m_out_norm_gate_g, m_norm2_g, zero, m_norm1_g]
    small_v = [v_q_norm_g, v_k_norm_g, v_attn_sinks, v_gate_ln_g, v_gate_ln_b, v_w_spatial, v_b_spatial, v_out_norm_attn_g, v_out_norm_gate_g, v_norm2_g, zero, v_norm1_g]
    pw, meta = _pack(small_w)
    sg, sd, sm, sv = _allreduce_small_adamw(early_slots, _pack([dg1])[0], pw, _pack(small_m)[0], _pack(small_v)[0], after=last)
    big_update("w_in", sl_win, sd)
    ug, ud, um, uv = _unpack(sg, meta), _unpack(sd, meta), _unpack(sm, meta), _unpack(sv, meta)
    small_out = {nm: (ug[k], ud[k], um[k], uv[k]) for k, nm in enumerate(small_names)}
    loss = small_out["loss"][0].reshape(())

    order = ["norm1_g", "w_in", "q_norm_g", "k_norm_g", "attn_sinks", "gate_ln_g", "gate_ln_b", "w_spatial", "b_spatial",
             "out_norm_attn_g", "out_norm_gate_g", "w_out", "norm2_g", "w_ffn_gate", "w_ffn_up", "w_ffn_down"]
    allo = {**big_out, **small_out}
    outs = [loss, dx.reshape(nseq, seq, d)]
    for k in range(4):
        outs += [allo[nm][k] for nm in order]
    return tuple(outs)
```

```python
import math

import jax
import jax.numpy as jnp
from jax import lax
from jax.experimental import pallas as pl
from jax.experimental.pallas import tpu as pltpu
from jax.experimental.pallas import tpu_sc as plsc

F32 = jnp.float32
MXU = jnp.bfloat16
WIRE = jnp.bfloat16
EPS = 1e-6
BLOCK = 128
GROUP_DIM = 128
N_KV_HEADS = 2
NEG = -1e30
N_DEV = 8
LANES = 128
MIB = 1024 * 1024

ADAM_LR = 0.001
ADAM_B1 = 0.9
ADAM_B2 = 0.999
ADAM_EPS = 1e-08
ADAM_WD = 0.01
ADAM_STEP = 10

MESH = pl.DeviceIdType.MESH
ANY = pl.BlockSpec(memory_space=pl.ANY)


def _pick(n, cands):
    for c in cands:
        if n % c == 0:
            return c
    return n


def _cparams(sem, vmem_mb):
    return pltpu.CompilerParams(dimension_semantics=sem, vmem_limit_bytes=vmem_mb * MIB)


VMEM_TILE_BUDGET = 50 * MIB
HBM_BYTES_PER_US = 3.0e6
STEP_US = 0.4
MXU_COLS = 256
MIN_TILE_N = 2 * MXU_COLS


def _tile_candidates(n):
    return [c for c in range(min(n, 2048), 0, -LANES) if n % c == 0 and c % LANES == 0] or [n]


def _matmul_tiles(m, n, kk, esz, extra_sizes, out_sizes, full_rows):
    best = None
    wide = [n] if full_rows else [c for c in _tile_candidates(n) if c >= MIN_TILE_N and c % MXU_COLS == 0] or _tile_candidates(n)
    for tm in _tile_candidates(m):
        for tn in wide:
            b_buffers = 1 if full_rows else 2
            vmem = (2 * tm + b_buffers * tn) * kk * esz + tm * tn * (4 + 2 * sum(extra_sizes) + 2 * sum(out_sizes))
            if vmem > VMEM_TILE_BUDGET:
                continue
            cost = (m // tm) * n * kk * esz / HBM_BYTES_PER_US + (m // tm) * (n // tn) * STEP_US
            if best is None or cost < best[0]:
                best = (cost, tm, tn, vmem)
    assert best is not None, (m, n, kk)
    return best[1:]


def _matmul(name, a, b, mode, out_dtypes, epilogue=None, extras=(), rowvecs=(), after=(), partial=False, col_sum=False, full_rows=False):
    if mode == "nn":
        (m, kk), n = a.shape, b.shape[1]
        dn = (((1,), (0,)), ((), ()))
    elif mode == "nt":
        (m, kk), n = a.shape, b.shape[0]
        dn = (((1,), (1,)), ((), ()))
    else:
        (kk, m), n = a.shape, b.shape[1]
        dn = (((0,), (0,)), ((), ()))
    tm, tn, vmem = _matmul_tiles(m, n, kk, a.dtype.itemsize, [e.dtype.itemsize for e in extras],
                                 [jnp.dtype(dt).itemsize for dt in out_dtypes], full_rows)
    a_spec = pl.BlockSpec((kk, tm), lambda i, j: (0, i)) if mode == "tn" else pl.BlockSpec((tm, kk), lambda i, j: (i, 0))
    resident = dict(pipeline_mode=pl.Buffered(1)) if full_rows else {}
    b_spec = pl.BlockSpec((tn, kk), lambda i, j: (j, 0), **resident) if mode == "nt" else pl.BlockSpec((kk, tn), lambda i, j: (0, j), **resident)
    tile = pl.BlockSpec((tm, tn), lambda i, j: (i, j))
    row = pl.BlockSpec((1, tn), lambda i, j: (0, j))
    ne, nr, na, no = len(extras), len(rowvecs), len(after), len(out_dtypes)

    def body(a_ref, b_ref, *rest):
        in_refs, out_refs = rest[:ne + nr], rest[ne + nr + na:]
        acc = lax.dot_general(a_ref[...], b_ref[...], dn, preferred_element_type=F32)
        vals = (acc,) if epilogue is None else epilogue(acc, *[r[...] for r in in_refs])
        for o_ref, t in zip(out_refs[:no], vals[:no]):
            o_ref[...] = t.astype(o_ref.dtype)
        if partial:
            out_refs[no][...] = jnp.full((8, LANES), vals[no], F32)
        if col_sum:
            sum_ref = out_refs[-1]

            @pl.when(pl.program_id(0) == 0)
            def _():
                sum_ref[...] = jnp.zeros_like(sum_ref)

            sum_ref[...] += vals[-1]

    out_specs = [tile] * no
    out_shape = [jax.ShapeDtypeStruct((m, n), dt) for dt in out_dtypes]
    if partial:
        out_specs.append(pl.BlockSpec((8, LANES), lambda i, j: (i, j)))
        out_shape.append(jax.ShapeDtypeStruct((m // tm * 8, n // tn * LANES), F32))
    if col_sum:
        out_specs.append(row)
        out_shape.append(jax.ShapeDtypeStruct((1, n), F32))
    return pl.pallas_call(
        body, name=name, grid=(m // tm, n // tn),
        in_specs=[a_spec, b_spec] + [tile] * ne + [row] * nr + [ANY] * na,
        out_specs=out_specs, out_shape=out_shape,
        compiler_params=_cparams(("arbitrary" if col_sum else "parallel", "arbitrary"), min(vmem // MIB + 8, 58)),
    )(a, b, *extras, *rowvecs, *after)


def _rms_fwd(name, x, g):
    t, d = x.shape
    tm = _pick(t, (512, 256, 128))

    def body(x_ref, g_ref, h_ref):
        xv = x_ref[...]
        r = lax.rsqrt(jnp.mean(xv * xv, axis=-1, keepdims=True) + EPS)
        h_ref[...] = (xv * r * g_ref[...]).astype(h_ref.dtype)

    return pl.pallas_call(
        body, name=name, grid=(t // tm,),
        in_specs=[pl.BlockSpec((tm, d), lambda i: (i, 0)), pl.BlockSpec((1, d), lambda i: (0, 0))],
        out_specs=pl.BlockSpec((tm, d), lambda i: (i, 0)),
        out_shape=jax.ShapeDtypeStruct((t, d), MXU),
        compiler_params=_cparams(("parallel",), 32),
    )(x, g)


def _norm_bwd_matmul(name, w, dh, x, g, res, after):
    t, d = x.shape
    m = w.shape[0]
    tn = _pick(t, (256, 128))

    def body(w_ref, dh_ref, x_ref, g_ref, res_ref, after_ref, out_ref, dx_ref, dxb_ref, dg_ref):
        del after_ref

        @pl.when(pl.program_id(0) == 0)
        def _():
            dg_ref[...] = jnp.zeros_like(dg_ref)

        xv, dhv = x_ref[...], dh_ref[...]
        r = lax.rsqrt(jnp.mean(xv * xv, axis=-1, keepdims=True) + EPS)
        xh = xv * r
        dg_ref[...] += jnp.sum(dhv * xh, axis=0, keepdims=True)
        dxh = dhv * g_ref[...]
        dx = res_ref[...] + r * (dxh - xh * jnp.mean(dxh * xh, axis=-1, keepdims=True))
        dx_ref[...] = dx
        dxb = dx.astype(MXU)
        dxb_ref[...] = dxb
        out_ref[...] = lax.dot_general(w_ref[...], dxb, (((1,), (1,)), ((), ())), preferred_element_type=F32)

    row = pl.BlockSpec((tn, d), lambda j: (j, 0))
    vec = pl.BlockSpec((1, d), lambda j: (0, 0))
    return pl.pallas_call(
        body, name=name, grid=(t // tn,),
        in_specs=[pl.BlockSpec((m, d), lambda j: (0, 0), pipeline_mode=pl.Buffered(1)), row, row, vec, row, ANY],
        out_specs=[pl.BlockSpec((m, tn), lambda j: (0, j)), row, row, vec],
        out_shape=[jax.ShapeDtypeStruct((m, t), F32), jax.ShapeDtypeStruct((t, d), F32), jax.ShapeDtypeStruct((t, d), MXU),
                   jax.ShapeDtypeStruct((1, d), F32)],
        compiler_params=_cparams(("arbitrary",), 52),
    )(w, dh, x, g, res, after)


_INV_SQRT2 = 0.7071067811865476
_INV_SQRT_2PI = 0.3989422804014327


def _dot_nt(a, b):
    return lax.dot_general(a, b, (((1,), (1,)), ((), ())), preferred_element_type=F32)


def _dot_tn(a, b):
    return lax.dot_general(a, b, (((0,), (0,)), ((), ())), preferred_element_type=F32)


def _dot(a, b):
    return jnp.dot(a, b, preferred_element_type=F32)


def _col_rms(v):
    return lax.rsqrt(jnp.mean(v * v, axis=0, keepdims=True) + EPS)


class _Dims:
    def __init__(self, d_model, in_width, head_dim):
        self.d = d_model
        self.aw = d_model // 2
        self.gw = d_model - self.aw
        self.kvw = (in_width - self.aw - 2 * self.gw) // 2
        self.hd = head_dim
        self.nh = self.aw // head_dim
        self.nkv = self.kvw // head_dim
        self.grp = self.nh // self.nkv
        self.ng = self.gw // GROUP_DIM
        self.inw = in_width
        self.zoff = self.aw + 2 * self.kvw
        assert self.nkv == N_KV_HEADS and self.zoff + 2 * self.gw == in_width and self.aw % (2 * self.kvw) == 0


def _band_masks(first):
    r = lax.broadcasted_iota(jnp.int32, (BLOCK, BLOCK), 0)
    t = lax.broadcasted_iota(jnp.int32, (BLOCK, BLOCK), 1)
    upper = r > t
    dist = t - r + jnp.where(upper, BLOCK, 0)
    return upper, jnp.logical_not(upper & first), dist.astype(F32)


def _fold(full, upper):
    return jnp.where(upper, full[:BLOCK], full[BLOCK:])


def _unfold(folded, upper):
    zero = jnp.zeros_like(folded)
    return jnp.concatenate([jnp.where(upper, folded, zero), jnp.where(upper, zero, folded)], axis=0)


def _kv_band(dm, kh, p_ref, pkv_ref, gk2):
    ko = dm.aw + kh * dm.hd
    vo = dm.aw + dm.kvw + kh * dm.hd
    k_t = jnp.concatenate([pkv_ref[kh * dm.hd:(kh + 1) * dm.hd, :], p_ref[ko:ko + dm.hd, :]], axis=1)
    v_t = jnp.concatenate([pkv_ref[dm.kvw + kh * dm.hd:dm.kvw + (kh + 1) * dm.hd, :], p_ref[vo:vo + dm.hd, :]], axis=1)
    kn_t = k_t * _col_rms(k_t) * gk2
    return kn_t.astype(MXU), kn_t.T.astype(MXU), v_t.astype(MXU), v_t.T.astype(MXU)


def _group_heads(dm, kh):
    return range(kh * dm.grp, (kh + 1) * dm.grp)


def _attn_group_fwd(dm, kh, p_ref, gq, kn, v_tb, sink_ref, masks):
    heads = _group_heads(dm, kh)
    q = jnp.concatenate([p_ref[h * dm.hd:(h + 1) * dm.hd, :] for h in heads], axis=1)
    rq = _col_rms(q)
    qh = q * rq
    qnb = (qh * gq).astype(MXU)
    upper, valid, dist = masks
    s = _dot(kn, qnb)
    probs, probs_b, sink_probs = [], [], []
    for g, h in enumerate(heads):
        slope, sink = math.pow(2.0, -8.0 * (h + 1) / dm.nh), sink_ref[h]
        logits = jnp.where(valid, _fold(s[:, g * BLOCK:(g + 1) * BLOCK], upper) * (dm.hd ** -0.5) - slope * dist, NEG)
        m = jnp.maximum(jnp.max(logits, axis=0, keepdims=True), sink)
        e = jnp.exp(logits - m)
        es = jnp.exp(sink - m)
        inv = 1.0 / (jnp.sum(e, axis=0, keepdims=True) + es)
        probs.append(e * inv)
        probs_b.append(_unfold(probs[g], upper).astype(MXU))
        sink_probs.append(es * inv)
    probs_b = jnp.concatenate(probs_b, axis=1)
    o = _dot(v_tb, probs_b)
    return o, probs, probs_b, sink_probs, rq, qh, qnb


def _gelu_cdf(z):
    return 0.5 * (1.0 + lax.erf(z * _INV_SQRT2))


def _by_group(v, ng):
    return v.reshape(ng, GROUP_DIM, v.shape[1])


def _gate_fwd(dm, p_ref, lng_ref, lnb_ref, ws_ref, bs_ref, tril):
    zu, zv = p_ref[dm.zoff:dm.zoff + dm.gw, :], p_ref[dm.zoff + dm.gw:dm.zoff + 2 * dm.gw, :]
    cu, cv = _gelu_cdf(zu), _gelu_cdf(zv)
    u, v = zu * cu, zv * cv
    v3 = _by_group(v, dm.ng)
    xc = v3 - jnp.mean(v3, axis=1, keepdims=True)
    rstd = lax.rsqrt(jnp.mean(xc * xc, axis=1, keepdims=True) + EPS)
    xh = (xc * rstd).reshape(dm.gw, BLOCK)
    vnb = (xh * lng_ref[...] + lnb_ref[...]).astype(MXU)
    wts = [jnp.where(tril, ws_ref[g], 0.0).astype(MXU) for g in range(dm.ng)]
    mixed = jnp.concatenate([_dot_nt(vnb[g * GROUP_DIM:(g + 1) * GROUP_DIM], wts[g]) + bs_ref[g:g + 1, :]
                             for g in range(dm.ng)], axis=0)
    return u * mixed, u, mixed, xh, rstd, vnb, wts, (zu, cu), (zv, cv)


def _mixer_specs(dm, nb, clamp):
    kvblk = dm.aw // (2 * dm.kvw)

    def cur(s, i):
        return (0, s * nb + clamp(i))

    def prev(s, i):
        return (kvblk, s * nb + jnp.maximum(clamp(i) - 1, 0))

    full = lambda shape: pl.BlockSpec(shape, lambda s, i: tuple(0 for _ in shape))
    return cur, prev, full


def _tril():
    return lax.broadcasted_iota(jnp.int32, (BLOCK, BLOCK), 0) >= lax.broadcasted_iota(jnp.int32, (BLOCK, BLOCK), 1)


def _mixer_fwd(proj_t, prm, dm, nseq, nb):
    gq, gk2, sinks, lng, lnb, ws, bs, goa, gog = prm
    t = proj_t.shape[1]
    cur, prev, full = _mixer_specs(dm, nb, lambda i: i)

    def body(p_ref, pkv_ref, gq_ref, gk_ref, sink_ref, lng_ref, lnb_ref, ws_ref, bs_ref, goa_ref, gog_ref, y_ref, att_scr):
        i = pl.program_id(1)
        masks = _band_masks(i == 0)
        gqv, gkv = gq_ref[...], gk_ref[...]
        for kh in range(dm.nkv):
            _, kn, v_tb, _ = _kv_band(dm, kh, p_ref, pkv_ref, gkv)
            o = _attn_group_fwd(dm, kh, p_ref, gqv, kn, v_tb, sink_ref, masks)[0]
            for g, h in enumerate(_group_heads(dm, kh)):
                att_scr[h * dm.hd:(h + 1) * dm.hd, :] = o[:, g * BLOCK:(g + 1) * BLOCK]
        att = att_scr[...]
        y_ref[:dm.aw, :] = (att * _col_rms(att) * goa_ref[...]).astype(y_ref.dtype)
        gt = _gate_fwd(dm, p_ref, lng_ref, lnb_ref, ws_ref, bs_ref, _tril())[0]
        y_ref[dm.aw:, :] = (gt * _col_rms(gt) * gog_ref[...]).astype(y_ref.dtype)

    return pl.pallas_call(
        body, name="mixer_fwd", grid=(nseq, nb),
        in_specs=[pl.BlockSpec((dm.inw, BLOCK), cur), pl.BlockSpec((2 * dm.kvw, BLOCK), prev),
                  full(gq.shape), full(gk2.shape), pl.BlockSpec(memory_space=pltpu.SMEM),
                  full(lng.shape), full(lnb.shape), full(ws.shape), full(bs.shape), full(goa.shape), full(gog.shape)],
        out_specs=pl.BlockSpec((dm.d, BLOCK), cur),
        out_shape=jax.ShapeDtypeStruct((dm.d, t), MXU),
        scratch_shapes=[pltpu.VMEM((dm.aw, BLOCK), F32)],
        compiler_params=_cparams(("parallel", "arbitrary"), 40),
    )(proj_t, proj_t, gq, gk2, sinks, lng, lnb, ws, bs, goa, gog)


def _mixer_bwd(proj_t, dy_t, prm, dm, nseq, nb):
    gq, gk2, sinks, lng, lnb, ws, bs, goa, gog = prm
    t = proj_t.shape[1]
    clamp = lambda i: jnp.minimum(i, nb - 1)
    cur, prev, full = _mixer_specs(dm, nb, clamp)
    kvw2 = 2 * dm.kvw

    def prev_kv_out(s, i):
        return (0, s * nb + jnp.maximum(i - 1, 0))

    def body(p_ref, pkv_ref, dy_ref, gq_ref, gk_ref, sink_ref, lng_ref, lnb_ref, ws_ref, bs_ref, goa_ref, gog_ref,
             dproj_ref, dkv_ref, dgq_ref, dgk_ref, dsink_ref, dlng_ref, dlnb_ref, dws_ref, dbs_ref, dgoa_ref, dgog_ref,
             att_scr, datt_scr, carry_scr, prevpart_scr, curpart_scr, kprev_scr,
             a_gq, a_gk, a_sink, a_lng, a_lnb, a_goa, a_gog):
        s_id, i = pl.program_id(0), pl.program_id(1)
        lane_accs = ((a_gq, dgq_ref), (a_gk, dgk_ref), (a_sink, dsink_ref), (a_lng, dlng_ref), (a_lnb, dlnb_ref),
                     (a_goa, dgoa_ref), (a_gog, dgog_ref))

        @pl.when((s_id == 0) & (i == 0))
        def _():
            for acc, _ in lane_accs:
                acc[...] = jnp.zeros_like(acc)
            dws_ref[...] = jnp.zeros_like(dws_ref)
            dbs_ref[...] = jnp.zeros_like(dbs_ref)

        gqv, gkv = gq_ref[...], gk_ref[...]

        @pl.when(i < nb)
        def _():
            masks = _band_masks(i == 0)
            upper = masks[0]
            kvs, fwd = [], []
            for kh in range(dm.nkv):
                kv = _kv_band(dm, kh, p_ref, pkv_ref, gkv)
                kvs.append(kv)
                fwd.append(_attn_group_fwd(dm, kh, p_ref, gqv, kv[1], kv[2], sink_ref, masks))
                for g, h in enumerate(_group_heads(dm, kh)):
                    att_scr[h * dm.hd:(h + 1) * dm.hd, :] = fwd[kh][0][:, g * BLOCK:(g + 1) * BLOCK]
            att = att_scr[...]
            dya = dy_ref[:dm.aw, :]
            ra = _col_rms(att)
            ah = att * ra
            a_goa[...] += dya * ah
            dah = dya * goa_ref[...]
            datt_scr[...] = ra * (dah - ah * jnp.mean(dah * ah, axis=0, keepdims=True))
            for kh in range(dm.nkv):
                kn_tb, kn, v_tb, vb = kvs[kh]
                _, probs, probs_b, sink_probs, rq, qh, qnb = fwd[kh]
                heads = _group_heads(dm, kh)
                do_b = jnp.concatenate([datt_scr[h * dm.hd:(h + 1) * dm.hd, :] for h in heads], axis=1).astype(MXU)
                dp = _dot(vb, do_b)
                ds = []
                for g, h in enumerate(heads):
                    p, dp_h = probs[g], _fold(dp[:, g * BLOCK:(g + 1) * BLOCK], upper)
                    delta = jnp.sum(p * dp_h, axis=0, keepdims=True)
                    ds.append(_unfold(p * (dp_h - delta) * (dm.hd ** -0.5), upper).astype(MXU))
                    a_sink[h:h + 1, :] += -(sink_probs[g] * delta)
                dsb = jnp.concatenate(ds, axis=1)
                dqn = _dot(kn_tb, dsb)
                dkn = _dot_nt(qnb, dsb)
                dvb = _dot_nt(do_b, probs_b)
                a_gq[...] += dqn * qh
                dqh = dqn * gqv
                dq = rq * (dqh - qh * jnp.mean(dqh * qh, axis=0, keepdims=True))
                for g, h in enumerate(heads):
                    dproj_ref[h * dm.hd:(h + 1) * dm.hd, :] = dq[:, g * BLOCK:(g + 1) * BLOCK].astype(dproj_ref.dtype)
                krows = slice(kh * dm.hd, (kh + 1) * dm.hd)
                vrows = slice(dm.kvw + kh * dm.hd, dm.kvw + (kh + 1) * dm.hd)
                prevpart_scr[krows, :] = dkn[:, :BLOCK]
                prevpart_scr[vrows, :] = dvb[:, :BLOCK]
                curpart_scr[krows, :] = dkn[:, BLOCK:]
                curpart_scr[vrows, :] = dvb[:, BLOCK:]
            dproj_ref[dm.aw:dm.zoff, :] = jnp.zeros((kvw2, BLOCK), dproj_ref.dtype)
            tril = _tril()
            gt, u, mixed, xh, rstd, vnb, wts, (zu, cu), (zv, cv) = _gate_fwd(dm, p_ref, lng_ref, lnb_ref, ws_ref, bs_ref, tril)
            dyg = dy_ref[dm.aw:, :]
            rg = _col_rms(gt)
            gh = gt * rg
            a_gog[...] += dyg * gh
            dgh = dyg * gog_ref[...]
            dgt = rg * (dgh - gh * jnp.mean(dgh * gh, axis=0, keepdims=True))
            du = dgt * mixed
            dmix = dgt * u
            dmixb = dmix.astype(MXU)
            dbs_ref[...] += jnp.sum(_by_group(dmix, dm.ng), axis=1)
            dvn = []
            for g in range(dm.ng):
                rows = slice(g * GROUP_DIM, (g + 1) * GROUP_DIM)
                dws_ref[g] += jnp.where(tril, _dot_tn(dmixb[rows], vnb[rows]), 0.0)
                dvn.append(_dot(dmixb[rows], wts[g]))
            dvn = jnp.concatenate(dvn, axis=0)
            a_lng[...] += dvn * xh
            a_lnb[...] += dvn
            dxh3, xh3 = _by_group(dvn * lng_ref[...], dm.ng), _by_group(xh, dm.ng)
            dv = (rstd * (dxh3 - jnp.mean(dxh3, axis=1, keepdims=True) - xh3 * jnp.mean(dxh3 * xh3, axis=1, keepdims=True))).reshape(dm.gw, BLOCK)
            dgu = cu + zu * (jnp.exp(-0.5 * zu * zu) * _INV_SQRT_2PI)
            dgv = cv + zv * (jnp.exp(-0.5 * zv * zv) * _INV_SQRT_2PI)
            dproj_ref[dm.zoff:dm.zoff + dm.gw, :] = (du * dgu).astype(dproj_ref.dtype)
            dproj_ref[dm.zoff + dm.gw:, :] = (dv * dgv).astype(dproj_ref.dtype)

        @pl.when(i == nb)
        def _():
            prevpart_scr[...] = jnp.zeros_like(prevpart_scr)

        @pl.when(i >= 1)
        def _():
            tot = carry_scr[...] + prevpart_scr[...]
            for kh in range(dm.nkv):
                krows = slice(kh * dm.hd, (kh + 1) * dm.hd)
                kraw = kprev_scr[krows, :]
                rk = _col_rms(kraw)
                khat = kraw * rk
                dkn = tot[krows, :]
                a_gk[...] += dkn * khat
                dkh = dkn * gkv[:, :BLOCK]
                dk = rk * (dkh - khat * jnp.mean(dkh * khat, axis=0, keepdims=True))
                dkv_ref[krows, :] = dk.astype(dkv_ref.dtype)
            dkv_ref[dm.kvw:, :] = tot[dm.kvw:, :].astype(dkv_ref.dtype)

        @pl.when(i < nb)
        def _():
            carry_scr[...] = curpart_scr[...]
            kprev_scr[...] = p_ref[dm.aw:dm.aw + dm.kvw, :]

        @pl.when((s_id == nseq - 1) & (i == nb))
        def _():
            for acc, out in lane_accs:
                out[...] = jnp.sum(acc[...], axis=1, keepdims=True)

    col = lambda rows: jax.ShapeDtypeStruct((rows, 1), F32)
    lane = lambda rows: pltpu.VMEM((rows, LANES), F32)
    return pl.pallas_call(
        body, name="mixer_bwd", grid=(nseq, nb + 1),
        in_specs=[pl.BlockSpec((dm.inw, BLOCK), cur), pl.BlockSpec((kvw2, BLOCK), prev), pl.BlockSpec((dm.d, BLOCK), cur),
                  full(gq.shape), full(gk2.shape), pl.BlockSpec(memory_space=pltpu.SMEM),
                  full(lng.shape), full(lnb.shape), full(ws.shape), full(bs.shape), full(goa.shape), full(gog.shape)],
        out_specs=[pl.BlockSpec((dm.inw, BLOCK), cur), pl.BlockSpec((kvw2, BLOCK), prev_kv_out),
                   full((dm.hd, 1)), full((dm.hd, 1)), full((dm.nh, 1)), full((dm.gw, 1)), full((dm.gw, 1)), full(ws.shape),
                   full(bs.shape), full((dm.aw, 1)), full((dm.gw, 1))],
        out_shape=[jax.ShapeDtypeStruct((dm.inw, t), MXU), jax.ShapeDtypeStruct((kvw2, t), MXU),
                   col(dm.hd), col(dm.hd), col(dm.nh), col(dm.gw), col(dm.gw), jax.ShapeDtypeStruct(ws.shape, F32),
                   jax.ShapeDtypeStruct(bs.shape, F32), col(dm.aw), col(dm.gw)],
        scratch_shapes=[pltpu.VMEM((dm.aw, BLOCK), F32), pltpu.VMEM((dm.aw, BLOCK), F32),
                        pltpu.VMEM((kvw2, BLOCK), F32), pltpu.VMEM((kvw2, BLOCK), F32), pltpu.VMEM((kvw2, BLOCK), F32),
                        pltpu.VMEM((dm.kvw, BLOCK), F32),
                        pltpu.VMEM((dm.hd, dm.grp * BLOCK), F32), lane(dm.hd), lane(dm.nh), lane(dm.gw), lane(dm.gw), lane(dm.aw), lane(dm.gw)],
        compiler_params=_cparams(("arbitrary", "arbitrary"), 48),
    )(proj_t, proj_t, dy_t, gq, gk2, sinks, lng, lnb, ws, bs, goa, gog)


def _patch_kv(dproj_t, dkv_t, dm):
    t = dproj_t.shape[1]
    tc = _pick(t, (1024, 512, 256, 128))
    kvw2 = 2 * dm.kvw
    kvblk = dm.aw // kvw2

    def body(dproj_hbm, dkv_ref, out_ref):
        del dproj_hbm
        out_ref[...] = dkv_ref[...]

    return pl.pallas_call(
        body, name="patch_kv", grid=(t // tc,),
        in_specs=[ANY, pl.BlockSpec((kvw2, tc), lambda i: (0, i))],
        out_specs=pl.BlockSpec((kvw2, tc), lambda i: (kvblk, i)),
        out_shape=jax.ShapeDtypeStruct(dproj_t.shape, dproj_t.dtype),
        input_output_aliases={0: 0},
        compiler_params=_cparams(("parallel",), 32),
    )(dproj_t, dkv_t)


def _place():
    x, y, c = lax.axis_index("x"), lax.axis_index("y"), lax.axis_index("c")
    return x, y, c


def _handshake(peers):
    barrier = pltpu.get_barrier_semaphore()
    for p in peers:
        pl.semaphore_signal(barrier, inc=1, device_id=p, device_id_type=MESH)
    pl.semaphore_wait(barrier, len(peers))


def _sequencer_mesh():
    return plsc.ScalarSubcoreMesh(axis_name="sequencer", num_cores=1)


GATHER_CHUNKS = 4
BF16_ROWS = 16


def _row_chunks(n, k):
    tiles = n // BF16_ROWS
    sizes = [(tiles // k + (1 if i < tiles % k else 0)) * BF16_ROWS for i in range(k)]
    return [(sum(sizes[:i]), sz) for i, sz in enumerate(sizes) if sz]


def _allgather_weight(name, collective_id, shard):
    n = shard.shape[0]
    assert n % BF16_ROWS == 0
    chunks = _row_chunks(n, GATHER_CHUNKS)
    nc = len(chunks)

    def body(src, out, send_sems, recv_sems, local_sem):
        x, y, c = _place()
        me, sib, xn, yn, diag = (x, y, c), (x, y, 1 - c), (1 - x, y, c), (x, 1 - y, c), (1 - x, 1 - y, c)
        relay_to = (x ^ c, y ^ (1 - c), c)
        relay_of = (x ^ (1 - c), y ^ c, c)
        _handshake([sib, xn, yn])

        def rows(place, ci):
            px, py, pc = place
            off, size = chunks[ci]
            return out.at[pl.ds(pl.multiple_of((4 * px + 2 * py + pc) * n + off, BF16_ROWS), size), :]

        def copy(k, ci, block, to, from_src=False):
            off, size = chunks[ci]
            return pltpu.make_async_remote_copy(
                src_ref=src.at[pl.ds(off, size), :] if from_src else rows(block, ci), dst_ref=rows(block, ci),
                send_sem=send_sems.at[ci, k], recv_sem=recv_sems.at[ci, k], device_id=to, device_id_type=MESH)

        mine = pltpu.make_async_copy(src, out.at[pl.ds(pl.multiple_of((4 * x + 2 * y + c) * n, BF16_ROWS), n), :], local_sem)
        mine.start()
        sent = []
        for ci in range(nc):
            sent += [copy(0, ci, me, sib, from_src=True), copy(1, ci, me, xn, from_src=True), copy(2, ci, me, yn, from_src=True)]
        for cp in sent:
            cp.start()
        for ci in range(nc):
            copy(1, ci, xn, me).wait_recv()
            copy(2, ci, yn, me).wait_recv()
            passed = [copy(3, ci, relay_of, relay_to), copy(4, ci, xn, sib), copy(5, ci, yn, sib)]
            for cp in passed:
                cp.start()
            sent += passed
        for ci in range(nc):
            copy(3, ci, diag, me).wait_recv()
            passed = copy(6, ci, diag, sib)
            passed.start()
            sent.append(passed)
        for ci in range(nc):
            copy(0, ci, sib, me).wait_recv()
            for k, block in ((4, (1 - x, y, 1 - c)), (5, (x, 1 - y, 1 - c)), (6, (1 - x, 1 - y, 1 - c))):
                copy(k, ci, block, me).wait_recv()
        for cp in sent:
            cp.wait_send()
        mine.wait()

    return pl.kernel(
        body, name=name,
        out_type=jax.ShapeDtypeStruct((N_DEV * n, shard.shape[1]), shard.dtype),
        mesh=_sequencer_mesh(),
        scratch_types=[pltpu.SemaphoreType.DMA((nc, 7)), pltpu.SemaphoreType.DMA((nc, 7)), pltpu.SemaphoreType.DMA],
        compiler_params=pltpu.CompilerParams(collective_id=collective_id),
    )(shard)


_FLIPS = [(0, 0, 1), (1, 0, 0), (0, 1, 0), (1, 1, 0), (1, 0, 1), (0, 1, 1), (1, 1, 1)]


def _scatter_grad(name, collective_id, grad):
    n = grad.shape[0] // N_DEV

    def body(src, out, send_sems, recv_sems, local_sem):
        x, y, c = _place()
        me_idx = 4 * x + 2 * y + c
        peers = [(x ^ fx, y ^ fy, c ^ fc) for (fx, fy, fc) in _FLIPS]
        _handshake(peers)

        def block(idx):
            return src.at[pl.ds(pl.multiple_of(idx * n, 16), n), :]

        copies = [pltpu.make_async_remote_copy(
            src_ref=block(4 * px + 2 * py + pc), dst_ref=out.at[me_idx], send_sem=send_sems.at[k], recv_sem=recv_sems.at[k],
            device_id=(px, py, pc), device_id_type=MESH) for k, (px, py, pc) in enumerate(peers)]
        mine = pltpu.make_async_copy(block(me_idx), out.at[me_idx], local_sem)
        mine.start()
        for cp in copies:
            cp.start()
        for cp in copies:
            cp.wait_recv()
        for cp in copies:
            cp.wait_send()
        mine.wait()

    return pl.kernel(
        body, name=name,
        out_type=jax.ShapeDtypeStruct((N_DEV, n, grad.shape[1]), grad.dtype),
        mesh=_sequencer_mesh(),
        scratch_types=[pltpu.SemaphoreType.DMA((7,)), pltpu.SemaphoreType.DMA((7,)), pltpu.SemaphoreType.DMA],
        compiler_params=pltpu.CompilerParams(collective_id=collective_id),
    )(grad)


def _allgather_rows(name, collective_id, part):
    def body(src, out, send_sems, recv_sems, local_sem):
        x, y, c = _place()
        me_idx = 4 * x + 2 * y + c
        peers = [(x ^ fx, y ^ fy, c ^ fc) for (fx, fy, fc) in _FLIPS]
        _handshake(peers)
        copies = [pltpu.make_async_remote_copy(
            src_ref=src, dst_ref=out.at[me_idx], send_sem=send_sems.at[k], recv_sem=recv_sems.at[k],
            device_id=peer, device_id_type=MESH) for k, peer in enumerate(peers)]
        mine = pltpu.make_async_copy(src, out.at[me_idx], local_sem)
        mine.start()
        for cp in copies:
            cp.start()
        for cp in copies:
            cp.wait_recv()
        for cp in copies:
            cp.wait_send()
        mine.wait()

    return pl.kernel(
        body, name=name,
        out_type=jax.ShapeDtypeStruct((N_DEV,) + part.shape, part.dtype),
        mesh=_sequencer_mesh(),
        scratch_types=[pltpu.SemaphoreType.DMA((7,)), pltpu.SemaphoreType.DMA((7,)), pltpu.SemaphoreType.DMA],
        compiler_params=pltpu.CompilerParams(collective_id=collective_id),
    )(part)


def _adamw_math(w, g, m, v):
    m = ADAM_B1 * m + (1.0 - ADAM_B1) * g
    v = ADAM_B2 * v + (1.0 - ADAM_B2) * (g * g)
    m_hat = m / (1.0 - ADAM_B1 ** ADAM_STEP)
    v_hat = v / (1.0 - ADAM_B2 ** ADAM_STEP)
    delta = -ADAM_LR * (m_hat / (jnp.sqrt(v_hat) + ADAM_EPS) + ADAM_WD * w)
    return delta, m, v


def _sum_adamw(name, slots, w, m, v, after):
    _, n, kk = slots.shape
    tr = _pick(n, (208, 176, 128, 96, 64, 32, 16))

    def body(s_ref, w_ref, m_ref, v_ref, after_ref, g_ref, d_ref, nm_ref, nv_ref):
        del after_ref
        g = s_ref[0].astype(F32)
        for p in range(1, N_DEV):
            g = g + s_ref[p].astype(F32)
        g_ref[...] = g
        d_ref[...], nm_ref[...], nv_ref[...] = _adamw_math(w_ref[...], g, m_ref[...], v_ref[...])

    row = pl.BlockSpec((tr, kk), lambda i: (i, 0))
    return pl.pallas_call(
        body, name=name, grid=(n // tr,),
        in_specs=[pl.BlockSpec((N_DEV, tr, kk), lambda i: (0, i, 0)), row, row, row, ANY],
        out_specs=[row] * 4,
        out_shape=[jax.ShapeDtypeStruct((n, kk), F32)] * 4,
        compiler_params=_cparams(("parallel",), 48),
    )(slots, w, m, v, after)


def _allreduce_small_adamw(early_slots, late, w, m, v, after):
    ra, rb = early_slots.shape[1], late.shape[0]

    def body(early_ref, late_ref, w_ref, m_ref, v_ref, after_ref, g_ref, d_ref, nm_ref, nv_ref, slots, send_sems, recv_sems):
        del after_ref
        x, y, c = _place()
        me_idx = 4 * x + 2 * y + c
        copies = []
        for k, (fx, fy, fc) in enumerate(_FLIPS):
            px, py, pc = x ^ fx, y ^ fy, c ^ fc
            copies.append(pltpu.make_async_remote_copy(
                src_ref=late_ref, dst_ref=slots.at[me_idx], send_sem=send_sems.at[k], recv_sem=recv_sems.at[k],
                device_id=(px, py, pc), device_id_type=MESH))
        for cp in copies:
            cp.start()
        slots[me_idx] = late_ref[...]
        g = early_ref[0]
        for p in range(1, N_DEV):
            g = g + early_ref[p]
        early = pl.ds(0, ra)
        g_ref[early, :] = g
        d_ref[early, :], nm_ref[early, :], nv_ref[early, :] = _adamw_math(w_ref[early, :], g, m_ref[early, :], v_ref[early, :])
        for cp in copies:
            cp.wait_recv()
        for cp in copies:
            cp.wait_send()
        g = slots[0]
        for p in range(1, N_DEV):
            g = g + slots[p]
        tail = pl.ds(ra, rb)
        g_ref[tail, :] = g
        d_ref[tail, :], nm_ref[tail, :], nv_ref[tail, :] = _adamw_math(w_ref[tail, :], g, m_ref[tail, :], v_ref[tail, :])

    vm = pl.BlockSpec(memory_space=pltpu.VMEM)
    return pl.pallas_call(
        body, name="allreduce_small_adamw",
        in_specs=[vm] * 5 + [ANY], out_specs=[vm] * 4,
        out_shape=[jax.ShapeDtypeStruct((ra + rb, LANES), F32)] * 4,
        scratch_shapes=[pltpu.VMEM((N_DEV, rb, LANES), F32), pltpu.SemaphoreType.DMA((7,)), pltpu.SemaphoreType.DMA((7,))],
        compiler_params=pltpu.CompilerParams(vmem_limit_bytes=48 * MIB),
    )(early_slots, late, w, m, v, after)


def _pack(arrs):
    parts, meta, off = [], [], 0
    for a in arrs:
        flat = a.reshape(-1).astype(F32)
        rows = -(-flat.shape[0] // LANES)
        rows8 = -(-rows // 8) * 8
        flat = jnp.pad(flat, (0, rows8 * LANES - flat.shape[0]))
        parts.append(flat.reshape(rows8, LANES))
        meta.append((off, a.shape, a.size))
        off += rows8
    return jnp.concatenate(parts, axis=0), meta


def _unpack(packed, meta):
    outs = []
    for off, shape, size in meta:
        rows = -(-size // LANES)
        outs.append(packed[off:off + rows].reshape(-1)[:size].reshape(shape))
    return outs


def _silu_parts(a):
    sg = 0.5 + 0.5 * jnp.tanh(0.5 * a)
    return a * sg, sg * (1.0 + a * (1.0 - sg))


def kernel(x, norm1_g, w_in, q_norm_g, k_norm_g, attn_sinks, gate_ln_g, gate_ln_b, w_spatial, b_spatial, out_norm_attn_g, out_norm_gate_g, w_out, norm2_g, w_ffn_gate, w_ffn_up, w_ffn_down, loss_target, m_norm1_g, m_w_in, m_q_norm_g, m_k_norm_g, m_attn_sinks, m_gate_ln_g, m_gate_ln_b, m_w_spatial, m_b_spatial, m_out_norm_attn_g, m_out_norm_gate_g, m_w_out, m_norm2_g, m_w_ffn_gate, m_w_ffn_up, m_w_ffn_down, v_norm1_g, v_w_in, v_q_norm_g, v_k_norm_g, v_attn_sinks, v_gate_ln_g, v_gate_ln_b, v_w_spatial, v_b_spatial, v_out_norm_attn_g, v_out_norm_gate_g, v_w_out, v_norm2_g, v_w_ffn_gate, v_w_ffn_up, v_w_ffn_down):
    nseq, seq, d = x.shape
    t = nseq * seq
    nb = seq // BLOCK
    inw = w_in.shape[2] * N_DEV
    dm = _Dims(d, inw, q_norm_g.shape[-1])
    xf = x.reshape(t, d)
    tgt = loss_target.reshape(t, d)

    rows = lambda wv, transposed: jnp.swapaxes(wv, 1, 2)[0] if transposed else wv[0]
    big = {"w_in": (w_in, m_w_in, v_w_in, True), "w_out": (w_out, m_w_out, v_w_out, False),
           "w_ffn_gate": (w_ffn_gate, m_w_ffn_gate, v_w_ffn_gate, True), "w_ffn_up": (w_ffn_up, m_w_ffn_up, v_w_ffn_up, True),
           "w_ffn_down": (w_ffn_down, m_w_ffn_down, v_w_ffn_down, False)}
    big_rows = {nm: tuple(rows(arr, tr) for arr in (wv, mv, vv)) for nm, (wv, mv, vv, tr) in big.items()}
    shard = lambda nm: big_rows[nm][0].astype(WIRE)
    win_t = _allgather_weight("gather_w_in", 1, shard("w_in"))
    wout = _allgather_weight("gather_w_out", 2, shard("w_out"))
    wg_t = _allgather_weight("gather_w_ffn_gate", 3, shard("w_ffn_gate"))
    wu_t = _allgather_weight("gather_w_ffn_up", 9, shard("w_ffn_up"))
    wd = _allgather_weight("gather_w_ffn_down", 10, shard("w_ffn_down"))

    lanes = lambda v, n=BLOCK: jnp.broadcast_to(v.reshape(-1, 1), (v.size, n))
    prm = (lanes(q_norm_g, dm.grp * BLOCK), lanes(k_norm_g, 2 * BLOCK), attn_sinks[0], lanes(gate_ln_g), lanes(gate_ln_b), w_spatial[0], b_spatial[0],
           lanes(out_norm_attn_g), lanes(out_norm_gate_g))

    h1 = _rms_fwd("rms1_fwd", xf, norm1_g)
    (proj_t,) = _matmul("mm_in", win_t, h1, "nt", [F32])
    y_t = _mixer_fwd(proj_t, prm, dm, nseq, nb)

    def residual_norm(acc, xr, g2):
        x2v = xr + acc
        return x2v, x2v * lax.rsqrt(jnp.mean(x2v * x2v, axis=-1, keepdims=True) + EPS) * g2

    x2, h2 = _matmul("mm_out", y_t, wout, "tn", [F32, MXU], epilogue=residual_norm, extras=[xf], rowvecs=[norm2_g], full_rows=True)
    silu_a, dsilu_a = _matmul("mm_gate", h2, wg_t, "nt", [MXU, MXU], epilogue=_silu_parts)
    s, dsilu_a_b = _matmul("mm_up", h2, wu_t, "nt", [MXU, MXU], epilogue=lambda ub, sv, dv: (sv.astype(F32) * ub, dv.astype(F32) * ub),
                           extras=[silu_a, dsilu_a])

    def loss_epilogue(acc, x2v, tv):
        diff = (x2v + acc) - tv
        dx3 = diff * (1.0 / d)
        return dx3, dx3, jnp.sum(diff * diff)

    dx3, dx3b, lossp = _matmul("mm_down", s, wd, "nn", [F32, MXU], epilogue=loss_epilogue, extras=[x2, tgt], partial=True)
    loss_part = (0.5 / d) * jnp.sum(lossp[::8, ::LANES])

    def dswiglu(ds, dsilu_b, silu):
        return ds * dsilu_b.astype(F32), ds * silu.astype(F32)

    (g_wd,) = _matmul("mm_gw_down", s, dx3b, "tn", [WIRE])
    sl_wd = _scatter_grad("scatter_w_ffn_down", 4, g_wd)
    da, db = _matmul("mm_d_down", dx3b, wd, "nt", [MXU, MXU], epilogue=dswiglu, extras=[dsilu_a_b, silu_a], after=[g_wd])
    (g_wg,) = _matmul("mm_gw_gate", da, h2, "tn", [WIRE])
    sl_wg = _scatter_grad("scatter_w_ffn_gate", 5, g_wg)
    (g_wu,) = _matmul("mm_gw_up", db, h2, "tn", [WIRE], after=[g_wg])
    sl_wu = _scatter_grad("scatter_w_ffn_up", 6, g_wu)
    (dh2a,) = _matmul("mm_dh2_gate", da, wg_t, "nn", [F32], after=[g_wu])
    (dh2,) = _matmul("mm_dh2_up", db, wu_t, "nn", [F32], epilogue=lambda acc, pv: (pv + acc,), extras=[dh2a])

    dy_t, dx2, dx2b, dg2 = _norm_bwd_matmul("mm_d_out", wout, dh2, x2, norm2_g, dx3, after=dh2a)
    (g_wout,) = _matmul("mm_gw_out", y_t, dx2b, "nn", [WIRE], after=[dy_t])
    sl_wout = _scatter_grad("scatter_w_out", 7, g_wout)
    (dproj0, dkv, dgq, dgk, dsink, dlng, dlnb, dws, dbs, dgoa, dgog) = _mixer_bwd(proj_t, dy_t, prm, dm, nseq, nb)
    early_g = [dgq, dgk, dsink, dlng, dlnb, dws, dbs, dgoa, dgog, dg2, loss_part.reshape(1)]
    early_slots = _allgather_rows("gather_small_grads", 11, _pack(early_g)[0])
    dproj_t = _patch_kv(dproj0, dkv, dm)
    (g_win,) = _matmul("mm_gw_in", dproj_t, h1, "nn", [WIRE])
    sl_win = _scatter_grad("scatter_w_in", 8, g_win)

    def norm1_backward(dh1, xv, dx2v, g1):
        r = lax.rsqrt(jnp.mean(xv * xv, axis=-1, keepdims=True) + EPS)
        xh = xv * r
        dxh = dh1 * g1
        return dx2v + r * (dxh - xh * jnp.mean(dxh * xh, axis=-1, keepdims=True)), jnp.sum(dh1 * xh, axis=0, keepdims=True)

    dx, dg1 = _matmul("mm_d_in", dproj_t, win_t, "tn", [F32], epilogue=norm1_backward, extras=[xf, dx2], rowvecs=[norm1_g],
                      after=[g_win], col_sum=True, full_rows=True)

    big_out = {}
    last = dx

    def big_update(nm, sl, after):
        res = _sum_adamw("adamw_" + nm, sl, *big_rows[nm], after=after)
        big_out[nm] = tuple(jnp.swapaxes(r[None], 1, 2) if big[nm][3] else r[None] for r in res)
        return res[1]

    for nm, sl in (("w_ffn_down", sl_wd), ("w_ffn_gate", sl_wg), ("w_ffn_up", sl_wu), ("w_out", sl_wout)):
        last = big_update(nm, sl, last)

    zero = jnp.zeros((1,), F32)
    small_names = ["q_norm_g", "k_norm_g", "attn_sinks", "gate_ln_g", "gate_ln_b", "w_spatial", "b_spatial",
                   "out_norm_attn_g", "out_norm_gate_g", "norm2_g", "loss", "norm1_g"]
    small_w = [q_norm_g, k_norm_g, attn_sinks, gate_ln_g, gate_ln_b, w_spatial, b_spatial, out_norm_attn_g, out_norm_gate_g, norm2_g, zero, norm1_g]
    small_m = [m_q_norm_g, m_k_norm_g, m_attn_sinks, m_gate_ln_g, m_gate_ln_b, m_w_spatial, m_b_spatial, m_out_norm_attn_g, m_out_norm_gate_g, m_norm2_g, zero, m_norm1_g]
    small_v = [v_q_norm_g, v_k_norm_g, v_attn_sinks, v_gate_ln_g, v_gate_ln_b, v_w_spatial, v_b_spatial, v_out_norm_attn_g, v_out_norm_gate_g, v_norm2_g, zero, v_norm1_g]
    pw, meta = _pack(small_w)
    sg, sd, sm, sv = _allreduce_small_adamw(early_slots, _pack([dg1])[0], pw, _pack(small_m)[0], _pack(small_v)[0], after=last)
    big_update("w_in", sl_win, sd)
    ug, ud, um, uv = _unpack(sg, meta), _unpack(sd, meta), _unpack(sm, meta), _unpack(sv, meta)
    small_out = {nm: (ug[k], ud[k], um[k], uv[k]) for k, nm in enumerate(small_names)}
    loss = small_out["loss"][0].reshape(())

    order = ["norm1_g", "w_in", "q_norm_g", "k_norm_g", "attn_sinks", "gate_ln_g", "gate_ln_b", "w_spatial", "b_spatial",
             "out_norm_attn_g", "out_norm_gate_g", "w_out", "norm2_g", "w_ffn_gate", "w_ffn_up", "w_ffn_down"]
    allo = {**big_out, **small_out}
    outs = [loss, dx.reshape(nseq, seq, d)]
    for k in range(4):
        outs += [allo[nm][k] for nm in order]
    return tuple(outs)
```

```python
import math

import jax
import jax.numpy as jnp
from jax import lax
from jax.experimental import pallas as pl
from jax.experimental.pallas import tpu as pltpu
from jax.experimental.pallas import tpu_sc as plsc

F32 = jnp.float32
MXU = jnp.bfloat16
WIRE = jnp.bfloat16
EPS = 1e-6
BLOCK = 128
GROUP_DIM = 128
N_KV_HEADS = 2
NEG = -1e30
N_DEV = 8
LANES = 128
MIB = 1024 * 1024

ADAM_LR = 0.001
ADAM_B1 = 0.9
ADAM_B2 = 0.999
ADAM_EPS = 1e-08
ADAM_WD = 0.01
ADAM_STEP = 10

MESH = pl.DeviceIdType.MESH
ANY = pl.BlockSpec(memory_space=pl.ANY)


def _pick(n, cands):
    for c in cands:
        if n % c == 0:
            return c
    return n


def _cparams(sem, vmem_mb):
    return pltpu.CompilerParams(dimension_semantics=sem, vmem_limit_bytes=vmem_mb * MIB)


VMEM_TILE_BUDGET = 50 * MIB
HBM_BYTES_PER_US = 3.0e6
STEP_US = 0.4
MXU_COLS = 256
MIN_TILE_N = 2 * MXU_COLS


def _tile_candidates(n):
    return [c for c in range(min(n, 2048), 0, -LANES) if n % c == 0 and c % LANES == 0] or [n]


def _matmul_tiles(m, n, kk, esz, extra_sizes, out_sizes, full_rows):
    best = None
    wide = [n] if full_rows else [c for c in _tile_candidates(n) if c >= MIN_TILE_N and c % MXU_COLS == 0] or _tile_candidates(n)
    for tm in _tile_candidates(m):
        for tn in wide:
            b_buffers = 1 if full_rows else 2
            temps = 2 if full_rows else 0
            vmem = (2 * tm + b_buffers * tn) * kk * esz + tm * tn * (4 * (1 + temps) + 2 * sum(extra_sizes) + 2 * sum(out_sizes))
            if vmem > VMEM_TILE_BUDGET:
                continue
            cost = (m // tm) * n * kk * esz / HBM_BYTES_PER_US + (m // tm) * (n // tn) * STEP_US
            if best is None or cost < best[0]:
                best = (cost, tm, tn, vmem)
    assert best is not None, (m, n, kk)
    return best[1:]


def _matmul(name, a, b, mode, out_dtypes, epilogue=None, extras=(), rowvecs=(), after=(), partial=False, col_sum=False, full_rows=False):
    if mode == "nn":
        (m, kk), n = a.shape, b.shape[1]
        dn = (((1,), (0,)), ((), ()))
    elif mode == "nt":
        (m, kk), n = a.shape, b.shape[0]
        dn = (((1,), (1,)), ((), ()))
    else:
        (kk, m), n = a.shape, b.shape[1]
        dn = (((0,), (0,)), ((), ()))
    tm, tn, vmem = _matmul_tiles(m, n, kk, a.dtype.itemsize, [e.dtype.itemsize for e in extras],
                                 [jnp.dtype(dt).itemsize for dt in out_dtypes], full_rows)
    a_spec = pl.BlockSpec((kk, tm), lambda i, j: (0, i)) if mode == "tn" else pl.BlockSpec((tm, kk), lambda i, j: (i, 0))
    resident = dict(pipeline_mode=pl.Buffered(1)) if full_rows else {}
    b_spec = pl.BlockSpec((tn, kk), lambda i, j: (j, 0), **resident) if mode == "nt" else pl.BlockSpec((kk, tn), lambda i, j: (0, j), **resident)
    tile = pl.BlockSpec((tm, tn), lambda i, j: (i, j))
    row = pl.BlockSpec((1, tn), lambda i, j: (0, j))
    ne, nr, na, no = len(extras), len(rowvecs), len(after), len(out_dtypes)

    def body(a_ref, b_ref, *rest):
        in_refs, out_refs = rest[:ne + nr], rest[ne + nr + na:]
        acc = lax.dot_general(a_ref[...], b_ref[...], dn, preferred_element_type=F32)
        vals = (acc,) if epilogue is None else epilogue(acc, *[r[...] for r in in_refs])
        for o_ref, t in zip(out_refs[:no], vals[:no]):
            o_ref[...] = t.astype(o_ref.dtype)
        if partial:
            out_refs[no][...] = jnp.full((8, LANES), vals[no], F32)
        if col_sum:
            sum_ref = out_refs[-1]

            @pl.when(pl.program_id(0) == 0)
            def _():
                sum_ref[...] = jnp.zeros_like(sum_ref)

            sum_ref[...] += vals[-1]

    out_specs = [tile] * no
    out_shape = [jax.ShapeDtypeStruct((m, n), dt) for dt in out_dtypes]
    if partial:
        out_specs.append(pl.BlockSpec((8, LANES), lambda i, j: (i, j)))
        out_shape.append(jax.ShapeDtypeStruct((m // tm * 8, n // tn * LANES), F32))
    if col_sum:
        out_specs.append(row)
        out_shape.append(jax.ShapeDtypeStruct((1, n), F32))
    return pl.pallas_call(
        body, name=name, grid=(m // tm, n // tn),
        in_specs=[a_spec, b_spec] + [tile] * ne + [row] * nr + [ANY] * na,
        out_specs=out_specs, out_shape=out_shape,
        compiler_params=_cparams(("arbitrary" if col_sum else "parallel", "arbitrary"), min(vmem // MIB + 8, 58)),
    )(a, b, *extras, *rowvecs, *after)


def _rms_fwd(name, x, g):
    t, d = x.shape
    tm = _pick(t, (512, 256, 128))

    def body(x_ref, g_ref, h_ref):
        xv = x_ref[...]
        r = lax.rsqrt(jnp.mean(xv * xv, axis=-1, keepdims=True) + EPS)
        h_ref[...] = (xv * r * g_ref[...]).astype(h_ref.dtype)

    return pl.pallas_call(
        body, name=name, grid=(t // tm,),
        in_specs=[pl.BlockSpec((tm, d), lambda i: (i, 0)), pl.BlockSpec((1, d), lambda i: (0, 0))],
        out_specs=pl.BlockSpec((tm, d), lambda i: (i, 0)),
        out_shape=jax.ShapeDtypeStruct((t, d), MXU),
        compiler_params=_cparams(("parallel",), 32),
    )(x, g)


def _norm_bwd_matmul(name, w, dh, x, g, res, after):
    t, d = x.shape
    m = w.shape[0]
    tn = _pick(t, (256, 128))

    def body(w_ref, dh_ref, x_ref, g_ref, res_ref, after_ref, out_ref, dx_ref, dxb_ref, dg_ref):
        del after_ref

        @pl.when(pl.program_id(0) == 0)
        def _():
            dg_ref[...] = jnp.zeros_like(dg_ref)

        xv, dhv = x_ref[...], dh_ref[...]
        r = lax.rsqrt(jnp.mean(xv * xv, axis=-1, keepdims=True) + EPS)
        xh = xv * r
        dg_ref[...] += jnp.sum(dhv * xh, axis=0, keepdims=True)
        dxh = dhv * g_ref[...]
        dx = res_ref[...] + r * (dxh - xh * jnp.mean(dxh * xh, axis=-1, keepdims=True))
        dx_ref[...] = dx
        dxb = dx.astype(MXU)
        dxb_ref[...] = dxb
        out_ref[...] = lax.dot_general(w_ref[...], dxb, (((1,), (1,)), ((), ())), preferred_element_type=F32)

    row = pl.BlockSpec((tn, d), lambda j: (j, 0))
    vec = pl.BlockSpec((1, d), lambda j: (0, 0))
    return pl.pallas_call(
        body, name=name, grid=(t // tn,),
        in_specs=[pl.BlockSpec((m, d), lambda j: (0, 0), pipeline_mode=pl.Buffered(1)), row, row, vec, row, ANY],
        out_specs=[pl.BlockSpec((m, tn), lambda j: (0, j)), row, row, vec],
        out_shape=[jax.ShapeDtypeStruct((m, t), F32), jax.ShapeDtypeStruct((t, d), F32), jax.ShapeDtypeStruct((t, d), MXU),
                   jax.ShapeDtypeStruct((1, d), F32)],
        compiler_params=_cparams(("arbitrary",), 52),
    )(w, dh, x, g, res, after)


_INV_SQRT2 = 0.7071067811865476
_INV_SQRT_2PI = 0.3989422804014327


def _dot_nt(a, b):
    return lax.dot_general(a, b, (((1,), (1,)), ((), ())), preferred_element_type=F32)


def _dot_tn(a, b):
    return lax.dot_general(a, b, (((0,), (0,)), ((), ())), preferred_element_type=F32)


def _dot(a, b):
    return jnp.dot(a, b, preferred_element_type=F32)


def _col_rms(v):
    return lax.rsqrt(jnp.mean(v * v, axis=0, keepdims=True) + EPS)


class _Dims:
    def __init__(self, d_model, in_width, head_dim):
        self.d = d_model
        self.aw = d_model // 2
        self.gw = d_model - self.aw
        self.kvw = (in_width - self.aw - 2 * self.gw) // 2
        self.hd = head_dim
        self.nh = self.aw // head_dim
        self.nkv = self.kvw // head_dim
        self.grp = self.nh // self.nkv
        self.ng = self.gw // GROUP_DIM
        self.inw = in_width
        self.zoff = self.aw + 2 * self.kvw
        assert self.nkv == N_KV_HEADS and self.zoff + 2 * self.gw == in_width and self.aw % (2 * self.kvw) == 0


def _band_masks(first):
    r = lax.broadcasted_iota(jnp.int32, (BLOCK, BLOCK), 0)
    t = lax.broadcasted_iota(jnp.int32, (BLOCK, BLOCK), 1)
    upper = r > t
    dist = t - r + jnp.where(upper, BLOCK, 0)
    return upper, jnp.logical_not(upper & first), dist.astype(F32)


def _fold(full, upper):
    return jnp.where(upper, full[:BLOCK], full[BLOCK:])


def _unfold(folded, upper):
    zero = jnp.zeros_like(folded)
    return jnp.concatenate([jnp.where(upper, folded, zero), jnp.where(upper, zero, folded)], axis=0)


def _kv_band(dm, kh, p_ref, pkv_ref, gk2):
    ko = dm.aw + kh * dm.hd
    vo = dm.aw + dm.kvw + kh * dm.hd
    k_t = jnp.concatenate([pkv_ref[kh * dm.hd:(kh + 1) * dm.hd, :], p_ref[ko:ko + dm.hd, :]], axis=1)
    v_t = jnp.concatenate([pkv_ref[dm.kvw + kh * dm.hd:dm.kvw + (kh + 1) * dm.hd, :], p_ref[vo:vo + dm.hd, :]], axis=1)
    kn_t = k_t * _col_rms(k_t) * gk2
    return kn_t.astype(MXU), kn_t.T.astype(MXU), v_t.astype(MXU), v_t.T.astype(MXU)


def _group_heads(dm, kh):
    return range(kh * dm.grp, (kh + 1) * dm.grp)


def _attn_group_fwd(dm, kh, p_ref, gq, kn, v_tb, sink_ref, masks):
    heads = _group_heads(dm, kh)
    q = jnp.concatenate([p_ref[h * dm.hd:(h + 1) * dm.hd, :] for h in heads], axis=1)
    rq = _col_rms(q)
    qh = q * rq
    qnb = (qh * gq).astype(MXU)
    upper, valid, dist = masks
    s = _dot(kn, qnb)
    probs, probs_b, sink_probs = [], [], []
    for g, h in enumerate(heads):
        slope, sink = math.pow(2.0, -8.0 * (h + 1) / dm.nh), sink_ref[h]
        logits = jnp.where(valid, _fold(s[:, g * BLOCK:(g + 1) * BLOCK], upper) * (dm.hd ** -0.5) - slope * dist, NEG)
        m = jnp.maximum(jnp.max(logits, axis=0, keepdims=True), sink)
        e = jnp.exp(logits - m)
        es = jnp.exp(sink - m)
        inv = 1.0 / (jnp.sum(e, axis=0, keepdims=True) + es)
        probs.append(e * inv)
        probs_b.append(_unfold(probs[g], upper).astype(MXU))
        sink_probs.append(es * inv)
    probs_b = jnp.concatenate(probs_b, axis=1)
    o = _dot(v_tb, probs_b)
    return o, probs, probs_b, sink_probs, rq, qh, qnb


def _gelu_cdf(z):
    return 0.5 * (1.0 + lax.erf(z * _INV_SQRT2))


def _by_group(v, ng):
    return v.reshape(ng, GROUP_DIM, v.shape[1])


def _gate_fwd(dm, p_ref, lng_ref, lnb_ref, ws_ref, bs_ref, tril):
    zu, zv = p_ref[dm.zoff:dm.zoff + dm.gw, :], p_ref[dm.zoff + dm.gw:dm.zoff + 2 * dm.gw, :]
    cu, cv = _gelu_cdf(zu), _gelu_cdf(zv)
    u, v = zu * cu, zv * cv
    v3 = _by_group(v, dm.ng)
    xc = v3 - jnp.mean(v3, axis=1, keepdims=True)
    rstd = lax.rsqrt(jnp.mean(xc * xc, axis=1, keepdims=True) + EPS)
    xh = (xc * rstd).reshape(dm.gw, BLOCK)
    vnb = (xh * lng_ref[...] + lnb_ref[...]).astype(MXU)
    wts = [jnp.where(tril, ws_ref[g], 0.0).astype(MXU) for g in range(dm.ng)]
    mixed = jnp.concatenate([_dot_nt(vnb[g * GROUP_DIM:(g + 1) * GROUP_DIM], wts[g]) + bs_ref[g:g + 1, :]
                             for g in range(dm.ng)], axis=0)
    return u * mixed, u, mixed, xh, rstd, vnb, wts, (zu, cu), (zv, cv)


def _mixer_specs(dm, nb, clamp):
    kvblk = dm.aw // (2 * dm.kvw)

    def cur(s, i):
        return (0, s * nb + clamp(i))

    def prev(s, i):
        return (kvblk, s * nb + jnp.maximum(clamp(i) - 1, 0))

    full = lambda shape: pl.BlockSpec(shape, lambda s, i: tuple(0 for _ in shape))
    return cur, prev, full


def _tril():
    return lax.broadcasted_iota(jnp.int32, (BLOCK, BLOCK), 0) >= lax.broadcasted_iota(jnp.int32, (BLOCK, BLOCK), 1)


def _mixer_fwd(proj_t, prm, dm, nseq, nb):
    gq, gk2, sinks, lng, lnb, ws, bs, goa, gog = prm
    t = proj_t.shape[1]
    cur, prev, full = _mixer_specs(dm, nb, lambda i: i)

    def body(p_ref, pkv_ref, gq_ref, gk_ref, sink_ref, lng_ref, lnb_ref, ws_ref, bs_ref, goa_ref, gog_ref, y_ref, att_scr):
        i = pl.program_id(1)
        masks = _band_masks(i == 0)
        gqv, gkv = gq_ref[...], gk_ref[...]
        for kh in range(dm.nkv):
            _, kn, v_tb, _ = _kv_band(dm, kh, p_ref, pkv_ref, gkv)
            o = _attn_group_fwd(dm, kh, p_ref, gqv, kn, v_tb, sink_ref, masks)[0]
            for g, h in enumerate(_group_heads(dm, kh)):
                att_scr[h * dm.hd:(h + 1) * dm.hd, :] = o[:, g * BLOCK:(g + 1) * BLOCK]
        att = att_scr[...]
        y_ref[:dm.aw, :] = (att * _col_rms(att) * goa_ref[...]).astype(y_ref.dtype)
        gt = _gate_fwd(dm, p_ref, lng_ref, lnb_ref, ws_ref, bs_ref, _tril())[0]
        y_ref[dm.aw:, :] = (gt * _col_rms(gt) * gog_ref[...]).astype(y_ref.dtype)

    return pl.pallas_call(
        body, name="mixer_fwd", grid=(nseq, nb),
        in_specs=[pl.BlockSpec((dm.inw, BLOCK), cur), pl.BlockSpec((2 * dm.kvw, BLOCK), prev),
                  full(gq.shape), full(gk2.shape), pl.BlockSpec(memory_space=pltpu.SMEM),
                  full(lng.shape), full(lnb.shape), full(ws.shape), full(bs.shape), full(goa.shape), full(gog.shape)],
        out_specs=pl.BlockSpec((dm.d, BLOCK), cur),
        out_shape=jax.ShapeDtypeStruct((dm.d, t), MXU),
        scratch_shapes=[pltpu.VMEM((dm.aw, BLOCK), F32)],
        compiler_params=_cparams(("parallel", "arbitrary"), 40),
    )(proj_t, proj_t, gq, gk2, sinks, lng, lnb, ws, bs, goa, gog)


def _mixer_bwd(proj_t, dy_t, prm, dm, nseq, nb):
    gq, gk2, sinks, lng, lnb, ws, bs, goa, gog = prm
    t = proj_t.shape[1]
    clamp = lambda i: jnp.minimum(i, nb - 1)
    cur, prev, full = _mixer_specs(dm, nb, clamp)
    kvw2 = 2 * dm.kvw

    def prev_kv_out(s, i):
        return (0, s * nb + jnp.maximum(i - 1, 0))

    def body(p_ref, pkv_ref, dy_ref, gq_ref, gk_ref, sink_ref, lng_ref, lnb_ref, ws_ref, bs_ref, goa_ref, gog_ref,
             dproj_ref, dkv_ref, dgq_ref, dgk_ref, dsink_ref, dlng_ref, dlnb_ref, dws_ref, dbs_ref, dgoa_ref, dgog_ref,
             att_scr, datt_scr, carry_scr, prevpart_scr, curpart_scr, kprev_scr,
             a_gq, a_gk, a_sink, a_lng, a_lnb, a_goa, a_gog):
        s_id, i = pl.program_id(0), pl.program_id(1)
        lane_accs = ((a_gq, dgq_ref), (a_gk, dgk_ref), (a_sink, dsink_ref), (a_lng, dlng_ref), (a_lnb, dlnb_ref),
                     (a_goa, dgoa_ref), (a_gog, dgog_ref))

        @pl.when((s_id == 0) & (i == 0))
        def _():
            for acc, _ in lane_accs:
                acc[...] = jnp.zeros_like(acc)
            dws_ref[...] = jnp.zeros_like(dws_ref)
            dbs_ref[...] = jnp.zeros_like(dbs_ref)

        gqv, gkv = gq_ref[...], gk_ref[...]

        @pl.when(i < nb)
        def _():
            masks = _band_masks(i == 0)
            upper = masks[0]
            kvs, fwd = [], []
            for kh in range(dm.nkv):
                kv = _kv_band(dm, kh, p_ref, pkv_ref, gkv)
                kvs.append(kv)
                fwd.append(_attn_group_fwd(dm, kh, p_ref, gqv, kv[1], kv[2], sink_ref, masks))
                for g, h in enumerate(_group_heads(dm, kh)):
                    att_scr[h * dm.hd:(h + 1) * dm.hd, :] = fwd[kh][0][:, g * BLOCK:(g + 1) * BLOCK]
            att = att_scr[...]
            dya = dy_ref[:dm.aw, :]
            ra = _col_rms(att)
            ah = att * ra
            a_goa[...] += dya * ah
            dah = dya * goa_ref[...]
            datt_scr[...] = ra * (dah - ah * jnp.mean(dah * ah, axis=0, keepdims=True))
            for kh in range(dm.nkv):
                kn_tb, kn, v_tb, vb = kvs[kh]
                _, probs, probs_b, sink_probs, rq, qh, qnb = fwd[kh]
                heads = _group_heads(dm, kh)
                do_b = jnp.concatenate([datt_scr[h * dm.hd:(h + 1) * dm.hd, :] for h in heads], axis=1).astype(MXU)
                dp = _dot(vb, do_b)
                ds = []
                for g, h in enumerate(heads):
                    p, dp_h = probs[g], _fold(dp[:, g * BLOCK:(g + 1) * BLOCK], upper)
                    delta = jnp.sum(p * dp_h, axis=0, keepdims=True)
                    ds.append(_unfold(p * (dp_h - delta) * (dm.hd ** -0.5), upper).astype(MXU))
                    a_sink[h:h + 1, :] += -(sink_probs[g] * delta)
                dsb = jnp.concatenate(ds, axis=1)
                dqn = _dot(kn_tb, dsb)
                dkn = _dot_nt(qnb, dsb)
                dvb = _dot_nt(do_b, probs_b)
                a_gq[...] += dqn * qh
                dqh = dqn * gqv
                dq = rq * (dqh - qh * jnp.mean(dqh * qh, axis=0, keepdims=True))
                for g, h in enumerate(heads):
                    dproj_ref[h * dm.hd:(h + 1) * dm.hd, :] = dq[:, g * BLOCK:(g + 1) * BLOCK].astype(dproj_ref.dtype)
                krows = slice(kh * dm.hd, (kh + 1) * dm.hd)
                vrows = slice(dm.kvw + kh * dm.hd, dm.kvw + (kh + 1) * dm.hd)
                prevpart_scr[krows, :] = dkn[:, :BLOCK]
                prevpart_scr[vrows, :] = dvb[:, :BLOCK]
                curpart_scr[krows, :] = dkn[:, BLOCK:]
                curpart_scr[vrows, :] = dvb[:, BLOCK:]
            dproj_ref[dm.aw:dm.zoff, :] = jnp.zeros((kvw2, BLOCK), dproj_ref.dtype)
            tril = _tril()
            gt, u, mixed, xh, rstd, vnb, wts, (zu, cu), (zv, cv) = _gate_fwd(dm, p_ref, lng_ref, lnb_ref, ws_ref, bs_ref, tril)
            dyg = dy_ref[dm.aw:, :]
            rg = _col_rms(gt)
            gh = gt * rg
            a_gog[...] += dyg * gh
            dgh = dyg * gog_ref[...]
            dgt = rg * (dgh - gh * jnp.mean(dgh * gh, axis=0, keepdims=True))
            du = dgt * mixed
            dmix = dgt * u
            dmixb = dmix.astype(MXU)
            dbs_ref[...] += jnp.sum(_by_group(dmix, dm.ng), axis=1)
            dvn = []
            for g in range(dm.ng):
                rows = slice(g * GROUP_DIM, (g + 1) * GROUP_DIM)
                dws_ref[g] += jnp.where(tril, _dot_tn(dmixb[rows], vnb[rows]), 0.0)
                dvn.append(_dot(dmixb[rows], wts[g]))
            dvn = jnp.concatenate(dvn, axis=0)
            a_lng[...] += dvn * xh
            a_lnb[...] += dvn
            dxh3, xh3 = _by_group(dvn * lng_ref[...], dm.ng), _by_group(xh, dm.ng)
            dv = (rstd * (dxh3 - jnp.mean(dxh3, axis=1, keepdims=True) - xh3 * jnp.mean(dxh3 * xh3, axis=1, keepdims=True))).reshape(dm.gw, BLOCK)
            dgu = cu + zu * (jnp.exp(-0.5 * zu * zu) * _INV_SQRT_2PI)
            dgv = cv + zv * (jnp.exp(-0.5 * zv * zv) * _INV_SQRT_2PI)
            dproj_ref[dm.zoff:dm.zoff + dm.gw, :] = (du * dgu).astype(dproj_ref.dtype)
            dproj_ref[dm.zoff + dm.gw:, :] = (dv * dgv).astype(dproj_ref.dtype)

        @pl.when(i == nb)
        def _():
            prevpart_scr[...] = jnp.zeros_like(prevpart_scr)

        @pl.when(i >= 1)
        def _():
            tot = carry_scr[...] + prevpart_scr[...]
            for kh in range(dm.nkv):
                krows = slice(kh * dm.hd, (kh + 1) * dm.hd)
                kraw = kprev_scr[krows, :]
                rk = _col_rms(kraw)
                khat = kraw * rk
                dkn = tot[krows, :]
                a_gk[...] += dkn * khat
                dkh = dkn * gkv[:, :BLOCK]
                dk = rk * (dkh - khat * jnp.mean(dkh * khat, axis=0, keepdims=True))
                dkv_ref[krows, :] = dk.astype(dkv_ref.dtype)
            dkv_ref[dm.kvw:, :] = tot[dm.kvw:, :].astype(dkv_ref.dtype)

        @pl.when(i < nb)
        def _():
            carry_scr[...] = curpart_scr[...]
            kprev_scr[...] = p_ref[dm.aw:dm.aw + dm.kvw, :]

        @pl.when((s_id == nseq - 1) & (i == nb))
        def _():
            for acc, out in lane_accs:
                out[...] = jnp.sum(acc[...], axis=1, keepdims=True)

    col = lambda rows: jax.ShapeDtypeStruct((rows, 1), F32)
    lane = lambda rows: pltpu.VMEM((rows, LANES), F32)
    return pl.pallas_call(
        body, name="mixer_bwd", grid=(nseq, nb + 1),
        in_specs=[pl.BlockSpec((dm.inw, BLOCK), cur), pl.BlockSpec((kvw2, BLOCK), prev), pl.BlockSpec((dm.d, BLOCK), cur),
                  full(gq.shape), full(gk2.shape), pl.BlockSpec(memory_space=pltpu.SMEM),
                  full(lng.shape), full(lnb.shape), full(ws.shape), full(bs.shape), full(goa.shape), full(gog.shape)],
        out_specs=[pl.BlockSpec((dm.inw, BLOCK), cur), pl.BlockSpec((kvw2, BLOCK), prev_kv_out),
                   full((dm.hd, 1)), full((dm.hd, 1)), full((dm.nh, 1)), full((dm.gw, 1)), full((dm.gw, 1)), full(ws.shape),
                   full(bs.shape), full((dm.aw, 1)), full((dm.gw, 1))],
        out_shape=[jax.ShapeDtypeStruct((dm.inw, t), MXU), jax.ShapeDtypeStruct((kvw2, t), MXU),
                   col(dm.hd), col(dm.hd), col(dm.nh), col(dm.gw), col(dm.gw), jax.ShapeDtypeStruct(ws.shape, F32),
                   jax.ShapeDtypeStruct(bs.shape, F32), col(dm.aw), col(dm.gw)],
        scratch_shapes=[pltpu.VMEM((dm.aw, BLOCK), F32), pltpu.VMEM((dm.aw, BLOCK), F32),
                        pltpu.VMEM((kvw2, BLOCK), F32), pltpu.VMEM((kvw2, BLOCK), F32), pltpu.VMEM((kvw2, BLOCK), F32),
                        pltpu.VMEM((dm.kvw, BLOCK), F32),
                        pltpu.VMEM((dm.hd, dm.grp * BLOCK), F32), lane(dm.hd), lane(dm.nh), lane(dm.gw), lane(dm.gw), lane(dm.aw), lane(dm.gw)],
        compiler_params=_cparams(("arbitrary", "arbitrary"), 48),
    )(proj_t, proj_t, dy_t, gq, gk2, sinks, lng, lnb, ws, bs, goa, gog)


def _patch_kv(dproj_t, dkv_t, dm):
    t = dproj_t.shape[1]
    tc = _pick(t, (1024, 512, 256, 128))
    kvw2 = 2 * dm.kvw
    kvblk = dm.aw // kvw2

    def body(dproj_hbm, dkv_ref, out_ref):
        del dproj_hbm
        out_ref[...] = dkv_ref[...]

    return pl.pallas_call(
        body, name="patch_kv", grid=(t // tc,),
        in_specs=[ANY, pl.BlockSpec((kvw2, tc), lambda i: (0, i))],
        out_specs=pl.BlockSpec((kvw2, tc), lambda i: (kvblk, i)),
        out_shape=jax.ShapeDtypeStruct(dproj_t.shape, dproj_t.dtype),
        input_output_aliases={0: 0},
        compiler_params=_cparams(("parallel",), 32),
    )(dproj_t, dkv_t)


def _place():
    x, y, c = lax.axis_index("x"), lax.axis_index("y"), lax.axis_index("c")
    return x, y, c


def _handshake(peers):
    barrier = pltpu.get_barrier_semaphore()
    for p in peers:
        pl.semaphore_signal(barrier, inc=1, device_id=p, device_id_type=MESH)
    pl.semaphore_wait(barrier, len(peers))


def _sequencer_mesh():
    return plsc.ScalarSubcoreMesh(axis_name="sequencer", num_cores=1)


GATHER_CHUNKS = 4
BF16_ROWS = 16


def _row_chunks(n, k):
    tiles = n // BF16_ROWS
    sizes = [(tiles // k + (1 if i < tiles % k else 0)) * BF16_ROWS for i in range(k)]
    return [(sum(sizes[:i]), sz) for i, sz in enumerate(sizes) if sz]


def _allgather_weight(name, collective_id, shard):
    n = shard.shape[0]
    assert n % BF16_ROWS == 0
    chunks = _row_chunks(n, GATHER_CHUNKS)
    nc = len(chunks)

    def body(src, out, send_sems, recv_sems, local_sem):
        x, y, c = _place()
        me, sib, xn, yn, diag = (x, y, c), (x, y, 1 - c), (1 - x, y, c), (x, 1 - y, c), (1 - x, 1 - y, c)
        relay_to = (x ^ c, y ^ (1 - c), c)
        relay_of = (x ^ (1 - c), y ^ c, c)
        _handshake([sib, xn, yn])

        def rows(place, ci):
            px, py, pc = place
            off, size = chunks[ci]
            return out.at[pl.ds(pl.multiple_of((4 * px + 2 * py + pc) * n + off, BF16_ROWS), size), :]

        def copy(k, ci, block, to, from_src=False):
            off, size = chunks[ci]
            return pltpu.make_async_remote_copy(
                src_ref=src.at[pl.ds(off, size), :] if from_src else rows(block, ci), dst_ref=rows(block, ci),
                send_sem=send_sems.at[ci, k], recv_sem=recv_sems.at[ci, k], device_id=to, device_id_type=MESH)

        mine = pltpu.make_async_copy(src, out.at[pl.ds(pl.multiple_of((4 * x + 2 * y + c) * n, BF16_ROWS), n), :], local_sem)
        mine.start()
        sent = []
        for ci in range(nc):
            sent += [copy(0, ci, me, sib, from_src=True), copy(1, ci, me, xn, from_src=True), copy(2, ci, me, yn, from_src=True)]
        for cp in sent:
            cp.start()
        for ci in range(nc):
            copy(1, ci, xn, me).wait_recv()
            copy(2, ci, yn, me).wait_recv()
            passed = [copy(3, ci, relay_of, relay_to), copy(4, ci, xn, sib), copy(5, ci, yn, sib)]
            for cp in passed:
                cp.start()
            sent += passed
        for ci in range(nc):
            copy(3, ci, diag, me).wait_recv()
            passed = copy(6, ci, diag, sib)
            passed.start()
            sent.append(passed)
        for ci in range(nc):
            copy(0, ci, sib, me).wait_recv()
            for k, block in ((4, (1 - x, y, 1 - c)), (5, (x, 1 - y, 1 - c)), (6, (1 - x, 1 - y, 1 - c))):
                copy(k, ci, block, me).wait_recv()
        for cp in sent:
            cp.wait_send()
        mine.wait()

    return pl.kernel(
        body, name=name,
        out_type=jax.ShapeDtypeStruct((N_DEV * n, shard.shape[1]), shard.dtype),
        mesh=_sequencer_mesh(),
        scratch_types=[pltpu.SemaphoreType.DMA((nc, 7)), pltpu.SemaphoreType.DMA((nc, 7)), pltpu.SemaphoreType.DMA],
        compiler_params=pltpu.CompilerParams(collective_id=collective_id),
    )(shard)


_FLIPS = [(0, 0, 1), (1, 0, 0), (0, 1, 0), (1, 1, 0), (1, 0, 1), (0, 1, 1), (1, 1, 1)]


def _scatter_grad(name, collective_id, grad):
    n = grad.shape[0] // N_DEV

    def body(src, out, send_sems, recv_sems, local_sem):
        x, y, c = _place()
        me_idx = 4 * x + 2 * y + c
        peers = [(x ^ fx, y ^ fy, c ^ fc) for (fx, fy, fc) in _FLIPS]
        _handshake(peers)

        def block(idx):
            return src.at[pl.ds(pl.multiple_of(idx * n, 16), n), :]

        copies = [pltpu.make_async_remote_copy(
            src_ref=block(4 * px + 2 * py + pc), dst_ref=out.at[me_idx], send_sem=send_sems.at[k], recv_sem=recv_sems.at[k],
            device_id=(px, py, pc), device_id_type=MESH) for k, (px, py, pc) in enumerate(peers)]
        mine = pltpu.make_async_copy(block(me_idx), out.at[me_idx], local_sem)
        mine.start()
        for cp in copies:
            cp.start()
        for cp in copies:
            cp.wait_recv()
        for cp in copies:
            cp.wait_send()
        mine.wait()

    return pl.kernel(
        body, name=name,
        out_type=jax.ShapeDtypeStruct((N_DEV, n, grad.shape[1]), grad.dtype),
        mesh=_sequencer_mesh(),
        scratch_types=[pltpu.SemaphoreType.DMA((7,)), pltpu.SemaphoreType.DMA((7,)), pltpu.SemaphoreType.DMA],
        compiler_params=pltpu.CompilerParams(collective_id=collective_id),
    )(grad)


def _allgather_rows(name, collective_id, part):
    def body(src, out, send_sems, recv_sems, local_sem):
        x, y, c = _place()
        me_idx = 4 * x + 2 * y + c
        peers = [(x ^ fx, y ^ fy, c ^ fc) for (fx, fy, fc) in _FLIPS]
        _handshake(peers)
        copies = [pltpu.make_async_remote_copy(
            src_ref=src, dst_ref=out.at[me_idx], send_sem=send_sems.at[k], recv_sem=recv_sems.at[k],
            device_id=peer, device_id_type=MESH) for k, peer in enumerate(peers)]
        mine = pltpu.make_async_copy(src, out.at[me_idx], local_sem)
        mine.start()
        for cp in copies:
            cp.start()
        for cp in copies:
            cp.wait_recv()
        for cp in copies:
            cp.wait_send()
        mine.wait()

    return pl.kernel(
        body, name=name,
        out_type=jax.ShapeDtypeStruct((N_DEV,) + part.shape, part.dtype),
        mesh=_sequencer_mesh(),
        scratch_types=[pltpu.SemaphoreType.DMA((7,)), pltpu.SemaphoreType.DMA((7,)), pltpu.SemaphoreType.DMA],
        compiler_params=pltpu.CompilerParams(collective_id=collective_id),
    )(part)


def _adamw_math(w, g, m, v):
    m = ADAM_B1 * m + (1.0 - ADAM_B1) * g
    v = ADAM_B2 * v + (1.0 - ADAM_B2) * (g * g)
    m_hat = m / (1.0 - ADAM_B1 ** ADAM_STEP)
    v_hat = v / (1.0 - ADAM_B2 ** ADAM_STEP)
    delta = -ADAM_LR * (m_hat / (jnp.sqrt(v_hat) + ADAM_EPS) + ADAM_WD * w)
    return delta, m, v


def _sum_adamw(name, slots, w, m, v, after):
    _, n, kk = slots.shape
    tr = _pick(n, (208, 176, 128, 96, 64, 32, 16))

    def body(s_ref, w_ref, m_ref, v_ref, after_ref, g_ref, d_ref, nm_ref, nv_ref):
        del after_ref
        g = s_ref[0].astype(F32)
        for p in range(1, N_DEV):
            g = g + s_ref[p].astype(F32)
        g_ref[...] = g
        d_ref[...], nm_ref[...], nv_ref[...] = _adamw_math(w_ref[...], g, m_ref[...], v_ref[...])

    row = pl.BlockSpec((tr, kk), lambda i: (i, 0))
    return pl.pallas_call(
        body, name=name, grid=(n // tr,),
        in_specs=[pl.BlockSpec((N_DEV, tr, kk), lambda i: (0, i, 0)), row, row, row, ANY],
        out_specs=[row] * 4,
        out_shape=[jax.ShapeDtypeStruct((n, kk), F32)] * 4,
        compiler_params=_cparams(("parallel",), 48),
    )(slots, w, m, v, after)


def _allreduce_small_adamw(early_slots, late, w, m, v, after):
    ra, rb = early_slots.shape[1], late.shape[0]

    def body(early_ref, late_ref, w_ref, m_ref, v_ref, after_ref, g_ref, d_ref, nm_ref, nv_ref, slots, send_sems, recv_sems):
        del after_ref
        x, y, c = _place()
        me_idx = 4 * x + 2 * y + c
        copies = []
        for k, (fx, fy, fc) in enumerate(_FLIPS):
            px, py, pc = x ^ fx, y ^ fy, c ^ fc
            copies.append(pltpu.make_async_remote_copy(
                src_ref=late_ref, dst_ref=slots.at[me_idx], send_sem=send_sems.at[k], recv_sem=recv_sems.at[k],
                device_id=(px, py, pc), device_id_type=MESH))
        for cp in copies:
            cp.start()
        slots[me_idx] = late_ref[...]
        g = early_ref[0]
        for p in range(1, N_DEV):
            g = g + early_ref[p]
        early = pl.ds(0, ra)
        g_ref[early, :] = g
        d_ref[early, :], nm_ref[early, :], nv_ref[early, :] = _adamw_math(w_ref[early, :], g, m_ref[early, :], v_ref[early, :])
        for cp in copies:
            cp.wait_recv()
        for cp in copies:
            cp.wait_send()
        g = slots[0]
        for p in range(1, N_DEV):
            g = g + slots[p]
        tail = pl.ds(ra, rb)
        g_ref[tail, :] = g
        d_ref[tail, :], nm_ref[tail, :], nv_ref[tail, :] = _adamw_math(w_ref[tail, :], g, m_ref[tail, :], v_ref[tail, :])

    vm = pl.BlockSpec(memory_space=pltpu.VMEM)
    return pl.pallas_call(
        body, name="allreduce_small_adamw",
        in_specs=[vm] * 5 + [ANY], out_specs=[vm] * 4,
        out_shape=[jax.ShapeDtypeStruct((ra + rb, LANES), F32)] * 4,
        scratch_shapes=[pltpu.VMEM((N_DEV, rb, LANES), F32), pltpu.SemaphoreType.DMA((7,)), pltpu.SemaphoreType.DMA((7,))],
        compiler_params=pltpu.CompilerParams(vmem_limit_bytes=48 * MIB),
    )(early_slots, late, w, m, v, after)


def _pack(arrs):
    parts, meta, off = [], [], 0
    for a in arrs:
        flat = a.reshape(-1).astype(F32)
        rows = -(-flat.shape[0] // LANES)
        rows8 = -(-rows // 8) * 8
        flat = jnp.pad(flat, (0, rows8 * LANES - flat.shape[0]))
        parts.append(flat.reshape(rows8, LANES))
        meta.append((off, a.shape, a.size))
        off += rows8
    return jnp.concatenate(parts, axis=0), meta


def _unpack(packed, meta):
    outs = []
    for off, shape, size in meta:
        rows = -(-size // LANES)
        outs.append(packed[off:off + rows].reshape(-1)[:size].reshape(shape))
    return outs


def _silu_parts(a):
    sg = 0.5 + 0.5 * jnp.tanh(0.5 * a)
    return a * sg, sg * (1.0 + a * (1.0 - sg))


def kernel(x, norm1_g, w_in, q_norm_g, k_norm_g, attn_sinks, gate_ln_g, gate_ln_b, w_spatial, b_spatial, out_norm_attn_g, out_norm_gate_g, w_out, norm2_g, w_ffn_gate, w_ffn_up, w_ffn_down, loss_target, m_norm1_g, m_w_in, m_q_norm_g, m_k_norm_g, m_attn_sinks, m_gate_ln_g, m_gate_ln_b, m_w_spatial, m_b_spatial, m_out_norm_attn_g, m_out_norm_gate_g, m_w_out, m_norm2_g, m_w_ffn_gate, m_w_ffn_up, m_w_ffn_down, v_norm1_g, v_w_in, v_q_norm_g, v_k_norm_g, v_attn_sinks, v_gate_ln_g, v_gate_ln_b, v_w_spatial, v_b_spatial, v_out_norm_attn_g, v_out_norm_gate_g, v_w_out, v_norm2_g, v_w_ffn_gate, v_w_ffn_up, v_w_ffn_down):
    nseq, seq, d = x.shape
    t = nseq * seq
    nb = seq // BLOCK
    inw = w_in.shape[2] * N_DEV
    dm = _Dims(d, inw, q_norm_g.shape[-1])
    xf = x.reshape(t, d)
    tgt = loss_target.reshape(t, d)

    rows = lambda wv, transposed: jnp.swapaxes(wv, 1, 2)[0] if transposed else wv[0]
    big = {"w_in": (w_in, m_w_in, v_w_in, True), "w_out": (w_out, m_w_out, v_w_out, False),
           "w_ffn_gate": (w_ffn_gate, m_w_ffn_gate, v_w_ffn_gate, True), "w_ffn_up": (w_ffn_up, m_w_ffn_up, v_w_ffn_up, True),
           "w_ffn_down": (w_ffn_down, m_w_ffn_down, v_w_ffn_down, False)}
    big_rows = {nm: tuple(rows(arr, tr) for arr in (wv, mv, vv)) for nm, (wv, mv, vv, tr) in big.items()}
    shard = lambda nm: big_rows[nm][0].astype(WIRE)
    win_t = _allgather_weight("gather_w_in", 1, shard("w_in"))
    wout = _allgather_weight("gather_w_out", 2, shard("w_out"))
    wg_t = _allgather_weight("gather_w_ffn_gate", 3, shard("w_ffn_gate"))
    wu_t = _allgather_weight("gather_w_ffn_up", 9, shard("w_ffn_up"))
    wd = _allgather_weight("gather_w_ffn_down", 10, shard("w_ffn_down"))

    lanes = lambda v, n=BLOCK: jnp.broadcast_to(v.reshape(-1, 1), (v.size, n))
    prm = (lanes(q_norm_g, dm.grp * BLOCK), lanes(k_norm_g, 2 * BLOCK), attn_sinks[0], lanes(gate_ln_g), lanes(gate_ln_b), w_spatial[0], b_spatial[0],
           lanes(out_norm_attn_g), lanes(out_norm_gate_g))

    h1 = _rms_fwd("rms1_fwd", xf, norm1_g)
    (proj_t,) = _matmul("mm_in", win_t, h1, "nt", [F32])
    y_t = _mixer_fwd(proj_t, prm, dm, nseq, nb)

    def residual_norm(acc, xr, g2):
        x2v = xr + acc
        return x2v, x2v * lax.rsqrt(jnp.mean(x2v * x2v, axis=-1, keepdims=True) + EPS) * g2

    x2, h2 = _matmul("mm_out", y_t, wout, "tn", [F32, MXU], epilogue=residual_norm, extras=[xf], rowvecs=[norm2_g], full_rows=True)
    silu_a, dsilu_a = _matmul("mm_gate", h2, wg_t, "nt", [MXU, MXU], epilogue=_silu_parts)
    s, dsilu_a_b = _matmul("mm_up", h2, wu_t, "nt", [MXU, MXU], epilogue=lambda ub, sv, dv: (sv.astype(F32) * ub, dv.astype(F32) * ub),
                           extras=[silu_a, dsilu_a])

    def loss_epilogue(acc, x2v, tv):
        diff = (x2v + acc) - tv
        dx3 = diff * (1.0 / d)
        return dx3, dx3, jnp.sum(diff * diff)

    dx3, dx3b, lossp = _matmul("mm_down", s, wd, "nn", [F32, MXU], epilogue=loss_epilogue, extras=[x2, tgt], partial=True)
    loss_part = (0.5 / d) * jnp.sum(lossp[::8, ::LANES])

    def dswiglu(ds, dsilu_b, silu):
        return ds * dsilu_b.astype(F32), ds * silu.astype(F32)

    (g_wd,) = _matmul("mm_gw_down", s, dx3b, "tn", [WIRE])
    sl_wd = _scatter_grad("scatter_w_ffn_down", 4, g_wd)
    da, db = _matmul("mm_d_down", dx3b, wd, "nt", [MXU, MXU], epilogue=dswiglu, extras=[dsilu_a_b, silu_a], after=[g_wd])
    (g_wg,) = _matmul("mm_gw_gate", da, h2, "tn", [WIRE])
    sl_wg = _scatter_grad("scatter_w_ffn_gate", 5, g_wg)
    (g_wu,) = _matmul("mm_gw_up", db, h2, "tn", [WIRE], after=[g_wg])
    sl_wu = _scatter_grad("scatter_w_ffn_up", 6, g_wu)
    (dh2a,) = _matmul("mm_dh2_gate", da, wg_t, "nn", [F32], after=[g_wu])
    (dh2,) = _matmul("mm_dh2_up", db, wu_t, "nn", [F32], epilogue=lambda acc, pv: (pv + acc,), extras=[dh2a])

    dy_t, dx2, dx2b, dg2 = _norm_bwd_matmul("mm_d_out", wout, dh2, x2, norm2_g, dx3, after=dh2a)
    (g_wout,) = _matmul("mm_gw_out", y_t, dx2b, "nn", [WIRE], after=[dy_t])
    sl_wout = _scatter_grad("scatter_w_out", 7, g_wout)
    (dproj0, dkv, dgq, dgk, dsink, dlng, dlnb, dws, dbs, dgoa, dgog) = _mixer_bwd(proj_t, dy_t, prm, dm, nseq, nb)
    early_g = [dgq, dgk, dsink, dlng, dlnb, dws, dbs, dgoa, dgog, dg2, loss_part.reshape(1)]
    early_slots = _allgather_rows("gather_small_grads", 11, _pack(early_g)[0])
    dproj_t = _patch_kv(dproj0, dkv, dm)
    (g_win,) = _matmul("mm_gw_in", dproj_t, h1, "nn", [WIRE])
    sl_win = _scatter_grad("scatter_w_in", 8, g_win)

    def norm1_backward(dh1, xv, dx2v, g1):
        r = lax.rsqrt(jnp.mean(xv * xv, axis=-1, keepdims=True) + EPS)
        xh = xv * r
        dxh = dh1 * g1
        return dx2v + r * (dxh - xh * jnp.mean(dxh * xh, axis=-1, keepdims=True)), jnp.sum(dh1 * xh, axis=0, keepdims=True)

    dx, dg1 = _matmul("mm_d_in", dproj_t, win_t, "tn", [F32], epilogue=norm1_backward, extras=[xf, dx2], rowvecs=[norm1_g],
                      after=[g_win], col_sum=True, full_rows=True)

    big_out = {}
    last = dx

    def big_update(nm, sl, after):
        res = _sum_adamw("adamw_" + nm, sl, *big_rows[nm], after=after)
        big_out[nm] = tuple(jnp.swapaxes(r[None], 1, 2) if big[nm][3] else r[None] for r in res)
        return res[1]

    for nm, sl in (("w_ffn_down", sl_wd), ("w_ffn_gate", sl_wg), ("w_ffn_up", sl_wu), ("w_out", sl_wout)):
        last = big_update(nm, sl, last)

    zero = jnp.zeros((1,), F32)
    small_names = ["q_norm_g", "k_norm_g", "attn_sinks", "gate_ln_g", "gate_ln_b", "w_spatial", "b_spatial",
                   "out_norm_attn_g", "out_norm_gate_g", "norm2_g", "loss", "norm1_g"]
    small_w = [q_norm_g, k_norm_g, attn_sinks, gate_ln_g, gate_ln_b, w_spatial, b_spatial, out_norm_attn_g, out_norm_gate_g, norm2_g, zero, norm1_g]
    small_m = [m_q_norm_g, m_k_norm_g, m_attn_sinks, m_gate_ln_g, m_gate_ln_b, m_w_spatial, m_b_spatial, m_out_norm_attn_g, m_out_norm_gate_g, m_norm2_g, zero, m_norm1_g]
    small_v = [v_q_norm_g, v_k_norm_g, v_attn_sinks, v_gate_ln_g, v_gate_ln_b, v_w_spatial, v_b_spatial, v_out_norm_attn_g, v_out_norm_gate_g, v_norm2_g, zero, v_norm1_g]
    pw, meta = _pack(small_w)
    sg, sd, sm, sv = _allreduce_small_adamw(early_slots, _pack([dg1])[0], pw, _pack(small_m)[0], _pack(small_v)[0], after=last)
    big_update("w_in", sl_win, sd)
    ug, ud, um, uv = _unpack(sg, meta), _unpack(sd, meta), _unpack(sm, meta), _unpack(sv, meta)
    small_out = {nm: (ug[k], ud[k], um[k], uv[k]) for k, nm in enumerate(small_names)}
    loss = small_out["loss"][0].reshape(())

    order = ["norm1_g", "w_in", "q_norm_g", "k_norm_g", "attn_sinks", "gate_ln_g", "gate_ln_b", "w_spatial", "b_spatial",
             "out_norm_attn_g", "out_norm_gate_g", "w_out", "norm2_g", "w_ffn_gate", "w_ffn_up", "w_ffn_down"]
    allo = {**big_out, **small_out}
    outs = [loss, dx.reshape(nseq, seq, d)]
    for k in range(4):
        outs += [allo[nm][k] for nm in order]
    return tuple(outs)
```

```python
import math

import jax
import jax.numpy as jnp
from jax import lax
from jax.experimental import pallas as pl
from jax.experimental.pallas import tpu as pltpu
from jax.experimental.pallas import tpu_sc as plsc

F32 = jnp.float32
MXU = jnp.bfloat16
WIRE = jnp.bfloat16
EPS = 1e-6
BLOCK = 128
GROUP_DIM = 128
N_KV_HEADS = 2
NEG = -1e30
N_DEV = 8
LANES = 128
MIB = 1024 * 1024

ADAM_LR = 0.001
ADAM_B1 = 0.9
ADAM_B2 = 0.999
ADAM_EPS = 1e-08
ADAM_WD = 0.01
ADAM_STEP = 10

MESH = pl.DeviceIdType.MESH
ANY = pl.BlockSpec(memory_space=pl.ANY)


def _pick(n, cands):
    for c in cands:
        if n % c == 0:
            return c
    return n


def _cparams(sem, vmem_mb):
    return pltpu.CompilerParams(dimension_semantics=sem, vmem_limit_bytes=vmem_mb * MIB)


VMEM_TILE_BUDGET = 50 * MIB
HBM_BYTES_PER_US = 3.0e6
STEP_US = 0.4
MXU_COLS = 256
MIN_TILE_N = 2 * MXU_COLS


def _tile_candidates(n):
    return [c for c in range(min(n, 2048), 0, -LANES) if n % c == 0 and c % LANES == 0] or [n]


def _matmul_tiles(m, n, kk, esz, extra_sizes, out_sizes, full_rows):
    best = None
    wide = [n] if full_rows else [c for c in _tile_candidates(n) if c >= MIN_TILE_N and c % MXU_COLS == 0] or _tile_candidates(n)
    for tm in _tile_candidates(m):
        for tn in wide:
            b_buffers = 1 if full_rows else 2
            temps = 2 if full_rows else 0
            vmem = (2 * tm + b_buffers * tn) * kk * esz + tm * tn * (4 * (1 + temps) + 2 * sum(extra_sizes) + 2 * sum(out_sizes))
            if vmem > VMEM_TILE_BUDGET:
                continue
            cost = (m // tm) * n * kk * esz / HBM_BYTES_PER_US + (m // tm) * (n // tn) * STEP_US
            if best is None or cost < best[0]:
                best = (cost, tm, tn, vmem)
    assert best is not None, (m, n, kk)
    return best[1:]


def _matmul(name, a, b, mode, out_dtypes, epilogue=None, extras=(), rowvecs=(), after=(), partial=False, col_sum=False, full_rows=False):
    if mode == "nn":
        (m, kk), n = a.shape, b.shape[1]
        dn = (((1,), (0,)), ((), ()))
    elif mode == "nt":
        (m, kk), n = a.shape, b.shape[0]
        dn = (((1,), (1,)), ((), ()))
    else:
        (kk, m), n = a.shape, b.shape[1]
        dn = (((0,), (0,)), ((), ()))
    tm, tn, vmem = _matmul_tiles(m, n, kk, a.dtype.itemsize, [e.dtype.itemsize for e in extras],
                                 [jnp.dtype(dt).itemsize for dt in out_dtypes], full_rows)
    a_spec = pl.BlockSpec((kk, tm), lambda i, j: (0, i)) if mode == "tn" else pl.BlockSpec((tm, kk), lambda i, j: (i, 0))
    resident = dict(pipeline_mode=pl.Buffered(1)) if full_rows else {}
    b_spec = pl.BlockSpec((tn, kk), lambda i, j: (j, 0), **resident) if mode == "nt" else pl.BlockSpec((kk, tn), lambda i, j: (0, j), **resident)
    tile = pl.BlockSpec((tm, tn), lambda i, j: (i, j))
    row = pl.BlockSpec((1, tn), lambda i, j: (0, j))
    ne, nr, na, no = len(extras), len(rowvecs), len(after), len(out_dtypes)

    def body(a_ref, b_ref, *rest):
        in_refs, out_refs = rest[:ne + nr], rest[ne + nr + na:]
        acc = lax.dot_general(a_ref[...], b_ref[...], dn, preferred_element_type=F32)
        vals = (acc,) if epilogue is None else epilogue(acc, *[r[...] for r in in_refs])
        for o_ref, t in zip(out_refs[:no], vals[:no]):
            o_ref[...] = t.astype(o_ref.dtype)
        if partial:
            out_refs[no][...] = jnp.full((8, LANES), vals[no], F32)
        if col_sum:
            sum_ref = out_refs[-1]

            @pl.when(pl.program_id(0) == 0)
            def _():
                sum_ref[...] = jnp.zeros_like(sum_ref)

            sum_ref[...] += vals[-1]

    out_specs = [tile] * no
    out_shape = [jax.ShapeDtypeStruct((m, n), dt) for dt in out_dtypes]
    if partial:
        out_specs.append(pl.BlockSpec((8, LANES), lambda i, j: (i, j)))
        out_shape.append(jax.ShapeDtypeStruct((m // tm * 8, n // tn * LANES), F32))
    if col_sum:
        out_specs.append(row)
        out_shape.append(jax.ShapeDtypeStruct((1, n), F32))
    return pl.pallas_call(
        body, name=name, grid=(m // tm, n // tn),
        in_specs=[a_spec, b_spec] + [tile] * ne + [row] * nr + [ANY] * na,
        out_specs=out_specs, out_shape=out_shape,
        compiler_params=_cparams(("arbitrary" if col_sum else "parallel", "arbitrary"), min(vmem // MIB + 8, 58)),
    )(a, b, *extras, *rowvecs, *after)


def _rms_fwd(name, x, g):
    t, d = x.shape
    tm = _pick(t, (512, 256, 128))

    def body(x_ref, g_ref, h_ref):
        xv = x_ref[...]
        r = lax.rsqrt(jnp.mean(xv * xv, axis=-1, keepdims=True) + EPS)
        h_ref[...] = (xv * r * g_ref[...]).astype(h_ref.dtype)

    return pl.pallas_call(
        body, name=name, grid=(t // tm,),
        in_specs=[pl.BlockSpec((tm, d), lambda i: (i, 0)), pl.BlockSpec((1, d), lambda i: (0, 0))],
        out_specs=pl.BlockSpec((tm, d), lambda i: (i, 0)),
        out_shape=jax.ShapeDtypeStruct((t, d), MXU),
        compiler_params=_cparams(("parallel",), 32),
    )(x, g)


def _norm_bwd_matmul(name, w, dh, x, g, res, after):
    t, d = x.shape
    m = w.shape[0]
    tn = _pick(t, (256, 128))

    def body(w_ref, dh_ref, x_ref, g_ref, res_ref, after_ref, out_ref, dx_ref, dxb_ref, dg_ref):
        del after_ref

        @pl.when(pl.program_id(0) == 0)
        def _():
            dg_ref[...] = jnp.zeros_like(dg_ref)

        xv, dhv = x_ref[...], dh_ref[...]
        r = lax.rsqrt(jnp.mean(xv * xv, axis=-1, keepdims=True) + EPS)
        xh = xv * r
        dg_ref[...] += jnp.sum(dhv * xh, axis=0, keepdims=True)
        dxh = dhv * g_ref[...]
        dx = res_ref[...] + r * (dxh - xh * jnp.mean(dxh * xh, axis=-1, keepdims=True))
        dx_ref[...] = dx
        dxb = dx.astype(MXU)
        dxb_ref[...] = dxb
        out_ref[...] = lax.dot_general(w_ref[...], dxb, (((1,), (1,)), ((), ())), preferred_element_type=F32)

    row = pl.BlockSpec((tn, d), lambda j: (j, 0))
    vec = pl.BlockSpec((1, d), lambda j: (0, 0))
    return pl.pallas_call(
        body, name=name, grid=(t // tn,),
        in_specs=[pl.BlockSpec((m, d), lambda j: (0, 0), pipeline_mode=pl.Buffered(1)), row, row, vec, row, ANY],
        out_specs=[pl.BlockSpec((m, tn), lambda j: (0, j)), row, row, vec],
        out_shape=[jax.ShapeDtypeStruct((m, t), F32), jax.ShapeDtypeStruct((t, d), F32), jax.ShapeDtypeStruct((t, d), MXU),
                   jax.ShapeDtypeStruct((1, d), F32)],
        compiler_params=_cparams(("arbitrary",), 52),
    )(w, dh, x, g, res, after)


_INV_SQRT2 = 0.7071067811865476
_INV_SQRT_2PI = 0.3989422804014327


def _dot_nt(a, b):
    return lax.dot_general(a, b, (((1,), (1,)), ((), ())), preferred_element_type=F32)


def _dot_tn(a, b):
    return lax.dot_general(a, b, (((0,), (0,)), ((), ())), preferred_element_type=F32)


def _dot(a, b):
    return jnp.dot(a, b, preferred_element_type=F32)


def _col_rms(v):
    return lax.rsqrt(jnp.mean(v * v, axis=0, keepdims=True) + EPS)


class _Dims:
    def __init__(self, d_model, in_width, head_dim):
        self.d = d_model
        self.aw = d_model // 2
        self.gw = d_model - self.aw
        self.kvw = (in_width - self.aw - 2 * self.gw) // 2
        self.hd = head_dim
        self.nh = self.aw // head_dim
        self.nkv = self.kvw // head_dim
        self.grp = self.nh // self.nkv
        self.ng = self.gw // GROUP_DIM
        self.inw = in_width
        self.zoff = self.aw + 2 * self.kvw
        assert self.nkv == N_KV_HEADS and self.zoff + 2 * self.gw == in_width and self.aw % (2 * self.kvw) == 0


def _band_masks(first):
    r = lax.broadcasted_iota(jnp.int32, (BLOCK, BLOCK), 0)
    t = lax.broadcasted_iota(jnp.int32, (BLOCK, BLOCK), 1)
    upper = r > t
    dist = t - r + jnp.where(upper, BLOCK, 0)
    return upper, jnp.logical_not(upper & first), dist.astype(F32)


def _fold(full, upper):
    return jnp.where(upper, full[:BLOCK], full[BLOCK:])


def _unfold(folded, upper):
    zero = jnp.zeros_like(folded)
    return jnp.concatenate([jnp.where(upper, folded, zero), jnp.where(upper, zero, folded)], axis=0)


def _kv_band(dm, kh, p_ref, pkv_ref, gk2):
    ko = dm.aw + kh * dm.hd
    vo = dm.aw + dm.kvw + kh * dm.hd
    k_t = jnp.concatenate([pkv_ref[kh * dm.hd:(kh + 1) * dm.hd, :], p_ref[ko:ko + dm.hd, :]], axis=1)
    v_t = jnp.concatenate([pkv_ref[dm.kvw + kh * dm.hd:dm.kvw + (kh + 1) * dm.hd, :], p_ref[vo:vo + dm.hd, :]], axis=1)
    kn_t = k_t * _col_rms(k_t) * gk2
    return kn_t.astype(MXU), kn_t.T.astype(MXU), v_t.astype(MXU), v_t.T.astype(MXU)


def _group_heads(dm, kh):
    return range(kh * dm.grp, (kh + 1) * dm.grp)


def _attn_group_fwd(dm, kh, p_ref, gq, kn, v_tb, sink_ref, masks):
    heads = _group_heads(dm, kh)
    q = jnp.concatenate([p_ref[h * dm.hd:(h + 1) * dm.hd, :] for h in heads], axis=1)
    rq = _col_rms(q)
    qh = q * rq
    qnb = (qh * gq).astype(MXU)
    upper, valid, dist = masks
    s = _dot(kn, qnb)
    probs, probs_b, sink_probs = [], [], []
    for g, h in enumerate(heads):
        slope, sink = math.pow(2.0, -8.0 * (h + 1) / dm.nh), sink_ref[h]
        logits = jnp.where(valid, _fold(s[:, g * BLOCK:(g + 1) * BLOCK], upper) * (dm.hd ** -0.5) - slope * dist, NEG)
        m = jnp.maximum(jnp.max(logits, axis=0, keepdims=True), sink)
        e = jnp.exp(logits - m)
        es = jnp.exp(sink - m)
        inv = 1.0 / (jnp.sum(e, axis=0, keepdims=True) + es)
        probs.append(e * inv)
        probs_b.append(_unfold(probs[g], upper).astype(MXU))
        sink_probs.append(es * inv)
    probs_b = jnp.concatenate(probs_b, axis=1)
    o = _dot(v_tb, probs_b)
    return o, probs, probs_b, sink_probs, rq, qh, qnb


def _gelu_cdf(z):
    return 0.5 * (1.0 + lax.erf(z * _INV_SQRT2))


def _by_group(v, ng):
    return v.reshape(ng, GROUP_DIM, v.shape[1])


def _gate_fwd(dm, p_ref, lng_ref, lnb_ref, ws_ref, bs_ref, tril):
    zu, zv = p_ref[dm.zoff:dm.zoff + dm.gw, :], p_ref[dm.zoff + dm.gw:dm.zoff + 2 * dm.gw, :]
    cu, cv = _gelu_cdf(zu), _gelu_cdf(zv)
    u, v = zu * cu, zv * cv
    v3 = _by_group(v, dm.ng)
    xc = v3 - jnp.mean(v3, axis=1, keepdims=True)
    rstd = lax.rsqrt(jnp.mean(xc * xc, axis=1, keepdims=True) + EPS)
    xh = (xc * rstd).reshape(dm.gw, BLOCK)
    vnb = (xh * lng_ref[...] + lnb_ref[...]).astype(MXU)
    wts = [jnp.where(tril, ws_ref[g], 0.0).astype(MXU) for g in range(dm.ng)]
    mixed = jnp.concatenate([_dot_nt(vnb[g * GROUP_DIM:(g + 1) * GROUP_DIM], wts[g]) + bs_ref[g:g + 1, :]
                             for g in range(dm.ng)], axis=0)
    return u * mixed, u, mixed, xh, rstd, vnb, wts, (zu, cu), (zv, cv)


def _mixer_specs(dm, nb, clamp):
    kvblk = dm.aw // (2 * dm.kvw)

    def cur(s, i):
        return (0, s * nb + clamp(i))

    def prev(s, i):
        return (kvblk, s * nb + jnp.maximum(clamp(i) - 1, 0))

    full = lambda shape: pl.BlockSpec(shape, lambda s, i: tuple(0 for _ in shape))
    return cur, prev, full


def _tril():
    return lax.broadcasted_iota(jnp.int32, (BLOCK, BLOCK), 0) >= lax.broadcasted_iota(jnp.int32, (BLOCK, BLOCK), 1)


def _mixer_fwd(proj_t, prm, dm, nseq, nb):
    gq, gk2, sinks, lng, lnb, ws, bs, goa, gog = prm
    t = proj_t.shape[1]
    cur, prev, full = _mixer_specs(dm, nb, lambda i: i)

    def body(p_ref, pkv_ref, gq_ref, gk_ref, sink_ref, lng_ref, lnb_ref, ws_ref, bs_ref, goa_ref, gog_ref, y_ref, att_scr):
        i = pl.program_id(1)
        masks = _band_masks(i == 0)
        gqv, gkv = gq_ref[...], gk_ref[...]
        for kh in range(dm.nkv):
            _, kn, v_tb, _ = _kv_band(dm, kh, p_ref, pkv_ref, gkv)
            o = _attn_group_fwd(dm, kh, p_ref, gqv, kn, v_tb, sink_ref, masks)[0]
            for g, h in enumerate(_group_heads(dm, kh)):
                att_scr[h * dm.hd:(h + 1) * dm.hd, :] = o[:, g * BLOCK:(g + 1) * BLOCK]
        att = att_scr[...]
        y_ref[:dm.aw, :] = (att * _col_rms(att) * goa_ref[...]).astype(y_ref.dtype)
        gt = _gate_fwd(dm, p_ref, lng_ref, lnb_ref, ws_ref, bs_ref, _tril())[0]
        y_ref[dm.aw:, :] = (gt * _col_rms(gt) * gog_ref[...]).astype(y_ref.dtype)

    return pl.pallas_call(
        body, name="mixer_fwd", grid=(nseq, nb),
        in_specs=[pl.BlockSpec((dm.inw, BLOCK), cur), pl.BlockSpec((2 * dm.kvw, BLOCK), prev),
                  full(gq.shape), full(gk2.shape), pl.BlockSpec(memory_space=pltpu.SMEM),
                  full(lng.shape), full(lnb.shape), full(ws.shape), full(bs.shape), full(goa.shape), full(gog.shape)],
        out_specs=pl.BlockSpec((dm.d, BLOCK), cur),
        out_shape=jax.ShapeDtypeStruct((dm.d, t), MXU),
        scratch_shapes=[pltpu.VMEM((dm.aw, BLOCK), F32)],
        compiler_params=_cparams(("parallel", "arbitrary"), 40),
    )(proj_t, proj_t, gq, gk2, sinks, lng, lnb, ws, bs, goa, gog)


def _mixer_bwd(proj_t, dy_t, prm, dm, nseq, nb):
    gq, gk2, sinks, lng, lnb, ws, bs, goa, gog = prm
    t = proj_t.shape[1]
    clamp = lambda i: jnp.minimum(i, nb - 1)
    cur, prev, full = _mixer_specs(dm, nb, clamp)
    kvw2 = 2 * dm.kvw

    def prev_kv_out(s, i):
        return (0, s * nb + jnp.maximum(i - 1, 0))

    def body(p_ref, pkv_ref, dy_ref, gq_ref, gk_ref, sink_ref, lng_ref, lnb_ref, ws_ref, bs_ref, goa_ref, gog_ref,
             dproj_ref, dkv_ref, dgq_ref, dgk_ref, dsink_ref, dlng_ref, dlnb_ref, dws_ref, dbs_ref, dgoa_ref, dgog_ref,
             att_scr, datt_scr, carry_scr, prevpart_scr, curpart_scr, kprev_scr,
             a_gq, a_gk, a_sink, a_lng, a_lnb, a_goa, a_gog):
        s_id, i = pl.program_id(0), pl.program_id(1)
        lane_accs = ((a_gq, dgq_ref), (a_gk, dgk_ref), (a_sink, dsink_ref), (a_lng, dlng_ref), (a_lnb, dlnb_ref),
                     (a_goa, dgoa_ref), (a_gog, dgog_ref))

        @pl.when((s_id == 0) & (i == 0))
        def _():
            for acc, _ in lane_accs:
                acc[...] = jnp.zeros_like(acc)
            dws_ref[...] = jnp.zeros_like(dws_ref)
            dbs_ref[...] = jnp.zeros_like(dbs_ref)

        gqv, gkv = gq_ref[...], gk_ref[...]

        @pl.when(i < nb)
        def _():
            masks = _band_masks(i == 0)
            upper = masks[0]
            kvs, fwd = [], []
            for kh in range(dm.nkv):
                kv = _kv_band(dm, kh, p_ref, pkv_ref, gkv)
                kvs.append(kv)
                fwd.append(_attn_group_fwd(dm, kh, p_ref, gqv, kv[1], kv[2], sink_ref, masks))
                for g, h in enumerate(_group_heads(dm, kh)):
                    att_scr[h * dm.hd:(h + 1) * dm.hd, :] = fwd[kh][0][:, g * BLOCK:(g + 1) * BLOCK]
            att = att_scr[...]
            dya = dy_ref[:dm.aw, :]
            ra = _col_rms(att)
            ah = att * ra
            a_goa[...] += dya * ah
            dah = dya * goa_ref[...]
            datt_scr[...] = ra * (dah - ah * jnp.mean(dah * ah, axis=0, keepdims=True))
            for kh in range(dm.nkv):
                kn_tb, kn, v_tb, vb = kvs[kh]
                _, probs, probs_b, sink_probs, rq, qh, qnb = fwd[kh]
                heads = _group_heads(dm, kh)
                do_b = jnp.concatenate([datt_scr[h * dm.hd:(h + 1) * dm.hd, :] for h in heads], axis=1).astype(MXU)
                dp = _dot(vb, do_b)
                ds = []
                for g, h in enumerate(heads):
                    p, dp_h = probs[g], _fold(dp[:, g * BLOCK:(g + 1) * BLOCK], upper)
                    delta = jnp.sum(p * dp_h, axis=0, keepdims=True)
                    ds.append(_unfold(p * (dp_h - delta) * (dm.hd ** -0.5), upper).astype(MXU))
                    a_sink[h:h + 1, :] += -(sink_probs[g] * delta)
                dsb = jnp.concatenate(ds, axis=1)
                dqn = _dot(kn_tb, dsb)
                dkn = _dot_nt(qnb, dsb)
                dvb = _dot_nt(do_b, probs_b)
                a_gq[...] += dqn * qh
                dqh = dqn * gqv
                dq = rq * (dqh - qh * jnp.mean(dqh * qh, axis=0, keepdims=True))
                for g, h in enumerate(heads):
                    dproj_ref[h * dm.hd:(h + 1) * dm.hd, :] = dq[:, g * BLOCK:(g + 1) * BLOCK].astype(dproj_ref.dtype)
                krows = slice(kh * dm.hd, (kh + 1) * dm.hd)
                vrows = slice(dm.kvw + kh * dm.hd, dm.kvw + (kh + 1) * dm.hd)
                prevpart_scr[krows, :] = dkn[:, :BLOCK]
                prevpart_scr[vrows, :] = dvb[:, :BLOCK]
                curpart_scr[krows, :] = dkn[:, BLOCK:]
                curpart_scr[vrows, :] = dvb[:, BLOCK:]
            dproj_ref[dm.aw:dm.zoff, :] = jnp.zeros((kvw2, BLOCK), dproj_ref.dtype)
            tril = _tril()
            gt, u, mixed, xh, rstd, vnb, wts, (zu, cu), (zv, cv) = _gate_fwd(dm, p_ref, lng_ref, lnb_ref, ws_ref, bs_ref, tril)
            dyg = dy_ref[dm.aw:, :]
            rg = _col_rms(gt)
            gh = gt * rg
            a_gog[...] += dyg * gh
            dgh = dyg * gog_ref[...]
            dgt = rg * (dgh - gh * jnp.mean(dgh * gh, axis=0, keepdims=True))
            du = dgt * mixed
            dmix = dgt * u
            dmixb = dmix.astype(MXU)
            dbs_ref[...] += jnp.sum(_by_group(dmix, dm.ng), axis=1)
            dvn = []
            for g in range(dm.ng):
                rows = slice(g * GROUP_DIM, (g + 1) * GROUP_DIM)
                dws_ref[g] += jnp.where(tril, _dot_tn(dmixb[rows], vnb[rows]), 0.0)
                dvn.append(_dot(dmixb[rows], wts[g]))
            dvn = jnp.concatenate(dvn, axis=0)
            a_lng[...] += dvn * xh
            a_lnb[...] += dvn
            dxh3, xh3 = _by_group(dvn * lng_ref[...], dm.ng), _by_group(xh, dm.ng)
            dv = (rstd * (dxh3 - jnp.mean(dxh3, axis=1, keepdims=True) - xh3 * jnp.mean(dxh3 * xh3, axis=1, keepdims=True))).reshape(dm.gw, BLOCK)
            dgu = cu + zu * (jnp.exp(-0.5 * zu * zu) * _INV_SQRT_2PI)
            dgv = cv + zv * (jnp.exp(-0.5 * zv * zv) * _INV_SQRT_2PI)
            dproj_ref[dm.zoff:dm.zoff + dm.gw, :] = (du * dgu).astype(dproj_ref.dtype)
            dproj_ref[dm.zoff + dm.gw:, :] = (dv * dgv).astype(dproj_ref.dtype)

        @pl.when(i == nb)
        def _():
            prevpart_scr[...] = jnp.zeros_like(prevpart_scr)

        @pl.when(i >= 1)
        def _():
            tot = carry_scr[...] + prevpart_scr[...]
            for kh in range(dm.nkv):
                krows = slice(kh * dm.hd, (kh + 1) * dm.hd)
                kraw = kprev_scr[krows, :]
                rk = _col_rms(kraw)
                khat = kraw * rk
                dkn = tot[krows, :]
                a_gk[...] += dkn * khat
                dkh = dkn * gkv[:, :BLOCK]
                dk = rk * (dkh - khat * jnp.mean(dkh * khat, axis=0, keepdims=True))
                dkv_ref[krows, :] = dk.astype(dkv_ref.dtype)
            dkv_ref[dm.kvw:, :] = tot[dm.kvw:, :].astype(dkv_ref.dtype)

        @pl.when(i < nb)
        def _():
            carry_scr[...] = curpart_scr[...]
            kprev_scr[...] = p_ref[dm.aw:dm.aw + dm.kvw, :]

        @pl.when((s_id == nseq - 1) & (i == nb))
        def _():
            for acc, out in lane_accs:
                out[...] = jnp.sum(acc[...], axis=1, keepdims=True)

    col = lambda rows: jax.ShapeDtypeStruct((rows, 1), F32)
    lane = lambda rows: pltpu.VMEM((rows, LANES), F32)
    return pl.pallas_call(
        body, name="mixer_bwd", grid=(nseq, nb + 1),
        in_specs=[pl.BlockSpec((dm.inw, BLOCK), cur), pl.BlockSpec((kvw2, BLOCK), prev), pl.BlockSpec((dm.d, BLOCK), cur),
                  full(gq.shape), full(gk2.shape), pl.BlockSpec(memory_space=pltpu.SMEM),
                  full(lng.shape), full(lnb.shape), full(ws.shape), full(bs.shape), full(goa.shape), full(gog.shape)],
        out_specs=[pl.BlockSpec((dm.inw, BLOCK), cur), pl.BlockSpec((kvw2, BLOCK), prev_kv_out),
                   full((dm.hd, 1)), full((dm.hd, 1)), full((dm.nh, 1)), full((dm.gw, 1)), full((dm.gw, 1)), full(ws.shape),
                   full(bs.shape), full((dm.aw, 1)), full((dm.gw, 1))],
        out_shape=[jax.ShapeDtypeStruct((dm.inw, t), MXU), jax.ShapeDtypeStruct((kvw2, t), MXU),
                   col(dm.hd), col(dm.hd), col(dm.nh), col(dm.gw), col(dm.gw), jax.ShapeDtypeStruct(ws.shape, F32),
                   jax.ShapeDtypeStruct(bs.shape, F32), col(dm.aw), col(dm.gw)],
        scratch_shapes=[pltpu.VMEM((dm.aw, BLOCK), F32), pltpu.VMEM((dm.aw, BLOCK), F32),
                        pltpu.VMEM((kvw2, BLOCK), F32), pltpu.VMEM((kvw2, BLOCK), F32), pltpu.VMEM((kvw2, BLOCK), F32),
                        pltpu.VMEM((dm.kvw, BLOCK), F32),
                        pltpu.VMEM((dm.hd, dm.grp * BLOCK), F32), lane(dm.hd), lane(dm.nh), lane(dm.gw), lane(dm.gw), lane(dm.aw), lane(dm.gw)],
        compiler_params=_cparams(("arbitrary", "arbitrary"), 48),
    )(proj_t, proj_t, dy_t, gq, gk2, sinks, lng, lnb, ws, bs, goa, gog)


def _patch_kv(dproj_t, dkv_t, dm):
    t = dproj_t.shape[1]
    tc = _pick(t, (1024, 512, 256, 128))
    kvw2 = 2 * dm.kvw
    kvblk = dm.aw // kvw2

    def body(dproj_hbm, dkv_ref, out_ref):
        del dproj_hbm
        out_ref[...] = dkv_ref[...]

    return pl.pallas_call(
        body, name="patch_kv", grid=(t // tc,),
        in_specs=[ANY, pl.BlockSpec((kvw2, tc), lambda i: (0, i))],
        out_specs=pl.BlockSpec((kvw2, tc), lambda i: (kvblk, i)),
        out_shape=jax.ShapeDtypeStruct(dproj_t.shape, dproj_t.dtype),
        input_output_aliases={0: 0},
        compiler_params=_cparams(("parallel",), 32),
    )(dproj_t, dkv_t)


def _place():
    x, y, c = lax.axis_index("x"), lax.axis_index("y"), lax.axis_index("c")
    return x, y, c


def _handshake(peers):
    barrier = pltpu.get_barrier_semaphore()
    for p in peers:
        pl.semaphore_signal(barrier, inc=1, device_id=p, device_id_type=MESH)
    pl.semaphore_wait(barrier, len(peers))


def _sequencer_mesh():
    return plsc.ScalarSubcoreMesh(axis_name="sequencer", num_cores=1)


GATHER_CHUNKS = 8
BF16_ROWS = 16


def _row_chunks(n, k):
    tiles = n // BF16_ROWS
    sizes = [(tiles // k + (1 if i < tiles % k else 0)) * BF16_ROWS for i in range(k)]
    return [(sum(sizes[:i]), sz) for i, sz in enumerate(sizes) if sz]


def _allgather_weight(name, collective_id, shard):
    n = shard.shape[0]
    assert n % BF16_ROWS == 0
    chunks = _row_chunks(n, GATHER_CHUNKS)
    nc = len(chunks)

    def body(src, out, send_sems, recv_sems, local_sem):
        x, y, c = _place()
        me, sib, xn, yn, diag = (x, y, c), (x, y, 1 - c), (1 - x, y, c), (x, 1 - y, c), (1 - x, 1 - y, c)
        relay_to = (x ^ c, y ^ (1 - c), c)
        relay_of = (x ^ (1 - c), y ^ c, c)
        _handshake([sib, xn, yn])

        def rows(place, ci):
            px, py, pc = place
            off, size = chunks[ci]
            return out.at[pl.ds(pl.multiple_of((4 * px + 2 * py + pc) * n + off, BF16_ROWS), size), :]

        def copy(k, ci, block, to, from_src=False):
            off, size = chunks[ci]
            return pltpu.make_async_remote_copy(
                src_ref=src.at[pl.ds(off, size), :] if from_src else rows(block, ci), dst_ref=rows(block, ci),
                send_sem=send_sems.at[ci, k], recv_sem=recv_sems.at[ci, k], device_id=to, device_id_type=MESH)

        mine = pltpu.make_async_copy(src, out.at[pl.ds(pl.multiple_of((4 * x + 2 * y + c) * n, BF16_ROWS), n), :], local_sem)
        mine.start()
        sent = []
        for ci in range(nc):
            sent += [copy(0, ci, me, sib, from_src=True), copy(1, ci, me, xn, from_src=True), copy(2, ci, me, yn, from_src=True)]
        for cp in sent:
            cp.start()
        for ci in range(nc):
            copy(1, ci, xn, me).wait_recv()
            copy(2, ci, yn, me).wait_recv()
            passed = [copy(3, ci, relay_of, relay_to), copy(4, ci, xn, sib), copy(5, ci, yn, sib)]
            for cp in passed:
                cp.start()
            sent += passed
        for ci in range(nc):
            copy(3, ci, diag, me).wait_recv()
            passed = copy(6, ci, diag, sib)
            passed.start()
            sent.append(passed)
        for ci in range(nc):
            copy(0, ci, sib, me).wait_recv()
            for k, block in ((4, (1 - x, y, 1 - c)), (5, (x, 1 - y, 1 - c)), (6, (1 - x, 1 - y, 1 - c))):
                copy(k, ci, block, me).wait_recv()
        for cp in sent:
            cp.wait_send()
        mine.wait()

    return pl.kernel(
        body, name=name,
        out_type=jax.ShapeDtypeStruct((N_DEV * n, shard.shape[1]), shard.dtype),
        mesh=_sequencer_mesh(),
        scratch_types=[pltpu.SemaphoreType.DMA((nc, 7)), pltpu.SemaphoreType.DMA((nc, 7)), pltpu.SemaphoreType.DMA],
        compiler_params=pltpu.CompilerParams(collective_id=collective_id),
    )(shard)


_FLIPS = [(0, 0, 1), (1, 0, 0), (0, 1, 0), (1, 1, 0), (1, 0, 1), (0, 1, 1), (1, 1, 1)]


def _scatter_grad(name, collective_id, grad):
    n = grad.shape[0] // N_DEV

    def body(src, out, send_sems, recv_sems, local_sem):
        x, y, c = _place()
        me_idx = 4 * x + 2 * y + c
        peers = [(x ^ fx, y ^ fy, c ^ fc) for (fx, fy, fc) in _FLIPS]
        _handshake(peers)

        def block(idx):
            return src.at[pl.ds(pl.multiple_of(idx * n, 16), n), :]

        copies = [pltpu.make_async_remote_copy(
            src_ref=block(4 * px + 2 * py + pc), dst_ref=out.at[me_idx], send_sem=send_sems.at[k], recv_sem=recv_sems.at[k],
            device_id=(px, py, pc), device_id_type=MESH) for k, (px, py, pc) in enumerate(peers)]
        mine = pltpu.make_async_copy(block(me_idx), out.at[me_idx], local_sem)
        mine.start()
        for cp in copies:
            cp.start()
        for cp in copies:
            cp.wait_recv()
        for cp in copies:
            cp.wait_send()
        mine.wait()

    return pl.kernel(
        body, name=name,
        out_type=jax.ShapeDtypeStruct((N_DEV, n, grad.shape[1]), grad.dtype),
        mesh=_sequencer_mesh(),
        scratch_types=[pltpu.SemaphoreType.DMA((7,)), pltpu.SemaphoreType.DMA((7,)), pltpu.SemaphoreType.DMA],
        compiler_params=pltpu.CompilerParams(collective_id=collective_id),
    )(grad)


def _allgather_rows(name, collective_id, part):
    def body(src, out, send_sems, recv_sems, local_sem):
        x, y, c = _place()
        me_idx = 4 * x + 2 * y + c
        peers = [(x ^ fx, y ^ fy, c ^ fc) for (fx, fy, fc) in _FLIPS]
        _handshake(peers)
        copies = [pltpu.make_async_remote_copy(
            src_ref=src, dst_ref=out.at[me_idx], send_sem=send_sems.at[k], recv_sem=recv_sems.at[k],
            device_id=peer, device_id_type=MESH) for k, peer in enumerate(peers)]
        mine = pltpu.make_async_copy(src, out.at[me_idx], local_sem)
        mine.start()
        for cp in copies:
            cp.start()
        for cp in copies:
            cp.wait_recv()
        for cp in copies:
            cp.wait_send()
        mine.wait()

    return pl.kernel(
        body, name=name,
        out_type=jax.ShapeDtypeStruct((N_DEV,) + part.shape, part.dtype),
        mesh=_sequencer_mesh(),
        scratch_types=[pltpu.SemaphoreType.DMA((7,)), pltpu.SemaphoreType.DMA((7,)), pltpu.SemaphoreType.DMA],
        compiler_params=pltpu.CompilerParams(collective_id=collective_id),
    )(part)


def _adamw_math(w, g, m, v):
    m = ADAM_B1 * m + (1.0 - ADAM_B1) * g
    v = ADAM_B2 * v + (1.0 - ADAM_B2) * (g * g)
    m_hat = m / (1.0 - ADAM_B1 ** ADAM_STEP)
    v_hat = v / (1.0 - ADAM_B2 ** ADAM_STEP)
    delta = -ADAM_LR * (m_hat / (jnp.sqrt(v_hat) + ADAM_EPS) + ADAM_WD * w)
    return delta, m, v


def _sum_adamw(name, slots, w, m, v, after):
    _, n, kk = slots.shape
    tr = _pick(n, (208, 176, 128, 96, 64, 32, 16))

    def body(s_ref, w_ref, m_ref, v_ref, after_ref, g_ref, d_ref, nm_ref, nv_ref):
        del after_ref
        g = s_ref[0].astype(F32)
        for p in range(1, N_DEV):
            g = g + s_ref[p].astype(F32)
        g_ref[...] = g
        d_ref[...], nm_ref[...], nv_ref[...] = _adamw_math(w_ref[...], g, m_ref[...], v_ref[...])

    row = pl.BlockSpec((tr, kk), lambda i: (i, 0))
    return pl.pallas_call(
        body, name=name, grid=(n // tr,),
        in_specs=[pl.BlockSpec((N_DEV, tr, kk), lambda i: (0, i, 0)), row, row, row, ANY],
        out_specs=[row] * 4,
        out_shape=[jax.ShapeDtypeStruct((n, kk), F32)] * 4,
        compiler_params=_cparams(("parallel",), 48),
    )(slots, w, m, v, after)


def _allreduce_small_adamw(early_slots, late, w, m, v, after):
    ra, rb = early_slots.shape[1], late.shape[0]

    def body(early_ref, late_ref, w_ref, m_ref, v_ref, after_ref, g_ref, d_ref, nm_ref, nv_ref, slots, send_sems, recv_sems):
        del after_ref
        x, y, c = _place()
        me_idx = 4 * x + 2 * y + c
        copies = []
        for k, (fx, fy, fc) in enumerate(_FLIPS):
            px, py, pc = x ^ fx, y ^ fy, c ^ fc
            copies.append(pltpu.make_async_remote_copy(
                src_ref=late_ref, dst_ref=slots.at[me_idx], send_sem=send_sems.at[k], recv_sem=recv_sems.at[k],
                device_id=(px, py, pc), device_id_type=MESH))
        for cp in copies:
            cp.start()
        slots[me_idx] = late_ref[...]
        g = early_ref[0]
        for p in range(1, N_DEV):
            g = g + early_ref[p]
        early = pl.ds(0, ra)
        g_ref[early, :] = g
        d_ref[early, :], nm_ref[early, :], nv_ref[early, :] = _adamw_math(w_ref[early, :], g, m_ref[early, :], v_ref[early, :])
        for cp in copies:
            cp.wait_recv()
        for cp in copies:
            cp.wait_send()
        g = slots[0]
        for p in range(1, N_DEV):
            g = g + slots[p]
        tail = pl.ds(ra, rb)
        g_ref[tail, :] = g
        d_ref[tail, :], nm_ref[tail, :], nv_ref[tail, :] = _adamw_math(w_ref[tail, :], g, m_ref[tail, :], v_ref[tail, :])

    vm = pl.BlockSpec(memory_space=pltpu.VMEM)
    return pl.pallas_call(
        body, name="allreduce_small_adamw",
        in_specs=[vm] * 5 + [ANY], out_specs=[vm] * 4,
        out_shape=[jax.ShapeDtypeStruct((ra + rb, LANES), F32)] * 4,
        scratch_shapes=[pltpu.VMEM((N_DEV, rb, LANES), F32), pltpu.SemaphoreType.DMA((7,)), pltpu.SemaphoreType.DMA((7,))],
        compiler_params=pltpu.CompilerParams(vmem_limit_bytes=48 * MIB),
    )(early_slots, late, w, m, v, after)


def _pack(arrs):
    parts, meta, off = [], [], 0
    for a in arrs:
        flat = a.reshape(-1).astype(F32)
        rows = -(-flat.shape[0] // LANES)
        rows8 = -(-rows // 8) * 8
        flat = jnp.pad(flat, (0, rows8 * LANES - flat.shape[0]))
        parts.append(flat.reshape(rows8, LANES))
        meta.append((off, a.shape, a.size))
        off += rows8
    return jnp.concatenate(parts, axis=0), meta


def _unpack(packed, meta):
    outs = []
    for off, shape, size in meta:
        rows = -(-size // LANES)
        outs.append(packed[off:off + rows].reshape(-1)[:size].reshape(shape))
    return outs


def _silu_parts(a):
    sg = 0.5 + 0.5 * jnp.tanh(0.5 * a)
    return a * sg, sg * (1.0 + a * (1.0 - sg))


def kernel(x, norm1_g, w_in, q_norm_g, k_norm_g, attn_sinks, gate_ln_g, gate_ln_b, w_spatial, b_spatial, out_norm_attn_g, out_norm_gate_g, w_out, norm2_g, w_ffn_gate, w_ffn_up, w_ffn_down, loss_target, m_norm1_g, m_w_in, m_q_norm_g, m_k_norm_g, m_attn_sinks, m_gate_ln_g, m_gate_ln_b, m_w_spatial, m_b_spatial, m_out_norm_attn_g, m_out_norm_gate_g, m_w_out, m_norm2_g, m_w_ffn_gate, m_w_ffn_up, m_w_ffn_down, v_norm1_g, v_w_in, v_q_norm_g, v_k_norm_g, v_attn_sinks, v_gate_ln_g, v_gate_ln_b, v_w_spatial, v_b_spatial, v_out_norm_attn_g, v_out_norm_gate_g, v_w_out, v_norm2_g, v_w_ffn_gate, v_w_ffn_up, v_w_ffn_down):
    nseq, seq, d = x.shape
    t = nseq * seq
    nb = seq // BLOCK
    inw = w_in.shape[2] * N_DEV
    dm = _Dims(d, inw, q_norm_g.shape[-1])
    xf = x.reshape(t, d)
    tgt = loss_target.reshape(t, d)

    rows = lambda wv, transposed: jnp.swapaxes(wv, 1, 2)[0] if transposed else wv[0]
    big = {"w_in": (w_in, m_w_in, v_w_in, True), "w_out": (w_out, m_w_out, v_w_out, False),
           "w_ffn_gate": (w_ffn_gate, m_w_ffn_gate, v_w_ffn_gate, True), "w_ffn_up": (w_ffn_up, m_w_ffn_up, v_w_ffn_up, True),
           "w_ffn_down": (w_ffn_down, m_w_ffn_down, v_w_ffn_down, False)}
    big_rows = {nm: tuple(rows(arr, tr) for arr in (wv, mv, vv)) for nm, (wv, mv, vv, tr) in big.items()}
    shard = lambda nm: big_rows[nm][0].astype(WIRE)
    win_t = _allgather_weight("gather_w_in", 1, shard("w_in"))
    wout = _allgather_weight("gather_w_out", 2, shard("w_out"))
    wg_t = _allgather_weight("gather_w_ffn_gate", 3, shard("w_ffn_gate"))
    wu_t = _allgather_weight("gather_w_ffn_up", 9, shard("w_ffn_up"))
    wd = _allgather_weight("gather_w_ffn_down", 10, shard("w_ffn_down"))

    lanes = lambda v, n=BLOCK: jnp.broadcast_to(v.reshape(-1, 1), (v.size, n))
    prm = (lanes(q_norm_g, dm.grp * BLOCK), lanes(k_norm_g, 2 * BLOCK), attn_sinks[0], lanes(gate_ln_g), lanes(gate_ln_b), w_spatial[0], b_spatial[0],
           lanes(out_norm_attn_g), lanes(out_norm_gate_g))

    h1 = _rms_fwd("rms1_fwd", xf, norm1_g)
    (proj_t,) = _matmul("mm_in", win_t, h1, "nt", [F32])
    y_t = _mixer_fwd(proj_t, prm, dm, nseq, nb)

    def residual_norm(acc, xr, g2):
        x2v = xr + acc
        return x2v, x2v * lax.rsqrt(jnp.mean(x2v * x2v, axis=-1, keepdims=True) + EPS) * g2

    x2, h2 = _matmul("mm_out", y_t, wout, "tn", [F32, MXU], epilogue=residual_norm, extras=[xf], rowvecs=[norm2_g], full_rows=True)
    silu_a, dsilu_a = _matmul("mm_gate", h2, wg_t, "nt", [MXU, MXU], epilogue=_silu_parts)
    s, dsilu_a_b = _matmul("mm_up", h2, wu_t, "nt", [MXU, MXU], epilogue=lambda ub, sv, dv: (sv.astype(F32) * ub, dv.astype(F32) * ub),
                           extras=[silu_a, dsilu_a])

    def loss_epilogue(acc, x2v, tv):
        diff = (x2v + acc) - tv
        dx3 = diff * (1.0 / d)
        return dx3, dx3, jnp.sum(diff * diff)

    dx3, dx3b, lossp = _matmul("mm_down", s, wd, "nn", [F32, MXU], epilogue=loss_epilogue, extras=[x2, tgt], partial=True)
    loss_part = (0.5 / d) * jnp.sum(lossp[::8, ::LANES])

    def dswiglu(ds, dsilu_b, silu):
        return ds * dsilu_b.astype(F32), ds * silu.astype(F32)

    (g_wd,) = _matmul("mm_gw_down", s, dx3b, "tn", [WIRE])
    sl_wd = _scatter_grad("scatter_w_ffn_down", 4, g_wd)
    da, db = _matmul("mm_d_down", dx3b, wd, "nt", [MXU, MXU], epilogue=dswiglu, extras=[dsilu_a_b, silu_a], after=[g_wd])
    (g_wg,) = _matmul("mm_gw_gate", da, h2, "tn", [WIRE])
    sl_wg = _scatter_grad("scatter_w_ffn_gate", 5, g_wg)
    (g_wu,) = _matmul("mm_gw_up", db, h2, "tn", [WIRE], after=[g_wg])
    sl_wu = _scatter_grad("scatter_w_ffn_up", 6, g_wu)
    (dh2a,) = _matmul("mm_dh2_gate", da, wg_t, "nn", [F32], after=[g_wu])
    (dh2,) = _matmul("mm_dh2_up", db, wu_t, "nn", [F32], epilogue=lambda acc, pv: (pv + acc,), extras=[dh2a])

    dy_t, dx2, dx2b, dg2 = _norm_bwd_matmul("mm_d_out", wout, dh2, x2, norm2_g, dx3, after=dh2a)
    (g_wout,) = _matmul("mm_gw_out", y_t, dx2b, "nn", [WIRE], after=[dy_t])
    sl_wout = _scatter_grad("scatter_w_out", 7, g_wout)
    (dproj0, dkv, dgq, dgk, dsink, dlng, dlnb, dws, dbs, dgoa, dgog) = _mixer_bwd(proj_t, dy_t, prm, dm, nseq, nb)
    early_g = [dgq, dgk, dsink, dlng, dlnb, dws, dbs, dgoa, dgog, dg2, loss_part.reshape(1)]
    early_slots = _allgather_rows("gather_small_grads", 11, _pack(early_g)[0])
    dproj_t = _patch_kv(dproj0, dkv, dm)
    (g_win,) = _matmul("mm_gw_in", dproj_t, h1, "nn", [WIRE])
    sl_win = _scatter_grad("scatter_w_in", 8, g_win)

    def norm1_backward(dh1, xv, dx2v, g1):
        r = lax.rsqrt(jnp.mean(xv * xv, axis=-1, keepdims=True) + EPS)
        xh = xv * r
        dxh = dh1 * g1
        return dx2v + r * (dxh - xh * jnp.mean(dxh * xh, axis=-1, keepdims=True)), jnp.sum(dh1 * xh, axis=0, keepdims=True)

    dx, dg1 = _matmul("mm_d_in", dproj_t, win_t, "tn", [F32], epilogue=norm1_backward, extras=[xf, dx2], rowvecs=[norm1_g],
                      after=[g_win], col_sum=True, full_rows=True)

    big_out = {}
    last = dx

    def big_update(nm, sl, after):
        res = _sum_adamw("adamw_" + nm, sl, *big_rows[nm], after=after)
        big_out[nm] = tuple(jnp.swapaxes(r[None], 1, 2) if big[nm][3] else r[None] for r in res)
        return res[1]

    for nm, sl in (("w_ffn_down", sl_wd), ("w_ffn_gate", sl_wg), ("w_ffn_up", sl_wu), ("w_out", sl_wout)):
        last = big_update(nm, sl, last)

    zero = jnp.zeros((1,), F32)
    small_names = ["q_norm_g", "k_norm_g", "attn_sinks", "gate_ln_g", "gate_ln_b", "w_spatial", "b_spatial",
                   "out_norm_attn_g", "out_norm_gate_g", "norm2_g", "loss", "norm1_g"]
    small_w = [q_norm_g, k_norm_g, attn_sinks, gate_ln_g, gate_ln_b, w_spatial, b_spatial, out_norm_attn_g, out_norm_gate_g, norm2_g, zero, norm1_g]
    small_m = [m_q_norm_g, m_k_norm_g, m_attn_sinks, m_gate_ln_g, m_gate_ln_b, m_w_spatial, m_b_spatial, m_out_norm_attn_g, m_out_norm_gate_g, m_norm2_g, zero, m_norm1_g]
    small_v = [v_q_norm_g, v_k_norm_g, v_attn_sinks, v_gate_ln_g, v_gate_ln_b, v_w_spatial, v_b_spatial, v_out_norm_attn_g, v_out_norm_gate_g, v_norm2_g, zero, v_norm1_g]
    pw, meta = _pack(small_w)
    sg, sd, sm, sv = _allreduce_small_adamw(early_slots, _pack([dg1])[0], pw, _pack(small_m)[0], _pack(small_v)[0], after=last)
    big_update("w_in", sl_win, sd)
    ug, ud, um, uv = _unpack(sg, meta), _unpack(sd, meta), _unpack(sm, meta), _unpack(sv, meta)
    small_out = {nm: (ug[k], ud[k], um[k], uv[k]) for k, nm in enumerate(small_names)}
    loss = small_out["loss"][0].reshape(())

    order = ["norm1_g", "w_in", "q_norm_g", "k_norm_g", "attn_sinks", "gate_ln_g", "gate_ln_b", "w_spatial", "b_spatial",
             "out_norm_attn_g", "out_norm_gate_g", "w_out", "norm2_g", "w_ffn_gate", "w_ffn_up", "w_ffn_down"]
    allo = {**big_out, **small_out}
    outs = [loss, dx.reshape(nseq, seq, d)]
    for k in range(4):
        outs += [allo[nm][k] for nm in order]
    return tuple(outs)
```

```python
import math

import jax
import jax.numpy as jnp
from jax import lax
from jax.experimental import pallas as pl
from jax.experimental.pallas import tpu as pltpu
from jax.experimental.pallas import tpu_sc as plsc

F32 = jnp.float32
MXU = jnp.bfloat16
WIRE = jnp.bfloat16
EPS = 1e-6
BLOCK = 128
GROUP_DIM = 128
N_KV_HEADS = 2
NEG = -1e30
N_DEV = 8
LANES = 128
MIB = 1024 * 1024

ADAM_LR = 0.001
ADAM_B1 = 0.9
ADAM_B2 = 0.999
ADAM_EPS = 1e-08
ADAM_WD = 0.01
ADAM_STEP = 10

MESH = pl.DeviceIdType.MESH
ANY = pl.BlockSpec(memory_space=pl.ANY)


def _pick(n, cands):
    for c in cands:
        if n % c == 0:
            return c
    return n


def _cparams(sem, vmem_mb):
    return pltpu.CompilerParams(dimension_semantics=sem, vmem_limit_bytes=vmem_mb * MIB)


VMEM_TILE_BUDGET = 50 * MIB
HBM_BYTES_PER_US = 3.0e6
STEP_US = 0.4
MXU_COLS = 256
MIN_TILE_N = 2 * MXU_COLS


def _tile_candidates(n):
    return [c for c in range(min(n, 2048), 0, -LANES) if n % c == 0 and c % LANES == 0] or [n]


def _matmul_tiles(m, n, kk, esz, extra_sizes, out_sizes, full_rows):
    best = None
    wide = [n] if full_rows else [c for c in _tile_candidates(n) if c >= MIN_TILE_N and c % MXU_COLS == 0] or _tile_candidates(n)
    for tm in _tile_candidates(m):
        for tn in wide:
            b_buffers = 1 if full_rows else 2
            temps = 2 if full_rows else 0
            vmem = (2 * tm + b_buffers * tn) * kk * esz + tm * tn * (4 * (1 + temps) + 2 * sum(extra_sizes) + 2 * sum(out_sizes))
            if vmem > VMEM_TILE_BUDGET:
                continue
            cost = (m // tm) * n * kk * esz / HBM_BYTES_PER_US + (m // tm) * (n // tn) * STEP_US
            if best is None or cost < best[0]:
                best = (cost, tm, tn, vmem)
    assert best is not None, (m, n, kk)
    return best[1:]


def _matmul(name, a, b, mode, out_dtypes, epilogue=None, extras=(), rowvecs=(), after=(), partial=False, col_sum=False, full_rows=False):
    if mode == "nn":
        (m, kk), n = a.shape, b.shape[1]
        dn = (((1,), (0,)), ((), ()))
    elif mode == "nt":
        (m, kk), n = a.shape, b.shape[0]
        dn = (((1,), (1,)), ((), ()))
    else:
        (kk, m), n = a.shape, b.shape[1]
        dn = (((0,), (0,)), ((), ()))
    tm, tn, vmem = _matmul_tiles(m, n, kk, a.dtype.itemsize, [e.dtype.itemsize for e in extras],
                                 [jnp.dtype(dt).itemsize for dt in out_dtypes], full_rows)
    a_spec = pl.BlockSpec((kk, tm), lambda i, j: (0, i)) if mode == "tn" else pl.BlockSpec((tm, kk), lambda i, j: (i, 0))
    resident = dict(pipeline_mode=pl.Buffered(1)) if full_rows else {}
    b_spec = pl.BlockSpec((tn, kk), lambda i, j: (j, 0), **resident) if mode == "nt" else pl.BlockSpec((kk, tn), lambda i, j: (0, j), **resident)
    tile = pl.BlockSpec((tm, tn), lambda i, j: (i, j))
    row = pl.BlockSpec((1, tn), lambda i, j: (0, j))
    ne, nr, na, no = len(extras), len(rowvecs), len(after), len(out_dtypes)

    def body(a_ref, b_ref, *rest):
        in_refs, out_refs = rest[:ne + nr], rest[ne + nr + na:]
        acc = lax.dot_general(a_ref[...], b_ref[...], dn, preferred_element_type=F32)
        vals = (acc,) if epilogue is None else epilogue(acc, *[r[...] for r in in_refs])
        for o_ref, t in zip(out_refs[:no], vals[:no]):
            o_ref[...] = t.astype(o_ref.dtype)
        if partial:
            out_refs[no][...] = jnp.full((8, LANES), vals[no], F32)
        if col_sum:
            sum_ref = out_refs[-1]

            @pl.when(pl.program_id(0) == 0)
            def _():
                sum_ref[...] = jnp.zeros_like(sum_ref)

            sum_ref[...] += vals[-1]

    out_specs = [tile] * no
    out_shape = [jax.ShapeDtypeStruct((m, n), dt) for dt in out_dtypes]
    if partial:
        out_specs.append(pl.BlockSpec((8, LANES), lambda i, j: (i, j)))
        out_shape.append(jax.ShapeDtypeStruct((m // tm * 8, n // tn * LANES), F32))
    if col_sum:
        out_specs.append(row)
        out_shape.append(jax.ShapeDtypeStruct((1, n), F32))
    return pl.pallas_call(
        body, name=name, grid=(m // tm, n // tn),
        in_specs=[a_spec, b_spec] + [tile] * ne + [row] * nr + [ANY] * na,
        out_specs=out_specs, out_shape=out_shape,
        compiler_params=_cparams(("arbitrary" if col_sum else "parallel", "arbitrary"), min(vmem // MIB + 8, 58)),
    )(a, b, *extras, *rowvecs, *after)


def _rms_fwd(name, x, g):
    t, d = x.shape
    tm = _pick(t, (512, 256, 128))

    def body(x_ref, g_ref, h_ref):
        xv = x_ref[...]
        r = lax.rsqrt(jnp.mean(xv * xv, axis=-1, keepdims=True) + EPS)
        h_ref[...] = (xv * r * g_ref[...]).astype(h_ref.dtype)

    return pl.pallas_call(
        body, name=name, grid=(t // tm,),
        in_specs=[pl.BlockSpec((tm, d), lambda i: (i, 0)), pl.BlockSpec((1, d), lambda i: (0, 0))],
        out_specs=pl.BlockSpec((tm, d), lambda i: (i, 0)),
        out_shape=jax.ShapeDtypeStruct((t, d), MXU),
        compiler_params=_cparams(("parallel",), 32),
    )(x, g)


def _norm_bwd_matmul(name, w, dh, x, g, res, after):
    t, d = x.shape
    m = w.shape[0]
    tn = _pick(t, (256, 128))

    def body(w_ref, dh_ref, x_ref, g_ref, res_ref, after_ref, out_ref, dx_ref, dxb_ref, dg_ref):
        del after_ref

        @pl.when(pl.program_id(0) == 0)
        def _():
            dg_ref[...] = jnp.zeros_like(dg_ref)

        xv, dhv = x_ref[...], dh_ref[...]
        r = lax.rsqrt(jnp.mean(xv * xv, axis=-1, keepdims=True) + EPS)
        xh = xv * r
        dg_ref[...] += jnp.sum(dhv * xh, axis=0, keepdims=True)
        dxh = dhv * g_ref[...]
        dx = res_ref[...] + r * (dxh - xh * jnp.mean(dxh * xh, axis=-1, keepdims=True))
        dx_ref[...] = dx
        dxb = dx.astype(MXU)
        dxb_ref[...] = dxb
        out_ref[...] = lax.dot_general(w_ref[...], dxb, (((1,), (1,)), ((), ())), preferred_element_type=F32)

    row = pl.BlockSpec((tn, d), lambda j: (j, 0))
    vec = pl.BlockSpec((1, d), lambda j: (0, 0))
    return pl.pallas_call(
        body, name=name, grid=(t // tn,),
        in_specs=[pl.BlockSpec((m, d), lambda j: (0, 0), pipeline_mode=pl.Buffered(1)), row, row, vec, row, ANY],
        out_specs=[pl.BlockSpec((m, tn), lambda j: (0, j)), row, row, vec],
        out_shape=[jax.ShapeDtypeStruct((m, t), F32), jax.ShapeDtypeStruct((t, d), F32), jax.ShapeDtypeStruct((t, d), MXU),
                   jax.ShapeDtypeStruct((1, d), F32)],
        compiler_params=_cparams(("arbitrary",), 52),
    )(w, dh, x, g, res, after)


_INV_SQRT2 = 0.7071067811865476
_INV_SQRT_2PI = 0.3989422804014327


def _dot_nt(a, b):
    return lax.dot_general(a, b, (((1,), (1,)), ((), ())), preferred_element_type=F32)


def _dot_tn(a, b):
    return lax.dot_general(a, b, (((0,), (0,)), ((), ())), preferred_element_type=F32)


def _dot(a, b):
    return jnp.dot(a, b, preferred_element_type=F32)


def _col_rms(v):
    return lax.rsqrt(jnp.mean(v * v, axis=0, keepdims=True) + EPS)


class _Dims:
    def __init__(self, d_model, in_width, head_dim):
        self.d = d_model
        self.aw = d_model // 2
        self.gw = d_model - self.aw
        self.kvw = (in_width - self.aw - 2 * self.gw) // 2
        self.hd = head_dim
        self.nh = self.aw // head_dim
        self.nkv = self.kvw // head_dim
        self.grp = self.nh // self.nkv
        self.ng = self.gw // GROUP_DIM
        self.inw = in_width
        self.zoff = self.aw + 2 * self.kvw
        assert self.nkv == N_KV_HEADS and self.zoff + 2 * self.gw == in_width and self.aw % (2 * self.kvw) == 0


def _band_masks(first):
    r = lax.broadcasted_iota(jnp.int32, (BLOCK, BLOCK), 0)
    t = lax.broadcasted_iota(jnp.int32, (BLOCK, BLOCK), 1)
    upper = r > t
    dist = t - r + jnp.where(upper, BLOCK, 0)
    return upper, jnp.logical_not(upper & first), dist.astype(F32)


def _fold(full, upper):
    return jnp.where(upper, full[:BLOCK], full[BLOCK:])


def _unfold(folded, upper):
    zero = jnp.zeros_like(folded)
    return jnp.concatenate([jnp.where(upper, folded, zero), jnp.where(upper, zero, folded)], axis=0)


def _kv_band(dm, kh, p_ref, pkv_ref, gk2):
    ko = dm.aw + kh * dm.hd
    vo = dm.aw + dm.kvw + kh * dm.hd
    k_t = jnp.concatenate([pkv_ref[kh * dm.hd:(kh + 1) * dm.hd, :], p_ref[ko:ko + dm.hd, :]], axis=1)
    v_t = jnp.concatenate([pkv_ref[dm.kvw + kh * dm.hd:dm.kvw + (kh + 1) * dm.hd, :], p_ref[vo:vo + dm.hd, :]], axis=1)
    kn_t = k_t * _col_rms(k_t) * gk2
    return kn_t.astype(MXU), kn_t.T.astype(MXU), v_t.astype(MXU), v_t.T.astype(MXU)


def _group_heads(dm, kh):
    return range(kh * dm.grp, (kh + 1) * dm.grp)


def _group_queries(dm, kh, p_ref, gq):
    q = jnp.concatenate([p_ref[h * dm.hd:(h + 1) * dm.hd, :] for h in _group_heads(dm, kh)], axis=1)
    rq = _col_rms(q)
    qh = q * rq
    return rq, qh, (qh * gq).astype(MXU)


def _attn_group_fwd(dm, kh, p_ref, gq, kn, v_tb, sink_ref, masks):
    heads = _group_heads(dm, kh)
    rq, qh, qnb = _group_queries(dm, kh, p_ref, gq)
    upper, valid, dist = masks
    s = _dot(kn, qnb)
    probs, probs_b, sink_probs = [], [], []
    for g, h in enumerate(heads):
        slope, sink = math.pow(2.0, -8.0 * (h + 1) / dm.nh), sink_ref[h]
        logits = jnp.where(valid, _fold(s[:, g * BLOCK:(g + 1) * BLOCK], upper) * (dm.hd ** -0.5) - slope * dist, NEG)
        m = jnp.maximum(jnp.max(logits, axis=0, keepdims=True), sink)
        e = jnp.exp(logits - m)
        es = jnp.exp(sink - m)
        inv = 1.0 / (jnp.sum(e, axis=0, keepdims=True) + es)
        probs.append(e * inv)
        probs_b.append(_unfold(probs[g], upper).astype(MXU))
        sink_probs.append(es * inv)
    probs_b = jnp.concatenate(probs_b, axis=1)
    o = _dot(v_tb, probs_b)
    return o, probs, probs_b, sink_probs, rq, qh, qnb


def _gelu_cdf(z):
    return 0.5 * (1.0 + lax.erf(z * _INV_SQRT2))


def _by_group(v, ng):
    return v.reshape(ng, GROUP_DIM, v.shape[1])


def _gate_fwd(dm, p_ref, lng_ref, lnb_ref, ws_ref, bs_ref, tril):
    zu, zv = p_ref[dm.zoff:dm.zoff + dm.gw, :], p_ref[dm.zoff + dm.gw:dm.zoff + 2 * dm.gw, :]
    cu, cv = _gelu_cdf(zu), _gelu_cdf(zv)
    u, v = zu * cu, zv * cv
    v3 = _by_group(v, dm.ng)
    xc = v3 - jnp.mean(v3, axis=1, keepdims=True)
    rstd = lax.rsqrt(jnp.mean(xc * xc, axis=1, keepdims=True) + EPS)
    xh = (xc * rstd).reshape(dm.gw, BLOCK)
    vnb = (xh * lng_ref[...] + lnb_ref[...]).astype(MXU)
    wts = [jnp.where(tril, ws_ref[g], 0.0).astype(MXU) for g in range(dm.ng)]
    mixed = jnp.concatenate([_dot_nt(vnb[g * GROUP_DIM:(g + 1) * GROUP_DIM], wts[g]) + bs_ref[g:g + 1, :]
                             for g in range(dm.ng)], axis=0)
    return u * mixed, u, mixed, xh, rstd, vnb, wts, (zu, cu), (zv, cv)


def _mixer_specs(dm, nb, clamp):
    kvblk = dm.aw // (2 * dm.kvw)

    def cur(s, i):
        return (0, s * nb + clamp(i))

    def prev(s, i):
        return (kvblk, s * nb + jnp.maximum(clamp(i) - 1, 0))

    full = lambda shape: pl.BlockSpec(shape, lambda s, i: tuple(0 for _ in shape))
    return cur, prev, full


def _tril():
    return lax.broadcasted_iota(jnp.int32, (BLOCK, BLOCK), 0) >= lax.broadcasted_iota(jnp.int32, (BLOCK, BLOCK), 1)


def _mixer_fwd(proj_t, prm, dm, nseq, nb):
    gq, gk2, sinks, lng, lnb, ws, bs, goa, gog = prm
    t = proj_t.shape[1]
    cur, prev, full = _mixer_specs(dm, nb, lambda i: i)

    def body(p_ref, pkv_ref, gq_ref, gk_ref, sink_ref, lng_ref, lnb_ref, ws_ref, bs_ref, goa_ref, gog_ref,
             y_ref, probs_ref, sink_probs_ref, att_ref):
        i = pl.program_id(1)
        masks = _band_masks(i == 0)
        gqv, gkv = gq_ref[...], gk_ref[...]
        for kh in range(dm.nkv):
            _, kn, v_tb, _ = _kv_band(dm, kh, p_ref, pkv_ref, gkv)
            o, probs, _, sink_probs, _, _, _ = _attn_group_fwd(dm, kh, p_ref, gqv, kn, v_tb, sink_ref, masks)
            for g, h in enumerate(_group_heads(dm, kh)):
                att_ref[h * dm.hd:(h + 1) * dm.hd, :] = o[:, g * BLOCK:(g + 1) * BLOCK]
                probs_ref[h * BLOCK:(h + 1) * BLOCK, :] = probs[g]
                sink_probs_ref[h:h + 1, :] = sink_probs[g]
        att = att_ref[...]
        y_ref[:dm.aw, :] = (att * _col_rms(att) * goa_ref[...]).astype(y_ref.dtype)
        gt = _gate_fwd(dm, p_ref, lng_ref, lnb_ref, ws_ref, bs_ref, _tril())[0]
        y_ref[dm.aw:, :] = (gt * _col_rms(gt) * gog_ref[...]).astype(y_ref.dtype)

    return pl.pallas_call(
        body, name="mixer_fwd", grid=(nseq, nb),
        in_specs=[pl.BlockSpec((dm.inw, BLOCK), cur), pl.BlockSpec((2 * dm.kvw, BLOCK), prev),
                  full(gq.shape), full(gk2.shape), pl.BlockSpec(memory_space=pltpu.SMEM),
                  full(lng.shape), full(lnb.shape), full(ws.shape), full(bs.shape), full(goa.shape), full(gog.shape)],
        out_specs=[pl.BlockSpec((dm.d, BLOCK), cur), pl.BlockSpec((dm.nh * BLOCK, BLOCK), cur), pl.BlockSpec((dm.nh, BLOCK), cur),
                   pl.BlockSpec((dm.aw, BLOCK), cur)],
        out_shape=[jax.ShapeDtypeStruct((dm.d, t), MXU), jax.ShapeDtypeStruct((dm.nh * BLOCK, t), F32),
                   jax.ShapeDtypeStruct((dm.nh, t), F32), jax.ShapeDtypeStruct((dm.aw, t), F32)],
        compiler_params=_cparams(("parallel", "arbitrary"), 40),
    )(proj_t, proj_t, gq, gk2, sinks, lng, lnb, ws, bs, goa, gog)


def _mixer_bwd(proj_t, dy_t, saved, prm, dm, nseq, nb):
    gq, gk2, sinks, lng, lnb, ws, bs, goa, gog = prm
    t = proj_t.shape[1]
    clamp = lambda i: jnp.minimum(i, nb - 1)
    cur, prev, full = _mixer_specs(dm, nb, clamp)
    kvw2 = 2 * dm.kvw

    def prev_kv_out(s, i):
        return (0, s * nb + jnp.maximum(i - 1, 0))

    def body(p_ref, pkv_ref, dy_ref, probs_ref, sink_probs_ref, att_ref, gq_ref, gk_ref, sink_ref, lng_ref, lnb_ref, ws_ref, bs_ref,
             goa_ref, gog_ref, dproj_ref, dkv_ref, dgq_ref, dgk_ref, dsink_ref, dlng_ref, dlnb_ref, dws_ref, dbs_ref, dgoa_ref, dgog_ref,
             datt_scr, carry_scr, prevpart_scr, curpart_scr, kprev_scr,
             a_gq, a_gk, a_sink, a_lng, a_lnb, a_goa, a_gog):
        s_id, i = pl.program_id(0), pl.program_id(1)
        lane_accs = ((a_gq, dgq_ref), (a_gk, dgk_ref), (a_sink, dsink_ref), (a_lng, dlng_ref), (a_lnb, dlnb_ref),
                     (a_goa, dgoa_ref), (a_gog, dgog_ref))

        @pl.when((s_id == 0) & (i == 0))
        def _():
            for acc, _ in lane_accs:
                acc[...] = jnp.zeros_like(acc)
            dws_ref[...] = jnp.zeros_like(dws_ref)
            dbs_ref[...] = jnp.zeros_like(dbs_ref)

        gqv, gkv = gq_ref[...], gk_ref[...]

        @pl.when(i < nb)
        def _():
            masks = _band_masks(i == 0)
            upper = masks[0]
            att = att_ref[...]
            dya = dy_ref[:dm.aw, :]
            ra = _col_rms(att)
            ah = att * ra
            a_goa[...] += dya * ah
            dah = dya * goa_ref[...]
            datt_scr[...] = ra * (dah - ah * jnp.mean(dah * ah, axis=0, keepdims=True))
            for kh in range(dm.nkv):
                kn_tb, _, _, vb = _kv_band(dm, kh, p_ref, pkv_ref, gkv)
                rq, qh, qnb = _group_queries(dm, kh, p_ref, gqv)
                heads = _group_heads(dm, kh)
                probs = [probs_ref[h * BLOCK:(h + 1) * BLOCK, :] for h in heads]
                probs_b = jnp.concatenate([_unfold(p, upper).astype(MXU) for p in probs], axis=1)
                do_b = jnp.concatenate([datt_scr[h * dm.hd:(h + 1) * dm.hd, :] for h in heads], axis=1).astype(MXU)
                dp = _dot(vb, do_b)
                ds = []
                for g, h in enumerate(heads):
                    p, dp_h = probs[g], _fold(dp[:, g * BLOCK:(g + 1) * BLOCK], upper)
                    delta = jnp.sum(p * dp_h, axis=0, keepdims=True)
                    ds.append(_unfold(p * (dp_h - delta) * (dm.hd ** -0.5), upper).astype(MXU))
                    a_sink[h:h + 1, :] += -(sink_probs_ref[h:h + 1, :] * delta)
                dsb = jnp.concatenate(ds, axis=1)
                dqn = _dot(kn_tb, dsb)
                dkn = _dot_nt(qnb, dsb)
                dvb = _dot_nt(do_b, probs_b)
                a_gq[...] += dqn * qh
                dqh = dqn * gqv
                dq = rq * (dqh - qh * jnp.mean(dqh * qh, axis=0, keepdims=True))
                for g, h in enumerate(heads):
                    dproj_ref[h * dm.hd:(h + 1) * dm.hd, :] = dq[:, g * BLOCK:(g + 1) * BLOCK].astype(dproj_ref.dtype)
                krows = slice(kh * dm.hd, (kh + 1) * dm.hd)
                vrows = slice(dm.kvw + kh * dm.hd, dm.kvw + (kh + 1) * dm.hd)
                prevpart_scr[krows, :] = dkn[:, :BLOCK]
                prevpart_scr[vrows, :] = dvb[:, :BLOCK]
                curpart_scr[krows, :] = dkn[:, BLOCK:]
                curpart_scr[vrows, :] = dvb[:, BLOCK:]
            dproj_ref[dm.aw:dm.zoff, :] = jnp.zeros((kvw2, BLOCK), dproj_ref.dtype)
            tril = _tril()
            gt, u, mixed, xh, rstd, vnb, wts, (zu, cu), (zv, cv) = _gate_fwd(dm, p_ref, lng_ref, lnb_ref, ws_ref, bs_ref, tril)
            dyg = dy_ref[dm.aw:, :]
            rg = _col_rms(gt)
            gh = gt * rg
            a_gog[...] += dyg * gh
            dgh = dyg * gog_ref[...]
            dgt = rg * (dgh - gh * jnp.mean(dgh * gh, axis=0, keepdims=True))
            du = dgt * mixed
            dmix = dgt * u
            dmixb = dmix.astype(MXU)
            dbs_ref[...] += jnp.sum(_by_group(dmix, dm.ng), axis=1)
            dvn = []
            for g in range(dm.ng):
                rows = slice(g * GROUP_DIM, (g + 1) * GROUP_DIM)
                dws_ref[g] += jnp.where(tril, _dot_tn(dmixb[rows], vnb[rows]), 0.0)
                dvn.append(_dot(dmixb[rows], wts[g]))
            dvn = jnp.concatenate(dvn, axis=0)
            a_lng[...] += dvn * xh
            a_lnb[...] += dvn
            dxh3, xh3 = _by_group(dvn * lng_ref[...], dm.ng), _by_group(xh, dm.ng)
            dv = (rstd * (dxh3 - jnp.mean(dxh3, axis=1, keepdims=True) - xh3 * jnp.mean(dxh3 * xh3, axis=1, keepdims=True))).reshape(dm.gw, BLOCK)
            dgu = cu + zu * (jnp.exp(-0.5 * zu * zu) * _INV_SQRT_2PI)
            dgv = cv + zv * (jnp.exp(-0.5 * zv * zv) * _INV_SQRT_2PI)
            dproj_ref[dm.zoff:dm.zoff + dm.gw, :] = (du * dgu).astype(dproj_ref.dtype)
            dproj_ref[dm.zoff + dm.gw:, :] = (dv * dgv).astype(dproj_ref.dtype)

        @pl.when(i == nb)
        def _():
            prevpart_scr[...] = jnp.zeros_like(prevpart_scr)

        @pl.when(i >= 1)
        def _():
            tot = carry_scr[...] + prevpart_scr[...]
            for kh in range(dm.nkv):
                krows = slice(kh * dm.hd, (kh + 1) * dm.hd)
                kraw = kprev_scr[krows, :]
                rk = _col_rms(kraw)
                khat = kraw * rk
                dkn = tot[krows, :]
                a_gk[...] += dkn * khat
                dkh = dkn * gkv[:, :BLOCK]
                dk = rk * (dkh - khat * jnp.mean(dkh * khat, axis=0, keepdims=True))
                dkv_ref[krows, :] = dk.astype(dkv_ref.dtype)
            dkv_ref[dm.kvw:, :] = tot[dm.kvw:, :].astype(dkv_ref.dtype)

        @pl.when(i < nb)
        def _():
            carry_scr[...] = curpart_scr[...]
            kprev_scr[...] = p_ref[dm.aw:dm.aw + dm.kvw, :]

        @pl.when((s_id == nseq - 1) & (i == nb))
        def _():
            for acc, out in lane_accs:
                out[...] = jnp.sum(acc[...], axis=1, keepdims=True)

    col = lambda rows: jax.ShapeDtypeStruct((rows, 1), F32)
    lane = lambda rows: pltpu.VMEM((rows, LANES), F32)
    return pl.pallas_call(
        body, name="mixer_bwd", grid=(nseq, nb + 1),
        in_specs=[pl.BlockSpec((dm.inw, BLOCK), cur), pl.BlockSpec((kvw2, BLOCK), prev), pl.BlockSpec((dm.d, BLOCK), cur),
                  pl.BlockSpec((dm.nh * BLOCK, BLOCK), cur), pl.BlockSpec((dm.nh, BLOCK), cur), pl.BlockSpec((dm.aw, BLOCK), cur),
                  full(gq.shape), full(gk2.shape), pl.BlockSpec(memory_space=pltpu.SMEM),
                  full(lng.shape), full(lnb.shape), full(ws.shape), full(bs.shape), full(goa.shape), full(gog.shape)],
        out_specs=[pl.BlockSpec((dm.inw, BLOCK), cur), pl.BlockSpec((kvw2, BLOCK), prev_kv_out),
                   full((dm.hd, 1)), full((dm.hd, 1)), full((dm.nh, 1)), full((dm.gw, 1)), full((dm.gw, 1)), full(ws.shape),
                   full(bs.shape), full((dm.aw, 1)), full((dm.gw, 1))],
        out_shape=[jax.ShapeDtypeStruct((dm.inw, t), MXU), jax.ShapeDtypeStruct((kvw2, t), MXU),
                   col(dm.hd), col(dm.hd), col(dm.nh), col(dm.gw), col(dm.gw), jax.ShapeDtypeStruct(ws.shape, F32),
                   jax.ShapeDtypeStruct(bs.shape, F32), col(dm.aw), col(dm.gw)],
        scratch_shapes=[pltpu.VMEM((dm.aw, BLOCK), F32),
                        pltpu.VMEM((kvw2, BLOCK), F32), pltpu.VMEM((kvw2, BLOCK), F32), pltpu.VMEM((kvw2, BLOCK), F32),
                        pltpu.VMEM((dm.kvw, BLOCK), F32),
                        pltpu.VMEM((dm.hd, dm.grp * BLOCK), F32), lane(dm.hd), lane(dm.nh), lane(dm.gw), lane(dm.gw), lane(dm.aw), lane(dm.gw)],
        compiler_params=_cparams(("arbitrary", "arbitrary"), 48),
    )(proj_t, proj_t, dy_t, *saved, gq, gk2, sinks, lng, lnb, ws, bs, goa, gog)


def _patch_kv(dproj_t, dkv_t, dm):
    t = dproj_t.shape[1]
    tc = _pick(t, (1024, 512, 256, 128))
    kvw2 = 2 * dm.kvw
    kvblk = dm.aw // kvw2

    def body(dproj_hbm, dkv_ref, out_ref):
        del dproj_hbm
        out_ref[...] = dkv_ref[...]

    return pl.pallas_call(
        body, name="patch_kv", grid=(t // tc,),
        in_specs=[ANY, pl.BlockSpec((kvw2, tc), lambda i: (0, i))],
        out_specs=pl.BlockSpec((kvw2, tc), lambda i: (kvblk, i)),
        out_shape=jax.ShapeDtypeStruct(dproj_t.shape, dproj_t.dtype),
        input_output_aliases={0: 0},
        compiler_params=_cparams(("parallel",), 32),
    )(dproj_t, dkv_t)


def _place():
    x, y, c = lax.axis_index("x"), lax.axis_index("y"), lax.axis_index("c")
    return x, y, c


def _handshake(peers):
    barrier = pltpu.get_barrier_semaphore()
    for p in peers:
        pl.semaphore_signal(barrier, inc=1, device_id=p, device_id_type=MESH)
    pl.semaphore_wait(barrier, len(peers))


def _sequencer_mesh():
    return plsc.ScalarSubcoreMesh(axis_name="sequencer", num_cores=1)


GATHER_CHUNKS = 8
BF16_ROWS = 16


def _row_chunks(n, k):
    tiles = n // BF16_ROWS
    sizes = [(tiles // k + (1 if i < tiles % k else 0)) * BF16_ROWS for i in range(k)]
    return [(sum(sizes[:i]), sz) for i, sz in enumerate(sizes) if sz]


def _allgather_weight(name, collective_id, shard):
    n = shard.shape[0]
    assert n % BF16_ROWS == 0
    chunks = _row_chunks(n, GATHER_CHUNKS)
    nc = len(chunks)

    def body(src, out, send_sems, recv_sems, local_sem):
        x, y, c = _place()
        me, sib, xn, yn, diag = (x, y, c), (x, y, 1 - c), (1 - x, y, c), (x, 1 - y, c), (1 - x, 1 - y, c)
        relay_to = (x ^ c, y ^ (1 - c), c)
        relay_of = (x ^ (1 - c), y ^ c, c)
        _handshake([sib, xn, yn])

        def rows(place, ci):
            px, py, pc = place
            off, size = chunks[ci]
            return out.at[pl.ds(pl.multiple_of((4 * px + 2 * py + pc) * n + off, BF16_ROWS), size), :]

        def copy(k, ci, block, to, from_src=False):
            off, size = chunks[ci]
            return pltpu.make_async_remote_copy(
                src_ref=src.at[pl.ds(off, size), :] if from_src else rows(block, ci), dst_ref=rows(block, ci),
                send_sem=send_sems.at[ci, k], recv_sem=recv_sems.at[ci, k], device_id=to, device_id_type=MESH)

        mine = pltpu.make_async_copy(src, out.at[pl.ds(pl.multiple_of((4 * x + 2 * y + c) * n, BF16_ROWS), n), :], local_sem)
        mine.start()
        sent = []
        for ci in range(nc):
            sent += [copy(0, ci, me, sib, from_src=True), copy(1, ci, me, xn, from_src=True), copy(2, ci, me, yn, from_src=True)]
        for cp in sent:
            cp.start()
        for ci in range(nc):
            copy(1, ci, xn, me).wait_recv()
            copy(2, ci, yn, me).wait_recv()
            passed = [copy(3, ci, relay_of, relay_to), copy(4, ci, xn, sib), copy(5, ci, yn, sib)]
            for cp in passed:
                cp.start()
            sent += passed
        for ci in range(nc):
            copy(3, ci, diag, me).wait_recv()
            passed = copy(6, ci, diag, sib)
            passed.start()
            sent.append(passed)
        for ci in range(nc):
            copy(0, ci, sib, me).wait_recv()
            for k, block in ((4, (1 - x, y, 1 - c)), (5, (x, 1 - y, 1 - c)), (6, (1 - x, 1 - y, 1 - c))):
                copy(k, ci, block, me).wait_recv()
        for cp in sent:
            cp.wait_send()
        mine.wait()

    return pl.kernel(
        body, name=name,
        out_type=jax.ShapeDtypeStruct((N_DEV * n, shard.shape[1]), shard.dtype),
        mesh=_sequencer_mesh(),
        scratch_types=[pltpu.SemaphoreType.DMA((nc, 7)), pltpu.SemaphoreType.DMA((nc, 7)), pltpu.SemaphoreType.DMA],
        compiler_params=pltpu.CompilerParams(collective_id=collective_id),
    )(shard)


_FLIPS = [(0, 0, 1), (1, 0, 0), (0, 1, 0), (1, 1, 0), (1, 0, 1), (0, 1, 1), (1, 1, 1)]


def _scatter_grad(name, collective_id, grad):
    n = grad.shape[0] // N_DEV

    def body(src, out, send_sems, recv_sems, local_sem):
        x, y, c = _place()
        me_idx = 4 * x + 2 * y + c
        peers = [(x ^ fx, y ^ fy, c ^ fc) for (fx, fy, fc) in _FLIPS]
        _handshake(peers)

        def block(idx):
            return src.at[pl.ds(pl.multiple_of(idx * n, 16), n), :]

        copies = [pltpu.make_async_remote_copy(
            src_ref=block(4 * px + 2 * py + pc), dst_ref=out.at[me_idx], send_sem=send_sems.at[k], recv_sem=recv_sems.at[k],
            device_id=(px, py, pc), device_id_type=MESH) for k, (px, py, pc) in enumerate(peers)]
        mine = pltpu.make_async_copy(block(me_idx), out.at[me_idx], local_sem)
        mine.start()
        for cp in copies:
            cp.start()
        for cp in copies:
            cp.wait_recv()
        for cp in copies:
            cp.wait_send()
        mine.wait()

    return pl.kernel(
        body, name=name,
        out_type=jax.ShapeDtypeStruct((N_DEV, n, grad.shape[1]), grad.dtype),
        mesh=_sequencer_mesh(),
        scratch_types=[pltpu.SemaphoreType.DMA((7,)), pltpu.SemaphoreType.DMA((7,)), pltpu.SemaphoreType.DMA],
        compiler_params=pltpu.CompilerParams(collective_id=collective_id),
    )(grad)


def _allgather_rows(name, collective_id, part):
    def body(src, out, send_sems, recv_sems, local_sem):
        x, y, c = _place()
        me_idx = 4 * x + 2 * y + c
        peers = [(x ^ fx, y ^ fy, c ^ fc) for (fx, fy, fc) in _FLIPS]
        _handshake(peers)
        copies = [pltpu.make_async_remote_copy(
            src_ref=src, dst_ref=out.at[me_idx], send_sem=send_sems.at[k], recv_sem=recv_sems.at[k],
            device_id=peer, device_id_type=MESH) for k, peer in enumerate(peers)]
        mine = pltpu.make_async_copy(src, out.at[me_idx], local_sem)
        mine.start()
        for cp in copies:
            cp.start()
        for cp in copies:
            cp.wait_recv()
        for cp in copies:
            cp.wait_send()
        mine.wait()

    return pl.kernel(
        body, name=name,
        out_type=jax.ShapeDtypeStruct((N_DEV,) + part.shape, part.dtype),
        mesh=_sequencer_mesh(),
        scratch_types=[pltpu.SemaphoreType.DMA((7,)), pltpu.SemaphoreType.DMA((7,)), pltpu.SemaphoreType.DMA],
        compiler_params=pltpu.CompilerParams(collective_id=collective_id),
    )(part)


def _adamw_math(w, g, m, v):
    m = ADAM_B1 * m + (1.0 - ADAM_B1) * g
    v = ADAM_B2 * v + (1.0 - ADAM_B2) * (g * g)
    m_hat = m / (1.0 - ADAM_B1 ** ADAM_STEP)
    v_hat = v / (1.0 - ADAM_B2 ** ADAM_STEP)
    delta = -ADAM_LR * (m_hat / (jnp.sqrt(v_hat) + ADAM_EPS) + ADAM_WD * w)
    return delta, m, v


def _sum_adamw(name, slots, w, m, v, after):
    _, n, kk = slots.shape
    tr = _pick(n, (208, 176, 128, 96, 64, 32, 16))

    def body(s_ref, w_ref, m_ref, v_ref, after_ref, g_ref, d_ref, nm_ref, nv_ref):
        del after_ref
        g = s_ref[0].astype(F32)
        for p in range(1, N_DEV):
            g = g + s_ref[p].astype(F32)
        g_ref[...] = g
        d_ref[...], nm_ref[...], nv_ref[...] = _adamw_math(w_ref[...], g, m_ref[...], v_ref[...])

    row = pl.BlockSpec((tr, kk), lambda i: (i, 0))
    return pl.pallas_call(
        body, name=name, grid=(n // tr,),
        in_specs=[pl.BlockSpec((N_DEV, tr, kk), lambda i: (0, i, 0)), row, row, row, ANY],
        out_specs=[row] * 4,
        out_shape=[jax.ShapeDtypeStruct((n, kk), F32)] * 4,
        compiler_params=_cparams(("parallel",), 48),
    )(slots, w, m, v, after)


def _allreduce_small_adamw(early_slots, late, w, m, v, after):
    ra, rb = early_slots.shape[1], late.shape[0]

    def body(early_ref, late_ref, w_ref, m_ref, v_ref, after_ref, g_ref, d_ref, nm_ref, nv_ref, slots, send_sems, recv_sems):
        del after_ref
        x, y, c = _place()
        me_idx = 4 * x + 2 * y + c
        copies = []
        for k, (fx, fy, fc) in enumerate(_FLIPS):
            px, py, pc = x ^ fx, y ^ fy, c ^ fc
            copies.append(pltpu.make_async_remote_copy(
                src_ref=late_ref, dst_ref=slots.at[me_idx], send_sem=send_sems.at[k], recv_sem=recv_sems.at[k],
                device_id=(px, py, pc), device_id_type=MESH))
        for cp in copies:
            cp.start()
        slots[me_idx] = late_ref[...]
        g = early_ref[0]
        for p in range(1, N_DEV):
            g = g + early_ref[p]
        early = pl.ds(0, ra)
        g_ref[early, :] = g
        d_ref[early, :], nm_ref[early, :], nv_ref[early, :] = _adamw_math(w_ref[early, :], g, m_ref[early, :], v_ref[early, :])
        for cp in copies:
            cp.wait_recv()
        for cp in copies:
            cp.wait_send()
        g = slots[0]
        for p in range(1, N_DEV):
            g = g + slots[p]
        tail = pl.ds(ra, rb)
        g_ref[tail, :] = g
        d_ref[tail, :], nm_ref[tail, :], nv_ref[tail, :] = _adamw_math(w_ref[tail, :], g, m_ref[tail, :], v_ref[tail, :])

    vm = pl.BlockSpec(memory_space=pltpu.VMEM)
    return pl.pallas_call(
        body, name="allreduce_small_adamw",
        in_specs=[vm] * 5 + [ANY], out_specs=[vm] * 4,
        out_shape=[jax.ShapeDtypeStruct((ra + rb, LANES), F32)] * 4,
        scratch_shapes=[pltpu.VMEM((N_DEV, rb, LANES), F32), pltpu.SemaphoreType.DMA((7,)), pltpu.SemaphoreType.DMA((7,))],
        compiler_params=pltpu.CompilerParams(vmem_limit_bytes=48 * MIB),
    )(early_slots, late, w, m, v, after)


def _pack(arrs):
    parts, meta, off = [], [], 0
    for a in arrs:
        flat = a.reshape(-1).astype(F32)
        rows = -(-flat.shape[0] // LANES)
        rows8 = -(-rows // 8) * 8
        flat = jnp.pad(flat, (0, rows8 * LANES - flat.shape[0]))
        parts.append(flat.reshape(rows8, LANES))
        meta.append((off, a.shape, a.size))
        off += rows8
    return jnp.concatenate(parts, axis=0), meta


def _unpack(packed, meta):
    outs = []
    for off, shape, size in meta:
        rows = -(-size // LANES)
        outs.append(packed[off:off + rows].reshape(-1)[:size].reshape(shape))
    return outs


def _silu_parts(a):
    sg = 0.5 + 0.5 * jnp.tanh(0.5 * a)
    return a * sg, sg * (1.0 + a * (1.0 - sg))


def kernel(x, norm1_g, w_in, q_norm_g, k_norm_g, attn_sinks, gate_ln_g, gate_ln_b, w_spatial, b_spatial, out_norm_attn_g, out_norm_gate_g, w_out, norm2_g, w_ffn_gate, w_ffn_up, w_ffn_down, loss_target, m_norm1_g, m_w_in, m_q_norm_g, m_k_norm_g, m_attn_sinks, m_gate_ln_g, m_gate_ln_b, m_w_spatial, m_b_spatial, m_out_norm_attn_g, m_out_norm_gate_g, m_w_out, m_norm2_g, m_w_ffn_gate, m_w_ffn_up, m_w_ffn_down, v_norm1_g, v_w_in, v_q_norm_g, v_k_norm_g, v_attn_sinks, v_gate_ln_g, v_gate_ln_b, v_w_spatial, v_b_spatial, v_out_norm_attn_g, v_out_norm_gate_g, v_w_out, v_norm2_g, v_w_ffn_gate, v_w_ffn_up, v_w_ffn_down):
    nseq, seq, d = x.shape
    t = nseq * seq
    nb = seq // BLOCK
    inw = w_in.shape[2] * N_DEV
    dm = _Dims(d, inw, q_norm_g.shape[-1])
    xf = x.reshape(t, d)
    tgt = loss_target.reshape(t, d)

    rows = lambda wv, transposed: jnp.swapaxes(wv, 1, 2)[0] if transposed else wv[0]
    big = {"w_in": (w_in, m_w_in, v_w_in, True), "w_out": (w_out, m_w_out, v_w_out, False),
           "w_ffn_gate": (w_ffn_gate, m_w_ffn_gate, v_w_ffn_gate, True), "w_ffn_up": (w_ffn_up, m_w_ffn_up, v_w_ffn_up, True),
           "w_ffn_down": (w_ffn_down, m_w_ffn_down, v_w_ffn_down, False)}
    big_rows = {nm: tuple(rows(arr, tr) for arr in (wv, mv, vv)) for nm, (wv, mv, vv, tr) in big.items()}
    shard = lambda nm: big_rows[nm][0].astype(WIRE)
    win_t = _allgather_weight("gather_w_in", 1, shard("w_in"))
    wout = _allgather_weight("gather_w_out", 2, shard("w_out"))
    wg_t = _allgather_weight("gather_w_ffn_gate", 3, shard("w_ffn_gate"))
    wu_t = _allgather_weight("gather_w_ffn_up", 9, shard("w_ffn_up"))
    wd = _allgather_weight("gather_w_ffn_down", 10, shard("w_ffn_down"))

    lanes = lambda v, n=BLOCK: jnp.broadcast_to(v.reshape(-1, 1), (v.size, n))
    prm = (lanes(q_norm_g, dm.grp * BLOCK), lanes(k_norm_g, 2 * BLOCK), attn_sinks[0], lanes(gate_ln_g), lanes(gate_ln_b), w_spatial[0], b_spatial[0],
           lanes(out_norm_attn_g), lanes(out_norm_gate_g))

    h1 = _rms_fwd("rms1_fwd", xf, norm1_g)
    (proj_t,) = _matmul("mm_in", win_t, h1, "nt", [F32])
    y_t, *attn_saved = _mixer_fwd(proj_t, prm, dm, nseq, nb)

    def residual_norm(acc, xr, g2):
        x2v = xr + acc
        return x2v, x2v * lax.rsqrt(jnp.mean(x2v * x2v, axis=-1, keepdims=True) + EPS) * g2

    x2, h2 = _matmul("mm_out", y_t, wout, "tn", [F32, MXU], epilogue=residual_norm, extras=[xf], rowvecs=[norm2_g], full_rows=True)
    silu_a, dsilu_a = _matmul("mm_gate", h2, wg_t, "nt", [MXU, MXU], epilogue=_silu_parts)
    s, dsilu_a_b = _matmul("mm_up", h2, wu_t, "nt", [MXU, MXU], epilogue=lambda ub, sv, dv: (sv.astype(F32) * ub, dv.astype(F32) * ub),
                           extras=[silu_a, dsilu_a])

    def loss_epilogue(acc, x2v, tv):
        diff = (x2v + acc) - tv
        dx3 = diff * (1.0 / d)
        return dx3, dx3, jnp.sum(diff * diff)

    dx3, dx3b, lossp = _matmul("mm_down", s, wd, "nn", [F32, MXU], epilogue=loss_epilogue, extras=[x2, tgt], partial=True)
    loss_part = (0.5 / d) * jnp.sum(lossp[::8, ::LANES])

    def dswiglu(ds, dsilu_b, silu):
        return ds * dsilu_b.astype(F32), ds * silu.astype(F32)

    (g_wd,) = _matmul("mm_gw_down", s, dx3b, "tn", [WIRE])
    sl_wd = _scatter_grad("scatter_w_ffn_down", 4, g_wd)
    da, db = _matmul("mm_d_down", dx3b, wd, "nt", [MXU, MXU], epilogue=dswiglu, extras=[dsilu_a_b, silu_a], after=[g_wd])
    (g_wg,) = _matmul("mm_gw_gate", da, h2, "tn", [WIRE])
    sl_wg = _scatter_grad("scatter_w_ffn_gate", 5, g_wg)
    (g_wu,) = _matmul("mm_gw_up", db, h2, "tn", [WIRE], after=[g_wg])
    sl_wu = _scatter_grad("scatter_w_ffn_up", 6, g_wu)
    (dh2a,) = _matmul("mm_dh2_gate", da, wg_t, "nn", [F32], after=[g_wu])
    (dh2,) = _matmul("mm_dh2_up", db, wu_t, "nn", [F32], epilogue=lambda acc, pv: (pv + acc,), extras=[dh2a])

    dy_t, dx2, dx2b, dg2 = _norm_bwd_matmul("mm_d_out", wout, dh2, x2, norm2_g, dx3, after=dh2a)
    (g_wout,) = _matmul("mm_gw_out", y_t, dx2b, "nn", [WIRE], after=[dy_t])
    sl_wout = _scatter_grad("scatter_w_out", 7, g_wout)
    (dproj0, dkv, dgq, dgk, dsink, dlng, dlnb, dws, dbs, dgoa, dgog) = _mixer_bwd(proj_t, dy_t, attn_saved, prm, dm, nseq, nb)
    early_g = [dgq, dgk, dsink, dlng, dlnb, dws, dbs, dgoa, dgog, dg2, loss_part.reshape(1)]
    early_slots = _allgather_rows("gather_small_grads", 11, _pack(early_g)[0])
    dproj_t = _patch_kv(dproj0, dkv, dm)
    (g_win,) = _matmul("mm_gw_in", dproj_t, h1, "nn", [WIRE])
    sl_win = _scatter_grad("scatter_w_in", 8, g_win)

    def norm1_backward(dh1, xv, dx2v, g1):
        r = lax.rsqrt(jnp.mean(xv * xv, axis=-1, keepdims=True) + EPS)
        xh = xv * r
        dxh = dh1 * g1
        return dx2v + r * (dxh - xh * jnp.mean(dxh * xh, axis=-1, keepdims=True)), jnp.sum(dh1 * xh, axis=0, keepdims=True)

    dx, dg1 = _matmul("mm_d_in", dproj_t, win_t, "tn", [F32], epilogue=norm1_backward, extras=[xf, dx2], rowvecs=[norm1_g],
                      after=[g_win], col_sum=True, full_rows=True)

    big_out = {}
    last = dx

    def big_update(nm, sl, after):
        res = _sum_adamw("adamw_" + nm, sl, *big_rows[nm], after=after)
        big_out[nm] = tuple(jnp.swapaxes(r[None], 1, 2) if big[nm][3] else r[None] for r in res)
        return res[1]

    for nm, sl in (("w_ffn_down", sl_wd), ("w_ffn_gate", sl_wg), ("w_ffn_up", sl_wu), ("w_out", sl_wout)):
        last = big_update(nm, sl, last)

    zero = jnp.zeros((1,), F32)
    small_names = ["q_norm_g", "k_norm_g", "attn_sinks", "gate_ln_g", "gate_ln_b", "w_spatial", "b_spatial",
                   "out_norm_attn_g", "out_norm_gate_g", "norm2_g", "loss", "norm1_g"]
    small_w = [q_norm_g, k_norm_g, attn_sinks, gate_ln_g, gate_ln_b, w_spatial, b_spatial, out_norm_attn_g, out_norm_gate_g, norm2_g, zero, norm1_g]
    small_m = [m_q_norm_g, m_k_norm_g, m_attn_sinks, m_gate_ln_g, m_gate_ln_b, m_w_spatial, m_b_spatial, m_out_norm_attn_g, m_out_norm_gate_g, m_norm2_g, zero, m_norm1_g]
    small_v = [v_q_norm_g, v_k_norm_g, v_attn_sinks, v_gate_ln_g, v_gate_ln_b, v_w_spatial, v_b_spatial, v_out_norm_attn_g, v_out_norm_gate_g, v_norm2_g, zero, v_norm1_g]
    pw, meta = _pack(small_w)
    sg, sd, sm, sv = _allreduce_small_adamw(early_slots, _pack([dg1])[0], pw, _pack(small_m)[0], _pack(small_v)[0], after=last)
    big_update("w_in", sl_win, sd)
    ug, ud, um, uv = _unpack(sg, meta), _unpack(sd, meta), _unpack(sm, meta), _unpack(sv, meta)
    small_out = {nm: (ug[k], ud[k], um[k], uv[k]) for k, nm in enumerate(small_names)}
    loss = small_out["loss"][0].reshape(())

    order = ["norm1_g", "w_in", "q_norm_g", "k_norm_g", "attn_sinks", "gate_ln_g", "gate_ln_b", "w_spatial", "b_spatial",
             "out_norm_attn_g", "out_norm_gate_g", "w_out", "norm2_g", "w_ffn_gate", "w_ffn_up", "w_ffn_down"]
    allo = {**big_out, **small_out}
    outs = [loss, dx.reshape(nseq, seq, d)]
    for k in range(4):
        outs += [allo[nm][k] for nm in order]
    return tuple(outs)
```

```python
import math

import jax
import jax.numpy as jnp
from jax import lax
from jax.experimental import pallas as pl
from jax.experimental.pallas import tpu as pltpu
from jax.experimental.pallas import tpu_sc as plsc

F32 = jnp.float32
MXU = jnp.bfloat16
WIRE = jnp.bfloat16
EPS = 1e-6
BLOCK = 128
GROUP_DIM = 128
N_KV_HEADS = 2
NEG = -1e30
N_DEV = 8
LANES = 128
MIB = 1024 * 1024

ADAM_LR = 0.001
ADAM_B1 = 0.9
ADAM_B2 = 0.999
ADAM_EPS = 1e-08
ADAM_WD = 0.01
ADAM_STEP = 10

MESH = pl.DeviceIdType.MESH
ANY = pl.BlockSpec(memory_space=pl.ANY)


def _pick(n, cands):
    for c in cands:
        if n % c == 0:
            return c
    return n


def _cparams(sem, vmem_mb):
    return pltpu.CompilerParams(dimension_semantics=sem, vmem_limit_bytes=vmem_mb * MIB)


VMEM_TILE_BUDGET = 50 * MIB
HBM_BYTES_PER_US = 3.0e6
STEP_US = 0.4
MXU_COLS = 256
MIN_TILE_N = 2 * MXU_COLS


def _tile_candidates(n):
    return [c for c in range(min(n, 2048), 0, -LANES) if n % c == 0 and c % LANES == 0] or [n]


def _matmul_tiles(m, n, kk, esz, extra_sizes, out_sizes, full_rows):
    best = None
    wide = [n] if full_rows else [c for c in _tile_candidates(n) if c >= MIN_TILE_N and c % MXU_COLS == 0] or _tile_candidates(n)
    for tm in _tile_candidates(m):
        for tn in wide:
            b_buffers = 1 if full_rows else 2
            temps = 2 if full_rows else 0
            vmem = (2 * tm + b_buffers * tn) * kk * esz + tm * tn * (4 * (1 + temps) + 2 * sum(extra_sizes) + 2 * sum(out_sizes))
            if vmem > VMEM_TILE_BUDGET:
                continue
            cost = (m // tm) * n * kk * esz / HBM_BYTES_PER_US + (m // tm) * (n // tn) * STEP_US
            if best is None or cost < best[0]:
                best = (cost, tm, tn, vmem)
    assert best is not None, (m, n, kk)
    return best[1:]


def _matmul(name, a, b, mode, out_dtypes, epilogue=None, extras=(), rowvecs=(), after=(), partial=False, col_sum=False, full_rows=False):
    if mode == "nn":
        (m, kk), n = a.shape, b.shape[1]
        dn = (((1,), (0,)), ((), ()))
    elif mode == "nt":
        (m, kk), n = a.shape, b.shape[0]
        dn = (((1,), (1,)), ((), ()))
    else:
        (kk, m), n = a.shape, b.shape[1]
        dn = (((0,), (0,)), ((), ()))
    tm, tn, vmem = _matmul_tiles(m, n, kk, a.dtype.itemsize, [e.dtype.itemsize for e in extras],
                                 [jnp.dtype(dt).itemsize for dt in out_dtypes], full_rows)
    a_spec = pl.BlockSpec((kk, tm), lambda i, j: (0, i)) if mode == "tn" else pl.BlockSpec((tm, kk), lambda i, j: (i, 0))
    resident = dict(pipeline_mode=pl.Buffered(1)) if full_rows else {}
    b_spec = pl.BlockSpec((tn, kk), lambda i, j: (j, 0), **resident) if mode == "nt" else pl.BlockSpec((kk, tn), lambda i, j: (0, j), **resident)
    tile = pl.BlockSpec((tm, tn), lambda i, j: (i, j))
    row = pl.BlockSpec((1, tn), lambda i, j: (0, j))
    ne, nr, na, no = len(extras), len(rowvecs), len(after), len(out_dtypes)

    def body(a_ref, b_ref, *rest):
        in_refs, out_refs = rest[:ne + nr], rest[ne + nr + na:]
        acc = lax.dot_general(a_ref[...], b_ref[...], dn, preferred_element_type=F32)
        vals = (acc,) if epilogue is None else epilogue(acc, *[r[...] for r in in_refs])
        for o_ref, t in zip(out_refs[:no], vals[:no]):
            o_ref[...] = t.astype(o_ref.dtype)
        if partial:
            out_refs[no][...] = jnp.full((8, LANES), vals[no], F32)
        if col_sum:
            sum_ref = out_refs[-1]

            @pl.when(pl.program_id(0) == 0)
            def _():
                sum_ref[...] = jnp.zeros_like(sum_ref)

            sum_ref[...] += vals[-1]

    out_specs = [tile] * no
    out_shape = [jax.ShapeDtypeStruct((m, n), dt) for dt in out_dtypes]
    if partial:
        out_specs.append(pl.BlockSpec((8, LANES), lambda i, j: (i, j)))
        out_shape.append(jax.ShapeDtypeStruct((m // tm * 8, n // tn * LANES), F32))
    if col_sum:
        out_specs.append(row)
        out_shape.append(jax.ShapeDtypeStruct((1, n), F32))
    return pl.pallas_call(
        body, name=name, grid=(m // tm, n // tn),
        in_specs=[a_spec, b_spec] + [tile] * ne + [row] * nr + [ANY] * na,
        out_specs=out_specs, out_shape=out_shape,
        compiler_params=_cparams(("arbitrary" if col_sum else "parallel", "arbitrary"), min(vmem // MIB + 8, 58)),
    )(a, b, *extras, *rowvecs, *after)


def _rms_fwd(name, x, g):
    t, d = x.shape
    tm = _pick(t, (512, 256, 128))

    def body(x_ref, g_ref, h_ref):
        xv = x_ref[...]
        r = lax.rsqrt(jnp.mean(xv * xv, axis=-1, keepdims=True) + EPS)
        h_ref[...] = (xv * r * g_ref[...]).astype(h_ref.dtype)

    return pl.pallas_call(
        body, name=name, grid=(t // tm,),
        in_specs=[pl.BlockSpec((tm, d), lambda i: (i, 0)), pl.BlockSpec((1, d), lambda i: (0, 0))],
        out_specs=pl.BlockSpec((tm, d), lambda i: (i, 0)),
        out_shape=jax.ShapeDtypeStruct((t, d), MXU),
        compiler_params=_cparams(("parallel",), 32),
    )(x, g)


def _norm_bwd_matmul(name, w, dh, x, g, res, after):
    t, d = x.shape
    m = w.shape[0]
    tn = _pick(t, (256, 128))

    def body(w_ref, dh_ref, x_ref, g_ref, res_ref, after_ref, out_ref, dx_ref, dxb_ref, dg_ref):
        del after_ref

        @pl.when(pl.program_id(0) == 0)
        def _():
            dg_ref[...] = jnp.zeros_like(dg_ref)

        xv, dhv = x_ref[...], dh_ref[...]
        r = lax.rsqrt(jnp.mean(xv * xv, axis=-1, keepdims=True) + EPS)
        xh = xv * r
        dg_ref[...] += jnp.sum(dhv * xh, axis=0, keepdims=True)
        dxh = dhv * g_ref[...]
        dx = res_ref[...] + r * (dxh - xh * jnp.mean(dxh * xh, axis=-1, keepdims=True))
        dx_ref[...] = dx
        dxb = dx.astype(MXU)
        dxb_ref[...] = dxb
        out_ref[...] = lax.dot_general(w_ref[...], dxb, (((1,), (1,)), ((), ())), preferred_element_type=F32)

    row = pl.BlockSpec((tn, d), lambda j: (j, 0))
    vec = pl.BlockSpec((1, d), lambda j: (0, 0))
    return pl.pallas_call(
        body, name=name, grid=(t // tn,),
        in_specs=[pl.BlockSpec((m, d), lambda j: (0, 0), pipeline_mode=pl.Buffered(1)), row, row, vec, row, ANY],
        out_specs=[pl.BlockSpec((m, tn), lambda j: (0, j)), row, row, vec],
        out_shape=[jax.ShapeDtypeStruct((m, t), F32), jax.ShapeDtypeStruct((t, d), F32), jax.ShapeDtypeStruct((t, d), MXU),
                   jax.ShapeDtypeStruct((1, d), F32)],
        compiler_params=_cparams(("arbitrary",), 52),
    )(w, dh, x, g, res, after)


_INV_SQRT2 = 0.7071067811865476
_INV_SQRT_2PI = 0.3989422804014327


def _dot_nt(a, b):
    return lax.dot_general(a, b, (((1,), (1,)), ((), ())), preferred_element_type=F32)


def _dot_tn(a, b):
    return lax.dot_general(a, b, (((0,), (0,)), ((), ())), preferred_element_type=F32)


def _dot(a, b):
    return jnp.dot(a, b, preferred_element_type=F32)


def _col_mean(v):
    rows = v.shape[0]
    if rows > GROUP_DIM and rows % GROUP_DIM == 0:
        v = jnp.sum(v.reshape(rows // GROUP_DIM, GROUP_DIM, v.shape[1]), axis=0)
    return jnp.sum(v, axis=0, keepdims=True) * (1.0 / rows)


def _col_rms(v):
    return lax.rsqrt(_col_mean(v * v) + EPS)


class _Dims:
    def __init__(self, d_model, in_width, head_dim):
        self.d = d_model
        self.aw = d_model // 2
        self.gw = d_model - self.aw
        self.kvw = (in_width - self.aw - 2 * self.gw) // 2
        self.hd = head_dim
        self.nh = self.aw // head_dim
        self.nkv = self.kvw // head_dim
        self.grp = self.nh // self.nkv
        self.ng = self.gw // GROUP_DIM
        self.inw = in_width
        self.zoff = self.aw + 2 * self.kvw
        assert self.nkv == N_KV_HEADS and self.zoff + 2 * self.gw == in_width and self.aw % (2 * self.kvw) == 0


def _band_masks(first):
    r = lax.broadcasted_iota(jnp.int32, (BLOCK, BLOCK), 0)
    t = lax.broadcasted_iota(jnp.int32, (BLOCK, BLOCK), 1)
    upper = r > t
    dist = t - r + jnp.where(upper, BLOCK, 0)
    return upper, jnp.logical_not(upper & first), dist.astype(F32)


def _fold(full, upper):
    return jnp.where(upper, full[:BLOCK], full[BLOCK:])


def _unfold(folded, upper):
    zero = jnp.zeros_like(folded)
    return jnp.concatenate([jnp.where(upper, folded, zero), jnp.where(upper, zero, folded)], axis=0)


def _kv_band(dm, kh, p_ref, pkv_ref, gk2):
    ko = dm.aw + kh * dm.hd
    vo = dm.aw + dm.kvw + kh * dm.hd
    k_t = jnp.concatenate([pkv_ref[kh * dm.hd:(kh + 1) * dm.hd, :], p_ref[ko:ko + dm.hd, :]], axis=1)
    v_t = jnp.concatenate([pkv_ref[dm.kvw + kh * dm.hd:dm.kvw + (kh + 1) * dm.hd, :], p_ref[vo:vo + dm.hd, :]], axis=1)
    kn_t = k_t * _col_rms(k_t) * gk2
    return kn_t.astype(MXU), kn_t.T.astype(MXU), v_t.astype(MXU), v_t.T.astype(MXU)


def _group_heads(dm, kh):
    return range(kh * dm.grp, (kh + 1) * dm.grp)


def _group_queries(dm, kh, p_ref, gq):
    q = jnp.concatenate([p_ref[h * dm.hd:(h + 1) * dm.hd, :] for h in _group_heads(dm, kh)], axis=1)
    rq = _col_rms(q)
    qh = q * rq
    return rq, qh, (qh * gq).astype(MXU)


def _attn_group_fwd(dm, kh, p_ref, gq, kn, v_tb, sink_ref, masks):
    heads = _group_heads(dm, kh)
    rq, qh, qnb = _group_queries(dm, kh, p_ref, gq)
    upper, valid, dist = masks
    s = _dot(kn, qnb)
    probs, probs_b, sink_probs = [], [], []
    for g, h in enumerate(heads):
        slope, sink = math.pow(2.0, -8.0 * (h + 1) / dm.nh), sink_ref[h]
        logits = jnp.where(valid, _fold(s[:, g * BLOCK:(g + 1) * BLOCK], upper) * (dm.hd ** -0.5) - slope * dist, NEG)
        m = jnp.maximum(jnp.max(logits, axis=0, keepdims=True), sink)
        e = jnp.exp(logits - m)
        es = jnp.exp(sink - m)
        inv = 1.0 / (jnp.sum(e, axis=0, keepdims=True) + es)
        probs.append(e * inv)
        probs_b.append(_unfold(probs[g], upper).astype(MXU))
        sink_probs.append(es * inv)
    probs_b = jnp.concatenate(probs_b, axis=1)
    o = _dot(v_tb, probs_b)
    return o, probs, probs_b, sink_probs, rq, qh, qnb


def _gelu_cdf(z):
    return 0.5 * (1.0 + lax.erf(z * _INV_SQRT2))


def _by_group(v, ng):
    return v.reshape(ng, GROUP_DIM, v.shape[1])


def _gate_fwd(dm, p_ref, lng_ref, lnb_ref, ws_ref, bs_ref, tril):
    zu, zv = p_ref[dm.zoff:dm.zoff + dm.gw, :], p_ref[dm.zoff + dm.gw:dm.zoff + 2 * dm.gw, :]
    cu, cv = _gelu_cdf(zu), _gelu_cdf(zv)
    u, v = zu * cu, zv * cv
    v3 = _by_group(v, dm.ng)
    xc = v3 - jnp.mean(v3, axis=1, keepdims=True)
    rstd = lax.rsqrt(jnp.mean(xc * xc, axis=1, keepdims=True) + EPS)
    xh = (xc * rstd).reshape(dm.gw, BLOCK)
    vnb = (xh * lng_ref[...] + lnb_ref[...]).astype(MXU)
    wts = [jnp.where(tril, ws_ref[g], 0.0).astype(MXU) for g in range(dm.ng)]
    mixed = jnp.concatenate([_dot_nt(vnb[g * GROUP_DIM:(g + 1) * GROUP_DIM], wts[g]) + bs_ref[g:g + 1, :]
                             for g in range(dm.ng)], axis=0)
    return u * mixed, u, mixed, xh, rstd, vnb, wts, (zu, cu), (zv, cv)


def _mixer_specs(dm, nb, clamp):
    kvblk = dm.aw // (2 * dm.kvw)

    def cur(s, i):
        return (0, s * nb + clamp(i))

    def prev(s, i):
        return (kvblk, s * nb + jnp.maximum(clamp(i) - 1, 0))

    full = lambda shape: pl.BlockSpec(shape, lambda s, i: tuple(0 for _ in shape))
    return cur, prev, full


def _tril():
    return lax.broadcasted_iota(jnp.int32, (BLOCK, BLOCK), 0) >= lax.broadcasted_iota(jnp.int32, (BLOCK, BLOCK), 1)


def _mixer_fwd(proj_t, prm, dm, nseq, nb):
    gq, gk2, sinks, lng, lnb, ws, bs, goa, gog = prm
    t = proj_t.shape[1]
    cur, prev, full = _mixer_specs(dm, nb, lambda i: i)

    def body(p_ref, pkv_ref, gq_ref, gk_ref, sink_ref, lng_ref, lnb_ref, ws_ref, bs_ref, goa_ref, gog_ref,
             y_ref, probs_ref, sink_probs_ref, att_ref):
        i = pl.program_id(1)
        masks = _band_masks(i == 0)
        gqv, gkv = gq_ref[...], gk_ref[...]
        for kh in range(dm.nkv):
            _, kn, v_tb, _ = _kv_band(dm, kh, p_ref, pkv_ref, gkv)
            o, probs, _, sink_probs, _, _, _ = _attn_group_fwd(dm, kh, p_ref, gqv, kn, v_tb, sink_ref, masks)
            for g, h in enumerate(_group_heads(dm, kh)):
                att_ref[h * dm.hd:(h + 1) * dm.hd, :] = o[:, g * BLOCK:(g + 1) * BLOCK]
                probs_ref[h * BLOCK:(h + 1) * BLOCK, :] = probs[g]
                sink_probs_ref[h:h + 1, :] = sink_probs[g]
        att = att_ref[...]
        y_ref[:dm.aw, :] = (att * _col_rms(att) * goa_ref[...]).astype(y_ref.dtype)
        gt = _gate_fwd(dm, p_ref, lng_ref, lnb_ref, ws_ref, bs_ref, _tril())[0]
        y_ref[dm.aw:, :] = (gt * _col_rms(gt) * gog_ref[...]).astype(y_ref.dtype)

    return pl.pallas_call(
        body, name="mixer_fwd", grid=(nseq, nb),
        in_specs=[pl.BlockSpec((dm.inw, BLOCK), cur), pl.BlockSpec((2 * dm.kvw, BLOCK), prev),
                  full(gq.shape), full(gk2.shape), pl.BlockSpec(memory_space=pltpu.SMEM),
                  full(lng.shape), full(lnb.shape), full(ws.shape), full(bs.shape), full(goa.shape), full(gog.shape)],
        out_specs=[pl.BlockSpec((dm.d, BLOCK), cur), pl.BlockSpec((dm.nh * BLOCK, BLOCK), cur), pl.BlockSpec((dm.nh, BLOCK), cur),
                   pl.BlockSpec((dm.aw, BLOCK), cur)],
        out_shape=[jax.ShapeDtypeStruct((dm.d, t), MXU), jax.ShapeDtypeStruct((dm.nh * BLOCK, t), F32),
                   jax.ShapeDtypeStruct((dm.nh, t), F32), jax.ShapeDtypeStruct((dm.aw, t), F32)],
        compiler_params=_cparams(("parallel", "arbitrary"), 40),
    )(proj_t, proj_t, gq, gk2, sinks, lng, lnb, ws, bs, goa, gog)


def _mixer_bwd(proj_t, dy_t, saved, prm, dm, nseq, nb):
    gq, gk2, sinks, lng, lnb, ws, bs, goa, gog = prm
    t = proj_t.shape[1]
    clamp = lambda i: jnp.minimum(i, nb - 1)
    cur, prev, full = _mixer_specs(dm, nb, clamp)
    kvw2 = 2 * dm.kvw

    def prev_kv_out(s, i):
        return (0, s * nb + jnp.maximum(i - 1, 0))

    def body(p_ref, pkv_ref, dy_ref, probs_ref, sink_probs_ref, att_ref, gq_ref, gk_ref, sink_ref, lng_ref, lnb_ref, ws_ref, bs_ref,
             goa_ref, gog_ref, dproj_ref, dkv_ref, dgq_ref, dgk_ref, dsink_ref, dlng_ref, dlnb_ref, dws_ref, dbs_ref, dgoa_ref, dgog_ref,
             datt_scr, carry_scr, prevpart_scr, curpart_scr, kprev_scr,
             a_gq, a_gk, a_sink, a_lng, a_lnb, a_goa, a_gog):
        s_id, i = pl.program_id(0), pl.program_id(1)
        lane_accs = ((a_gq, dgq_ref), (a_gk, dgk_ref), (a_sink, dsink_ref), (a_lng, dlng_ref), (a_lnb, dlnb_ref),
                     (a_goa, dgoa_ref), (a_gog, dgog_ref))

        @pl.when((s_id == 0) & (i == 0))
        def _():
            for acc, _ in lane_accs:
                acc[...] = jnp.zeros_like(acc)
            dws_ref[...] = jnp.zeros_like(dws_ref)
            dbs_ref[...] = jnp.zeros_like(dbs_ref)

        gqv, gkv = gq_ref[...], gk_ref[...]

        @pl.when(i < nb)
        def _():
            masks = _band_masks(i == 0)
            upper = masks[0]
            att = att_ref[...]
            dya = dy_ref[:dm.aw, :]
            ra = _col_rms(att)
            ah = att * ra
            a_goa[...] += dya * ah
            dah = dya * goa_ref[...]
            datt_scr[...] = ra * (dah - ah * _col_mean(dah * ah))
            for kh in range(dm.nkv):
                kn_tb, _, _, vb = _kv_band(dm, kh, p_ref, pkv_ref, gkv)
                rq, qh, qnb = _group_queries(dm, kh, p_ref, gqv)
                heads = _group_heads(dm, kh)
                probs = [probs_ref[h * BLOCK:(h + 1) * BLOCK, :] for h in heads]
                probs_b = jnp.concatenate([_unfold(p, upper).astype(MXU) for p in probs], axis=1)
                do_b = jnp.concatenate([datt_scr[h * dm.hd:(h + 1) * dm.hd, :] for h in heads], axis=1).astype(MXU)
                dp = _dot(vb, do_b)
                ds = []
                for g, h in enumerate(heads):
                    p, dp_h = probs[g], _fold(dp[:, g * BLOCK:(g + 1) * BLOCK], upper)
                    delta = jnp.sum(p * dp_h, axis=0, keepdims=True)
                    ds.append(_unfold(p * (dp_h - delta) * (dm.hd ** -0.5), upper).astype(MXU))
                    a_sink[h:h + 1, :] += -(sink_probs_ref[h:h + 1, :] * delta)
                dsb = jnp.concatenate(ds, axis=1)
                dqn = _dot(kn_tb, dsb)
                dkn = _dot_nt(qnb, dsb)
                dvb = _dot_nt(do_b, probs_b)
                a_gq[...] += dqn * qh
                dqh = dqn * gqv
                dq = rq * (dqh - qh * jnp.mean(dqh * qh, axis=0, keepdims=True))
                for g, h in enumerate(heads):
                    dproj_ref[h * dm.hd:(h + 1) * dm.hd, :] = dq[:, g * BLOCK:(g + 1) * BLOCK].astype(dproj_ref.dtype)
                krows = slice(kh * dm.hd, (kh + 1) * dm.hd)
                vrows = slice(dm.kvw + kh * dm.hd, dm.kvw + (kh + 1) * dm.hd)
                prevpart_scr[krows, :] = dkn[:, :BLOCK]
                prevpart_scr[vrows, :] = dvb[:, :BLOCK]
                curpart_scr[krows, :] = dkn[:, BLOCK:]
                curpart_scr[vrows, :] = dvb[:, BLOCK:]
            dproj_ref[dm.aw:dm.zoff, :] = jnp.zeros((kvw2, BLOCK), dproj_ref.dtype)
            tril = _tril()
            gt, u, mixed, xh, rstd, vnb, wts, (zu, cu), (zv, cv) = _gate_fwd(dm, p_ref, lng_ref, lnb_ref, ws_ref, bs_ref, tril)
            dyg = dy_ref[dm.aw:, :]
            rg = _col_rms(gt)
            gh = gt * rg
            a_gog[...] += dyg * gh
            dgh = dyg * gog_ref[...]
            dgt = rg * (dgh - gh * _col_mean(dgh * gh))
            du = dgt * mixed
            dmix = dgt * u
            dmixb = dmix.astype(MXU)
            dbs_ref[...] += jnp.sum(_by_group(dmix, dm.ng), axis=1)
            dvn = []
            for g in range(dm.ng):
                rows = slice(g * GROUP_DIM, (g + 1) * GROUP_DIM)
                dws_ref[g] += jnp.where(tril, _dot_tn(dmixb[rows], vnb[rows]), 0.0)
                dvn.append(_dot(dmixb[rows], wts[g]))
            dvn = jnp.concatenate(dvn, axis=0)
            a_lng[...] += dvn * xh
            a_lnb[...] += dvn
            dxh3, xh3 = _by_group(dvn * lng_ref[...], dm.ng), _by_group(xh, dm.ng)
            dv = (rstd * (dxh3 - jnp.mean(dxh3, axis=1, keepdims=True) - xh3 * jnp.mean(dxh3 * xh3, axis=1, keepdims=True))).reshape(dm.gw, BLOCK)
            dgu = cu + zu * (jnp.exp(-0.5 * zu * zu) * _INV_SQRT_2PI)
            dgv = cv + zv * (jnp.exp(-0.5 * zv * zv) * _INV_SQRT_2PI)
            dproj_ref[dm.zoff:dm.zoff + dm.gw, :] = (du * dgu).astype(dproj_ref.dtype)
            dproj_ref[dm.zoff + dm.gw:, :] = (dv * dgv).astype(dproj_ref.dtype)

        @pl.when(i == nb)
        def _():
            prevpart_scr[...] = jnp.zeros_like(prevpart_scr)

        @pl.when(i >= 1)
        def _():
            tot = carry_scr[...] + prevpart_scr[...]
            for kh in range(dm.nkv):
                krows = slice(kh * dm.hd, (kh + 1) * dm.hd)
                kraw = kprev_scr[krows, :]
                rk = _col_rms(kraw)
                khat = kraw * rk
                dkn = tot[krows, :]
                a_gk[...] += dkn * khat
                dkh = dkn * gkv[:, :BLOCK]
                dk = rk * (dkh - khat * jnp.mean(dkh * khat, axis=0, keepdims=True))
                dkv_ref[krows, :] = dk.astype(dkv_ref.dtype)
            dkv_ref[dm.kvw:, :] = tot[dm.kvw:, :].astype(dkv_ref.dtype)

        @pl.when(i < nb)
        def _():
            carry_scr[...] = curpart_scr[...]
            kprev_scr[...] = p_ref[dm.aw:dm.aw + dm.kvw, :]

        @pl.when((s_id == nseq - 1) & (i == nb))
        def _():
            for acc, out in lane_accs:
                out[...] = jnp.sum(acc[...], axis=1, keepdims=True)

    col = lambda rows: jax.ShapeDtypeStruct((rows, 1), F32)
    lane = lambda rows: pltpu.VMEM((rows, LANES), F32)
    return pl.pallas_call(
        body, name="mixer_bwd", grid=(nseq, nb + 1),
        in_specs=[pl.BlockSpec((dm.inw, BLOCK), cur), pl.BlockSpec((kvw2, BLOCK), prev), pl.BlockSpec((dm.d, BLOCK), cur),
                  pl.BlockSpec((dm.nh * BLOCK, BLOCK), cur), pl.BlockSpec((dm.nh, BLOCK), cur), pl.BlockSpec((dm.aw, BLOCK), cur),
                  full(gq.shape), full(gk2.shape), pl.BlockSpec(memory_space=pltpu.SMEM),
                  full(lng.shape), full(lnb.shape), full(ws.shape), full(bs.shape), full(goa.shape), full(gog.shape)],
        out_specs=[pl.BlockSpec((dm.inw, BLOCK), cur), pl.BlockSpec((kvw2, BLOCK), prev_kv_out),
                   full((dm.hd, 1)), full((dm.hd, 1)), full((dm.nh, 1)), full((dm.gw, 1)), full((dm.gw, 1)), full(ws.shape),
                   full(bs.shape), full((dm.aw, 1)), full((dm.gw, 1))],
        out_shape=[jax.ShapeDtypeStruct((dm.inw, t), MXU), jax.ShapeDtypeStruct((kvw2, t), MXU),
                   col(dm.hd), col(dm.hd), col(dm.nh), col(dm.gw), col(dm.gw), jax.ShapeDtypeStruct(ws.shape, F32),
                   jax.ShapeDtypeStruct(bs.shape, F32), col(dm.aw), col(dm.gw)],
        scratch_shapes=[pltpu.VMEM((dm.aw, BLOCK), F32),
                        pltpu.VMEM((kvw2, BLOCK), F32), pltpu.VMEM((kvw2, BLOCK), F32), pltpu.VMEM((kvw2, BLOCK), F32),
                        pltpu.VMEM((dm.kvw, BLOCK), F32),
                        pltpu.VMEM((dm.hd, dm.grp * BLOCK), F32), lane(dm.hd), lane(dm.nh), lane(dm.gw), lane(dm.gw), lane(dm.aw), lane(dm.gw)],
        compiler_params=_cparams(("arbitrary", "arbitrary"), 48),
    )(proj_t, proj_t, dy_t, *saved, gq, gk2, sinks, lng, lnb, ws, bs, goa, gog)


def _patch_kv(dproj_t, dkv_t, dm):
    t = dproj_t.shape[1]
    tc = _pick(t, (1024, 512, 256, 128))
    kvw2 = 2 * dm.kvw
    kvblk = dm.aw // kvw2

    def body(dproj_hbm, dkv_ref, out_ref):
        del dproj_hbm
        out_ref[...] = dkv_ref[...]

    return pl.pallas_call(
        body, name="patch_kv", grid=(t // tc,),
        in_specs=[ANY, pl.BlockSpec((kvw2, tc), lambda i: (0, i))],
        out_specs=pl.BlockSpec((kvw2, tc), lambda i: (kvblk, i)),
        out_shape=jax.ShapeDtypeStruct(dproj_t.shape, dproj_t.dtype),
        input_output_aliases={0: 0},
        compiler_params=_cparams(("parallel",), 32),
    )(dproj_t, dkv_t)


def _place():
    x, y, c = lax.axis_index("x"), lax.axis_index("y"), lax.axis_index("c")
    return x, y, c


def _handshake(peers):
    barrier = pltpu.get_barrier_semaphore()
    for p in peers:
        pl.semaphore_signal(barrier, inc=1, device_id=p, device_id_type=MESH)
    pl.semaphore_wait(barrier, len(peers))


def _sequencer_mesh():
    return plsc.ScalarSubcoreMesh(axis_name="sequencer", num_cores=1)


GATHER_CHUNKS = 8
BF16_ROWS = 16


def _row_chunks(n, k):
    tiles = n // BF16_ROWS
    sizes = [(tiles // k + (1 if i < tiles % k else 0)) * BF16_ROWS for i in range(k)]
    return [(sum(sizes[:i]), sz) for i, sz in enumerate(sizes) if sz]


def _allgather_weight(name, collective_id, shard):
    n = shard.shape[0]
    assert n % BF16_ROWS == 0
    chunks = _row_chunks(n, GATHER_CHUNKS)
    nc = len(chunks)

    def body(src, out, send_sems, recv_sems, local_sem):
        x, y, c = _place()
        me, sib, xn, yn, diag = (x, y, c), (x, y, 1 - c), (1 - x, y, c), (x, 1 - y, c), (1 - x, 1 - y, c)
        relay_to = (x ^ c, y ^ (1 - c), c)
        relay_of = (x ^ (1 - c), y ^ c, c)
        _handshake([sib, xn, yn])

        def rows(place, ci):
            px, py, pc = place
            off, size = chunks[ci]
            return out.at[pl.ds(pl.multiple_of((4 * px + 2 * py + pc) * n + off, BF16_ROWS), size), :]

        def copy(k, ci, block, to, from_src=False):
            off, size = chunks[ci]
            return pltpu.make_async_remote_copy(
                src_ref=src.at[pl.ds(off, size), :] if from_src else rows(block, ci), dst_ref=rows(block, ci),
                send_sem=send_sems.at[ci, k], recv_sem=recv_sems.at[ci, k], device_id=to, device_id_type=MESH)

        mine = pltpu.make_async_copy(src, out.at[pl.ds(pl.multiple_of((4 * x + 2 * y + c) * n, BF16_ROWS), n), :], local_sem)
        mine.start()
        sent = []
        for ci in range(nc):
            sent += [copy(0, ci, me, sib, from_src=True), copy(1, ci, me, xn, from_src=True), copy(2, ci, me, yn, from_src=True)]
        for cp in sent:
            cp.start()
        for ci in range(nc):
            copy(1, ci, xn, me).wait_recv()
            copy(2, ci, yn, me).wait_recv()
            passed = [copy(3, ci, relay_of, relay_to), copy(4, ci, xn, sib), copy(5, ci, yn, sib)]
            for cp in passed:
                cp.start()
            sent += passed
        for ci in range(nc):
            copy(3, ci, diag, me).wait_recv()
            passed = copy(6, ci, diag, sib)
            passed.start()
            sent.append(passed)
        for ci in range(nc):
            copy(0, ci, sib, me).wait_recv()
            for k, block in ((4, (1 - x, y, 1 - c)), (5, (x, 1 - y, 1 - c)), (6, (1 - x, 1 - y, 1 - c))):
                copy(k, ci, block, me).wait_recv()
        for cp in sent:
            cp.wait_send()
        mine.wait()

    return pl.kernel(
        body, name=name,
        out_type=jax.ShapeDtypeStruct((N_DEV * n, shard.shape[1]), shard.dtype),
        mesh=_sequencer_mesh(),
        scratch_types=[pltpu.SemaphoreType.DMA((nc, 7)), pltpu.SemaphoreType.DMA((nc, 7)), pltpu.SemaphoreType.DMA],
        compiler_params=pltpu.CompilerParams(collective_id=collective_id),
    )(shard)


_FLIPS = [(0, 0, 1), (1, 0, 0), (0, 1, 0), (1, 1, 0), (1, 0, 1), (0, 1, 1), (1, 1, 1)]


def _scatter_grad(name, collective_id, grad):
    n = grad.shape[0] // N_DEV

    def body(src, out, send_sems, recv_sems, local_sem):
        x, y, c = _place()
        me_idx = 4 * x + 2 * y + c
        peers = [(x ^ fx, y ^ fy, c ^ fc) for (fx, fy, fc) in _FLIPS]
        _handshake(peers)

        def block(idx):
            return src.at[pl.ds(pl.multiple_of(idx * n, 16), n), :]

        copies = [pltpu.make_async_remote_copy(
            src_ref=block(4 * px + 2 * py + pc), dst_ref=out.at[me_idx], send_sem=send_sems.at[k], recv_sem=recv_sems.at[k],
            device_id=(px, py, pc), device_id_type=MESH) for k, (px, py, pc) in enumerate(peers)]
        mine = pltpu.make_async_copy(block(me_idx), out.at[me_idx], local_sem)
        mine.start()
        for cp in copies:
            cp.start()
        for cp in copies:
            cp.wait_recv()
        for cp in copies:
            cp.wait_send()
        mine.wait()

    return pl.kernel(
        body, name=name,
        out_type=jax.ShapeDtypeStruct((N_DEV, n, grad.shape[1]), grad.dtype),
        mesh=_sequencer_mesh(),
        scratch_types=[pltpu.SemaphoreType.DMA((7,)), pltpu.SemaphoreType.DMA((7,)), pltpu.SemaphoreType.DMA],
        compiler_params=pltpu.CompilerParams(collective_id=collective_id),
    )(grad)


def _allgather_rows(name, collective_id, part):
    def body(src, out, send_sems, recv_sems, local_sem):
        x, y, c = _place()
        me_idx = 4 * x + 2 * y + c
        peers = [(x ^ fx, y ^ fy, c ^ fc) for (fx, fy, fc) in _FLIPS]
        _handshake(peers)
        copies = [pltpu.make_async_remote_copy(
            src_ref=src, dst_ref=out.at[me_idx], send_sem=send_sems.at[k], recv_sem=recv_sems.at[k],
            device_id=peer, device_id_type=MESH) for k, peer in enumerate(peers)]
        mine = pltpu.make_async_copy(src, out.at[me_idx], local_sem)
        mine.start()
        for cp in copies:
            cp.start()
        for cp in copies:
            cp.wait_recv()
        for cp in copies:
            cp.wait_send()
        mine.wait()

    return pl.kernel(
        body, name=name,
        out_type=jax.ShapeDtypeStruct((N_DEV,) + part.shape, part.dtype),
        mesh=_sequencer_mesh(),
        scratch_types=[pltpu.SemaphoreType.DMA((7,)), pltpu.SemaphoreType.DMA((7,)), pltpu.SemaphoreType.DMA],
        compiler_params=pltpu.CompilerParams(collective_id=collective_id),
    )(part)


def _adamw_math(w, g, m, v):
    m = ADAM_B1 * m + (1.0 - ADAM_B1) * g
    v = ADAM_B2 * v + (1.0 - ADAM_B2) * (g * g)
    m_hat = m / (1.0 - ADAM_B1 ** ADAM_STEP)
    v_hat = v / (1.0 - ADAM_B2 ** ADAM_STEP)
    delta = -ADAM_LR * (m_hat / (jnp.sqrt(v_hat) + ADAM_EPS) + ADAM_WD * w)
    return delta, m, v


def _sum_adamw(name, slots, w, m, v, after):
    _, n, kk = slots.shape
    tr = _pick(n, (208, 176, 128, 96, 64, 32, 16))

    def body(s_ref, w_ref, m_ref, v_ref, after_ref, g_ref, d_ref, nm_ref, nv_ref):
        del after_ref
        g = s_ref[0].astype(F32)
        for p in range(1, N_DEV):
            g = g + s_ref[p].astype(F32)
        g_ref[...] = g
        d_ref[...], nm_ref[...], nv_ref[...] = _adamw_math(w_ref[...], g, m_ref[...], v_ref[...])

    row = pl.BlockSpec((tr, kk), lambda i: (i, 0))
    return pl.pallas_call(
        body, name=name, grid=(n // tr,),
        in_specs=[pl.BlockSpec((N_DEV, tr, kk), lambda i: (0, i, 0)), row, row, row, ANY],
        out_specs=[row] * 4,
        out_shape=[jax.ShapeDtypeStruct((n, kk), F32)] * 4,
        compiler_params=_cparams(("parallel",), 48),
    )(slots, w, m, v, after)


def _allreduce_small_adamw(early_slots, late, w, m, v, after):
    ra, rb = early_slots.shape[1], late.shape[0]

    def body(early_ref, late_ref, w_ref, m_ref, v_ref, after_ref, g_ref, d_ref, nm_ref, nv_ref, slots, send_sems, recv_sems):
        del after_ref
        x, y, c = _place()
        me_idx = 4 * x + 2 * y + c
        copies = []
        for k, (fx, fy, fc) in enumerate(_FLIPS):
            px, py, pc = x ^ fx, y ^ fy, c ^ fc
            copies.append(pltpu.make_async_remote_copy(
                src_ref=late_ref, dst_ref=slots.at[me_idx], send_sem=send_sems.at[k], recv_sem=recv_sems.at[k],
                device_id=(px, py, pc), device_id_type=MESH))
        for cp in copies:
            cp.start()
        slots[me_idx] = late_ref[...]
        g = early_ref[0]
        for p in range(1, N_DEV):
            g = g + early_ref[p]
        early = pl.ds(0, ra)
        g_ref[early, :] = g
        d_ref[early, :], nm_ref[early, :], nv_ref[early, :] = _adamw_math(w_ref[early, :], g, m_ref[early, :], v_ref[early, :])
        for cp in copies:
            cp.wait_recv()
        for cp in copies:
            cp.wait_send()
        g = slots[0]
        for p in range(1, N_DEV):
            g = g + slots[p]
        tail = pl.ds(ra, rb)
        g_ref[tail, :] = g
        d_ref[tail, :], nm_ref[tail, :], nv_ref[tail, :] = _adamw_math(w_ref[tail, :], g, m_ref[tail, :], v_ref[tail, :])

    vm = pl.BlockSpec(memory_space=pltpu.VMEM)
    return pl.pallas_call(
        body, name="allreduce_small_adamw",
        in_specs=[vm] * 5 + [ANY], out_specs=[vm] * 4,
        out_shape=[jax.ShapeDtypeStruct((ra + rb, LANES), F32)] * 4,
        scratch_shapes=[pltpu.VMEM((N_DEV, rb, LANES), F32), pltpu.SemaphoreType.DMA((7,)), pltpu.SemaphoreType.DMA((7,))],
        compiler_params=pltpu.CompilerParams(vmem_limit_bytes=48 * MIB),
    )(early_slots, late, w, m, v, after)


def _pack(arrs):
    parts, meta, off = [], [], 0
    for a in arrs:
        flat = a.reshape(-1).astype(F32)
        rows = -(-flat.shape[0] // LANES)
        rows8 = -(-rows // 8) * 8
        flat = jnp.pad(flat, (0, rows8 * LANES - flat.shape[0]))
        parts.append(flat.reshape(rows8, LANES))
        meta.append((off, a.shape, a.size))
        off += rows8
    return jnp.concatenate(parts, axis=0), meta


def _unpack(packed, meta):
    outs = []
    for off, shape, size in meta:
        rows = -(-size // LANES)
        outs.append(packed[off:off + rows].reshape(-1)[:size].reshape(shape))
    return outs


def _silu_parts(a):
    sg = 0.5 + 0.5 * jnp.tanh(0.5 * a)
    return a * sg, sg * (1.0 + a * (1.0 - sg))


def kernel(x, norm1_g, w_in, q_norm_g, k_norm_g, attn_sinks, gate_ln_g, gate_ln_b, w_spatial, b_spatial, out_norm_attn_g, out_norm_gate_g, w_out, norm2_g, w_ffn_gate, w_ffn_up, w_ffn_down, loss_target, m_norm1_g, m_w_in, m_q_norm_g, m_k_norm_g, m_attn_sinks, m_gate_ln_g, m_gate_ln_b, m_w_spatial, m_b_spatial, m_out_norm_attn_g, m_out_norm_gate_g, m_w_out, m_norm2_g, m_w_ffn_gate, m_w_ffn_up, m_w_ffn_down, v_norm1_g, v_w_in, v_q_norm_g, v_k_norm_g, v_attn_sinks, v_gate_ln_g, v_gate_ln_b, v_w_spatial, v_b_spatial, v_out_norm_attn_g, v_out_norm_gate_g, v_w_out, v_norm2_g, v_w_ffn_gate, v_w_ffn_up, v_w_ffn_down):
    nseq, seq, d = x.shape
    t = nseq * seq
    nb = seq // BLOCK
    inw = w_in.shape[2] * N_DEV
    dm = _Dims(d, inw, q_norm_g.shape[-1])
    xf = x.reshape(t, d)
    tgt = loss_target.reshape(t, d)

    rows = lambda wv, transposed: jnp.swapaxes(wv, 1, 2)[0] if transposed else wv[0]
    big = {"w_in": (w_in, m_w_in, v_w_in, True), "w_out": (w_out, m_w_out, v_w_out, False),
           "w_ffn_gate": (w_ffn_gate, m_w_ffn_gate, v_w_ffn_gate, True), "w_ffn_up": (w_ffn_up, m_w_ffn_up, v_w_ffn_up, True),
           "w_ffn_down": (w_ffn_down, m_w_ffn_down, v_w_ffn_down, False)}
    big_rows = {nm: tuple(rows(arr, tr) for arr in (wv, mv, vv)) for nm, (wv, mv, vv, tr) in big.items()}
    shard = lambda nm: big_rows[nm][0].astype(WIRE)
    win_t = _allgather_weight("gather_w_in", 1, shard("w_in"))
    wout = _allgather_weight("gather_w_out", 2, shard("w_out"))
    wg_t = _allgather_weight("gather_w_ffn_gate", 3, shard("w_ffn_gate"))
    wu_t = _allgather_weight("gather_w_ffn_up", 9, shard("w_ffn_up"))
    wd = _allgather_weight("gather_w_ffn_down", 10, shard("w_ffn_down"))

    lanes = lambda v, n=BLOCK: jnp.broadcast_to(v.reshape(-1, 1), (v.size, n))
    prm = (lanes(q_norm_g, dm.grp * BLOCK), lanes(k_norm_g, 2 * BLOCK), attn_sinks[0], lanes(gate_ln_g), lanes(gate_ln_b), w_spatial[0], b_spatial[0],
           lanes(out_norm_attn_g), lanes(out_norm_gate_g))

    h1 = _rms_fwd("rms1_fwd", xf, norm1_g)
    (proj_t,) = _matmul("mm_in", win_t, h1, "nt", [F32])
    y_t, *attn_saved = _mixer_fwd(proj_t, prm, dm, nseq, nb)

    def residual_norm(acc, xr, g2):
        x2v = xr + acc
        return x2v, x2v * lax.rsqrt(jnp.mean(x2v * x2v, axis=-1, keepdims=True) + EPS) * g2

    x2, h2 = _matmul("mm_out", y_t, wout, "tn", [F32, MXU], epilogue=residual_norm, extras=[xf], rowvecs=[norm2_g], full_rows=True)
    silu_a, dsilu_a = _matmul("mm_gate", h2, wg_t, "nt", [MXU, MXU], epilogue=_silu_parts)
    s, dsilu_a_b = _matmul("mm_up", h2, wu_t, "nt", [MXU, MXU], epilogue=lambda ub, sv, dv: (sv.astype(F32) * ub, dv.astype(F32) * ub),
                           extras=[silu_a, dsilu_a])

    def loss_epilogue(acc, x2v, tv):
        diff = (x2v + acc) - tv
        dx3 = diff * (1.0 / d)
        return dx3, dx3, jnp.sum(diff * diff)

    dx3, dx3b, lossp = _matmul("mm_down", s, wd, "nn", [F32, MXU], epilogue=loss_epilogue, extras=[x2, tgt], partial=True)
    loss_part = (0.5 / d) * jnp.sum(lossp[::8, ::LANES])

    def dswiglu(ds, dsilu_b, silu):
        return ds * dsilu_b.astype(F32), ds * silu.astype(F32)

    (g_wd,) = _matmul("mm_gw_down", s, dx3b, "tn", [WIRE])
    sl_wd = _scatter_grad("scatter_w_ffn_down", 4, g_wd)
    da, db = _matmul("mm_d_down", dx3b, wd, "nt", [MXU, MXU], epilogue=dswiglu, extras=[dsilu_a_b, silu_a], after=[g_wd])
    (g_wg,) = _matmul("mm_gw_gate", da, h2, "tn", [WIRE])
    sl_wg = _scatter_grad("scatter_w_ffn_gate", 5, g_wg)
    (g_wu,) = _matmul("mm_gw_up", db, h2, "tn", [WIRE], after=[g_wg])
    sl_wu = _scatter_grad("scatter_w_ffn_up", 6, g_wu)
    (dh2a,) = _matmul("mm_dh2_gate", da, wg_t, "nn", [F32], after=[g_wu])
    (dh2,) = _matmul("mm_dh2_up", db, wu_t, "nn", [F32], epilogue=lambda acc, pv: (pv + acc,), extras=[dh2a])

    dy_t, dx2, dx2b, dg2 = _norm_bwd_matmul("mm_d_out", wout, dh2, x2, norm2_g, dx3, after=dh2a)
    (g_wout,) = _matmul("mm_gw_out", y_t, dx2b, "nn", [WIRE], after=[dy_t])
    sl_wout = _scatter_grad("scatter_w_out", 7, g_wout)
    (dproj0, dkv, dgq, dgk, dsink, dlng, dlnb, dws, dbs, dgoa, dgog) = _mixer_bwd(proj_t, dy_t, attn_saved, prm, dm, nseq, nb)
    early_g = [dgq, dgk, dsink, dlng, dlnb, dws, dbs, dgoa, dgog, dg2, loss_part.reshape(1)]
    early_slots = _allgather_rows("gather_small_grads", 11, _pack(early_g)[0])
    dproj_t = _patch_kv(dproj0, dkv, dm)
    (g_win,) = _matmul("mm_gw_in", dproj_t, h1, "nn", [WIRE])
    sl_win = _scatter_grad("scatter_w_in", 8, g_win)

    def norm1_backward(dh1, xv, dx2v, g1):
        r = lax.rsqrt(jnp.mean(xv * xv, axis=-1, keepdims=True) + EPS)
        xh = xv * r
        dxh = dh1 * g1
        return dx2v + r * (dxh - xh * jnp.mean(dxh * xh, axis=-1, keepdims=True)), jnp.sum(dh1 * xh, axis=0, keepdims=True)

    dx, dg1 = _matmul("mm_d_in", dproj_t, win_t, "tn", [F32], epilogue=norm1_backward, extras=[xf, dx2], rowvecs=[norm1_g],
                      after=[g_win], col_sum=True, full_rows=True)

    big_out = {}
    last = dx

    def big_update(nm, sl, after):
        res = _sum_adamw("adamw_" + nm, sl, *big_rows[nm], after=after)
        big_out[nm] = tuple(jnp.swapaxes(r[None], 1, 2) if big[nm][3] else r[None] for r in res)
        return res[1]

    for nm, sl in (("w_ffn_down", sl_wd), ("w_ffn_gate", sl_wg), ("w_ffn_up", sl_wu), ("w_out", sl_wout)):
        last = big_update(nm, sl, last)

    zero = jnp.zeros((1,), F32)
    small_names = ["q_norm_g", "k_norm_g", "attn_sinks", "gate_ln_g", "gate_ln_b", "w_spatial", "b_spatial",
                   "out_norm_attn_g", "out_norm_gate_g", "norm2_g", "loss", "norm1_g"]
    small_w = [q_norm_g, k_norm_g, attn_sinks, gate_ln_g, gate_ln_b, w_spatial, b_spatial, out_norm_attn_g, out_norm_gate_g, norm2_g, zero, norm1_g]
    small_m = [m_q_norm_g, m_k_norm_g, m_attn_sinks, m_gate_ln_g, m_gate_ln_b, m_w_spatial, m_b_spatial, m_out_norm_attn_g, m_out_norm_gate_g, m_norm2_g, zero, m_norm1_g]
    small_v = [v_q_norm_g, v_k_norm_g, v_attn_sinks, v_gate_ln_g, v_gate_ln_b, v_w_spatial, v_b_spatial, v_out_norm_attn_g, v_out_norm_gate_g, v_norm2_g, zero, v_norm1_g]
    pw, meta = _pack(small_w)
    sg, sd, sm, sv = _allreduce_small_adamw(early_slots, _pack([dg1])[0], pw, _pack(small_m)[0], _pack(small_v)[0], after=last)
    big_update("w_in", sl_win, sd)
    ug, ud, um, uv = _unpack(sg, meta), _unpack(sd, meta), _unpack(sm, meta), _unpack(sv, meta)
    small_out = {nm: (ug[k], ud[k], um[k], uv[k]) for k, nm in enumerate(small_names)}
    loss = small_out["loss"][0].reshape(())

    order = ["norm1_g", "w_in", "q_norm_g", "k_norm_g", "attn_sinks", "gate_ln_g", "gate_ln_b", "w_spatial", "b_spatial",
             "out_norm_attn_g", "out_norm_gate_g", "w_out", "norm2_g", "w_ffn_gate", "w_ffn_up", "w_ffn_down"]
    allo = {**big_out, **small_out}
    outs = [loss, dx.reshape(nseq, seq, d)]
    for k in range(4):
        outs += [allo[nm][k] for nm in order]
    return tuple(outs)
```

```python
import math

import jax
import jax.numpy as jnp
from jax import lax
from jax.experimental import pallas as pl
from jax.experimental.pallas import tpu as pltpu
from jax.experimental.pallas import tpu_sc as plsc

F32 = jnp.float32
MXU = jnp.bfloat16
WIRE = jnp.bfloat16
EPS = 1e-6
BLOCK = 128
GROUP_DIM = 128
N_KV_HEADS = 2
NEG = -1e30
N_DEV = 8
LANES = 128
MIB = 1024 * 1024

ADAM_LR = 0.001
ADAM_B1 = 0.9
ADAM_B2 = 0.999
ADAM_EPS = 1e-08
ADAM_WD = 0.01
ADAM_STEP = 10

MESH = pl.DeviceIdType.MESH
ANY = pl.BlockSpec(memory_space=pl.ANY)


def _pick(n, cands):
    for c in cands:
        if n % c == 0:
            return c
    return n


def _cparams(sem, vmem_mb):
    return pltpu.CompilerParams(dimension_semantics=sem, vmem_limit_bytes=vmem_mb * MIB)


VMEM_TILE_BUDGET = 50 * MIB
HBM_BYTES_PER_US = 3.0e6
STEP_US = 0.4
MXU_COLS = 256
MIN_TILE_N = 2 * MXU_COLS


def _tile_candidates(n):
    return [c for c in range(min(n, 2048), 0, -LANES) if n % c == 0 and c % LANES == 0] or [n]


def _matmul_tiles(m, n, kk, esz, extra_sizes, out_sizes, full_rows):
    best = None
    wide = [n] if full_rows else [c for c in _tile_candidates(n) if c >= MIN_TILE_N and c % MXU_COLS == 0] or _tile_candidates(n)
    for tm in _tile_candidates(m):
        for tn in wide:
            b_buffers = 1 if full_rows else 2
            temps = 2 if full_rows else 0
            vmem = (2 * tm + b_buffers * tn) * kk * esz + tm * tn * (4 * (1 + temps) + 2 * sum(extra_sizes) + 2 * sum(out_sizes))
            if vmem > VMEM_TILE_BUDGET:
                continue
            cost = (m // tm) * n * kk * esz / HBM_BYTES_PER_US + (m // tm) * (n // tn) * STEP_US
            if best is None or cost < best[0]:
                best = (cost, tm, tn, vmem)
    assert best is not None, (m, n, kk)
    return best[1:]


def _matmul(name, a, b, mode, out_dtypes, epilogue=None, extras=(), rowvecs=(), after=(), partial=False, col_sum=False, full_rows=False):
    if mode == "nn":
        (m, kk), n = a.shape, b.shape[1]
        dn = (((1,), (0,)), ((), ()))
    elif mode == "nt":
        (m, kk), n = a.shape, b.shape[0]
        dn = (((1,), (1,)), ((), ()))
    else:
        (kk, m), n = a.shape, b.shape[1]
        dn = (((0,), (0,)), ((), ()))
    tm, tn, vmem = _matmul_tiles(m, n, kk, a.dtype.itemsize, [e.dtype.itemsize for e in extras],
                                 [jnp.dtype(dt).itemsize for dt in out_dtypes], full_rows)
    a_spec = pl.BlockSpec((kk, tm), lambda i, j: (0, i)) if mode == "tn" else pl.BlockSpec((tm, kk), lambda i, j: (i, 0))
    resident = dict(pipeline_mode=pl.Buffered(1)) if full_rows else {}
    b_spec = pl.BlockSpec((tn, kk), lambda i, j: (j, 0), **resident) if mode == "nt" else pl.BlockSpec((kk, tn), lambda i, j: (0, j), **resident)
    tile = pl.BlockSpec((tm, tn), lambda i, j: (i, j))
    row = pl.BlockSpec((1, tn), lambda i, j: (0, j))
    ne, nr, na, no = len(extras), len(rowvecs), len(after), len(out_dtypes)

    def body(a_ref, b_ref, *rest):
        in_refs, out_refs = rest[:ne + nr], rest[ne + nr + na:]
        acc = lax.dot_general(a_ref[...], b_ref[...], dn, preferred_element_type=F32)
        vals = (acc,) if epilogue is None else epilogue(acc, *[r[...] for r in in_refs])
        for o_ref, t in zip(out_refs[:no], vals[:no]):
            o_ref[...] = t.astype(o_ref.dtype)
        if partial:
            out_refs[no][...] = jnp.full((8, LANES), vals[no], F32)
        if col_sum:
            sum_ref = out_refs[-1]

            @pl.when(pl.program_id(0) == 0)
            def _():
                sum_ref[...] = jnp.zeros_like(sum_ref)

            sum_ref[...] += vals[-1]

    out_specs = [tile] * no
    out_shape = [jax.ShapeDtypeStruct((m, n), dt) for dt in out_dtypes]
    if partial:
        out_specs.append(pl.BlockSpec((8, LANES), lambda i, j: (i, j)))
        out_shape.append(jax.ShapeDtypeStruct((m // tm * 8, n // tn * LANES), F32))
    if col_sum:
        out_specs.append(row)
        out_shape.append(jax.ShapeDtypeStruct((1, n), F32))
    return pl.pallas_call(
        body, name=name, grid=(m // tm, n // tn),
        in_specs=[a_spec, b_spec] + [tile] * ne + [row] * nr + [ANY] * na,
        out_specs=out_specs, out_shape=out_shape,
        compiler_params=_cparams(("arbitrary" if col_sum else "parallel", "arbitrary"), min(vmem // MIB + 8, 58)),
    )(a, b, *extras, *rowvecs, *after)


def _rms_fwd(name, x, g):
    t, d = x.shape
    tm = _pick(t, (512, 256, 128))

    def body(x_ref, g_ref, h_ref):
        xv = x_ref[...]
        r = lax.rsqrt(jnp.mean(xv * xv, axis=-1, keepdims=True) + EPS)
        h_ref[...] = (xv * r * g_ref[...]).astype(h_ref.dtype)

    return pl.pallas_call(
        body, name=name, grid=(t // tm,),
        in_specs=[pl.BlockSpec((tm, d), lambda i: (i, 0)), pl.BlockSpec((1, d), lambda i: (0, 0))],
        out_specs=pl.BlockSpec((tm, d), lambda i: (i, 0)),
        out_shape=jax.ShapeDtypeStruct((t, d), MXU),
        compiler_params=_cparams(("parallel",), 32),
    )(x, g)


def _norm_bwd_matmul(name, w, dh, x, g, res, after):
    t, d = x.shape
    m = w.shape[0]
    tn = _pick(t, (256, 128))

    def body(w_ref, dh_ref, x_ref, g_ref, res_ref, after_ref, out_ref, dx_ref, dxb_ref, dg_ref):
        del after_ref

        @pl.when(pl.program_id(0) == 0)
        def _():
            dg_ref[...] = jnp.zeros_like(dg_ref)

        xv, dhv = x_ref[...], dh_ref[...]
        r = lax.rsqrt(jnp.mean(xv * xv, axis=-1, keepdims=True) + EPS)
        xh = xv * r
        dg_ref[...] += jnp.sum(dhv * xh, axis=0, keepdims=True)
        dxh = dhv * g_ref[...]
        dx = res_ref[...] + r * (dxh - xh * jnp.mean(dxh * xh, axis=-1, keepdims=True))
        dx_ref[...] = dx
        dxb = dx.astype(MXU)
        dxb_ref[...] = dxb
        out_ref[...] = lax.dot_general(w_ref[...], dxb, (((1,), (1,)), ((), ())), preferred_element_type=F32)

    row = pl.BlockSpec((tn, d), lambda j: (j, 0))
    vec = pl.BlockSpec((1, d), lambda j: (0, 0))
    return pl.pallas_call(
        body, name=name, grid=(t // tn,),
        in_specs=[pl.BlockSpec((m, d), lambda j: (0, 0), pipeline_mode=pl.Buffered(1)), row, row, vec, row, ANY],
        out_specs=[pl.BlockSpec((m, tn), lambda j: (0, j)), row, row, vec],
        out_shape=[jax.ShapeDtypeStruct((m, t), F32), jax.ShapeDtypeStruct((t, d), F32), jax.ShapeDtypeStruct((t, d), MXU),
                   jax.ShapeDtypeStruct((1, d), F32)],
        compiler_params=_cparams(("arbitrary",), 52),
    )(w, dh, x, g, res, after)


_INV_SQRT2 = 0.7071067811865476
_INV_SQRT_2PI = 0.3989422804014327


def _dot_nt(a, b):
    return lax.dot_general(a, b, (((1,), (1,)), ((), ())), preferred_element_type=F32)


def _dot_tn(a, b):
    return lax.dot_general(a, b, (((0,), (0,)), ((), ())), preferred_element_type=F32)


def _dot(a, b):
    return jnp.dot(a, b, preferred_element_type=F32)


def _col_mean(v):
    rows = v.shape[0]
    if rows > GROUP_DIM and rows % GROUP_DIM == 0:
        v = jnp.sum(v.reshape(rows // GROUP_DIM, GROUP_DIM, v.shape[1]), axis=0)
    return jnp.sum(v, axis=0, keepdims=True) * (1.0 / rows)


def _col_rms(v):
    return lax.rsqrt(_col_mean(v * v) + EPS)


class _Dims:
    def __init__(self, d_model, in_width, head_dim):
        self.d = d_model
        self.aw = d_model // 2
        self.gw = d_model - self.aw
        self.kvw = (in_width - self.aw - 2 * self.gw) // 2
        self.hd = head_dim
        self.nh = self.aw // head_dim
        self.nkv = self.kvw // head_dim
        self.grp = self.nh // self.nkv
        self.ng = self.gw // GROUP_DIM
        self.inw = in_width
        self.zoff = self.aw + 2 * self.kvw
        assert self.nkv == N_KV_HEADS and self.zoff + 2 * self.gw == in_width and self.aw % (2 * self.kvw) == 0


def _band_masks(first):
    r = lax.broadcasted_iota(jnp.int32, (BLOCK, BLOCK), 0)
    t = lax.broadcasted_iota(jnp.int32, (BLOCK, BLOCK), 1)
    upper = r > t
    dist = t - r + jnp.where(upper, BLOCK, 0)
    return upper, jnp.logical_not(upper & first), dist.astype(F32)


def _fold(full, upper):
    return jnp.where(upper, full[:BLOCK], full[BLOCK:])


def _unfold(folded, upper):
    zero = jnp.zeros_like(folded)
    return jnp.concatenate([jnp.where(upper, folded, zero), jnp.where(upper, zero, folded)], axis=0)


def _kv_band(dm, kh, p_ref, pkv_ref, gk2):
    ko = dm.aw + kh * dm.hd
    vo = dm.aw + dm.kvw + kh * dm.hd
    k_t = jnp.concatenate([pkv_ref[kh * dm.hd:(kh + 1) * dm.hd, :], p_ref[ko:ko + dm.hd, :]], axis=1)
    v_t = jnp.concatenate([pkv_ref[dm.kvw + kh * dm.hd:dm.kvw + (kh + 1) * dm.hd, :], p_ref[vo:vo + dm.hd, :]], axis=1)
    kn_t = k_t * _col_rms(k_t) * gk2
    return kn_t.astype(MXU), kn_t.T.astype(MXU), v_t.astype(MXU), v_t.T.astype(MXU)


def _group_heads(dm, kh):
    return range(kh * dm.grp, (kh + 1) * dm.grp)


def _group_queries(dm, kh, p_ref, gq):
    q = jnp.concatenate([p_ref[h * dm.hd:(h + 1) * dm.hd, :] for h in _group_heads(dm, kh)], axis=1)
    rq = _col_rms(q)
    qh = q * rq
    return rq, qh, (qh * gq).astype(MXU)


def _attn_group_fwd(dm, kh, p_ref, gq, kn, v_tb, sink_ref, masks):
    heads = _group_heads(dm, kh)
    rq, qh, qnb = _group_queries(dm, kh, p_ref, gq)
    upper, valid, dist = masks
    s = _dot(kn, qnb)
    probs, probs_b, sink_probs = [], [], []
    for g, h in enumerate(heads):
        slope, sink = math.pow(2.0, -8.0 * (h + 1) / dm.nh), sink_ref[h]
        logits = jnp.where(valid, _fold(s[:, g * BLOCK:(g + 1) * BLOCK], upper) * (dm.hd ** -0.5) - slope * dist, NEG)
        m = jnp.maximum(jnp.max(logits, axis=0, keepdims=True), sink)
        e = jnp.exp(logits - m)
        es = jnp.exp(sink - m)
        inv = 1.0 / (jnp.sum(e, axis=0, keepdims=True) + es)
        probs.append(e * inv)
        probs_b.append(_unfold(probs[g], upper).astype(MXU))
        sink_probs.append(es * inv)
    probs_b = jnp.concatenate(probs_b, axis=1)
    o = _dot(v_tb, probs_b)
    return o, probs, probs_b, sink_probs, rq, qh, qnb


def _gelu_cdf(z):
    return 0.5 * (1.0 + lax.erf(z * _INV_SQRT2))


def _by_group(v, ng):
    return v.reshape(ng, GROUP_DIM, v.shape[1])


def _gate_fwd(dm, p_ref, lng_ref, lnb_ref, ws_ref, bs_ref, tril):
    zu, zv = p_ref[dm.zoff:dm.zoff + dm.gw, :], p_ref[dm.zoff + dm.gw:dm.zoff + 2 * dm.gw, :]
    cu, cv = _gelu_cdf(zu), _gelu_cdf(zv)
    u, v = zu * cu, zv * cv
    v3 = _by_group(v, dm.ng)
    xc = v3 - jnp.mean(v3, axis=1, keepdims=True)
    rstd = lax.rsqrt(jnp.mean(xc * xc, axis=1, keepdims=True) + EPS)
    xh = (xc * rstd).reshape(dm.gw, BLOCK)
    vnb = (xh * lng_ref[...] + lnb_ref[...]).astype(MXU)
    wts = [jnp.where(tril, ws_ref[g], 0.0).astype(MXU) for g in range(dm.ng)]
    mixed = jnp.concatenate([_dot_nt(vnb[g * GROUP_DIM:(g + 1) * GROUP_DIM], wts[g]) + bs_ref[g:g + 1, :]
                             for g in range(dm.ng)], axis=0)
    return u * mixed, u, mixed, xh, rstd, vnb, wts, (zu, cu), (zv, cv)


def _mixer_specs(dm, nb, clamp):
    kvblk = dm.aw // (2 * dm.kvw)

    def cur(s, i):
        return (0, s * nb + clamp(i))

    def prev(s, i):
        return (kvblk, s * nb + jnp.maximum(clamp(i) - 1, 0))

    full = lambda shape: pl.BlockSpec(shape, lambda s, i: tuple(0 for _ in shape))
    return cur, prev, full


def _tril():
    return lax.broadcasted_iota(jnp.int32, (BLOCK, BLOCK), 0) >= lax.broadcasted_iota(jnp.int32, (BLOCK, BLOCK), 1)


def _mixer_fwd(proj_t, prm, dm, nseq, nb):
    gq, gk2, sinks, lng, lnb, ws, bs, goa, gog = prm
    t = proj_t.shape[1]
    cur, prev, full = _mixer_specs(dm, nb, lambda i: i)

    def body(p_ref, pkv_ref, gq_ref, gk_ref, sink_ref, lng_ref, lnb_ref, ws_ref, bs_ref, goa_ref, gog_ref,
             y_ref, probs_ref, sink_probs_ref, att_ref):
        i = pl.program_id(1)
        masks = _band_masks(i == 0)
        gqv, gkv = gq_ref[...], gk_ref[...]
        for kh in range(dm.nkv):
            _, kn, v_tb, _ = _kv_band(dm, kh, p_ref, pkv_ref, gkv)
            o, probs, _, sink_probs, _, _, _ = _attn_group_fwd(dm, kh, p_ref, gqv, kn, v_tb, sink_ref, masks)
            for g, h in enumerate(_group_heads(dm, kh)):
                att_ref[h * dm.hd:(h + 1) * dm.hd, :] = o[:, g * BLOCK:(g + 1) * BLOCK]
                probs_ref[h * BLOCK:(h + 1) * BLOCK, :] = probs[g]
                sink_probs_ref[h:h + 1, :] = sink_probs[g]
        att = att_ref[...]
        y_ref[:dm.aw, :] = (att * _col_rms(att) * goa_ref[...]).astype(y_ref.dtype)
        gt = _gate_fwd(dm, p_ref, lng_ref, lnb_ref, ws_ref, bs_ref, _tril())[0]
        y_ref[dm.aw:, :] = (gt * _col_rms(gt) * gog_ref[...]).astype(y_ref.dtype)

    return pl.pallas_call(
        body, name="mixer_fwd", grid=(nseq, nb),
        in_specs=[pl.BlockSpec((dm.inw, BLOCK), cur), pl.BlockSpec((2 * dm.kvw, BLOCK), prev),
                  full(gq.shape), full(gk2.shape), pl.BlockSpec(memory_space=pltpu.SMEM),
                  full(lng.shape), full(lnb.shape), full(ws.shape), full(bs.shape), full(goa.shape), full(gog.shape)],
        out_specs=[pl.BlockSpec((dm.d, BLOCK), cur), pl.BlockSpec((dm.nh * BLOCK, BLOCK), cur), pl.BlockSpec((dm.nh, BLOCK), cur),
                   pl.BlockSpec((dm.aw, BLOCK), cur)],
        out_shape=[jax.ShapeDtypeStruct((dm.d, t), MXU), jax.ShapeDtypeStruct((dm.nh * BLOCK, t), F32),
                   jax.ShapeDtypeStruct((dm.nh, t), F32), jax.ShapeDtypeStruct((dm.aw, t), F32)],
        compiler_params=_cparams(("parallel", "arbitrary"), 40),
    )(proj_t, proj_t, gq, gk2, sinks, lng, lnb, ws, bs, goa, gog)


def _mixer_bwd(proj_t, dy_t, saved, prm, dm, nseq, nb):
    gq, gk2, sinks, lng, lnb, ws, bs, goa, gog = prm
    t = proj_t.shape[1]
    clamp = lambda i: jnp.minimum(i, nb - 1)
    cur, prev, full = _mixer_specs(dm, nb, clamp)
    kvw2 = 2 * dm.kvw

    def prev_kv_out(s, i):
        return (0, s * nb + jnp.maximum(i - 1, 0))

    def body(p_ref, pkv_ref, dy_ref, probs_ref, sink_probs_ref, att_ref, gq_ref, gk_ref, sink_ref, lng_ref, lnb_ref, ws_ref, bs_ref,
             goa_ref, gog_ref, dproj_ref, dkv_ref, dgq_ref, dgk_ref, dsink_ref, dlng_ref, dlnb_ref, dws_ref, dbs_ref, dgoa_ref, dgog_ref,
             datt_scr, carry_scr, prevpart_scr, curpart_scr, kprev_scr,
             a_gq, a_gk, a_sink, a_lng, a_lnb, a_goa, a_gog):
        s_id, i = pl.program_id(0), pl.program_id(1)
        lane_accs = ((a_gq, dgq_ref), (a_gk, dgk_ref), (a_sink, dsink_ref), (a_lng, dlng_ref), (a_lnb, dlnb_ref),
                     (a_goa, dgoa_ref), (a_gog, dgog_ref))

        @pl.when((s_id == 0) & (i == 0))
        def _():
            for acc, _ in lane_accs:
                acc[...] = jnp.zeros_like(acc)
            dws_ref[...] = jnp.zeros_like(dws_ref)
            dbs_ref[...] = jnp.zeros_like(dbs_ref)

        gqv, gkv = gq_ref[...], gk_ref[...]

        @pl.when(i < nb)
        def _():
            masks = _band_masks(i == 0)
            upper = masks[0]
            att = att_ref[...]
            dya = dy_ref[:dm.aw, :]
            ra = _col_rms(att)
            ah = att * ra
            a_goa[...] += dya * ah
            dah = dya * goa_ref[...]
            datt_scr[...] = ra * (dah - ah * _col_mean(dah * ah))
            for kh in range(dm.nkv):
                kn_tb, _, _, vb = _kv_band(dm, kh, p_ref, pkv_ref, gkv)
                rq, qh, qnb = _group_queries(dm, kh, p_ref, gqv)
                heads = _group_heads(dm, kh)
                probs = [probs_ref[h * BLOCK:(h + 1) * BLOCK, :] for h in heads]
                probs_b = jnp.concatenate([_unfold(p, upper).astype(MXU) for p in probs], axis=1)
                do_b = jnp.concatenate([datt_scr[h * dm.hd:(h + 1) * dm.hd, :] for h in heads], axis=1).astype(MXU)
                dp = _dot(vb, do_b)
                ds = []
                for g, h in enumerate(heads):
                    p, dp_h = probs[g], _fold(dp[:, g * BLOCK:(g + 1) * BLOCK], upper)
                    delta = jnp.sum(p * dp_h, axis=0, keepdims=True)
                    ds.append(_unfold(p * (dp_h - delta) * (dm.hd ** -0.5), upper).astype(MXU))
                    a_sink[h:h + 1, :] += -(sink_probs_ref[h:h + 1, :] * delta)
                dsb = jnp.concatenate(ds, axis=1)
                dqn = _dot(kn_tb, dsb)
                dkn = _dot_nt(qnb, dsb)
                dvb = _dot_nt(do_b, probs_b)
                a_gq[...] += dqn * qh
                dqh = dqn * gqv
                dq = rq * (dqh - qh * jnp.mean(dqh * qh, axis=0, keepdims=True))
                for g, h in enumerate(heads):
                    dproj_ref[h * dm.hd:(h + 1) * dm.hd, :] = dq[:, g * BLOCK:(g + 1) * BLOCK].astype(dproj_ref.dtype)
                krows = slice(kh * dm.hd, (kh + 1) * dm.hd)
                vrows = slice(dm.kvw + kh * dm.hd, dm.kvw + (kh + 1) * dm.hd)
                prevpart_scr[krows, :] = dkn[:, :BLOCK]
                prevpart_scr[vrows, :] = dvb[:, :BLOCK]
                curpart_scr[krows, :] = dkn[:, BLOCK:]
                curpart_scr[vrows, :] = dvb[:, BLOCK:]
            dproj_ref[dm.aw:dm.zoff, :] = jnp.zeros((kvw2, BLOCK), dproj_ref.dtype)
            tril = _tril()
            gt, u, mixed, xh, rstd, vnb, wts, (zu, cu), (zv, cv) = _gate_fwd(dm, p_ref, lng_ref, lnb_ref, ws_ref, bs_ref, tril)
            dyg = dy_ref[dm.aw:, :]
            rg = _col_rms(gt)
            gh = gt * rg
            a_gog[...] += dyg * gh
            dgh = dyg * gog_ref[...]
            dgt = rg * (dgh - gh * _col_mean(dgh * gh))
            du = dgt * mixed
            dmix = dgt * u
            dmixb = dmix.astype(MXU)
            dbs_ref[...] += jnp.sum(_by_group(dmix, dm.ng), axis=1)
            dvn = []
            for g in range(dm.ng):
                rows = slice(g * GROUP_DIM, (g + 1) * GROUP_DIM)
                dws_ref[g] += jnp.where(tril, _dot_tn(dmixb[rows], vnb[rows]), 0.0)
                dvn.append(_dot(dmixb[rows], wts[g]))
            dvn = jnp.concatenate(dvn, axis=0)
            a_lng[...] += dvn * xh
            a_lnb[...] += dvn
            dxh3, xh3 = _by_group(dvn * lng_ref[...], dm.ng), _by_group(xh, dm.ng)
            dv = (rstd * (dxh3 - jnp.mean(dxh3, axis=1, keepdims=True) - xh3 * jnp.mean(dxh3 * xh3, axis=1, keepdims=True))).reshape(dm.gw, BLOCK)
            dgu = cu + zu * (jnp.exp(-0.5 * zu * zu) * _INV_SQRT_2PI)
            dgv = cv + zv * (jnp.exp(-0.5 * zv * zv) * _INV_SQRT_2PI)
            dproj_ref[dm.zoff:dm.zoff + dm.gw, :] = (du * dgu).astype(dproj_ref.dtype)
            dproj_ref[dm.zoff + dm.gw:, :] = (dv * dgv).astype(dproj_ref.dtype)

        @pl.when(i == nb)
        def _():
            prevpart_scr[...] = jnp.zeros_like(prevpart_scr)

        @pl.when(i >= 1)
        def _():
            tot = carry_scr[...] + prevpart_scr[...]
            for kh in range(dm.nkv):
                krows = slice(kh * dm.hd, (kh + 1) * dm.hd)
                kraw = kprev_scr[krows, :]
                rk = _col_rms(kraw)
                khat = kraw * rk
                dkn = tot[krows, :]
                a_gk[...] += dkn * khat
                dkh = dkn * gkv[:, :BLOCK]
                dk = rk * (dkh - khat * jnp.mean(dkh * khat, axis=0, keepdims=True))
                dkv_ref[krows, :] = dk.astype(dkv_ref.dtype)
            dkv_ref[dm.kvw:, :] = tot[dm.kvw:, :].astype(dkv_ref.dtype)

        @pl.when(i < nb)
        def _():
            carry_scr[...] = curpart_scr[...]
            kprev_scr[...] = p_ref[dm.aw:dm.aw + dm.kvw, :]

        @pl.when((s_id == nseq - 1) & (i == nb))
        def _():
            for acc, out in lane_accs:
                out[...] = jnp.sum(acc[...], axis=1, keepdims=True)

    col = lambda rows: jax.ShapeDtypeStruct((rows, 1), F32)
    lane = lambda rows: pltpu.VMEM((rows, LANES), F32)
    return pl.pallas_call(
        body, name="mixer_bwd", grid=(nseq, nb + 1),
        in_specs=[pl.BlockSpec((dm.inw, BLOCK), cur), pl.BlockSpec((kvw2, BLOCK), prev), pl.BlockSpec((dm.d, BLOCK), cur),
                  pl.BlockSpec((dm.nh * BLOCK, BLOCK), cur), pl.BlockSpec((dm.nh, BLOCK), cur), pl.BlockSpec((dm.aw, BLOCK), cur),
                  full(gq.shape), full(gk2.shape), pl.BlockSpec(memory_space=pltpu.SMEM),
                  full(lng.shape), full(lnb.shape), full(ws.shape), full(bs.shape), full(goa.shape), full(gog.shape)],
        out_specs=[pl.BlockSpec((dm.inw, BLOCK), cur), pl.BlockSpec((kvw2, BLOCK), prev_kv_out),
                   full((dm.hd, 1)), full((dm.hd, 1)), full((dm.nh, 1)), full((dm.gw, 1)), full((dm.gw, 1)), full(ws.shape),
                   full(bs.shape), full((dm.aw, 1)), full((dm.gw, 1))],
        out_shape=[jax.ShapeDtypeStruct((dm.inw, t), MXU), jax.ShapeDtypeStruct((kvw2, t), MXU),
                   col(dm.hd), col(dm.hd), col(dm.nh), col(dm.gw), col(dm.gw), jax.ShapeDtypeStruct(ws.shape, F32),
                   jax.ShapeDtypeStruct(bs.shape, F32), col(dm.aw), col(dm.gw)],
        scratch_shapes=[pltpu.VMEM((dm.aw, BLOCK), F32),
                        pltpu.VMEM((kvw2, BLOCK), F32), pltpu.VMEM((kvw2, BLOCK), F32), pltpu.VMEM((kvw2, BLOCK), F32),
                        pltpu.VMEM((dm.kvw, BLOCK), F32),
                        pltpu.VMEM((dm.hd, dm.grp * BLOCK), F32), lane(dm.hd), lane(dm.nh), lane(dm.gw), lane(dm.gw), lane(dm.aw), lane(dm.gw)],
        compiler_params=_cparams(("arbitrary", "arbitrary"), 48),
    )(proj_t, proj_t, dy_t, *saved, gq, gk2, sinks, lng, lnb, ws, bs, goa, gog)


def _patch_kv(dproj_t, dkv_t, dm):
    t = dproj_t.shape[1]
    tc = _pick(t, (1024, 512, 256, 128))
    kvw2 = 2 * dm.kvw
    kvblk = dm.aw // kvw2

    def body(dproj_hbm, dkv_ref, out_ref):
        del dproj_hbm
        out_ref[...] = dkv_ref[...]

    return pl.pallas_call(
        body, name="patch_kv", grid=(t // tc,),
        in_specs=[ANY, pl.BlockSpec((kvw2, tc), lambda i: (0, i))],
        out_specs=pl.BlockSpec((kvw2, tc), lambda i: (kvblk, i)),
        out_shape=jax.ShapeDtypeStruct(dproj_t.shape, dproj_t.dtype),
        input_output_aliases={0: 0},
        compiler_params=_cparams(("parallel",), 32),
    )(dproj_t, dkv_t)


def _place():
    x, y, c = lax.axis_index("x"), lax.axis_index("y"), lax.axis_index("c")
    return x, y, c


def _handshake(peers):
    barrier = pltpu.get_barrier_semaphore()
    for p in peers:
        pl.semaphore_signal(barrier, inc=1, device_id=p, device_id_type=MESH)
    pl.semaphore_wait(barrier, len(peers))


def _sequencer_mesh():
    return plsc.ScalarSubcoreMesh(axis_name="sequencer", num_cores=1)


GATHER_CHUNKS = 8
BF16_ROWS = 16


def _row_chunks(n, k):
    tiles = n // BF16_ROWS
    sizes = [(tiles // k + (1 if i < tiles % k else 0)) * BF16_ROWS for i in range(k)]
    return [(sum(sizes[:i]), sz) for i, sz in enumerate(sizes) if sz]


def _allgather_weight(name, collective_id, shard, part=None, head=None):
    src_off, n = part or (0, shard.shape[0])
    stride, dst_off = (shard.shape[0], src_off) if head is not None else (n, 0)
    assert n % BF16_ROWS == 0 and src_off % BF16_ROWS == 0 and stride % BF16_ROWS == 0
    chunks = _row_chunks(n, GATHER_CHUNKS if part is None else GATHER_CHUNKS // 2)
    nc = len(chunks)

    def body(src, *refs):
        if head is not None:
            head_ref, out, send_sems, recv_sems, local_sem, head_sems = refs
        else:
            out, send_sems, recv_sems, local_sem = refs
        x, y, c = _place()
        me, sib, xn, yn, diag = (x, y, c), (x, y, 1 - c), (1 - x, y, c), (x, 1 - y, c), (1 - x, 1 - y, c)
        relay_to = (x ^ c, y ^ (1 - c), c)
        relay_of = (x ^ (1 - c), y ^ c, c)
        _handshake([sib, xn, yn])

        def rows(place, ci):
            px, py, pc = place
            off, size = chunks[ci]
            return out.at[pl.ds(pl.multiple_of((4 * px + 2 * py + pc) * stride + dst_off + off, BF16_ROWS), size), :]

        def copy(k, ci, block, to, from_src=False):
            off, size = chunks[ci]
            return pltpu.make_async_remote_copy(
                src_ref=src.at[pl.ds(src_off + off, size), :] if from_src else rows(block, ci), dst_ref=rows(block, ci),
                send_sem=send_sems.at[ci, k], recv_sem=recv_sems.at[ci, k], device_id=to, device_id_type=MESH)

        mine = pltpu.make_async_copy(
            src.at[pl.ds(src_off, n), :], out.at[pl.ds(pl.multiple_of((4 * x + 2 * y + c) * stride + dst_off, BF16_ROWS), n), :], local_sem)
        mine.start()
        heads = []
        if head is not None:
            heads = [pltpu.make_async_copy(head_ref.at[pl.ds(p * src_off, src_off), :], out.at[pl.ds(p * stride, src_off), :], head_sems.at[p])
                     for p in range(N_DEV)]
            for cp in heads:
                cp.start()
        sent = []
        for ci in range(nc):
            sent += [copy(0, ci, me, sib, from_src=True), copy(1, ci, me, xn, from_src=True), copy(2, ci, me, yn, from_src=True)]
        for cp in sent:
            cp.start()
        for ci in range(nc):
            copy(1, ci, xn, me).wait_recv()
            copy(2, ci, yn, me).wait_recv()
            passed = [copy(3, ci, relay_of, relay_to), copy(4, ci, xn, sib), copy(5, ci, yn, sib)]
            for cp in passed:
                cp.start()
            sent += passed
        for ci in range(nc):
            copy(3, ci, diag, me).wait_recv()
            passed = copy(6, ci, diag, sib)
            passed.start()
            sent.append(passed)
        for ci in range(nc):
            copy(0, ci, sib, me).wait_recv()
            for k, block in ((4, (1 - x, y, 1 - c)), (5, (x, 1 - y, 1 - c)), (6, (1 - x, 1 - y, 1 - c))):
                copy(k, ci, block, me).wait_recv()
        for cp in sent:
            cp.wait_send()
        mine.wait()
        for cp in heads:
            cp.wait()

    return pl.kernel(
        body, name=name,
        out_type=jax.ShapeDtypeStruct((N_DEV * stride, shard.shape[1]), shard.dtype),
        mesh=_sequencer_mesh(),
        scratch_types=[pltpu.SemaphoreType.DMA((nc, 7)), pltpu.SemaphoreType.DMA((nc, 7)), pltpu.SemaphoreType.DMA]
        + ([pltpu.SemaphoreType.DMA((N_DEV,))] if head is not None else []),
        compiler_params=pltpu.CompilerParams(collective_id=collective_id),
    )(*((shard,) if head is None else (shard, head)))


def _matmul_in_half(name, w, h1, half, proj):
    t, kk = h1.shape
    n2 = w.shape[0] // N_DEV // (1 if proj is None else 2)
    tn = 512

    def body(*refs):
        w_ref, h_ref, o_ref = refs[0], refs[1], refs[-1]
        acc = lax.dot_general(w_ref[...].reshape(N_DEV * n2, kk), h_ref[...], (((1,), (1,)), ((), ())), preferred_element_type=F32)
        o_ref[...] = acc.reshape(N_DEV, n2, tn)

    if proj is None:
        w_in, w_spec = w.reshape(N_DEV, n2, kk), pl.BlockSpec((N_DEV, n2, kk), lambda j: (0, 0, 0), pipeline_mode=pl.Buffered(1))
    else:
        w_in, w_spec = w.reshape(N_DEV, 2, n2, kk), pl.BlockSpec((N_DEV, None, n2, kk), lambda j: (0, half, 0, 0), pipeline_mode=pl.Buffered(1))
    out = pl.pallas_call(
        body, name=name, grid=(t // tn,),
        in_specs=[w_spec, pl.BlockSpec((tn, kk), lambda j: (j, 0))] + ([] if proj is None else [ANY]),
        out_specs=pl.BlockSpec((N_DEV, None, n2, tn), lambda j: (0, half, 0, j)),
        out_shape=jax.ShapeDtypeStruct((N_DEV, 2, n2, t), F32),
        input_output_aliases={} if proj is None else {2: 0},
        compiler_params=_cparams(("arbitrary",), 48),
    )(*((w_in, h1) if proj is None else (w_in, h1, proj.reshape(N_DEV, 2, n2, t))))
    return out.reshape(N_DEV * 2 * n2, t)


_FLIPS = [(0, 0, 1), (1, 0, 0), (0, 1, 0), (1, 1, 0), (1, 0, 1), (0, 1, 1), (1, 1, 1)]


def _scatter_grad(name, collective_id, grad):
    n = grad.shape[0] // N_DEV

    def body(src, out, send_sems, recv_sems, local_sem):
        x, y, c = _place()
        me_idx = 4 * x + 2 * y + c
        peers = [(x ^ fx, y ^ fy, c ^ fc) for (fx, fy, fc) in _FLIPS]
        _handshake(peers)

        def block(idx):
            return src.at[pl.ds(pl.multiple_of(idx * n, 16), n), :]

        copies = [pltpu.make_async_remote_copy(
            src_ref=block(4 * px + 2 * py + pc), dst_ref=out.at[me_idx], send_sem=send_sems.at[k], recv_sem=recv_sems.at[k],
            device_id=(px, py, pc), device_id_type=MESH) for k, (px, py, pc) in enumerate(peers)]
        mine = pltpu.make_async_copy(block(me_idx), out.at[me_idx], local_sem)
        mine.start()
        for cp in copies:
            cp.start()
        for cp in copies:
            cp.wait_recv()
        for cp in copies:
            cp.wait_send()
        mine.wait()

    return pl.kernel(
        body, name=name,
        out_type=jax.ShapeDtypeStruct((N_DEV, n, grad.shape[1]), grad.dtype),
        mesh=_sequencer_mesh(),
        scratch_types=[pltpu.SemaphoreType.DMA((7,)), pltpu.SemaphoreType.DMA((7,)), pltpu.SemaphoreType.DMA],
        compiler_params=pltpu.CompilerParams(collective_id=collective_id),
    )(grad)


def _allgather_rows(name, collective_id, part):
    def body(src, out, send_sems, recv_sems, local_sem):
        x, y, c = _place()
        me_idx = 4 * x + 2 * y + c
        peers = [(x ^ fx, y ^ fy, c ^ fc) for (fx, fy, fc) in _FLIPS]
        _handshake(peers)
        copies = [pltpu.make_async_remote_copy(
            src_ref=src, dst_ref=out.at[me_idx], send_sem=send_sems.at[k], recv_sem=recv_sems.at[k],
            device_id=peer, device_id_type=MESH) for k, peer in enumerate(peers)]
        mine = pltpu.make_async_copy(src, out.at[me_idx], local_sem)
        mine.start()
        for cp in copies:
            cp.start()
        for cp in copies:
            cp.wait_recv()
        for cp in copies:
            cp.wait_send()
        mine.wait()

    return pl.kernel(
        body, name=name,
        out_type=jax.ShapeDtypeStruct((N_DEV,) + part.shape, part.dtype),
        mesh=_sequencer_mesh(),
        scratch_types=[pltpu.SemaphoreType.DMA((7,)), pltpu.SemaphoreType.DMA((7,)), pltpu.SemaphoreType.DMA],
        compiler_params=pltpu.CompilerParams(collective_id=collective_id),
    )(part)


def _adamw_math(w, g, m, v):
    m = ADAM_B1 * m + (1.0 - ADAM_B1) * g
    v = ADAM_B2 * v + (1.0 - ADAM_B2) * (g * g)
    m_hat = m / (1.0 - ADAM_B1 ** ADAM_STEP)
    v_hat = v / (1.0 - ADAM_B2 ** ADAM_STEP)
    delta = -ADAM_LR * (m_hat / (jnp.sqrt(v_hat) + ADAM_EPS) + ADAM_WD * w)
    return delta, m, v


def _sum_adamw(name, slots, w, m, v, after):
    _, n, kk = slots.shape
    tr = _pick(n, (208, 176, 128, 96, 64, 32, 16))

    def body(s_ref, w_ref, m_ref, v_ref, after_ref, g_ref, d_ref, nm_ref, nv_ref):
        del after_ref
        g = s_ref[0].astype(F32)
        for p in range(1, N_DEV):
            g = g + s_ref[p].astype(F32)
        g_ref[...] = g
        d_ref[...], nm_ref[...], nv_ref[...] = _adamw_math(w_ref[...], g, m_ref[...], v_ref[...])

    row = pl.BlockSpec((tr, kk), lambda i: (i, 0))
    return pl.pallas_call(
        body, name=name, grid=(n // tr,),
        in_specs=[pl.BlockSpec((N_DEV, tr, kk), lambda i: (0, i, 0)), row, row, row, ANY],
        out_specs=[row] * 4,
        out_shape=[jax.ShapeDtypeStruct((n, kk), F32)] * 4,
        compiler_params=_cparams(("parallel",), 48),
    )(slots, w, m, v, after)


def _allreduce_small_adamw(early_slots, late, w, m, v, after):
    ra, rb = early_slots.shape[1], late.shape[0]

    def body(early_ref, late_ref, w_ref, m_ref, v_ref, after_ref, g_ref, d_ref, nm_ref, nv_ref, slots, send_sems, recv_sems):
        del after_ref
        x, y, c = _place()
        me_idx = 4 * x + 2 * y + c
        copies = []
        for k, (fx, fy, fc) in enumerate(_FLIPS):
            px, py, pc = x ^ fx, y ^ fy, c ^ fc
            copies.append(pltpu.make_async_remote_copy(
                src_ref=late_ref, dst_ref=slots.at[me_idx], send_sem=send_sems.at[k], recv_sem=recv_sems.at[k],
                device_id=(px, py, pc), device_id_type=MESH))
        for cp in copies:
            cp.start()
        slots[me_idx] = late_ref[...]
        g = early_ref[0]
        for p in range(1, N_DEV):
            g = g + early_ref[p]
        early = pl.ds(0, ra)
        g_ref[early, :] = g
        d_ref[early, :], nm_ref[early, :], nv_ref[early, :] = _adamw_math(w_ref[early, :], g, m_ref[early, :], v_ref[early, :])
        for cp in copies:
            cp.wait_recv()
        for cp in copies:
            cp.wait_send()
        g = slots[0]
        for p in range(1, N_DEV):
            g = g + slots[p]
        tail = pl.ds(ra, rb)
        g_ref[tail, :] = g
        d_ref[tail, :], nm_ref[tail, :], nv_ref[tail, :] = _adamw_math(w_ref[tail, :], g, m_ref[tail, :], v_ref[tail, :])

    vm = pl.BlockSpec(memory_space=pltpu.VMEM)
    return pl.pallas_call(
        body, name="allreduce_small_adamw",
        in_specs=[vm] * 5 + [ANY], out_specs=[vm] * 4,
        out_shape=[jax.ShapeDtypeStruct((ra + rb, LANES), F32)] * 4,
        scratch_shapes=[pltpu.VMEM((N_DEV, rb, LANES), F32), pltpu.SemaphoreType.DMA((7,)), pltpu.SemaphoreType.DMA((7,))],
        compiler_params=pltpu.CompilerParams(vmem_limit_bytes=48 * MIB),
    )(early_slots, late, w, m, v, after)


def _pack(arrs):
    parts, meta, off = [], [], 0
    for a in arrs:
        flat = a.reshape(-1).astype(F32)
        rows = -(-flat.shape[0] // LANES)
        rows8 = -(-rows // 8) * 8
        flat = jnp.pad(flat, (0, rows8 * LANES - flat.shape[0]))
        parts.append(flat.reshape(rows8, LANES))
        meta.append((off, a.shape, a.size))
        off += rows8
    return jnp.concatenate(parts, axis=0), meta


def _unpack(packed, meta):
    outs = []
    for off, shape, size in meta:
        rows = -(-size // LANES)
        outs.append(packed[off:off + rows].reshape(-1)[:size].reshape(shape))
    return outs


def _silu_parts(a):
    sg = 0.5 + 0.5 * jnp.tanh(0.5 * a)
    return a * sg, sg * (1.0 + a * (1.0 - sg))


def kernel(x, norm1_g, w_in, q_norm_g, k_norm_g, attn_sinks, gate_ln_g, gate_ln_b, w_spatial, b_spatial, out_norm_attn_g, out_norm_gate_g, w_out, norm2_g, w_ffn_gate, w_ffn_up, w_ffn_down, loss_target, m_norm1_g, m_w_in, m_q_norm_g, m_k_norm_g, m_attn_sinks, m_gate_ln_g, m_gate_ln_b, m_w_spatial, m_b_spatial, m_out_norm_attn_g, m_out_norm_gate_g, m_w_out, m_norm2_g, m_w_ffn_gate, m_w_ffn_up, m_w_ffn_down, v_norm1_g, v_w_in, v_q_norm_g, v_k_norm_g, v_attn_sinks, v_gate_ln_g, v_gate_ln_b, v_w_spatial, v_b_spatial, v_out_norm_attn_g, v_out_norm_gate_g, v_w_out, v_norm2_g, v_w_ffn_gate, v_w_ffn_up, v_w_ffn_down):
    nseq, seq, d = x.shape
    t = nseq * seq
    nb = seq // BLOCK
    inw = w_in.shape[2] * N_DEV
    dm = _Dims(d, inw, q_norm_g.shape[-1])
    xf = x.reshape(t, d)
    tgt = loss_target.reshape(t, d)

    rows = lambda wv, transposed: jnp.swapaxes(wv, 1, 2)[0] if transposed else wv[0]
    big = {"w_in": (w_in, m_w_in, v_w_in, True), "w_out": (w_out, m_w_out, v_w_out, False),
           "w_ffn_gate": (w_ffn_gate, m_w_ffn_gate, v_w_ffn_gate, True), "w_ffn_up": (w_ffn_up, m_w_ffn_up, v_w_ffn_up, True),
           "w_ffn_down": (w_ffn_down, m_w_ffn_down, v_w_ffn_down, False)}
    big_rows = {nm: tuple(rows(arr, tr) for arr in (wv, mv, vv)) for nm, (wv, mv, vv, tr) in big.items()}
    shard = lambda nm: big_rows[nm][0].astype(WIRE)
    half_in = shard("w_in").shape[0] // 2
    win_head = _allgather_weight("gather_w_in_head", 1, shard("w_in"), part=(0, half_in))
    win_t = _allgather_weight("gather_w_in", 12, shard("w_in"), part=(half_in, half_in), head=win_head)
    wout = _allgather_weight("gather_w_out", 2, shard("w_out"))
    wg_t = _allgather_weight("gather_w_ffn_gate", 3, shard("w_ffn_gate"))
    wu_t = _allgather_weight("gather_w_ffn_up", 9, shard("w_ffn_up"))
    wd = _allgather_weight("gather_w_ffn_down", 10, shard("w_ffn_down"))

    lanes = lambda v, n=BLOCK: jnp.broadcast_to(v.reshape(-1, 1), (v.size, n))
    prm = (lanes(q_norm_g, dm.grp * BLOCK), lanes(k_norm_g, 2 * BLOCK), attn_sinks[0], lanes(gate_ln_g), lanes(gate_ln_b), w_spatial[0], b_spatial[0],
           lanes(out_norm_attn_g), lanes(out_norm_gate_g))

    h1 = _rms_fwd("rms1_fwd", xf, norm1_g)
    proj_t = _matmul_in_half("mm_in_head", win_head, h1, 0, None)
    proj_t = _matmul_in_half("mm_in", win_t, h1, 1, proj_t)
    y_t, *attn_saved = _mixer_fwd(proj_t, prm, dm, nseq, nb)

    def residual_norm(acc, xr, g2):
        x2v = xr + acc
        return x2v, x2v * lax.rsqrt(jnp.mean(x2v * x2v, axis=-1, keepdims=True) + EPS) * g2

    x2, h2 = _matmul("mm_out", y_t, wout, "tn", [F32, MXU], epilogue=residual_norm, extras=[xf], rowvecs=[norm2_g], full_rows=True)
    silu_a, dsilu_a = _matmul("mm_gate", h2, wg_t, "nt", [MXU, MXU], epilogue=_silu_parts)
    s, dsilu_a_b = _matmul("mm_up", h2, wu_t, "nt", [MXU, MXU], epilogue=lambda ub, sv, dv: (sv.astype(F32) * ub, dv.astype(F32) * ub),
                           extras=[silu_a, dsilu_a])

    def loss_epilogue(acc, x2v, tv):
        diff = (x2v + acc) - tv
        dx3 = diff * (1.0 / d)
        return dx3, dx3, jnp.sum(diff * diff)

    dx3, dx3b, lossp = _matmul("mm_down", s, wd, "nn", [F32, MXU], epilogue=loss_epilogue, extras=[x2, tgt], partial=True)
    loss_part = (0.5 / d) * jnp.sum(lossp[::8, ::LANES])

    def dswiglu(ds, dsilu_b, silu):
        return ds * dsilu_b.astype(F32), ds * silu.astype(F32)

    (g_wd,) = _matmul("mm_gw_down", s, dx3b, "tn", [WIRE])
    sl_wd = _scatter_grad("scatter_w_ffn_down", 4, g_wd)
    da, db = _matmul("mm_d_down", dx3b, wd, "nt", [MXU, MXU], epilogue=dswiglu, extras=[dsilu_a_b, silu_a], after=[g_wd])
    (g_wg,) = _matmul("mm_gw_gate", da, h2, "tn", [WIRE])
    sl_wg = _scatter_grad("scatter_w_ffn_gate", 5, g_wg)
    (g_wu,) = _matmul("mm_gw_up", db, h2, "tn", [WIRE], after=[g_wg])
    sl_wu = _scatter_grad("scatter_w_ffn_up", 6, g_wu)
    (dh2a,) = _matmul("mm_dh2_gate", da, wg_t, "nn", [F32], after=[g_wu])
    (dh2,) = _matmul("mm_dh2_up", db, wu_t, "nn", [F32], epilogue=lambda acc, pv: (pv + acc,), extras=[dh2a])

    dy_t, dx2, dx2b, dg2 = _norm_bwd_matmul("mm_d_out", wout, dh2, x2, norm2_g, dx3, after=dh2a)
    (g_wout,) = _matmul("mm_gw_out", y_t, dx2b, "nn", [WIRE], after=[dy_t])
    sl_wout = _scatter_grad("scatter_w_out", 7, g_wout)
    (dproj0, dkv, dgq, dgk, dsink, dlng, dlnb, dws, dbs, dgoa, dgog) = _mixer_bwd(proj_t, dy_t, attn_saved, prm, dm, nseq, nb)
    early_g = [dgq, dgk, dsink, dlng, dlnb, dws, dbs, dgoa, dgog, dg2, loss_part.reshape(1)]
    early_slots = _allgather_rows("gather_small_grads", 11, _pack(early_g)[0])
    dproj_t = _patch_kv(dproj0, dkv, dm)
    (g_win,) = _matmul("mm_gw_in", dproj_t, h1, "nn", [WIRE])
    sl_win = _scatter_grad("scatter_w_in", 8, g_win)

    def norm1_backward(dh1, xv, dx2v, g1):
        r = lax.rsqrt(jnp.mean(xv * xv, axis=-1, keepdims=True) + EPS)
        xh = xv * r
        dxh = dh1 * g1
        return dx2v + r * (dxh - xh * jnp.mean(dxh * xh, axis=-1, keepdims=True)), jnp.sum(dh1 * xh, axis=0, keepdims=True)

    dx, dg1 = _matmul("mm_d_in", dproj_t, win_t, "tn", [F32], epilogue=norm1_backward, extras=[xf, dx2], rowvecs=[norm1_g],
                      after=[g_win], col_sum=True, full_rows=True)

    big_out = {}
    last = dx

    def big_update(nm, sl, after):
        res = _sum_adamw("adamw_" + nm, sl, *big_rows[nm], after=after)
        big_out[nm] = tuple(jnp.swapaxes(r[None], 1, 2) if big[nm][3] else r[None] for r in res)
        return res[1]

    for nm, sl in (("w_ffn_down", sl_wd), ("w_ffn_gate", sl_wg), ("w_ffn_up", sl_wu), ("w_out", sl_wout)):
        last = big_update(nm, sl, last)

    zero = jnp.zeros((1,), F32)
    small_names = ["q_norm_g", "k_norm_g", "attn_sinks", "gate_ln_g", "gate_ln_b", "w_spatial", "b_spatial",
                   "out_norm_attn_g", "out_norm_gate_g", "norm2_g", "loss", "norm1_g"]
    small_w = [q_norm_g, k_norm_g, attn_sinks, gate_ln_g, gate_ln_b, w_spatial, b_spatial, out_norm_attn_g, out_norm_gate_g, norm2_g, zero, norm1_g]
    small_m = [m_q_norm_g, m_k_norm_g, m_attn_sinks, m_gate_ln_g, m_gate_ln_b, m_w_spatial, m_b_spatial, m_out_norm_attn_g, m_out_norm_gate_g, m_norm2_g, zero, m_norm1_g]
    small_v = [v_q_norm_g, v_k_norm_g, v_attn_sinks, v_gate_ln_g, v_gate_ln_b, v_w_spatial, v_b_spatial, v_out_norm_attn_g, v_out_norm_gate_g, v_norm2_g, zero, v_norm1_g]
    pw, meta = _pack(small_w)
    sg, sd, sm, sv = _allreduce_small_adamw(early_slots, _pack([dg1])[0], pw, _pack(small_m)[0], _pack(small_v)[0], after=last)
    big_update("w_in", sl_win, sd)
    ug, ud, um, uv = _unpack(sg, meta), _unpack(sd, meta), _unpack(sm, meta), _unpack(sv, meta)
    small_out = {nm: (ug[k], ud[k], um[k], uv[k]) for k, nm in enumerate(small_names)}
    loss = small_out["loss"][0].reshape(())

    order = ["norm1_g", "w_in", "q_norm_g", "k_norm_g", "attn_sinks", "gate_ln_g", "gate_ln_b", "w_spatial", "b_spatial",
             "out_norm_attn_g", "out_norm_gate_g", "w_out", "norm2_g", "w_ffn_gate", "w_ffn_up", "w_ffn_down"]
    allo = {**big_out, **small_out}
    outs = [loss, dx.reshape(nseq, seq, d)]
    for k in range(4):
        outs += [allo[nm][k] for nm in order]
    return tuple(outs)
```
